```python
import math
import jax, jax.numpy as jnp
from jax import lax
import numpy as np


D_MODEL = 2048
BATCH = 8
SEQ = 8192
DEPTH = 1

MLA_HEADS = 8
MLA_NOPE_DIM = 128
MLA_ROPE_DIM = 64
MLA_QK_DIM = MLA_NOPE_DIM + MLA_ROPE_DIM
MLA_V_DIM = 128
MLA_Q_LORA = 512
MLA_KV_LORA = 256
MLA_WIDTH = MLA_HEADS * MLA_V_DIM
GDN_HEADS = 8
GDN_K_DIM = 128
GDN_V_DIM = 128
GDN_WIDTH = GDN_HEADS * GDN_V_DIM
GDN_QKV = 2 * GDN_HEADS * GDN_K_DIM + GDN_WIDTH
CONV_WIDTH = 4
CHUNK = 64
D_MIX = MLA_WIDTH + GDN_WIDTH
ROPE_THETA = 10000.0
NORM_EPS = 1e-6
Q_BLOCK = 128
SPLIT_SIZES = (MLA_Q_LORA, MLA_KV_LORA, MLA_ROPE_DIM, MLA_WIDTH,
               GDN_HEADS * GDN_K_DIM, GDN_HEADS * GDN_K_DIM, GDN_WIDTH,
               GDN_HEADS, GDN_HEADS, GDN_WIDTH)
IN_COLS = sum(SPLIT_SIZES)

kernel_name = 'hymba_mla_gdn_gated_hybrid'


def rms_norm(x, gain):
    xf = x.astype(jnp.float32)
    y = xf * lax.rsqrt(jnp.mean(xf * xf, axis=-1, keepdims=True) + NORM_EPS)
    return (y * gain.astype(jnp.float32)).astype(x.dtype)


def l2_norm(x):
    xf = x.astype(jnp.float32)
    return xf * lax.rsqrt(jnp.sum(xf * xf, axis=-1, keepdims=True) + NORM_EPS)


def rope(x, positions):
    half = x.shape[-1] // 2
    inv_freq = jnp.power(ROPE_THETA, -jnp.arange(half, dtype=jnp.float32) / half)
    ang = positions.astype(jnp.float32)[..., None] * inv_freq
    cos = jnp.cos(ang)[:, :, None, :]
    sin = jnp.sin(ang)[:, :, None, :]
    xf = x.astype(jnp.float32)
    x1, x2 = xf[..., :half], xf[..., half:]
    return jnp.concatenate([x1 * cos - x2 * sin, x2 * cos + x1 * sin], axis=-1).astype(x.dtype)


def mla_branch(cq, ckv, k_rope, positions, q_a_gain, kv_a_gain, w_uq, w_ukv, q_gain, k_gain):
    B, S, _ = cq.shape
    q = (rms_norm(cq, q_a_gain) @ w_uq).reshape(B, S, MLA_HEADS, MLA_QK_DIM)
    kv = (rms_norm(ckv, kv_a_gain) @ w_ukv).reshape(B, S, MLA_HEADS, MLA_NOPE_DIM + MLA_V_DIM)
    k_nope, v = kv[..., :MLA_NOPE_DIM], kv[..., MLA_NOPE_DIM:]
    k_shared = jnp.broadcast_to(k_rope[:, :, None, :], (B, S, MLA_HEADS, MLA_ROPE_DIM))
    k = jnp.concatenate([k_nope, k_shared], axis=-1)
    q = rms_norm(q, q_gain)
    k = rms_norm(k, k_gain)
    q = jnp.concatenate([q[..., :MLA_NOPE_DIM], rope(q[..., MLA_NOPE_DIM:], positions)], axis=-1)
    k = jnp.concatenate([k[..., :MLA_NOPE_DIM], rope(k[..., MLA_NOPE_DIM:], positions)], axis=-1)
    q = q.transpose(0, 2, 1, 3)
    k = k.transpose(0, 2, 1, 3)
    v = v.transpose(0, 2, 1, 3)
    scale = MLA_QK_DIM ** -0.5
    key_idx = jnp.arange(S)

    def block(i):
        start = i * Q_BLOCK
        qb = lax.dynamic_slice_in_dim(q, start, Q_BLOCK, axis=2)
        s = jnp.einsum('bhqd,bhkd->bhqk', qb, k, preferred_element_type=jnp.float32) * scale
        q_idx = start + jnp.arange(Q_BLOCK)
        s = jnp.where(key_idx[None, :] <= q_idx[:, None], s, -jnp.inf)
        p = jax.nn.softmax(s, axis=-1).astype(v.dtype)
        return jnp.einsum('bhqk,bhkd->bqhd', p, v)

    o = lax.map(block, jnp.arange(S // Q_BLOCK))
    return o.transpose(1, 0, 2, 3, 4).reshape(B, S, MLA_WIDTH)


def causal_conv_silu(x, w):
    S = x.shape[1]
    xp = jnp.pad(x, ((0, 0), (CONV_WIDTH - 1, 0), (0, 0)))
    y = xp[:, 0:S, :] * w[0]
    for j in range(1, CONV_WIDTH):
        y = y + xp[:, j:j + S, :] * w[j]
    return jax.nn.silu(y)


def chunked_gated_delta(q, k, v, g, beta):
    B, S, H, Dk = q.shape
    Dv = v.shape[-1]
    N = S // CHUNK

    def to_chunks(t):
        return t.reshape(B, N, CHUNK, H, t.shape[-1]).transpose(0, 3, 1, 2, 4)

    q, k, v = to_chunks(q), to_chunks(k), to_chunks(v)
    g = g.reshape(B, N, CHUNK, H).transpose(0, 3, 1, 2)
    beta = beta.reshape(B, N, CHUNK, H).transpose(0, 3, 1, 2)
    gc = jnp.cumsum(g, axis=-1)
    idx = jnp.arange(CHUNK)
    lower_incl = idx[:, None] >= idx[None, :]
    strict = idx[:, None] > idx[None, :]
    diff = gc[..., :, None] - gc[..., None, :]
    decay = jnp.where(lower_incl, jnp.exp(jnp.where(lower_incl, diff, 0.0)), 0.0)
    k_beta = k * beta[..., None]
    v_beta = v * beta[..., None]
    L = jnp.where(strict, jnp.einsum('bhncd,bhnjd->bhncj', k_beta, k) * decay, 0.0)
    eye = jnp.eye(CHUNK, dtype=jnp.float32)
    rhs = jnp.concatenate([v_beta, k_beta * jnp.exp(gc)[..., None]], axis=-1)
    sol = lax.linalg.triangular_solve(eye + L, rhs, left_side=True, lower=True, unit_diagonal=True)
    u, w = sol[..., :Dv], sol[..., Dv:]
    attn_intra = jnp.einsum('bhncd,bhnjd->bhncj', q, k) * decay
    q_dec = q * jnp.exp(gc)[..., None]
    k_dec = k * jnp.exp(gc[..., -1:] - gc)[..., None]
    g_last = jnp.exp(gc[..., -1])

    def mv(t):
        return jnp.moveaxis(t, 2, 0)

    xs = (mv(u), mv(w), mv(q_dec), mv(k_dec), mv(attn_intra), jnp.moveaxis(g_last, 2, 0))

    def step(state, inp):
        u_c, w_c, qd, kd, a_c, gl = inp
        v_new = u_c - jnp.einsum('bhck,bhkv->bhcv', w_c, state)
        o = jnp.einsum('bhck,bhkv->bhcv', qd, state) + jnp.einsum('bhcj,bhjv->bhcv', a_c, v_new)
        state = state * gl[..., None, None] + jnp.einsum('bhck,bhcv->bhkv', kd, v_new)
        return state, o

    state0 = jnp.zeros((B, H, Dk, Dv), jnp.float32)
    _, o = lax.scan(step, state0, xs)
    return o.transpose(1, 0, 3, 2, 4).reshape(B, S, H, Dv)


def gdn_branch(gq, gk, gv, ga, gb, conv_w, a_log, dt_bias, out_gain):
    B, S, _ = gq.shape
    qkv = causal_conv_silu(jnp.concatenate([gq, gk, gv], axis=-1), conv_w)
    hk = GDN_HEADS * GDN_K_DIM
    q = qkv[..., :hk].reshape(B, S, GDN_HEADS, GDN_K_DIM)
    k = qkv[..., hk:2 * hk].reshape(B, S, GDN_HEADS, GDN_K_DIM)
    v = qkv[..., 2 * hk:].reshape(B, S, GDN_HEADS, GDN_V_DIM).astype(jnp.float32)
    q = l2_norm(q) * (GDN_K_DIM ** -0.5)
    k = l2_norm(k)
    g = -jnp.exp(a_log.astype(jnp.float32)) * jax.nn.softplus(ga.astype(jnp.float32) + dt_bias.astype(jnp.float32))
    beta = jax.nn.sigmoid(gb.astype(jnp.float32))
    o = chunked_gated_delta(q, k, v, g, beta)
    o = rms_norm(o, out_gain).astype(gq.dtype)
    return o.reshape(B, S, GDN_WIDTH)


def _fwd_setup_inputs(seed: int = 0) -> dict:
    key = jax.random.key(seed)
    ks = jax.random.split(key, 16)
    f32 = jnp.float32

    def dense(k, shape, fan_in):
        return jax.random.normal(k, shape, f32) * fan_in ** -0.5

    def gain(k, shape):
        return 1.0 + 0.02 * jax.random.normal(k, shape, f32)

    x = jax.random.normal(ks[0], (BATCH, SEQ, D_MODEL), f32)
    positions = (jax.random.randint(ks[1], (BATCH, 1), 0, 1024, jnp.int32)
                 + jnp.arange(SEQ, dtype=jnp.int32)[None, :])
    dt = jnp.exp(jax.random.uniform(ks[11], (DEPTH, GDN_HEADS), f32,
                                    minval=math.log(1e-3), maxval=math.log(1e-1)))
    dt_bias = dt + jnp.log(-jnp.expm1(-dt))
    a_log = jnp.log(jax.random.uniform(ks[12], (DEPTH, GDN_HEADS), f32, minval=1.0, maxval=16.0))
    return {
        'x': x,
        'positions': positions,
        'norm_gain': gain(ks[2], (DEPTH, D_MODEL)),
        'w_in': dense(ks[3], (DEPTH, D_MODEL, IN_COLS), D_MODEL),
        'mla_q_a_gain': gain(ks[4], (DEPTH, MLA_Q_LORA)),
        'mla_kv_a_gain': gain(ks[5], (DEPTH, MLA_KV_LORA)),
        'w_uq': dense(ks[6], (DEPTH, MLA_Q_LORA, MLA_HEADS * MLA_QK_DIM), MLA_Q_LORA),
        'w_ukv': dense(ks[7], (DEPTH, MLA_KV_LORA, MLA_HEADS * (MLA_NOPE_DIM + MLA_V_DIM)), MLA_KV_LORA),
        'mla_q_norm_gain': gain(ks[8], (DEPTH, MLA_QK_DIM)),
        'mla_k_norm_gain': gain(ks[9], (DEPTH, MLA_QK_DIM)),
        'gdn_conv_w': dense(ks[10], (DEPTH, CONV_WIDTH, GDN_QKV), CONV_WIDTH),
        'gdn_a_log': a_log,
        'gdn_dt_bias': dt_bias,
        'gdn_out_norm_gain': gain(ks[13], (DEPTH, GDN_V_DIM)),
        'w_out': dense(ks[14], (DEPTH, D_MIX, D_MODEL), D_MIX),
    }


def _fwd_reference(x, positions, norm_gain, w_in, mla_q_a_gain, mla_kv_a_gain, w_uq, w_ukv,
              mla_q_norm_gain, mla_k_norm_gain, gdn_conv_w, gdn_a_log, gdn_dt_bias,
              gdn_out_norm_gain, w_out):
    B, S, _ = x.shape
    split_idx = np.cumsum(SPLIT_SIZES)[:-1].tolist()
    h = x
    for layer in range(DEPTH):
        xn = rms_norm(h, norm_gain[layer])
        proj = xn @ w_in[layer]
        cq, ckv, k_rope, mla_gate, gq, gk, gv, ga, gb, gdn_gate = jnp.split(proj, split_idx, axis=-1)
        o_mla = mla_branch(cq, ckv, k_rope, positions, mla_q_a_gain[layer], mla_kv_a_gain[layer],
                           w_uq[layer], w_ukv[layer], mla_q_norm_gain[layer], mla_k_norm_gain[layer])
        o_mla = o_mla * jax.nn.silu(mla_gate)
        o_gdn = gdn_branch(gq, gk, gv, ga, gb, gdn_conv_w[layer], gdn_a_log[layer],
                           gdn_dt_bias[layer], gdn_out_norm_gain[layer])
        o_gdn = o_gdn * jax.nn.silu(gdn_gate)
        mixed = jnp.concatenate([o_mla, o_gdn], axis=-1)
        h = h + mixed @ w_out[layer]
    return h


import jax as _jax
import jax.numpy as _jnp

TWIN_FORMAT = 'train_step'
FWD_PARAMS = ['x', 'positions', 'norm_gain', 'w_in', 'mla_q_a_gain', 'mla_kv_a_gain', 'w_uq', 'w_ukv', 'mla_q_norm_gain', 'mla_k_norm_gain', 'gdn_conv_w', 'gdn_a_log', 'gdn_dt_bias', 'gdn_out_norm_gain', 'w_out']
TWIN_WEIGHTS = ['norm_gain', 'w_in', 'mla_q_a_gain', 'mla_kv_a_gain', 'w_uq', 'w_ukv', 'mla_q_norm_gain', 'mla_k_norm_gain', 'gdn_conv_w', 'gdn_a_log', 'gdn_dt_bias', 'gdn_out_norm_gain', 'w_out']
TWIN_DIFF_INPUT = 'x'
TWIN_INPUTS = ['x', 'positions', 'norm_gain', 'w_in', 'mla_q_a_gain', 'mla_kv_a_gain', 'w_uq', 'w_ukv', 'mla_q_norm_gain', 'mla_k_norm_gain', 'gdn_conv_w', 'gdn_a_log', 'gdn_dt_bias', 'gdn_out_norm_gain', 'w_out', 'loss_target', 'm_norm_gain', 'm_w_in', 'm_mla_q_a_gain', 'm_mla_kv_a_gain', 'm_w_uq', 'm_w_ukv', 'm_mla_q_norm_gain', 'm_mla_k_norm_gain', 'm_gdn_conv_w', 'm_gdn_a_log', 'm_gdn_dt_bias', 'm_gdn_out_norm_gain', 'm_w_out', 'v_norm_gain', 'v_w_in', 'v_mla_q_a_gain', 'v_mla_kv_a_gain', 'v_w_uq', 'v_w_ukv', 'v_mla_q_norm_gain', 'v_mla_k_norm_gain', 'v_gdn_conv_w', 'v_gdn_a_log', 'v_gdn_dt_bias', 'v_gdn_out_norm_gain', 'v_w_out']
TWIN_OUTPUTS = ['loss', 'grad_x', 'grad_norm_gain', 'grad_w_in', 'grad_mla_q_a_gain', 'grad_mla_kv_a_gain', 'grad_w_uq', 'grad_w_ukv', 'grad_mla_q_norm_gain', 'grad_mla_k_norm_gain', 'grad_gdn_conv_w', 'grad_gdn_a_log', 'grad_gdn_dt_bias', 'grad_gdn_out_norm_gain', 'grad_w_out', 'delta_norm_gain', 'delta_w_in', 'delta_mla_q_a_gain', 'delta_mla_kv_a_gain', 'delta_w_uq', 'delta_w_ukv', 'delta_mla_q_norm_gain', 'delta_mla_k_norm_gain', 'delta_gdn_conv_w', 'delta_gdn_a_log', 'delta_gdn_dt_bias', 'delta_gdn_out_norm_gain', 'delta_w_out', 'new_m_norm_gain', 'new_m_w_in', 'new_m_mla_q_a_gain', 'new_m_mla_kv_a_gain', 'new_m_w_uq', 'new_m_w_ukv', 'new_m_mla_q_norm_gain', 'new_m_mla_k_norm_gain', 'new_m_gdn_conv_w', 'new_m_gdn_a_log', 'new_m_gdn_dt_bias', 'new_m_gdn_out_norm_gain', 'new_m_w_out', 'new_v_norm_gain', 'new_v_w_in', 'new_v_mla_q_a_gain', 'new_v_mla_kv_a_gain', 'new_v_w_uq', 'new_v_w_ukv', 'new_v_mla_q_norm_gain', 'new_v_mla_k_norm_gain', 'new_v_gdn_conv_w', 'new_v_gdn_a_log', 'new_v_gdn_dt_bias', 'new_v_gdn_out_norm_gain', 'new_v_w_out']
TWIN_LEAF_KINDS = {'loss': 'loss', 'grad_x': 'grad_x', 'grad_norm_gain': 'grad_w', 'grad_w_in': 'grad_w', 'grad_mla_q_a_gain': 'grad_w', 'grad_mla_kv_a_gain': 'grad_w', 'grad_w_uq': 'grad_w', 'grad_w_ukv': 'grad_w', 'grad_mla_q_norm_gain': 'grad_w', 'grad_mla_k_norm_gain': 'grad_w', 'grad_gdn_conv_w': 'grad_w', 'grad_gdn_a_log': 'grad_w', 'grad_gdn_dt_bias': 'grad_w', 'grad_gdn_out_norm_gain': 'grad_w', 'grad_w_out': 'grad_w', 'delta_norm_gain': 'delta_w', 'delta_w_in': 'delta_w', 'delta_mla_q_a_gain': 'delta_w', 'delta_mla_kv_a_gain': 'delta_w', 'delta_w_uq': 'delta_w', 'delta_w_ukv': 'delta_w', 'delta_mla_q_norm_gain': 'delta_w', 'delta_mla_k_norm_gain': 'delta_w', 'delta_gdn_conv_w': 'delta_w', 'delta_gdn_a_log': 'delta_w', 'delta_gdn_dt_bias': 'delta_w', 'delta_gdn_out_norm_gain': 'delta_w', 'delta_w_out': 'delta_w', 'new_m_norm_gain': 'new_m', 'new_m_w_in': 'new_m', 'new_m_mla_q_a_gain': 'new_m', 'new_m_mla_kv_a_gain': 'new_m', 'new_m_w_uq': 'new_m', 'new_m_w_ukv': 'new_m', 'new_m_mla_q_norm_gain': 'new_m', 'new_m_mla_k_norm_gain': 'new_m', 'new_m_gdn_conv_w': 'new_m', 'new_m_gdn_a_log': 'new_m', 'new_m_gdn_dt_bias': 'new_m', 'new_m_gdn_out_norm_gain': 'new_m', 'new_m_w_out': 'new_m', 'new_v_norm_gain': 'new_v', 'new_v_w_in': 'new_v', 'new_v_mla_q_a_gain': 'new_v', 'new_v_mla_kv_a_gain': 'new_v', 'new_v_w_uq': 'new_v', 'new_v_w_ukv': 'new_v', 'new_v_mla_q_norm_gain': 'new_v', 'new_v_mla_k_norm_gain': 'new_v', 'new_v_gdn_conv_w': 'new_v', 'new_v_gdn_a_log': 'new_v', 'new_v_gdn_dt_bias': 'new_v', 'new_v_gdn_out_norm_gain': 'new_v', 'new_v_w_out': 'new_v'}


def _forward(args):
    return _fwd_reference(*[args[k] for k in FWD_PARAMS])


def _output_shape():
    def fwd():
        inp = _fwd_setup_inputs(0)
        return _fwd_reference(*[inp[k] for k in FWD_PARAMS])
    out = _jax.eval_shape(fwd)
    return out.shape, out.dtype

N_MICROBATCH = 1
ADAM_LR = 0.001
ADAM_B1 = 0.9
ADAM_B2 = 0.999
ADAM_EPS = 1e-08
ADAM_WD = 0.01
ADAM_STEP = 10
PER_EXAMPLE_BATCH_AXIS = {'x': 0, 'positions': 0, 'loss_target': 0}
SHARED_INPUTS = []
_WEIGHT_DTYPES = {'norm_gain': _jnp.float32, 'w_in': _jnp.float32, 'mla_q_a_gain': _jnp.float32, 'mla_kv_a_gain': _jnp.float32, 'w_uq': _jnp.float32, 'w_ukv': _jnp.float32, 'mla_q_norm_gain': _jnp.float32, 'mla_k_norm_gain': _jnp.float32, 'gdn_conv_w': _jnp.float32, 'gdn_a_log': _jnp.float32, 'gdn_dt_bias': _jnp.float32, 'gdn_out_norm_gain': _jnp.float32, 'w_out': _jnp.float32}
MOMENT_SCALE = {'norm_gain': 6.416373e+00, 'w_in': 1.764540e-01, 'mla_q_a_gain': 3.445135e-02, 'mla_kv_a_gain': 4.006617e-01, 'w_uq': 1.957648e-02, 'w_ukv': 2.543882e-02, 'mla_q_norm_gain': 3.079774e-01, 'mla_k_norm_gain': 3.083103e-01, 'gdn_conv_w': 3.177394e-01, 'gdn_a_log': 3.609697e+01, 'gdn_dt_bias': 3.414497e+01, 'gdn_out_norm_gain': 9.226849e+01, 'w_out': 2.869339e-01}


def _to_microbatches(a, axis):
    t = _jnp.moveaxis(a, axis, 0)
    t = t.reshape((N_MICROBATCH, t.shape[0] // N_MICROBATCH) + t.shape[1:])
    return _jnp.moveaxis(t, 1, axis + 1)


def setup_inputs(seed: int = 0) -> dict:
    inp = _fwd_setup_inputs(seed)
    key = _jax.random.fold_in(_jax.random.key(seed), 7919)
    shape, _ = _output_shape()
    out = dict(inp)
    out["loss_target"] = _jax.random.normal(_jax.random.fold_in(key, 0), shape, _jnp.float32)
    for i, name in enumerate(TWIN_WEIGHTS):
        w = inp[name].astype(_jnp.float32)
        if MOMENT_SCALE is None:
            s = _jnp.sqrt(_jnp.mean(_jnp.square(w)) + 1e-30)
        else:
            s = MOMENT_SCALE[name]
        km, kv = _jax.random.split(_jax.random.fold_in(key, i + 1))
        out[name] = w
        out["m_" + name] = s * _jax.random.normal(km, w.shape, _jnp.float32)
        out["v_" + name] = (s * s) * _jax.random.uniform(kv, w.shape, _jnp.float32, 0.5, 1.5)
    if N_MICROBATCH > 1:
        for name, axis in PER_EXAMPLE_BATCH_AXIS.items():
            out[name] = _to_microbatches(out[name], axis)
    return {'x': out['x'], 'positions': out['positions'], 'norm_gain': out['norm_gain'], 'w_in': out['w_in'], 'mla_q_a_gain': out['mla_q_a_gain'], 'mla_kv_a_gain': out['mla_kv_a_gain'], 'w_uq': out['w_uq'], 'w_ukv': out['w_ukv'], 'mla_q_norm_gain': out['mla_q_norm_gain'], 'mla_k_norm_gain': out['mla_k_norm_gain'], 'gdn_conv_w': out['gdn_conv_w'], 'gdn_a_log': out['gdn_a_log'], 'gdn_dt_bias': out['gdn_dt_bias'], 'gdn_out_norm_gain': out['gdn_out_norm_gain'], 'w_out': out['w_out'], 'loss_target': out['loss_target'], 'm_norm_gain': out['m_norm_gain'], 'm_w_in': out['m_w_in'], 'm_mla_q_a_gain': out['m_mla_q_a_gain'], 'm_mla_kv_a_gain': out['m_mla_kv_a_gain'], 'm_w_uq': out['m_w_uq'], 'm_w_ukv': out['m_w_ukv'], 'm_mla_q_norm_gain': out['m_mla_q_norm_gain'], 'm_mla_k_norm_gain': out['m_mla_k_norm_gain'], 'm_gdn_conv_w': out['m_gdn_conv_w'], 'm_gdn_a_log': out['m_gdn_a_log'], 'm_gdn_dt_bias': out['m_gdn_dt_bias'], 'm_gdn_out_norm_gain': out['m_gdn_out_norm_gain'], 'm_w_out': out['m_w_out'], 'v_norm_gain': out['v_norm_gain'], 'v_w_in': out['v_w_in'], 'v_mla_q_a_gain': out['v_mla_q_a_gain'], 'v_mla_kv_a_gain': out['v_mla_kv_a_gain'], 'v_w_uq': out['v_w_uq'], 'v_w_ukv': out['v_w_ukv'], 'v_mla_q_norm_gain': out['v_mla_q_norm_gain'], 'v_mla_k_norm_gain': out['v_mla_k_norm_gain'], 'v_gdn_conv_w': out['v_gdn_conv_w'], 'v_gdn_a_log': out['v_gdn_a_log'], 'v_gdn_dt_bias': out['v_gdn_dt_bias'], 'v_gdn_out_norm_gain': out['v_gdn_out_norm_gain'], 'v_w_out': out['v_w_out']}


def _loss(weights, diff, rest, loss_target):
    with _jax.named_scope("forward"):
        args = {**rest, TWIN_DIFF_INPUT: diff, **{k: w.astype(_WEIGHT_DTYPES[k]) for k, w in weights.items()}}
        y = _forward(args)
    with _jax.named_scope("loss_head"):
        err = _jnp.square(y.astype(_jnp.float32) - loss_target)
        return 0.5 * _jnp.sum(_jnp.mean(err, axis=-1)) if err.ndim else 0.5 * err


def _adamw(w, g, m, v):
    m = ADAM_B1 * m + (1.0 - ADAM_B1) * g
    v = ADAM_B2 * v + (1.0 - ADAM_B2) * _jnp.square(g)
    m_hat = m / (1.0 - ADAM_B1 ** ADAM_STEP)
    v_hat = v / (1.0 - ADAM_B2 ** ADAM_STEP)
    delta = -ADAM_LR * (m_hat / (_jnp.sqrt(v_hat) + ADAM_EPS) + ADAM_WD * w)
    return delta, m, v


def reference(x, positions, norm_gain, w_in, mla_q_a_gain, mla_kv_a_gain, w_uq, w_ukv, mla_q_norm_gain, mla_k_norm_gain, gdn_conv_w, gdn_a_log, gdn_dt_bias, gdn_out_norm_gain, w_out, loss_target, m_norm_gain, m_w_in, m_mla_q_a_gain, m_mla_kv_a_gain, m_w_uq, m_w_ukv, m_mla_q_norm_gain, m_mla_k_norm_gain, m_gdn_conv_w, m_gdn_a_log, m_gdn_dt_bias, m_gdn_out_norm_gain, m_w_out, v_norm_gain, v_w_in, v_mla_q_a_gain, v_mla_kv_a_gain, v_w_uq, v_w_ukv, v_mla_q_norm_gain, v_mla_k_norm_gain, v_gdn_conv_w, v_gdn_a_log, v_gdn_dt_bias, v_gdn_out_norm_gain, v_w_out):
    given = dict(x=x, positions=positions, norm_gain=norm_gain, w_in=w_in, mla_q_a_gain=mla_q_a_gain, mla_kv_a_gain=mla_kv_a_gain, w_uq=w_uq, w_ukv=w_ukv, mla_q_norm_gain=mla_q_norm_gain, mla_k_norm_gain=mla_k_norm_gain, gdn_conv_w=gdn_conv_w, gdn_a_log=gdn_a_log, gdn_dt_bias=gdn_dt_bias, gdn_out_norm_gain=gdn_out_norm_gain, w_out=w_out, loss_target=loss_target, m_norm_gain=m_norm_gain, m_w_in=m_w_in, m_mla_q_a_gain=m_mla_q_a_gain, m_mla_kv_a_gain=m_mla_kv_a_gain, m_w_uq=m_w_uq, m_w_ukv=m_w_ukv, m_mla_q_norm_gain=m_mla_q_norm_gain, m_mla_k_norm_gain=m_mla_k_norm_gain, m_gdn_conv_w=m_gdn_conv_w, m_gdn_a_log=m_gdn_a_log, m_gdn_dt_bias=m_gdn_dt_bias, m_gdn_out_norm_gain=m_gdn_out_norm_gain, m_w_out=m_w_out, v_norm_gain=v_norm_gain, v_w_in=v_w_in, v_mla_q_a_gain=v_mla_q_a_gain, v_mla_kv_a_gain=v_mla_kv_a_gain, v_w_uq=v_w_uq, v_w_ukv=v_w_ukv, v_mla_q_norm_gain=v_mla_q_norm_gain, v_mla_k_norm_gain=v_mla_k_norm_gain, v_gdn_conv_w=v_gdn_conv_w, v_gdn_a_log=v_gdn_a_log, v_gdn_dt_bias=v_gdn_dt_bias, v_gdn_out_norm_gain=v_gdn_out_norm_gain, v_w_out=v_w_out)
    weights = {n: given[n] for n in TWIN_WEIGHTS}
    shared = {n: given[n] for n in SHARED_INPUTS}
    per_example = {n: given[n] for n in ['x', 'positions']}
    grad_fn = _jax.value_and_grad(_loss, argnums=(0, 1))

    def one_microbatch(ex, loss_target):
        ex = dict(ex)
        diff = ex.pop(TWIN_DIFF_INPUT)
        return grad_fn(weights, diff, {**shared, **ex}, loss_target)

    if N_MICROBATCH == 1:
        loss, (grad_w, grad_x) = one_microbatch(per_example, given["loss_target"])
    else:
        def body(carry, xs):
            loss_sum, grad_sum = carry
            l_k, (gw_k, gx_k) = one_microbatch(xs[0], xs[1])
            with _jax.named_scope("update"):
                return (loss_sum + l_k, _jax.tree.map(_jnp.add, grad_sum, gw_k)), gx_k

        init = (_jnp.zeros((), _jnp.float32), _jax.tree.map(_jnp.zeros_like, weights))
        (loss, grad_w), grad_x = _jax.lax.scan(body, init, (per_example, given["loss_target"]))
    with _jax.named_scope("update"):
        delta_w, new_m, new_v = {}, {}, {}
        for n in TWIN_WEIGHTS:
            delta_w[n], new_m[n], new_v[n] = _adamw(weights[n], grad_w[n], given["m_" + n], given["v_" + n])
    return (loss, grad_x, *[grad_w[n] for n in TWIN_WEIGHTS], *[delta_w[n] for n in TWIN_WEIGHTS],
            *[new_m[n] for n in TWIN_WEIGHTS], *[new_v[n] for n in TWIN_WEIGHTS])
```

```python
import functools

import jax
import jax.numpy as jnp
from jax import lax
from jax.experimental import pallas as pl
from jax.experimental.pallas import tpu as pltpu

F32 = jnp.float32
_BF = jnp.bfloat16
HI = lax.Precision.HIGHEST

D_MODEL = 2048
HEADS = 8
NOPE = 128
ROPE = 64
QK = NOPE + ROPE
Q_LORA = 512
KV_LORA = 256
HEAD_PAD = 256
GDN_DIM = 128
WIDTH = HEADS * 128
CONV_W = 4
CHUNK = 64
ROPE_THETA = 10000.0
EPS = 1e-6
N_DEV = 8
LANE = 128
SUBLANE = 8
VMEM_LIMIT = 48 * 1024 * 1024

ADAM_LR, ADAM_B1, ADAM_B2, ADAM_EPS, ADAM_WD, ADAM_STEP = 0.001, 0.9, 0.999, 1e-08, 0.01, 10

P_MGATE, P_GQ, P_GK, P_GV, P_GGATE = 0, 1024, 2048, 3072, 4096
P_CQ, P_CKV, P_KR, P_GAB = 5120, 5632, 5888, 6016
P_COLS = 6144
R_SPLITS = (512, 256, 64, 1024, 1024, 1024, 1024, 8, 8, 1024)


def _cparams(*sem):
    return pltpu.CompilerParams(dimension_semantics=sem, vmem_limit_bytes=VMEM_LIMIT)


def _d_nn(a, b):
    return jnp.dot(a.astype(_BF), b.astype(_BF), preferred_element_type=F32)


def _d_nt(a, b):
    return lax.dot_general(a.astype(_BF), b.astype(_BF), (((1,), (1,)), ((), ())), preferred_element_type=F32)


def _d_tn(a, b):
    return lax.dot_general(a.astype(_BF), b.astype(_BF), (((0,), (0,)), ((), ())), preferred_element_type=F32)


@jax.custom_vjp
def _mm(a, b):
    return _d_nn(a, b)


_mm.defvjp(lambda a, b: (_d_nn(a, b), (a, b)), lambda r, g: (_d_nt(g, r[1]), _d_tn(r[0], g)))


@jax.custom_vjp
def _mm_nt(a, b):
    return _d_nt(a, b)


_mm_nt.defvjp(lambda a, b: (_d_nt(a, b), (a, b)), lambda r, g: (_d_nn(g, r[1]), _d_tn(g, r[0])))


@jax.custom_vjp
def _mm_tn(a, b):
    return _d_tn(a, b)


_mm_tn.defvjp(lambda a, b: (_d_tn(a, b), (a, b)), lambda r, g: (_d_nt(r[1], g), _d_nn(r[0], g)))


def _hi(a, b):
    return jnp.dot(a, b, preferred_element_type=F32, precision=HI)


def _hi_nt(a, b):
    return lax.dot_general(a, b, (((1,), (1,)), ((), ())), preferred_element_type=F32, precision=HI)


@functools.partial(jax.custom_vjp, nondiff_argnums=(1, 2))
def _roll(x, shift, axis):
    return pltpu.roll(x, shift, axis)


def _roll_fwd(x, shift, axis):
    return pltpu.roll(x, shift, axis), None


def _roll_bwd(shift, axis, _, g):
    n = g.shape[axis]
    return (pltpu.roll(g, (n - shift) % n, axis),)


_roll.defvjp(_roll_fwd, _roll_bwd)


def _rms(x, gain):
    return x * lax.rsqrt(jnp.mean(x * x, axis=-1, keepdims=True) + EPS) * gain


def matmul(name, a, b, mode, out_dtype=F32, tm=512, tn=512, tk=512):
    if mode == "nn":
        (m, k), (k2, n) = a.shape, b.shape
    elif mode == "nt":
        (m, k), (n, k2) = a.shape, b.shape
    else:
        (k, m), (k2, n) = a.shape, b.shape
    assert k == k2, (name, a.shape, b.shape)
    tm, tn, tk = min(tm, m), min(tn, n), min(tk, k)
    assert m % tm == 0 and n % tn == 0 and k % tk == 0, (name, m, n, k)
    nk = k // tk
    dot = {"nn": _d_nn, "nt": _d_nt, "tn": _d_tn}[mode]

    def body(a_ref, b_ref, o_ref, acc_ref):
        kk = pl.program_id(2)

        @pl.when(kk == 0)
        def _():
            acc_ref[...] = jnp.zeros_like(acc_ref)

        acc_ref[...] += dot(a_ref[...], b_ref[...])

        @pl.when(kk == nk - 1)
        def _():
            o_ref[...] = acc_ref[...].astype(o_ref.dtype)

    if mode == "nn":
        a_spec = pl.BlockSpec((tm, tk), lambda i, j, kk: (i, kk))
        b_spec = pl.BlockSpec((tk, tn), lambda i, j, kk: (kk, j))
    elif mode == "nt":
        a_spec = pl.BlockSpec((tm, tk), lambda i, j, kk: (i, kk))
        b_spec = pl.BlockSpec((tn, tk), lambda i, j, kk: (j, kk))
    else:
        a_spec = pl.BlockSpec((tk, tm), lambda i, j, kk: (kk, i))
        b_spec = pl.BlockSpec((tk, tn), lambda i, j, kk: (kk, j))
    return pl.pallas_call(
        body,
        name=name,
        grid=(m // tm, n // tn, nk),
        in_specs=[a_spec, b_spec],
        out_specs=pl.BlockSpec((tm, tn), lambda i, j, kk: (i, j)),
        out_shape=jax.ShapeDtypeStruct((m, n), out_dtype),
        scratch_shapes=[pltpu.VMEM((tm, tn), F32)],
        compiler_params=_cparams("parallel", "parallel", "arbitrary"),
    )(a, b)


def rowwise(name, fn, t_len, tile, row_in, full_in, row_out, acc_out=(), carries=(), reverse=False):
    tile = min(tile, t_len)
    n = t_len // tile
    assert t_len % tile == 0 and tile % SUBLANE == 0
    n_in, n_ro, n_acc, n_car = len(row_in) + len(full_in), len(row_out), len(acc_out), len(carries)

    def ti(i):
        return (n - 1 - i) if reverse else i

    in_specs, args = [], []
    for arr, kind in row_in:
        if kind[0] == "r":
            in_specs.append(pl.BlockSpec((tile, kind[1]), lambda i, c=kind[2]: (ti(i), c)))
        elif kind[0] == "h":
            in_specs.append(pl.BlockSpec((arr.shape[0], tile, arr.shape[2]), lambda i: (0, ti(i), 0)))
        else:
            in_specs.append(pl.BlockSpec(
                (SUBLANE, kind[1]), lambda i, c=kind[2]: (jnp.maximum(ti(i) * (tile // SUBLANE) - 1, 0), c)))
        args.append(arr)
    for arr in full_in:
        in_specs.append(pl.BlockSpec(arr.shape, lambda i, nd=arr.ndim: (0,) * nd))
        args.append(arr)
    out_specs, out_shape = [], []
    for kind in row_out:
        if kind[0] == "r":
            out_specs.append(pl.BlockSpec((tile, kind[1]), lambda i: (ti(i), 0)))
            out_shape.append(jax.ShapeDtypeStruct((t_len, kind[1]), kind[2]))
        else:
            out_specs.append(pl.BlockSpec((kind[1], tile, kind[2]), lambda i: (0, ti(i), 0)))
            out_shape.append(jax.ShapeDtypeStruct((kind[1], t_len, kind[2]), kind[3]))
    for shp in acc_out:
        out_specs.append(pl.BlockSpec(shp, lambda i, nd=len(shp): (0,) * nd))
        out_shape.append(jax.ShapeDtypeStruct(shp, F32))

    def body(*refs):
        in_refs = refs[:n_in]
        ro_refs = refs[n_in:n_in + n_ro]
        acc_refs = refs[n_in + n_ro:n_in + n_ro + n_acc]
        car_refs = refs[n_in + n_ro + n_acc:]
        step = pl.program_id(0)
        if n_car:
            @pl.when(step == 0)
            def _():
                for r in car_refs:
                    r[...] = jnp.zeros_like(r)
        vals = [r[...].astype(F32) for r in in_refs] + [r[...] for r in car_refs]
        outs = fn(ti(step), *vals)
        assert len(outs) == n_ro + n_acc + n_car, (name, len(outs))
        for r, o in zip(ro_refs, outs[:n_ro]):
            r[...] = o.astype(r.dtype)
        for r, o in zip(acc_refs, outs[n_ro:n_ro + n_acc]):
            @pl.when(step == 0)
            def _(r=r, o=o):
                r[...] = o

            @pl.when(step != 0)
            def _(r=r, o=o):
                r[...] += o
        for r, o in zip(car_refs, outs[n_ro + n_acc:]):
            r[...] = o

    res = pl.pallas_call(
        body,
        name=name,
        grid=(n,),
        in_specs=in_specs,
        out_specs=out_specs,
        out_shape=out_shape,
        scratch_shapes=[pltpu.VMEM(s, F32) for s in carries],
        compiler_params=_cparams("arbitrary"),
    )(*args)
    return list(res)


def _vjp_fn(fn, n_diff, n_out):
    def g(i, *a):
        ins, cts = a[:len(a) - n_out], a[len(a) - n_out:]
        diff, rest = ins[:n_diff], ins[n_diff:]
        _, pull = jax.vjp(lambda *d: tuple(fn(i, *d, *rest)), *diff)
        return tuple(pull(tuple(cts)))

    return g


def f_rms_x(i, x, gain):
    return (_rms(x, gain),)


def f_lat(i, cq, ckv, gq, gkv):
    return _rms(cq, gq), _rms(ckv, gkv)


def _rope_tables(pos, invf):
    ang = pos * invf
    lane = lax.broadcasted_iota(jnp.int32, (1, LANE), 1)
    cosv, sinv = jnp.cos(ang), jnp.sin(ang)
    half = ROPE // 2
    c = jnp.where(lane < ROPE, cosv, 0.0)
    sa = jnp.where(lane < half, -sinv, 0.0)
    sb = jnp.where((lane >= half) & (lane < ROPE), sinv, 0.0)
    return c, sa, sb


def _rope(xh, tabs):
    c, sa, sb = tabs
    half = ROPE // 2
    return xh * c + _roll(xh, LANE - half, 1) * sa + _roll(xh, half, 1) * sb


def f_head(i, q_raw, kv_raw, kr, qg, kg, pos, invf):
    tabs = _rope_tables(pos, invf)
    qs, ks, vs = [], [], []
    kr_ss = jnp.sum(kr * kr, axis=-1, keepdims=True)
    for h in range(HEADS):
        lo = q_raw[:, HEAD_PAD * h:HEAD_PAD * h + NOPE]
        hi = q_raw[:, HEAD_PAD * h + NOPE:HEAD_PAD * (h + 1)]
        ss = jnp.sum(lo * lo, axis=-1, keepdims=True) + jnp.sum(hi * hi, axis=-1, keepdims=True)
        r = lax.rsqrt(ss * (1.0 / QK) + EPS)
        qs.append(jnp.concatenate([lo * r * qg[:, :NOPE], _rope(hi * r * qg[:, NOPE:], tabs)], axis=1))
        lo = kv_raw[:, 2 * NOPE * h:2 * NOPE * h + NOPE]
        ss = jnp.sum(lo * lo, axis=-1, keepdims=True) + kr_ss
        r = lax.rsqrt(ss * (1.0 / QK) + EPS)
        ks.append(jnp.concatenate([lo * r * kg[:, :NOPE], _rope(kr * r * kg[:, NOPE:], tabs)], axis=1))
        vs.append(kv_raw[:, 2 * NOPE * h + NOPE:2 * NOPE * (h + 1)])
    return jnp.stack(qs), jnp.stack(ks), jnp.stack(vs)


def f_mix(i, o_mla, mgate, o_gdn, ggate, og):
    parts = [o_mla * jax.nn.silu(mgate)]
    for h in range(HEADS):
        parts.append(_rms(o_gdn[h], og) * jax.nn.silu(ggate[:, LANE * h:LANE * (h + 1)]))
    return (jnp.concatenate(parts, axis=1),)


def _row(a, j):
    rows = lax.broadcasted_iota(jnp.int32, a.shape, 0)
    return jnp.sum(jnp.where(rows == j, a, 0.0), axis=0, keepdims=True)


def _shift_rows(x, halo, d):
    xs = _roll(x, d, 0)
    hs = _roll(halo, d, 0)
    r8 = lax.broadcasted_iota(jnp.int32, hs.shape, 0)
    top = jnp.where(r8 < d, hs, xs[:SUBLANE])
    return jnp.concatenate([top, xs[SUBLANE:]], axis=0)


def _conv_silu(x, halo, w):
    y = _row(w, CONV_W - 1) * x
    for j in range(CONV_W - 1):
        y = y + _row(w, j) * _shift_rows(x, halo, CONV_W - 1 - j)
    return jax.nn.silu(y)


def _head_select(offset):
    r = lax.broadcasted_iota(jnp.int32, (LANE, WIDTH), 0)
    c = lax.broadcasted_iota(jnp.int32, (LANE, WIDTH), 1)
    return (r == offset + lax.shift_right_logical(c, 7)).astype(F32)


def _row_to_heads(row, sel):
    return jnp.sum(_hi(jnp.broadcast_to(row, (SUBLANE, LANE)), sel), axis=0, keepdims=True) * (1.0 / SUBLANE)


def f_gdn_pre(i, gq, gk, gv, gab, hq, hk, hv, cwq, cwk, cwv, alog, dtb):
    live = jnp.where(i == 0, 0.0, 1.0)
    q = _conv_silu(gq, hq * live, cwq)
    k = _conv_silu(gk, hk * live, cwk)
    v = _conv_silu(gv, hv * live, cwv)
    sel_a, sel_b = _head_select(0), _head_select(HEADS)
    ga = _hi(gab, sel_a)
    gb = _hi(gab, sel_b)
    g = -jnp.exp(_row_to_heads(alog, sel_a)) * jax.nn.softplus(ga + _row_to_heads(dtb, sel_a))
    beta = jax.nn.sigmoid(gb)
    qs, ks, vs, gs, bs = [], [], [], [], []
    for h in range(HEADS):
        sl = slice(LANE * h, LANE * (h + 1))
        qh, kh = q[:, sl], k[:, sl]
        qs.append(qh * lax.rsqrt(jnp.sum(qh * qh, axis=-1, keepdims=True) + EPS) * (GDN_DIM ** -0.5))
        ks.append(kh * lax.rsqrt(jnp.sum(kh * kh, axis=-1, keepdims=True) + EPS))
        vs.append(v[:, sl])
        gs.append(g[:, sl])
        bs.append(beta[:, sl])
    return jnp.stack(qs), jnp.stack(ks), jnp.stack(vs), jnp.stack(gs), jnp.stack(bs)


def gdn_pre_bwd(i, gq, gk, gv, gab, hq, hk, hv, dq, dk, dv, dg, db, cwq, cwk, cwv, alog, dtb, cq, ck, cv):
    grads = _vjp_fn(f_gdn_pre, 12, 5)(i, gq, gk, gv, gab, hq, hk, hv, cwq, cwk, cwv, alog, dtb, dq, dk, dv, dg, db)
    dgq, dgk, dgv, dgab, dhq, dhk, dhv, dcwq, dcwk, dcwv, dalog, ddtb = grads

    def add_tail(dx, carry):
        return jnp.concatenate([dx[:-SUBLANE], dx[-SUBLANE:] + carry], axis=0)

    return (add_tail(dgq, cq), add_tail(dgk, ck), add_tail(dgv, cv), dgab,
            dcwq, dcwk, dcwv, dalog, ddtb, dhq, dhk, dhv)


def f_loss(i, x, h, tgt):
    e = x + h - tgt
    part = 0.5 * jnp.sum(e * e) * (1.0 / D_MODEL)
    return e * (1.0 / D_MODEL), jnp.zeros((SUBLANE, LANE), F32) + part


def f_delta(i, o, do):
    return (jnp.stack([jnp.sum(o[:, LANE * h:LANE * (h + 1)] * do[:, LANE * h:LANE * (h + 1)], axis=-1, keepdims=True)
                       for h in range(HEADS)]),)


def _flash_tile(t_len):
    return min(512, t_len)


def flash_fwd(q, k, v):
    h_n, t_len, _ = q.shape
    tq = tk = _flash_tile(t_len)
    nq, nk = t_len // tq, t_len // tk
    scale = QK ** -0.5

    def body(q_ref, k_ref, v_ref, o_ref, lse_ref, m_s, l_s, acc_s):
        qi, kj = pl.program_id(1), pl.program_id(2)

        @pl.when(kj == 0)
        def _():
            m_s[...] = jnp.full_like(m_s, -jnp.inf)
            l_s[...] = jnp.zeros_like(l_s)
            acc_s[...] = jnp.zeros_like(acc_s)

        @pl.when(kj <= qi)
        def _():
            s = _d_nt(q_ref[0], k_ref[0]) * scale
            rows = qi * tq + lax.broadcasted_iota(jnp.int32, (tq, tk), 0)
            cols = kj * tk + lax.broadcasted_iota(jnp.int32, (tq, tk), 1)
            s = jnp.where(cols <= rows, s, -jnp.inf)
            m_new = jnp.maximum(m_s[...], jnp.max(s, axis=-1, keepdims=True))
            alpha = jnp.exp(m_s[...] - m_new)
            p = jnp.exp(s - m_new)
            l_s[...] = alpha * l_s[...] + jnp.sum(p, axis=-1, keepdims=True)
            acc_s[...] = alpha * acc_s[...] + _d_nn(p, v_ref[0])
            m_s[...] = m_new

        @pl.when(kj == nk - 1)
        def _():
            o_ref[...] = acc_s[...] / l_s[...]
            lse_ref[0] = m_s[...] + jnp.log(l_s[...])

    return pl.pallas_call(
        body,
        name="flash_fwd",
        grid=(h_n, nq, nk),
        in_specs=[
            pl.BlockSpec((1, tq, HEAD_PAD), lambda h, i, j: (h, i, 0)),
            pl.BlockSpec((1, tk, HEAD_PAD), lambda h, i, j: (h, jnp.minimum(i, j), 0)),
            pl.BlockSpec((1, tk, LANE), lambda h, i, j: (h, jnp.minimum(i, j), 0)),
        ],
        out_specs=[
            pl.BlockSpec((tq, LANE), lambda h, i, j: (i, h)),
            pl.BlockSpec((1, tq, 1), lambda h, i, j: (h, i, 0)),
        ],
        out_shape=[jax.ShapeDtypeStruct((t_len, h_n * LANE), F32), jax.ShapeDtypeStruct((h_n, t_len, 1), F32)],
        scratch_shapes=[pltpu.VMEM((tq, 1), F32), pltpu.VMEM((tq, 1), F32), pltpu.VMEM((tq, LANE), F32)],
        compiler_params=_cparams("parallel", "parallel", "arbitrary"),
    )(q, k, v)


def flash_bwd(q, k, v, do, lse, delta):
    h_n, t_len, _ = q.shape
    tq = tk = _flash_tile(t_len)
    nq, nk = t_len // tq, t_len // tk
    scale = QK ** -0.5

    def body(q_ref, k_ref, v_ref, do_ref, lse_ref, dl_ref, dq_ref, dk_ref, dv_ref):
        kj, qi = pl.program_id(1), pl.program_id(2)

        @pl.when(qi == 0)
        def _():
            dk_ref[...] = jnp.zeros_like(dk_ref)
            dv_ref[...] = jnp.zeros_like(dv_ref)

        @pl.when(qi >= kj)
        def _():
            qb, kb, dob = q_ref[0], k_ref[0], do_ref[...]
            s = _d_nt(qb, kb) * scale
            rows = qi * tq + lax.broadcasted_iota(jnp.int32, (tq, tk), 0)
            cols = kj * tk + lax.broadcasted_iota(jnp.int32, (tq, tk), 1)
            p = jnp.where(cols <= rows, jnp.exp(s - lse_ref[0]), 0.0)
            dv_ref[0] += _d_tn(p, dob)
            dp = _d_nt(dob, v_ref[0])
            ds = p * (dp - dl_ref[0]) * scale
            dk_ref[0] += _d_tn(ds, qb)
            dq_part = _d_nn(ds, kb)
            rows_q = pl.ds(pl.multiple_of(qi * tq, tq), tq)

            @pl.when(kj == 0)
            def _():
                dq_ref[0, rows_q, :] = dq_part

            @pl.when(kj != 0)
            def _():
                dq_ref[0, rows_q, :] += dq_part

    return pl.pallas_call(
        body,
        name="flash_bwd",
        grid=(h_n, nk, nq),
        in_specs=[
            pl.BlockSpec((1, tq, HEAD_PAD), lambda h, j, i: (h, jnp.maximum(i, j), 0)),
            pl.BlockSpec((1, tk, HEAD_PAD), lambda h, j, i: (h, j, 0)),
            pl.BlockSpec((1, tk, LANE), lambda h, j, i: (h, j, 0)),
            pl.BlockSpec((tq, LANE), lambda h, j, i: (jnp.maximum(i, j), h)),
            pl.BlockSpec((1, tq, 1), lambda h, j, i: (h, jnp.maximum(i, j), 0)),
            pl.BlockSpec((1, tq, 1), lambda h, j, i: (h, jnp.maximum(i, j), 0)),
        ],
        out_specs=[
            pl.BlockSpec((1, t_len, HEAD_PAD), lambda h, j, i: (h, 0, 0)),
            pl.BlockSpec((1, tk, HEAD_PAD), lambda h, j, i: (h, j, 0)),
            pl.BlockSpec((1, tk, LANE), lambda h, j, i: (h, j, 0)),
        ],
        out_shape=[
            jax.ShapeDtypeStruct((h_n, t_len, HEAD_PAD), F32),
            jax.ShapeDtypeStruct((h_n, t_len, HEAD_PAD), F32),
            jax.ShapeDtypeStruct((h_n, t_len, LANE), F32),
        ],
        compiler_params=_cparams("parallel", "arbitrary", "arbitrary"),
    )(q, k, v, do, lse, delta)


def gdn_step(s, q, k, v, gb, bb):
    c = CHUNK
    ii = lax.broadcasted_iota(jnp.int32, (c, c), 0)
    jj = lax.broadcasted_iota(jnp.int32, (c, c), 1)
    incl, strict = ii >= jj, ii > jj
    gcb = _hi(incl.astype(F32), gb)
    lane = lax.broadcasted_iota(jnp.int32, (1, LANE), 1)
    e0, e1 = (lane == 0).astype(F32), (lane == 1).astype(F32)
    diff = _hi_nt(gcb * e0 + e1, e0 - gcb * e1)
    decay = jnp.where(incl, jnp.exp(jnp.where(incl, diff, 0.0)), 0.0)
    kb, vb = k * bb, v * bb
    egc = jnp.exp(gcb)
    lmat = jnp.where(strict, _mm_nt(kb, k) * decay, 0.0)
    inv = (ii == jj).astype(F32) - lmat
    pw = _hi(lmat, lmat)
    for step in range(5):
        inv = inv + _hi(inv, pw)
        if step < 4:
            pw = _hi(pw, pw)
    u = _hi(inv, vb)
    w = _hi(inv, kb * egc)
    attn = _mm_nt(q, k) * decay
    qd = q * egc
    g_end = jnp.sum(gb, axis=0, keepdims=True)
    kd = k * jnp.exp(g_end - gcb)
    v_new = u - _mm(w, s)
    o = _mm(qd, s) + _mm(attn, v_new)
    s_new = s * jnp.exp(g_end) + _mm_tn(kd, v_new)
    return s_new, o


def gdn_fwd(q, k, v, gb, bb):
    h_n, t_len, d = q.shape
    n = t_len // CHUNK
    blk = pl.BlockSpec((h_n, CHUNK, d), lambda i: (0, i, 0))

    def body(q_ref, k_ref, v_ref, g_ref, b_ref, o_ref, sall_ref, s_s):
        @pl.when(pl.program_id(0) == 0)
        def _():
            s_s[...] = jnp.zeros_like(s_s)

        for h in range(h_n):
            s = s_s[h]
            sall_ref[0, h] = s
            s_new, o = gdn_step(s, q_ref[h], k_ref[h], v_ref[h], g_ref[h], b_ref[h])
            o_ref[h] = o
            s_s[h] = s_new

    return pl.pallas_call(
        body,
        name="gdn_fwd",
        grid=(n,),
        in_specs=[blk] * 5,
        out_specs=[blk, pl.BlockSpec((1, h_n, d, d), lambda i: (i, 0, 0, 0))],
        out_shape=[jax.ShapeDtypeStruct((h_n, t_len, d), F32), jax.ShapeDtypeStruct((n, h_n, d, d), F32)],
        scratch_shapes=[pltpu.VMEM((h_n, d, d), F32)],
        compiler_params=_cparams("arbitrary"),
    )(q, k, v, gb, bb)


def gdn_bwd(q, k, v, gb, bb, s_all, do):
    h_n, t_len, d = q.shape
    n = t_len // CHUNK
    blk = pl.BlockSpec((h_n, CHUNK, d), lambda i: (0, n - 1 - i, 0))

    def body(q_ref, k_ref, v_ref, g_ref, b_ref, sall_ref, do_ref, dq_ref, dk_ref, dv_ref, dg_ref, db_ref, ds_s):
        @pl.when(pl.program_id(0) == 0)
        def _():
            ds_s[...] = jnp.zeros_like(ds_s)

        for h in range(h_n):
            _, pull = jax.vjp(gdn_step, sall_ref[0, h], q_ref[h], k_ref[h], v_ref[h], g_ref[h], b_ref[h])
            ds, dq, dk, dv, dg, db = pull((ds_s[h], do_ref[h]))
            ds_s[h] = ds
            dq_ref[h], dk_ref[h], dv_ref[h], dg_ref[h], db_ref[h] = dq, dk, dv, dg, db

    return pl.pallas_call(
        body,
        name="gdn_bwd",
        grid=(n,),
        in_specs=[blk] * 5 + [pl.BlockSpec((1, h_n, d, d), lambda i: (n - 1 - i, 0, 0, 0)), blk],
        out_specs=[blk] * 5,
        out_shape=[jax.ShapeDtypeStruct((h_n, t_len, d), F32)] * 5,
        scratch_shapes=[pltpu.VMEM((h_n, d, d), F32)],
        compiler_params=_cparams("arbitrary"),
    )(q, k, v, gb, bb, s_all, do)


def _pad_cols(a, n):
    return jnp.pad(a, ((0, 0), (0, n - a.shape[1])))


def arrange_w_in(w):
    pieces, start = [], 0
    for n in R_SPLITS:
        pieces.append(w[:, start:start + n])
        start += n
    cq, ckv, kr, mgate, gq, gk, gv, ga, gb, ggate = pieces
    return jnp.concatenate([mgate, gq, gk, gv, ggate, cq, ckv, _pad_cols(kr, LANE),
                            _pad_cols(jnp.concatenate([ga, gb], axis=1), LANE)], axis=1)


def unarrange_w_in(g):
    def cols(start, n):
        return g[:, start:start + n]
    return jnp.concatenate([cols(P_CQ, Q_LORA), cols(P_CKV, KV_LORA), cols(P_KR, ROPE), cols(P_MGATE, WIDTH),
                            cols(P_GQ, WIDTH), cols(P_GK, WIDTH), cols(P_GV, WIDTH), cols(P_GAB, HEADS),
                            cols(P_GAB + HEADS, HEADS), cols(P_GGATE, WIDTH)], axis=1)


def arrange_w_uq(w):
    w = w.reshape(w.shape[0], HEADS, QK)
    return jnp.pad(w, ((0, 0), (0, 0), (0, HEAD_PAD - QK))).reshape(w.shape[0], HEADS * HEAD_PAD)


def unarrange_w_uq(g):
    return g.reshape(g.shape[0], HEADS, HEAD_PAD)[:, :, :QK].reshape(g.shape[0], HEADS * QK)


def local_step(x, pos, tgt, p):
    t_len = x.shape[0]
    w_in, w_uq, w_ukv, w_out = p["w_in"], p["w_uq"], p["w_ukv"], p["w_out"]
    norm_gain = p["norm_gain"].reshape(1, D_MODEL)
    qa_gain = p["mla_q_a_gain"].reshape(1, Q_LORA)
    kva_gain = p["mla_kv_a_gain"].reshape(1, KV_LORA)
    qg = _pad_cols(p["mla_q_norm_gain"].reshape(1, QK), HEAD_PAD)
    kg = _pad_cols(p["mla_k_norm_gain"].reshape(1, QK), HEAD_PAD)
    cw = p["gdn_conv_w"].reshape(CONV_W, 3 * WIDTH)
    cwq, cwk, cwv = cw[:, :WIDTH], cw[:, WIDTH:2 * WIDTH], cw[:, 2 * WIDTH:]
    alog = _pad_cols(p["gdn_a_log"].reshape(1, HEADS), LANE)
    dtb = _pad_cols(p["gdn_dt_bias"].reshape(1, HEADS), LANE)
    og = p["gdn_out_norm_gain"].reshape(1, GDN_DIM)
    half = ROPE // 2
    inv_freq = jnp.power(ROPE_THETA, -jnp.arange(half, dtype=F32) / half)
    invf = _pad_cols(jnp.concatenate([inv_freq, inv_freq]).reshape(1, ROPE), LANE)

    rt = 256
    r = "r"
    (xn,) = rowwise("rms_x", f_rms_x, t_len, rt, [(x, (r, D_MODEL, 0))], [norm_gain], [(r, D_MODEL, _BF)])
    proj = matmul("proj", xn, w_in, "nn")
    cq_in = (proj, (r, Q_LORA, P_CQ // Q_LORA))
    ckv_in = (proj, (r, KV_LORA, P_CKV // KV_LORA))
    kr_in = (proj, (r, LANE, P_KR // LANE))
    mgate_in = (proj, (r, WIDTH, P_MGATE // WIDTH))
    ggate_in = (proj, (r, WIDTH, P_GGATE // WIDTH))
    gqkv_in = [(proj, (r, WIDTH, P_GQ // WIDTH)), (proj, (r, WIDTH, P_GK // WIDTH)), (proj, (r, WIDTH, P_GV // WIDTH))]
    gab_in = (proj, (r, LANE, P_GAB // LANE))
    halos = [(proj, ("halo", WIDTH, P_GQ // WIDTH)), (proj, ("halo", WIDTH, P_GK // WIDTH)),
             (proj, ("halo", WIDTH, P_GV // WIDTH))]

    q_lat, kv_lat = rowwise("lat", f_lat, t_len, rt, [cq_in, ckv_in], [qa_gain, kva_gain],
                            [(r, Q_LORA, _BF), (r, KV_LORA, _BF)])
    q_raw = matmul("q_up", q_lat, w_uq, "nn")
    kv_raw = matmul("kv_up", kv_lat, w_ukv, "nn")
    wide = HEADS * HEAD_PAD
    head_in = [(q_raw, (r, wide, 0)), (kv_raw, (r, wide, 0)), kr_in]
    pos_in = (pos, (r, 1, 0))
    q_full, k_full, v_mla = rowwise(
        "head", lambda i, qr, kvr, kr, ps, qg_, kg_, iv: f_head(i, qr, kvr, kr, qg_, kg_, ps, iv), t_len, rt,
        head_in + [pos_in], [qg, kg, invf],
        [("h", HEADS, HEAD_PAD, _BF), ("h", HEADS, HEAD_PAD, _BF), ("h", HEADS, LANE, _BF)])
    o_mla, lse = flash_fwd(q_full, k_full, v_mla)

    pre_in = gqkv_in + [gab_in] + halos
    pre_full = [cwq, cwk, cwv, alog, dtb]
    hkind = ("h", HEADS, GDN_DIM, F32)
    gq_n, gk_n, gv_n, g_b, b_b = rowwise("gdn_pre", f_gdn_pre, t_len, rt, pre_in, pre_full, [hkind] * 5)
    o_gdn, s_all = gdn_fwd(gq_n, gk_n, gv_n, g_b, b_b)

    mix_in = [(o_mla, (r, WIDTH, 0)), mgate_in, (o_gdn, ("h",)), ggate_in]
    (mixed,) = rowwise("mix", f_mix, t_len, rt, mix_in, [og], [(r, 2 * WIDTH, _BF)])
    h_out = matmul("out_proj", mixed, w_out, "nn")
    dy, loss_acc = rowwise("loss", f_loss, t_len, rt,
                           [(x, (r, D_MODEL, 0)), (h_out, (r, D_MODEL, 0)), (tgt, (r, D_MODEL, 0))], [],
                           [(r, D_MODEL, F32)], [(SUBLANE, LANE)])
    loss = loss_acc[0, 0]

    d_mixed = matmul("d_mixed", dy, w_out, "nt")
    g_w_out = matmul("g_w_out", mixed, dy, "tn")

    def mix_bwd(i, o_mla_, mgate_, o_gdn_, ggate_, d_mixed_, og_):
        return _vjp_fn(f_mix, 5, 1)(i, o_mla_, mgate_, o_gdn_, ggate_, og_, d_mixed_)

    do_mla, d_mgate, do_gdn, d_ggate, g_og = rowwise(
        "mix_bwd", mix_bwd, t_len, rt, mix_in + [(d_mixed, (r, 2 * WIDTH, 0))], [og],
        [(r, WIDTH, F32), (r, WIDTH, F32), hkind, (r, WIDTH, F32)], [(1, GDN_DIM)])
    dq_n, dk_n, dv_n, dg_b, db_b = gdn_bwd(gq_n, gk_n, gv_n, g_b, b_b, s_all, do_gdn)
    cts_in = [(a, ("h",)) for a in (dq_n, dk_n, dv_n, dg_b, db_b)]
    d_gq, d_gk, d_gv, d_gab, g_cwq, g_cwk, g_cwv, g_alog, g_dtb = rowwise(
        "gdn_pre_bwd", gdn_pre_bwd, t_len, rt, pre_in + cts_in, pre_full,
        [(r, WIDTH, F32)] * 3 + [(r, LANE, F32)],
        [(CONV_W, WIDTH)] * 3 + [(1, LANE)] * 2, carries=[(SUBLANE, WIDTH)] * 3, reverse=True)

    (delta,) = rowwise("delta", f_delta, t_len, rt, [(o_mla, (r, WIDTH, 0)), (do_mla, (r, WIDTH, 0))], [],
                       [("h", HEADS, 1, F32)])
    dq_full, dk_full, dv_mla = flash_bwd(q_full, k_full, v_mla, do_mla, lse, delta)
    head_cts = [(a, ("h",)) for a in (dq_full, dk_full, dv_mla)]

    def head_bwd(i, q_raw_, kv_raw_, kr_, pos_, dq_, dk_, dv_, qg_, kg_, invf_):
        return _vjp_fn(f_head, 5, 3)(i, q_raw_, kv_raw_, kr_, qg_, kg_, pos_, invf_, dq_, dk_, dv_)

    dq_raw, dkv_raw, d_kr, g_qg, g_kg = rowwise(
        "head_bwd", head_bwd, t_len, rt // 2, head_in + [pos_in] + head_cts, [qg, kg, invf],
        [(r, wide, F32), (r, wide, F32), (r, LANE, F32)], [(1, HEAD_PAD), (1, HEAD_PAD)])
    dq_lat = matmul("dq_lat", dq_raw, w_uq, "nt")
    g_w_uq = matmul("g_w_uq", q_lat, dq_raw, "tn")
    dkv_lat = matmul("dkv_lat", dkv_raw, w_ukv, "nt")
    g_w_ukv = matmul("g_w_ukv", kv_lat, dkv_raw, "tn")

    def lat_bwd(i, cq_, ckv_, dql_, dkl_, gq_, gkv_):
        return _vjp_fn(f_lat, 4, 2)(i, cq_, ckv_, gq_, gkv_, dql_, dkl_)

    d_cq, d_ckv, g_qa, g_kva = rowwise(
        "lat_bwd", lat_bwd, t_len, rt, [cq_in, ckv_in, (dq_lat, (r, Q_LORA, 0)), (dkv_lat, (r, KV_LORA, 0))],
        [qa_gain, kva_gain], [(r, Q_LORA, F32), (r, KV_LORA, F32)], [(1, Q_LORA), (1, KV_LORA)])

    d_proj = jnp.concatenate([d_mgate, d_gq, d_gk, d_gv, d_ggate, d_cq, d_ckv, d_kr, d_gab], axis=1)
    d_xn = matmul("d_xn", d_proj, w_in, "nt")
    g_w_in = matmul("g_w_in", xn, d_proj, "tn")

    def rms_x_bwd(i, x_, dxn_, dy_, gain_):
        dx, dgain = _vjp_fn(f_rms_x, 2, 1)(i, x_, gain_, dxn_)
        return dx + dy_, dgain

    grad_x, g_norm = rowwise("rms_x_bwd", rms_x_bwd, t_len, rt,
                             [(x, (r, D_MODEL, 0)), (d_xn, (r, D_MODEL, 0)), (dy, (r, D_MODEL, 0))], [norm_gain],
                             [(r, D_MODEL, F32)], [(1, D_MODEL)])

    grads = {
        "norm_gain": g_norm, "w_in": g_w_in, "mla_q_a_gain": g_qa, "mla_kv_a_gain": g_kva, "w_uq": g_w_uq,
        "w_ukv": g_w_ukv, "mla_q_norm_gain": g_qg[:, :QK], "mla_k_norm_gain": g_kg[:, :QK],
        "gdn_conv_w": jnp.concatenate([g_cwq, g_cwk, g_cwv], axis=1), "gdn_a_log": g_alog[:, :HEADS],
        "gdn_dt_bias": g_dtb[:, :HEADS], "gdn_out_norm_gain": g_og, "w_out": g_w_out,
    }
    return loss, grad_x, grads


MESH = pl.DeviceIdType.MESH
ANY = pl.BlockSpec(memory_space=pl.ANY)
CHIP_FLIPS = ((1, 0), (0, 1), (1, 1))


def _place():
    return lax.axis_index("x"), lax.axis_index("y"), lax.axis_index("c")


def _flip(v, f):
    return 1 - v if f else v


def all_gather(shards):
    n_arr = len(shards)

    def body(*refs):
        x_refs, o_refs = refs[:n_arr], refs[n_arr:2 * n_arr]
        send_sems, recv_sems, local_sems = refs[2 * n_arr:]
        x, y, c = _place()
        me, sibling = (x, y, c), (x, y, 1 - c)
        chips = [(_flip(x, fx), _flip(y, fy)) for fx, fy in CHIP_FLIPS]

        def copy(a, k, block, to, src=None):
            px, py, pc = block
            dst = o_refs[a].at[4 * px + 2 * py + pc]
            return pltpu.make_async_remote_copy(
                src_ref=dst if src is None else src, dst_ref=dst, send_sem=send_sems.at[a, k],
                recv_sem=recv_sems.at[a, k], device_id=to, device_id_type=MESH)

        mine, first, passed = [], [], []
        for a in range(n_arr):
            cp = pltpu.make_async_copy(x_refs[a], o_refs[a].at[4 * x + 2 * y + c], local_sems.at[a])
            cp.start()
            mine.append(cp)
            first.append(copy(a, 0, me, sibling, src=x_refs[a]))
            first += [copy(a, 1 + j, me, (*chip, c), src=x_refs[a]) for j, chip in enumerate(chips)]
        for cp in first:
            cp.start()
        for j, chip in enumerate(chips):
            for a in range(n_arr):
                copy(a, 1 + j, (*chip, c), me).wait_recv()
                cp = copy(a, 4 + j, (*chip, c), sibling)
                cp.start()
                passed.append(cp)
        for a in range(n_arr):
            copy(a, 0, sibling, me).wait_recv()
            for j, chip in enumerate(chips):
                copy(a, 4 + j, (*chip, 1 - c), me).wait_recv()
        for cp in first + passed:
            cp.wait_send()
        for cp in mine:
            cp.wait()

    return pl.pallas_call(
        body,
        name="all_gather",
        out_shape=[jax.ShapeDtypeStruct((N_DEV,) + s.shape, s.dtype) for s in shards],
        in_specs=[ANY] * n_arr,
        out_specs=[ANY] * n_arr,
        scratch_shapes=[pltpu.SemaphoreType.DMA((n_arr, 7)), pltpu.SemaphoreType.DMA((n_arr, 7)),
                        pltpu.SemaphoreType.DMA((n_arr,))],
    )(*shards)


def exchange_cores(grads):
    n_arr = len(grads)

    def body(*refs):
        g_refs, o_refs = refs[:n_arr], refs[n_arr:2 * n_arr]
        send_sems, recv_sems = refs[2 * n_arr:]
        x, y, c = _place()
        copies = []
        for a in range(n_arr):
            for q in range(4):
                cp = pltpu.make_async_remote_copy(
                    src_ref=g_refs[a].at[2 * q + (1 - c)], dst_ref=o_refs[a].at[q], send_sem=send_sems.at[a, q],
                    recv_sem=recv_sems.at[a, q], device_id=(x, y, 1 - c), device_id_type=MESH)
                cp.start()
                copies.append(cp)
        for cp in copies:
            cp.wait()

    return pl.pallas_call(
        body,
        name="exchange_cores",
        out_shape=[jax.ShapeDtypeStruct((4,) + g.shape[1:], g.dtype) for g in grads],
        in_specs=[ANY] * n_arr,
        out_specs=[ANY] * n_arr,
        scratch_shapes=[pltpu.SemaphoreType.DMA((n_arr, 4)), pltpu.SemaphoreType.DMA((n_arr, 4))],
    )(*grads)


def exchange_chips(parts):
    n_arr = len(parts)

    def body(*refs):
        p_refs, o_refs = refs[:n_arr], refs[n_arr:2 * n_arr]
        send_sems, recv_sems = refs[2 * n_arr:]
        x, y, c = _place()
        copies = []
        for a in range(n_arr):
            for j, (fx, fy) in enumerate(CHIP_FLIPS):
                px, py = _flip(x, fx), _flip(y, fy)
                cp = pltpu.make_async_remote_copy(
                    src_ref=p_refs[a].at[2 * px + py], dst_ref=o_refs[a].at[j], send_sem=send_sems.at[a, j],
                    recv_sem=recv_sems.at[a, j], device_id=(px, py, c), device_id_type=MESH)
                cp.start()
                copies.append(cp)
        for cp in copies:
            cp.wait()

    return pl.pallas_call(
        body,
        name="exchange_chips",
        out_shape=[jax.ShapeDtypeStruct((3,) + p.shape[1:], p.dtype) for p in parts],
        in_specs=[ANY] * n_arr,
        out_specs=[ANY] * n_arr,
        scratch_shapes=[pltpu.SemaphoreType.DMA((n_arr, 3)), pltpu.SemaphoreType.DMA((n_arr, 3))],
    )(*parts)


def gather_small(v):
    def body(v_ref, o_ref, send_sems, recv_sems, local_sem):
        x, y, c = _place()
        me = 4 * x + 2 * y + c
        mine = pltpu.make_async_copy(v_ref, o_ref.at[me], local_sem)
        mine.start()
        copies = []
        for k in range(1, N_DEV):
            fx, fy, fc = (k >> 2) & 1, (k >> 1) & 1, k & 1
            cp = pltpu.make_async_remote_copy(
                src_ref=v_ref, dst_ref=o_ref.at[me], send_sem=send_sems.at[k - 1], recv_sem=recv_sems.at[k - 1],
                device_id=(_flip(x, fx), _flip(y, fy), _flip(c, fc)), device_id_type=MESH)
            cp.start()
            copies.append(cp)
        for cp in copies:
            cp.wait()
        mine.wait()

    return pl.pallas_call(
        body,
        name="gather_small",
        out_shape=jax.ShapeDtypeStruct((N_DEV,) + v.shape, v.dtype),
        in_specs=[ANY],
        out_specs=ANY,
        scratch_shapes=[pltpu.SemaphoreType.DMA((N_DEV - 1,)), pltpu.SemaphoreType.DMA((N_DEV - 1,)),
                        pltpu.SemaphoreType.DMA],
    )(v)


def _row_tile(rows):
    for t in (256, 128, 64, 32, 16, 8):
        if rows % t == 0:
            return t
    return rows


def add_core_parts(name, g, recv, c_idx):
    _, rows, cols = g.shape
    tr = _row_tile(rows)

    def body(c_ref, g_ref, r_ref, o_ref):
        o_ref[...] = g_ref[...] + r_ref[...]

    return pl.pallas_call(
        body,
        name=name,
        grid_spec=pltpu.PrefetchScalarGridSpec(
            num_scalar_prefetch=1,
            grid=(4, rows // tr),
            in_specs=[pl.BlockSpec((1, tr, cols), lambda q, i, c_ref: (2 * q + c_ref[0], i, 0)),
                      pl.BlockSpec((1, tr, cols), lambda q, i, c_ref: (q, i, 0))],
            out_specs=pl.BlockSpec((1, tr, cols), lambda q, i, c_ref: (q, i, 0)),
        ),
        out_shape=jax.ShapeDtypeStruct((4, rows, cols), F32),
        compiler_params=_cparams("parallel", "parallel"),
    )(c_idx, g, recv)


def _adamw(w, g, m, v):
    m = ADAM_B1 * m + (1.0 - ADAM_B1) * g
    v = ADAM_B2 * v + (1.0 - ADAM_B2) * (g * g)
    m_hat = m / (1.0 - ADAM_B1 ** ADAM_STEP)
    v_hat = v / (1.0 - ADAM_B2 ** ADAM_STEP)
    delta = -ADAM_LR * (m_hat / (jnp.sqrt(v_hat) + ADAM_EPS) + ADAM_WD * w)
    return delta, m, v


def adamw_sharded(name, parts, recv, q_idx, w, m, v):
    rows, cols = w.shape
    tr = _row_tile(rows)

    def body(q_ref, p_ref, r_ref, w_ref, m_ref, v_ref, g_out, d_out, m_out, v_out):
        g = p_ref[0] + r_ref[0] + r_ref[1] + r_ref[2]
        d, m_new, v_new = _adamw(w_ref[...], g, m_ref[...], v_ref[...])
        g_out[...], d_out[...], m_out[...], v_out[...] = g, d, m_new, v_new

    blk = pl.BlockSpec((tr, cols), lambda i, q_ref: (i, 0))
    return pl.pallas_call(
        body,
        name=name,
        grid_spec=pltpu.PrefetchScalarGridSpec(
            num_scalar_prefetch=1,
            grid=(rows // tr,),
            in_specs=[pl.BlockSpec((1, tr, cols), lambda i, q_ref: (q_ref[0], i, 0)),
                      pl.BlockSpec((3, tr, cols), lambda i, q_ref: (0, i, 0)), blk, blk, blk],
            out_specs=[blk] * 4,
        ),
        out_shape=[jax.ShapeDtypeStruct((rows, cols), F32)] * 4,
        compiler_params=_cparams("parallel"),
    )(q_idx, parts, recv, w, m, v)


def adamw_small(gathered, w, m, v):
    def body(g_ref, w_ref, m_ref, v_ref, g_out, d_out, m_out, v_out):
        g = g_ref[0]
        for j in range(1, N_DEV):
            g = g + g_ref[j]
        d, m_new, v_new = _adamw(w_ref[...], g, m_ref[...], v_ref[...])
        g_out[...], d_out[...], m_out[...], v_out[...] = g, d, m_new, v_new

    return pl.pallas_call(body, name="adamw_small", out_shape=[jax.ShapeDtypeStruct(w.shape, F32)] * 4)(gathered, w, m, v)


SHARDED = ("w_in", "w_uq", "w_ukv", "gdn_conv_w", "w_out")
SMALL = (("norm_gain", D_MODEL), ("mla_q_a_gain", Q_LORA), ("mla_kv_a_gain", KV_LORA), ("mla_q_norm_gain", QK),
         ("mla_k_norm_gain", QK), ("gdn_a_log", HEADS), ("gdn_dt_bias", HEADS), ("gdn_out_norm_gain", GDN_DIM))
WEIGHT_ORDER = ("norm_gain", "w_in", "mla_q_a_gain", "mla_kv_a_gain", "w_uq", "w_ukv", "mla_q_norm_gain",
                "mla_k_norm_gain", "gdn_conv_w", "gdn_a_log", "gdn_dt_bias", "gdn_out_norm_gain", "w_out")


def _pack_small(d):
    rows = []
    for name, n in SMALL:
        a = d[name].reshape(-1).astype(F32)
        n_pad = -(-n // LANE) * LANE
        rows.append(jnp.pad(a, (0, n_pad - n)).reshape(n_pad // LANE, LANE))
    packed = jnp.concatenate(rows, axis=0)
    return jnp.pad(packed, ((0, -packed.shape[0] % SUBLANE), (0, 0)))


def _unpack_small(packed):
    out, row = {}, 0
    for name, n in SMALL:
        n_rows = -(-n // LANE)
        out[name] = packed[row:row + n_rows].reshape(-1)[:n].reshape(1, n)
        row += n_rows
    return out


def kernel(x, positions, norm_gain, w_in, mla_q_a_gain, mla_kv_a_gain, w_uq, w_ukv, mla_q_norm_gain, mla_k_norm_gain, gdn_conv_w, gdn_a_log, gdn_dt_bias, gdn_out_norm_gain, w_out, loss_target, m_norm_gain, m_w_in, m_mla_q_a_gain, m_mla_kv_a_gain, m_w_uq, m_w_ukv, m_mla_q_norm_gain, m_mla_k_norm_gain, m_gdn_conv_w, m_gdn_a_log, m_gdn_dt_bias, m_gdn_out_norm_gain, m_w_out, v_norm_gain, v_w_in, v_mla_q_a_gain, v_mla_kv_a_gain, v_w_uq, v_w_ukv, v_mla_q_norm_gain, v_mla_k_norm_gain, v_gdn_conv_w, v_gdn_a_log, v_gdn_dt_bias, v_gdn_out_norm_gain, v_w_out):
    w = dict(norm_gain=norm_gain, w_in=w_in, mla_q_a_gain=mla_q_a_gain, mla_kv_a_gain=mla_kv_a_gain, w_uq=w_uq,
             w_ukv=w_ukv, mla_q_norm_gain=mla_q_norm_gain, mla_k_norm_gain=mla_k_norm_gain, gdn_conv_w=gdn_conv_w,
             gdn_a_log=gdn_a_log, gdn_dt_bias=gdn_dt_bias, gdn_out_norm_gain=gdn_out_norm_gain, w_out=w_out)
    m = dict(norm_gain=m_norm_gain, w_in=m_w_in, mla_q_a_gain=m_mla_q_a_gain, mla_kv_a_gain=m_mla_kv_a_gain,
             w_uq=m_w_uq, w_ukv=m_w_ukv, mla_q_norm_gain=m_mla_q_norm_gain, mla_k_norm_gain=m_mla_k_norm_gain,
             gdn_conv_w=m_gdn_conv_w, gdn_a_log=m_gdn_a_log, gdn_dt_bias=m_gdn_dt_bias,
             gdn_out_norm_gain=m_gdn_out_norm_gain, w_out=m_w_out)
    v = dict(norm_gain=v_norm_gain, w_in=v_w_in, mla_q_a_gain=v_mla_q_a_gain, mla_kv_a_gain=v_mla_kv_a_gain,
             w_uq=v_w_uq, w_ukv=v_w_ukv, mla_q_norm_gain=v_mla_q_norm_gain, mla_k_norm_gain=v_mla_k_norm_gain,
             gdn_conv_w=v_gdn_conv_w, gdn_a_log=v_gdn_a_log, gdn_dt_bias=v_gdn_dt_bias,
             gdn_out_norm_gain=v_gdn_out_norm_gain, w_out=v_w_out)
    t_len = x.shape[1]

    shards = [w[n][0] if n == "gdn_conv_w" else w[n][0].astype(_BF) for n in SHARDED]
    a_w_in, a_w_uq, a_w_ukv, a_cw, a_w_out = all_gather(shards)

    def cols_whole(g):
        return g.transpose(1, 0, 2).reshape(g.shape[1], N_DEV * g.shape[2])

    p = {n: w[n] for n, _ in SMALL}
    p["w_in"] = arrange_w_in(cols_whole(a_w_in))
    p["w_uq"] = arrange_w_uq(cols_whole(a_w_uq))
    p["w_ukv"] = cols_whole(a_w_ukv)
    p["gdn_conv_w"] = cols_whole(a_cw)
    p["w_out"] = a_w_out.reshape(N_DEV * a_w_out.shape[1], a_w_out.shape[2])

    pos = positions.reshape(t_len, 1).astype(F32)
    loss, grad_x, grads = local_step(x.reshape(t_len, D_MODEL), pos, loss_target.reshape(t_len, D_MODEL), p)
    loss = lax.psum(loss, ("x", "y", "c"))

    def col_blocks(g):
        return g.reshape(g.shape[0], N_DEV, g.shape[1] // N_DEV).transpose(1, 0, 2)

    blocks = [col_blocks(unarrange_w_in(grads["w_in"])), col_blocks(unarrange_w_uq(grads["w_uq"])),
              col_blocks(grads["w_ukv"]), col_blocks(grads["gdn_conv_w"]),
              grads["w_out"].reshape(N_DEV, D_MODEL // N_DEV, D_MODEL)]
    xi, yi, ci = _place()
    c_idx = jnp.reshape(ci, (1,)).astype(jnp.int32)
    q_idx = jnp.reshape(2 * xi + yi, (1,)).astype(jnp.int32)
    from_sibling = exchange_cores(blocks)
    parts = [add_core_parts("add_" + n, g, r, c_idx) for n, g, r in zip(SHARDED, blocks, from_sibling)]
    from_chips = exchange_chips(parts)
    out = {}
    for n, prt, rcv in zip(SHARDED, parts, from_chips):
        shape = w[n].shape
        res = adamw_sharded("adamw_" + n, prt, rcv, q_idx, w[n].reshape(shape[-2:]), m[n].reshape(shape[-2:]),
                            v[n].reshape(shape[-2:]))
        out[n] = [a.reshape(shape) for a in res]

    small_all = gather_small(_pack_small(grads))
    res = adamw_small(small_all, _pack_small(w), _pack_small(m), _pack_small(v))
    unpacked = [_unpack_small(a) for a in res]
    for n, _ in SMALL:
        out[n] = [u[n] for u in unpacked]

    return (loss, grad_x.reshape(x.shape), *[out[n][0] for n in WEIGHT_ORDER], *[out[n][1] for n in WEIGHT_ORDER],
            *[out[n][2] for n in WEIGHT_ORDER], *[out[n][3] for n in WEIGHT_ORDER])
```

```python
import functools

import jax
import jax.numpy as jnp
from jax import lax
from jax.experimental import pallas as pl
from jax.experimental.pallas import tpu as pltpu

F32 = jnp.float32
_BF = jnp.bfloat16
HI = lax.Precision.HIGHEST

D_MODEL = 2048
HEADS = 8
NOPE = 128
ROPE = 64
QK = NOPE + ROPE
Q_LORA = 512
KV_LORA = 256
HEAD_PAD = 256
GDN_DIM = 128
WIDTH = HEADS * 128
CONV_W = 4
CHUNK = 64
ROPE_THETA = 10000.0
EPS = 1e-6
N_DEV = 8
LANE = 128
SUBLANE = 8
VMEM_LIMIT = 48 * 1024 * 1024

ADAM_LR, ADAM_B1, ADAM_B2, ADAM_EPS, ADAM_WD, ADAM_STEP = 0.001, 0.9, 0.999, 1e-08, 0.01, 10

P_MGATE, P_GQ, P_GK, P_GV, P_GGATE = 0, 1024, 2048, 3072, 4096
P_CQ, P_CKV, P_KR, P_GAB = 5120, 5632, 5888, 6016
P_COLS = 6144
R_SPLITS = (512, 256, 64, 1024, 1024, 1024, 1024, 8, 8, 1024)


def _cparams(*sem):
    return pltpu.CompilerParams(dimension_semantics=sem, vmem_limit_bytes=VMEM_LIMIT)


def _d_nn(a, b):
    return jnp.dot(a.astype(_BF), b.astype(_BF), preferred_element_type=F32)


def _d_nt(a, b):
    return lax.dot_general(a.astype(_BF), b.astype(_BF), (((1,), (1,)), ((), ())), preferred_element_type=F32)


def _d_tn(a, b):
    return lax.dot_general(a.astype(_BF), b.astype(_BF), (((0,), (0,)), ((), ())), preferred_element_type=F32)


@jax.custom_vjp
def _mm(a, b):
    return _d_nn(a, b)


_mm.defvjp(lambda a, b: (_d_nn(a, b), (a, b)), lambda r, g: (_d_nt(g, r[1]), _d_tn(r[0], g)))


@jax.custom_vjp
def _mm_nt(a, b):
    return _d_nt(a, b)


_mm_nt.defvjp(lambda a, b: (_d_nt(a, b), (a, b)), lambda r, g: (_d_nn(g, r[1]), _d_tn(g, r[0])))


@jax.custom_vjp
def _mm_tn(a, b):
    return _d_tn(a, b)


_mm_tn.defvjp(lambda a, b: (_d_tn(a, b), (a, b)), lambda r, g: (_d_nt(r[1], g), _d_nn(r[0], g)))


def _hi(a, b):
    return jnp.dot(a, b, preferred_element_type=F32, precision=HI)


_NN3 = (((2,), (1,)), ((0,), (0,)))
_NT3 = (((2,), (2,)), ((0,), (0,)))
_TN3 = (((1,), (1,)), ((0,), (0,)))


def _bdot(a, b, dims, hi):
    if hi:
        return lax.dot_general(a, b, dims, preferred_element_type=F32, precision=hi)
    return lax.dot_general(a.astype(_BF), b.astype(_BF), dims, preferred_element_type=F32)


def _batched_matmuls(hi):
    nn = jax.custom_vjp(lambda a, b: _bdot(a, b, _NN3, hi))
    nt = jax.custom_vjp(lambda a, b: _bdot(a, b, _NT3, hi))
    tn = jax.custom_vjp(lambda a, b: _bdot(a, b, _TN3, hi))
    nn.defvjp(lambda a, b: (_bdot(a, b, _NN3, hi), (a, b)),
              lambda r, g: (_bdot(g, r[1], _NT3, hi), _bdot(r[0], g, _TN3, hi)))
    nt.defvjp(lambda a, b: (_bdot(a, b, _NT3, hi), (a, b)),
              lambda r, g: (_bdot(g, r[1], _NN3, hi), _bdot(g, r[0], _TN3, hi)))
    tn.defvjp(lambda a, b: (_bdot(a, b, _TN3, hi), (a, b)),
              lambda r, g: (_bdot(r[1], g, _NT3, hi), _bdot(r[0], g, _NN3, hi)))
    return nn, nt, tn


_bmm, _bmm_nt, _bmm_tn = _batched_matmuls(False)
_bhi, _bhi_nt, _bhi_tn = _batched_matmuls(HI)
_bh3, _bh3_nt, _bh3_tn = _batched_matmuls(lax.Precision.HIGH)


@functools.partial(jax.custom_vjp, nondiff_argnums=(1, 2))
def _roll(x, shift, axis):
    return pltpu.roll(x, shift, axis)


def _roll_fwd(x, shift, axis):
    return pltpu.roll(x, shift, axis), None


def _roll_bwd(shift, axis, _, g):
    n = g.shape[axis]
    return (pltpu.roll(g, (n - shift) % n, axis),)


_roll.defvjp(_roll_fwd, _roll_bwd)


def _rms(x, gain):
    return x * lax.rsqrt(jnp.mean(x * x, axis=-1, keepdims=True) + EPS) * gain


def matmul(name, a, b, mode, out_dtype=F32, tm=512, tn=512, tk=512):
    if mode == "nn":
        (m, k), (k2, n) = a.shape, b.shape
    elif mode == "nt":
        (m, k), (n, k2) = a.shape, b.shape
    else:
        (k, m), (k2, n) = a.shape, b.shape
    assert k == k2, (name, a.shape, b.shape)
    tm, tn, tk = min(tm, m), min(tn, n), min(tk, k)
    assert m % tm == 0 and n % tn == 0 and k % tk == 0, (name, m, n, k)
    nk = k // tk
    dot = {"nn": _d_nn, "nt": _d_nt, "tn": _d_tn}[mode]

    def body(a_ref, b_ref, o_ref, acc_ref):
        kk = pl.program_id(2)

        @pl.when(kk == 0)
        def _():
            acc_ref[...] = jnp.zeros_like(acc_ref)

        acc_ref[...] += dot(a_ref[...], b_ref[...])

        @pl.when(kk == nk - 1)
        def _():
            o_ref[...] = acc_ref[...].astype(o_ref.dtype)

    if mode == "nn":
        a_spec = pl.BlockSpec((tm, tk), lambda i, j, kk: (i, kk))
        b_spec = pl.BlockSpec((tk, tn), lambda i, j, kk: (kk, j))
    elif mode == "nt":
        a_spec = pl.BlockSpec((tm, tk), lambda i, j, kk: (i, kk))
        b_spec = pl.BlockSpec((tn, tk), lambda i, j, kk: (j, kk))
    else:
        a_spec = pl.BlockSpec((tk, tm), lambda i, j, kk: (kk, i))
        b_spec = pl.BlockSpec((tk, tn), lambda i, j, kk: (kk, j))
    return pl.pallas_call(
        body,
        name=name,
        grid=(m // tm, n // tn, nk),
        in_specs=[a_spec, b_spec],
        out_specs=pl.BlockSpec((tm, tn), lambda i, j, kk: (i, j)),
        out_shape=jax.ShapeDtypeStruct((m, n), out_dtype),
        scratch_shapes=[pltpu.VMEM((tm, tn), F32)],
        compiler_params=_cparams("parallel", "parallel", "arbitrary"),
    )(a, b)


def rowwise(name, fn, t_len, tile, row_in, full_in, row_out, acc_out=(), carries=(), reverse=False):
    tile = min(tile, t_len)
    n = t_len // tile
    assert t_len % tile == 0 and tile % SUBLANE == 0
    n_in, n_ro, n_acc, n_car = len(row_in) + len(full_in), len(row_out), len(acc_out), len(carries)

    def ti(i):
        return (n - 1 - i) if reverse else i

    in_specs, args = [], []
    for arr, kind in row_in:
        if kind[0] == "r":
            in_specs.append(pl.BlockSpec((tile, kind[1]), lambda i, c=kind[2]: (ti(i), c)))
        elif kind[0] == "h":
            in_specs.append(pl.BlockSpec((arr.shape[0], tile, arr.shape[2]), lambda i: (0, ti(i), 0)))
        else:
            in_specs.append(pl.BlockSpec(
                (SUBLANE, kind[1]), lambda i, c=kind[2]: (jnp.maximum(ti(i) * (tile // SUBLANE) - 1, 0), c)))
        args.append(arr)
    for arr in full_in:
        in_specs.append(pl.BlockSpec(arr.shape, lambda i, nd=arr.ndim: (0,) * nd))
        args.append(arr)
    out_specs, out_shape = [], []
    for kind in row_out:
        if kind[0] == "r":
            out_specs.append(pl.BlockSpec((tile, kind[1]), lambda i: (ti(i), 0)))
            out_shape.append(jax.ShapeDtypeStruct((t_len, kind[1]), kind[2]))
        else:
            out_specs.append(pl.BlockSpec((kind[1], tile, kind[2]), lambda i: (0, ti(i), 0)))
            out_shape.append(jax.ShapeDtypeStruct((kind[1], t_len, kind[2]), kind[3]))
    for shp in acc_out:
        out_specs.append(pl.BlockSpec(shp, lambda i, nd=len(shp): (0,) * nd))
        out_shape.append(jax.ShapeDtypeStruct(shp, F32))

    def body(*refs):
        in_refs = refs[:n_in]
        ro_refs = refs[n_in:n_in + n_ro]
        acc_refs = refs[n_in + n_ro:n_in + n_ro + n_acc]
        car_refs = refs[n_in + n_ro + n_acc:]
        step = pl.program_id(0)
        if n_car:
            @pl.when(step == 0)
            def _():
                for r in car_refs:
                    r[...] = jnp.zeros_like(r)
        vals = [r[...].astype(F32) for r in in_refs] + [r[...] for r in car_refs]
        outs = fn(ti(step), *vals)
        assert len(outs) == n_ro + n_acc + n_car, (name, len(outs))
        for r, o in zip(ro_refs, outs[:n_ro]):
            r[...] = o.astype(r.dtype)
        for r, o in zip(acc_refs, outs[n_ro:n_ro + n_acc]):
            @pl.when(step == 0)
            def _(r=r, o=o):
                r[...] = o

            @pl.when(step != 0)
            def _(r=r, o=o):
                r[...] += o
        for r, o in zip(car_refs, outs[n_ro + n_acc:]):
            r[...] = o

    res = pl.pallas_call(
        body,
        name=name,
        grid=(n,),
        in_specs=in_specs,
        out_specs=out_specs,
        out_shape=out_shape,
        scratch_shapes=[pltpu.VMEM(s, F32) for s in carries],
        compiler_params=_cparams("arbitrary"),
    )(*args)
    return list(res)


def _vjp_fn(fn, n_diff, n_out):
    def g(i, *a):
        ins, cts = a[:len(a) - n_out], a[len(a) - n_out:]
        diff, rest = ins[:n_diff], ins[n_diff:]
        _, pull = jax.vjp(lambda *d: tuple(fn(i, *d, *rest)), *diff)
        return tuple(pull(tuple(cts)))

    return g


def f_rms_x(i, x, gain):
    return (_rms(x, gain),)


def f_lat(i, cq, ckv, gq, gkv):
    return _rms(cq, gq), _rms(ckv, gkv)


def _rope_tables(pos, invf):
    ang = pos * invf
    lane = lax.broadcasted_iota(jnp.int32, (1, LANE), 1)
    cosv, sinv = jnp.cos(ang), jnp.sin(ang)
    half = ROPE // 2
    c = jnp.where(lane < ROPE, cosv, 0.0)
    sa = jnp.where(lane < half, -sinv, 0.0)
    sb = jnp.where((lane >= half) & (lane < ROPE), sinv, 0.0)
    return c, sa, sb


def _rope(xh, tabs):
    c, sa, sb = tabs
    half = ROPE // 2
    return xh * c + _roll(xh, LANE - half, 1) * sa + _roll(xh, half, 1) * sb


def f_head(i, q_raw, kv_raw, kr, qg, kg, pos, invf):
    tabs = _rope_tables(pos, invf)
    qs, ks, vs = [], [], []
    kr_ss = jnp.sum(kr * kr, axis=-1, keepdims=True)
    for h in range(HEADS):
        lo = q_raw[:, HEAD_PAD * h:HEAD_PAD * h + NOPE]
        hi = q_raw[:, HEAD_PAD * h + NOPE:HEAD_PAD * (h + 1)]
        ss = jnp.sum(lo * lo, axis=-1, keepdims=True) + jnp.sum(hi * hi, axis=-1, keepdims=True)
        r = lax.rsqrt(ss * (1.0 / QK) + EPS)
        qs.append(jnp.concatenate([lo * r * qg[:, :NOPE], _rope(hi * r * qg[:, NOPE:], tabs)], axis=1))
        lo = kv_raw[:, 2 * NOPE * h:2 * NOPE * h + NOPE]
        ss = jnp.sum(lo * lo, axis=-1, keepdims=True) + kr_ss
        r = lax.rsqrt(ss * (1.0 / QK) + EPS)
        ks.append(jnp.concatenate([lo * r * kg[:, :NOPE], _rope(kr * r * kg[:, NOPE:], tabs)], axis=1))
        vs.append(kv_raw[:, 2 * NOPE * h + NOPE:2 * NOPE * (h + 1)])
    return jnp.stack(qs), jnp.stack(ks), jnp.stack(vs)


def f_mix(i, o_mla, mgate, o_gdn, ggate, og):
    parts = [o_mla * jax.nn.silu(mgate)]
    for h in range(HEADS):
        parts.append(_rms(o_gdn[h], og) * jax.nn.silu(ggate[:, LANE * h:LANE * (h + 1)]))
    return (jnp.concatenate(parts, axis=1),)


def _row(a, j):
    rows = lax.broadcasted_iota(jnp.int32, a.shape, 0)
    return jnp.sum(jnp.where(rows == j, a, 0.0), axis=0, keepdims=True)


def _shift_rows(x, halo, d):
    xs = _roll(x, d, 0)
    hs = _roll(halo, d, 0)
    r8 = lax.broadcasted_iota(jnp.int32, hs.shape, 0)
    top = jnp.where(r8 < d, hs, xs[:SUBLANE])
    return jnp.concatenate([top, xs[SUBLANE:]], axis=0)


def _conv_silu(x, halo, w):
    y = _row(w, CONV_W - 1) * x
    for j in range(CONV_W - 1):
        y = y + _row(w, j) * _shift_rows(x, halo, CONV_W - 1 - j)
    return jax.nn.silu(y)


def _head_select(offset):
    r = lax.broadcasted_iota(jnp.int32, (LANE, WIDTH), 0)
    c = lax.broadcasted_iota(jnp.int32, (LANE, WIDTH), 1)
    return (r == offset + lax.shift_right_logical(c, 7)).astype(F32)


def _row_to_heads(row, sel):
    return jnp.sum(_hi(jnp.broadcast_to(row, (SUBLANE, LANE)), sel), axis=0, keepdims=True) * (1.0 / SUBLANE)


def f_gdn_pre(i, gq, gk, gv, gab, hq, hk, hv, cwq, cwk, cwv, alog, dtb):
    live = jnp.where(i == 0, 0.0, 1.0)
    q = _conv_silu(gq, hq * live, cwq)
    k = _conv_silu(gk, hk * live, cwk)
    v = _conv_silu(gv, hv * live, cwv)
    sel_a, sel_b = _head_select(0), _head_select(HEADS)
    ga = _hi(gab, sel_a)
    gb = _hi(gab, sel_b)
    g = -jnp.exp(_row_to_heads(alog, sel_a)) * jax.nn.softplus(ga + _row_to_heads(dtb, sel_a))
    beta = jax.nn.sigmoid(gb)
    qs, ks, vs, gs, bs = [], [], [], [], []
    for h in range(HEADS):
        sl = slice(LANE * h, LANE * (h + 1))
        qh, kh = q[:, sl], k[:, sl]
        qs.append(qh * lax.rsqrt(jnp.sum(qh * qh, axis=-1, keepdims=True) + EPS) * (GDN_DIM ** -0.5))
        ks.append(kh * lax.rsqrt(jnp.sum(kh * kh, axis=-1, keepdims=True) + EPS))
        vs.append(v[:, sl])
        gs.append(g[:, sl])
        bs.append(beta[:, sl])
    return jnp.stack(qs), jnp.stack(ks), jnp.stack(vs), jnp.stack(gs), jnp.stack(bs)


def gdn_pre_bwd(i, gq, gk, gv, gab, hq, hk, hv, dq, dk, dv, dg, db, cwq, cwk, cwv, alog, dtb, cq, ck, cv):
    grads = _vjp_fn(f_gdn_pre, 12, 5)(i, gq, gk, gv, gab, hq, hk, hv, cwq, cwk, cwv, alog, dtb, dq, dk, dv, dg, db)
    dgq, dgk, dgv, dgab, dhq, dhk, dhv, dcwq, dcwk, dcwv, dalog, ddtb = grads

    def add_tail(dx, carry):
        return jnp.concatenate([dx[:-SUBLANE], dx[-SUBLANE:] + carry], axis=0)

    return (add_tail(dgq, cq), add_tail(dgk, ck), add_tail(dgv, cv), dgab,
            dcwq, dcwk, dcwv, dalog, ddtb, dhq, dhk, dhv)


def f_loss(i, x, h, tgt):
    e = x + h - tgt
    part = 0.5 * jnp.sum(e * e) * (1.0 / D_MODEL)
    return e * (1.0 / D_MODEL), jnp.zeros((SUBLANE, LANE), F32) + part


def f_delta(i, o, do):
    return (jnp.stack([jnp.sum(o[:, LANE * h:LANE * (h + 1)] * do[:, LANE * h:LANE * (h + 1)], axis=-1, keepdims=True)
                       for h in range(HEADS)]),)


def _flash_tile(t_len):
    return min(512, t_len)


def flash_fwd(q, k, v):
    h_n, t_len, _ = q.shape
    tq = tk = _flash_tile(t_len)
    nq, nk = t_len // tq, t_len // tk
    scale = QK ** -0.5

    def body(q_ref, k_ref, v_ref, o_ref, lse_ref, m_s, l_s, acc_s):
        qi, kj = pl.program_id(1), pl.program_id(2)

        @pl.when(kj == 0)
        def _():
            m_s[...] = jnp.full_like(m_s, -jnp.inf)
            l_s[...] = jnp.zeros_like(l_s)
            acc_s[...] = jnp.zeros_like(acc_s)

        @pl.when(kj <= qi)
        def _():
            s = _d_nt(q_ref[0], k_ref[0]) * scale
            rows = qi * tq + lax.broadcasted_iota(jnp.int32, (tq, tk), 0)
            cols = kj * tk + lax.broadcasted_iota(jnp.int32, (tq, tk), 1)
            s = jnp.where(cols <= rows, s, -jnp.inf)
            m_new = jnp.maximum(m_s[...], jnp.max(s, axis=-1, keepdims=True))
            alpha = jnp.exp(m_s[...] - m_new)
            p = jnp.exp(s - m_new)
            l_s[...] = alpha * l_s[...] + jnp.sum(p, axis=-1, keepdims=True)
            acc_s[...] = alpha * acc_s[...] + _d_nn(p, v_ref[0])
            m_s[...] = m_new

        @pl.when(kj == nk - 1)
        def _():
            o_ref[...] = acc_s[...] / l_s[...]
            lse_ref[0] = m_s[...] + jnp.log(l_s[...])

    return pl.pallas_call(
        body,
        name="flash_fwd",
        grid=(h_n, nq, nk),
        in_specs=[
            pl.BlockSpec((1, tq, HEAD_PAD), lambda h, i, j: (h, i, 0)),
            pl.BlockSpec((1, tk, HEAD_PAD), lambda h, i, j: (h, jnp.minimum(i, j), 0)),
            pl.BlockSpec((1, tk, LANE), lambda h, i, j: (h, jnp.minimum(i, j), 0)),
        ],
        out_specs=[
            pl.BlockSpec((tq, LANE), lambda h, i, j: (i, h)),
            pl.BlockSpec((1, tq, 1), lambda h, i, j: (h, i, 0)),
        ],
        out_shape=[jax.ShapeDtypeStruct((t_len, h_n * LANE), F32), jax.ShapeDtypeStruct((h_n, t_len, 1), F32)],
        scratch_shapes=[pltpu.VMEM((tq, 1), F32), pltpu.VMEM((tq, 1), F32), pltpu.VMEM((tq, LANE), F32)],
        compiler_params=_cparams("parallel", "parallel", "arbitrary"),
    )(q, k, v)


def flash_bwd(q, k, v, do, lse, delta):
    h_n, t_len, _ = q.shape
    tq = tk = _flash_tile(t_len)
    nq, nk = t_len // tq, t_len // tk
    scale = QK ** -0.5

    def body(q_ref, k_ref, v_ref, do_ref, lse_ref, dl_ref, dq_ref, dk_ref, dv_ref):
        kj, qi = pl.program_id(1), pl.program_id(2)

        @pl.when(qi == 0)
        def _():
            dk_ref[...] = jnp.zeros_like(dk_ref)
            dv_ref[...] = jnp.zeros_like(dv_ref)

        @pl.when(qi >= kj)
        def _():
            qb, kb, dob = q_ref[0], k_ref[0], do_ref[...]
            s = _d_nt(qb, kb) * scale
            rows = qi * tq + lax.broadcasted_iota(jnp.int32, (tq, tk), 0)
            cols = kj * tk + lax.broadcasted_iota(jnp.int32, (tq, tk), 1)
            p = jnp.where(cols <= rows, jnp.exp(s - lse_ref[0]), 0.0)
            dv_ref[0] += _d_tn(p, dob)
            dp = _d_nt(dob, v_ref[0])
            ds = p * (dp - dl_ref[0]) * scale
            dk_ref[0] += _d_tn(ds, qb)
            dq_part = _d_nn(ds, kb)
            rows_q = pl.ds(pl.multiple_of(qi * tq, tq), tq)

            @pl.when(kj == 0)
            def _():
                dq_ref[0, rows_q, :] = dq_part

            @pl.when(kj != 0)
            def _():
                dq_ref[0, rows_q, :] += dq_part

    return pl.pallas_call(
        body,
        name="flash_bwd",
        grid=(h_n, nk, nq),
        in_specs=[
            pl.BlockSpec((1, tq, HEAD_PAD), lambda h, j, i: (h, jnp.maximum(i, j), 0)),
            pl.BlockSpec((1, tk, HEAD_PAD), lambda h, j, i: (h, j, 0)),
            pl.BlockSpec((1, tk, LANE), lambda h, j, i: (h, j, 0)),
            pl.BlockSpec((tq, LANE), lambda h, j, i: (jnp.maximum(i, j), h)),
            pl.BlockSpec((1, tq, 1), lambda h, j, i: (h, jnp.maximum(i, j), 0)),
            pl.BlockSpec((1, tq, 1), lambda h, j, i: (h, jnp.maximum(i, j), 0)),
        ],
        out_specs=[
            pl.BlockSpec((1, t_len, HEAD_PAD), lambda h, j, i: (h, 0, 0)),
            pl.BlockSpec((1, tk, HEAD_PAD), lambda h, j, i: (h, j, 0)),
            pl.BlockSpec((1, tk, LANE), lambda h, j, i: (h, j, 0)),
        ],
        out_shape=[
            jax.ShapeDtypeStruct((h_n, t_len, HEAD_PAD), F32),
            jax.ShapeDtypeStruct((h_n, t_len, HEAD_PAD), F32),
            jax.ShapeDtypeStruct((h_n, t_len, LANE), F32),
        ],
        compiler_params=_cparams("parallel", "arbitrary", "arbitrary"),
    )(q, k, v, do, lse, delta)


def gdn_step(s, q, k, v, gb, bb):
    h_n, c = q.shape[0], CHUNK
    ii = lax.broadcasted_iota(jnp.int32, (1, c, c), 1)
    jj = lax.broadcasted_iota(jnp.int32, (1, c, c), 2)
    incl, strict = ii >= jj, ii > jj
    gcb = _bhi(jnp.broadcast_to(incl.astype(F32), (h_n, c, c)), gb)
    lane = lax.broadcasted_iota(jnp.int32, (1, 1, LANE), 2)
    e0, e1 = (lane == 0).astype(F32), (lane == 1).astype(F32)
    diff = _bhi_nt(gcb * e0 + e1, e0 - gcb * e1)
    decay = jnp.where(incl, jnp.exp(jnp.where(incl, diff, 0.0)), 0.0)
    kb, vb = k * bb, v * bb
    egc = jnp.exp(gcb)
    lmat = jnp.where(strict, _bmm_nt(kb, k) * decay, 0.0)
    inv = (ii == jj).astype(F32) - lmat
    pw = _bh3(lmat, lmat)
    for step in range(5):
        inv = inv + _bh3(inv, pw)
        if step < 4:
            pw = _bh3(pw, pw)
    u = _bh3(inv, vb)
    w = _bh3(inv, kb * egc)
    attn = _bmm_nt(q, k) * decay
    qd = q * egc
    g_end = jnp.sum(gb, axis=1, keepdims=True)
    kd = k * jnp.exp(g_end - gcb)
    v_new = u - _bmm(w, s)
    o = _bmm(qd, s) + _bmm(attn, v_new)
    s_new = s * jnp.exp(g_end) + _bmm_tn(kd, v_new)
    return s_new, o


def gdn_fwd(q, k, v, gb, bb):
    h_n, t_len, d = q.shape
    n = t_len // CHUNK
    blk = pl.BlockSpec((h_n, CHUNK, d), lambda i: (0, i, 0))

    def body(q_ref, k_ref, v_ref, g_ref, b_ref, o_ref, sall_ref, s_s):
        @pl.when(pl.program_id(0) == 0)
        def _():
            s_s[...] = jnp.zeros_like(s_s)

        s = s_s[...]
        sall_ref[0] = s
        s_new, o = gdn_step(s, q_ref[...], k_ref[...], v_ref[...], g_ref[...], b_ref[...])
        o_ref[...] = o
        s_s[...] = s_new

    return pl.pallas_call(
        body,
        name="gdn_fwd",
        grid=(n,),
        in_specs=[blk] * 5,
        out_specs=[blk, pl.BlockSpec((1, h_n, d, d), lambda i: (i, 0, 0, 0))],
        out_shape=[jax.ShapeDtypeStruct((h_n, t_len, d), F32), jax.ShapeDtypeStruct((n, h_n, d, d), F32)],
        scratch_shapes=[pltpu.VMEM((h_n, d, d), F32)],
        compiler_params=_cparams("arbitrary"),
    )(q, k, v, gb, bb)


def gdn_bwd(q, k, v, gb, bb, s_all, do):
    h_n, t_len, d = q.shape
    n = t_len // CHUNK
    blk = pl.BlockSpec((h_n, CHUNK, d), lambda i: (0, n - 1 - i, 0))

    def body(q_ref, k_ref, v_ref, g_ref, b_ref, sall_ref, do_ref, dq_ref, dk_ref, dv_ref, dg_ref, db_ref, ds_s):
        @pl.when(pl.program_id(0) == 0)
        def _():
            ds_s[...] = jnp.zeros_like(ds_s)

        _, pull = jax.vjp(gdn_step, sall_ref[0], q_ref[...], k_ref[...], v_ref[...], g_ref[...], b_ref[...])
        ds, dq, dk, dv, dg, db = pull((ds_s[...], do_ref[...]))
        ds_s[...] = ds
        dq_ref[...], dk_ref[...], dv_ref[...], dg_ref[...], db_ref[...] = dq, dk, dv, dg, db

    return pl.pallas_call(
        body,
        name="gdn_bwd",
        grid=(n,),
        in_specs=[blk] * 5 + [pl.BlockSpec((1, h_n, d, d), lambda i: (n - 1 - i, 0, 0, 0)), blk],
        out_specs=[blk] * 5,
        out_shape=[jax.ShapeDtypeStruct((h_n, t_len, d), F32)] * 5,
        scratch_shapes=[pltpu.VMEM((h_n, d, d), F32)],
        compiler_params=_cparams("arbitrary"),
    )(q, k, v, gb, bb, s_all, do)


def _pad_cols(a, n):
    return jnp.pad(a, ((0, 0), (0, n - a.shape[1])))


def arrange_w_in(w):
    pieces, start = [], 0
    for n in R_SPLITS:
        pieces.append(w[:, start:start + n])
        start += n
    cq, ckv, kr, mgate, gq, gk, gv, ga, gb, ggate = pieces
    return jnp.concatenate([mgate, gq, gk, gv, ggate, cq, ckv, _pad_cols(kr, LANE),
                            _pad_cols(jnp.concatenate([ga, gb], axis=1), LANE)], axis=1)


def unarrange_w_in(g):
    def cols(start, n):
        return g[:, start:start + n]
    return jnp.concatenate([cols(P_CQ, Q_LORA), cols(P_CKV, KV_LORA), cols(P_KR, ROPE), cols(P_MGATE, WIDTH),
                            cols(P_GQ, WIDTH), cols(P_GK, WIDTH), cols(P_GV, WIDTH), cols(P_GAB, HEADS),
                            cols(P_GAB + HEADS, HEADS), cols(P_GGATE, WIDTH)], axis=1)


def arrange_w_uq(w):
    w = w.reshape(w.shape[0], HEADS, QK)
    return jnp.pad(w, ((0, 0), (0, 0), (0, HEAD_PAD - QK))).reshape(w.shape[0], HEADS * HEAD_PAD)


def unarrange_w_uq(g):
    return g.reshape(g.shape[0], HEADS, HEAD_PAD)[:, :, :QK].reshape(g.shape[0], HEADS * QK)


def local_step(x, pos, tgt, p):
    t_len = x.shape[0]
    w_in, w_uq, w_ukv, w_out = p["w_in"], p["w_uq"], p["w_ukv"], p["w_out"]
    norm_gain = p["norm_gain"].reshape(1, D_MODEL)
    qa_gain = p["mla_q_a_gain"].reshape(1, Q_LORA)
    kva_gain = p["mla_kv_a_gain"].reshape(1, KV_LORA)
    qg = _pad_cols(p["mla_q_norm_gain"].reshape(1, QK), HEAD_PAD)
    kg = _pad_cols(p["mla_k_norm_gain"].reshape(1, QK), HEAD_PAD)
    cw = p["gdn_conv_w"].reshape(CONV_W, 3 * WIDTH)
    cwq, cwk, cwv = cw[:, :WIDTH], cw[:, WIDTH:2 * WIDTH], cw[:, 2 * WIDTH:]
    alog = _pad_cols(p["gdn_a_log"].reshape(1, HEADS), LANE)
    dtb = _pad_cols(p["gdn_dt_bias"].reshape(1, HEADS), LANE)
    og = p["gdn_out_norm_gain"].reshape(1, GDN_DIM)
    half = ROPE // 2
    inv_freq = jnp.power(ROPE_THETA, -jnp.arange(half, dtype=F32) / half)
    invf = _pad_cols(jnp.concatenate([inv_freq, inv_freq]).reshape(1, ROPE), LANE)

    rt = 256
    r = "r"
    (xn,) = rowwise("rms_x", f_rms_x, t_len, rt, [(x, (r, D_MODEL, 0))], [norm_gain], [(r, D_MODEL, _BF)])
    proj = matmul("proj", xn, w_in, "nn")
    cq_in = (proj, (r, Q_LORA, P_CQ // Q_LORA))
    ckv_in = (proj, (r, KV_LORA, P_CKV // KV_LORA))
    kr_in = (proj, (r, LANE, P_KR // LANE))
    mgate_in = (proj, (r, WIDTH, P_MGATE // WIDTH))
    ggate_in = (proj, (r, WIDTH, P_GGATE // WIDTH))
    gqkv_in = [(proj, (r, WIDTH, P_GQ // WIDTH)), (proj, (r, WIDTH, P_GK // WIDTH)), (proj, (r, WIDTH, P_GV // WIDTH))]
    gab_in = (proj, (r, LANE, P_GAB // LANE))
    halos = [(proj, ("halo", WIDTH, P_GQ // WIDTH)), (proj, ("halo", WIDTH, P_GK // WIDTH)),
             (proj, ("halo", WIDTH, P_GV // WIDTH))]

    q_lat, kv_lat = rowwise("lat", f_lat, t_len, rt, [cq_in, ckv_in], [qa_gain, kva_gain],
                            [(r, Q_LORA, _BF), (r, KV_LORA, _BF)])
    q_raw = matmul("q_up", q_lat, w_uq, "nn")
    kv_raw = matmul("kv_up", kv_lat, w_ukv, "nn")
    wide = HEADS * HEAD_PAD
    head_in = [(q_raw, (r, wide, 0)), (kv_raw, (r, wide, 0)), kr_in]
    pos_in = (pos, (r, 1, 0))
    q_full, k_full, v_mla = rowwise(
        "head", lambda i, qr, kvr, kr, ps, qg_, kg_, iv: f_head(i, qr, kvr, kr, qg_, kg_, ps, iv), t_len, rt,
        head_in + [pos_in], [qg, kg, invf],
        [("h", HEADS, HEAD_PAD, _BF), ("h", HEADS, HEAD_PAD, _BF), ("h", HEADS, LANE, _BF)])
    o_mla, lse = flash_fwd(q_full, k_full, v_mla)

    pre_in = gqkv_in + [gab_in] + halos
    pre_full = [cwq, cwk, cwv, alog, dtb]
    hkind = ("h", HEADS, GDN_DIM, F32)
    gq_n, gk_n, gv_n, g_b, b_b = rowwise("gdn_pre", f_gdn_pre, t_len, rt, pre_in, pre_full, [hkind] * 5)
    o_gdn, s_all = gdn_fwd(gq_n, gk_n, gv_n, g_b, b_b)

    mix_in = [(o_mla, (r, WIDTH, 0)), mgate_in, (o_gdn, ("h",)), ggate_in]
    (mixed,) = rowwise("mix", f_mix, t_len, rt, mix_in, [og], [(r, 2 * WIDTH, _BF)])
    h_out = matmul("out_proj", mixed, w_out, "nn")
    dy, loss_acc = rowwise("loss", f_loss, t_len, rt,
                           [(x, (r, D_MODEL, 0)), (h_out, (r, D_MODEL, 0)), (tgt, (r, D_MODEL, 0))], [],
                           [(r, D_MODEL, F32)], [(SUBLANE, LANE)])
    loss = loss_acc[0, 0]

    d_mixed = matmul("d_mixed", dy, w_out, "nt")
    g_w_out = matmul("g_w_out", mixed, dy, "tn")

    def mix_bwd(i, o_mla_, mgate_, o_gdn_, ggate_, d_mixed_, og_):
        return _vjp_fn(f_mix, 5, 1)(i, o_mla_, mgate_, o_gdn_, ggate_, og_, d_mixed_)

    do_mla, d_mgate, do_gdn, d_ggate, g_og = rowwise(
        "mix_bwd", mix_bwd, t_len, rt, mix_in + [(d_mixed, (r, 2 * WIDTH, 0))], [og],
        [(r, WIDTH, F32), (r, WIDTH, F32), hkind, (r, WIDTH, F32)], [(1, GDN_DIM)])
    dq_n, dk_n, dv_n, dg_b, db_b = gdn_bwd(gq_n, gk_n, gv_n, g_b, b_b, s_all, do_gdn)
    cts_in = [(a, ("h",)) for a in (dq_n, dk_n, dv_n, dg_b, db_b)]
    d_gq, d_gk, d_gv, d_gab, g_cwq, g_cwk, g_cwv, g_alog, g_dtb = rowwise(
        "gdn_pre_bwd", gdn_pre_bwd, t_len, rt, pre_in + cts_in, pre_full,
        [(r, WIDTH, F32)] * 3 + [(r, LANE, F32)],
        [(CONV_W, WIDTH)] * 3 + [(1, LANE)] * 2, carries=[(SUBLANE, WIDTH)] * 3, reverse=True)

    (delta,) = rowwise("delta", f_delta, t_len, rt, [(o_mla, (r, WIDTH, 0)), (do_mla, (r, WIDTH, 0))], [],
                       [("h", HEADS, 1, F32)])
    dq_full, dk_full, dv_mla = flash_bwd(q_full, k_full, v_mla, do_mla, lse, delta)
    head_cts = [(a, ("h",)) for a in (dq_full, dk_full, dv_mla)]

    def head_bwd(i, q_raw_, kv_raw_, kr_, pos_, dq_, dk_, dv_, qg_, kg_, invf_):
        return _vjp_fn(f_head, 5, 3)(i, q_raw_, kv_raw_, kr_, qg_, kg_, pos_, invf_, dq_, dk_, dv_)

    dq_raw, dkv_raw, d_kr, g_qg, g_kg = rowwise(
        "head_bwd", head_bwd, t_len, rt // 2, head_in + [pos_in] + head_cts, [qg, kg, invf],
        [(r, wide, F32), (r, wide, F32), (r, LANE, F32)], [(1, HEAD_PAD), (1, HEAD_PAD)])
    dq_lat = matmul("dq_lat", dq_raw, w_uq, "nt")
    g_w_uq = matmul("g_w_uq", q_lat, dq_raw, "tn")
    dkv_lat = matmul("dkv_lat", dkv_raw, w_ukv, "nt")
    g_w_ukv = matmul("g_w_ukv", kv_lat, dkv_raw, "tn")

    def lat_bwd(i, cq_, ckv_, dql_, dkl_, gq_, gkv_):
        return _vjp_fn(f_lat, 4, 2)(i, cq_, ckv_, gq_, gkv_, dql_, dkl_)

    d_cq, d_ckv, g_qa, g_kva = rowwise(
        "lat_bwd", lat_bwd, t_len, rt, [cq_in, ckv_in, (dq_lat, (r, Q_LORA, 0)), (dkv_lat, (r, KV_LORA, 0))],
        [qa_gain, kva_gain], [(r, Q_LORA, F32), (r, KV_LORA, F32)], [(1, Q_LORA), (1, KV_LORA)])

    d_proj = jnp.concatenate([d_mgate, d_gq, d_gk, d_gv, d_ggate, d_cq, d_ckv, d_kr, d_gab], axis=1)
    d_xn = matmul("d_xn", d_proj, w_in, "nt")
    g_w_in = matmul("g_w_in", xn, d_proj, "tn")

    def rms_x_bwd(i, x_, dxn_, dy_, gain_):
        dx, dgain = _vjp_fn(f_rms_x, 2, 1)(i, x_, gain_, dxn_)
        return dx + dy_, dgain

    grad_x, g_norm = rowwise("rms_x_bwd", rms_x_bwd, t_len, rt,
                             [(x, (r, D_MODEL, 0)), (d_xn, (r, D_MODEL, 0)), (dy, (r, D_MODEL, 0))], [norm_gain],
                             [(r, D_MODEL, F32)], [(1, D_MODEL)])

    grads = {
        "norm_gain": g_norm, "w_in": g_w_in, "mla_q_a_gain": g_qa, "mla_kv_a_gain": g_kva, "w_uq": g_w_uq,
        "w_ukv": g_w_ukv, "mla_q_norm_gain": g_qg[:, :QK], "mla_k_norm_gain": g_kg[:, :QK],
        "gdn_conv_w": jnp.concatenate([g_cwq, g_cwk, g_cwv], axis=1), "gdn_a_log": g_alog[:, :HEADS],
        "gdn_dt_bias": g_dtb[:, :HEADS], "gdn_out_norm_gain": g_og, "w_out": g_w_out,
    }
    return loss, grad_x, grads


MESH = pl.DeviceIdType.MESH
ANY = pl.BlockSpec(memory_space=pl.ANY)
CHIP_FLIPS = ((1, 0), (0, 1), (1, 1))


def _place():
    return lax.axis_index("x"), lax.axis_index("y"), lax.axis_index("c")


def _flip(v, f):
    return 1 - v if f else v


def all_gather(shards):
    n_arr = len(shards)

    def body(*refs):
        x_refs, o_refs = refs[:n_arr], refs[n_arr:2 * n_arr]
        send_sems, recv_sems, local_sems = refs[2 * n_arr:]
        x, y, c = _place()
        me, sibling = (x, y, c), (x, y, 1 - c)
        chips = [(_flip(x, fx), _flip(y, fy)) for fx, fy in CHIP_FLIPS]

        def copy(a, k, block, to, src=None):
            px, py, pc = block
            dst = o_refs[a].at[4 * px + 2 * py + pc]
            return pltpu.make_async_remote_copy(
                src_ref=dst if src is None else src, dst_ref=dst, send_sem=send_sems.at[a, k],
                recv_sem=recv_sems.at[a, k], device_id=to, device_id_type=MESH)

        mine, first, passed = [], [], []
        for a in range(n_arr):
            cp = pltpu.make_async_copy(x_refs[a], o_refs[a].at[4 * x + 2 * y + c], local_sems.at[a])
            cp.start()
            mine.append(cp)
            first.append(copy(a, 0, me, sibling, src=x_refs[a]))
            first += [copy(a, 1 + j, me, (*chip, c), src=x_refs[a]) for j, chip in enumerate(chips)]
        for cp in first:
            cp.start()
        for j, chip in enumerate(chips):
            for a in range(n_arr):
                copy(a, 1 + j, (*chip, c), me).wait_recv()
                cp = copy(a, 4 + j, (*chip, c), sibling)
                cp.start()
                passed.append(cp)
        for a in range(n_arr):
            copy(a, 0, sibling, me).wait_recv()
            for j, chip in enumerate(chips):
                copy(a, 4 + j, (*chip, 1 - c), me).wait_recv()
        for cp in first + passed:
            cp.wait_send()
        for cp in mine:
            cp.wait()

    return pl.pallas_call(
        body,
        name="all_gather",
        out_shape=[jax.ShapeDtypeStruct((N_DEV,) + s.shape, s.dtype) for s in shards],
        in_specs=[ANY] * n_arr,
        out_specs=[ANY] * n_arr,
        scratch_shapes=[pltpu.SemaphoreType.DMA((n_arr, 7)), pltpu.SemaphoreType.DMA((n_arr, 7)),
                        pltpu.SemaphoreType.DMA((n_arr,))],
    )(*shards)


def exchange_cores(grads):
    n_arr = len(grads)

    def body(*refs):
        g_refs, o_refs = refs[:n_arr], refs[n_arr:2 * n_arr]
        send_sems, recv_sems = refs[2 * n_arr:]
        x, y, c = _place()
        copies = []
        for a in range(n_arr):
            for q in range(4):
                cp = pltpu.make_async_remote_copy(
                    src_ref=g_refs[a].at[2 * q + (1 - c)], dst_ref=o_refs[a].at[q], send_sem=send_sems.at[a, q],
                    recv_sem=recv_sems.at[a, q], device_id=(x, y, 1 - c), device_id_type=MESH)
                cp.start()
                copies.append(cp)
        for cp in copies:
            cp.wait()

    return pl.pallas_call(
        body,
        name="exchange_cores",
        out_shape=[jax.ShapeDtypeStruct((4,) + g.shape[1:], g.dtype) for g in grads],
        in_specs=[ANY] * n_arr,
        out_specs=[ANY] * n_arr,
        scratch_shapes=[pltpu.SemaphoreType.DMA((n_arr, 4)), pltpu.SemaphoreType.DMA((n_arr, 4))],
    )(*grads)


def exchange_chips(parts):
    n_arr = len(parts)

    def body(*refs):
        p_refs, o_refs = refs[:n_arr], refs[n_arr:2 * n_arr]
        send_sems, recv_sems = refs[2 * n_arr:]
        x, y, c = _place()
        copies = []
        for a in range(n_arr):
            for j, (fx, fy) in enumerate(CHIP_FLIPS):
                px, py = _flip(x, fx), _flip(y, fy)
                cp = pltpu.make_async_remote_copy(
                    src_ref=p_refs[a].at[2 * px + py], dst_ref=o_refs[a].at[j], send_sem=send_sems.at[a, j],
                    recv_sem=recv_sems.at[a, j], device_id=(px, py, c), device_id_type=MESH)
                cp.start()
                copies.append(cp)
        for cp in copies:
            cp.wait()

    return pl.pallas_call(
        body,
        name="exchange_chips",
        out_shape=[jax.ShapeDtypeStruct((3,) + p.shape[1:], p.dtype) for p in parts],
        in_specs=[ANY] * n_arr,
        out_specs=[ANY] * n_arr,
        scratch_shapes=[pltpu.SemaphoreType.DMA((n_arr, 3)), pltpu.SemaphoreType.DMA((n_arr, 3))],
    )(*parts)


def gather_small(v):
    def body(v_ref, o_ref, send_sems, recv_sems, local_sem):
        x, y, c = _place()
        me = 4 * x + 2 * y + c
        mine = pltpu.make_async_copy(v_ref, o_ref.at[me], local_sem)
        mine.start()
        copies = []
        for k in range(1, N_DEV):
            fx, fy, fc = (k >> 2) & 1, (k >> 1) & 1, k & 1
            cp = pltpu.make_async_remote_copy(
                src_ref=v_ref, dst_ref=o_ref.at[me], send_sem=send_sems.at[k - 1], recv_sem=recv_sems.at[k - 1],
                device_id=(_flip(x, fx), _flip(y, fy), _flip(c, fc)), device_id_type=MESH)
            cp.start()
            copies.append(cp)
        for cp in copies:
            cp.wait()
        mine.wait()

    return pl.pallas_call(
        body,
        name="gather_small",
        out_shape=jax.ShapeDtypeStruct((N_DEV,) + v.shape, v.dtype),
        in_specs=[ANY],
        out_specs=ANY,
        scratch_shapes=[pltpu.SemaphoreType.DMA((N_DEV - 1,)), pltpu.SemaphoreType.DMA((N_DEV - 1,)),
                        pltpu.SemaphoreType.DMA],
    )(v)


def _row_tile(rows):
    for t in (256, 128, 64, 32, 16, 8):
        if rows % t == 0:
            return t
    return rows


def add_core_parts(name, g, recv, c_idx):
    _, rows, cols = g.shape
    tr = _row_tile(rows)

    def body(c_ref, g_ref, r_ref, o_ref):
        o_ref[...] = g_ref[...] + r_ref[...]

    return pl.pallas_call(
        body,
        name=name,
        grid_spec=pltpu.PrefetchScalarGridSpec(
            num_scalar_prefetch=1,
            grid=(4, rows // tr),
            in_specs=[pl.BlockSpec((1, tr, cols), lambda q, i, c_ref: (2 * q + c_ref[0], i, 0)),
                      pl.BlockSpec((1, tr, cols), lambda q, i, c_ref: (q, i, 0))],
            out_specs=pl.BlockSpec((1, tr, cols), lambda q, i, c_ref: (q, i, 0)),
        ),
        out_shape=jax.ShapeDtypeStruct((4, rows, cols), F32),
        compiler_params=_cparams("parallel", "parallel"),
    )(c_idx, g, recv)


def _adamw(w, g, m, v):
    m = ADAM_B1 * m + (1.0 - ADAM_B1) * g
    v = ADAM_B2 * v + (1.0 - ADAM_B2) * (g * g)
    m_hat = m / (1.0 - ADAM_B1 ** ADAM_STEP)
    v_hat = v / (1.0 - ADAM_B2 ** ADAM_STEP)
    delta = -ADAM_LR * (m_hat / (jnp.sqrt(v_hat) + ADAM_EPS) + ADAM_WD * w)
    return delta, m, v


def adamw_sharded(name, parts, recv, q_idx, w, m, v):
    rows, cols = w.shape
    tr = _row_tile(rows)

    def body(q_ref, p_ref, r_ref, w_ref, m_ref, v_ref, g_out, d_out, m_out, v_out):
        g = p_ref[0] + r_ref[0] + r_ref[1] + r_ref[2]
        d, m_new, v_new = _adamw(w_ref[...], g, m_ref[...], v_ref[...])
        g_out[...], d_out[...], m_out[...], v_out[...] = g, d, m_new, v_new

    blk = pl.BlockSpec((tr, cols), lambda i, q_ref: (i, 0))
    return pl.pallas_call(
        body,
        name=name,
        grid_spec=pltpu.PrefetchScalarGridSpec(
            num_scalar_prefetch=1,
            grid=(rows // tr,),
            in_specs=[pl.BlockSpec((1, tr, cols), lambda i, q_ref: (q_ref[0], i, 0)),
                      pl.BlockSpec((3, tr, cols), lambda i, q_ref: (0, i, 0)), blk, blk, blk],
            out_specs=[blk] * 4,
        ),
        out_shape=[jax.ShapeDtypeStruct((rows, cols), F32)] * 4,
        compiler_params=_cparams("parallel"),
    )(q_idx, parts, recv, w, m, v)


def adamw_small(gathered, w, m, v):
    def body(g_ref, w_ref, m_ref, v_ref, g_out, d_out, m_out, v_out):
        g = g_ref[0]
        for j in range(1, N_DEV):
            g = g + g_ref[j]
        d, m_new, v_new = _adamw(w_ref[...], g, m_ref[...], v_ref[...])
        g_out[...], d_out[...], m_out[...], v_out[...] = g, d, m_new, v_new

    return pl.pallas_call(body, name="adamw_small", out_shape=[jax.ShapeDtypeStruct(w.shape, F32)] * 4)(gathered, w, m, v)


SHARDED = ("w_in", "w_uq", "w_ukv", "gdn_conv_w", "w_out")
SMALL = (("norm_gain", D_MODEL), ("mla_q_a_gain", Q_LORA), ("mla_kv_a_gain", KV_LORA), ("mla_q_norm_gain", QK),
         ("mla_k_norm_gain", QK), ("gdn_a_log", HEADS), ("gdn_dt_bias", HEADS), ("gdn_out_norm_gain", GDN_DIM))
WEIGHT_ORDER = ("norm_gain", "w_in", "mla_q_a_gain", "mla_kv_a_gain", "w_uq", "w_ukv", "mla_q_norm_gain",
                "mla_k_norm_gain", "gdn_conv_w", "gdn_a_log", "gdn_dt_bias", "gdn_out_norm_gain", "w_out")


def _pack_small(d):
    rows = []
    for name, n in SMALL:
        a = d[name].reshape(-1).astype(F32)
        n_pad = -(-n // LANE) * LANE
        rows.append(jnp.pad(a, (0, n_pad - n)).reshape(n_pad // LANE, LANE))
    packed = jnp.concatenate(rows, axis=0)
    return jnp.pad(packed, ((0, -packed.shape[0] % SUBLANE), (0, 0)))


def _unpack_small(packed):
    out, row = {}, 0
    for name, n in SMALL:
        n_rows = -(-n // LANE)
        out[name] = packed[row:row + n_rows].reshape(-1)[:n].reshape(1, n)
        row += n_rows
    return out


def kernel(x, positions, norm_gain, w_in, mla_q_a_gain, mla_kv_a_gain, w_uq, w_ukv, mla_q_norm_gain, mla_k_norm_gain, gdn_conv_w, gdn_a_log, gdn_dt_bias, gdn_out_norm_gain, w_out, loss_target, m_norm_gain, m_w_in, m_mla_q_a_gain, m_mla_kv_a_gain, m_w_uq, m_w_ukv, m_mla_q_norm_gain, m_mla_k_norm_gain, m_gdn_conv_w, m_gdn_a_log, m_gdn_dt_bias, m_gdn_out_norm_gain, m_w_out, v_norm_gain, v_w_in, v_mla_q_a_gain, v_mla_kv_a_gain, v_w_uq, v_w_ukv, v_mla_q_norm_gain, v_mla_k_norm_gain, v_gdn_conv_w, v_gdn_a_log, v_gdn_dt_bias, v_gdn_out_norm_gain, v_w_out):
    w = dict(norm_gain=norm_gain, w_in=w_in, mla_q_a_gain=mla_q_a_gain, mla_kv_a_gain=mla_kv_a_gain, w_uq=w_uq,
             w_ukv=w_ukv, mla_q_norm_gain=mla_q_norm_gain, mla_k_norm_gain=mla_k_norm_gain, gdn_conv_w=gdn_conv_w,
             gdn_a_log=gdn_a_log, gdn_dt_bias=gdn_dt_bias, gdn_out_norm_gain=gdn_out_norm_gain, w_out=w_out)
    m = dict(norm_gain=m_norm_gain, w_in=m_w_in, mla_q_a_gain=m_mla_q_a_gain, mla_kv_a_gain=m_mla_kv_a_gain,
             w_uq=m_w_uq, w_ukv=m_w_ukv, mla_q_norm_gain=m_mla_q_norm_gain, mla_k_norm_gain=m_mla_k_norm_gain,
             gdn_conv_w=m_gdn_conv_w, gdn_a_log=m_gdn_a_log, gdn_dt_bias=m_gdn_dt_bias,
             gdn_out_norm_gain=m_gdn_out_norm_gain, w_out=m_w_out)
    v = dict(norm_gain=v_norm_gain, w_in=v_w_in, mla_q_a_gain=v_mla_q_a_gain, mla_kv_a_gain=v_mla_kv_a_gain,
             w_uq=v_w_uq, w_ukv=v_w_ukv, mla_q_norm_gain=v_mla_q_norm_gain, mla_k_norm_gain=v_mla_k_norm_gain,
             gdn_conv_w=v_gdn_conv_w, gdn_a_log=v_gdn_a_log, gdn_dt_bias=v_gdn_dt_bias,
             gdn_out_norm_gain=v_gdn_out_norm_gain, w_out=v_w_out)
    t_len = x.shape[1]

    shards = [w[n][0] if n == "gdn_conv_w" else w[n][0].astype(_BF) for n in SHARDED]
    a_w_in, a_w_uq, a_w_ukv, a_cw, a_w_out = all_gather(shards)

    def cols_whole(g):
        return g.transpose(1, 0, 2).reshape(g.shape[1], N_DEV * g.shape[2])

    p = {n: w[n] for n, _ in SMALL}
    p["w_in"] = arrange_w_in(cols_whole(a_w_in))
    p["w_uq"] = arrange_w_uq(cols_whole(a_w_uq))
    p["w_ukv"] = cols_whole(a_w_ukv)
    p["gdn_conv_w"] = cols_whole(a_cw)
    p["w_out"] = a_w_out.reshape(N_DEV * a_w_out.shape[1], a_w_out.shape[2])

    pos = positions.reshape(t_len, 1).astype(F32)
    loss, grad_x, grads = local_step(x.reshape(t_len, D_MODEL), pos, loss_target.reshape(t_len, D_MODEL), p)
    loss = lax.psum(loss, ("x", "y", "c"))

    def col_blocks(g):
        return g.reshape(g.shape[0], N_DEV, g.shape[1] // N_DEV).transpose(1, 0, 2)

    blocks = [col_blocks(unarrange_w_in(grads["w_in"])), col_blocks(unarrange_w_uq(grads["w_uq"])),
              col_blocks(grads["w_ukv"]), col_blocks(grads["gdn_conv_w"]),
              grads["w_out"].reshape(N_DEV, D_MODEL // N_DEV, D_MODEL)]
    xi, yi, ci = _place()
    c_idx = jnp.reshape(ci, (1,)).astype(jnp.int32)
    q_idx = jnp.reshape(2 * xi + yi, (1,)).astype(jnp.int32)
    from_sibling = exchange_cores(blocks)
    parts = [add_core_parts("add_" + n, g, r, c_idx) for n, g, r in zip(SHARDED, blocks, from_sibling)]
    from_chips = exchange_chips(parts)
    out = {}
    for n, prt, rcv in zip(SHARDED, parts, from_chips):
        shape = w[n].shape
        res = adamw_sharded("adamw_" + n, prt, rcv, q_idx, w[n].reshape(shape[-2:]), m[n].reshape(shape[-2:]),
                            v[n].reshape(shape[-2:]))
        out[n] = [a.reshape(shape) for a in res]

    small_all = gather_small(_pack_small(grads))
    res = adamw_small(small_all, _pack_small(w), _pack_small(m), _pack_small(v))
    unpacked = [_unpack_small(a) for a in res]
    for n, _ in SMALL:
        out[n] = [u[n] for u in unpacked]

    return (loss, grad_x.reshape(x.shape), *[out[n][0] for n in WEIGHT_ORDER], *[out[n][1] for n in WEIGHT_ORDER],
            *[out[n][2] for n in WEIGHT_ORDER], *[out[n][3] for n in WEIGHT_ORDER])
```

```python
import functools

import jax
import jax.numpy as jnp
from jax import lax
from jax.experimental import pallas as pl
from jax.experimental.pallas import tpu as pltpu

F32 = jnp.float32
_BF = jnp.bfloat16
HI = lax.Precision.HIGHEST

D_MODEL = 2048
HEADS = 8
NOPE = 128
ROPE = 64
QK = NOPE + ROPE
Q_LORA = 512
KV_LORA = 256
HEAD_PAD = 256
GDN_DIM = 128
WIDTH = HEADS * 128
CONV_W = 4
CHUNK = 64
ROPE_THETA = 10000.0
EPS = 1e-6
N_DEV = 8
LANE = 128
SUBLANE = 8
VMEM_LIMIT = 48 * 1024 * 1024

ADAM_LR, ADAM_B1, ADAM_B2, ADAM_EPS, ADAM_WD, ADAM_STEP = 0.001, 0.9, 0.999, 1e-08, 0.01, 10

P_MGATE, P_GQ, P_GK, P_GV, P_GGATE = 0, 1024, 2048, 3072, 4096
P_CQ, P_CKV, P_KR, P_GAB = 5120, 5632, 5888, 6016
P_COLS = 6144
R_SPLITS = (512, 256, 64, 1024, 1024, 1024, 1024, 8, 8, 1024)


def _cparams(*sem):
    return pltpu.CompilerParams(dimension_semantics=sem, vmem_limit_bytes=VMEM_LIMIT)


def _d_nn(a, b):
    return jnp.dot(a.astype(_BF), b.astype(_BF), preferred_element_type=F32)


def _d_nt(a, b):
    return lax.dot_general(a.astype(_BF), b.astype(_BF), (((1,), (1,)), ((), ())), preferred_element_type=F32)


def _d_tn(a, b):
    return lax.dot_general(a.astype(_BF), b.astype(_BF), (((0,), (0,)), ((), ())), preferred_element_type=F32)


@jax.custom_vjp
def _mm(a, b):
    return _d_nn(a, b)


_mm.defvjp(lambda a, b: (_d_nn(a, b), (a, b)), lambda r, g: (_d_nt(g, r[1]), _d_tn(r[0], g)))


@jax.custom_vjp
def _mm_nt(a, b):
    return _d_nt(a, b)


_mm_nt.defvjp(lambda a, b: (_d_nt(a, b), (a, b)), lambda r, g: (_d_nn(g, r[1]), _d_tn(g, r[0])))


@jax.custom_vjp
def _mm_tn(a, b):
    return _d_tn(a, b)


_mm_tn.defvjp(lambda a, b: (_d_tn(a, b), (a, b)), lambda r, g: (_d_nt(r[1], g), _d_nn(r[0], g)))


def _hi(a, b):
    return jnp.dot(a, b, preferred_element_type=F32, precision=HI)


_NN3 = (((2,), (1,)), ((0,), (0,)))
_NT3 = (((2,), (2,)), ((0,), (0,)))
_TN3 = (((1,), (1,)), ((0,), (0,)))


def _bdot(a, b, dims, hi):
    if hi:
        return lax.dot_general(a, b, dims, preferred_element_type=F32, precision=hi)
    return lax.dot_general(a.astype(_BF), b.astype(_BF), dims, preferred_element_type=F32)


def _batched_matmuls(hi):
    nn = jax.custom_vjp(lambda a, b: _bdot(a, b, _NN3, hi))
    nt = jax.custom_vjp(lambda a, b: _bdot(a, b, _NT3, hi))
    tn = jax.custom_vjp(lambda a, b: _bdot(a, b, _TN3, hi))
    nn.defvjp(lambda a, b: (_bdot(a, b, _NN3, hi), (a, b)),
              lambda r, g: (_bdot(g, r[1], _NT3, hi), _bdot(r[0], g, _TN3, hi)))
    nt.defvjp(lambda a, b: (_bdot(a, b, _NT3, hi), (a, b)),
              lambda r, g: (_bdot(g, r[1], _NN3, hi), _bdot(g, r[0], _TN3, hi)))
    tn.defvjp(lambda a, b: (_bdot(a, b, _TN3, hi), (a, b)),
              lambda r, g: (_bdot(r[1], g, _NT3, hi), _bdot(r[0], g, _NN3, hi)))
    return nn, nt, tn


_bmm, _bmm_nt, _bmm_tn = _batched_matmuls(False)
_bhi, _bhi_nt, _bhi_tn = _batched_matmuls(HI)
_bh3, _bh3_nt, _bh3_tn = _batched_matmuls(lax.Precision.HIGH)


@functools.partial(jax.custom_vjp, nondiff_argnums=(1, 2))
def _roll(x, shift, axis):
    return pltpu.roll(x, shift, axis)


def _roll_fwd(x, shift, axis):
    return pltpu.roll(x, shift, axis), None


def _roll_bwd(shift, axis, _, g):
    n = g.shape[axis]
    return (pltpu.roll(g, (n - shift) % n, axis),)


_roll.defvjp(_roll_fwd, _roll_bwd)


def _rms(x, gain):
    return x * lax.rsqrt(jnp.mean(x * x, axis=-1, keepdims=True) + EPS) * gain


MM_TILE = 1024


def matmul(name, a, b, mode):
    if mode == "nn":
        (m, k), (k2, n) = a.shape, b.shape
    elif mode == "nt":
        (m, k), (n, k2) = a.shape, b.shape
    else:
        (k, m), (k2, n) = a.shape, b.shape
    assert k == k2, (name, a.shape, b.shape)
    tm, tn, tk = min(MM_TILE, m), min(MM_TILE, n), min(MM_TILE, k)
    assert m % tm == 0 and n % tn == 0 and k % tk == 0, (name, m, n, k)
    dot = {"nn": _d_nn, "nt": _d_nt, "tn": _d_tn}[mode]

    def body(a_ref, b_ref, o_ref):
        kk = pl.program_id(2)
        part = dot(a_ref[...], b_ref[...])

        @pl.when(kk == 0)
        def _():
            o_ref[...] = part

        @pl.when(kk != 0)
        def _():
            o_ref[...] += part

    if mode == "nn":
        a_spec = pl.BlockSpec((tm, tk), lambda j, i, kk: (i, kk))
        b_spec = pl.BlockSpec((tk, tn), lambda j, i, kk: (kk, j))
    elif mode == "nt":
        a_spec = pl.BlockSpec((tm, tk), lambda j, i, kk: (i, kk))
        b_spec = pl.BlockSpec((tn, tk), lambda j, i, kk: (j, kk))
    else:
        a_spec = pl.BlockSpec((tk, tm), lambda j, i, kk: (kk, i))
        b_spec = pl.BlockSpec((tk, tn), lambda j, i, kk: (kk, j))
    return pl.pallas_call(
        body,
        name=name,
        grid=(n // tn, m // tm, k // tk),
        in_specs=[a_spec, b_spec],
        out_specs=pl.BlockSpec((tm, tn), lambda j, i, kk: (i, j)),
        out_shape=jax.ShapeDtypeStruct((m, n), F32),
        compiler_params=_cparams("parallel", "parallel", "arbitrary"),
    )(a, b)


def rowwise(name, fn, t_len, tile, row_in, full_in, row_out, acc_out=(), carries=(), reverse=False):
    tile = min(tile, t_len)
    n = t_len // tile
    assert t_len % tile == 0 and tile % SUBLANE == 0
    n_in, n_ro, n_acc, n_car = len(row_in) + len(full_in), len(row_out), len(acc_out), len(carries)

    def ti(i):
        return (n - 1 - i) if reverse else i

    in_specs, args = [], []
    for arr, kind in row_in:
        if kind[0] == "r":
            in_specs.append(pl.BlockSpec((tile, kind[1]), lambda i, c=kind[2]: (ti(i), c)))
        elif kind[0] == "h":
            in_specs.append(pl.BlockSpec((arr.shape[0], tile, arr.shape[2]), lambda i: (0, ti(i), 0)))
        else:
            in_specs.append(pl.BlockSpec(
                (SUBLANE, kind[1]), lambda i, c=kind[2]: (jnp.maximum(ti(i) * (tile // SUBLANE) - 1, 0), c)))
        args.append(arr)
    for arr in full_in:
        in_specs.append(pl.BlockSpec(arr.shape, lambda i, nd=arr.ndim: (0,) * nd))
        args.append(arr)
    out_specs, out_shape = [], []
    for kind in row_out:
        if kind[0] == "r":
            out_specs.append(pl.BlockSpec((tile, kind[1]), lambda i: (ti(i), 0)))
            out_shape.append(jax.ShapeDtypeStruct((t_len, kind[1]), kind[2]))
        else:
            out_specs.append(pl.BlockSpec((kind[1], tile, kind[2]), lambda i: (0, ti(i), 0)))
            out_shape.append(jax.ShapeDtypeStruct((kind[1], t_len, kind[2]), kind[3]))
    for shp in acc_out:
        out_specs.append(pl.BlockSpec(shp, lambda i, nd=len(shp): (0,) * nd))
        out_shape.append(jax.ShapeDtypeStruct(shp, F32))

    def body(*refs):
        in_refs = refs[:n_in]
        ro_refs = refs[n_in:n_in + n_ro]
        acc_refs = refs[n_in + n_ro:n_in + n_ro + n_acc]
        car_refs = refs[n_in + n_ro + n_acc:]
        step = pl.program_id(0)
        if n_car:
            @pl.when(step == 0)
            def _():
                for r in car_refs:
                    r[...] = jnp.zeros_like(r)
        vals = [r[...].astype(F32) for r in in_refs] + [r[...] for r in car_refs]
        outs = fn(ti(step), *vals)
        assert len(outs) == n_ro + n_acc + n_car, (name, len(outs))
        for r, o in zip(ro_refs, outs[:n_ro]):
            r[...] = o.astype(r.dtype)
        for r, o in zip(acc_refs, outs[n_ro:n_ro + n_acc]):
            @pl.when(step == 0)
            def _(r=r, o=o):
                r[...] = o

            @pl.when(step != 0)
            def _(r=r, o=o):
                r[...] += o
        for r, o in zip(car_refs, outs[n_ro + n_acc:]):
            r[...] = o

    res = pl.pallas_call(
        body,
        name=name,
        grid=(n,),
        in_specs=in_specs,
        out_specs=out_specs,
        out_shape=out_shape,
        scratch_shapes=[pltpu.VMEM(s, F32) for s in carries],
        compiler_params=_cparams("arbitrary"),
    )(*args)
    return list(res)


def _vjp_fn(fn, n_diff, n_out):
    def g(i, *a):
        ins, cts = a[:len(a) - n_out], a[len(a) - n_out:]
        diff, rest = ins[:n_diff], ins[n_diff:]
        _, pull = jax.vjp(lambda *d: tuple(fn(i, *d, *rest)), *diff)
        return tuple(pull(tuple(cts)))

    return g


def f_rms_x(i, x, gain):
    return (_rms(x, gain),)


def f_lat(i, cq, ckv, gq, gkv):
    return _rms(cq, gq), _rms(ckv, gkv)


def _rope_tables(pos, invf):
    ang = pos * invf
    lane = lax.broadcasted_iota(jnp.int32, (1, LANE), 1)
    cosv, sinv = jnp.cos(ang), jnp.sin(ang)
    half = ROPE // 2
    c = jnp.where(lane < ROPE, cosv, 0.0)
    sa = jnp.where(lane < half, -sinv, 0.0)
    sb = jnp.where((lane >= half) & (lane < ROPE), sinv, 0.0)
    return c, sa, sb


def _rope(xh, tabs):
    c, sa, sb = tabs
    half = ROPE // 2
    return xh * c + _roll(xh, LANE - half, 1) * sa + _roll(xh, half, 1) * sb


def f_head(i, q_raw, kv_raw, kr, qg, kg, pos, invf):
    tabs = _rope_tables(pos, invf)
    qs, ks, vs = [], [], []
    kr_ss = jnp.sum(kr * kr, axis=-1, keepdims=True)
    for h in range(HEADS):
        lo = q_raw[:, HEAD_PAD * h:HEAD_PAD * h + NOPE]
        hi = q_raw[:, HEAD_PAD * h + NOPE:HEAD_PAD * (h + 1)]
        ss = jnp.sum(lo * lo, axis=-1, keepdims=True) + jnp.sum(hi * hi, axis=-1, keepdims=True)
        r = lax.rsqrt(ss * (1.0 / QK) + EPS)
        qs.append(jnp.concatenate([lo * r * qg[:, :NOPE], _rope(hi * r * qg[:, NOPE:], tabs)], axis=1))
        lo = kv_raw[:, 2 * NOPE * h:2 * NOPE * h + NOPE]
        ss = jnp.sum(lo * lo, axis=-1, keepdims=True) + kr_ss
        r = lax.rsqrt(ss * (1.0 / QK) + EPS)
        ks.append(jnp.concatenate([lo * r * kg[:, :NOPE], _rope(kr * r * kg[:, NOPE:], tabs)], axis=1))
        vs.append(kv_raw[:, 2 * NOPE * h + NOPE:2 * NOPE * (h + 1)])
    return jnp.stack(qs), jnp.stack(ks), jnp.stack(vs)


def f_mix(i, o_mla, mgate, o_gdn, ggate, og):
    parts = [o_mla * jax.nn.silu(mgate)]
    for h in range(HEADS):
        parts.append(_rms(o_gdn[h], og) * jax.nn.silu(ggate[:, LANE * h:LANE * (h + 1)]))
    return (jnp.concatenate(parts, axis=1),)


def _row(a, j):
    rows = lax.broadcasted_iota(jnp.int32, a.shape, 0)
    return jnp.sum(jnp.where(rows == j, a, 0.0), axis=0, keepdims=True)


def _shift_rows(x, halo, d):
    xs = _roll(x, d, 0)
    hs = _roll(halo, d, 0)
    r8 = lax.broadcasted_iota(jnp.int32, hs.shape, 0)
    top = jnp.where(r8 < d, hs, xs[:SUBLANE])
    return jnp.concatenate([top, xs[SUBLANE:]], axis=0)


def _conv_silu(x, halo, w):
    y = _row(w, CONV_W - 1) * x
    for j in range(CONV_W - 1):
        y = y + _row(w, j) * _shift_rows(x, halo, CONV_W - 1 - j)
    return jax.nn.silu(y)


def _head_select(offset):
    r = lax.broadcasted_iota(jnp.int32, (LANE, WIDTH), 0)
    c = lax.broadcasted_iota(jnp.int32, (LANE, WIDTH), 1)
    return (r == offset + lax.shift_right_logical(c, 7)).astype(F32)


def _row_to_heads(row, sel):
    return jnp.sum(_hi(jnp.broadcast_to(row, (SUBLANE, LANE)), sel), axis=0, keepdims=True) * (1.0 / SUBLANE)


def f_gdn_pre(i, gq, gk, gv, gab, hq, hk, hv, cwq, cwk, cwv, alog, dtb):
    live = jnp.where(i == 0, 0.0, 1.0)
    q = _conv_silu(gq, hq * live, cwq)
    k = _conv_silu(gk, hk * live, cwk)
    v = _conv_silu(gv, hv * live, cwv)
    sel_a, sel_b = _head_select(0), _head_select(HEADS)
    ga = _hi(gab, sel_a)
    gb = _hi(gab, sel_b)
    g = -jnp.exp(_row_to_heads(alog, sel_a)) * jax.nn.softplus(ga + _row_to_heads(dtb, sel_a))
    beta = jax.nn.sigmoid(gb)
    qs, ks, vs, gs, bs = [], [], [], [], []
    for h in range(HEADS):
        sl = slice(LANE * h, LANE * (h + 1))
        qh, kh = q[:, sl], k[:, sl]
        qs.append(qh * lax.rsqrt(jnp.sum(qh * qh, axis=-1, keepdims=True) + EPS) * (GDN_DIM ** -0.5))
        ks.append(kh * lax.rsqrt(jnp.sum(kh * kh, axis=-1, keepdims=True) + EPS))
        vs.append(v[:, sl])
        gs.append(g[:, sl])
        bs.append(beta[:, sl])
    return jnp.stack(qs), jnp.stack(ks), jnp.stack(vs), jnp.stack(gs), jnp.stack(bs)


def gdn_pre_bwd(i, gq, gk, gv, gab, hq, hk, hv, dq, dk, dv, dg, db, cwq, cwk, cwv, alog, dtb, cq, ck, cv):
    grads = _vjp_fn(f_gdn_pre, 12, 5)(i, gq, gk, gv, gab, hq, hk, hv, cwq, cwk, cwv, alog, dtb, dq, dk, dv, dg, db)
    dgq, dgk, dgv, dgab, dhq, dhk, dhv, dcwq, dcwk, dcwv, dalog, ddtb = grads

    def add_tail(dx, carry):
        return jnp.concatenate([dx[:-SUBLANE], dx[-SUBLANE:] + carry], axis=0)

    return (add_tail(dgq, cq), add_tail(dgk, ck), add_tail(dgv, cv), dgab,
            dcwq, dcwk, dcwv, dalog, ddtb, dhq, dhk, dhv)


def f_loss(i, x, h, tgt):
    e = x + h - tgt
    part = 0.5 * jnp.sum(e * e) * (1.0 / D_MODEL)
    dy = e * (1.0 / D_MODEL)
    return dy, dy, jnp.zeros((SUBLANE, LANE), F32) + part


def f_delta(i, o, do):
    return (jnp.stack([jnp.sum(o[:, LANE * h:LANE * (h + 1)] * do[:, LANE * h:LANE * (h + 1)], axis=-1, keepdims=True)
                       for h in range(HEADS)]),)


def _flash_tile(t_len):
    return min(512, t_len)


FLASH_HEADS = 4
FLASH_ROWS = 512
LOG2E = 1.4426950408889634


def _tri_pairs(n, key_major):
    if key_major:
        pairs = [(i, j) for j in range(n) for i in range(j, n)]
    else:
        pairs = [(i, j) for i in range(n) for j in range(i + 1)]
    return (jnp.array([p[0] for p in pairs], jnp.int32), jnp.array([p[1] for p in pairs], jnp.int32))


def _causal(rows0, shape):
    r = rows0 + lax.broadcasted_iota(jnp.int32, shape, 0)
    c = lax.broadcasted_iota(jnp.int32, shape, 1)
    return c <= r


def flash_fwd(q, k, v):
    h_n, t_len, _ = q.shape
    tq = _flash_tile(t_len)
    nq = t_len // tq
    hb = FLASH_HEADS
    c2 = (QK ** -0.5) * LOG2E
    qt, kt = _tri_pairs(nq, key_major=False)

    def body(qt_ref, kt_ref, q_ref, k_ref, v_ref, o_ref, lse_ref, m_s, l_s, acc_s):
        step = pl.program_id(1)
        qi, kj = qt_ref[step], kt_ref[step]

        @pl.when(kj == 0)
        def _():
            m_s[...] = jnp.full_like(m_s, -jnp.inf)
            l_s[...] = jnp.zeros_like(l_s)
            acc_s[...] = jnp.zeros_like(acc_s)

        def tile(diagonal):
            s = _bdot(q_ref[...], k_ref[...], _NT3, False) * c2
            if diagonal:
                s = jnp.where(_causal(0, (tq, tq))[None], s, -jnp.inf)
            m_old = m_s[...]
            m_new = jnp.maximum(m_old, jnp.max(s, axis=-1, keepdims=True))
            alpha = jnp.exp2(m_old - m_new)
            p = jnp.exp2(s - m_new)
            l_s[...] = alpha * l_s[...] + jnp.sum(p, axis=-1, keepdims=True)
            acc_s[...] = alpha * acc_s[...] + _bdot(p, v_ref[...], _NN3, False)
            m_s[...] = m_new

        @pl.when(kj < qi)
        def _():
            tile(False)

        @pl.when(kj == qi)
        def _():
            tile(True)
            o = acc_s[...] / l_s[...]
            for hh in range(hb):
                o_ref[:, LANE * hh:LANE * (hh + 1)] = o[hh]
            lse_ref[...] = m_s[...] + jnp.log2(l_s[...])

    return pl.pallas_call(
        body,
        name="flash_fwd",
        grid_spec=pltpu.PrefetchScalarGridSpec(
            num_scalar_prefetch=2,
            grid=(h_n // hb, qt.shape[0]),
            in_specs=[
                pl.BlockSpec((hb, tq, HEAD_PAD), lambda h, s, qt_ref, kt_ref: (h, qt_ref[s], 0)),
                pl.BlockSpec((hb, tq, HEAD_PAD), lambda h, s, qt_ref, kt_ref: (h, kt_ref[s], 0)),
                pl.BlockSpec((hb, tq, LANE), lambda h, s, qt_ref, kt_ref: (h, kt_ref[s], 0)),
            ],
            out_specs=[
                pl.BlockSpec((tq, hb * LANE), lambda h, s, qt_ref, kt_ref: (qt_ref[s], h)),
                pl.BlockSpec((hb, tq, 1), lambda h, s, qt_ref, kt_ref: (h, qt_ref[s], 0)),
            ],
            scratch_shapes=[pltpu.VMEM((hb, tq, 1), F32), pltpu.VMEM((hb, tq, 1), F32),
                            pltpu.VMEM((hb, tq, LANE), F32)],
        ),
        out_shape=[jax.ShapeDtypeStruct((t_len, h_n * LANE), F32), jax.ShapeDtypeStruct((h_n, t_len, 1), F32)],
        compiler_params=_cparams("parallel", "arbitrary"),
    )(qt, kt, q, k, v)


def flash_bwd(q, k, v, do, lse, delta):
    h_n, t_len, _ = q.shape
    tq = _flash_tile(t_len)
    nq = t_len // tq
    rg = min(FLASH_ROWS, tq)
    scale = QK ** -0.5
    c2 = scale * LOG2E
    qt, kt = _tri_pairs(nq, key_major=True)

    def body(qt_ref, kt_ref, q_ref, k_ref, v_ref, do_ref, lse_ref, dl_ref, dq_ref, dk_ref, dv_ref):
        step = pl.program_id(1)
        qi, kj = qt_ref[step], kt_ref[step]

        @pl.when(step == 0)
        def _():
            dq_ref[...] = jnp.zeros_like(dq_ref)

        def tile(diagonal):
            kb, vb = k_ref[0], v_ref[0]
            dk_acc = jnp.zeros((tq, HEAD_PAD), F32)
            dv_acc = jnp.zeros((tq, LANE), F32)
            dq_parts = []
            for g in range(tq // rg):
                rows = slice(g * rg, (g + 1) * rg)
                qg, dog = q_ref[0, rows, :], do_ref[rows, :]
                p = jnp.exp2(_d_nt(qg, kb) * c2 - lse_ref[0, rows])
                if diagonal:
                    p = jnp.where(_causal(g * rg, p.shape), p, 0.0)
                dv_acc = dv_acc + _d_tn(p, dog)
                ds = p * (_d_nt(dog, vb) - dl_ref[0, rows]) * scale
                dk_acc = dk_acc + _d_tn(ds, qg)
                dq_parts.append(_d_nn(ds, kb))
            dq_ref[0, pl.ds(pl.multiple_of(qi * tq, tq), tq), :] += jnp.concatenate(dq_parts, axis=0)
            return dk_acc, dv_acc

        @pl.when(kj == qi)
        def _():
            dk_ref[0], dv_ref[0] = tile(True)

        @pl.when(kj < qi)
        def _():
            dk_acc, dv_acc = tile(False)
            dk_ref[0] += dk_acc
            dv_ref[0] += dv_acc

    def qmap(h, s, qt_ref, kt_ref):
        return (h, qt_ref[s], 0)

    def kmap(h, s, qt_ref, kt_ref):
        return (h, kt_ref[s], 0)

    return pl.pallas_call(
        body,
        name="flash_bwd",
        grid_spec=pltpu.PrefetchScalarGridSpec(
            num_scalar_prefetch=2,
            grid=(h_n, qt.shape[0]),
            in_specs=[
                pl.BlockSpec((1, tq, HEAD_PAD), qmap),
                pl.BlockSpec((1, tq, HEAD_PAD), kmap),
                pl.BlockSpec((1, tq, LANE), kmap),
                pl.BlockSpec((tq, LANE), lambda h, s, qt_ref, kt_ref: (qt_ref[s], h)),
                pl.BlockSpec((1, tq, 1), qmap),
                pl.BlockSpec((1, tq, 1), qmap),
            ],
            out_specs=[
                pl.BlockSpec((1, t_len, HEAD_PAD), lambda h, s, qt_ref, kt_ref: (h, 0, 0)),
                pl.BlockSpec((1, tq, HEAD_PAD), kmap),
                pl.BlockSpec((1, tq, LANE), kmap),
            ],
        ),
        out_shape=[
            jax.ShapeDtypeStruct((h_n, t_len, HEAD_PAD), F32),
            jax.ShapeDtypeStruct((h_n, t_len, HEAD_PAD), F32),
            jax.ShapeDtypeStruct((h_n, t_len, LANE), F32),
        ],
        compiler_params=_cparams("parallel", "arbitrary"),
    )(qt, kt, q, k, v, do, lse, delta)


def gdn_step(s, q, k, v, gb, bb):
    h_n, c = q.shape[0], CHUNK
    ii = lax.broadcasted_iota(jnp.int32, (1, c, c), 1)
    jj = lax.broadcasted_iota(jnp.int32, (1, c, c), 2)
    incl, strict = ii >= jj, ii > jj
    gcb = _bhi(jnp.broadcast_to(incl.astype(F32), (h_n, c, c)), gb)
    lane = lax.broadcasted_iota(jnp.int32, (1, 1, LANE), 2)
    e0, e1 = (lane == 0).astype(F32), (lane == 1).astype(F32)
    diff = _bhi_nt(gcb * e0 + e1, e0 - gcb * e1)
    decay = jnp.where(incl, jnp.exp(jnp.where(incl, diff, 0.0)), 0.0)
    kb, vb = k * bb, v * bb
    egc = jnp.exp(gcb)
    lmat = jnp.where(strict, _bmm_nt(kb, k) * decay, 0.0)
    inv = (ii == jj).astype(F32) - lmat
    pw = _bh3(lmat, lmat)
    for step in range(5):
        inv = inv + _bh3(inv, pw)
        if step < 4:
            pw = _bh3(pw, pw)
    u = _bh3(inv, vb)
    w = _bh3(inv, kb * egc)
    attn = _bmm_nt(q, k) * decay
    qd = q * egc
    g_end = jnp.sum(gb, axis=1, keepdims=True)
    kd = k * jnp.exp(g_end - gcb)
    v_new = u - _bmm(w, s)
    o = _bmm(qd, s) + _bmm(attn, v_new)
    s_new = s * jnp.exp(g_end) + _bmm_tn(kd, v_new)
    return s_new, o


def gdn_fwd(q, k, v, gb, bb):
    h_n, t_len, d = q.shape
    n = t_len // CHUNK
    blk = pl.BlockSpec((h_n, CHUNK, d), lambda i: (0, i, 0))

    def body(q_ref, k_ref, v_ref, g_ref, b_ref, o_ref, sall_ref, s_s):
        @pl.when(pl.program_id(0) == 0)
        def _():
            s_s[...] = jnp.zeros_like(s_s)

        s = s_s[...]
        sall_ref[0] = s
        s_new, o = gdn_step(s, q_ref[...], k_ref[...], v_ref[...], g_ref[...], b_ref[...])
        o_ref[...] = o
        s_s[...] = s_new

    return pl.pallas_call(
        body,
        name="gdn_fwd",
        grid=(n,),
        in_specs=[blk] * 5,
        out_specs=[blk, pl.BlockSpec((1, h_n, d, d), lambda i: (i, 0, 0, 0))],
        out_shape=[jax.ShapeDtypeStruct((h_n, t_len, d), F32), jax.ShapeDtypeStruct((n, h_n, d, d), F32)],
        scratch_shapes=[pltpu.VMEM((h_n, d, d), F32)],
        compiler_params=_cparams("arbitrary"),
    )(q, k, v, gb, bb)


def gdn_bwd(q, k, v, gb, bb, s_all, do):
    h_n, t_len, d = q.shape
    n = t_len // CHUNK
    blk = pl.BlockSpec((h_n, CHUNK, d), lambda i: (0, n - 1 - i, 0))

    def body(q_ref, k_ref, v_ref, g_ref, b_ref, sall_ref, do_ref, dq_ref, dk_ref, dv_ref, dg_ref, db_ref, ds_s):
        @pl.when(pl.program_id(0) == 0)
        def _():
            ds_s[...] = jnp.zeros_like(ds_s)

        _, pull = jax.vjp(gdn_step, sall_ref[0], q_ref[...], k_ref[...], v_ref[...], g_ref[...], b_ref[...])
        ds, dq, dk, dv, dg, db = pull((ds_s[...], do_ref[...]))
        ds_s[...] = ds
        dq_ref[...], dk_ref[...], dv_ref[...], dg_ref[...], db_ref[...] = dq, dk, dv, dg, db

    return pl.pallas_call(
        body,
        name="gdn_bwd",
        grid=(n,),
        in_specs=[blk] * 5 + [pl.BlockSpec((1, h_n, d, d), lambda i: (n - 1 - i, 0, 0, 0)), blk],
        out_specs=[blk] * 5,
        out_shape=[jax.ShapeDtypeStruct((h_n, t_len, d), F32)] * 5,
        scratch_shapes=[pltpu.VMEM((h_n, d, d), F32)],
        compiler_params=_cparams("arbitrary"),
    )(q, k, v, gb, bb, s_all, do)


def _pad_cols(a, n):
    return jnp.pad(a, ((0, 0), (0, n - a.shape[1])))


def arrange_w_in(w):
    pieces, start = [], 0
    for n in R_SPLITS:
        pieces.append(w[:, start:start + n])
        start += n
    cq, ckv, kr, mgate, gq, gk, gv, ga, gb, ggate = pieces
    return jnp.concatenate([mgate, gq, gk, gv, ggate, cq, ckv, _pad_cols(kr, LANE),
                            _pad_cols(jnp.concatenate([ga, gb], axis=1), LANE)], axis=1)


def unarrange_w_in(g):
    def cols(start, n):
        return g[:, start:start + n]
    return jnp.concatenate([cols(P_CQ, Q_LORA), cols(P_CKV, KV_LORA), cols(P_KR, ROPE), cols(P_MGATE, WIDTH),
                            cols(P_GQ, WIDTH), cols(P_GK, WIDTH), cols(P_GV, WIDTH), cols(P_GAB, HEADS),
                            cols(P_GAB + HEADS, HEADS), cols(P_GGATE, WIDTH)], axis=1)


def arrange_w_uq(w):
    w = w.reshape(w.shape[0], HEADS, QK)
    return jnp.pad(w, ((0, 0), (0, 0), (0, HEAD_PAD - QK))).reshape(w.shape[0], HEADS * HEAD_PAD)


def unarrange_w_uq(g):
    return g.reshape(g.shape[0], HEADS, HEAD_PAD)[:, :, :QK].reshape(g.shape[0], HEADS * QK)


def local_step(x, pos, tgt, p):
    t_len = x.shape[0]
    w_in, w_uq, w_ukv, w_out = p["w_in"], p["w_uq"], p["w_ukv"], p["w_out"]
    norm_gain = p["norm_gain"].reshape(1, D_MODEL)
    qa_gain = p["mla_q_a_gain"].reshape(1, Q_LORA)
    kva_gain = p["mla_kv_a_gain"].reshape(1, KV_LORA)
    qg = _pad_cols(p["mla_q_norm_gain"].reshape(1, QK), HEAD_PAD)
    kg = _pad_cols(p["mla_k_norm_gain"].reshape(1, QK), HEAD_PAD)
    cw = p["gdn_conv_w"].reshape(CONV_W, 3 * WIDTH)
    cwq, cwk, cwv = cw[:, :WIDTH], cw[:, WIDTH:2 * WIDTH], cw[:, 2 * WIDTH:]
    alog = _pad_cols(p["gdn_a_log"].reshape(1, HEADS), LANE)
    dtb = _pad_cols(p["gdn_dt_bias"].reshape(1, HEADS), LANE)
    og = p["gdn_out_norm_gain"].reshape(1, GDN_DIM)
    half = ROPE // 2
    inv_freq = jnp.power(ROPE_THETA, -jnp.arange(half, dtype=F32) / half)
    invf = _pad_cols(jnp.concatenate([inv_freq, inv_freq]).reshape(1, ROPE), LANE)

    rt = 256
    r = "r"
    (xn,) = rowwise("rms_x", f_rms_x, t_len, rt, [(x, (r, D_MODEL, 0))], [norm_gain], [(r, D_MODEL, _BF)])
    proj = matmul("proj", xn, w_in, "nn")
    cq_in = (proj, (r, Q_LORA, P_CQ // Q_LORA))
    ckv_in = (proj, (r, KV_LORA, P_CKV // KV_LORA))
    kr_in = (proj, (r, LANE, P_KR // LANE))
    mgate_in = (proj, (r, WIDTH, P_MGATE // WIDTH))
    ggate_in = (proj, (r, WIDTH, P_GGATE // WIDTH))
    gqkv_in = [(proj, (r, WIDTH, P_GQ // WIDTH)), (proj, (r, WIDTH, P_GK // WIDTH)), (proj, (r, WIDTH, P_GV // WIDTH))]
    gab_in = (proj, (r, LANE, P_GAB // LANE))
    halos = [(proj, ("halo", WIDTH, P_GQ // WIDTH)), (proj, ("halo", WIDTH, P_GK // WIDTH)),
             (proj, ("halo", WIDTH, P_GV // WIDTH))]

    q_lat, kv_lat = rowwise("lat", f_lat, t_len, rt, [cq_in, ckv_in], [qa_gain, kva_gain],
                            [(r, Q_LORA, _BF), (r, KV_LORA, _BF)])
    q_raw = matmul("q_up", q_lat, w_uq, "nn")
    kv_raw = matmul("kv_up", kv_lat, w_ukv, "nn")
    wide = HEADS * HEAD_PAD
    head_in = [(q_raw, (r, wide, 0)), (kv_raw, (r, wide, 0)), kr_in]
    pos_in = (pos, (r, 1, 0))
    q_full, k_full, v_mla = rowwise(
        "head", lambda i, qr, kvr, kr, ps, qg_, kg_, iv: f_head(i, qr, kvr, kr, qg_, kg_, ps, iv), t_len, rt,
        head_in + [pos_in], [qg, kg, invf],
        [("h", HEADS, HEAD_PAD, _BF), ("h", HEADS, HEAD_PAD, _BF), ("h", HEADS, LANE, _BF)])
    o_mla, lse = flash_fwd(q_full, k_full, v_mla)

    pre_in = gqkv_in + [gab_in] + halos
    pre_full = [cwq, cwk, cwv, alog, dtb]
    hkind = ("h", HEADS, GDN_DIM, F32)
    gq_n, gk_n, gv_n, g_b, b_b = rowwise("gdn_pre", f_gdn_pre, t_len, rt, pre_in, pre_full, [hkind] * 5)
    o_gdn, s_all = gdn_fwd(gq_n, gk_n, gv_n, g_b, b_b)

    mix_in = [(o_mla, (r, WIDTH, 0)), mgate_in, (o_gdn, ("h",)), ggate_in]
    (mixed,) = rowwise("mix", f_mix, t_len, rt, mix_in, [og], [(r, 2 * WIDTH, _BF)])
    h_out = matmul("out_proj", mixed, w_out, "nn")
    dy, dy_mx, loss_acc = rowwise("loss", f_loss, t_len, rt,
                                  [(x, (r, D_MODEL, 0)), (h_out, (r, D_MODEL, 0)), (tgt, (r, D_MODEL, 0))], [],
                                  [(r, D_MODEL, F32), (r, D_MODEL, _BF)], [(SUBLANE, LANE)])
    loss = loss_acc[0, 0]

    d_mixed = matmul("d_mixed", dy_mx, w_out, "nt")
    g_w_out = matmul("g_w_out", mixed, dy_mx, "tn")

    def mix_bwd(i, o_mla_, mgate_, o_gdn_, ggate_, d_mixed_, og_):
        return _vjp_fn(f_mix, 5, 1)(i, o_mla_, mgate_, o_gdn_, ggate_, og_, d_mixed_)

    do_mla, d_mgate, do_gdn, d_ggate, g_og = rowwise(
        "mix_bwd", mix_bwd, t_len, rt, mix_in + [(d_mixed, (r, 2 * WIDTH, 0))], [og],
        [(r, WIDTH, F32), (r, WIDTH, _BF), hkind, (r, WIDTH, _BF)], [(1, GDN_DIM)])
    dq_n, dk_n, dv_n, dg_b, db_b = gdn_bwd(gq_n, gk_n, gv_n, g_b, b_b, s_all, do_gdn)
    cts_in = [(a, ("h",)) for a in (dq_n, dk_n, dv_n, dg_b, db_b)]
    d_gq, d_gk, d_gv, d_gab, g_cwq, g_cwk, g_cwv, g_alog, g_dtb = rowwise(
        "gdn_pre_bwd", gdn_pre_bwd, t_len, rt, pre_in + cts_in, pre_full,
        [(r, WIDTH, _BF)] * 3 + [(r, LANE, _BF)],
        [(CONV_W, WIDTH)] * 3 + [(1, LANE)] * 2, carries=[(SUBLANE, WIDTH)] * 3, reverse=True)

    (delta,) = rowwise("delta", f_delta, t_len, rt, [(o_mla, (r, WIDTH, 0)), (do_mla, (r, WIDTH, 0))], [],
                       [("h", HEADS, 1, F32)])
    dq_full, dk_full, dv_mla = flash_bwd(q_full, k_full, v_mla, do_mla, lse, delta)
    head_cts = [(a, ("h",)) for a in (dq_full, dk_full, dv_mla)]

    def head_bwd(i, q_raw_, kv_raw_, kr_, pos_, dq_, dk_, dv_, qg_, kg_, invf_):
        return _vjp_fn(f_head, 5, 3)(i, q_raw_, kv_raw_, kr_, qg_, kg_, pos_, invf_, dq_, dk_, dv_)

    dq_raw, dkv_raw, d_kr, g_qg, g_kg = rowwise(
        "head_bwd", head_bwd, t_len, rt // 2, head_in + [pos_in] + head_cts, [qg, kg, invf],
        [(r, wide, _BF), (r, wide, _BF), (r, LANE, _BF)], [(1, HEAD_PAD), (1, HEAD_PAD)])
    dq_lat = matmul("dq_lat", dq_raw, w_uq, "nt")
    g_w_uq = matmul("g_w_uq", q_lat, dq_raw, "tn")
    dkv_lat = matmul("dkv_lat", dkv_raw, w_ukv, "nt")
    g_w_ukv = matmul("g_w_ukv", kv_lat, dkv_raw, "tn")

    def lat_bwd(i, cq_, ckv_, dql_, dkl_, gq_, gkv_):
        return _vjp_fn(f_lat, 4, 2)(i, cq_, ckv_, gq_, gkv_, dql_, dkl_)

    d_cq, d_ckv, g_qa, g_kva = rowwise(
        "lat_bwd", lat_bwd, t_len, rt, [cq_in, ckv_in, (dq_lat, (r, Q_LORA, 0)), (dkv_lat, (r, KV_LORA, 0))],
        [qa_gain, kva_gain], [(r, Q_LORA, _BF), (r, KV_LORA, _BF)], [(1, Q_LORA), (1, KV_LORA)])

    d_proj = jnp.concatenate([d_mgate, d_gq, d_gk, d_gv, d_ggate, d_cq, d_ckv, d_kr, d_gab], axis=1)
    d_xn = matmul("d_xn", d_proj, w_in, "nt")
    g_w_in = matmul("g_w_in", xn, d_proj, "tn")

    def rms_x_bwd(i, x_, dxn_, dy_, gain_):
        dx, dgain = _vjp_fn(f_rms_x, 2, 1)(i, x_, gain_, dxn_)
        return dx + dy_, dgain

    grad_x, g_norm = rowwise("rms_x_bwd", rms_x_bwd, t_len, rt,
                             [(x, (r, D_MODEL, 0)), (d_xn, (r, D_MODEL, 0)), (dy, (r, D_MODEL, 0))], [norm_gain],
                             [(r, D_MODEL, F32)], [(1, D_MODEL)])

    grads = {
        "norm_gain": g_norm, "w_in": g_w_in, "mla_q_a_gain": g_qa, "mla_kv_a_gain": g_kva, "w_uq": g_w_uq,
        "w_ukv": g_w_ukv, "mla_q_norm_gain": g_qg[:, :QK], "mla_k_norm_gain": g_kg[:, :QK],
        "gdn_conv_w": jnp.concatenate([g_cwq, g_cwk, g_cwv], axis=1), "gdn_a_log": g_alog[:, :HEADS],
        "gdn_dt_bias": g_dtb[:, :HEADS], "gdn_out_norm_gain": g_og, "w_out": g_w_out,
    }
    return loss, grad_x, grads


MESH = pl.DeviceIdType.MESH
ANY = pl.BlockSpec(memory_space=pl.ANY)
CHIP_FLIPS = ((1, 0), (0, 1), (1, 1))


def _place():
    return lax.axis_index("x"), lax.axis_index("y"), lax.axis_index("c")


def _flip(v, f):
    return 1 - v if f else v


def all_gather(shards):
    n_arr = len(shards)

    def body(*refs):
        x_refs, o_refs = refs[:n_arr], refs[n_arr:2 * n_arr]
        send_sems, recv_sems, local_sems = refs[2 * n_arr:]
        x, y, c = _place()
        me, sibling = (x, y, c), (x, y, 1 - c)
        chips = [(_flip(x, fx), _flip(y, fy)) for fx, fy in CHIP_FLIPS]

        def copy(a, k, block, to, src=None):
            px, py, pc = block
            dst = o_refs[a].at[4 * px + 2 * py + pc]
            return pltpu.make_async_remote_copy(
                src_ref=dst if src is None else src, dst_ref=dst, send_sem=send_sems.at[a, k],
                recv_sem=recv_sems.at[a, k], device_id=to, device_id_type=MESH)

        mine, first, passed = [], [], []
        for a in range(n_arr):
            cp = pltpu.make_async_copy(x_refs[a], o_refs[a].at[4 * x + 2 * y + c], local_sems.at[a])
            cp.start()
            mine.append(cp)
            first.append(copy(a, 0, me, sibling, src=x_refs[a]))
            first += [copy(a, 1 + j, me, (*chip, c), src=x_refs[a]) for j, chip in enumerate(chips)]
        for cp in first:
            cp.start()
        for j, chip in enumerate(chips):
            for a in range(n_arr):
                copy(a, 1 + j, (*chip, c), me).wait_recv()
                cp = copy(a, 4 + j, (*chip, c), sibling)
                cp.start()
                passed.append(cp)
        for a in range(n_arr):
            copy(a, 0, sibling, me).wait_recv()
            for j, chip in enumerate(chips):
                copy(a, 4 + j, (*chip, 1 - c), me).wait_recv()
        for cp in first + passed:
            cp.wait_send()
        for cp in mine:
            cp.wait()

    return pl.pallas_call(
        body,
        name="all_gather",
        out_shape=[jax.ShapeDtypeStruct((N_DEV,) + s.shape, s.dtype) for s in shards],
        in_specs=[ANY] * n_arr,
        out_specs=[ANY] * n_arr,
        scratch_shapes=[pltpu.SemaphoreType.DMA((n_arr, 7)), pltpu.SemaphoreType.DMA((n_arr, 7)),
                        pltpu.SemaphoreType.DMA((n_arr,))],
    )(*shards)


def exchange_cores(grads):
    n_arr = len(grads)

    def body(*refs):
        g_refs, o_refs = refs[:n_arr], refs[n_arr:2 * n_arr]
        send_sems, recv_sems = refs[2 * n_arr:]
        x, y, c = _place()
        copies = []
        for a in range(n_arr):
            for q in range(4):
                cp = pltpu.make_async_remote_copy(
                    src_ref=g_refs[a].at[2 * q + (1 - c)], dst_ref=o_refs[a].at[q], send_sem=send_sems.at[a, q],
                    recv_sem=recv_sems.at[a, q], device_id=(x, y, 1 - c), device_id_type=MESH)
                cp.start()
                copies.append(cp)
        for cp in copies:
            cp.wait()

    return pl.pallas_call(
        body,
        name="exchange_cores",
        out_shape=[jax.ShapeDtypeStruct((4,) + g.shape[1:], g.dtype) for g in grads],
        in_specs=[ANY] * n_arr,
        out_specs=[ANY] * n_arr,
        scratch_shapes=[pltpu.SemaphoreType.DMA((n_arr, 4)), pltpu.SemaphoreType.DMA((n_arr, 4))],
    )(*grads)


def exchange_chips(parts):
    n_arr = len(parts)

    def body(*refs):
        p_refs, o_refs = refs[:n_arr], refs[n_arr:2 * n_arr]
        send_sems, recv_sems = refs[2 * n_arr:]
        x, y, c = _place()
        copies = []
        for a in range(n_arr):
            for j, (fx, fy) in enumerate(CHIP_FLIPS):
                px, py = _flip(x, fx), _flip(y, fy)
                cp = pltpu.make_async_remote_copy(
                    src_ref=p_refs[a].at[2 * px + py], dst_ref=o_refs[a].at[j], send_sem=send_sems.at[a, j],
                    recv_sem=recv_sems.at[a, j], device_id=(px, py, c), device_id_type=MESH)
                cp.start()
                copies.append(cp)
        for cp in copies:
            cp.wait()

    return pl.pallas_call(
        body,
        name="exchange_chips",
        out_shape=[jax.ShapeDtypeStruct((3,) + p.shape[1:], p.dtype) for p in parts],
        in_specs=[ANY] * n_arr,
        out_specs=[ANY] * n_arr,
        scratch_shapes=[pltpu.SemaphoreType.DMA((n_arr, 3)), pltpu.SemaphoreType.DMA((n_arr, 3))],
    )(*parts)


def gather_small(v):
    def body(v_ref, o_ref, send_sems, recv_sems, local_sem):
        x, y, c = _place()
        me = 4 * x + 2 * y + c
        mine = pltpu.make_async_copy(v_ref, o_ref.at[me], local_sem)
        mine.start()
        copies = []
        for k in range(1, N_DEV):
            fx, fy, fc = (k >> 2) & 1, (k >> 1) & 1, k & 1
            cp = pltpu.make_async_remote_copy(
                src_ref=v_ref, dst_ref=o_ref.at[me], send_sem=send_sems.at[k - 1], recv_sem=recv_sems.at[k - 1],
                device_id=(_flip(x, fx), _flip(y, fy), _flip(c, fc)), device_id_type=MESH)
            cp.start()
            copies.append(cp)
        for cp in copies:
            cp.wait()
        mine.wait()

    return pl.pallas_call(
        body,
        name="gather_small",
        out_shape=jax.ShapeDtypeStruct((N_DEV,) + v.shape, v.dtype),
        in_specs=[ANY],
        out_specs=ANY,
        scratch_shapes=[pltpu.SemaphoreType.DMA((N_DEV - 1,)), pltpu.SemaphoreType.DMA((N_DEV - 1,)),
                        pltpu.SemaphoreType.DMA],
    )(v)


def _row_tile(rows):
    for t in (256, 128, 64, 32, 16, 8):
        if rows % t == 0:
            return t
    return rows


def add_core_parts(name, g, recv, c_idx):
    _, rows, cols = g.shape
    tr = _row_tile(rows)

    def body(c_ref, g_ref, r_ref, o_ref):
        o_ref[...] = g_ref[...] + r_ref[...]

    return pl.pallas_call(
        body,
        name=name,
        grid_spec=pltpu.PrefetchScalarGridSpec(
            num_scalar_prefetch=1,
            grid=(4, rows // tr),
            in_specs=[pl.BlockSpec((1, tr, cols), lambda q, i, c_ref: (2 * q + c_ref[0], i, 0)),
                      pl.BlockSpec((1, tr, cols), lambda q, i, c_ref: (q, i, 0))],
            out_specs=pl.BlockSpec((1, tr, cols), lambda q, i, c_ref: (q, i, 0)),
        ),
        out_shape=jax.ShapeDtypeStruct((4, rows, cols), F32),
        compiler_params=_cparams("parallel", "parallel"),
    )(c_idx, g, recv)


def _adamw(w, g, m, v):
    m = ADAM_B1 * m + (1.0 - ADAM_B1) * g
    v = ADAM_B2 * v + (1.0 - ADAM_B2) * (g * g)
    m_hat = m / (1.0 - ADAM_B1 ** ADAM_STEP)
    v_hat = v / (1.0 - ADAM_B2 ** ADAM_STEP)
    delta = -ADAM_LR * (m_hat / (jnp.sqrt(v_hat) + ADAM_EPS) + ADAM_WD * w)
    return delta, m, v


def adamw_sharded(name, parts, recv, q_idx, w, m, v):
    rows, cols = w.shape
    tr = _row_tile(rows)

    def body(q_ref, p_ref, r_ref, w_ref, m_ref, v_ref, g_out, d_out, m_out, v_out):
        g = p_ref[0] + r_ref[0] + r_ref[1] + r_ref[2]
        d, m_new, v_new = _adamw(w_ref[...], g, m_ref[...], v_ref[...])
        g_out[...], d_out[...], m_out[...], v_out[...] = g, d, m_new, v_new

    blk = pl.BlockSpec((tr, cols), lambda i, q_ref: (i, 0))
    return pl.pallas_call(
        body,
        name=name,
        grid_spec=pltpu.PrefetchScalarGridSpec(
            num_scalar_prefetch=1,
            grid=(rows // tr,),
            in_specs=[pl.BlockSpec((1, tr, cols), lambda i, q_ref: (q_ref[0], i, 0)),
                      pl.BlockSpec((3, tr, cols), lambda i, q_ref: (0, i, 0)), blk, blk, blk],
            out_specs=[blk] * 4,
        ),
        out_shape=[jax.ShapeDtypeStruct((rows, cols), F32)] * 4,
        compiler_params=_cparams("parallel"),
    )(q_idx, parts, recv, w, m, v)


def adamw_small(gathered, w, m, v):
    def body(g_ref, w_ref, m_ref, v_ref, g_out, d_out, m_out, v_out):
        g = g_ref[0]
        for j in range(1, N_DEV):
            g = g + g_ref[j]
        d, m_new, v_new = _adamw(w_ref[...], g, m_ref[...], v_ref[...])
        g_out[...], d_out[...], m_out[...], v_out[...] = g, d, m_new, v_new

    return pl.pallas_call(body, name="adamw_small", out_shape=[jax.ShapeDtypeStruct(w.shape, F32)] * 4)(gathered, w, m, v)


SHARDED = ("w_in", "w_uq", "w_ukv", "gdn_conv_w", "w_out")
SMALL = (("norm_gain", D_MODEL), ("mla_q_a_gain", Q_LORA), ("mla_kv_a_gain", KV_LORA), ("mla_q_norm_gain", QK),
         ("mla_k_norm_gain", QK), ("gdn_a_log", HEADS), ("gdn_dt_bias", HEADS), ("gdn_out_norm_gain", GDN_DIM))
WEIGHT_ORDER = ("norm_gain", "w_in", "mla_q_a_gain", "mla_kv_a_gain", "w_uq", "w_ukv", "mla_q_norm_gain",
                "mla_k_norm_gain", "gdn_conv_w", "gdn_a_log", "gdn_dt_bias", "gdn_out_norm_gain", "w_out")


def _pack_small(d):
    rows = []
    for name, n in SMALL:
        a = d[name].reshape(-1).astype(F32)
        n_pad = -(-n // LANE) * LANE
        rows.append(jnp.pad(a, (0, n_pad - n)).reshape(n_pad // LANE, LANE))
    packed = jnp.concatenate(rows, axis=0)
    return jnp.pad(packed, ((0, -packed.shape[0] % SUBLANE), (0, 0)))


def _unpack_small(packed):
    out, row = {}, 0
    for name, n in SMALL:
        n_rows = -(-n // LANE)
        out[name] = packed[row:row + n_rows].reshape(-1)[:n].reshape(1, n)
        row += n_rows
    return out


def kernel(x, positions, norm_gain, w_in, mla_q_a_gain, mla_kv_a_gain, w_uq, w_ukv, mla_q_norm_gain, mla_k_norm_gain, gdn_conv_w, gdn_a_log, gdn_dt_bias, gdn_out_norm_gain, w_out, loss_target, m_norm_gain, m_w_in, m_mla_q_a_gain, m_mla_kv_a_gain, m_w_uq, m_w_ukv, m_mla_q_norm_gain, m_mla_k_norm_gain, m_gdn_conv_w, m_gdn_a_log, m_gdn_dt_bias, m_gdn_out_norm_gain, m_w_out, v_norm_gain, v_w_in, v_mla_q_a_gain, v_mla_kv_a_gain, v_w_uq, v_w_ukv, v_mla_q_norm_gain, v_mla_k_norm_gain, v_gdn_conv_w, v_gdn_a_log, v_gdn_dt_bias, v_gdn_out_norm_gain, v_w_out):
    w = dict(norm_gain=norm_gain, w_in=w_in, mla_q_a_gain=mla_q_a_gain, mla_kv_a_gain=mla_kv_a_gain, w_uq=w_uq,
             w_ukv=w_ukv, mla_q_norm_gain=mla_q_norm_gain, mla_k_norm_gain=mla_k_norm_gain, gdn_conv_w=gdn_conv_w,
             gdn_a_log=gdn_a_log, gdn_dt_bias=gdn_dt_bias, gdn_out_norm_gain=gdn_out_norm_gain, w_out=w_out)
    m = dict(norm_gain=m_norm_gain, w_in=m_w_in, mla_q_a_gain=m_mla_q_a_gain, mla_kv_a_gain=m_mla_kv_a_gain,
             w_uq=m_w_uq, w_ukv=m_w_ukv, mla_q_norm_gain=m_mla_q_norm_gain, mla_k_norm_gain=m_mla_k_norm_gain,
             gdn_conv_w=m_gdn_conv_w, gdn_a_log=m_gdn_a_log, gdn_dt_bias=m_gdn_dt_bias,
             gdn_out_norm_gain=m_gdn_out_norm_gain, w_out=m_w_out)
    v = dict(norm_gain=v_norm_gain, w_in=v_w_in, mla_q_a_gain=v_mla_q_a_gain, mla_kv_a_gain=v_mla_kv_a_gain,
             w_uq=v_w_uq, w_ukv=v_w_ukv, mla_q_norm_gain=v_mla_q_norm_gain, mla_k_norm_gain=v_mla_k_norm_gain,
             gdn_conv_w=v_gdn_conv_w, gdn_a_log=v_gdn_a_log, gdn_dt_bias=v_gdn_dt_bias,
             gdn_out_norm_gain=v_gdn_out_norm_gain, w_out=v_w_out)
    t_len = x.shape[1]

    shards = [w[n][0] if n == "gdn_conv_w" else w[n][0].astype(_BF) for n in SHARDED]
    a_w_in, a_w_uq, a_w_ukv, a_cw, a_w_out = all_gather(shards)

    def cols_whole(g):
        return g.transpose(1, 0, 2).reshape(g.shape[1], N_DEV * g.shape[2])

    p = {n: w[n] for n, _ in SMALL}
    p["w_in"] = arrange_w_in(cols_whole(a_w_in))
    p["w_uq"] = arrange_w_uq(cols_whole(a_w_uq))
    p["w_ukv"] = cols_whole(a_w_ukv)
    p["gdn_conv_w"] = cols_whole(a_cw)
    p["w_out"] = a_w_out.reshape(N_DEV * a_w_out.shape[1], a_w_out.shape[2])

    pos = positions.reshape(t_len, 1).astype(F32)
    loss, grad_x, grads = local_step(x.reshape(t_len, D_MODEL), pos, loss_target.reshape(t_len, D_MODEL), p)
    loss = lax.psum(loss, ("x", "y", "c"))

    def col_blocks(g):
        return g.reshape(g.shape[0], N_DEV, g.shape[1] // N_DEV).transpose(1, 0, 2)

    blocks = [col_blocks(unarrange_w_in(grads["w_in"])), col_blocks(unarrange_w_uq(grads["w_uq"])),
              col_blocks(grads["w_ukv"]), col_blocks(grads["gdn_conv_w"]),
              grads["w_out"].reshape(N_DEV, D_MODEL // N_DEV, D_MODEL)]
    xi, yi, ci = _place()
    c_idx = jnp.reshape(ci, (1,)).astype(jnp.int32)
    q_idx = jnp.reshape(2 * xi + yi, (1,)).astype(jnp.int32)
    from_sibling = exchange_cores(blocks)
    parts = [add_core_parts("add_" + n, g, r, c_idx) for n, g, r in zip(SHARDED, blocks, from_sibling)]
    from_chips = exchange_chips(parts)
    out = {}
    for n, prt, rcv in zip(SHARDED, parts, from_chips):
        shape = w[n].shape
        res = adamw_sharded("adamw_" + n, prt, rcv, q_idx, w[n].reshape(shape[-2:]), m[n].reshape(shape[-2:]),
                            v[n].reshape(shape[-2:]))
        out[n] = [a.reshape(shape) for a in res]

    small_all = gather_small(_pack_small(grads))
    res = adamw_small(small_all, _pack_small(w), _pack_small(m), _pack_small(v))
    unpacked = [_unpack_small(a) for a in res]
    for n, _ in SMALL:
        out[n] = [u[n] for u in unpacked]

    return (loss, grad_x.reshape(x.shape), *[out[n][0] for n in WEIGHT_ORDER], *[out[n][1] for n in WEIGHT_ORDER],
            *[out[n][2] for n in WEIGHT_ORDER], *[out[n][3] for n in WEIGHT_ORDER])
```

```python
import functools

import jax
import jax.numpy as jnp
from jax import lax
from jax.experimental import pallas as pl
from jax.experimental.pallas import tpu as pltpu

F32 = jnp.float32
_BF = jnp.bfloat16
HI = lax.Precision.HIGHEST

D_MODEL = 2048
HEADS = 8
NOPE = 128
ROPE = 64
QK = NOPE + ROPE
Q_LORA = 512
KV_LORA = 256
HEAD_PAD = 256
GDN_DIM = 128
WIDTH = HEADS * 128
CONV_W = 4
CHUNK = 64
ROPE_THETA = 10000.0
EPS = 1e-6
N_DEV = 8
LANE = 128
SUBLANE = 8
VMEM_LIMIT = 48 * 1024 * 1024

ADAM_LR, ADAM_B1, ADAM_B2, ADAM_EPS, ADAM_WD, ADAM_STEP = 0.001, 0.9, 0.999, 1e-08, 0.01, 10

P_MGATE, P_GQ, P_GK, P_GV, P_GGATE = 0, 1024, 2048, 3072, 4096
P_CQ, P_CKV, P_KR, P_GAB = 5120, 5632, 5888, 6016
P_COLS = 6144
R_SPLITS = (512, 256, 64, 1024, 1024, 1024, 1024, 8, 8, 1024)


def _cparams(*sem):
    return pltpu.CompilerParams(dimension_semantics=sem, vmem_limit_bytes=VMEM_LIMIT)


def _d_nn(a, b):
    return jnp.dot(a.astype(_BF), b.astype(_BF), preferred_element_type=F32)


def _d_nt(a, b):
    return lax.dot_general(a.astype(_BF), b.astype(_BF), (((1,), (1,)), ((), ())), preferred_element_type=F32)


def _d_tn(a, b):
    return lax.dot_general(a.astype(_BF), b.astype(_BF), (((0,), (0,)), ((), ())), preferred_element_type=F32)


@jax.custom_vjp
def _mm(a, b):
    return _d_nn(a, b)


_mm.defvjp(lambda a, b: (_d_nn(a, b), (a, b)), lambda r, g: (_d_nt(g, r[1]), _d_tn(r[0], g)))


@jax.custom_vjp
def _mm_nt(a, b):
    return _d_nt(a, b)


_mm_nt.defvjp(lambda a, b: (_d_nt(a, b), (a, b)), lambda r, g: (_d_nn(g, r[1]), _d_tn(g, r[0])))


@jax.custom_vjp
def _mm_tn(a, b):
    return _d_tn(a, b)


_mm_tn.defvjp(lambda a, b: (_d_tn(a, b), (a, b)), lambda r, g: (_d_nt(r[1], g), _d_nn(r[0], g)))


def _hi(a, b):
    return jnp.dot(a, b, preferred_element_type=F32, precision=HI)


_NN3 = (((2,), (1,)), ((0,), (0,)))
_NT3 = (((2,), (2,)), ((0,), (0,)))
_TN3 = (((1,), (1,)), ((0,), (0,)))


def _bdot(a, b, dims, hi):
    if hi:
        return lax.dot_general(a, b, dims, preferred_element_type=F32, precision=hi)
    return lax.dot_general(a.astype(_BF), b.astype(_BF), dims, preferred_element_type=F32)


def _batched_matmuls(hi):
    nn = jax.custom_vjp(lambda a, b: _bdot(a, b, _NN3, hi))
    nt = jax.custom_vjp(lambda a, b: _bdot(a, b, _NT3, hi))
    tn = jax.custom_vjp(lambda a, b: _bdot(a, b, _TN3, hi))
    nn.defvjp(lambda a, b: (_bdot(a, b, _NN3, hi), (a, b)),
              lambda r, g: (_bdot(g, r[1], _NT3, hi), _bdot(r[0], g, _TN3, hi)))
    nt.defvjp(lambda a, b: (_bdot(a, b, _NT3, hi), (a, b)),
              lambda r, g: (_bdot(g, r[1], _NN3, hi), _bdot(g, r[0], _TN3, hi)))
    tn.defvjp(lambda a, b: (_bdot(a, b, _TN3, hi), (a, b)),
              lambda r, g: (_bdot(r[1], g, _NT3, hi), _bdot(r[0], g, _NN3, hi)))
    return nn, nt, tn


_bmm, _bmm_nt, _bmm_tn = _batched_matmuls(False)
_bhi, _bhi_nt, _bhi_tn = _batched_matmuls(HI)
_bh3, _bh3_nt, _bh3_tn = _batched_matmuls(lax.Precision.HIGH)


@functools.partial(jax.custom_vjp, nondiff_argnums=(1, 2))
def _roll(x, shift, axis):
    return pltpu.roll(x, shift, axis)


def _roll_fwd(x, shift, axis):
    return pltpu.roll(x, shift, axis), None


def _roll_bwd(shift, axis, _, g):
    n = g.shape[axis]
    return (pltpu.roll(g, (n - shift) % n, axis),)


_roll.defvjp(_roll_fwd, _roll_bwd)


def _rms(x, gain):
    return x * lax.rsqrt(jnp.mean(x * x, axis=-1, keepdims=True) + EPS) * gain


MM_TILE = 1024


def matmul(name, a, b, mode):
    if mode == "nn":
        (m, k), (k2, n) = a.shape, b.shape
    elif mode == "nt":
        (m, k), (n, k2) = a.shape, b.shape
    else:
        (k, m), (k2, n) = a.shape, b.shape
    assert k == k2, (name, a.shape, b.shape)
    tm, tn, tk = min(MM_TILE, m), min(MM_TILE, n), min(MM_TILE, k)
    assert m % tm == 0 and n % tn == 0 and k % tk == 0, (name, m, n, k)
    dot = {"nn": _d_nn, "nt": _d_nt, "tn": _d_tn}[mode]

    def body(a_ref, b_ref, o_ref):
        kk = pl.program_id(2)
        part = dot(a_ref[...], b_ref[...])

        @pl.when(kk == 0)
        def _():
            o_ref[...] = part

        @pl.when(kk != 0)
        def _():
            o_ref[...] += part

    if mode == "nn":
        a_spec = pl.BlockSpec((tm, tk), lambda j, i, kk: (i, kk))
        b_spec = pl.BlockSpec((tk, tn), lambda j, i, kk: (kk, j))
    elif mode == "nt":
        a_spec = pl.BlockSpec((tm, tk), lambda j, i, kk: (i, kk))
        b_spec = pl.BlockSpec((tn, tk), lambda j, i, kk: (j, kk))
    else:
        a_spec = pl.BlockSpec((tk, tm), lambda j, i, kk: (kk, i))
        b_spec = pl.BlockSpec((tk, tn), lambda j, i, kk: (kk, j))
    return pl.pallas_call(
        body,
        name=name,
        grid=(n // tn, m // tm, k // tk),
        in_specs=[a_spec, b_spec],
        out_specs=pl.BlockSpec((tm, tn), lambda j, i, kk: (i, j)),
        out_shape=jax.ShapeDtypeStruct((m, n), F32),
        compiler_params=_cparams("parallel", "parallel", "arbitrary"),
    )(a, b)


def rowwise(name, fn, t_len, tile, row_in, full_in, row_out, acc_out=(), carries=(), reverse=False):
    tile = min(tile, t_len)
    n = t_len // tile
    assert t_len % tile == 0 and tile % SUBLANE == 0
    n_in, n_ro, n_acc, n_car = len(row_in) + len(full_in), len(row_out), len(acc_out), len(carries)

    def ti(i):
        return (n - 1 - i) if reverse else i

    in_specs, args = [], []
    for arr, kind in row_in:
        if kind[0] == "r":
            in_specs.append(pl.BlockSpec((tile, kind[1]), lambda i, c=kind[2]: (ti(i), c)))
        elif kind[0] == "h":
            in_specs.append(pl.BlockSpec((arr.shape[0], tile, arr.shape[2]), lambda i: (0, ti(i), 0)))
        else:
            in_specs.append(pl.BlockSpec(
                (SUBLANE, kind[1]), lambda i, c=kind[2]: (jnp.maximum(ti(i) * (tile // SUBLANE) - 1, 0), c)))
        args.append(arr)
    for arr in full_in:
        in_specs.append(pl.BlockSpec(arr.shape, lambda i, nd=arr.ndim: (0,) * nd))
        args.append(arr)
    out_specs, out_shape = [], []
    for kind in row_out:
        if kind[0] == "r":
            out_specs.append(pl.BlockSpec((tile, kind[1]), lambda i: (ti(i), 0)))
            out_shape.append(jax.ShapeDtypeStruct((t_len, kind[1]), kind[2]))
        else:
            out_specs.append(pl.BlockSpec((kind[1], tile, kind[2]), lambda i: (0, ti(i), 0)))
            out_shape.append(jax.ShapeDtypeStruct((kind[1], t_len, kind[2]), kind[3]))
    for shp in acc_out:
        out_specs.append(pl.BlockSpec(shp, lambda i, nd=len(shp): (0,) * nd))
        out_shape.append(jax.ShapeDtypeStruct(shp, F32))

    def body(*refs):
        in_refs = refs[:n_in]
        ro_refs = refs[n_in:n_in + n_ro]
        acc_refs = refs[n_in + n_ro:n_in + n_ro + n_acc]
        car_refs = refs[n_in + n_ro + n_acc:]
        step = pl.program_id(0)
        if n_car:
            @pl.when(step == 0)
            def _():
                for r in car_refs:
                    r[...] = jnp.zeros_like(r)
        vals = [r[...].astype(F32) for r in in_refs] + [r[...] for r in car_refs]
        outs = fn(ti(step), *vals)
        assert len(outs) == n_ro + n_acc + n_car, (name, len(outs))
        for r, o in zip(ro_refs, outs[:n_ro]):
            r[...] = o.astype(r.dtype)
        for r, o in zip(acc_refs, outs[n_ro:n_ro + n_acc]):
            @pl.when(step == 0)
            def _(r=r, o=o):
                r[...] = o

            @pl.when(step != 0)
            def _(r=r, o=o):
                r[...] += o
        for r, o in zip(car_refs, outs[n_ro + n_acc:]):
            r[...] = o

    res = pl.pallas_call(
        body,
        name=name,
        grid=(n,),
        in_specs=in_specs,
        out_specs=out_specs,
        out_shape=out_shape,
        scratch_shapes=[pltpu.VMEM(s, F32) for s in carries],
        compiler_params=_cparams("arbitrary"),
    )(*args)
    return list(res)


def _vjp_fn(fn, n_diff, n_out):
    def g(i, *a):
        ins, cts = a[:len(a) - n_out], a[len(a) - n_out:]
        diff, rest = ins[:n_diff], ins[n_diff:]
        _, pull = jax.vjp(lambda *d: tuple(fn(i, *d, *rest)), *diff)
        return tuple(pull(tuple(cts)))

    return g


def f_rms_x(i, x, gain):
    return (_rms(x, gain),)


def f_lat(i, cq, ckv, gq, gkv):
    return _rms(cq, gq), _rms(ckv, gkv)


def _rope_tables(pos, invf):
    ang = pos * invf
    lane = lax.broadcasted_iota(jnp.int32, (1, LANE), 1)
    cosv, sinv = jnp.cos(ang), jnp.sin(ang)
    half = ROPE // 2
    c = jnp.where(lane < ROPE, cosv, 0.0)
    sa = jnp.where(lane < half, -sinv, 0.0)
    sb = jnp.where((lane >= half) & (lane < ROPE), sinv, 0.0)
    return c, sa, sb


def _rope(xh, tabs):
    c, sa, sb = tabs
    half = ROPE // 2
    return xh * c + _roll(xh, LANE - half, 1) * sa + _roll(xh, half, 1) * sb


def f_head(i, q_raw, kv_raw, kr, qg, kg, pos, invf):
    tabs = _rope_tables(pos, invf)
    qs, ks, vs = [], [], []
    kr_ss = jnp.sum(kr * kr, axis=-1, keepdims=True)
    for h in range(HEADS):
        lo = q_raw[:, HEAD_PAD * h:HEAD_PAD * h + NOPE]
        hi = q_raw[:, HEAD_PAD * h + NOPE:HEAD_PAD * (h + 1)]
        ss = jnp.sum(lo * lo, axis=-1, keepdims=True) + jnp.sum(hi * hi, axis=-1, keepdims=True)
        r = lax.rsqrt(ss * (1.0 / QK) + EPS)
        qs.append(jnp.concatenate([lo * r * qg[:, :NOPE], _rope(hi * r * qg[:, NOPE:], tabs)], axis=1))
        lo = kv_raw[:, 2 * NOPE * h:2 * NOPE * h + NOPE]
        ss = jnp.sum(lo * lo, axis=-1, keepdims=True) + kr_ss
        r = lax.rsqrt(ss * (1.0 / QK) + EPS)
        ks.append(jnp.concatenate([lo * r * kg[:, :NOPE], _rope(kr * r * kg[:, NOPE:], tabs)], axis=1))
        vs.append(kv_raw[:, 2 * NOPE * h + NOPE:2 * NOPE * (h + 1)])
    return jnp.stack(qs), jnp.stack(ks), jnp.stack(vs)


def f_mix(i, o_mla, mgate, o_gdn, ggate, og):
    parts = [o_mla * jax.nn.silu(mgate)]
    for h in range(HEADS):
        parts.append(_rms(o_gdn[h], og) * jax.nn.silu(ggate[:, LANE * h:LANE * (h + 1)]))
    return (jnp.concatenate(parts, axis=1),)


def _row(a, j):
    rows = lax.broadcasted_iota(jnp.int32, a.shape, 0)
    return jnp.sum(jnp.where(rows == j, a, 0.0), axis=0, keepdims=True)


def _shift_rows(x, halo, d):
    xs = _roll(x, d, 0)
    hs = _roll(halo, d, 0)
    r8 = lax.broadcasted_iota(jnp.int32, hs.shape, 0)
    top = jnp.where(r8 < d, hs, xs[:SUBLANE])
    return jnp.concatenate([top, xs[SUBLANE:]], axis=0)


def _conv_silu(x, halo, w):
    y = _row(w, CONV_W - 1) * x
    for j in range(CONV_W - 1):
        y = y + _row(w, j) * _shift_rows(x, halo, CONV_W - 1 - j)
    return jax.nn.silu(y)


def _head_select(offset):
    r = lax.broadcasted_iota(jnp.int32, (LANE, WIDTH), 0)
    c = lax.broadcasted_iota(jnp.int32, (LANE, WIDTH), 1)
    return (r == offset + lax.shift_right_logical(c, 7)).astype(F32)


def _row_to_heads(row, sel):
    return jnp.sum(_hi(jnp.broadcast_to(row, (SUBLANE, LANE)), sel), axis=0, keepdims=True) * (1.0 / SUBLANE)


def f_gdn_pre(i, gq, gk, gv, gab, hq, hk, hv, cwq, cwk, cwv, alog, dtb):
    live = jnp.where(i == 0, 0.0, 1.0)
    q = _conv_silu(gq, hq * live, cwq)
    k = _conv_silu(gk, hk * live, cwk)
    v = _conv_silu(gv, hv * live, cwv)
    sel_a, sel_b = _head_select(0), _head_select(HEADS)
    ga = _hi(gab, sel_a)
    gb = _hi(gab, sel_b)
    g = -jnp.exp(_row_to_heads(alog, sel_a)) * jax.nn.softplus(ga + _row_to_heads(dtb, sel_a))
    beta = jax.nn.sigmoid(gb)
    qs, ks, vs, gs, bs = [], [], [], [], []
    for h in range(HEADS):
        sl = slice(LANE * h, LANE * (h + 1))
        qh, kh = q[:, sl], k[:, sl]
        qs.append(qh * lax.rsqrt(jnp.sum(qh * qh, axis=-1, keepdims=True) + EPS) * (GDN_DIM ** -0.5))
        ks.append(kh * lax.rsqrt(jnp.sum(kh * kh, axis=-1, keepdims=True) + EPS))
        vs.append(v[:, sl])
        gs.append(g[:, sl])
        bs.append(beta[:, sl])
    return jnp.stack(qs), jnp.stack(ks), jnp.stack(vs), jnp.stack(gs), jnp.stack(bs)


def gdn_pre_bwd(i, gq, gk, gv, gab, hq, hk, hv, dq, dk, dv, dg, db, cwq, cwk, cwv, alog, dtb, cq, ck, cv):
    grads = _vjp_fn(f_gdn_pre, 12, 5)(i, gq, gk, gv, gab, hq, hk, hv, cwq, cwk, cwv, alog, dtb, dq, dk, dv, dg, db)
    dgq, dgk, dgv, dgab, dhq, dhk, dhv, dcwq, dcwk, dcwv, dalog, ddtb = grads

    def add_tail(dx, carry):
        return jnp.concatenate([dx[:-SUBLANE], dx[-SUBLANE:] + carry], axis=0)

    return (add_tail(dgq, cq), add_tail(dgk, ck), add_tail(dgv, cv), dgab,
            dcwq, dcwk, dcwv, dalog, ddtb, dhq, dhk, dhv)


def f_loss(i, x, h, tgt):
    e = x + h - tgt
    part = 0.5 * jnp.sum(e * e) * (1.0 / D_MODEL)
    dy = e * (1.0 / D_MODEL)
    return dy, dy, jnp.zeros((SUBLANE, LANE), F32) + part


def f_delta(i, o, do):
    return (jnp.stack([jnp.sum(o[:, LANE * h:LANE * (h + 1)] * do[:, LANE * h:LANE * (h + 1)], axis=-1, keepdims=True)
                       for h in range(HEADS)]),)


def _flash_tile(t_len):
    return min(512, t_len)


FLASH_HEADS = 4
FLASH_BWD_HEADS = 2
LOG2E = 1.4426950408889634


def _tri_pairs(n, key_major):
    if key_major:
        pairs = [(i, j) for j in range(n) for i in range(j, n)]
    else:
        pairs = [(i, j) for i in range(n) for j in range(i + 1)]
    return (jnp.array([p[0] for p in pairs], jnp.int32), jnp.array([p[1] for p in pairs], jnp.int32))


def _causal(rows0, shape):
    r = rows0 + lax.broadcasted_iota(jnp.int32, shape, 0)
    c = lax.broadcasted_iota(jnp.int32, shape, 1)
    return c <= r


def flash_fwd(q, k, v):
    h_n, t_len, _ = q.shape
    tq = _flash_tile(t_len)
    nq = t_len // tq
    hb = FLASH_HEADS
    c2 = (QK ** -0.5) * LOG2E
    qt, kt = _tri_pairs(nq, key_major=False)

    def body(qt_ref, kt_ref, q_ref, k_ref, v_ref, o_ref, lse_ref, m_s, acc_s):
        step = pl.program_id(1)
        qi, kj = qt_ref[step], kt_ref[step]

        @pl.when(kj == 0)
        def _():
            m_s[...] = jnp.full_like(m_s, -jnp.inf)
            acc_s[...] = jnp.zeros_like(acc_s)

        def tile(diagonal):
            s = _bdot(q_ref[...], k_ref[...], _NT3, False) * c2
            if diagonal:
                s = jnp.where(_causal(0, (tq, tq))[None], s, -jnp.inf)
            m_old = m_s[...]
            m_new = jnp.maximum(m_old, jnp.max(s, axis=-1, keepdims=True))
            p = jnp.exp2(s - m_new).astype(_BF)
            v_ones = jnp.concatenate([v_ref[...], jnp.ones((hb, tq, LANE), _BF)], axis=2)
            acc_s[...] = jnp.exp2(m_old - m_new) * acc_s[...] + _bdot(p, v_ones, _NN3, False)
            m_s[...] = m_new

        @pl.when(kj < qi)
        def _():
            tile(False)

        @pl.when(kj == qi)
        def _():
            tile(True)
            acc = acc_s[...]
            l_sum = acc[:, :, LANE:]
            o = acc[:, :, :LANE] / l_sum
            for hh in range(hb):
                o_ref[:, LANE * hh:LANE * (hh + 1)] = o[hh]
            lse_ref[...] = m_s[...] + jnp.log2(jnp.max(l_sum, axis=-1, keepdims=True))

    return pl.pallas_call(
        body,
        name="flash_fwd",
        grid_spec=pltpu.PrefetchScalarGridSpec(
            num_scalar_prefetch=2,
            grid=(h_n // hb, qt.shape[0]),
            in_specs=[
                pl.BlockSpec((hb, tq, HEAD_PAD), lambda h, s, qt_ref, kt_ref: (h, qt_ref[s], 0)),
                pl.BlockSpec((hb, tq, HEAD_PAD), lambda h, s, qt_ref, kt_ref: (h, kt_ref[s], 0)),
                pl.BlockSpec((hb, tq, LANE), lambda h, s, qt_ref, kt_ref: (h, kt_ref[s], 0)),
            ],
            out_specs=[
                pl.BlockSpec((tq, hb * LANE), lambda h, s, qt_ref, kt_ref: (qt_ref[s], h)),
                pl.BlockSpec((hb, tq, 1), lambda h, s, qt_ref, kt_ref: (h, qt_ref[s], 0)),
            ],
            scratch_shapes=[pltpu.VMEM((hb, tq, 1), F32), pltpu.VMEM((hb, tq, 2 * LANE), F32)],
        ),
        out_shape=[jax.ShapeDtypeStruct((t_len, h_n * LANE), F32), jax.ShapeDtypeStruct((h_n, t_len, 1), F32)],
        compiler_params=_cparams("parallel", "arbitrary"),
    )(qt, kt, q, k, v)


def flash_bwd(q, k, v, do, lse, delta):
    h_n, t_len, _ = q.shape
    tq = _flash_tile(t_len)
    nq = t_len // tq
    hb = FLASH_BWD_HEADS
    n_steps = nq * (nq + 1) // 2
    scale = QK ** -0.5
    c2 = scale * LOG2E
    qt, kt = _tri_pairs(nq, key_major=True)

    def body(qt_ref, kt_ref, q_ref, k_ref, v_ref, do_ref, lse_ref, dl_ref, dq_hbm, dk_ref, dv_ref, dq_s, dq_sem):
        group, step = pl.program_id(0), pl.program_id(1)
        qi, kj = qt_ref[step], kt_ref[step]

        @pl.when(step == 0)
        def _():
            dq_s[...] = jnp.zeros_like(dq_s)

        def tile(diagonal):
            qb, kb = q_ref[...], k_ref[...]
            dob = jnp.stack([do_ref[:, LANE * hh:LANE * (hh + 1)] for hh in range(hb)])
            p = jnp.exp2(_bdot(qb, kb, _NT3, False) * c2 - lse_ref[...])
            if diagonal:
                p = jnp.where(_causal(0, (tq, tq))[None], p, 0.0)
            dv = _bdot(p, dob, _TN3, False)
            ds = p * (_bdot(dob, v_ref[...], _NT3, False) - dl_ref[...]) * scale
            dk = _bdot(ds, qb, _TN3, False)
            dq_s[:, pl.ds(pl.multiple_of(qi * tq, tq), tq), :] += _bdot(ds, kb, _NN3, False)
            return dk, dv

        @pl.when(kj == qi)
        def _():
            dk_ref[...], dv_ref[...] = tile(True)

        @pl.when(kj < qi)
        def _():
            dk, dv = tile(False)
            dk_ref[...] += dk
            dv_ref[...] += dv

        @pl.when(step == n_steps - 1)
        def _():
            out = pltpu.make_async_copy(dq_s, dq_hbm.at[pl.ds(group * hb, hb)], dq_sem)
            out.start()
            out.wait()

    def qmap(h, s, qt_ref, kt_ref):
        return (h, qt_ref[s], 0)

    def kmap(h, s, qt_ref, kt_ref):
        return (h, kt_ref[s], 0)

    return pl.pallas_call(
        body,
        name="flash_bwd",
        grid_spec=pltpu.PrefetchScalarGridSpec(
            num_scalar_prefetch=2,
            grid=(h_n // hb, n_steps),
            in_specs=[
                pl.BlockSpec((hb, tq, HEAD_PAD), qmap),
                pl.BlockSpec((hb, tq, HEAD_PAD), kmap),
                pl.BlockSpec((hb, tq, LANE), kmap),
                pl.BlockSpec((tq, hb * LANE), lambda h, s, qt_ref, kt_ref: (qt_ref[s], h)),
                pl.BlockSpec((hb, tq, 1), qmap),
                pl.BlockSpec((hb, tq, 1), qmap),
            ],
            out_specs=[
                pl.BlockSpec(memory_space=pl.ANY),
                pl.BlockSpec((hb, tq, HEAD_PAD), kmap),
                pl.BlockSpec((hb, tq, LANE), kmap),
            ],
            scratch_shapes=[pltpu.VMEM((hb, t_len, HEAD_PAD), F32), pltpu.SemaphoreType.DMA],
        ),
        out_shape=[
            jax.ShapeDtypeStruct((h_n, t_len, HEAD_PAD), F32),
            jax.ShapeDtypeStruct((h_n, t_len, HEAD_PAD), F32),
            jax.ShapeDtypeStruct((h_n, t_len, LANE), F32),
        ],
        compiler_params=_cparams("parallel", "arbitrary"),
    )(qt, kt, q, k, v, do, lse, delta)


def gdn_step(s, q, k, v, gb, bb):
    h_n, c = q.shape[0], CHUNK
    ii = lax.broadcasted_iota(jnp.int32, (1, c, c), 1)
    jj = lax.broadcasted_iota(jnp.int32, (1, c, c), 2)
    incl, strict = ii >= jj, ii > jj
    gcb = _bhi(jnp.broadcast_to(incl.astype(F32), (h_n, c, c)), gb)
    lane = lax.broadcasted_iota(jnp.int32, (1, 1, LANE), 2)
    e0, e1 = (lane == 0).astype(F32), (lane == 1).astype(F32)
    diff = _bhi_nt(gcb * e0 + e1, e0 - gcb * e1)
    decay = jnp.where(incl, jnp.exp(jnp.where(incl, diff, 0.0)), 0.0)
    kb, vb = k * bb, v * bb
    egc = jnp.exp(gcb)
    lmat = jnp.where(strict, _bmm_nt(kb, k) * decay, 0.0)
    inv = (ii == jj).astype(F32) - lmat
    pw = _bh3(lmat, lmat)
    for step in range(5):
        inv = inv + _bh3(inv, pw)
        if step < 4:
            pw = _bh3(pw, pw)
    u = _bh3(inv, vb)
    w = _bh3(inv, kb * egc)
    attn = _bmm_nt(q, k) * decay
    qd = q * egc
    g_end = jnp.sum(gb, axis=1, keepdims=True)
    kd = k * jnp.exp(g_end - gcb)
    v_new = u - _bmm(w, s)
    o = _bmm(qd, s) + _bmm(attn, v_new)
    s_new = s * jnp.exp(g_end) + _bmm_tn(kd, v_new)
    return s_new, o


def gdn_fwd(q, k, v, gb, bb):
    h_n, t_len, d = q.shape
    n = t_len // CHUNK
    blk = pl.BlockSpec((h_n, CHUNK, d), lambda i: (0, i, 0))

    def body(q_ref, k_ref, v_ref, g_ref, b_ref, o_ref, sall_ref, s_s):
        @pl.when(pl.program_id(0) == 0)
        def _():
            s_s[...] = jnp.zeros_like(s_s)

        s = s_s[...]
        sall_ref[0] = s
        s_new, o = gdn_step(s, q_ref[...], k_ref[...], v_ref[...], g_ref[...], b_ref[...])
        o_ref[...] = o
        s_s[...] = s_new

    return pl.pallas_call(
        body,
        name="gdn_fwd",
        grid=(n,),
        in_specs=[blk] * 5,
        out_specs=[blk, pl.BlockSpec((1, h_n, d, d), lambda i: (i, 0, 0, 0))],
        out_shape=[jax.ShapeDtypeStruct((h_n, t_len, d), F32), jax.ShapeDtypeStruct((n, h_n, d, d), F32)],
        scratch_shapes=[pltpu.VMEM((h_n, d, d), F32)],
        compiler_params=_cparams("arbitrary"),
    )(q, k, v, gb, bb)


def gdn_bwd(q, k, v, gb, bb, s_all, do):
    h_n, t_len, d = q.shape
    n = t_len // CHUNK
    blk = pl.BlockSpec((h_n, CHUNK, d), lambda i: (0, n - 1 - i, 0))

    def body(q_ref, k_ref, v_ref, g_ref, b_ref, sall_ref, do_ref, dq_ref, dk_ref, dv_ref, dg_ref, db_ref, ds_s):
        @pl.when(pl.program_id(0) == 0)
        def _():
            ds_s[...] = jnp.zeros_like(ds_s)

        _, pull = jax.vjp(gdn_step, sall_ref[0], q_ref[...], k_ref[...], v_ref[...], g_ref[...], b_ref[...])
        ds, dq, dk, dv, dg, db = pull((ds_s[...], do_ref[...]))
        ds_s[...] = ds
        dq_ref[...], dk_ref[...], dv_ref[...], dg_ref[...], db_ref[...] = dq, dk, dv, dg, db

    return pl.pallas_call(
        body,
        name="gdn_bwd",
        grid=(n,),
        in_specs=[blk] * 5 + [pl.BlockSpec((1, h_n, d, d), lambda i: (n - 1 - i, 0, 0, 0)), blk],
        out_specs=[blk] * 5,
        out_shape=[jax.ShapeDtypeStruct((h_n, t_len, d), F32)] * 5,
        scratch_shapes=[pltpu.VMEM((h_n, d, d), F32)],
        compiler_params=_cparams("arbitrary"),
    )(q, k, v, gb, bb, s_all, do)


def _pad_cols(a, n):
    return jnp.pad(a, ((0, 0), (0, n - a.shape[1])))


def arrange_w_in(w):
    pieces, start = [], 0
    for n in R_SPLITS:
        pieces.append(w[:, start:start + n])
        start += n
    cq, ckv, kr, mgate, gq, gk, gv, ga, gb, ggate = pieces
    return jnp.concatenate([mgate, gq, gk, gv, ggate, cq, ckv, _pad_cols(kr, LANE),
                            _pad_cols(jnp.concatenate([ga, gb], axis=1), LANE)], axis=1)


def unarrange_w_in(g):
    def cols(start, n):
        return g[:, start:start + n]
    return jnp.concatenate([cols(P_CQ, Q_LORA), cols(P_CKV, KV_LORA), cols(P_KR, ROPE), cols(P_MGATE, WIDTH),
                            cols(P_GQ, WIDTH), cols(P_GK, WIDTH), cols(P_GV, WIDTH), cols(P_GAB, HEADS),
                            cols(P_GAB + HEADS, HEADS), cols(P_GGATE, WIDTH)], axis=1)


def arrange_w_uq(w):
    w = w.reshape(w.shape[0], HEADS, QK)
    return jnp.pad(w, ((0, 0), (0, 0), (0, HEAD_PAD - QK))).reshape(w.shape[0], HEADS * HEAD_PAD)


def unarrange_w_uq(g):
    return g.reshape(g.shape[0], HEADS, HEAD_PAD)[:, :, :QK].reshape(g.shape[0], HEADS * QK)


def local_step(x, pos, tgt, p):
    t_len = x.shape[0]
    w_in, w_uq, w_ukv, w_out = p["w_in"], p["w_uq"], p["w_ukv"], p["w_out"]
    norm_gain = p["norm_gain"].reshape(1, D_MODEL)
    qa_gain = p["mla_q_a_gain"].reshape(1, Q_LORA)
    kva_gain = p["mla_kv_a_gain"].reshape(1, KV_LORA)
    qg = _pad_cols(p["mla_q_norm_gain"].reshape(1, QK), HEAD_PAD)
    kg = _pad_cols(p["mla_k_norm_gain"].reshape(1, QK), HEAD_PAD)
    cw = p["gdn_conv_w"].reshape(CONV_W, 3 * WIDTH)
    cwq, cwk, cwv = cw[:, :WIDTH], cw[:, WIDTH:2 * WIDTH], cw[:, 2 * WIDTH:]
    alog = _pad_cols(p["gdn_a_log"].reshape(1, HEADS), LANE)
    dtb = _pad_cols(p["gdn_dt_bias"].reshape(1, HEADS), LANE)
    og = p["gdn_out_norm_gain"].reshape(1, GDN_DIM)
    half = ROPE // 2
    inv_freq = jnp.power(ROPE_THETA, -jnp.arange(half, dtype=F32) / half)
    invf = _pad_cols(jnp.concatenate([inv_freq, inv_freq]).reshape(1, ROPE), LANE)

    rt = 256
    r = "r"
    (xn,) = rowwise("rms_x", f_rms_x, t_len, rt, [(x, (r, D_MODEL, 0))], [norm_gain], [(r, D_MODEL, _BF)])
    proj = matmul("proj", xn, w_in, "nn")
    cq_in = (proj, (r, Q_LORA, P_CQ // Q_LORA))
    ckv_in = (proj, (r, KV_LORA, P_CKV // KV_LORA))
    kr_in = (proj, (r, LANE, P_KR // LANE))
    mgate_in = (proj, (r, WIDTH, P_MGATE // WIDTH))
    ggate_in = (proj, (r, WIDTH, P_GGATE // WIDTH))
    gqkv_in = [(proj, (r, WIDTH, P_GQ // WIDTH)), (proj, (r, WIDTH, P_GK // WIDTH)), (proj, (r, WIDTH, P_GV // WIDTH))]
    gab_in = (proj, (r, LANE, P_GAB // LANE))
    halos = [(proj, ("halo", WIDTH, P_GQ // WIDTH)), (proj, ("halo", WIDTH, P_GK // WIDTH)),
             (proj, ("halo", WIDTH, P_GV // WIDTH))]

    q_lat, kv_lat = rowwise("lat", f_lat, t_len, rt, [cq_in, ckv_in], [qa_gain, kva_gain],
                            [(r, Q_LORA, _BF), (r, KV_LORA, _BF)])
    q_raw = matmul("q_up", q_lat, w_uq, "nn")
    kv_raw = matmul("kv_up", kv_lat, w_ukv, "nn")
    wide = HEADS * HEAD_PAD
    head_in = [(q_raw, (r, wide, 0)), (kv_raw, (r, wide, 0)), kr_in]
    pos_in = (pos, (r, 1, 0))
    q_full, k_full, v_mla = rowwise(
        "head", lambda i, qr, kvr, kr, ps, qg_, kg_, iv: f_head(i, qr, kvr, kr, qg_, kg_, ps, iv), t_len, rt,
        head_in + [pos_in], [qg, kg, invf],
        [("h", HEADS, HEAD_PAD, _BF), ("h", HEADS, HEAD_PAD, _BF), ("h", HEADS, LANE, _BF)])
    o_mla, lse = flash_fwd(q_full, k_full, v_mla)

    pre_in = gqkv_in + [gab_in] + halos
    pre_full = [cwq, cwk, cwv, alog, dtb]
    hkind = ("h", HEADS, GDN_DIM, F32)
    gq_n, gk_n, gv_n, g_b, b_b = rowwise("gdn_pre", f_gdn_pre, t_len, rt, pre_in, pre_full, [hkind] * 5)
    o_gdn, s_all = gdn_fwd(gq_n, gk_n, gv_n, g_b, b_b)

    mix_in = [(o_mla, (r, WIDTH, 0)), mgate_in, (o_gdn, ("h",)), ggate_in]
    (mixed,) = rowwise("mix", f_mix, t_len, rt, mix_in, [og], [(r, 2 * WIDTH, _BF)])
    h_out = matmul("out_proj", mixed, w_out, "nn")
    dy, dy_mx, loss_acc = rowwise("loss", f_loss, t_len, rt,
                                  [(x, (r, D_MODEL, 0)), (h_out, (r, D_MODEL, 0)), (tgt, (r, D_MODEL, 0))], [],
                                  [(r, D_MODEL, F32), (r, D_MODEL, _BF)], [(SUBLANE, LANE)])
    loss = loss_acc[0, 0]

    d_mixed = matmul("d_mixed", dy_mx, w_out, "nt")
    g_w_out = matmul("g_w_out", mixed, dy_mx, "tn")

    def mix_bwd(i, o_mla_, mgate_, o_gdn_, ggate_, d_mixed_, og_):
        return _vjp_fn(f_mix, 5, 1)(i, o_mla_, mgate_, o_gdn_, ggate_, og_, d_mixed_)

    do_mla, d_mgate, do_gdn, d_ggate, g_og = rowwise(
        "mix_bwd", mix_bwd, t_len, rt, mix_in + [(d_mixed, (r, 2 * WIDTH, 0))], [og],
        [(r, WIDTH, F32), (r, WIDTH, _BF), hkind, (r, WIDTH, _BF)], [(1, GDN_DIM)])
    dq_n, dk_n, dv_n, dg_b, db_b = gdn_bwd(gq_n, gk_n, gv_n, g_b, b_b, s_all, do_gdn)
    cts_in = [(a, ("h",)) for a in (dq_n, dk_n, dv_n, dg_b, db_b)]
    d_gq, d_gk, d_gv, d_gab, g_cwq, g_cwk, g_cwv, g_alog, g_dtb = rowwise(
        "gdn_pre_bwd", gdn_pre_bwd, t_len, rt, pre_in + cts_in, pre_full,
        [(r, WIDTH, _BF)] * 3 + [(r, LANE, _BF)],
        [(CONV_W, WIDTH)] * 3 + [(1, LANE)] * 2, carries=[(SUBLANE, WIDTH)] * 3, reverse=True)

    (delta,) = rowwise("delta", f_delta, t_len, rt, [(o_mla, (r, WIDTH, 0)), (do_mla, (r, WIDTH, 0))], [],
                       [("h", HEADS, 1, F32)])
    dq_full, dk_full, dv_mla = flash_bwd(q_full, k_full, v_mla, do_mla, lse, delta)
    head_cts = [(a, ("h",)) for a in (dq_full, dk_full, dv_mla)]

    def head_bwd(i, q_raw_, kv_raw_, kr_, pos_, dq_, dk_, dv_, qg_, kg_, invf_):
        return _vjp_fn(f_head, 5, 3)(i, q_raw_, kv_raw_, kr_, qg_, kg_, pos_, invf_, dq_, dk_, dv_)

    dq_raw, dkv_raw, d_kr, g_qg, g_kg = rowwise(
        "head_bwd", head_bwd, t_len, rt // 2, head_in + [pos_in] + head_cts, [qg, kg, invf],
        [(r, wide, _BF), (r, wide, _BF), (r, LANE, _BF)], [(1, HEAD_PAD), (1, HEAD_PAD)])
    dq_lat = matmul("dq_lat", dq_raw, w_uq, "nt")
    g_w_uq = matmul("g_w_uq", q_lat, dq_raw, "tn")
    dkv_lat = matmul("dkv_lat", dkv_raw, w_ukv, "nt")
    g_w_ukv = matmul("g_w_ukv", kv_lat, dkv_raw, "tn")

    def lat_bwd(i, cq_, ckv_, dql_, dkl_, gq_, gkv_):
        return _vjp_fn(f_lat, 4, 2)(i, cq_, ckv_, gq_, gkv_, dql_, dkl_)

    d_cq, d_ckv, g_qa, g_kva = rowwise(
        "lat_bwd", lat_bwd, t_len, rt, [cq_in, ckv_in, (dq_lat, (r, Q_LORA, 0)), (dkv_lat, (r, KV_LORA, 0))],
        [qa_gain, kva_gain], [(r, Q_LORA, _BF), (r, KV_LORA, _BF)], [(1, Q_LORA), (1, KV_LORA)])

    d_proj = jnp.concatenate([d_mgate, d_gq, d_gk, d_gv, d_ggate, d_cq, d_ckv, d_kr, d_gab], axis=1)
    d_xn = matmul("d_xn", d_proj, w_in, "nt")
    g_w_in = matmul("g_w_in", xn, d_proj, "tn")

    def rms_x_bwd(i, x_, dxn_, dy_, gain_):
        dx, dgain = _vjp_fn(f_rms_x, 2, 1)(i, x_, gain_, dxn_)
        return dx + dy_, dgain

    grad_x, g_norm = rowwise("rms_x_bwd", rms_x_bwd, t_len, rt,
                             [(x, (r, D_MODEL, 0)), (d_xn, (r, D_MODEL, 0)), (dy, (r, D_MODEL, 0))], [norm_gain],
                             [(r, D_MODEL, F32)], [(1, D_MODEL)])

    grads = {
        "norm_gain": g_norm, "w_in": g_w_in, "mla_q_a_gain": g_qa, "mla_kv_a_gain": g_kva, "w_uq": g_w_uq,
        "w_ukv": g_w_ukv, "mla_q_norm_gain": g_qg[:, :QK], "mla_k_norm_gain": g_kg[:, :QK],
        "gdn_conv_w": jnp.concatenate([g_cwq, g_cwk, g_cwv], axis=1), "gdn_a_log": g_alog[:, :HEADS],
        "gdn_dt_bias": g_dtb[:, :HEADS], "gdn_out_norm_gain": g_og, "w_out": g_w_out,
    }
    return loss, grad_x, grads


MESH = pl.DeviceIdType.MESH
ANY = pl.BlockSpec(memory_space=pl.ANY)
CHIP_FLIPS = ((1, 0), (0, 1), (1, 1))


def _place():
    return lax.axis_index("x"), lax.axis_index("y"), lax.axis_index("c")


def _flip(v, f):
    return 1 - v if f else v


def all_gather(shards):
    n_arr = len(shards)

    def body(*refs):
        x_refs, o_refs = refs[:n_arr], refs[n_arr:2 * n_arr]
        send_sems, recv_sems, local_sems = refs[2 * n_arr:]
        x, y, c = _place()
        me, sibling = (x, y, c), (x, y, 1 - c)
        chips = [(_flip(x, fx), _flip(y, fy)) for fx, fy in CHIP_FLIPS]

        def copy(a, k, block, to, src=None):
            px, py, pc = block
            dst = o_refs[a].at[4 * px + 2 * py + pc]
            return pltpu.make_async_remote_copy(
                src_ref=dst if src is None else src, dst_ref=dst, send_sem=send_sems.at[a, k],
                recv_sem=recv_sems.at[a, k], device_id=to, device_id_type=MESH)

        mine, first, passed = [], [], []
        for a in range(n_arr):
            cp = pltpu.make_async_copy(x_refs[a], o_refs[a].at[4 * x + 2 * y + c], local_sems.at[a])
            cp.start()
            mine.append(cp)
            first.append(copy(a, 0, me, sibling, src=x_refs[a]))
            first += [copy(a, 1 + j, me, (*chip, c), src=x_refs[a]) for j, chip in enumerate(chips)]
        for cp in first:
            cp.start()
        for j, chip in enumerate(chips):
            for a in range(n_arr):
                copy(a, 1 + j, (*chip, c), me).wait_recv()
                cp = copy(a, 4 + j, (*chip, c), sibling)
                cp.start()
                passed.append(cp)
        for a in range(n_arr):
            copy(a, 0, sibling, me).wait_recv()
            for j, chip in enumerate(chips):
                copy(a, 4 + j, (*chip, 1 - c), me).wait_recv()
        for cp in first + passed:
            cp.wait_send()
        for cp in mine:
            cp.wait()

    return pl.pallas_call(
        body,
        name="all_gather",
        out_shape=[jax.ShapeDtypeStruct((N_DEV,) + s.shape, s.dtype) for s in shards],
        in_specs=[ANY] * n_arr,
        out_specs=[ANY] * n_arr,
        scratch_shapes=[pltpu.SemaphoreType.DMA((n_arr, 7)), pltpu.SemaphoreType.DMA((n_arr, 7)),
                        pltpu.SemaphoreType.DMA((n_arr,))],
    )(*shards)


def exchange_cores(grads):
    n_arr = len(grads)

    def body(*refs):
        g_refs, o_refs = refs[:n_arr], refs[n_arr:2 * n_arr]
        send_sems, recv_sems = refs[2 * n_arr:]
        x, y, c = _place()
        copies = []
        for a in range(n_arr):
            for q in range(4):
                cp = pltpu.make_async_remote_copy(
                    src_ref=g_refs[a].at[2 * q + (1 - c)], dst_ref=o_refs[a].at[q], send_sem=send_sems.at[a, q],
                    recv_sem=recv_sems.at[a, q], device_id=(x, y, 1 - c), device_id_type=MESH)
                cp.start()
                copies.append(cp)
        for cp in copies:
            cp.wait()

    return pl.pallas_call(
        body,
        name="exchange_cores",
        out_shape=[jax.ShapeDtypeStruct((4,) + g.shape[1:], g.dtype) for g in grads],
        in_specs=[ANY] * n_arr,
        out_specs=[ANY] * n_arr,
        scratch_shapes=[pltpu.SemaphoreType.DMA((n_arr, 4)), pltpu.SemaphoreType.DMA((n_arr, 4))],
    )(*grads)


def exchange_chips(parts):
    n_arr = len(parts)

    def body(*refs):
        p_refs, o_refs = refs[:n_arr], refs[n_arr:2 * n_arr]
        send_sems, recv_sems = refs[2 * n_arr:]
        x, y, c = _place()
        copies = []
        for a in range(n_arr):
            for j, (fx, fy) in enumerate(CHIP_FLIPS):
                px, py = _flip(x, fx), _flip(y, fy)
                cp = pltpu.make_async_remote_copy(
                    src_ref=p_refs[a].at[2 * px + py], dst_ref=o_refs[a].at[j], send_sem=send_sems.at[a, j],
                    recv_sem=recv_sems.at[a, j], device_id=(px, py, c), device_id_type=MESH)
                cp.start()
                copies.append(cp)
        for cp in copies:
            cp.wait()

    return pl.pallas_call(
        body,
        name="exchange_chips",
        out_shape=[jax.ShapeDtypeStruct((3,) + p.shape[1:], p.dtype) for p in parts],
        in_specs=[ANY] * n_arr,
        out_specs=[ANY] * n_arr,
        scratch_shapes=[pltpu.SemaphoreType.DMA((n_arr, 3)), pltpu.SemaphoreType.DMA((n_arr, 3))],
    )(*parts)


def gather_small(v):
    def body(v_ref, o_ref, send_sems, recv_sems, local_sem):
        x, y, c = _place()
        me = 4 * x + 2 * y + c
        mine = pltpu.make_async_copy(v_ref, o_ref.at[me], local_sem)
        mine.start()
        copies = []
        for k in range(1, N_DEV):
            fx, fy, fc = (k >> 2) & 1, (k >> 1) & 1, k & 1
            cp = pltpu.make_async_remote_copy(
                src_ref=v_ref, dst_ref=o_ref.at[me], send_sem=send_sems.at[k - 1], recv_sem=recv_sems.at[k - 1],
                device_id=(_flip(x, fx), _flip(y, fy), _flip(c, fc)), device_id_type=MESH)
            cp.start()
            copies.append(cp)
        for cp in copies:
            cp.wait()
        mine.wait()

    return pl.pallas_call(
        body,
        name="gather_small",
        out_shape=jax.ShapeDtypeStruct((N_DEV,) + v.shape, v.dtype),
        in_specs=[ANY],
        out_specs=ANY,
        scratch_shapes=[pltpu.SemaphoreType.DMA((N_DEV - 1,)), pltpu.SemaphoreType.DMA((N_DEV - 1,)),
                        pltpu.SemaphoreType.DMA],
    )(v)


def _row_tile(rows):
    for t in (256, 128, 64, 32, 16, 8):
        if rows % t == 0:
            return t
    return rows


def add_core_parts(name, g, recv, c_idx):
    _, rows, cols = g.shape
    tr = _row_tile(rows)

    def body(c_ref, g_ref, r_ref, o_ref):
        o_ref[...] = g_ref[...] + r_ref[...]

    return pl.pallas_call(
        body,
        name=name,
        grid_spec=pltpu.PrefetchScalarGridSpec(
            num_scalar_prefetch=1,
            grid=(4, rows // tr),
            in_specs=[pl.BlockSpec((1, tr, cols), lambda q, i, c_ref: (2 * q + c_ref[0], i, 0)),
                      pl.BlockSpec((1, tr, cols), lambda q, i, c_ref: (q, i, 0))],
            out_specs=pl.BlockSpec((1, tr, cols), lambda q, i, c_ref: (q, i, 0)),
        ),
        out_shape=jax.ShapeDtypeStruct((4, rows, cols), F32),
        compiler_params=_cparams("parallel", "parallel"),
    )(c_idx, g, recv)


def _adamw(w, g, m, v):
    m = ADAM_B1 * m + (1.0 - ADAM_B1) * g
    v = ADAM_B2 * v + (1.0 - ADAM_B2) * (g * g)
    m_hat = m / (1.0 - ADAM_B1 ** ADAM_STEP)
    v_hat = v / (1.0 - ADAM_B2 ** ADAM_STEP)
    delta = -ADAM_LR * (m_hat / (jnp.sqrt(v_hat) + ADAM_EPS) + ADAM_WD * w)
    return delta, m, v


def adamw_sharded(name, parts, recv, q_idx, w, m, v):
    rows, cols = w.shape
    tr = _row_tile(rows)

    def body(q_ref, p_ref, r_ref, w_ref, m_ref, v_ref, g_out, d_out, m_out, v_out):
        g = p_ref[0] + r_ref[0] + r_ref[1] + r_ref[2]
        d, m_new, v_new = _adamw(w_ref[...], g, m_ref[...], v_ref[...])
        g_out[...], d_out[...], m_out[...], v_out[...] = g, d, m_new, v_new

    blk = pl.BlockSpec((tr, cols), lambda i, q_ref: (i, 0))
    return pl.pallas_call(
        body,
        name=name,
        grid_spec=pltpu.PrefetchScalarGridSpec(
            num_scalar_prefetch=1,
            grid=(rows // tr,),
            in_specs=[pl.BlockSpec((1, tr, cols), lambda i, q_ref: (q_ref[0], i, 0)),
                      pl.BlockSpec((3, tr, cols), lambda i, q_ref: (0, i, 0)), blk, blk, blk],
            out_specs=[blk] * 4,
        ),
        out_shape=[jax.ShapeDtypeStruct((rows, cols), F32)] * 4,
        compiler_params=_cparams("parallel"),
    )(q_idx, parts, recv, w, m, v)


def adamw_small(gathered, w, m, v):
    def body(g_ref, w_ref, m_ref, v_ref, g_out, d_out, m_out, v_out):
        g = g_ref[0]
        for j in range(1, N_DEV):
            g = g + g_ref[j]
        d, m_new, v_new = _adamw(w_ref[...], g, m_ref[...], v_ref[...])
        g_out[...], d_out[...], m_out[...], v_out[...] = g, d, m_new, v_new

    return pl.pallas_call(body, name="adamw_small", out_shape=[jax.ShapeDtypeStruct(w.shape, F32)] * 4)(gathered, w, m, v)


SHARDED = ("w_in", "w_uq", "w_ukv", "gdn_conv_w", "w_out")
SMALL = (("norm_gain", D_MODEL), ("mla_q_a_gain", Q_LORA), ("mla_kv_a_gain", KV_LORA), ("mla_q_norm_gain", QK),
         ("mla_k_norm_gain", QK), ("gdn_a_log", HEADS), ("gdn_dt_bias", HEADS), ("gdn_out_norm_gain", GDN_DIM))
WEIGHT_ORDER = ("norm_gain", "w_in", "mla_q_a_gain", "mla_kv_a_gain", "w_uq", "w_ukv", "mla_q_norm_gain",
                "mla_k_norm_gain", "gdn_conv_w", "gdn_a_log", "gdn_dt_bias", "gdn_out_norm_gain", "w_out")


def _pack_small(d):
    rows = []
    for name, n in SMALL:
        a = d[name].reshape(-1).astype(F32)
        n_pad = -(-n // LANE) * LANE
        rows.append(jnp.pad(a, (0, n_pad - n)).reshape(n_pad // LANE, LANE))
    packed = jnp.concatenate(rows, axis=0)
    return jnp.pad(packed, ((0, -packed.shape[0] % SUBLANE), (0, 0)))


def _unpack_small(packed):
    out, row = {}, 0
    for name, n in SMALL:
        n_rows = -(-n // LANE)
        out[name] = packed[row:row + n_rows].reshape(-1)[:n].reshape(1, n)
        row += n_rows
    return out


def kernel(x, positions, norm_gain, w_in, mla_q_a_gain, mla_kv_a_gain, w_uq, w_ukv, mla_q_norm_gain, mla_k_norm_gain, gdn_conv_w, gdn_a_log, gdn_dt_bias, gdn_out_norm_gain, w_out, loss_target, m_norm_gain, m_w_in, m_mla_q_a_gain, m_mla_kv_a_gain, m_w_uq, m_w_ukv, m_mla_q_norm_gain, m_mla_k_norm_gain, m_gdn_conv_w, m_gdn_a_log, m_gdn_dt_bias, m_gdn_out_norm_gain, m_w_out, v_norm_gain, v_w_in, v_mla_q_a_gain, v_mla_kv_a_gain, v_w_uq, v_w_ukv, v_mla_q_norm_gain, v_mla_k_norm_gain, v_gdn_conv_w, v_gdn_a_log, v_gdn_dt_bias, v_gdn_out_norm_gain, v_w_out):
    w = dict(norm_gain=norm_gain, w_in=w_in, mla_q_a_gain=mla_q_a_gain, mla_kv_a_gain=mla_kv_a_gain, w_uq=w_uq,
             w_ukv=w_ukv, mla_q_norm_gain=mla_q_norm_gain, mla_k_norm_gain=mla_k_norm_gain, gdn_conv_w=gdn_conv_w,
             gdn_a_log=gdn_a_log, gdn_dt_bias=gdn_dt_bias, gdn_out_norm_gain=gdn_out_norm_gain, w_out=w_out)
    m = dict(norm_gain=m_norm_gain, w_in=m_w_in, mla_q_a_gain=m_mla_q_a_gain, mla_kv_a_gain=m_mla_kv_a_gain,
             w_uq=m_w_uq, w_ukv=m_w_ukv, mla_q_norm_gain=m_mla_q_norm_gain, mla_k_norm_gain=m_mla_k_norm_gain,
             gdn_conv_w=m_gdn_conv_w, gdn_a_log=m_gdn_a_log, gdn_dt_bias=m_gdn_dt_bias,
             gdn_out_norm_gain=m_gdn_out_norm_gain, w_out=m_w_out)
    v = dict(norm_gain=v_norm_gain, w_in=v_w_in, mla_q_a_gain=v_mla_q_a_gain, mla_kv_a_gain=v_mla_kv_a_gain,
             w_uq=v_w_uq, w_ukv=v_w_ukv, mla_q_norm_gain=v_mla_q_norm_gain, mla_k_norm_gain=v_mla_k_norm_gain,
             gdn_conv_w=v_gdn_conv_w, gdn_a_log=v_gdn_a_log, gdn_dt_bias=v_gdn_dt_bias,
             gdn_out_norm_gain=v_gdn_out_norm_gain, w_out=v_w_out)
    t_len = x.shape[1]

    shards = [w[n][0] if n == "gdn_conv_w" else w[n][0].astype(_BF) for n in SHARDED]
    a_w_in, a_w_uq, a_w_ukv, a_cw, a_w_out = all_gather(shards)

    def cols_whole(g):
        return g.transpose(1, 0, 2).reshape(g.shape[1], N_DEV * g.shape[2])

    p = {n: w[n] for n, _ in SMALL}
    p["w_in"] = arrange_w_in(cols_whole(a_w_in))
    p["w_uq"] = arrange_w_uq(cols_whole(a_w_uq))
    p["w_ukv"] = cols_whole(a_w_ukv)
    p["gdn_conv_w"] = cols_whole(a_cw)
    p["w_out"] = a_w_out.reshape(N_DEV * a_w_out.shape[1], a_w_out.shape[2])

    pos = positions.reshape(t_len, 1).astype(F32)
    loss, grad_x, grads = local_step(x.reshape(t_len, D_MODEL), pos, loss_target.reshape(t_len, D_MODEL), p)
    loss = lax.psum(loss, ("x", "y", "c"))

    def col_blocks(g):
        return g.reshape(g.shape[0], N_DEV, g.shape[1] // N_DEV).transpose(1, 0, 2)

    blocks = [col_blocks(unarrange_w_in(grads["w_in"])), col_blocks(unarrange_w_uq(grads["w_uq"])),
              col_blocks(grads["w_ukv"]), col_blocks(grads["gdn_conv_w"]),
              grads["w_out"].reshape(N_DEV, D_MODEL // N_DEV, D_MODEL)]
    xi, yi, ci = _place()
    c_idx = jnp.reshape(ci, (1,)).astype(jnp.int32)
    q_idx = jnp.reshape(2 * xi + yi, (1,)).astype(jnp.int32)
    from_sibling = exchange_cores(blocks)
    parts = [add_core_parts("add_" + n, g, r, c_idx) for n, g, r in zip(SHARDED, blocks, from_sibling)]
    from_chips = exchange_chips(parts)
    out = {}
    for n, prt, rcv in zip(SHARDED, parts, from_chips):
        shape = w[n].shape
        res = adamw_sharded("adamw_" + n, prt, rcv, q_idx, w[n].reshape(shape[-2:]), m[n].reshape(shape[-2:]),
                            v[n].reshape(shape[-2:]))
        out[n] = [a.reshape(shape) for a in res]

    small_all = gather_small(_pack_small(grads))
    res = adamw_small(small_all, _pack_small(w), _pack_small(m), _pack_small(v))
    unpacked = [_unpack_small(a) for a in res]
    for n, _ in SMALL:
        out[n] = [u[n] for u in unpacked]

    return (loss, grad_x.reshape(x.shape), *[out[n][0] for n in WEIGHT_ORDER], *[out[n][1] for n in WEIGHT_ORDER],
            *[out[n][2] for n in WEIGHT_ORDER], *[out[n][3] for n in WEIGHT_ORDER])
```

```python
import functools

import jax
import jax.numpy as jnp
from jax import lax
from jax.experimental import pallas as pl
from jax.experimental.pallas import tpu as pltpu

F32 = jnp.float32
_BF = jnp.bfloat16
HI = lax.Precision.HIGHEST

D_MODEL = 2048
HEADS = 8
NOPE = 128
ROPE = 64
QK = NOPE + ROPE
Q_LORA = 512
KV_LORA = 256
HEAD_PAD = 256
GDN_DIM = 128
WIDTH = HEADS * 128
CONV_W = 4
CHUNK = 64
ROPE_THETA = 10000.0
EPS = 1e-6
N_DEV = 8
LANE = 128
SUBLANE = 8
VMEM_LIMIT = 48 * 1024 * 1024

ADAM_LR, ADAM_B1, ADAM_B2, ADAM_EPS, ADAM_WD, ADAM_STEP = 0.001, 0.9, 0.999, 1e-08, 0.01, 10

P_MGATE, P_GQ, P_GK, P_GV, P_GGATE = 0, 1024, 2048, 3072, 4096
P_CQ, P_CKV, P_KR, P_GAB = 5120, 5632, 5888, 6016
P_COLS = 6144
R_SPLITS = (512, 256, 64, 1024, 1024, 1024, 1024, 8, 8, 1024)


def _cparams(*sem):
    return pltpu.CompilerParams(dimension_semantics=sem, vmem_limit_bytes=VMEM_LIMIT)


def _d_nn(a, b):
    return jnp.dot(a.astype(_BF), b.astype(_BF), preferred_element_type=F32)


def _d_nt(a, b):
    return lax.dot_general(a.astype(_BF), b.astype(_BF), (((1,), (1,)), ((), ())), preferred_element_type=F32)


def _d_tn(a, b):
    return lax.dot_general(a.astype(_BF), b.astype(_BF), (((0,), (0,)), ((), ())), preferred_element_type=F32)


@jax.custom_vjp
def _mm(a, b):
    return _d_nn(a, b)


_mm.defvjp(lambda a, b: (_d_nn(a, b), (a, b)), lambda r, g: (_d_nt(g, r[1]), _d_tn(r[0], g)))


@jax.custom_vjp
def _mm_nt(a, b):
    return _d_nt(a, b)


_mm_nt.defvjp(lambda a, b: (_d_nt(a, b), (a, b)), lambda r, g: (_d_nn(g, r[1]), _d_tn(g, r[0])))


@jax.custom_vjp
def _mm_tn(a, b):
    return _d_tn(a, b)


_mm_tn.defvjp(lambda a, b: (_d_tn(a, b), (a, b)), lambda r, g: (_d_nt(r[1], g), _d_nn(r[0], g)))


def _hi(a, b):
    return jnp.dot(a, b, preferred_element_type=F32, precision=HI)


_NN3 = (((2,), (1,)), ((0,), (0,)))
_NT3 = (((2,), (2,)), ((0,), (0,)))
_TN3 = (((1,), (1,)), ((0,), (0,)))


def _bdot(a, b, dims, hi):
    if hi:
        return lax.dot_general(a, b, dims, preferred_element_type=F32, precision=hi)
    return lax.dot_general(a.astype(_BF), b.astype(_BF), dims, preferred_element_type=F32)


def _batched_matmuls(hi):
    nn = jax.custom_vjp(lambda a, b: _bdot(a, b, _NN3, hi))
    nt = jax.custom_vjp(lambda a, b: _bdot(a, b, _NT3, hi))
    tn = jax.custom_vjp(lambda a, b: _bdot(a, b, _TN3, hi))
    nn.defvjp(lambda a, b: (_bdot(a, b, _NN3, hi), (a, b)),
              lambda r, g: (_bdot(g, r[1], _NT3, hi), _bdot(r[0], g, _TN3, hi)))
    nt.defvjp(lambda a, b: (_bdot(a, b, _NT3, hi), (a, b)),
              lambda r, g: (_bdot(g, r[1], _NN3, hi), _bdot(g, r[0], _TN3, hi)))
    tn.defvjp(lambda a, b: (_bdot(a, b, _TN3, hi), (a, b)),
              lambda r, g: (_bdot(r[1], g, _NT3, hi), _bdot(r[0], g, _NN3, hi)))
    return nn, nt, tn


_bmm, _bmm_nt, _bmm_tn = _batched_matmuls(False)
_bhi, _bhi_nt, _bhi_tn = _batched_matmuls(HI)


def _split2(x):
    hi = x.astype(_BF)
    return hi, (x - hi.astype(F32)).astype(_BF)


def _pdot(a, b, mode):
    (a_hi, a_lo), (b_hi, b_lo) = _split2(a), _split2(b)
    a_ax, b_ax, dims = {"nn": (2, 1, _NN3), "nt": (2, 2, _NT3), "tn": (1, 1, _TN3)}[mode]
    lhs = jnp.concatenate([a_hi, a_lo, a_hi], axis=a_ax)
    rhs = jnp.concatenate([b_hi, b_hi, b_lo], axis=b_ax)
    return lax.dot_general(lhs, rhs, dims, preferred_element_type=F32)


def _packed_matmuls():
    nn = jax.custom_vjp(lambda a, b: _pdot(a, b, "nn"))
    nn.defvjp(lambda a, b: (_pdot(a, b, "nn"), (a, b)), lambda r, g: (_pdot(g, r[1], "nt"), _pdot(r[0], g, "tn")))
    return nn


_bh3 = _packed_matmuls()


@functools.partial(jax.custom_vjp, nondiff_argnums=(1, 2))
def _roll(x, shift, axis):
    return pltpu.roll(x, shift, axis)


def _roll_fwd(x, shift, axis):
    return pltpu.roll(x, shift, axis), None


def _roll_bwd(shift, axis, _, g):
    n = g.shape[axis]
    return (pltpu.roll(g, (n - shift) % n, axis),)


_roll.defvjp(_roll_fwd, _roll_bwd)


def _rms(x, gain):
    return x * lax.rsqrt(jnp.mean(x * x, axis=-1, keepdims=True) + EPS) * gain


MM_TILE = 1024


def matmul(name, a, b, mode):
    if mode == "nn":
        (m, k), (k2, n) = a.shape, b.shape
    elif mode == "nt":
        (m, k), (n, k2) = a.shape, b.shape
    else:
        (k, m), (k2, n) = a.shape, b.shape
    assert k == k2, (name, a.shape, b.shape)
    tm, tn, tk = min(MM_TILE, m), min(MM_TILE, n), min(MM_TILE, k)
    assert m % tm == 0 and n % tn == 0 and k % tk == 0, (name, m, n, k)
    dot = {"nn": _d_nn, "nt": _d_nt, "tn": _d_tn}[mode]

    def body(a_ref, b_ref, o_ref):
        kk = pl.program_id(2)
        part = dot(a_ref[...], b_ref[...])

        @pl.when(kk == 0)
        def _():
            o_ref[...] = part

        @pl.when(kk != 0)
        def _():
            o_ref[...] += part

    if mode == "nn":
        a_spec = pl.BlockSpec((tm, tk), lambda j, i, kk: (i, kk))
        b_spec = pl.BlockSpec((tk, tn), lambda j, i, kk: (kk, j))
    elif mode == "nt":
        a_spec = pl.BlockSpec((tm, tk), lambda j, i, kk: (i, kk))
        b_spec = pl.BlockSpec((tn, tk), lambda j, i, kk: (j, kk))
    else:
        a_spec = pl.BlockSpec((tk, tm), lambda j, i, kk: (kk, i))
        b_spec = pl.BlockSpec((tk, tn), lambda j, i, kk: (kk, j))
    return pl.pallas_call(
        body,
        name=name,
        grid=(n // tn, m // tm, k // tk),
        in_specs=[a_spec, b_spec],
        out_specs=pl.BlockSpec((tm, tn), lambda j, i, kk: (i, j)),
        out_shape=jax.ShapeDtypeStruct((m, n), F32),
        compiler_params=_cparams("parallel", "parallel", "arbitrary"),
    )(a, b)


def rowwise(name, fn, t_len, tile, row_in, full_in, row_out, acc_out=(), carries=(), reverse=False):
    tile = min(tile, t_len)
    n = t_len // tile
    assert t_len % tile == 0 and tile % SUBLANE == 0
    n_in, n_ro, n_acc, n_car = len(row_in) + len(full_in), len(row_out), len(acc_out), len(carries)

    def ti(i):
        return (n - 1 - i) if reverse else i

    in_specs, args = [], []
    for arr, kind in row_in:
        if kind[0] == "r":
            in_specs.append(pl.BlockSpec((tile, kind[1]), lambda i, c=kind[2]: (ti(i), c)))
        elif kind[0] == "h":
            in_specs.append(pl.BlockSpec((arr.shape[0], tile, arr.shape[2]), lambda i: (0, ti(i), 0)))
        else:
            in_specs.append(pl.BlockSpec(
                (SUBLANE, kind[1]), lambda i, c=kind[2]: (jnp.maximum(ti(i) * (tile // SUBLANE) - 1, 0), c)))
        args.append(arr)
    for arr in full_in:
        in_specs.append(pl.BlockSpec(arr.shape, lambda i, nd=arr.ndim: (0,) * nd))
        args.append(arr)
    out_specs, out_shape = [], []
    for kind in row_out:
        if kind[0] == "r":
            out_specs.append(pl.BlockSpec((tile, kind[1]), lambda i: (ti(i), 0)))
            out_shape.append(jax.ShapeDtypeStruct((t_len, kind[1]), kind[2]))
        else:
            out_specs.append(pl.BlockSpec((kind[1], tile, kind[2]), lambda i: (0, ti(i), 0)))
            out_shape.append(jax.ShapeDtypeStruct((kind[1], t_len, kind[2]), kind[3]))
    for shp in acc_out:
        out_specs.append(pl.BlockSpec(shp, lambda i, nd=len(shp): (0,) * nd))
        out_shape.append(jax.ShapeDtypeStruct(shp, F32))

    def body(*refs):
        in_refs = refs[:n_in]
        ro_refs = refs[n_in:n_in + n_ro]
        acc_refs = refs[n_in + n_ro:n_in + n_ro + n_acc]
        car_refs = refs[n_in + n_ro + n_acc:]
        step = pl.program_id(0)
        if n_car:
            @pl.when(step == 0)
            def _():
                for r in car_refs:
                    r[...] = jnp.zeros_like(r)
        vals = [r[...].astype(F32) for r in in_refs] + [r[...] for r in car_refs]
        outs = fn(ti(step), *vals)
        assert len(outs) == n_ro + n_acc + n_car, (name, len(outs))
        for r, o in zip(ro_refs, outs[:n_ro]):
            r[...] = o.astype(r.dtype)
        for r, o in zip(acc_refs, outs[n_ro:n_ro + n_acc]):
            @pl.when(step == 0)
            def _(r=r, o=o):
                r[...] = o

            @pl.when(step != 0)
            def _(r=r, o=o):
                r[...] += o
        for r, o in zip(car_refs, outs[n_ro + n_acc:]):
            r[...] = o

    res = pl.pallas_call(
        body,
        name=name,
        grid=(n,),
        in_specs=in_specs,
        out_specs=out_specs,
        out_shape=out_shape,
        scratch_shapes=[pltpu.VMEM(s, F32) for s in carries],
        compiler_params=_cparams("arbitrary"),
    )(*args)
    return list(res)


def _vjp_fn(fn, n_diff, n_out):
    def g(i, *a):
        ins, cts = a[:len(a) - n_out], a[len(a) - n_out:]
        diff, rest = ins[:n_diff], ins[n_diff:]
        _, pull = jax.vjp(lambda *d: tuple(fn(i, *d, *rest)), *diff)
        return tuple(pull(tuple(cts)))

    return g


def f_rms_x(i, x, gain):
    return (_rms(x, gain),)


def f_lat(i, cq, ckv, gq, gkv):
    return _rms(cq, gq), _rms(ckv, gkv)


def _rope_tables(pos, invf):
    ang = pos * invf
    lane = lax.broadcasted_iota(jnp.int32, (1, LANE), 1)
    cosv, sinv = jnp.cos(ang), jnp.sin(ang)
    half = ROPE // 2
    c = jnp.where(lane < ROPE, cosv, 0.0)
    sa = jnp.where(lane < half, -sinv, 0.0)
    sb = jnp.where((lane >= half) & (lane < ROPE), sinv, 0.0)
    return c, sa, sb


def _rope(xh, tabs):
    c, sa, sb = tabs
    half = ROPE // 2
    return xh * c + _roll(xh, LANE - half, 1) * sa + _roll(xh, half, 1) * sb


def f_head(i, q_raw, kv_raw, kr, qg, kg, pos, invf):
    tabs = _rope_tables(pos, invf)
    qs, ks, vs = [], [], []
    kr_ss = jnp.sum(kr * kr, axis=-1, keepdims=True)
    for h in range(HEADS):
        lo = q_raw[:, HEAD_PAD * h:HEAD_PAD * h + NOPE]
        hi = q_raw[:, HEAD_PAD * h + NOPE:HEAD_PAD * (h + 1)]
        ss = jnp.sum(lo * lo, axis=-1, keepdims=True) + jnp.sum(hi * hi, axis=-1, keepdims=True)
        r = lax.rsqrt(ss * (1.0 / QK) + EPS)
        qs.append(jnp.concatenate([lo * r * qg[:, :NOPE], _rope(hi * r * qg[:, NOPE:], tabs)], axis=1))
        lo = kv_raw[:, 2 * NOPE * h:2 * NOPE * h + NOPE]
        ss = jnp.sum(lo * lo, axis=-1, keepdims=True) + kr_ss
        r = lax.rsqrt(ss * (1.0 / QK) + EPS)
        ks.append(jnp.concatenate([lo * r * kg[:, :NOPE], _rope(kr * r * kg[:, NOPE:], tabs)], axis=1))
        vs.append(kv_raw[:, 2 * NOPE * h + NOPE:2 * NOPE * (h + 1)])
    return jnp.stack(qs), jnp.stack(ks), jnp.stack(vs)


def f_mix(i, o_mla, mgate, o_gdn, ggate, og):
    parts = [o_mla * jax.nn.silu(mgate)]
    for h in range(HEADS):
        parts.append(_rms(o_gdn[h], og) * jax.nn.silu(ggate[:, LANE * h:LANE * (h + 1)]))
    return (jnp.concatenate(parts, axis=1),)


def _row(a, j):
    rows = lax.broadcasted_iota(jnp.int32, a.shape, 0)
    return jnp.sum(jnp.where(rows == j, a, 0.0), axis=0, keepdims=True)


def _shift_rows(x, halo, d):
    xs = _roll(x, d, 0)
    hs = _roll(halo, d, 0)
    r8 = lax.broadcasted_iota(jnp.int32, hs.shape, 0)
    top = jnp.where(r8 < d, hs, xs[:SUBLANE])
    return jnp.concatenate([top, xs[SUBLANE:]], axis=0)


def _conv_silu(x, halo, w):
    y = _row(w, CONV_W - 1) * x
    for j in range(CONV_W - 1):
        y = y + _row(w, j) * _shift_rows(x, halo, CONV_W - 1 - j)
    return jax.nn.silu(y)


def _head_select(offset):
    r = lax.broadcasted_iota(jnp.int32, (LANE, WIDTH), 0)
    c = lax.broadcasted_iota(jnp.int32, (LANE, WIDTH), 1)
    return (r == offset + lax.shift_right_logical(c, 7)).astype(_BF)


def _split3(x):
    x1 = x.astype(_BF)
    r1 = x - x1.astype(F32)
    x2 = r1.astype(_BF)
    return x1, x2, (r1 - x2.astype(F32)).astype(_BF)


@jax.custom_vjp
def _spread(x, sel):
    return _d_nn(jnp.concatenate(_split3(x), axis=1), jnp.concatenate([sel, sel, sel], axis=0))


def _spread_fwd(x, sel):
    return _spread(x, sel), sel


def _spread_bwd(sel, g):
    g1, g2, g3 = _split3(g)
    return _d_nt(g1, sel) + _d_nt(g2, sel) + _d_nt(g3, sel), jnp.zeros_like(sel)


_spread.defvjp(_spread_fwd, _spread_bwd)


def f_gdn_pre(i, gq, gk, gv, gab, hq, hk, hv, cwq, cwk, cwv, alog, dtb):
    live = jnp.where(i == 0, 0.0, 1.0)
    q = _conv_silu(gq, hq * live, cwq)
    k = _conv_silu(gk, hk * live, cwk)
    v = _conv_silu(gv, hv * live, cwv)
    g = _spread(-jnp.exp(alog) * jax.nn.softplus(gab + dtb), _head_select(0))
    beta = _spread(jax.nn.sigmoid(gab), _head_select(HEADS))
    qs, ks, vs, gs, bs = [], [], [], [], []
    for h in range(HEADS):
        sl = slice(LANE * h, LANE * (h + 1))
        qh, kh = q[:, sl], k[:, sl]
        qs.append(qh * lax.rsqrt(jnp.sum(qh * qh, axis=-1, keepdims=True) + EPS) * (GDN_DIM ** -0.5))
        ks.append(kh * lax.rsqrt(jnp.sum(kh * kh, axis=-1, keepdims=True) + EPS))
        vs.append(v[:, sl])
        gs.append(g[:, sl])
        bs.append(beta[:, sl])
    return jnp.stack(qs), jnp.stack(ks), jnp.stack(vs), jnp.stack(gs), jnp.stack(bs)


def gdn_pre_bwd(i, gq, gk, gv, gab, hq, hk, hv, dq, dk, dv, dg, db, cwq, cwk, cwv, alog, dtb, cq, ck, cv):
    grads = _vjp_fn(f_gdn_pre, 12, 5)(i, gq, gk, gv, gab, hq, hk, hv, cwq, cwk, cwv, alog, dtb, dq, dk, dv, dg, db)
    dgq, dgk, dgv, dgab, dhq, dhk, dhv, dcwq, dcwk, dcwv, dalog, ddtb = grads

    def add_tail(dx, carry):
        return jnp.concatenate([dx[:-SUBLANE], dx[-SUBLANE:] + carry], axis=0)

    return (add_tail(dgq, cq), add_tail(dgk, ck), add_tail(dgv, cv), dgab,
            dcwq, dcwk, dcwv, dalog, ddtb, dhq, dhk, dhv)


def f_loss(i, x, h, tgt):
    e = x + h - tgt
    part = 0.5 * jnp.sum(e * e) * (1.0 / D_MODEL)
    dy = e * (1.0 / D_MODEL)
    return dy, dy, jnp.zeros((SUBLANE, LANE), F32) + part


def f_delta(i, o, do):
    return (jnp.stack([jnp.sum(o[:, LANE * h:LANE * (h + 1)] * do[:, LANE * h:LANE * (h + 1)], axis=-1, keepdims=True)
                       for h in range(HEADS)]),)


def _flash_tile(t_len):
    return min(512, t_len)


FLASH_HEADS = 4
FLASH_BWD_HEADS = 2
LOG2E = 1.4426950408889634


def _tri_pairs(n, key_major):
    if key_major:
        pairs = [(i, j) for j in range(n) for i in range(j, n)]
    else:
        pairs = [(i, j) for i in range(n) for j in range(i + 1)]
    return (jnp.array([p[0] for p in pairs], jnp.int32), jnp.array([p[1] for p in pairs], jnp.int32))


def _causal(rows0, shape):
    r = rows0 + lax.broadcasted_iota(jnp.int32, shape, 0)
    c = lax.broadcasted_iota(jnp.int32, shape, 1)
    return c <= r


def flash_fwd(q, k, v):
    h_n, t_len, _ = q.shape
    tq = _flash_tile(t_len)
    nq = t_len // tq
    hb = FLASH_HEADS
    c2 = (QK ** -0.5) * LOG2E
    qt, kt = _tri_pairs(nq, key_major=False)

    def body(qt_ref, kt_ref, q_ref, k_ref, v_ref, o_ref, lse_ref, m_s, acc_s):
        step = pl.program_id(1)
        qi, kj = qt_ref[step], kt_ref[step]

        @pl.when(kj == 0)
        def _():
            m_s[...] = jnp.full_like(m_s, -jnp.inf)
            acc_s[...] = jnp.zeros_like(acc_s)

        def tile(diagonal):
            s = _bdot(q_ref[...], k_ref[...], _NT3, False) * c2
            if diagonal:
                s = jnp.where(_causal(0, (tq, tq))[None], s, -jnp.inf)
            m_old = m_s[...]
            m_new = jnp.maximum(m_old, jnp.max(s, axis=-1, keepdims=True))
            p = jnp.exp2(s - m_new).astype(_BF)
            v_ones = jnp.concatenate([v_ref[...], jnp.ones((hb, tq, LANE), _BF)], axis=2)
            acc_s[...] = jnp.exp2(m_old - m_new) * acc_s[...] + _bdot(p, v_ones, _NN3, False)
            m_s[...] = m_new

        @pl.when(kj < qi)
        def _():
            tile(False)

        @pl.when(kj == qi)
        def _():
            tile(True)
            acc = acc_s[...]
            l_sum = acc[:, :, LANE:]
            o = acc[:, :, :LANE] / l_sum
            for hh in range(hb):
                o_ref[:, LANE * hh:LANE * (hh + 1)] = o[hh]
            lse_ref[...] = m_s[...] + jnp.log2(jnp.max(l_sum, axis=-1, keepdims=True))

    return pl.pallas_call(
        body,
        name="flash_fwd",
        grid_spec=pltpu.PrefetchScalarGridSpec(
            num_scalar_prefetch=2,
            grid=(h_n // hb, qt.shape[0]),
            in_specs=[
                pl.BlockSpec((hb, tq, HEAD_PAD), lambda h, s, qt_ref, kt_ref: (h, qt_ref[s], 0)),
                pl.BlockSpec((hb, tq, HEAD_PAD), lambda h, s, qt_ref, kt_ref: (h, kt_ref[s], 0)),
                pl.BlockSpec((hb, tq, LANE), lambda h, s, qt_ref, kt_ref: (h, kt_ref[s], 0)),
            ],
            out_specs=[
                pl.BlockSpec((tq, hb * LANE), lambda h, s, qt_ref, kt_ref: (qt_ref[s], h)),
                pl.BlockSpec((hb, tq, 1), lambda h, s, qt_ref, kt_ref: (h, qt_ref[s], 0)),
            ],
            scratch_shapes=[pltpu.VMEM((hb, tq, 1), F32), pltpu.VMEM((hb, tq, 2 * LANE), F32)],
        ),
        out_shape=[jax.ShapeDtypeStruct((t_len, h_n * LANE), F32), jax.ShapeDtypeStruct((h_n, t_len, 1), F32)],
        compiler_params=_cparams("parallel", "arbitrary"),
    )(qt, kt, q, k, v)


def flash_bwd(q, k, v, do, lse, delta):
    h_n, t_len, _ = q.shape
    tq = _flash_tile(t_len)
    nq = t_len // tq
    hb = FLASH_BWD_HEADS
    n_steps = nq * (nq + 1) // 2
    scale = QK ** -0.5
    c2 = scale * LOG2E
    qt, kt = _tri_pairs(nq, key_major=True)

    def body(qt_ref, kt_ref, q_ref, k_ref, v_ref, do_ref, lse_ref, dl_ref, dq_hbm, dk_ref, dv_ref, dq_s, dq_sem):
        group, step = pl.program_id(0), pl.program_id(1)
        qi, kj = qt_ref[step], kt_ref[step]

        @pl.when(step == 0)
        def _():
            dq_s[...] = jnp.zeros_like(dq_s)

        def tile(diagonal):
            qb, kb = q_ref[...], k_ref[...]
            dob = jnp.stack([do_ref[:, LANE * hh:LANE * (hh + 1)] for hh in range(hb)])
            p = jnp.exp2(_bdot(qb, kb, _NT3, False) * c2 - lse_ref[...])
            if diagonal:
                p = jnp.where(_causal(0, (tq, tq))[None], p, 0.0)
            dv = _bdot(p, dob, _TN3, False)
            ds = p * (_bdot(dob, v_ref[...], _NT3, False) - dl_ref[...]) * scale
            dk = _bdot(ds, qb, _TN3, False)
            dq_s[:, pl.ds(pl.multiple_of(qi * tq, tq), tq), :] += _bdot(ds, kb, _NN3, False)
            return dk, dv

        @pl.when(kj == qi)
        def _():
            dk_ref[...], dv_ref[...] = tile(True)

        @pl.when(kj < qi)
        def _():
            dk, dv = tile(False)
            dk_ref[...] += dk
            dv_ref[...] += dv

        @pl.when(step == n_steps - 1)
        def _():
            out = pltpu.make_async_copy(dq_s, dq_hbm.at[pl.ds(group * hb, hb)], dq_sem)
            out.start()
            out.wait()

    def qmap(h, s, qt_ref, kt_ref):
        return (h, qt_ref[s], 0)

    def kmap(h, s, qt_ref, kt_ref):
        return (h, kt_ref[s], 0)

    return pl.pallas_call(
        body,
        name="flash_bwd",
        grid_spec=pltpu.PrefetchScalarGridSpec(
            num_scalar_prefetch=2,
            grid=(h_n // hb, n_steps),
            in_specs=[
                pl.BlockSpec((hb, tq, HEAD_PAD), qmap),
                pl.BlockSpec((hb, tq, HEAD_PAD), kmap),
                pl.BlockSpec((hb, tq, LANE), kmap),
                pl.BlockSpec((tq, hb * LANE), lambda h, s, qt_ref, kt_ref: (qt_ref[s], h)),
                pl.BlockSpec((hb, tq, 1), qmap),
                pl.BlockSpec((hb, tq, 1), qmap),
            ],
            out_specs=[
                pl.BlockSpec(memory_space=pl.ANY),
                pl.BlockSpec((hb, tq, HEAD_PAD), kmap),
                pl.BlockSpec((hb, tq, LANE), kmap),
            ],
            scratch_shapes=[pltpu.VMEM((hb, t_len, HEAD_PAD), F32), pltpu.SemaphoreType.DMA],
        ),
        out_shape=[
            jax.ShapeDtypeStruct((h_n, t_len, HEAD_PAD), F32),
            jax.ShapeDtypeStruct((h_n, t_len, HEAD_PAD), F32),
            jax.ShapeDtypeStruct((h_n, t_len, LANE), F32),
        ],
        compiler_params=_cparams("parallel", "arbitrary"),
    )(qt, kt, q, k, v, do, lse, delta)


def gdn_step(s, q, k, v, gb, bb):
    h_n, c = q.shape[0], CHUNK
    ii = lax.broadcasted_iota(jnp.int32, (1, c, c), 1)
    jj = lax.broadcasted_iota(jnp.int32, (1, c, c), 2)
    incl, strict = ii >= jj, ii > jj
    gcb = _bhi(jnp.broadcast_to(incl.astype(F32), (h_n, c, c)), gb)
    lane = lax.broadcasted_iota(jnp.int32, (1, 1, LANE), 2)
    e0, e1 = (lane == 0).astype(F32), (lane == 1).astype(F32)
    diff = _bhi_nt(gcb * e0 + e1, e0 - gcb * e1)
    decay = jnp.where(incl, jnp.exp(jnp.where(incl, diff, 0.0)), 0.0)
    kb, vb = k * bb, v * bb
    egc = jnp.exp(gcb)
    lmat = jnp.where(strict, _bmm_nt(kb, k) * decay, 0.0)
    inv = (ii == jj).astype(F32) - lmat
    pw = _bh3(lmat, lmat)
    for step in range(5):
        inv = inv + _bh3(inv, pw)
        if step < 4:
            pw = _bh3(pw, pw)
    u = _bh3(inv, vb)
    w = _bh3(inv, kb * egc)
    attn = _bmm_nt(q, k) * decay
    qd = q * egc
    g_end = jnp.sum(gb, axis=1, keepdims=True)
    kd = k * jnp.exp(g_end - gcb)
    v_new = u - _bmm(w, s)
    o = _bmm(qd, s) + _bmm(attn, v_new)
    s_new = s * jnp.exp(g_end) + _bmm_tn(kd, v_new)
    return s_new, o


def gdn_fwd(q, k, v, gb, bb):
    h_n, t_len, d = q.shape
    n = t_len // CHUNK
    blk = pl.BlockSpec((h_n, CHUNK, d), lambda i: (0, i, 0))

    def body(q_ref, k_ref, v_ref, g_ref, b_ref, o_ref, sall_ref, s_s):
        @pl.when(pl.program_id(0) == 0)
        def _():
            s_s[...] = jnp.zeros_like(s_s)

        s = s_s[...]
        sall_ref[0] = s
        s_new, o = gdn_step(s, q_ref[...], k_ref[...], v_ref[...], g_ref[...], b_ref[...])
        o_ref[...] = o
        s_s[...] = s_new

    return pl.pallas_call(
        body,
        name="gdn_fwd",
        grid=(n,),
        in_specs=[blk] * 5,
        out_specs=[blk, pl.BlockSpec((1, h_n, d, d), lambda i: (i, 0, 0, 0))],
        out_shape=[jax.ShapeDtypeStruct((h_n, t_len, d), F32), jax.ShapeDtypeStruct((n, h_n, d, d), F32)],
        scratch_shapes=[pltpu.VMEM((h_n, d, d), F32)],
        compiler_params=_cparams("arbitrary"),
    )(q, k, v, gb, bb)


def gdn_bwd(q, k, v, gb, bb, s_all, do):
    h_n, t_len, d = q.shape
    n = t_len // CHUNK
    blk = pl.BlockSpec((h_n, CHUNK, d), lambda i: (0, n - 1 - i, 0))

    def body(q_ref, k_ref, v_ref, g_ref, b_ref, sall_ref, do_ref, dq_ref, dk_ref, dv_ref, dg_ref, db_ref, ds_s):
        @pl.when(pl.program_id(0) == 0)
        def _():
            ds_s[...] = jnp.zeros_like(ds_s)

        _, pull = jax.vjp(gdn_step, sall_ref[0], q_ref[...], k_ref[...], v_ref[...], g_ref[...], b_ref[...])
        ds, dq, dk, dv, dg, db = pull((ds_s[...], do_ref[...]))
        ds_s[...] = ds
        dq_ref[...], dk_ref[...], dv_ref[...], dg_ref[...], db_ref[...] = dq, dk, dv, dg, db

    return pl.pallas_call(
        body,
        name="gdn_bwd",
        grid=(n,),
        in_specs=[blk] * 5 + [pl.BlockSpec((1, h_n, d, d), lambda i: (n - 1 - i, 0, 0, 0)), blk],
        out_specs=[blk] * 5,
        out_shape=[jax.ShapeDtypeStruct((h_n, t_len, d), F32)] * 5,
        scratch_shapes=[pltpu.VMEM((h_n, d, d), F32)],
        compiler_params=_cparams("arbitrary"),
    )(q, k, v, gb, bb, s_all, do)


def _pad_cols(a, n):
    return jnp.pad(a, ((0, 0), (0, n - a.shape[1])))


def arrange_w_in(w):
    pieces, start = [], 0
    for n in R_SPLITS:
        pieces.append(w[:, start:start + n])
        start += n
    cq, ckv, kr, mgate, gq, gk, gv, ga, gb, ggate = pieces
    return jnp.concatenate([mgate, gq, gk, gv, ggate, cq, ckv, _pad_cols(kr, LANE),
                            _pad_cols(jnp.concatenate([ga, gb], axis=1), LANE)], axis=1)


def unarrange_w_in(g):
    def cols(start, n):
        return g[:, start:start + n]
    return jnp.concatenate([cols(P_CQ, Q_LORA), cols(P_CKV, KV_LORA), cols(P_KR, ROPE), cols(P_MGATE, WIDTH),
                            cols(P_GQ, WIDTH), cols(P_GK, WIDTH), cols(P_GV, WIDTH), cols(P_GAB, HEADS),
                            cols(P_GAB + HEADS, HEADS), cols(P_GGATE, WIDTH)], axis=1)


def arrange_w_uq(w):
    w = w.reshape(w.shape[0], HEADS, QK)
    return jnp.pad(w, ((0, 0), (0, 0), (0, HEAD_PAD - QK))).reshape(w.shape[0], HEADS * HEAD_PAD)


def unarrange_w_uq(g):
    return g.reshape(g.shape[0], HEADS, HEAD_PAD)[:, :, :QK].reshape(g.shape[0], HEADS * QK)


def local_step(x, pos, tgt, p):
    t_len = x.shape[0]
    w_in, w_uq, w_ukv, w_out = p["w_in"], p["w_uq"], p["w_ukv"], p["w_out"]
    norm_gain = p["norm_gain"].reshape(1, D_MODEL)
    qa_gain = p["mla_q_a_gain"].reshape(1, Q_LORA)
    kva_gain = p["mla_kv_a_gain"].reshape(1, KV_LORA)
    qg = _pad_cols(p["mla_q_norm_gain"].reshape(1, QK), HEAD_PAD)
    kg = _pad_cols(p["mla_k_norm_gain"].reshape(1, QK), HEAD_PAD)
    cw = p["gdn_conv_w"].reshape(CONV_W, 3 * WIDTH)
    cwq, cwk, cwv = cw[:, :WIDTH], cw[:, WIDTH:2 * WIDTH], cw[:, 2 * WIDTH:]
    alog = _pad_cols(p["gdn_a_log"].reshape(1, HEADS), LANE)
    dtb = _pad_cols(p["gdn_dt_bias"].reshape(1, HEADS), LANE)
    og = p["gdn_out_norm_gain"].reshape(1, GDN_DIM)
    half = ROPE // 2
    inv_freq = jnp.power(ROPE_THETA, -jnp.arange(half, dtype=F32) / half)
    invf = _pad_cols(jnp.concatenate([inv_freq, inv_freq]).reshape(1, ROPE), LANE)

    rt = 256
    r = "r"
    (xn,) = rowwise("rms_x", f_rms_x, t_len, rt, [(x, (r, D_MODEL, 0))], [norm_gain], [(r, D_MODEL, _BF)])
    proj = matmul("proj", xn, w_in, "nn")
    cq_in = (proj, (r, Q_LORA, P_CQ // Q_LORA))
    ckv_in = (proj, (r, KV_LORA, P_CKV // KV_LORA))
    kr_in = (proj, (r, LANE, P_KR // LANE))
    mgate_in = (proj, (r, WIDTH, P_MGATE // WIDTH))
    ggate_in = (proj, (r, WIDTH, P_GGATE // WIDTH))
    gqkv_in = [(proj, (r, WIDTH, P_GQ // WIDTH)), (proj, (r, WIDTH, P_GK // WIDTH)), (proj, (r, WIDTH, P_GV // WIDTH))]
    gab_in = (proj, (r, LANE, P_GAB // LANE))
    halos = [(proj, ("halo", WIDTH, P_GQ // WIDTH)), (proj, ("halo", WIDTH, P_GK // WIDTH)),
             (proj, ("halo", WIDTH, P_GV // WIDTH))]

    q_lat, kv_lat = rowwise("lat", f_lat, t_len, rt, [cq_in, ckv_in], [qa_gain, kva_gain],
                            [(r, Q_LORA, _BF), (r, KV_LORA, _BF)])
    q_raw = matmul("q_up", q_lat, w_uq, "nn")
    kv_raw = matmul("kv_up", kv_lat, w_ukv, "nn")
    wide = HEADS * HEAD_PAD
    head_in = [(q_raw, (r, wide, 0)), (kv_raw, (r, wide, 0)), kr_in]
    pos_in = (pos, (r, 1, 0))
    q_full, k_full, v_mla = rowwise(
        "head", lambda i, qr, kvr, kr, ps, qg_, kg_, iv: f_head(i, qr, kvr, kr, qg_, kg_, ps, iv), t_len, rt,
        head_in + [pos_in], [qg, kg, invf],
        [("h", HEADS, HEAD_PAD, _BF), ("h", HEADS, HEAD_PAD, _BF), ("h", HEADS, LANE, _BF)])
    o_mla, lse = flash_fwd(q_full, k_full, v_mla)

    pre_in = gqkv_in + [gab_in] + halos
    pre_full = [cwq, cwk, cwv, alog, dtb]
    hkind = ("h", HEADS, GDN_DIM, F32)
    gq_n, gk_n, gv_n, g_b, b_b = rowwise("gdn_pre", f_gdn_pre, t_len, rt, pre_in, pre_full, [hkind] * 5)
    o_gdn, s_all = gdn_fwd(gq_n, gk_n, gv_n, g_b, b_b)

    mix_in = [(o_mla, (r, WIDTH, 0)), mgate_in, (o_gdn, ("h",)), ggate_in]
    (mixed,) = rowwise("mix", f_mix, t_len, rt, mix_in, [og], [(r, 2 * WIDTH, _BF)])
    h_out = matmul("out_proj", mixed, w_out, "nn")
    dy, dy_mx, loss_acc = rowwise("loss", f_loss, t_len, rt,
                                  [(x, (r, D_MODEL, 0)), (h_out, (r, D_MODEL, 0)), (tgt, (r, D_MODEL, 0))], [],
                                  [(r, D_MODEL, F32), (r, D_MODEL, _BF)], [(SUBLANE, LANE)])
    loss = loss_acc[0, 0]

    d_mixed = matmul("d_mixed", dy_mx, w_out, "nt")
    g_w_out = matmul("g_w_out", mixed, dy_mx, "tn")

    def mix_bwd(i, o_mla_, mgate_, o_gdn_, ggate_, d_mixed_, og_):
        return _vjp_fn(f_mix, 5, 1)(i, o_mla_, mgate_, o_gdn_, ggate_, og_, d_mixed_)

    do_mla, d_mgate, do_gdn, d_ggate, g_og = rowwise(
        "mix_bwd", mix_bwd, t_len, rt, mix_in + [(d_mixed, (r, 2 * WIDTH, 0))], [og],
        [(r, WIDTH, F32), (r, WIDTH, _BF), hkind, (r, WIDTH, _BF)], [(1, GDN_DIM)])
    dq_n, dk_n, dv_n, dg_b, db_b = gdn_bwd(gq_n, gk_n, gv_n, g_b, b_b, s_all, do_gdn)
    cts_in = [(a, ("h",)) for a in (dq_n, dk_n, dv_n, dg_b, db_b)]
    d_gq, d_gk, d_gv, d_gab, g_cwq, g_cwk, g_cwv, g_alog, g_dtb = rowwise(
        "gdn_pre_bwd", gdn_pre_bwd, t_len, rt, pre_in + cts_in, pre_full,
        [(r, WIDTH, _BF)] * 3 + [(r, LANE, _BF)],
        [(CONV_W, WIDTH)] * 3 + [(1, LANE)] * 2, carries=[(SUBLANE, WIDTH)] * 3, reverse=True)

    (delta,) = rowwise("delta", f_delta, t_len, rt, [(o_mla, (r, WIDTH, 0)), (do_mla, (r, WIDTH, 0))], [],
                       [("h", HEADS, 1, F32)])
    dq_full, dk_full, dv_mla = flash_bwd(q_full, k_full, v_mla, do_mla, lse, delta)
    head_cts = [(a, ("h",)) for a in (dq_full, dk_full, dv_mla)]

    def head_bwd(i, q_raw_, kv_raw_, kr_, pos_, dq_, dk_, dv_, qg_, kg_, invf_):
        return _vjp_fn(f_head, 5, 3)(i, q_raw_, kv_raw_, kr_, qg_, kg_, pos_, invf_, dq_, dk_, dv_)

    dq_raw, dkv_raw, d_kr, g_qg, g_kg = rowwise(
        "head_bwd", head_bwd, t_len, rt // 2, head_in + [pos_in] + head_cts, [qg, kg, invf],
        [(r, wide, _BF), (r, wide, _BF), (r, LANE, _BF)], [(1, HEAD_PAD), (1, HEAD_PAD)])
    dq_lat = matmul("dq_lat", dq_raw, w_uq, "nt")
    g_w_uq = matmul("g_w_uq", q_lat, dq_raw, "tn")
    dkv_lat = matmul("dkv_lat", dkv_raw, w_ukv, "nt")
    g_w_ukv = matmul("g_w_ukv", kv_lat, dkv_raw, "tn")

    def lat_bwd(i, cq_, ckv_, dql_, dkl_, gq_, gkv_):
        return _vjp_fn(f_lat, 4, 2)(i, cq_, ckv_, gq_, gkv_, dql_, dkl_)

    d_cq, d_ckv, g_qa, g_kva = rowwise(
        "lat_bwd", lat_bwd, t_len, rt, [cq_in, ckv_in, (dq_lat, (r, Q_LORA, 0)), (dkv_lat, (r, KV_LORA, 0))],
        [qa_gain, kva_gain], [(r, Q_LORA, _BF), (r, KV_LORA, _BF)], [(1, Q_LORA), (1, KV_LORA)])

    d_proj = jnp.concatenate([d_mgate, d_gq, d_gk, d_gv, d_ggate, d_cq, d_ckv, d_kr, d_gab], axis=1)
    d_xn = matmul("d_xn", d_proj, w_in, "nt")
    g_w_in = matmul("g_w_in", xn, d_proj, "tn")

    def rms_x_bwd(i, x_, dxn_, dy_, gain_):
        dx, dgain = _vjp_fn(f_rms_x, 2, 1)(i, x_, gain_, dxn_)
        return dx + dy_, dgain

    grad_x, g_norm = rowwise("rms_x_bwd", rms_x_bwd, t_len, rt,
                             [(x, (r, D_MODEL, 0)), (d_xn, (r, D_MODEL, 0)), (dy, (r, D_MODEL, 0))], [norm_gain],
                             [(r, D_MODEL, F32)], [(1, D_MODEL)])

    grads = {
        "norm_gain": g_norm, "w_in": g_w_in, "mla_q_a_gain": g_qa, "mla_kv_a_gain": g_kva, "w_uq": g_w_uq,
        "w_ukv": g_w_ukv, "mla_q_norm_gain": g_qg[:, :QK], "mla_k_norm_gain": g_kg[:, :QK],
        "gdn_conv_w": jnp.concatenate([g_cwq, g_cwk, g_cwv], axis=1), "gdn_a_log": g_alog[:, :HEADS],
        "gdn_dt_bias": g_dtb[:, :HEADS], "gdn_out_norm_gain": g_og, "w_out": g_w_out,
    }
    return loss, grad_x, grads


MESH = pl.DeviceIdType.MESH
ANY = pl.BlockSpec(memory_space=pl.ANY)
CHIP_FLIPS = ((1, 0), (0, 1), (1, 1))


def _place():
    return lax.axis_index("x"), lax.axis_index("y"), lax.axis_index("c")


def _flip(v, f):
    return 1 - v if f else v


def all_gather(shards):
    n_arr = len(shards)

    def body(*refs):
        x_refs, o_refs = refs[:n_arr], refs[n_arr:2 * n_arr]
        send_sems, recv_sems, local_sems = refs[2 * n_arr:]
        x, y, c = _place()
        me, sibling = (x, y, c), (x, y, 1 - c)
        chips = [(_flip(x, fx), _flip(y, fy)) for fx, fy in CHIP_FLIPS]

        def copy(a, k, block, to, src=None):
            px, py, pc = block
            dst = o_refs[a].at[4 * px + 2 * py + pc]
            return pltpu.make_async_remote_copy(
                src_ref=dst if src is None else src, dst_ref=dst, send_sem=send_sems.at[a, k],
                recv_sem=recv_sems.at[a, k], device_id=to, device_id_type=MESH)

        mine, first, passed = [], [], []
        for a in range(n_arr):
            cp = pltpu.make_async_copy(x_refs[a], o_refs[a].at[4 * x + 2 * y + c], local_sems.at[a])
            cp.start()
            mine.append(cp)
            first.append(copy(a, 0, me, sibling, src=x_refs[a]))
            first += [copy(a, 1 + j, me, (*chip, c), src=x_refs[a]) for j, chip in enumerate(chips)]
        for cp in first:
            cp.start()
        for j, chip in enumerate(chips):
            for a in range(n_arr):
                copy(a, 1 + j, (*chip, c), me).wait_recv()
                cp = copy(a, 4 + j, (*chip, c), sibling)
                cp.start()
                passed.append(cp)
        for a in range(n_arr):
            copy(a, 0, sibling, me).wait_recv()
            for j, chip in enumerate(chips):
                copy(a, 4 + j, (*chip, 1 - c), me).wait_recv()
        for cp in first + passed:
            cp.wait_send()
        for cp in mine:
            cp.wait()

    return pl.pallas_call(
        body,
        name="all_gather",
        out_shape=[jax.ShapeDtypeStruct((N_DEV,) + s.shape, s.dtype) for s in shards],
        in_specs=[ANY] * n_arr,
        out_specs=[ANY] * n_arr,
        scratch_shapes=[pltpu.SemaphoreType.DMA((n_arr, 7)), pltpu.SemaphoreType.DMA((n_arr, 7)),
                        pltpu.SemaphoreType.DMA((n_arr,))],
    )(*shards)


def exchange_cores(grads):
    n_arr = len(grads)

    def body(*refs):
        g_refs, o_refs = refs[:n_arr], refs[n_arr:2 * n_arr]
        send_sems, recv_sems = refs[2 * n_arr:]
        x, y, c = _place()
        copies = []
        for a in range(n_arr):
            for q in range(4):
                cp = pltpu.make_async_remote_copy(
                    src_ref=g_refs[a].at[2 * q + (1 - c)], dst_ref=o_refs[a].at[q], send_sem=send_sems.at[a, q],
                    recv_sem=recv_sems.at[a, q], device_id=(x, y, 1 - c), device_id_type=MESH)
                cp.start()
                copies.append(cp)
        for cp in copies:
            cp.wait()

    return pl.pallas_call(
        body,
        name="exchange_cores",
        out_shape=[jax.ShapeDtypeStruct((4,) + g.shape[1:], g.dtype) for g in grads],
        in_specs=[ANY] * n_arr,
        out_specs=[ANY] * n_arr,
        scratch_shapes=[pltpu.SemaphoreType.DMA((n_arr, 4)), pltpu.SemaphoreType.DMA((n_arr, 4))],
    )(*grads)


def exchange_chips(parts):
    n_arr = len(parts)

    def body(*refs):
        p_refs, o_refs = refs[:n_arr], refs[n_arr:2 * n_arr]
        send_sems, recv_sems = refs[2 * n_arr:]
        x, y, c = _place()
        copies = []
        for a in range(n_arr):
            for j, (fx, fy) in enumerate(CHIP_FLIPS):
                px, py = _flip(x, fx), _flip(y, fy)
                cp = pltpu.make_async_remote_copy(
                    src_ref=p_refs[a].at[2 * px + py], dst_ref=o_refs[a].at[j], send_sem=send_sems.at[a, j],
                    recv_sem=recv_sems.at[a, j], device_id=(px, py, c), device_id_type=MESH)
                cp.start()
                copies.append(cp)
        for cp in copies:
            cp.wait()

    return pl.pallas_call(
        body,
        name="exchange_chips",
        out_shape=[jax.ShapeDtypeStruct((3,) + p.shape[1:], p.dtype) for p in parts],
        in_specs=[ANY] * n_arr,
        out_specs=[ANY] * n_arr,
        scratch_shapes=[pltpu.SemaphoreType.DMA((n_arr, 3)), pltpu.SemaphoreType.DMA((n_arr, 3))],
    )(*parts)


def gather_small(v):
    def body(v_ref, o_ref, send_sems, recv_sems, local_sem):
        x, y, c = _place()
        me = 4 * x + 2 * y + c
        mine = pltpu.make_async_copy(v_ref, o_ref.at[me], local_sem)
        mine.start()
        copies = []
        for k in range(1, N_DEV):
            fx, fy, fc = (k >> 2) & 1, (k >> 1) & 1, k & 1
            cp = pltpu.make_async_remote_copy(
                src_ref=v_ref, dst_ref=o_ref.at[me], send_sem=send_sems.at[k - 1], recv_sem=recv_sems.at[k - 1],
                device_id=(_flip(x, fx), _flip(y, fy), _flip(c, fc)), device_id_type=MESH)
            cp.start()
            copies.append(cp)
        for cp in copies:
            cp.wait()
        mine.wait()

    return pl.pallas_call(
        body,
        name="gather_small",
        out_shape=jax.ShapeDtypeStruct((N_DEV,) + v.shape, v.dtype),
        in_specs=[ANY],
        out_specs=ANY,
        scratch_shapes=[pltpu.SemaphoreType.DMA((N_DEV - 1,)), pltpu.SemaphoreType.DMA((N_DEV - 1,)),
                        pltpu.SemaphoreType.DMA],
    )(v)


def _row_tile(rows):
    for t in (256, 128, 64, 32, 16, 8):
        if rows % t == 0:
            return t
    return rows


def add_core_parts(name, g, recv, c_idx, wire):
    _, rows, cols = g.shape
    tr = _row_tile(rows)

    def body(c_ref, g_ref, r_ref, o_ref, w_ref):
        part = g_ref[...] + r_ref[...]
        o_ref[...] = part
        w_ref[...] = part.astype(w_ref.dtype)

    blk = pl.BlockSpec((1, tr, cols), lambda q, i, c_ref: (q, i, 0))
    return pl.pallas_call(
        body,
        name=name,
        grid_spec=pltpu.PrefetchScalarGridSpec(
            num_scalar_prefetch=1,
            grid=(4, rows // tr),
            in_specs=[pl.BlockSpec((1, tr, cols), lambda q, i, c_ref: (2 * q + c_ref[0], i, 0)), blk],
            out_specs=[blk, blk],
        ),
        out_shape=[jax.ShapeDtypeStruct((4, rows, cols), F32), jax.ShapeDtypeStruct((4, rows, cols), wire)],
        compiler_params=_cparams("parallel", "parallel"),
    )(c_idx, g, recv)


def _adamw(w, g, m, v):
    m = ADAM_B1 * m + (1.0 - ADAM_B1) * g
    v = ADAM_B2 * v + (1.0 - ADAM_B2) * (g * g)
    m_hat = m / (1.0 - ADAM_B1 ** ADAM_STEP)
    v_hat = v / (1.0 - ADAM_B2 ** ADAM_STEP)
    delta = -ADAM_LR * (m_hat / (jnp.sqrt(v_hat) + ADAM_EPS) + ADAM_WD * w)
    return delta, m, v


def adamw_sharded(name, parts, recv, q_idx, w, m, v):
    rows, cols = w.shape
    tr = _row_tile(rows)

    def body(q_ref, p_ref, r_ref, w_ref, m_ref, v_ref, g_out, d_out, m_out, v_out):
        g = p_ref[0] + r_ref[0].astype(F32) + r_ref[1].astype(F32) + r_ref[2].astype(F32)
        d, m_new, v_new = _adamw(w_ref[...], g, m_ref[...], v_ref[...])
        g_out[...], d_out[...], m_out[...], v_out[...] = g, d, m_new, v_new

    blk = pl.BlockSpec((tr, cols), lambda i, q_ref: (i, 0))
    return pl.pallas_call(
        body,
        name=name,
        grid_spec=pltpu.PrefetchScalarGridSpec(
            num_scalar_prefetch=1,
            grid=(rows // tr,),
            in_specs=[pl.BlockSpec((1, tr, cols), lambda i, q_ref: (q_ref[0], i, 0)),
                      pl.BlockSpec((3, tr, cols), lambda i, q_ref: (0, i, 0)), blk, blk, blk],
            out_specs=[blk] * 4,
        ),
        out_shape=[jax.ShapeDtypeStruct((rows, cols), F32)] * 4,
        compiler_params=_cparams("parallel"),
    )(q_idx, parts, recv, w, m, v)


def adamw_small(gathered, w, m, v):
    def body(g_ref, w_ref, m_ref, v_ref, g_out, d_out, m_out, v_out):
        g = g_ref[0]
        for j in range(1, N_DEV):
            g = g + g_ref[j]
        d, m_new, v_new = _adamw(w_ref[...], g, m_ref[...], v_ref[...])
        g_out[...], d_out[...], m_out[...], v_out[...] = g, d, m_new, v_new

    return pl.pallas_call(body, name="adamw_small", out_shape=[jax.ShapeDtypeStruct(w.shape, F32)] * 4)(gathered, w, m, v)


SHARDED = ("w_in", "w_uq", "w_ukv", "gdn_conv_w", "w_out")
SMALL = (("norm_gain", D_MODEL), ("mla_q_a_gain", Q_LORA), ("mla_kv_a_gain", KV_LORA), ("mla_q_norm_gain", QK),
         ("mla_k_norm_gain", QK), ("gdn_a_log", HEADS), ("gdn_dt_bias", HEADS), ("gdn_out_norm_gain", GDN_DIM))
WEIGHT_ORDER = ("norm_gain", "w_in", "mla_q_a_gain", "mla_kv_a_gain", "w_uq", "w_ukv", "mla_q_norm_gain",
                "mla_k_norm_gain", "gdn_conv_w", "gdn_a_log", "gdn_dt_bias", "gdn_out_norm_gain", "w_out")


def _pack_small(d):
    rows = []
    for name, n in SMALL:
        a = d[name].reshape(-1).astype(F32)
        n_pad = -(-n // LANE) * LANE
        rows.append(jnp.pad(a, (0, n_pad - n)).reshape(n_pad // LANE, LANE))
    packed = jnp.concatenate(rows, axis=0)
    return jnp.pad(packed, ((0, -packed.shape[0] % SUBLANE), (0, 0)))


def _unpack_small(packed):
    out, row = {}, 0
    for name, n in SMALL:
        n_rows = -(-n // LANE)
        out[name] = packed[row:row + n_rows].reshape(-1)[:n].reshape(1, n)
        row += n_rows
    return out


def kernel(x, positions, norm_gain, w_in, mla_q_a_gain, mla_kv_a_gain, w_uq, w_ukv, mla_q_norm_gain, mla_k_norm_gain, gdn_conv_w, gdn_a_log, gdn_dt_bias, gdn_out_norm_gain, w_out, loss_target, m_norm_gain, m_w_in, m_mla_q_a_gain, m_mla_kv_a_gain, m_w_uq, m_w_ukv, m_mla_q_norm_gain, m_mla_k_norm_gain, m_gdn_conv_w, m_gdn_a_log, m_gdn_dt_bias, m_gdn_out_norm_gain, m_w_out, v_norm_gain, v_w_in, v_mla_q_a_gain, v_mla_kv_a_gain, v_w_uq, v_w_ukv, v_mla_q_norm_gain, v_mla_k_norm_gain, v_gdn_conv_w, v_gdn_a_log, v_gdn_dt_bias, v_gdn_out_norm_gain, v_w_out):
    w = dict(norm_gain=norm_gain, w_in=w_in, mla_q_a_gain=mla_q_a_gain, mla_kv_a_gain=mla_kv_a_gain, w_uq=w_uq,
             w_ukv=w_ukv, mla_q_norm_gain=mla_q_norm_gain, mla_k_norm_gain=mla_k_norm_gain, gdn_conv_w=gdn_conv_w,
             gdn_a_log=gdn_a_log, gdn_dt_bias=gdn_dt_bias, gdn_out_norm_gain=gdn_out_norm_gain, w_out=w_out)
    m = dict(norm_gain=m_norm_gain, w_in=m_w_in, mla_q_a_gain=m_mla_q_a_gain, mla_kv_a_gain=m_mla_kv_a_gain,
             w_uq=m_w_uq, w_ukv=m_w_ukv, mla_q_norm_gain=m_mla_q_norm_gain, mla_k_norm_gain=m_mla_k_norm_gain,
             gdn_conv_w=m_gdn_conv_w, gdn_a_log=m_gdn_a_log, gdn_dt_bias=m_gdn_dt_bias,
             gdn_out_norm_gain=m_gdn_out_norm_gain, w_out=m_w_out)
    v = dict(norm_gain=v_norm_gain, w_in=v_w_in, mla_q_a_gain=v_mla_q_a_gain, mla_kv_a_gain=v_mla_kv_a_gain,
             w_uq=v_w_uq, w_ukv=v_w_ukv, mla_q_norm_gain=v_mla_q_norm_gain, mla_k_norm_gain=v_mla_k_norm_gain,
             gdn_conv_w=v_gdn_conv_w, gdn_a_log=v_gdn_a_log, gdn_dt_bias=v_gdn_dt_bias,
             gdn_out_norm_gain=v_gdn_out_norm_gain, w_out=v_w_out)
    t_len = x.shape[1]

    shards = [w[n][0] if n == "gdn_conv_w" else w[n][0].astype(_BF) for n in SHARDED]
    a_w_in, a_w_uq, a_w_ukv, a_cw, a_w_out = all_gather(shards)

    def cols_whole(g):
        return g.transpose(1, 0, 2).reshape(g.shape[1], N_DEV * g.shape[2])

    p = {n: w[n] for n, _ in SMALL}
    p["w_in"] = arrange_w_in(cols_whole(a_w_in))
    p["w_uq"] = arrange_w_uq(cols_whole(a_w_uq))
    p["w_ukv"] = cols_whole(a_w_ukv)
    p["gdn_conv_w"] = cols_whole(a_cw)
    p["w_out"] = a_w_out.reshape(N_DEV * a_w_out.shape[1], a_w_out.shape[2])

    pos = positions.reshape(t_len, 1).astype(F32)
    loss, grad_x, grads = local_step(x.reshape(t_len, D_MODEL), pos, loss_target.reshape(t_len, D_MODEL), p)
    loss = lax.psum(loss, ("x", "y", "c"))

    def col_blocks(g):
        return g.reshape(g.shape[0], N_DEV, g.shape[1] // N_DEV).transpose(1, 0, 2)

    blocks = [col_blocks(unarrange_w_in(grads["w_in"])), col_blocks(unarrange_w_uq(grads["w_uq"])),
              col_blocks(grads["w_ukv"]), col_blocks(grads["gdn_conv_w"]),
              grads["w_out"].reshape(N_DEV, D_MODEL // N_DEV, D_MODEL)]
    xi, yi, ci = _place()
    c_idx = jnp.reshape(ci, (1,)).astype(jnp.int32)
    q_idx = jnp.reshape(2 * xi + yi, (1,)).astype(jnp.int32)
    from_sibling = exchange_cores(blocks)
    parts = [add_core_parts("add_" + n, g, r, c_idx, F32 if n == "gdn_conv_w" else _BF)
             for n, g, r in zip(SHARDED, blocks, from_sibling)]
    from_chips = exchange_chips([wire for _, wire in parts])
    out = {}
    for n, (prt, _), rcv in zip(SHARDED, parts, from_chips):
        shape = w[n].shape
        res = adamw_sharded("adamw_" + n, prt, rcv, q_idx, w[n].reshape(shape[-2:]), m[n].reshape(shape[-2:]),
                            v[n].reshape(shape[-2:]))
        out[n] = [a.reshape(shape) for a in res]

    small_all = gather_small(_pack_small(grads))
    res = adamw_small(small_all, _pack_small(w), _pack_small(m), _pack_small(v))
    unpacked = [_unpack_small(a) for a in res]
    for n, _ in SMALL:
        out[n] = [u[n] for u in unpacked]

    return (loss, grad_x.reshape(x.shape), *[out[n][0] for n in WEIGHT_ORDER], *[out[n][1] for n in WEIGHT_ORDER],
            *[out[n][2] for n in WEIGHT_ORDER], *[out[n][3] for n in WEIGHT_ORDER])
```

```python
import functools

import jax
import jax.numpy as jnp
from jax import lax
from jax.experimental import pallas as pl
from jax.experimental.pallas import tpu as pltpu

F32 = jnp.float32
_BF = jnp.bfloat16
HI = lax.Precision.HIGHEST

D_MODEL = 2048
HEADS = 8
NOPE = 128
ROPE = 64
QK = NOPE + ROPE
Q_LORA = 512
KV_LORA = 256
HEAD_PAD = 256
GDN_DIM = 128
WIDTH = HEADS * 128
CONV_W = 4
CHUNK = 64
ROPE_THETA = 10000.0
EPS = 1e-6
N_DEV = 8
LANE = 128
SUBLANE = 8
VMEM_LIMIT = 48 * 1024 * 1024

ADAM_LR, ADAM_B1, ADAM_B2, ADAM_EPS, ADAM_WD, ADAM_STEP = 0.001, 0.9, 0.999, 1e-08, 0.01, 10

P_MGATE, P_GQ, P_GK, P_GV, P_GGATE = 0, 1024, 2048, 3072, 4096
P_CQ, P_CKV, P_KR, P_GAB = 5120, 5632, 5888, 6016
P_COLS = 6144
R_SPLITS = (512, 256, 64, 1024, 1024, 1024, 1024, 8, 8, 1024)


def _cparams(*sem):
    return pltpu.CompilerParams(dimension_semantics=sem, vmem_limit_bytes=VMEM_LIMIT)


def _d_nn(a, b):
    return jnp.dot(a.astype(_BF), b.astype(_BF), preferred_element_type=F32)


def _d_nt(a, b):
    return lax.dot_general(a.astype(_BF), b.astype(_BF), (((1,), (1,)), ((), ())), preferred_element_type=F32)


def _d_tn(a, b):
    return lax.dot_general(a.astype(_BF), b.astype(_BF), (((0,), (0,)), ((), ())), preferred_element_type=F32)


@jax.custom_vjp
def _mm(a, b):
    return _d_nn(a, b)


_mm.defvjp(lambda a, b: (_d_nn(a, b), (a, b)), lambda r, g: (_d_nt(g, r[1]), _d_tn(r[0], g)))


@jax.custom_vjp
def _mm_nt(a, b):
    return _d_nt(a, b)


_mm_nt.defvjp(lambda a, b: (_d_nt(a, b), (a, b)), lambda r, g: (_d_nn(g, r[1]), _d_tn(g, r[0])))


@jax.custom_vjp
def _mm_tn(a, b):
    return _d_tn(a, b)


_mm_tn.defvjp(lambda a, b: (_d_tn(a, b), (a, b)), lambda r, g: (_d_nt(r[1], g), _d_nn(r[0], g)))


def _hi(a, b):
    return jnp.dot(a, b, preferred_element_type=F32, precision=HI)


_NN3 = (((2,), (1,)), ((0,), (0,)))
_NT3 = (((2,), (2,)), ((0,), (0,)))
_TN3 = (((1,), (1,)), ((0,), (0,)))


def _bdot(a, b, dims, hi):
    if hi:
        return lax.dot_general(a, b, dims, preferred_element_type=F32, precision=hi)
    return lax.dot_general(a.astype(_BF), b.astype(_BF), dims, preferred_element_type=F32)


def _batched_matmuls(hi):
    nn = jax.custom_vjp(lambda a, b: _bdot(a, b, _NN3, hi))
    nt = jax.custom_vjp(lambda a, b: _bdot(a, b, _NT3, hi))
    tn = jax.custom_vjp(lambda a, b: _bdot(a, b, _TN3, hi))
    nn.defvjp(lambda a, b: (_bdot(a, b, _NN3, hi), (a, b)),
              lambda r, g: (_bdot(g, r[1], _NT3, hi), _bdot(r[0], g, _TN3, hi)))
    nt.defvjp(lambda a, b: (_bdot(a, b, _NT3, hi), (a, b)),
              lambda r, g: (_bdot(g, r[1], _NN3, hi), _bdot(g, r[0], _TN3, hi)))
    tn.defvjp(lambda a, b: (_bdot(a, b, _TN3, hi), (a, b)),
              lambda r, g: (_bdot(r[1], g, _NT3, hi), _bdot(r[0], g, _NN3, hi)))
    return nn, nt, tn


_bmm, _bmm_nt, _bmm_tn = _batched_matmuls(False)
_bhi, _bhi_nt, _bhi_tn = _batched_matmuls(HI)


def _split2(x):
    hi = x.astype(_BF)
    return hi, (x - hi.astype(F32)).astype(_BF)


def _pdot(a, b, mode):
    (a_hi, a_lo), (b_hi, b_lo) = _split2(a), _split2(b)
    a_ax, b_ax, dims = {"nn": (2, 1, _NN3), "nt": (2, 2, _NT3), "tn": (1, 1, _TN3)}[mode]
    lhs = jnp.concatenate([a_hi, a_lo, a_hi], axis=a_ax)
    rhs = jnp.concatenate([b_hi, b_hi, b_lo], axis=b_ax)
    return lax.dot_general(lhs, rhs, dims, preferred_element_type=F32)


def _packed_matmuls():
    nn = jax.custom_vjp(lambda a, b: _pdot(a, b, "nn"))
    nn.defvjp(lambda a, b: (_pdot(a, b, "nn"), (a, b)), lambda r, g: (_pdot(g, r[1], "nt"), _pdot(r[0], g, "tn")))
    return nn


_bh3 = _packed_matmuls()
_bh3_passes = _batched_matmuls(lax.Precision.HIGH)[0]


@functools.partial(jax.custom_vjp, nondiff_argnums=(1, 2))
def _roll(x, shift, axis):
    return pltpu.roll(x, shift, axis)


def _roll_fwd(x, shift, axis):
    return pltpu.roll(x, shift, axis), None


def _roll_bwd(shift, axis, _, g):
    n = g.shape[axis]
    return (pltpu.roll(g, (n - shift) % n, axis),)


_roll.defvjp(_roll_fwd, _roll_bwd)


def _rms(x, gain):
    return x * lax.rsqrt(jnp.mean(x * x, axis=-1, keepdims=True) + EPS) * gain


MM_TILE = 1024


def matmul(name, a, b, mode):
    if mode == "nn":
        (m, k), (k2, n) = a.shape, b.shape
    elif mode == "nt":
        (m, k), (n, k2) = a.shape, b.shape
    else:
        (k, m), (k2, n) = a.shape, b.shape
    assert k == k2, (name, a.shape, b.shape)
    tm, tn, tk = min(MM_TILE, m), min(MM_TILE, n), min(MM_TILE, k)
    assert m % tm == 0 and n % tn == 0 and k % tk == 0, (name, m, n, k)
    dot = {"nn": _d_nn, "nt": _d_nt, "tn": _d_tn}[mode]

    def body(a_ref, b_ref, o_ref):
        kk = pl.program_id(2)
        part = dot(a_ref[...], b_ref[...])

        @pl.when(kk == 0)
        def _():
            o_ref[...] = part

        @pl.when(kk != 0)
        def _():
            o_ref[...] += part

    if mode == "nn":
        a_spec = pl.BlockSpec((tm, tk), lambda j, i, kk: (i, kk))
        b_spec = pl.BlockSpec((tk, tn), lambda j, i, kk: (kk, j))
    elif mode == "nt":
        a_spec = pl.BlockSpec((tm, tk), lambda j, i, kk: (i, kk))
        b_spec = pl.BlockSpec((tn, tk), lambda j, i, kk: (j, kk))
    else:
        a_spec = pl.BlockSpec((tk, tm), lambda j, i, kk: (kk, i))
        b_spec = pl.BlockSpec((tk, tn), lambda j, i, kk: (kk, j))
    return pl.pallas_call(
        body,
        name=name,
        grid=(n // tn, m // tm, k // tk),
        in_specs=[a_spec, b_spec],
        out_specs=pl.BlockSpec((tm, tn), lambda j, i, kk: (i, j)),
        out_shape=jax.ShapeDtypeStruct((m, n), F32),
        compiler_params=_cparams("parallel", "parallel", "arbitrary"),
    )(a, b)


def rowwise(name, fn, t_len, tile, row_in, full_in, row_out, acc_out=(), carries=(), reverse=False):
    tile = min(tile, t_len)
    n = t_len // tile
    assert t_len % tile == 0 and tile % SUBLANE == 0
    n_in, n_ro, n_acc, n_car = len(row_in) + len(full_in), len(row_out), len(acc_out), len(carries)

    def ti(i):
        return (n - 1 - i) if reverse else i

    in_specs, args = [], []
    for arr, kind in row_in:
        if kind[0] == "r":
            in_specs.append(pl.BlockSpec((tile, kind[1]), lambda i, c=kind[2]: (ti(i), c)))
        elif kind[0] == "h":
            in_specs.append(pl.BlockSpec((arr.shape[0], tile, arr.shape[2]), lambda i: (0, ti(i), 0)))
        else:
            in_specs.append(pl.BlockSpec(
                (SUBLANE, kind[1]), lambda i, c=kind[2]: (jnp.maximum(ti(i) * (tile // SUBLANE) - 1, 0), c)))
        args.append(arr)
    for arr in full_in:
        in_specs.append(pl.BlockSpec(arr.shape, lambda i, nd=arr.ndim: (0,) * nd))
        args.append(arr)
    out_specs, out_shape = [], []
    for kind in row_out:
        if kind[0] == "r":
            out_specs.append(pl.BlockSpec((tile, kind[1]), lambda i: (ti(i), 0)))
            out_shape.append(jax.ShapeDtypeStruct((t_len, kind[1]), kind[2]))
        else:
            out_specs.append(pl.BlockSpec((kind[1], tile, kind[2]), lambda i: (0, ti(i), 0)))
            out_shape.append(jax.ShapeDtypeStruct((kind[1], t_len, kind[2]), kind[3]))
    for shp in acc_out:
        out_specs.append(pl.BlockSpec(shp, lambda i, nd=len(shp): (0,) * nd))
        out_shape.append(jax.ShapeDtypeStruct(shp, F32))

    def body(*refs):
        in_refs = refs[:n_in]
        ro_refs = refs[n_in:n_in + n_ro]
        acc_refs = refs[n_in + n_ro:n_in + n_ro + n_acc]
        car_refs = refs[n_in + n_ro + n_acc:]
        step = pl.program_id(0)
        if n_car:
            @pl.when(step == 0)
            def _():
                for r in car_refs:
                    r[...] = jnp.zeros_like(r)
        vals = [r[...].astype(F32) for r in in_refs] + [r[...] for r in car_refs]
        outs = fn(ti(step), *vals)
        assert len(outs) == n_ro + n_acc + n_car, (name, len(outs))
        for r, o in zip(ro_refs, outs[:n_ro]):
            r[...] = o.astype(r.dtype)
        for r, o in zip(acc_refs, outs[n_ro:n_ro + n_acc]):
            @pl.when(step == 0)
            def _(r=r, o=o):
                r[...] = o

            @pl.when(step != 0)
            def _(r=r, o=o):
                r[...] += o
        for r, o in zip(car_refs, outs[n_ro + n_acc:]):
            r[...] = o

    res = pl.pallas_call(
        body,
        name=name,
        grid=(n,),
        in_specs=in_specs,
        out_specs=out_specs,
        out_shape=out_shape,
        scratch_shapes=[pltpu.VMEM(s, F32) for s in carries],
        compiler_params=_cparams("arbitrary"),
    )(*args)
    return list(res)


def _vjp_fn(fn, n_diff, n_out):
    def g(i, *a):
        ins, cts = a[:len(a) - n_out], a[len(a) - n_out:]
        diff, rest = ins[:n_diff], ins[n_diff:]
        _, pull = jax.vjp(lambda *d: tuple(fn(i, *d, *rest)), *diff)
        return tuple(pull(tuple(cts)))

    return g


def f_rms_x(i, x, gain):
    return (_rms(x, gain),)


def f_lat(i, cq, ckv, gq, gkv):
    return _rms(cq, gq), _rms(ckv, gkv)


def _rope_tables(pos, invf):
    ang = pos * invf
    lane = lax.broadcasted_iota(jnp.int32, (1, LANE), 1)
    cosv, sinv = jnp.cos(ang), jnp.sin(ang)
    half = ROPE // 2
    c = jnp.where(lane < ROPE, cosv, 0.0)
    sa = jnp.where(lane < half, -sinv, 0.0)
    sb = jnp.where((lane >= half) & (lane < ROPE), sinv, 0.0)
    return c, sa, sb


def _rope(xh, tabs):
    c, sa, sb = tabs
    half = ROPE // 2
    return xh * c + _roll(xh, LANE - half, 1) * sa + _roll(xh, half, 1) * sb


def f_head(i, q_raw, kv_raw, kr, qg, kg, pos, invf):
    tabs = _rope_tables(pos, invf)
    qs, ks, vs = [], [], []
    kr_ss = jnp.sum(kr * kr, axis=-1, keepdims=True)
    for h in range(HEADS):
        lo = q_raw[:, HEAD_PAD * h:HEAD_PAD * h + NOPE]
        hi = q_raw[:, HEAD_PAD * h + NOPE:HEAD_PAD * (h + 1)]
        ss = jnp.sum(lo * lo, axis=-1, keepdims=True) + jnp.sum(hi * hi, axis=-1, keepdims=True)
        r = lax.rsqrt(ss * (1.0 / QK) + EPS)
        qs.append(jnp.concatenate([lo * r * qg[:, :NOPE], _rope(hi * r * qg[:, NOPE:], tabs)], axis=1))
        lo = kv_raw[:, 2 * NOPE * h:2 * NOPE * h + NOPE]
        ss = jnp.sum(lo * lo, axis=-1, keepdims=True) + kr_ss
        r = lax.rsqrt(ss * (1.0 / QK) + EPS)
        ks.append(jnp.concatenate([lo * r * kg[:, :NOPE], _rope(kr * r * kg[:, NOPE:], tabs)], axis=1))
        vs.append(kv_raw[:, 2 * NOPE * h + NOPE:2 * NOPE * (h + 1)])
    return jnp.stack(qs), jnp.stack(ks), jnp.stack(vs)


def f_mix(i, o_mla, mgate, o_gdn, ggate, og):
    parts = [o_mla * jax.nn.silu(mgate)]
    for h in range(HEADS):
        parts.append(_rms(o_gdn[h], og) * jax.nn.silu(ggate[:, LANE * h:LANE * (h + 1)]))
    return (jnp.concatenate(parts, axis=1),)


def _row(a, j):
    rows = lax.broadcasted_iota(jnp.int32, a.shape, 0)
    return jnp.sum(jnp.where(rows == j, a, 0.0), axis=0, keepdims=True)


def _shift_rows(x, halo, d):
    xs = _roll(x, d, 0)
    hs = _roll(halo, d, 0)
    r8 = lax.broadcasted_iota(jnp.int32, hs.shape, 0)
    top = jnp.where(r8 < d, hs, xs[:SUBLANE])
    return jnp.concatenate([top, xs[SUBLANE:]], axis=0)


def _conv_silu(x, halo, w):
    y = _row(w, CONV_W - 1) * x
    for j in range(CONV_W - 1):
        y = y + _row(w, j) * _shift_rows(x, halo, CONV_W - 1 - j)
    return jax.nn.silu(y)


def _head_select(offset):
    r = lax.broadcasted_iota(jnp.int32, (LANE, WIDTH), 0)
    c = lax.broadcasted_iota(jnp.int32, (LANE, WIDTH), 1)
    return (r == offset + lax.shift_right_logical(c, 7)).astype(_BF)


def _split3(x):
    x1 = x.astype(_BF)
    r1 = x - x1.astype(F32)
    x2 = r1.astype(_BF)
    return x1, x2, (r1 - x2.astype(F32)).astype(_BF)


@jax.custom_vjp
def _spread(x, sel):
    return _d_nn(jnp.concatenate(_split3(x), axis=1), jnp.concatenate([sel, sel, sel], axis=0))


def _spread_fwd(x, sel):
    return _spread(x, sel), sel


def _spread_bwd(sel, g):
    g1, g2, g3 = _split3(g)
    return _d_nt(g1, sel) + _d_nt(g2, sel) + _d_nt(g3, sel), jnp.zeros_like(sel)


_spread.defvjp(_spread_fwd, _spread_bwd)


def f_gdn_pre(i, gq, gk, gv, gab, hq, hk, hv, cwq, cwk, cwv, alog, dtb):
    live = jnp.where(i == 0, 0.0, 1.0)
    q = _conv_silu(gq, hq * live, cwq)
    k = _conv_silu(gk, hk * live, cwk)
    v = _conv_silu(gv, hv * live, cwv)
    g = _spread(-jnp.exp(alog) * jax.nn.softplus(gab + dtb), _head_select(0))
    beta = _spread(jax.nn.sigmoid(gab), _head_select(HEADS))
    qs, ks, vs, gs, bs = [], [], [], [], []
    for h in range(HEADS):
        sl = slice(LANE * h, LANE * (h + 1))
        qh, kh = q[:, sl], k[:, sl]
        qs.append(qh * lax.rsqrt(jnp.sum(qh * qh, axis=-1, keepdims=True) + EPS) * (GDN_DIM ** -0.5))
        ks.append(kh * lax.rsqrt(jnp.sum(kh * kh, axis=-1, keepdims=True) + EPS))
        vs.append(v[:, sl])
        gs.append(g[:, sl])
        bs.append(beta[:, sl])
    return jnp.stack(qs), jnp.stack(ks), jnp.stack(vs), jnp.stack(gs), jnp.stack(bs)


def gdn_pre_bwd(i, gq, gk, gv, gab, hq, hk, hv, dq, dk, dv, dg, db, cwq, cwk, cwv, alog, dtb, cq, ck, cv):
    grads = _vjp_fn(f_gdn_pre, 12, 5)(i, gq, gk, gv, gab, hq, hk, hv, cwq, cwk, cwv, alog, dtb, dq, dk, dv, dg, db)
    dgq, dgk, dgv, dgab, dhq, dhk, dhv, dcwq, dcwk, dcwv, dalog, ddtb = grads

    def add_tail(dx, carry):
        return jnp.concatenate([dx[:-SUBLANE], dx[-SUBLANE:] + carry], axis=0)

    return (add_tail(dgq, cq), add_tail(dgk, ck), add_tail(dgv, cv), dgab,
            dcwq, dcwk, dcwv, dalog, ddtb, dhq, dhk, dhv)


def f_loss(i, x, h, tgt):
    e = x + h - tgt
    part = 0.5 * jnp.sum(e * e) * (1.0 / D_MODEL)
    dy = e * (1.0 / D_MODEL)
    return dy, dy, jnp.zeros((SUBLANE, LANE), F32) + part


def _flash_tile(t_len):
    return min(512, t_len)


FLASH_HEADS = 4
FLASH_BWD_HEADS = 2
LOG2E = 1.4426950408889634


def _causal(rows0, shape):
    r = rows0 + lax.broadcasted_iota(jnp.int32, shape, 0)
    c = lax.broadcasted_iota(jnp.int32, shape, 1)
    return c <= r


def flash_fwd(q, k, v):
    h_n, t_len, _ = q.shape
    tq = _flash_tile(t_len)
    nq = t_len // tq
    hb = FLASH_HEADS
    kw = 2 if nq % 2 == 0 else 1
    tk = kw * tq
    c2 = (QK ** -0.5) * LOG2E
    pairs = [(i, j) for i in range(nq) for j in range(i // kw + 1)]
    qt = jnp.array([p[0] for p in pairs], jnp.int32)
    kt = jnp.array([p[1] for p in pairs], jnp.int32)

    def body(qt_ref, kt_ref, q_ref, k_ref, v_ref, o_ref, lse_ref, m_s, acc_s):
        step = pl.program_id(1)
        qi, kj = qt_ref[step], kt_ref[step]
        last = qi // kw

        @pl.when(kj == 0)
        def _():
            m_s[...] = jnp.full_like(m_s, -jnp.inf)
            acc_s[...] = jnp.zeros_like(acc_s)

        def tile(diagonal):
            s = _bdot(q_ref[...], k_ref[...], _NT3, False) * c2
            if diagonal:
                s = jnp.where(_causal((qi % kw) * tq, (tq, tk))[None], s, -jnp.inf)
            m_old = m_s[...]
            m_new = jnp.maximum(m_old, jnp.max(s, axis=-1, keepdims=True))
            p = jnp.exp2(s - m_new).astype(_BF)
            v_ones = jnp.concatenate([v_ref[...], jnp.ones((hb, tk, LANE), _BF)], axis=2)
            acc_s[...] = jnp.exp2(m_old - m_new) * acc_s[...] + _bdot(p, v_ones, _NN3, False)
            m_s[...] = m_new

        @pl.when(kj < last)
        def _():
            tile(False)

        @pl.when(kj == last)
        def _():
            tile(True)
            acc = acc_s[...]
            l_sum = acc[:, :, LANE:]
            o = acc[:, :, :LANE] / l_sum
            for hh in range(hb):
                o_ref[:, LANE * hh:LANE * (hh + 1)] = o[hh]
            lse_ref[...] = m_s[...] + jnp.log2(jnp.max(l_sum, axis=-1, keepdims=True))

    return pl.pallas_call(
        body,
        name="flash_fwd",
        grid_spec=pltpu.PrefetchScalarGridSpec(
            num_scalar_prefetch=2,
            grid=(h_n // hb, qt.shape[0]),
            in_specs=[
                pl.BlockSpec((hb, tq, HEAD_PAD), lambda h, s, qt_ref, kt_ref: (h, qt_ref[s], 0)),
                pl.BlockSpec((hb, tk, HEAD_PAD), lambda h, s, qt_ref, kt_ref: (h, kt_ref[s], 0)),
                pl.BlockSpec((hb, tk, LANE), lambda h, s, qt_ref, kt_ref: (h, kt_ref[s], 0)),
            ],
            out_specs=[
                pl.BlockSpec((tq, hb * LANE), lambda h, s, qt_ref, kt_ref: (qt_ref[s], h)),
                pl.BlockSpec((hb, tq, 1), lambda h, s, qt_ref, kt_ref: (h, qt_ref[s], 0)),
            ],
            scratch_shapes=[pltpu.VMEM((hb, tq, 1), F32), pltpu.VMEM((hb, tq, 2 * LANE), F32)],
        ),
        out_shape=[jax.ShapeDtypeStruct((t_len, h_n * LANE), F32), jax.ShapeDtypeStruct((h_n, t_len, 1), F32)],
        compiler_params=_cparams("parallel", "arbitrary"),
    )(qt, kt, q, k, v)


def flash_bwd(q, k, v, do, lse, delta):
    h_n, t_len, _ = q.shape
    tq = _flash_tile(t_len)
    nq = t_len // tq
    hb = FLASH_BWD_HEADS
    kw = 2 if nq % 2 == 0 else 1
    tk = kw * tq
    pairs = [(i, j) for j in range(nq // kw) for i in range(kw * j, nq)]
    n_steps = len(pairs)
    qt = jnp.array([p[0] for p in pairs], jnp.int32)
    kt = jnp.array([p[1] for p in pairs], jnp.int32)
    scale = QK ** -0.5
    c2 = scale * LOG2E

    def body(qt_ref, kt_ref, q_ref, k_ref, v_ref, do_ref, lse_ref, dl_ref, dq_hbm, dk_ref, dv_ref, dq_s, dq_sem):
        group, step = pl.program_id(0), pl.program_id(1)
        qi, kj = qt_ref[step], kt_ref[step]

        @pl.when(step == 0)
        def _():
            dq_s[...] = jnp.zeros_like(dq_s)

        def tile(diagonal):
            qb, kb = q_ref[...], k_ref[...]
            dob = jnp.stack([do_ref[:, LANE * hh:LANE * (hh + 1)] for hh in range(hb)])
            p = jnp.exp2(_bdot(qb, kb, _NT3, False) * c2 - lse_ref[...])
            if diagonal:
                p = jnp.where(_causal((qi % kw) * tq, (tq, tk))[None], p, 0.0)
            dv = _bdot(p, dob, _TN3, False)
            ds = p * (_bdot(dob, v_ref[...], _NT3, False) - dl_ref[...]) * scale
            dk = _bdot(ds, qb, _TN3, False)
            dq_s[:, pl.ds(pl.multiple_of(qi * tq, tq), tq), :] += _bdot(ds, kb, _NN3, False)
            return dk, dv

        @pl.when(qi == kw * kj)
        def _():
            dk_ref[...], dv_ref[...] = tile(True)

        @pl.when((qi != kw * kj) & (qi // kw == kj))
        def _():
            dk, dv = tile(True)
            dk_ref[...] += dk
            dv_ref[...] += dv

        @pl.when(qi // kw > kj)
        def _():
            dk, dv = tile(False)
            dk_ref[...] += dk
            dv_ref[...] += dv

        @pl.when(step == n_steps - 1)
        def _():
            out = pltpu.make_async_copy(dq_s, dq_hbm.at[pl.ds(group * hb, hb)], dq_sem)
            out.start()
            out.wait()

    def qmap(h, s, qt_ref, kt_ref):
        return (h, qt_ref[s], 0)

    def kmap(h, s, qt_ref, kt_ref):
        return (h, kt_ref[s], 0)

    return pl.pallas_call(
        body,
        name="flash_bwd",
        grid_spec=pltpu.PrefetchScalarGridSpec(
            num_scalar_prefetch=2,
            grid=(h_n // hb, n_steps),
            in_specs=[
                pl.BlockSpec((hb, tq, HEAD_PAD), qmap),
                pl.BlockSpec((hb, tk, HEAD_PAD), kmap),
                pl.BlockSpec((hb, tk, LANE), kmap),
                pl.BlockSpec((tq, hb * LANE), lambda h, s, qt_ref, kt_ref: (qt_ref[s], h)),
                pl.BlockSpec((hb, tq, 1), qmap),
                pl.BlockSpec((hb, tq, 1), qmap),
            ],
            out_specs=[
                pl.BlockSpec(memory_space=pl.ANY),
                pl.BlockSpec((hb, tk, HEAD_PAD), kmap),
                pl.BlockSpec((hb, tk, LANE), kmap),
            ],
            scratch_shapes=[pltpu.VMEM((hb, t_len, HEAD_PAD), F32), pltpu.SemaphoreType.DMA],
        ),
        out_shape=[
            jax.ShapeDtypeStruct((h_n, t_len, HEAD_PAD), F32),
            jax.ShapeDtypeStruct((h_n, t_len, HEAD_PAD), F32),
            jax.ShapeDtypeStruct((h_n, t_len, LANE), F32),
        ],
        compiler_params=_cparams("parallel", "arbitrary"),
    )(qt, kt, q, k, v, do, lse, delta)


def gdn_step(s, q, k, v, gb, bb, mm3=_bh3):
    h_n, c = q.shape[0], CHUNK
    ii = lax.broadcasted_iota(jnp.int32, (1, c, c), 1)
    jj = lax.broadcasted_iota(jnp.int32, (1, c, c), 2)
    incl, strict = ii >= jj, ii > jj
    gcb = _bhi(jnp.broadcast_to(incl.astype(F32), (h_n, c, c)), gb)
    lane = lax.broadcasted_iota(jnp.int32, (1, 1, LANE), 2)
    e0, e1 = (lane == 0).astype(F32), (lane == 1).astype(F32)
    diff = _bhi_nt(gcb * e0 + e1, e0 - gcb * e1)
    decay = jnp.where(incl, jnp.exp(jnp.where(incl, diff, 0.0)), 0.0)
    kb, vb = k * bb, v * bb
    egc = jnp.exp(gcb)
    lmat = jnp.where(strict, _bmm_nt(kb, k) * decay, 0.0)
    inv = (ii == jj).astype(F32) - lmat
    pw = mm3(lmat, lmat)
    for step in range(5):
        inv = inv + mm3(inv, pw)
        if step < 4:
            pw = mm3(pw, pw)
    u = mm3(inv, vb)
    w = mm3(inv, kb * egc)
    attn = _bmm_nt(q, k) * decay
    qd = q * egc
    g_end = jnp.sum(gb, axis=1, keepdims=True)
    kd = k * jnp.exp(g_end - gcb)
    v_new = u - _bmm(w, s)
    o = _bmm(qd, s) + _bmm(attn, v_new)
    s_new = s * jnp.exp(g_end) + _bmm_tn(kd, v_new)
    return s_new, o


def gdn_fwd(q, k, v, gb, bb):
    h_n, t_len, d = q.shape
    n = t_len // CHUNK
    blk = pl.BlockSpec((h_n, CHUNK, d), lambda i: (0, i, 0))

    def body(q_ref, k_ref, v_ref, g_ref, b_ref, o_ref, sall_ref, s_s):
        @pl.when(pl.program_id(0) == 0)
        def _():
            s_s[...] = jnp.zeros_like(s_s)

        s = s_s[...]
        sall_ref[0] = s
        s_new, o = gdn_step(s, q_ref[...], k_ref[...], v_ref[...], g_ref[...], b_ref[...], mm3=_bh3_passes)
        o_ref[...] = o
        s_s[...] = s_new

    return pl.pallas_call(
        body,
        name="gdn_fwd",
        grid=(n,),
        in_specs=[blk] * 5,
        out_specs=[blk, pl.BlockSpec((1, h_n, d, d), lambda i: (i, 0, 0, 0))],
        out_shape=[jax.ShapeDtypeStruct((h_n, t_len, d), F32), jax.ShapeDtypeStruct((n, h_n, d, d), F32)],
        scratch_shapes=[pltpu.VMEM((h_n, d, d), F32)],
        compiler_params=_cparams("arbitrary"),
    )(q, k, v, gb, bb)


def gdn_bwd(q, k, v, gb, bb, s_all, do):
    h_n, t_len, d = q.shape
    n = t_len // CHUNK
    blk = pl.BlockSpec((h_n, CHUNK, d), lambda i: (0, n - 1 - i, 0))

    def body(q_ref, k_ref, v_ref, g_ref, b_ref, sall_ref, do_ref, dq_ref, dk_ref, dv_ref, dg_ref, db_ref, ds_s):
        @pl.when(pl.program_id(0) == 0)
        def _():
            ds_s[...] = jnp.zeros_like(ds_s)

        _, pull = jax.vjp(gdn_step, sall_ref[0], q_ref[...], k_ref[...], v_ref[...], g_ref[...], b_ref[...])
        ds, dq, dk, dv, dg, db = pull((ds_s[...], do_ref[...]))
        ds_s[...] = ds
        dq_ref[...], dk_ref[...], dv_ref[...], dg_ref[...], db_ref[...] = dq, dk, dv, dg, db

    return pl.pallas_call(
        body,
        name="gdn_bwd",
        grid=(n,),
        in_specs=[blk] * 5 + [pl.BlockSpec((1, h_n, d, d), lambda i: (n - 1 - i, 0, 0, 0)), blk],
        out_specs=[blk] * 5,
        out_shape=[jax.ShapeDtypeStruct((h_n, t_len, d), F32)] * 5,
        scratch_shapes=[pltpu.VMEM((h_n, d, d), F32)],
        compiler_params=_cparams("arbitrary"),
    )(q, k, v, gb, bb, s_all, do)


def _pad_cols(a, n):
    return jnp.pad(a, ((0, 0), (0, n - a.shape[1])))


def arrange_w_in(w):
    pieces, start = [], 0
    for n in R_SPLITS:
        pieces.append(w[:, start:start + n])
        start += n
    cq, ckv, kr, mgate, gq, gk, gv, ga, gb, ggate = pieces
    return jnp.concatenate([mgate, gq, gk, gv, ggate, cq, ckv, _pad_cols(kr, LANE),
                            _pad_cols(jnp.concatenate([ga, gb], axis=1), LANE)], axis=1)


def unarrange_w_in(g):
    def cols(start, n):
        return g[:, start:start + n]
    return jnp.concatenate([cols(P_CQ, Q_LORA), cols(P_CKV, KV_LORA), cols(P_KR, ROPE), cols(P_MGATE, WIDTH),
                            cols(P_GQ, WIDTH), cols(P_GK, WIDTH), cols(P_GV, WIDTH), cols(P_GAB, HEADS),
                            cols(P_GAB + HEADS, HEADS), cols(P_GGATE, WIDTH)], axis=1)


def arrange_w_uq(w):
    w = w.reshape(w.shape[0], HEADS, QK)
    return jnp.pad(w, ((0, 0), (0, 0), (0, HEAD_PAD - QK))).reshape(w.shape[0], HEADS * HEAD_PAD)


def unarrange_w_uq(g):
    return g.reshape(g.shape[0], HEADS, HEAD_PAD)[:, :, :QK].reshape(g.shape[0], HEADS * QK)


def local_step(x, pos, tgt, p):
    t_len = x.shape[0]
    w_in, w_uq, w_ukv, w_out = p["w_in"], p["w_uq"], p["w_ukv"], p["w_out"]
    norm_gain = p["norm_gain"].reshape(1, D_MODEL)
    qa_gain = p["mla_q_a_gain"].reshape(1, Q_LORA)
    kva_gain = p["mla_kv_a_gain"].reshape(1, KV_LORA)
    qg = _pad_cols(p["mla_q_norm_gain"].reshape(1, QK), HEAD_PAD)
    kg = _pad_cols(p["mla_k_norm_gain"].reshape(1, QK), HEAD_PAD)
    cw = p["gdn_conv_w"].reshape(CONV_W, 3 * WIDTH)
    cwq, cwk, cwv = cw[:, :WIDTH], cw[:, WIDTH:2 * WIDTH], cw[:, 2 * WIDTH:]
    alog = _pad_cols(p["gdn_a_log"].reshape(1, HEADS), LANE)
    dtb = _pad_cols(p["gdn_dt_bias"].reshape(1, HEADS), LANE)
    og = p["gdn_out_norm_gain"].reshape(1, GDN_DIM)
    half = ROPE // 2
    inv_freq = jnp.power(ROPE_THETA, -jnp.arange(half, dtype=F32) / half)
    invf = _pad_cols(jnp.concatenate([inv_freq, inv_freq]).reshape(1, ROPE), LANE)

    rt = 256
    r = "r"
    (xn,) = rowwise("rms_x", f_rms_x, t_len, rt, [(x, (r, D_MODEL, 0))], [norm_gain], [(r, D_MODEL, _BF)])
    proj = matmul("proj", xn, w_in, "nn")
    cq_in = (proj, (r, Q_LORA, P_CQ // Q_LORA))
    ckv_in = (proj, (r, KV_LORA, P_CKV // KV_LORA))
    kr_in = (proj, (r, LANE, P_KR // LANE))
    mgate_in = (proj, (r, WIDTH, P_MGATE // WIDTH))
    ggate_in = (proj, (r, WIDTH, P_GGATE // WIDTH))
    gqkv_in = [(proj, (r, WIDTH, P_GQ // WIDTH)), (proj, (r, WIDTH, P_GK // WIDTH)), (proj, (r, WIDTH, P_GV // WIDTH))]
    gab_in = (proj, (r, LANE, P_GAB // LANE))
    halos = [(proj, ("halo", WIDTH, P_GQ // WIDTH)), (proj, ("halo", WIDTH, P_GK // WIDTH)),
             (proj, ("halo", WIDTH, P_GV // WIDTH))]

    q_lat, kv_lat = rowwise("lat", f_lat, t_len, rt, [cq_in, ckv_in], [qa_gain, kva_gain],
                            [(r, Q_LORA, _BF), (r, KV_LORA, _BF)])
    q_raw = matmul("q_up", q_lat, w_uq, "nn")
    kv_raw = matmul("kv_up", kv_lat, w_ukv, "nn")
    wide = HEADS * HEAD_PAD
    head_in = [(q_raw, (r, wide, 0)), (kv_raw, (r, wide, 0)), kr_in]
    pos_in = (pos, (r, 1, 0))
    q_full, k_full, v_mla = rowwise(
        "head", lambda i, qr, kvr, kr, ps, qg_, kg_, iv: f_head(i, qr, kvr, kr, qg_, kg_, ps, iv), t_len, rt,
        head_in + [pos_in], [qg, kg, invf],
        [("h", HEADS, HEAD_PAD, _BF), ("h", HEADS, HEAD_PAD, _BF), ("h", HEADS, LANE, _BF)])
    o_mla, lse = flash_fwd(q_full, k_full, v_mla)

    pre_in = gqkv_in + [gab_in] + halos
    pre_full = [cwq, cwk, cwv, alog, dtb]
    hkind = ("h", HEADS, GDN_DIM, F32)
    gq_n, gk_n, gv_n, g_b, b_b = rowwise("gdn_pre", f_gdn_pre, t_len, rt, pre_in, pre_full, [hkind] * 5)
    o_gdn, s_all = gdn_fwd(gq_n, gk_n, gv_n, g_b, b_b)

    mix_in = [(o_mla, (r, WIDTH, 0)), mgate_in, (o_gdn, ("h",)), ggate_in]
    (mixed,) = rowwise("mix", f_mix, t_len, rt, mix_in, [og], [(r, 2 * WIDTH, _BF)])
    h_out = matmul("out_proj", mixed, w_out, "nn")
    dy, dy_mx, loss_acc = rowwise("loss", f_loss, t_len, rt,
                                  [(x, (r, D_MODEL, 0)), (h_out, (r, D_MODEL, 0)), (tgt, (r, D_MODEL, 0))], [],
                                  [(r, D_MODEL, F32), (r, D_MODEL, _BF)], [(SUBLANE, LANE)])
    loss = loss_acc[0, 0]

    d_mixed = matmul("d_mixed", dy_mx, w_out, "nt")
    g_w_out = matmul("g_w_out", mixed, dy_mx, "tn")

    def mix_bwd(i, o_mla_, mgate_, o_gdn_, ggate_, d_mixed_, og_):
        do_mla_, d_mgate_, do_gdn_, d_ggate_, g_og_ = _vjp_fn(f_mix, 5, 1)(i, o_mla_, mgate_, o_gdn_, ggate_, og_, d_mixed_)
        delta_ = jnp.stack([jnp.sum(o_mla_[:, LANE * h:LANE * (h + 1)] * do_mla_[:, LANE * h:LANE * (h + 1)],
                                    axis=-1, keepdims=True) for h in range(HEADS)])
        return do_mla_, d_mgate_, do_gdn_, d_ggate_, delta_, g_og_

    do_mla, d_mgate, do_gdn, d_ggate, delta, g_og = rowwise(
        "mix_bwd", mix_bwd, t_len, rt, mix_in + [(d_mixed, (r, 2 * WIDTH, 0))], [og],
        [(r, WIDTH, F32), (r, WIDTH, _BF), hkind, (r, WIDTH, _BF), ("h", HEADS, 1, F32)], [(1, GDN_DIM)])
    dq_n, dk_n, dv_n, dg_b, db_b = gdn_bwd(gq_n, gk_n, gv_n, g_b, b_b, s_all, do_gdn)
    cts_in = [(a, ("h",)) for a in (dq_n, dk_n, dv_n, dg_b, db_b)]
    d_gq, d_gk, d_gv, d_gab, g_cwq, g_cwk, g_cwv, g_alog, g_dtb = rowwise(
        "gdn_pre_bwd", gdn_pre_bwd, t_len, rt, pre_in + cts_in, pre_full,
        [(r, WIDTH, _BF)] * 3 + [(r, LANE, _BF)],
        [(CONV_W, WIDTH)] * 3 + [(1, LANE)] * 2, carries=[(SUBLANE, WIDTH)] * 3, reverse=True)

    dq_full, dk_full, dv_mla = flash_bwd(q_full, k_full, v_mla, do_mla, lse, delta)
    head_cts = [(a, ("h",)) for a in (dq_full, dk_full, dv_mla)]

    def head_bwd(i, q_raw_, kv_raw_, kr_, pos_, dq_, dk_, dv_, qg_, kg_, invf_):
        return _vjp_fn(f_head, 5, 3)(i, q_raw_, kv_raw_, kr_, qg_, kg_, pos_, invf_, dq_, dk_, dv_)

    dq_raw, dkv_raw, d_kr, g_qg, g_kg = rowwise(
        "head_bwd", head_bwd, t_len, rt // 2, head_in + [pos_in] + head_cts, [qg, kg, invf],
        [(r, wide, _BF), (r, wide, _BF), (r, LANE, _BF)], [(1, HEAD_PAD), (1, HEAD_PAD)])
    dq_lat = matmul("dq_lat", dq_raw, w_uq, "nt")
    g_w_uq = matmul("g_w_uq", q_lat, dq_raw, "tn")
    dkv_lat = matmul("dkv_lat", dkv_raw, w_ukv, "nt")
    g_w_ukv = matmul("g_w_ukv", kv_lat, dkv_raw, "tn")

    def lat_bwd(i, cq_, ckv_, dql_, dkl_, gq_, gkv_):
        return _vjp_fn(f_lat, 4, 2)(i, cq_, ckv_, gq_, gkv_, dql_, dkl_)

    d_cq, d_ckv, g_qa, g_kva = rowwise(
        "lat_bwd", lat_bwd, t_len, rt, [cq_in, ckv_in, (dq_lat, (r, Q_LORA, 0)), (dkv_lat, (r, KV_LORA, 0))],
        [qa_gain, kva_gain], [(r, Q_LORA, _BF), (r, KV_LORA, _BF)], [(1, Q_LORA), (1, KV_LORA)])

    d_proj = jnp.concatenate([d_mgate, d_gq, d_gk, d_gv, d_ggate, d_cq, d_ckv, d_kr, d_gab], axis=1)
    d_xn = matmul("d_xn", d_proj, w_in, "nt")
    g_w_in = matmul("g_w_in", xn, d_proj, "tn")

    def rms_x_bwd(i, x_, dxn_, dy_, gain_):
        dx, dgain = _vjp_fn(f_rms_x, 2, 1)(i, x_, gain_, dxn_)
        return dx + dy_, dgain

    grad_x, g_norm = rowwise("rms_x_bwd", rms_x_bwd, t_len, rt,
                             [(x, (r, D_MODEL, 0)), (d_xn, (r, D_MODEL, 0)), (dy, (r, D_MODEL, 0))], [norm_gain],
                             [(r, D_MODEL, F32)], [(1, D_MODEL)])

    grads = {
        "norm_gain": g_norm, "w_in": g_w_in, "mla_q_a_gain": g_qa, "mla_kv_a_gain": g_kva, "w_uq": g_w_uq,
        "w_ukv": g_w_ukv, "mla_q_norm_gain": g_qg[:, :QK], "mla_k_norm_gain": g_kg[:, :QK],
        "gdn_conv_w": jnp.concatenate([g_cwq, g_cwk, g_cwv], axis=1), "gdn_a_log": g_alog[:, :HEADS],
        "gdn_dt_bias": g_dtb[:, :HEADS], "gdn_out_norm_gain": g_og, "w_out": g_w_out,
    }
    return loss, grad_x, grads


MESH = pl.DeviceIdType.MESH
ANY = pl.BlockSpec(memory_space=pl.ANY)
CHIP_FLIPS = ((1, 0), (0, 1), (1, 1))


def _place():
    return lax.axis_index("x"), lax.axis_index("y"), lax.axis_index("c")


def _flip(v, f):
    return 1 - v if f else v


def all_gather(shards):
    n_arr = len(shards)

    def body(*refs):
        x_refs, o_refs = refs[:n_arr], refs[n_arr:2 * n_arr]
        send_sems, recv_sems, local_sems = refs[2 * n_arr:]
        x, y, c = _place()
        me, sibling = (x, y, c), (x, y, 1 - c)
        chips = [(_flip(x, fx), _flip(y, fy)) for fx, fy in CHIP_FLIPS]

        def copy(a, k, block, to, src=None):
            px, py, pc = block
            dst = o_refs[a].at[4 * px + 2 * py + pc]
            return pltpu.make_async_remote_copy(
                src_ref=dst if src is None else src, dst_ref=dst, send_sem=send_sems.at[a, k],
                recv_sem=recv_sems.at[a, k], device_id=to, device_id_type=MESH)

        mine, first, passed = [], [], []
        for a in range(n_arr):
            cp = pltpu.make_async_copy(x_refs[a], o_refs[a].at[4 * x + 2 * y + c], local_sems.at[a])
            cp.start()
            mine.append(cp)
            first.append(copy(a, 0, me, sibling, src=x_refs[a]))
            first += [copy(a, 1 + j, me, (*chip, c), src=x_refs[a]) for j, chip in enumerate(chips)]
        for cp in first:
            cp.start()
        for j, chip in enumerate(chips):
            for a in range(n_arr):
                copy(a, 1 + j, (*chip, c), me).wait_recv()
                cp = copy(a, 4 + j, (*chip, c), sibling)
                cp.start()
                passed.append(cp)
        for a in range(n_arr):
            copy(a, 0, sibling, me).wait_recv()
            for j, chip in enumerate(chips):
                copy(a, 4 + j, (*chip, 1 - c), me).wait_recv()
        for cp in first + passed:
            cp.wait_send()
        for cp in mine:
            cp.wait()

    return pl.pallas_call(
        body,
        name="all_gather",
        out_shape=[jax.ShapeDtypeStruct((N_DEV,) + s.shape, s.dtype) for s in shards],
        in_specs=[ANY] * n_arr,
        out_specs=[ANY] * n_arr,
        scratch_shapes=[pltpu.SemaphoreType.DMA((n_arr, 7)), pltpu.SemaphoreType.DMA((n_arr, 7)),
                        pltpu.SemaphoreType.DMA((n_arr,))],
    )(*shards)


def exchange_cores(grads):
    n_arr = len(grads)

    def body(*refs):
        g_refs, o_refs = refs[:n_arr], refs[n_arr:2 * n_arr]
        send_sems, recv_sems = refs[2 * n_arr:]
        x, y, c = _place()
        copies = []
        for a in range(n_arr):
            for q in range(4):
                cp = pltpu.make_async_remote_copy(
                    src_ref=g_refs[a].at[2 * q + (1 - c)], dst_ref=o_refs[a].at[q], send_sem=send_sems.at[a, q],
                    recv_sem=recv_sems.at[a, q], device_id=(x, y, 1 - c), device_id_type=MESH)
                cp.start()
                copies.append(cp)
        for cp in copies:
            cp.wait()

    return pl.pallas_call(
        body,
        name="exchange_cores",
        out_shape=[jax.ShapeDtypeStruct((4,) + g.shape[1:], g.dtype) for g in grads],
        in_specs=[ANY] * n_arr,
        out_specs=[ANY] * n_arr,
        scratch_shapes=[pltpu.SemaphoreType.DMA((n_arr, 4)), pltpu.SemaphoreType.DMA((n_arr, 4))],
    )(*grads)


def exchange_chips(parts):
    n_arr = len(parts)

    def body(*refs):
        p_refs, o_refs = refs[:n_arr], refs[n_arr:2 * n_arr]
        send_sems, recv_sems = refs[2 * n_arr:]
        x, y, c = _place()
        copies = []
        for a in range(n_arr):
            for j, (fx, fy) in enumerate(CHIP_FLIPS):
                px, py = _flip(x, fx), _flip(y, fy)
                cp = pltpu.make_async_remote_copy(
                    src_ref=p_refs[a].at[2 * px + py], dst_ref=o_refs[a].at[j], send_sem=send_sems.at[a, j],
                    recv_sem=recv_sems.at[a, j], device_id=(px, py, c), device_id_type=MESH)
                cp.start()
                copies.append(cp)
        for cp in copies:
            cp.wait()

    return pl.pallas_call(
        body,
        name="exchange_chips",
        out_shape=[jax.ShapeDtypeStruct((3,) + p.shape[1:], p.dtype) for p in parts],
        in_specs=[ANY] * n_arr,
        out_specs=[ANY] * n_arr,
        scratch_shapes=[pltpu.SemaphoreType.DMA((n_arr, 3)), pltpu.SemaphoreType.DMA((n_arr, 3))],
    )(*parts)


def gather_small(v):
    def body(v_ref, o_ref, send_sems, recv_sems, local_sem):
        x, y, c = _place()
        me = 4 * x + 2 * y + c
        mine = pltpu.make_async_copy(v_ref, o_ref.at[me], local_sem)
        mine.start()
        copies = []
        for k in range(1, N_DEV):
            fx, fy, fc = (k >> 2) & 1, (k >> 1) & 1, k & 1
            cp = pltpu.make_async_remote_copy(
                src_ref=v_ref, dst_ref=o_ref.at[me], send_sem=send_sems.at[k - 1], recv_sem=recv_sems.at[k - 1],
                device_id=(_flip(x, fx), _flip(y, fy), _flip(c, fc)), device_id_type=MESH)
            cp.start()
            copies.append(cp)
        for cp in copies:
            cp.wait()
        mine.wait()

    return pl.pallas_call(
        body,
        name="gather_small",
        out_shape=jax.ShapeDtypeStruct((N_DEV,) + v.shape, v.dtype),
        in_specs=[ANY],
        out_specs=ANY,
        scratch_shapes=[pltpu.SemaphoreType.DMA((N_DEV - 1,)), pltpu.SemaphoreType.DMA((N_DEV - 1,)),
                        pltpu.SemaphoreType.DMA],
    )(v)


def _row_tile(rows):
    for t in (256, 128, 64, 32, 16, 8):
        if rows % t == 0:
            return t
    return rows


def add_core_parts(name, g, recv, c_idx, wire):
    _, rows, cols = g.shape
    tr = _row_tile(rows)

    def body(c_ref, g_ref, r_ref, o_ref, w_ref):
        part = g_ref[...] + r_ref[...]
        o_ref[...] = part
        w_ref[...] = part.astype(w_ref.dtype)

    blk = pl.BlockSpec((1, tr, cols), lambda q, i, c_ref: (q, i, 0))
    return pl.pallas_call(
        body,
        name=name,
        grid_spec=pltpu.PrefetchScalarGridSpec(
            num_scalar_prefetch=1,
            grid=(4, rows // tr),
            in_specs=[pl.BlockSpec((1, tr, cols), lambda q, i, c_ref: (2 * q + c_ref[0], i, 0)), blk],
            out_specs=[blk, blk],
        ),
        out_shape=[jax.ShapeDtypeStruct((4, rows, cols), F32), jax.ShapeDtypeStruct((4, rows, cols), wire)],
        compiler_params=_cparams("parallel", "parallel"),
    )(c_idx, g, recv)


def _adamw(w, g, m, v):
    m = ADAM_B1 * m + (1.0 - ADAM_B1) * g
    v = ADAM_B2 * v + (1.0 - ADAM_B2) * (g * g)
    m_hat = m / (1.0 - ADAM_B1 ** ADAM_STEP)
    v_hat = v / (1.0 - ADAM_B2 ** ADAM_STEP)
    delta = -ADAM_LR * (m_hat / (jnp.sqrt(v_hat) + ADAM_EPS) + ADAM_WD * w)
    return delta, m, v


def adamw_sharded(name, parts, recv, q_idx, w, m, v):
    rows, cols = w.shape
    tr = _row_tile(rows)

    def body(q_ref, p_ref, r_ref, w_ref, m_ref, v_ref, g_out, d_out, m_out, v_out):
        g = p_ref[0] + r_ref[0].astype(F32) + r_ref[1].astype(F32) + r_ref[2].astype(F32)
        d, m_new, v_new = _adamw(w_ref[...], g, m_ref[...], v_ref[...])
        g_out[...], d_out[...], m_out[...], v_out[...] = g, d, m_new, v_new

    blk = pl.BlockSpec((tr, cols), lambda i, q_ref: (i, 0))
    return pl.pallas_call(
        body,
        name=name,
        grid_spec=pltpu.PrefetchScalarGridSpec(
            num_scalar_prefetch=1,
            grid=(rows // tr,),
            in_specs=[pl.BlockSpec((1, tr, cols), lambda i, q_ref: (q_ref[0], i, 0)),
                      pl.BlockSpec((3, tr, cols), lambda i, q_ref: (0, i, 0)), blk, blk, blk],
            out_specs=[blk] * 4,
        ),
        out_shape=[jax.ShapeDtypeStruct((rows, cols), F32)] * 4,
        compiler_params=_cparams("parallel"),
    )(q_idx, parts, recv, w, m, v)


def adamw_small(gathered, w, m, v):
    def body(g_ref, w_ref, m_ref, v_ref, g_out, d_out, m_out, v_out):
        g = g_ref[0]
        for j in range(1, N_DEV):
            g = g + g_ref[j]
        d, m_new, v_new = _adamw(w_ref[...], g, m_ref[...], v_ref[...])
        g_out[...], d_out[...], m_out[...], v_out[...] = g, d, m_new, v_new

    return pl.pallas_call(body, name="adamw_small", out_shape=[jax.ShapeDtypeStruct(w.shape, F32)] * 4)(gathered, w, m, v)


SHARDED = ("w_in", "w_uq", "w_ukv", "gdn_conv_w", "w_out")
SMALL = (("norm_gain", D_MODEL), ("mla_q_a_gain", Q_LORA), ("mla_kv_a_gain", KV_LORA), ("mla_q_norm_gain", QK),
         ("mla_k_norm_gain", QK), ("gdn_a_log", HEADS), ("gdn_dt_bias", HEADS), ("gdn_out_norm_gain", GDN_DIM))
WEIGHT_ORDER = ("norm_gain", "w_in", "mla_q_a_gain", "mla_kv_a_gain", "w_uq", "w_ukv", "mla_q_norm_gain",
                "mla_k_norm_gain", "gdn_conv_w", "gdn_a_log", "gdn_dt_bias", "gdn_out_norm_gain", "w_out")


def _pack_small(d):
    rows = []
    for name, n in SMALL:
        a = d[name].reshape(-1).astype(F32)
        n_pad = -(-n // LANE) * LANE
        rows.append(jnp.pad(a, (0, n_pad - n)).reshape(n_pad // LANE, LANE))
    packed = jnp.concatenate(rows, axis=0)
    return jnp.pad(packed, ((0, -packed.shape[0] % SUBLANE), (0, 0)))


def _unpack_small(packed):
    out, row = {}, 0
    for name, n in SMALL:
        n_rows = -(-n // LANE)
        out[name] = packed[row:row + n_rows].reshape(-1)[:n].reshape(1, n)
        row += n_rows
    return out


def kernel(x, positions, norm_gain, w_in, mla_q_a_gain, mla_kv_a_gain, w_uq, w_ukv, mla_q_norm_gain, mla_k_norm_gain, gdn_conv_w, gdn_a_log, gdn_dt_bias, gdn_out_norm_gain, w_out, loss_target, m_norm_gain, m_w_in, m_mla_q_a_gain, m_mla_kv_a_gain, m_w_uq, m_w_ukv, m_mla_q_norm_gain, m_mla_k_norm_gain, m_gdn_conv_w, m_gdn_a_log, m_gdn_dt_bias, m_gdn_out_norm_gain, m_w_out, v_norm_gain, v_w_in, v_mla_q_a_gain, v_mla_kv_a_gain, v_w_uq, v_w_ukv, v_mla_q_norm_gain, v_mla_k_norm_gain, v_gdn_conv_w, v_gdn_a_log, v_gdn_dt_bias, v_gdn_out_norm_gain, v_w_out):
    w = dict(norm_gain=norm_gain, w_in=w_in, mla_q_a_gain=mla_q_a_gain, mla_kv_a_gain=mla_kv_a_gain, w_uq=w_uq,
             w_ukv=w_ukv, mla_q_norm_gain=mla_q_norm_gain, mla_k_norm_gain=mla_k_norm_gain, gdn_conv_w=gdn_conv_w,
             gdn_a_log=gdn_a_log, gdn_dt_bias=gdn_dt_bias, gdn_out_norm_gain=gdn_out_norm_gain, w_out=w_out)
    m = dict(norm_gain=m_norm_gain, w_in=m_w_in, mla_q_a_gain=m_mla_q_a_gain, mla_kv_a_gain=m_mla_kv_a_gain,
             w_uq=m_w_uq, w_ukv=m_w_ukv, mla_q_norm_gain=m_mla_q_norm_gain, mla_k_norm_gain=m_mla_k_norm_gain,
             gdn_conv_w=m_gdn_conv_w, gdn_a_log=m_gdn_a_log, gdn_dt_bias=m_gdn_dt_bias,
             gdn_out_norm_gain=m_gdn_out_norm_gain, w_out=m_w_out)
    v = dict(norm_gain=v_norm_gain, w_in=v_w_in, mla_q_a_gain=v_mla_q_a_gain, mla_kv_a_gain=v_mla_kv_a_gain,
             w_uq=v_w_uq, w_ukv=v_w_ukv, mla_q_norm_gain=v_mla_q_norm_gain, mla_k_norm_gain=v_mla_k_norm_gain,
             gdn_conv_w=v_gdn_conv_w, gdn_a_log=v_gdn_a_log, gdn_dt_bias=v_gdn_dt_bias,
             gdn_out_norm_gain=v_gdn_out_norm_gain, w_out=v_w_out)
    t_len = x.shape[1]

    shards = [w[n][0] if n == "gdn_conv_w" else w[n][0].astype(_BF) for n in SHARDED]
    a_w_in, a_w_uq, a_w_ukv, a_cw, a_w_out = all_gather(shards)

    def cols_whole(g):
        return g.transpose(1, 0, 2).reshape(g.shape[1], N_DEV * g.shape[2])

    p = {n: w[n] for n, _ in SMALL}
    p["w_in"] = arrange_w_in(cols_whole(a_w_in))
    p["w_uq"] = arrange_w_uq(cols_whole(a_w_uq))
    p["w_ukv"] = cols_whole(a_w_ukv)
    p["gdn_conv_w"] = cols_whole(a_cw)
    p["w_out"] = a_w_out.reshape(N_DEV * a_w_out.shape[1], a_w_out.shape[2])

    pos = positions.reshape(t_len, 1).astype(F32)
    loss, grad_x, grads = local_step(x.reshape(t_len, D_MODEL), pos, loss_target.reshape(t_len, D_MODEL), p)
    loss = lax.psum(loss, ("x", "y", "c"))

    def col_blocks(g):
        return g.reshape(g.shape[0], N_DEV, g.shape[1] // N_DEV).transpose(1, 0, 2)

    blocks = [col_blocks(unarrange_w_in(grads["w_in"])), col_blocks(unarrange_w_uq(grads["w_uq"])),
              col_blocks(grads["w_ukv"]), col_blocks(grads["gdn_conv_w"]),
              grads["w_out"].reshape(N_DEV, D_MODEL // N_DEV, D_MODEL)]
    xi, yi, ci = _place()
    c_idx = jnp.reshape(ci, (1,)).astype(jnp.int32)
    q_idx = jnp.reshape(2 * xi + yi, (1,)).astype(jnp.int32)
    from_sibling = exchange_cores(blocks)
    parts = [add_core_parts("add_" + n, g, r, c_idx, F32 if n == "gdn_conv_w" else _BF)
             for n, g, r in zip(SHARDED, blocks, from_sibling)]
    from_chips = exchange_chips([wire for _, wire in parts])
    out = {}
    for n, (prt, _), rcv in zip(SHARDED, parts, from_chips):
        shape = w[n].shape
        res = adamw_sharded("adamw_" + n, prt, rcv, q_idx, w[n].reshape(shape[-2:]), m[n].reshape(shape[-2:]),
                            v[n].reshape(shape[-2:]))
        out[n] = [a.reshape(shape) for a in res]

    small_all = gather_small(_pack_small(grads))
    res = adamw_small(small_all, _pack_small(w), _pack_small(m), _pack_small(v))
    unpacked = [_unpack_small(a) for a in res]
    for n, _ in SMALL:
        out[n] = [u[n] for u in unpacked]

    return (loss, grad_x.reshape(x.shape), *[out[n][0] for n in WEIGHT_ORDER], *[out[n][1] for n in WEIGHT_ORDER],
            *[out[n][2] for n in WEIGHT_ORDER], *[out[n][3] for n in WEIGHT_ORDER])
```

```python
import functools

import jax
import jax.numpy as jnp
from jax import lax
from jax.experimental import pallas as pl
from jax.experimental.pallas import tpu as pltpu

F32 = jnp.float32
_BF = jnp.bfloat16
HI = lax.Precision.HIGHEST

D_MODEL = 2048
HEADS = 8
NOPE = 128
ROPE = 64
QK = NOPE + ROPE
Q_LORA = 512
KV_LORA = 256
HEAD_PAD = 256
GDN_DIM = 128
WIDTH = HEADS * 128
CONV_W = 4
CHUNK = 64
ROPE_THETA = 10000.0
EPS = 1e-6
N_DEV = 8
LANE = 128
SUBLANE = 8
VMEM_LIMIT = 48 * 1024 * 1024

ADAM_LR, ADAM_B1, ADAM_B2, ADAM_EPS, ADAM_WD, ADAM_STEP = 0.001, 0.9, 0.999, 1e-08, 0.01, 10

P_MGATE, P_GQ, P_GK, P_GV, P_GGATE = 0, 1024, 2048, 3072, 4096
P_CQ, P_CKV, P_KR, P_GAB = 5120, 5632, 5888, 6016
P_COLS = 6144
R_SPLITS = (512, 256, 64, 1024, 1024, 1024, 1024, 8, 8, 1024)


def _cparams(*sem):
    return pltpu.CompilerParams(dimension_semantics=sem, vmem_limit_bytes=VMEM_LIMIT)


def _d_nn(a, b):
    return jnp.dot(a.astype(_BF), b.astype(_BF), preferred_element_type=F32)


def _d_nt(a, b):
    return lax.dot_general(a.astype(_BF), b.astype(_BF), (((1,), (1,)), ((), ())), preferred_element_type=F32)


def _d_tn(a, b):
    return lax.dot_general(a.astype(_BF), b.astype(_BF), (((0,), (0,)), ((), ())), preferred_element_type=F32)


@jax.custom_vjp
def _mm(a, b):
    return _d_nn(a, b)


_mm.defvjp(lambda a, b: (_d_nn(a, b), (a, b)), lambda r, g: (_d_nt(g, r[1]), _d_tn(r[0], g)))


@jax.custom_vjp
def _mm_nt(a, b):
    return _d_nt(a, b)


_mm_nt.defvjp(lambda a, b: (_d_nt(a, b), (a, b)), lambda r, g: (_d_nn(g, r[1]), _d_tn(g, r[0])))


@jax.custom_vjp
def _mm_tn(a, b):
    return _d_tn(a, b)


_mm_tn.defvjp(lambda a, b: (_d_tn(a, b), (a, b)), lambda r, g: (_d_nt(r[1], g), _d_nn(r[0], g)))


def _hi(a, b):
    return jnp.dot(a, b, preferred_element_type=F32, precision=HI)


_NN3 = (((2,), (1,)), ((0,), (0,)))
_NT3 = (((2,), (2,)), ((0,), (0,)))
_TN3 = (((1,), (1,)), ((0,), (0,)))


def _bdot(a, b, dims, hi):
    if hi:
        return lax.dot_general(a, b, dims, preferred_element_type=F32, precision=hi)
    return lax.dot_general(a.astype(_BF), b.astype(_BF), dims, preferred_element_type=F32)


def _batched_matmuls(hi):
    nn = jax.custom_vjp(lambda a, b: _bdot(a, b, _NN3, hi))
    nt = jax.custom_vjp(lambda a, b: _bdot(a, b, _NT3, hi))
    tn = jax.custom_vjp(lambda a, b: _bdot(a, b, _TN3, hi))
    nn.defvjp(lambda a, b: (_bdot(a, b, _NN3, hi), (a, b)),
              lambda r, g: (_bdot(g, r[1], _NT3, hi), _bdot(r[0], g, _TN3, hi)))
    nt.defvjp(lambda a, b: (_bdot(a, b, _NT3, hi), (a, b)),
              lambda r, g: (_bdot(g, r[1], _NN3, hi), _bdot(g, r[0], _TN3, hi)))
    tn.defvjp(lambda a, b: (_bdot(a, b, _TN3, hi), (a, b)),
              lambda r, g: (_bdot(r[1], g, _NT3, hi), _bdot(r[0], g, _NN3, hi)))
    return nn, nt, tn


_bmm, _bmm_nt, _bmm_tn = _batched_matmuls(False)
_bhi, _bhi_nt, _bhi_tn = _batched_matmuls(HI)


def _split2(x):
    hi = x.astype(_BF)
    return hi, (x - hi.astype(F32)).astype(_BF)


def _pdot(a, b, mode):
    (a_hi, a_lo), (b_hi, b_lo) = _split2(a), _split2(b)
    a_ax, b_ax, dims = {"nn": (2, 1, _NN3), "nt": (2, 2, _NT3), "tn": (1, 1, _TN3)}[mode]
    lhs = jnp.concatenate([a_hi, a_lo, a_hi], axis=a_ax)
    rhs = jnp.concatenate([b_hi, b_hi, b_lo], axis=b_ax)
    return lax.dot_general(lhs, rhs, dims, preferred_element_type=F32)


def _packed_matmuls():
    nn = jax.custom_vjp(lambda a, b: _pdot(a, b, "nn"))
    nn.defvjp(lambda a, b: (_pdot(a, b, "nn"), (a, b)), lambda r, g: (_pdot(g, r[1], "nt"), _pdot(r[0], g, "tn")))
    return nn


_bh3 = _packed_matmuls()
_bh3_passes = _batched_matmuls(lax.Precision.HIGH)[0]


@functools.partial(jax.custom_vjp, nondiff_argnums=(1, 2))
def _roll(x, shift, axis):
    return pltpu.roll(x, shift, axis)


def _roll_fwd(x, shift, axis):
    return pltpu.roll(x, shift, axis), None


def _roll_bwd(shift, axis, _, g):
    n = g.shape[axis]
    return (pltpu.roll(g, (n - shift) % n, axis),)


_roll.defvjp(_roll_fwd, _roll_bwd)


def _rms(x, gain):
    return x * lax.rsqrt(jnp.mean(x * x, axis=-1, keepdims=True) + EPS) * gain


MM_TILE = 1024


def matmul(name, a, b, mode):
    if mode == "nn":
        (m, k), (k2, n) = a.shape, b.shape
    elif mode == "nt":
        (m, k), (n, k2) = a.shape, b.shape
    else:
        (k, m), (k2, n) = a.shape, b.shape
    assert k == k2, (name, a.shape, b.shape)
    tm, tn, tk = min(MM_TILE, m), min(MM_TILE, n), min(MM_TILE, k)
    assert m % tm == 0 and n % tn == 0 and k % tk == 0, (name, m, n, k)
    dot = {"nn": _d_nn, "nt": _d_nt, "tn": _d_tn}[mode]

    def body(a_ref, b_ref, o_ref):
        kk = pl.program_id(2)
        part = dot(a_ref[...], b_ref[...])

        @pl.when(kk == 0)
        def _():
            o_ref[...] = part

        @pl.when(kk != 0)
        def _():
            o_ref[...] += part

    if mode == "nn":
        a_spec = pl.BlockSpec((tm, tk), lambda j, i, kk: (i, kk))
        b_spec = pl.BlockSpec((tk, tn), lambda j, i, kk: (kk, j))
    elif mode == "nt":
        a_spec = pl.BlockSpec((tm, tk), lambda j, i, kk: (i, kk))
        b_spec = pl.BlockSpec((tn, tk), lambda j, i, kk: (j, kk))
    else:
        a_spec = pl.BlockSpec((tk, tm), lambda j, i, kk: (kk, i))
        b_spec = pl.BlockSpec((tk, tn), lambda j, i, kk: (kk, j))
    return pl.pallas_call(
        body,
        name=name,
        grid=(n // tn, m // tm, k // tk),
        in_specs=[a_spec, b_spec],
        out_specs=pl.BlockSpec((tm, tn), lambda j, i, kk: (i, j)),
        out_shape=jax.ShapeDtypeStruct((m, n), F32),
        compiler_params=_cparams("parallel", "parallel", "arbitrary"),
    )(a, b)


def rowwise(name, fn, t_len, tile, row_in, full_in, row_out, acc_out=(), carries=(), reverse=False):
    tile = min(tile, t_len)
    n = t_len // tile
    assert t_len % tile == 0 and tile % SUBLANE == 0
    n_in, n_ro, n_acc, n_car = len(row_in) + len(full_in), len(row_out), len(acc_out), len(carries)

    def ti(i):
        return (n - 1 - i) if reverse else i

    in_specs, args = [], []
    for arr, kind in row_in:
        if kind[0] == "r":
            in_specs.append(pl.BlockSpec((tile, kind[1]), lambda i, c=kind[2]: (ti(i), c)))
        elif kind[0] == "h":
            in_specs.append(pl.BlockSpec((arr.shape[0], tile, arr.shape[2]), lambda i: (0, ti(i), 0)))
        else:
            in_specs.append(pl.BlockSpec(
                (SUBLANE, kind[1]), lambda i, c=kind[2]: (jnp.maximum(ti(i) * (tile // SUBLANE) - 1, 0), c)))
        args.append(arr)
    for arr in full_in:
        in_specs.append(pl.BlockSpec(arr.shape, lambda i, nd=arr.ndim: (0,) * nd))
        args.append(arr)
    out_specs, out_shape = [], []
    for kind in row_out:
        if kind[0] == "r":
            out_specs.append(pl.BlockSpec((tile, kind[1]), lambda i: (ti(i), 0)))
            out_shape.append(jax.ShapeDtypeStruct((t_len, kind[1]), kind[2]))
        else:
            out_specs.append(pl.BlockSpec((kind[1], tile, kind[2]), lambda i: (0, ti(i), 0)))
            out_shape.append(jax.ShapeDtypeStruct((kind[1], t_len, kind[2]), kind[3]))
    for shp in acc_out:
        out_specs.append(pl.BlockSpec(shp, lambda i, nd=len(shp): (0,) * nd))
        out_shape.append(jax.ShapeDtypeStruct(shp, F32))

    def body(*refs):
        in_refs = refs[:n_in]
        ro_refs = refs[n_in:n_in + n_ro]
        acc_refs = refs[n_in + n_ro:n_in + n_ro + n_acc]
        car_refs = refs[n_in + n_ro + n_acc:]
        step = pl.program_id(0)
        if n_car:
            @pl.when(step == 0)
            def _():
                for r in car_refs:
                    r[...] = jnp.zeros_like(r)
        vals = [r[...].astype(F32) for r in in_refs] + [r[...] for r in car_refs]
        outs = fn(ti(step), *vals)
        assert len(outs) == n_ro + n_acc + n_car, (name, len(outs))
        for r, o in zip(ro_refs, outs[:n_ro]):
            r[...] = o.astype(r.dtype)
        for r, o in zip(acc_refs, outs[n_ro:n_ro + n_acc]):
            @pl.when(step == 0)
            def _(r=r, o=o):
                r[...] = o

            @pl.when(step != 0)
            def _(r=r, o=o):
                r[...] += o
        for r, o in zip(car_refs, outs[n_ro + n_acc:]):
            r[...] = o

    res = pl.pallas_call(
        body,
        name=name,
        grid=(n,),
        in_specs=in_specs,
        out_specs=out_specs,
        out_shape=out_shape,
        scratch_shapes=[pltpu.VMEM(s, F32) for s in carries],
        compiler_params=_cparams("arbitrary"),
    )(*args)
    return list(res)


def _vjp_fn(fn, n_diff, n_out):
    def g(i, *a):
        ins, cts = a[:len(a) - n_out], a[len(a) - n_out:]
        diff, rest = ins[:n_diff], ins[n_diff:]
        _, pull = jax.vjp(lambda *d: tuple(fn(i, *d, *rest)), *diff)
        return tuple(pull(tuple(cts)))

    return g


def f_rms_x(i, x, gain):
    return (_rms(x, gain),)


def f_lat(i, cq, ckv, gq, gkv):
    return _rms(cq, gq), _rms(ckv, gkv)


def _rope_tables(pos, invf):
    ang = pos * invf
    lane = lax.broadcasted_iota(jnp.int32, (1, LANE), 1)
    cosv, sinv = jnp.cos(ang), jnp.sin(ang)
    half = ROPE // 2
    c = jnp.where(lane < ROPE, cosv, 0.0)
    sa = jnp.where(lane < half, -sinv, 0.0)
    sb = jnp.where((lane >= half) & (lane < ROPE), sinv, 0.0)
    return c, sa, sb


def _rope(xh, tabs):
    c, sa, sb = tabs
    half = ROPE // 2
    return xh * c + _roll(xh, LANE - half, 1) * sa + _roll(xh, half, 1) * sb


def f_head(i, q_raw, kv_raw, kr, qg, kg, pos, invf):
    tabs = _rope_tables(pos, invf)
    qs, ks, vs = [], [], []
    kr_ss = jnp.sum(kr * kr, axis=-1, keepdims=True)
    for h in range(HEADS):
        lo = q_raw[:, HEAD_PAD * h:HEAD_PAD * h + NOPE]
        hi = q_raw[:, HEAD_PAD * h + NOPE:HEAD_PAD * (h + 1)]
        ss = jnp.sum(lo * lo, axis=-1, keepdims=True) + jnp.sum(hi * hi, axis=-1, keepdims=True)
        r = lax.rsqrt(ss * (1.0 / QK) + EPS)
        qs.append(jnp.concatenate([lo * r * qg[:, :NOPE], _rope(hi * r * qg[:, NOPE:], tabs)], axis=1))
        lo = kv_raw[:, 2 * NOPE * h:2 * NOPE * h + NOPE]
        ss = jnp.sum(lo * lo, axis=-1, keepdims=True) + kr_ss
        r = lax.rsqrt(ss * (1.0 / QK) + EPS)
        ks.append(jnp.concatenate([lo * r * kg[:, :NOPE], _rope(kr * r * kg[:, NOPE:], tabs)], axis=1))
        vs.append(kv_raw[:, 2 * NOPE * h + NOPE:2 * NOPE * (h + 1)])
    return jnp.stack(qs), jnp.stack(ks), jnp.stack(vs)


def f_mix(i, o_mla, mgate, o_gdn, ggate, og):
    parts = [o_mla * jax.nn.silu(mgate)]
    for h in range(HEADS):
        parts.append(_rms(o_gdn[h], og) * jax.nn.silu(ggate[:, LANE * h:LANE * (h + 1)]))
    return (jnp.concatenate(parts, axis=1),)


def _row(a, j):
    rows = lax.broadcasted_iota(jnp.int32, a.shape, 0)
    return jnp.sum(jnp.where(rows == j, a, 0.0), axis=0, keepdims=True)


def _shift_rows(x, halo, d):
    xs = _roll(x, d, 0)
    hs = _roll(halo, d, 0)
    r8 = lax.broadcasted_iota(jnp.int32, hs.shape, 0)
    top = jnp.where(r8 < d, hs, xs[:SUBLANE])
    return jnp.concatenate([top, xs[SUBLANE:]], axis=0)


def _conv_silu(x, halo, w):
    y = _row(w, CONV_W - 1) * x
    for j in range(CONV_W - 1):
        y = y + _row(w, j) * _shift_rows(x, halo, CONV_W - 1 - j)
    return jax.nn.silu(y)


def _head_select(offset):
    r = lax.broadcasted_iota(jnp.int32, (LANE, WIDTH), 0)
    c = lax.broadcasted_iota(jnp.int32, (LANE, WIDTH), 1)
    return (r == offset + lax.shift_right_logical(c, 7)).astype(_BF)


def _split3(x):
    x1 = x.astype(_BF)
    r1 = x - x1.astype(F32)
    x2 = r1.astype(_BF)
    return x1, x2, (r1 - x2.astype(F32)).astype(_BF)


@jax.custom_vjp
def _spread(x, sel):
    return _d_nn(jnp.concatenate(_split3(x), axis=1), jnp.concatenate([sel, sel, sel], axis=0))


def _spread_fwd(x, sel):
    return _spread(x, sel), sel


def _spread_bwd(sel, g):
    g1, g2, g3 = _split3(g)
    return _d_nt(g1, sel) + _d_nt(g2, sel) + _d_nt(g3, sel), jnp.zeros_like(sel)


_spread.defvjp(_spread_fwd, _spread_bwd)


def f_gdn_pre(i, gq, gk, gv, gab, hq, hk, hv, cwq, cwk, cwv, alog, dtb):
    live = jnp.where(i == 0, 0.0, 1.0)
    q = _conv_silu(gq, hq * live, cwq)
    k = _conv_silu(gk, hk * live, cwk)
    v = _conv_silu(gv, hv * live, cwv)
    g = _spread(-jnp.exp(alog) * jax.nn.softplus(gab + dtb), _head_select(0))
    beta = _spread(jax.nn.sigmoid(gab), _head_select(HEADS))
    qs, ks, vs, gs, bs = [], [], [], [], []
    for h in range(HEADS):
        sl = slice(LANE * h, LANE * (h + 1))
        qh, kh = q[:, sl], k[:, sl]
        qs.append(qh * lax.rsqrt(jnp.sum(qh * qh, axis=-1, keepdims=True) + EPS) * (GDN_DIM ** -0.5))
        ks.append(kh * lax.rsqrt(jnp.sum(kh * kh, axis=-1, keepdims=True) + EPS))
        vs.append(v[:, sl])
        gs.append(g[:, sl])
        bs.append(beta[:, sl])
    return jnp.stack(qs), jnp.stack(ks), jnp.stack(vs), jnp.stack(gs), jnp.stack(bs)


def gdn_pre_bwd(i, gq, gk, gv, gab, hq, hk, hv, dq, dk, dv, dg, db, cwq, cwk, cwv, alog, dtb, cq, ck, cv):
    grads = _vjp_fn(f_gdn_pre, 12, 5)(i, gq, gk, gv, gab, hq, hk, hv, cwq, cwk, cwv, alog, dtb, dq, dk, dv, dg, db)
    dgq, dgk, dgv, dgab, dhq, dhk, dhv, dcwq, dcwk, dcwv, dalog, ddtb = grads

    def add_tail(dx, carry):
        return jnp.concatenate([dx[:-SUBLANE], dx[-SUBLANE:] + carry], axis=0)

    return (add_tail(dgq, cq), add_tail(dgk, ck), add_tail(dgv, cv), dgab,
            dcwq, dcwk, dcwv, dalog, ddtb, dhq, dhk, dhv)


def f_loss(i, x, h, tgt):
    e = x + h - tgt
    part = 0.5 * jnp.sum(e * e) * (1.0 / D_MODEL)
    dy = e * (1.0 / D_MODEL)
    return dy, dy, jnp.zeros((SUBLANE, LANE), F32) + part


def _flash_tile(t_len):
    return min(512, t_len)


FLASH_HEADS = 4
FLASH_BWD_HEADS = 2
LOG2E = 1.4426950408889634


def _causal(rows0, shape):
    r = rows0 + lax.broadcasted_iota(jnp.int32, shape, 0)
    c = lax.broadcasted_iota(jnp.int32, shape, 1)
    return c <= r


def flash_fwd(q, k, v):
    h_n, t_len, _ = q.shape
    tq = _flash_tile(t_len)
    nq = t_len // tq
    hb = FLASH_HEADS
    kw = 2 if nq % 2 == 0 else 1
    tk = kw * tq
    c2 = (QK ** -0.5) * LOG2E
    pairs = [(i, j) for i in range(nq) for j in range(i // kw + 1)]
    qt = jnp.array([p[0] for p in pairs], jnp.int32)
    kt = jnp.array([p[1] for p in pairs], jnp.int32)

    def body(qt_ref, kt_ref, q_ref, k_ref, v_ref, o_ref, lse_ref, m_s, acc_s):
        step = pl.program_id(1)
        qi, kj = qt_ref[step], kt_ref[step]
        last = qi // kw

        @pl.when(kj == 0)
        def _():
            m_s[...] = jnp.full_like(m_s, -jnp.inf)
            acc_s[...] = jnp.zeros_like(acc_s)

        def tile(diagonal):
            s = _bdot(q_ref[...], k_ref[...], _NT3, False) * c2
            if diagonal:
                s = jnp.where(_causal((qi % kw) * tq, (tq, tk))[None], s, -jnp.inf)
            m_old = m_s[...]
            m_new = jnp.maximum(m_old, jnp.max(s, axis=-1, keepdims=True))
            p = jnp.exp2(s - m_new).astype(_BF)
            v_ones = jnp.concatenate([v_ref[...], jnp.ones((hb, tk, LANE), _BF)], axis=2)
            acc_s[...] = jnp.exp2(m_old - m_new) * acc_s[...] + _bdot(p, v_ones, _NN3, False)
            m_s[...] = m_new

        @pl.when(kj < last)
        def _():
            tile(False)

        @pl.when(kj == last)
        def _():
            tile(True)
            acc = acc_s[...]
            l_sum = acc[:, :, LANE:]
            o = acc[:, :, :LANE] / l_sum
            for hh in range(hb):
                o_ref[:, LANE * hh:LANE * (hh + 1)] = o[hh]
            lse_ref[...] = m_s[...] + jnp.log2(jnp.max(l_sum, axis=-1, keepdims=True))

    return pl.pallas_call(
        body,
        name="flash_fwd",
        grid_spec=pltpu.PrefetchScalarGridSpec(
            num_scalar_prefetch=2,
            grid=(h_n // hb, qt.shape[0]),
            in_specs=[
                pl.BlockSpec((hb, tq, HEAD_PAD), lambda h, s, qt_ref, kt_ref: (h, qt_ref[s], 0)),
                pl.BlockSpec((hb, tk, HEAD_PAD), lambda h, s, qt_ref, kt_ref: (h, kt_ref[s], 0)),
                pl.BlockSpec((hb, tk, LANE), lambda h, s, qt_ref, kt_ref: (h, kt_ref[s], 0)),
            ],
            out_specs=[
                pl.BlockSpec((tq, hb * LANE), lambda h, s, qt_ref, kt_ref: (qt_ref[s], h)),
                pl.BlockSpec((hb, tq, 1), lambda h, s, qt_ref, kt_ref: (h, qt_ref[s], 0)),
            ],
            scratch_shapes=[pltpu.VMEM((hb, tq, 1), F32), pltpu.VMEM((hb, tq, 2 * LANE), F32)],
        ),
        out_shape=[jax.ShapeDtypeStruct((t_len, h_n * LANE), F32), jax.ShapeDtypeStruct((h_n, t_len, 1), F32)],
        compiler_params=_cparams("parallel", "arbitrary"),
    )(qt, kt, q, k, v)


def flash_bwd(q, k, v, do, lse, delta):
    h_n, t_len, _ = q.shape
    tq = _flash_tile(t_len)
    nq = t_len // tq
    hb = FLASH_BWD_HEADS
    kw = 2 if nq % 2 == 0 else 1
    tk = kw * tq
    pairs = [(i, j) for j in range(nq // kw) for i in range(kw * j, nq)]
    n_steps = len(pairs)
    qt = jnp.array([p[0] for p in pairs], jnp.int32)
    kt = jnp.array([p[1] for p in pairs], jnp.int32)
    scale = QK ** -0.5
    c2 = scale * LOG2E

    def body(qt_ref, kt_ref, q_ref, k_ref, v_ref, do_ref, lse_ref, dl_ref, dq_hbm, dk_ref, dv_ref, dq_s, dq_sem):
        group, step = pl.program_id(0), pl.program_id(1)
        qi, kj = qt_ref[step], kt_ref[step]

        @pl.when(step == 0)
        def _():
            dq_s[...] = jnp.zeros_like(dq_s)

        def tile(diagonal):
            qb, kb = q_ref[...], k_ref[...]
            dob = jnp.stack([do_ref[:, LANE * hh:LANE * (hh + 1)] for hh in range(hb)])
            p = jnp.exp2(_bdot(qb, kb, _NT3, False) * c2 - lse_ref[...])
            if diagonal:
                p = jnp.where(_causal((qi % kw) * tq, (tq, tk))[None], p, 0.0)
            dv = _bdot(p, dob, _TN3, False)
            ds = p * (_bdot(dob, v_ref[...], _NT3, False) - dl_ref[...]) * scale
            dk = _bdot(ds, qb, _TN3, False)
            dq_s[:, pl.ds(pl.multiple_of(qi * tq, tq), tq), :] += _bdot(ds, kb, _NN3, False)
            return dk, dv

        @pl.when(qi == kw * kj)
        def _():
            dk_ref[...], dv_ref[...] = tile(True)

        @pl.when((qi != kw * kj) & (qi // kw == kj))
        def _():
            dk, dv = tile(True)
            dk_ref[...] += dk
            dv_ref[...] += dv

        @pl.when(qi // kw > kj)
        def _():
            dk, dv = tile(False)
            dk_ref[...] += dk
            dv_ref[...] += dv

        @pl.when(step == n_steps - 1)
        def _():
            out = pltpu.make_async_copy(dq_s, dq_hbm.at[pl.ds(group * hb, hb)], dq_sem)
            out.start()
            out.wait()

    def qmap(h, s, qt_ref, kt_ref):
        return (h, qt_ref[s], 0)

    def kmap(h, s, qt_ref, kt_ref):
        return (h, kt_ref[s], 0)

    return pl.pallas_call(
        body,
        name="flash_bwd",
        grid_spec=pltpu.PrefetchScalarGridSpec(
            num_scalar_prefetch=2,
            grid=(h_n // hb, n_steps),
            in_specs=[
                pl.BlockSpec((hb, tq, HEAD_PAD), qmap),
                pl.BlockSpec((hb, tk, HEAD_PAD), kmap),
                pl.BlockSpec((hb, tk, LANE), kmap),
                pl.BlockSpec((tq, hb * LANE), lambda h, s, qt_ref, kt_ref: (qt_ref[s], h)),
                pl.BlockSpec((hb, tq, 1), qmap),
                pl.BlockSpec((hb, tq, 1), qmap),
            ],
            out_specs=[
                pl.BlockSpec(memory_space=pl.ANY),
                pl.BlockSpec((hb, tk, HEAD_PAD), kmap),
                pl.BlockSpec((hb, tk, LANE), kmap),
            ],
            scratch_shapes=[pltpu.VMEM((hb, t_len, HEAD_PAD), F32), pltpu.SemaphoreType.DMA],
        ),
        out_shape=[
            jax.ShapeDtypeStruct((h_n, t_len, HEAD_PAD), F32),
            jax.ShapeDtypeStruct((h_n, t_len, HEAD_PAD), F32),
            jax.ShapeDtypeStruct((h_n, t_len, LANE), F32),
        ],
        compiler_params=_cparams("parallel", "arbitrary"),
    )(qt, kt, q, k, v, do, lse, delta)


def _tri_ones(h_n):
    ii = lax.broadcasted_iota(jnp.int32, (h_n, CHUNK, CHUNK), 1)
    jj = lax.broadcasted_iota(jnp.int32, (h_n, CHUNK, CHUNK), 2)
    return (ii >= jj).astype(_BF)


@jax.custom_vjp
def _chunk_cumsum(gb):
    tri = _tri_ones(gb.shape[0])
    return _bdot(jnp.concatenate([tri, tri, tri], axis=2), jnp.concatenate(_split3(gb), axis=1), _NN3, False)


def _chunk_cumsum_bwd(_, ct):
    tri = _tri_ones(ct.shape[0])
    return (_bdot(jnp.concatenate([tri, tri, tri], axis=1), jnp.concatenate(_split3(ct), axis=1), _TN3, False),)


_chunk_cumsum.defvjp(lambda gb: (_chunk_cumsum(gb), None), _chunk_cumsum_bwd)


@jax.custom_vjp
def _pair_diff(gcb):
    g1, g2, g3 = _split3(gcb)
    lane = lax.broadcasted_iota(jnp.int32, (1, 1, LANE), 2)
    one, zero = jnp.ones((), _BF), jnp.zeros((), _BF)
    a = jnp.where(lane == 0, g1, jnp.where(lane == 1, g2, jnp.where(lane == 2, g3, jnp.where(lane < 6, one, zero))))
    b = jnp.where(lane < 3, one, jnp.where(lane == 3, -g1, jnp.where(lane == 4, -g2, jnp.where(lane == 5, -g3, zero))))
    return _bdot(a, b, _NT3, False)


def _pair_diff_bwd(_, ct):
    parts = _split3(ct)
    ones = jnp.ones((ct.shape[0], 3 * CHUNK, LANE), _BF)
    rows = _bdot(jnp.concatenate(parts, axis=2), ones, _NN3, False)
    cols = _bdot(jnp.concatenate(parts, axis=1), ones, _TN3, False)
    lane = lax.broadcasted_iota(jnp.int32, (1, 1, LANE), 2)
    return (jnp.where(lane == 0, rows - cols, 0.0),)


_pair_diff.defvjp(lambda gcb: (_pair_diff(gcb), None), _pair_diff_bwd)


def gdn_step(s, q, k, v, gb, bb, mm3=_bh3):
    c = CHUNK
    ii = lax.broadcasted_iota(jnp.int32, (1, c, c), 1)
    jj = lax.broadcasted_iota(jnp.int32, (1, c, c), 2)
    incl, strict = ii >= jj, ii > jj
    gcb = _chunk_cumsum(gb)
    diff = _pair_diff(gcb)
    decay = jnp.where(incl, jnp.exp(jnp.where(incl, diff, 0.0)), 0.0)
    kb, vb = k * bb, v * bb
    egc = jnp.exp(gcb)
    lmat = jnp.where(strict, _bmm_nt(kb, k) * decay, 0.0)
    inv = (ii == jj).astype(F32) - lmat
    pw = mm3(lmat, lmat)
    for step in range(5):
        inv = inv + mm3(inv, pw)
        if step < 4:
            pw = mm3(pw, pw)
    u = mm3(inv, vb)
    w = mm3(inv, kb * egc)
    attn = _bmm_nt(q, k) * decay
    qd = q * egc
    g_end = jnp.sum(gb, axis=1, keepdims=True)
    kd = k * jnp.exp(g_end - gcb)
    v_new = u - _bmm(w, s)
    o = _bmm(qd, s) + _bmm(attn, v_new)
    s_new = s * jnp.exp(g_end) + _bmm_tn(kd, v_new)
    return s_new, o


def gdn_fwd(q, k, v, gb, bb):
    h_n, t_len, d = q.shape
    n = t_len // CHUNK
    blk = pl.BlockSpec((h_n, CHUNK, d), lambda i: (0, i, 0))

    def body(q_ref, k_ref, v_ref, g_ref, b_ref, o_ref, sall_ref, s_s):
        @pl.when(pl.program_id(0) == 0)
        def _():
            s_s[...] = jnp.zeros_like(s_s)

        s = s_s[...]
        sall_ref[0] = s
        s_new, o = gdn_step(s, q_ref[...], k_ref[...], v_ref[...], g_ref[...], b_ref[...], mm3=_bh3_passes)
        o_ref[...] = o
        s_s[...] = s_new

    return pl.pallas_call(
        body,
        name="gdn_fwd",
        grid=(n,),
        in_specs=[blk] * 5,
        out_specs=[blk, pl.BlockSpec((1, h_n, d, d), lambda i: (i, 0, 0, 0))],
        out_shape=[jax.ShapeDtypeStruct((h_n, t_len, d), F32), jax.ShapeDtypeStruct((n, h_n, d, d), F32)],
        scratch_shapes=[pltpu.VMEM((h_n, d, d), F32)],
        compiler_params=_cparams("arbitrary"),
    )(q, k, v, gb, bb)


def gdn_bwd(q, k, v, gb, bb, s_all, do):
    h_n, t_len, d = q.shape
    n = t_len // CHUNK
    blk = pl.BlockSpec((h_n, CHUNK, d), lambda i: (0, n - 1 - i, 0))

    def body(q_ref, k_ref, v_ref, g_ref, b_ref, sall_ref, do_ref, dq_ref, dk_ref, dv_ref, dg_ref, db_ref, ds_s):
        @pl.when(pl.program_id(0) == 0)
        def _():
            ds_s[...] = jnp.zeros_like(ds_s)

        _, pull = jax.vjp(gdn_step, sall_ref[0], q_ref[...], k_ref[...], v_ref[...], g_ref[...], b_ref[...])
        ds, dq, dk, dv, dg, db = pull((ds_s[...], do_ref[...]))
        ds_s[...] = ds
        dq_ref[...], dk_ref[...], dv_ref[...], dg_ref[...], db_ref[...] = dq, dk, dv, dg, db

    return pl.pallas_call(
        body,
        name="gdn_bwd",
        grid=(n,),
        in_specs=[blk] * 5 + [pl.BlockSpec((1, h_n, d, d), lambda i: (n - 1 - i, 0, 0, 0)), blk],
        out_specs=[blk] * 5,
        out_shape=[jax.ShapeDtypeStruct((h_n, t_len, d), F32)] * 5,
        scratch_shapes=[pltpu.VMEM((h_n, d, d), F32)],
        compiler_params=_cparams("arbitrary"),
    )(q, k, v, gb, bb, s_all, do)


def _pad_cols(a, n):
    return jnp.pad(a, ((0, 0), (0, n - a.shape[1])))


def arrange_w_in(w):
    pieces, start = [], 0
    for n in R_SPLITS:
        pieces.append(w[:, start:start + n])
        start += n
    cq, ckv, kr, mgate, gq, gk, gv, ga, gb, ggate = pieces
    return jnp.concatenate([mgate, gq, gk, gv, ggate, cq, ckv, _pad_cols(kr, LANE),
                            _pad_cols(jnp.concatenate([ga, gb], axis=1), LANE)], axis=1)


def unarrange_w_in(g):
    def cols(start, n):
        return g[:, start:start + n]
    return jnp.concatenate([cols(P_CQ, Q_LORA), cols(P_CKV, KV_LORA), cols(P_KR, ROPE), cols(P_MGATE, WIDTH),
                            cols(P_GQ, WIDTH), cols(P_GK, WIDTH), cols(P_GV, WIDTH), cols(P_GAB, HEADS),
                            cols(P_GAB + HEADS, HEADS), cols(P_GGATE, WIDTH)], axis=1)


def arrange_w_uq(w):
    w = w.reshape(w.shape[0], HEADS, QK)
    return jnp.pad(w, ((0, 0), (0, 0), (0, HEAD_PAD - QK))).reshape(w.shape[0], HEADS * HEAD_PAD)


def unarrange_w_uq(g):
    return g.reshape(g.shape[0], HEADS, HEAD_PAD)[:, :, :QK].reshape(g.shape[0], HEADS * QK)


def local_step(x, pos, tgt, p):
    t_len = x.shape[0]
    w_in, w_uq, w_ukv, w_out = p["w_in"], p["w_uq"], p["w_ukv"], p["w_out"]
    norm_gain = p["norm_gain"].reshape(1, D_MODEL)
    qa_gain = p["mla_q_a_gain"].reshape(1, Q_LORA)
    kva_gain = p["mla_kv_a_gain"].reshape(1, KV_LORA)
    qg = _pad_cols(p["mla_q_norm_gain"].reshape(1, QK), HEAD_PAD)
    kg = _pad_cols(p["mla_k_norm_gain"].reshape(1, QK), HEAD_PAD)
    cw = p["gdn_conv_w"].reshape(CONV_W, 3 * WIDTH)
    cwq, cwk, cwv = cw[:, :WIDTH], cw[:, WIDTH:2 * WIDTH], cw[:, 2 * WIDTH:]
    alog = _pad_cols(p["gdn_a_log"].reshape(1, HEADS), LANE)
    dtb = _pad_cols(p["gdn_dt_bias"].reshape(1, HEADS), LANE)
    og = p["gdn_out_norm_gain"].reshape(1, GDN_DIM)
    half = ROPE // 2
    inv_freq = jnp.power(ROPE_THETA, -jnp.arange(half, dtype=F32) / half)
    invf = _pad_cols(jnp.concatenate([inv_freq, inv_freq]).reshape(1, ROPE), LANE)

    rt = 256
    r = "r"
    (xn,) = rowwise("rms_x", f_rms_x, t_len, rt, [(x, (r, D_MODEL, 0))], [norm_gain], [(r, D_MODEL, _BF)])
    proj = matmul("proj", xn, w_in, "nn")
    cq_in = (proj, (r, Q_LORA, P_CQ // Q_LORA))
    ckv_in = (proj, (r, KV_LORA, P_CKV // KV_LORA))
    kr_in = (proj, (r, LANE, P_KR // LANE))
    mgate_in = (proj, (r, WIDTH, P_MGATE // WIDTH))
    ggate_in = (proj, (r, WIDTH, P_GGATE // WIDTH))
    gqkv_in = [(proj, (r, WIDTH, P_GQ // WIDTH)), (proj, (r, WIDTH, P_GK // WIDTH)), (proj, (r, WIDTH, P_GV // WIDTH))]
    gab_in = (proj, (r, LANE, P_GAB // LANE))
    halos = [(proj, ("halo", WIDTH, P_GQ // WIDTH)), (proj, ("halo", WIDTH, P_GK // WIDTH)),
             (proj, ("halo", WIDTH, P_GV // WIDTH))]

    q_lat, kv_lat = rowwise("lat", f_lat, t_len, rt, [cq_in, ckv_in], [qa_gain, kva_gain],
                            [(r, Q_LORA, _BF), (r, KV_LORA, _BF)])
    q_raw = matmul("q_up", q_lat, w_uq, "nn")
    kv_raw = matmul("kv_up", kv_lat, w_ukv, "nn")
    wide = HEADS * HEAD_PAD
    head_in = [(q_raw, (r, wide, 0)), (kv_raw, (r, wide, 0)), kr_in]
    pos_in = (pos, (r, 1, 0))
    q_full, k_full, v_mla = rowwise(
        "head", lambda i, qr, kvr, kr, ps, qg_, kg_, iv: f_head(i, qr, kvr, kr, qg_, kg_, ps, iv), t_len, rt,
        head_in + [pos_in], [qg, kg, invf],
        [("h", HEADS, HEAD_PAD, _BF), ("h", HEADS, HEAD_PAD, _BF), ("h", HEADS, LANE, _BF)])
    o_mla, lse = flash_fwd(q_full, k_full, v_mla)

    pre_in = gqkv_in + [gab_in] + halos
    pre_full = [cwq, cwk, cwv, alog, dtb]
    hkind = ("h", HEADS, GDN_DIM, F32)
    gq_n, gk_n, gv_n, g_b, b_b = rowwise("gdn_pre", f_gdn_pre, t_len, rt, pre_in, pre_full, [hkind] * 5)
    o_gdn, s_all = gdn_fwd(gq_n, gk_n, gv_n, g_b, b_b)

    mix_in = [(o_mla, (r, WIDTH, 0)), mgate_in, (o_gdn, ("h",)), ggate_in]
    (mixed,) = rowwise("mix", f_mix, t_len, rt, mix_in, [og], [(r, 2 * WIDTH, _BF)])
    h_out = matmul("out_proj", mixed, w_out, "nn")
    dy, dy_mx, loss_acc = rowwise("loss", f_loss, t_len, rt,
                                  [(x, (r, D_MODEL, 0)), (h_out, (r, D_MODEL, 0)), (tgt, (r, D_MODEL, 0))], [],
                                  [(r, D_MODEL, F32), (r, D_MODEL, _BF)], [(SUBLANE, LANE)])
    loss = loss_acc[0, 0]

    d_mixed = matmul("d_mixed", dy_mx, w_out, "nt")
    g_w_out = matmul("g_w_out", mixed, dy_mx, "tn")

    def mix_bwd(i, o_mla_, mgate_, o_gdn_, ggate_, d_mixed_, og_):
        do_mla_, d_mgate_, do_gdn_, d_ggate_, g_og_ = _vjp_fn(f_mix, 5, 1)(i, o_mla_, mgate_, o_gdn_, ggate_, og_, d_mixed_)
        delta_ = jnp.stack([jnp.sum(o_mla_[:, LANE * h:LANE * (h + 1)] * do_mla_[:, LANE * h:LANE * (h + 1)],
                                    axis=-1, keepdims=True) for h in range(HEADS)])
        return do_mla_, d_mgate_, do_gdn_, d_ggate_, delta_, g_og_

    do_mla, d_mgate, do_gdn, d_ggate, delta, g_og = rowwise(
        "mix_bwd", mix_bwd, t_len, rt, mix_in + [(d_mixed, (r, 2 * WIDTH, 0))], [og],
        [(r, WIDTH, F32), (r, WIDTH, _BF), hkind, (r, WIDTH, _BF), ("h", HEADS, 1, F32)], [(1, GDN_DIM)])
    dq_n, dk_n, dv_n, dg_b, db_b = gdn_bwd(gq_n, gk_n, gv_n, g_b, b_b, s_all, do_gdn)
    cts_in = [(a, ("h",)) for a in (dq_n, dk_n, dv_n, dg_b, db_b)]
    d_gq, d_gk, d_gv, d_gab, g_cwq, g_cwk, g_cwv, g_alog, g_dtb = rowwise(
        "gdn_pre_bwd", gdn_pre_bwd, t_len, rt, pre_in + cts_in, pre_full,
        [(r, WIDTH, _BF)] * 3 + [(r, LANE, _BF)],
        [(CONV_W, WIDTH)] * 3 + [(1, LANE)] * 2, carries=[(SUBLANE, WIDTH)] * 3, reverse=True)

    dq_full, dk_full, dv_mla = flash_bwd(q_full, k_full, v_mla, do_mla, lse, delta)
    head_cts = [(a, ("h",)) for a in (dq_full, dk_full, dv_mla)]

    def head_bwd(i, q_raw_, kv_raw_, kr_, pos_, dq_, dk_, dv_, qg_, kg_, invf_):
        return _vjp_fn(f_head, 5, 3)(i, q_raw_, kv_raw_, kr_, qg_, kg_, pos_, invf_, dq_, dk_, dv_)

    dq_raw, dkv_raw, d_kr, g_qg, g_kg = rowwise(
        "head_bwd", head_bwd, t_len, rt, head_in + [pos_in] + head_cts, [qg, kg, invf],
        [(r, wide, _BF), (r, wide, _BF), (r, LANE, _BF)], [(1, HEAD_PAD), (1, HEAD_PAD)])
    dq_lat = matmul("dq_lat", dq_raw, w_uq, "nt")
    g_w_uq = matmul("g_w_uq", q_lat, dq_raw, "tn")
    dkv_lat = matmul("dkv_lat", dkv_raw, w_ukv, "nt")
    g_w_ukv = matmul("g_w_ukv", kv_lat, dkv_raw, "tn")

    def lat_bwd(i, cq_, ckv_, dql_, dkl_, gq_, gkv_):
        return _vjp_fn(f_lat, 4, 2)(i, cq_, ckv_, gq_, gkv_, dql_, dkl_)

    d_cq, d_ckv, g_qa, g_kva = rowwise(
        "lat_bwd", lat_bwd, t_len, rt, [cq_in, ckv_in, (dq_lat, (r, Q_LORA, 0)), (dkv_lat, (r, KV_LORA, 0))],
        [qa_gain, kva_gain], [(r, Q_LORA, _BF), (r, KV_LORA, _BF)], [(1, Q_LORA), (1, KV_LORA)])

    d_proj = jnp.concatenate([d_mgate, d_gq, d_gk, d_gv, d_ggate, d_cq, d_ckv, d_kr, d_gab], axis=1)
    d_xn = matmul("d_xn", d_proj, w_in, "nt")
    g_w_in = matmul("g_w_in", xn, d_proj, "tn")

    def rms_x_bwd(i, x_, dxn_, dy_, gain_):
        dx, dgain = _vjp_fn(f_rms_x, 2, 1)(i, x_, gain_, dxn_)
        return dx + dy_, dgain

    grad_x, g_norm = rowwise("rms_x_bwd", rms_x_bwd, t_len, rt,
                             [(x, (r, D_MODEL, 0)), (d_xn, (r, D_MODEL, 0)), (dy, (r, D_MODEL, 0))], [norm_gain],
                             [(r, D_MODEL, F32)], [(1, D_MODEL)])

    grads = {
        "norm_gain": g_norm, "w_in": g_w_in, "mla_q_a_gain": g_qa, "mla_kv_a_gain": g_kva, "w_uq": g_w_uq,
        "w_ukv": g_w_ukv, "mla_q_norm_gain": g_qg[:, :QK], "mla_k_norm_gain": g_kg[:, :QK],
        "gdn_conv_w": jnp.concatenate([g_cwq, g_cwk, g_cwv], axis=1), "gdn_a_log": g_alog[:, :HEADS],
        "gdn_dt_bias": g_dtb[:, :HEADS], "gdn_out_norm_gain": g_og, "w_out": g_w_out,
    }
    return loss, grad_x, grads


MESH = pl.DeviceIdType.MESH
ANY = pl.BlockSpec(memory_space=pl.ANY)
CHIP_FLIPS = ((1, 0), (0, 1), (1, 1))


def _place():
    return lax.axis_index("x"), lax.axis_index("y"), lax.axis_index("c")


def _flip(v, f):
    return 1 - v if f else v


def all_gather(shards):
    n_arr = len(shards)

    def body(*refs):
        x_refs, o_refs = refs[:n_arr], refs[n_arr:2 * n_arr]
        send_sems, recv_sems, local_sems = refs[2 * n_arr:]
        x, y, c = _place()
        me, sibling = (x, y, c), (x, y, 1 - c)
        chips = [(_flip(x, fx), _flip(y, fy)) for fx, fy in CHIP_FLIPS]

        def copy(a, k, block, to, src=None):
            px, py, pc = block
            dst = o_refs[a].at[4 * px + 2 * py + pc]
            return pltpu.make_async_remote_copy(
                src_ref=dst if src is None else src, dst_ref=dst, send_sem=send_sems.at[a, k],
                recv_sem=recv_sems.at[a, k], device_id=to, device_id_type=MESH)

        mine, first, passed = [], [], []
        for a in range(n_arr):
            cp = pltpu.make_async_copy(x_refs[a], o_refs[a].at[4 * x + 2 * y + c], local_sems.at[a])
            cp.start()
            mine.append(cp)
            first.append(copy(a, 0, me, sibling, src=x_refs[a]))
            first += [copy(a, 1 + j, me, (*chip, c), src=x_refs[a]) for j, chip in enumerate(chips)]
        for cp in first:
            cp.start()
        for j, chip in enumerate(chips):
            for a in range(n_arr):
                copy(a, 1 + j, (*chip, c), me).wait_recv()
                cp = copy(a, 4 + j, (*chip, c), sibling)
                cp.start()
                passed.append(cp)
        for a in range(n_arr):
            copy(a, 0, sibling, me).wait_recv()
            for j, chip in enumerate(chips):
                copy(a, 4 + j, (*chip, 1 - c), me).wait_recv()
        for cp in first + passed:
            cp.wait_send()
        for cp in mine:
            cp.wait()

    return pl.pallas_call(
        body,
        name="all_gather",
        out_shape=[jax.ShapeDtypeStruct((N_DEV,) + s.shape, s.dtype) for s in shards],
        in_specs=[ANY] * n_arr,
        out_specs=[ANY] * n_arr,
        scratch_shapes=[pltpu.SemaphoreType.DMA((n_arr, 7)), pltpu.SemaphoreType.DMA((n_arr, 7)),
                        pltpu.SemaphoreType.DMA((n_arr,))],
    )(*shards)


def exchange_cores(grads):
    n_arr = len(grads)

    def body(*refs):
        g_refs, o_refs = refs[:n_arr], refs[n_arr:2 * n_arr]
        send_sems, recv_sems = refs[2 * n_arr:]
        x, y, c = _place()
        copies = []
        for a in range(n_arr):
            for q in range(4):
                cp = pltpu.make_async_remote_copy(
                    src_ref=g_refs[a].at[2 * q + (1 - c)], dst_ref=o_refs[a].at[q], send_sem=send_sems.at[a, q],
                    recv_sem=recv_sems.at[a, q], device_id=(x, y, 1 - c), device_id_type=MESH)
                cp.start()
                copies.append(cp)
        for cp in copies:
            cp.wait()

    return pl.pallas_call(
        body,
        name="exchange_cores",
        out_shape=[jax.ShapeDtypeStruct((4,) + g.shape[1:], g.dtype) for g in grads],
        in_specs=[ANY] * n_arr,
        out_specs=[ANY] * n_arr,
        scratch_shapes=[pltpu.SemaphoreType.DMA((n_arr, 4)), pltpu.SemaphoreType.DMA((n_arr, 4))],
    )(*grads)


def exchange_chips(parts):
    n_arr = len(parts)

    def body(*refs):
        p_refs, o_refs = refs[:n_arr], refs[n_arr:2 * n_arr]
        send_sems, recv_sems = refs[2 * n_arr:]
        x, y, c = _place()
        copies = []
        for a in range(n_arr):
            for j, (fx, fy) in enumerate(CHIP_FLIPS):
                px, py = _flip(x, fx), _flip(y, fy)
                cp = pltpu.make_async_remote_copy(
                    src_ref=p_refs[a].at[2 * px + py], dst_ref=o_refs[a].at[j], send_sem=send_sems.at[a, j],
                    recv_sem=recv_sems.at[a, j], device_id=(px, py, c), device_id_type=MESH)
                cp.start()
                copies.append(cp)
        for cp in copies:
            cp.wait()

    return pl.pallas_call(
        body,
        name="exchange_chips",
        out_shape=[jax.ShapeDtypeStruct((3,) + p.shape[1:], p.dtype) for p in parts],
        in_specs=[ANY] * n_arr,
        out_specs=[ANY] * n_arr,
        scratch_shapes=[pltpu.SemaphoreType.DMA((n_arr, 3)), pltpu.SemaphoreType.DMA((n_arr, 3))],
    )(*parts)


def gather_small(v):
    def body(v_ref, o_ref, send_sems, recv_sems, local_sem):
        x, y, c = _place()
        me = 4 * x + 2 * y + c
        mine = pltpu.make_async_copy(v_ref, o_ref.at[me], local_sem)
        mine.start()
        copies = []
        for k in range(1, N_DEV):
            fx, fy, fc = (k >> 2) & 1, (k >> 1) & 1, k & 1
            cp = pltpu.make_async_remote_copy(
                src_ref=v_ref, dst_ref=o_ref.at[me], send_sem=send_sems.at[k - 1], recv_sem=recv_sems.at[k - 1],
                device_id=(_flip(x, fx), _flip(y, fy), _flip(c, fc)), device_id_type=MESH)
            cp.start()
            copies.append(cp)
        for cp in copies:
            cp.wait()
        mine.wait()

    return pl.pallas_call(
        body,
        name="gather_small",
        out_shape=jax.ShapeDtypeStruct((N_DEV,) + v.shape, v.dtype),
        in_specs=[ANY],
        out_specs=ANY,
        scratch_shapes=[pltpu.SemaphoreType.DMA((N_DEV - 1,)), pltpu.SemaphoreType.DMA((N_DEV - 1,)),
                        pltpu.SemaphoreType.DMA],
    )(v)


def _row_tile(rows):
    for t in (256, 128, 64, 32, 16, 8):
        if rows % t == 0:
            return t
    return rows


def add_core_parts(name, g, recv, c_idx, wire):
    _, rows, cols = g.shape
    tr = _row_tile(rows)

    def body(c_ref, g_ref, r_ref, o_ref, w_ref):
        part = g_ref[...] + r_ref[...]
        o_ref[...] = part
        w_ref[...] = part.astype(w_ref.dtype)

    blk = pl.BlockSpec((1, tr, cols), lambda q, i, c_ref: (q, i, 0))
    return pl.pallas_call(
        body,
        name=name,
        grid_spec=pltpu.PrefetchScalarGridSpec(
            num_scalar_prefetch=1,
            grid=(4, rows // tr),
            in_specs=[pl.BlockSpec((1, tr, cols), lambda q, i, c_ref: (2 * q + c_ref[0], i, 0)), blk],
            out_specs=[blk, blk],
        ),
        out_shape=[jax.ShapeDtypeStruct((4, rows, cols), F32), jax.ShapeDtypeStruct((4, rows, cols), wire)],
        compiler_params=_cparams("parallel", "parallel"),
    )(c_idx, g, recv)


def _adamw(w, g, m, v):
    m = ADAM_B1 * m + (1.0 - ADAM_B1) * g
    v = ADAM_B2 * v + (1.0 - ADAM_B2) * (g * g)
    m_hat = m / (1.0 - ADAM_B1 ** ADAM_STEP)
    v_hat = v / (1.0 - ADAM_B2 ** ADAM_STEP)
    delta = -ADAM_LR * (m_hat / (jnp.sqrt(v_hat) + ADAM_EPS) + ADAM_WD * w)
    return delta, m, v


def adamw_sharded(name, parts, recv, q_idx, w, m, v):
    rows, cols = w.shape
    tr = _row_tile(rows)

    def body(q_ref, p_ref, r_ref, w_ref, m_ref, v_ref, g_out, d_out, m_out, v_out):
        g = p_ref[0] + r_ref[0].astype(F32) + r_ref[1].astype(F32) + r_ref[2].astype(F32)
        d, m_new, v_new = _adamw(w_ref[...], g, m_ref[...], v_ref[...])
        g_out[...], d_out[...], m_out[...], v_out[...] = g, d, m_new, v_new

    blk = pl.BlockSpec((tr, cols), lambda i, q_ref: (i, 0))
    return pl.pallas_call(
        body,
        name=name,
        grid_spec=pltpu.PrefetchScalarGridSpec(
            num_scalar_prefetch=1,
            grid=(rows // tr,),
            in_specs=[pl.BlockSpec((1, tr, cols), lambda i, q_ref: (q_ref[0], i, 0)),
                      pl.BlockSpec((3, tr, cols), lambda i, q_ref: (0, i, 0)), blk, blk, blk],
            out_specs=[blk] * 4,
        ),
        out_shape=[jax.ShapeDtypeStruct((rows, cols), F32)] * 4,
        compiler_params=_cparams("parallel"),
    )(q_idx, parts, recv, w, m, v)


def adamw_small(gathered, w, m, v):
    def body(g_ref, w_ref, m_ref, v_ref, g_out, d_out, m_out, v_out):
        g = g_ref[0]
        for j in range(1, N_DEV):
            g = g + g_ref[j]
        d, m_new, v_new = _adamw(w_ref[...], g, m_ref[...], v_ref[...])
        g_out[...], d_out[...], m_out[...], v_out[...] = g, d, m_new, v_new

    return pl.pallas_call(body, name="adamw_small", out_shape=[jax.ShapeDtypeStruct(w.shape, F32)] * 4)(gathered, w, m, v)


SHARDED = ("w_in", "w_uq", "w_ukv", "gdn_conv_w", "w_out")
SMALL = (("norm_gain", D_MODEL), ("mla_q_a_gain", Q_LORA), ("mla_kv_a_gain", KV_LORA), ("mla_q_norm_gain", QK),
         ("mla_k_norm_gain", QK), ("gdn_a_log", HEADS), ("gdn_dt_bias", HEADS), ("gdn_out_norm_gain", GDN_DIM))
WEIGHT_ORDER = ("norm_gain", "w_in", "mla_q_a_gain", "mla_kv_a_gain", "w_uq", "w_ukv", "mla_q_norm_gain",
                "mla_k_norm_gain", "gdn_conv_w", "gdn_a_log", "gdn_dt_bias", "gdn_out_norm_gain", "w_out")


def _pack_small(d):
    rows = []
    for name, n in SMALL:
        a = d[name].reshape(-1).astype(F32)
        n_pad = -(-n // LANE) * LANE
        rows.append(jnp.pad(a, (0, n_pad - n)).reshape(n_pad // LANE, LANE))
    packed = jnp.concatenate(rows, axis=0)
    return jnp.pad(packed, ((0, -packed.shape[0] % SUBLANE), (0, 0)))


def _unpack_small(packed):
    out, row = {}, 0
    for name, n in SMALL:
        n_rows = -(-n // LANE)
        out[name] = packed[row:row + n_rows].reshape(-1)[:n].reshape(1, n)
        row += n_rows
    return out


def kernel(x, positions, norm_gain, w_in, mla_q_a_gain, mla_kv_a_gain, w_uq, w_ukv, mla_q_norm_gain, mla_k_norm_gain, gdn_conv_w, gdn_a_log, gdn_dt_bias, gdn_out_norm_gain, w_out, loss_target, m_norm_gain, m_w_in, m_mla_q_a_gain, m_mla_kv_a_gain, m_w_uq, m_w_ukv, m_mla_q_norm_gain, m_mla_k_norm_gain, m_gdn_conv_w, m_gdn_a_log, m_gdn_dt_bias, m_gdn_out_norm_gain, m_w_out, v_norm_gain, v_w_in, v_mla_q_a_gain, v_mla_kv_a_gain, v_w_uq, v_w_ukv, v_mla_q_norm_gain, v_mla_k_norm_gain, v_gdn_conv_w, v_gdn_a_log, v_gdn_dt_bias, v_gdn_out_norm_gain, v_w_out):
    w = dict(norm_gain=norm_gain, w_in=w_in, mla_q_a_gain=mla_q_a_gain, mla_kv_a_gain=mla_kv_a_gain, w_uq=w_uq,
             w_ukv=w_ukv, mla_q_norm_gain=mla_q_norm_gain, mla_k_norm_gain=mla_k_norm_gain, gdn_conv_w=gdn_conv_w,
             gdn_a_log=gdn_a_log, gdn_dt_bias=gdn_dt_bias, gdn_out_norm_gain=gdn_out_norm_gain, w_out=w_out)
    m = dict(norm_gain=m_norm_gain, w_in=m_w_in, mla_q_a_gain=m_mla_q_a_gain, mla_kv_a_gain=m_mla_kv_a_gain,
             w_uq=m_w_uq, w_ukv=m_w_ukv, mla_q_norm_gain=m_mla_q_norm_gain, mla_k_norm_gain=m_mla_k_norm_gain,
             gdn_conv_w=m_gdn_conv_w, gdn_a_log=m_gdn_a_log, gdn_dt_bias=m_gdn_dt_bias,
             gdn_out_norm_gain=m_gdn_out_norm_gain, w_out=m_w_out)
    v = dict(norm_gain=v_norm_gain, w_in=v_w_in, mla_q_a_gain=v_mla_q_a_gain, mla_kv_a_gain=v_mla_kv_a_gain,
             w_uq=v_w_uq, w_ukv=v_w_ukv, mla_q_norm_gain=v_mla_q_norm_gain, mla_k_norm_gain=v_mla_k_norm_gain,
             gdn_conv_w=v_gdn_conv_w, gdn_a_log=v_gdn_a_log, gdn_dt_bias=v_gdn_dt_bias,
             gdn_out_norm_gain=v_gdn_out_norm_gain, w_out=v_w_out)
    t_len = x.shape[1]

    shards = [w[n][0] if n == "gdn_conv_w" else w[n][0].astype(_BF) for n in SHARDED]
    a_w_in, a_w_uq, a_w_ukv, a_cw, a_w_out = all_gather(shards)

    def cols_whole(g):
        return g.transpose(1, 0, 2).reshape(g.shape[1], N_DEV * g.shape[2])

    p = {n: w[n] for n, _ in SMALL}
    p["w_in"] = arrange_w_in(cols_whole(a_w_in))
    p["w_uq"] = arrange_w_uq(cols_whole(a_w_uq))
    p["w_ukv"] = cols_whole(a_w_ukv)
    p["gdn_conv_w"] = cols_whole(a_cw)
    p["w_out"] = a_w_out.reshape(N_DEV * a_w_out.shape[1], a_w_out.shape[2])

    pos = positions.reshape(t_len, 1).astype(F32)
    loss, grad_x, grads = local_step(x.reshape(t_len, D_MODEL), pos, loss_target.reshape(t_len, D_MODEL), p)
    loss = lax.psum(loss, ("x", "y", "c"))

    def col_blocks(g):
        return g.reshape(g.shape[0], N_DEV, g.shape[1] // N_DEV).transpose(1, 0, 2)

    blocks = [col_blocks(unarrange_w_in(grads["w_in"])), col_blocks(unarrange_w_uq(grads["w_uq"])),
              col_blocks(grads["w_ukv"]), col_blocks(grads["gdn_conv_w"]),
              grads["w_out"].reshape(N_DEV, D_MODEL // N_DEV, D_MODEL)]
    xi, yi, ci = _place()
    c_idx = jnp.reshape(ci, (1,)).astype(jnp.int32)
    q_idx = jnp.reshape(2 * xi + yi, (1,)).astype(jnp.int32)
    from_sibling = exchange_cores(blocks)
    parts = [add_core_parts("add_" + n, g, r, c_idx, F32 if n == "gdn_conv_w" else _BF)
             for n, g, r in zip(SHARDED, blocks, from_sibling)]
    from_chips = exchange_chips([wire for _, wire in parts])
    out = {}
    for n, (prt, _), rcv in zip(SHARDED, parts, from_chips):
        shape = w[n].shape
        res = adamw_sharded("adamw_" + n, prt, rcv, q_idx, w[n].reshape(shape[-2:]), m[n].reshape(shape[-2:]),
                            v[n].reshape(shape[-2:]))
        out[n] = [a.reshape(shape) for a in res]

    small_all = gather_small(_pack_small(grads))
    res = adamw_small(small_all, _pack_small(w), _pack_small(m), _pack_small(v))
    unpacked = [_unpack_small(a) for a in res]
    for n, _ in SMALL:
        out[n] = [u[n] for u in unpacked]

    return (loss, grad_x.reshape(x.shape), *[out[n][0] for n in WEIGHT_ORDER], *[out[n][1] for n in WEIGHT_ORDER],
            *[out[n][2] for n in WEIGHT_ORDER], *[out[n][3] for n in WEIGHT_ORDER])
```

```python
import functools

import jax
import jax.numpy as jnp
from jax import lax
from jax.experimental import pallas as pl
from jax.experimental.pallas import tpu as pltpu

F32 = jnp.float32
_BF = jnp.bfloat16
HI = lax.Precision.HIGHEST

D_MODEL = 2048
HEADS = 8
NOPE = 128
ROPE = 64
QK = NOPE + ROPE
Q_LORA = 512
KV_LORA = 256
HEAD_PAD = 256
GDN_DIM = 128
WIDTH = HEADS * 128
CONV_W = 4
CHUNK = 64
ROPE_THETA = 10000.0
EPS = 1e-6
N_DEV = 8
LANE = 128
SUBLANE = 8
VMEM_LIMIT = 48 * 1024 * 1024

ADAM_LR, ADAM_B1, ADAM_B2, ADAM_EPS, ADAM_WD, ADAM_STEP = 0.001, 0.9, 0.999, 1e-08, 0.01, 10

P_MGATE, P_GQ, P_GK, P_GV, P_GGATE = 0, 1024, 2048, 3072, 4096
P_CQ, P_CKV, P_KR, P_GAB = 5120, 5632, 5888, 6016
P_COLS = 6144
R_SPLITS = (512, 256, 64, 1024, 1024, 1024, 1024, 8, 8, 1024)


def _cparams(*sem):
    return pltpu.CompilerParams(dimension_semantics=sem, vmem_limit_bytes=VMEM_LIMIT)


def _d_nn(a, b):
    return jnp.dot(a.astype(_BF), b.astype(_BF), preferred_element_type=F32)


def _d_nt(a, b):
    return lax.dot_general(a.astype(_BF), b.astype(_BF), (((1,), (1,)), ((), ())), preferred_element_type=F32)


def _d_tn(a, b):
    return lax.dot_general(a.astype(_BF), b.astype(_BF), (((0,), (0,)), ((), ())), preferred_element_type=F32)


@jax.custom_vjp
def _mm(a, b):
    return _d_nn(a, b)


_mm.defvjp(lambda a, b: (_d_nn(a, b), (a, b)), lambda r, g: (_d_nt(g, r[1]), _d_tn(r[0], g)))


@jax.custom_vjp
def _mm_nt(a, b):
    return _d_nt(a, b)


_mm_nt.defvjp(lambda a, b: (_d_nt(a, b), (a, b)), lambda r, g: (_d_nn(g, r[1]), _d_tn(g, r[0])))


@jax.custom_vjp
def _mm_tn(a, b):
    return _d_tn(a, b)


_mm_tn.defvjp(lambda a, b: (_d_tn(a, b), (a, b)), lambda r, g: (_d_nt(r[1], g), _d_nn(r[0], g)))


def _hi(a, b):
    return jnp.dot(a, b, preferred_element_type=F32, precision=HI)


_NN3 = (((2,), (1,)), ((0,), (0,)))
_NT3 = (((2,), (2,)), ((0,), (0,)))
_TN3 = (((1,), (1,)), ((0,), (0,)))


def _bdot(a, b, dims, hi):
    if hi:
        return lax.dot_general(a, b, dims, preferred_element_type=F32, precision=hi)
    return lax.dot_general(a.astype(_BF), b.astype(_BF), dims, preferred_element_type=F32)


def _batched_matmuls(hi):
    nn = jax.custom_vjp(lambda a, b: _bdot(a, b, _NN3, hi))
    nt = jax.custom_vjp(lambda a, b: _bdot(a, b, _NT3, hi))
    tn = jax.custom_vjp(lambda a, b: _bdot(a, b, _TN3, hi))
    nn.defvjp(lambda a, b: (_bdot(a, b, _NN3, hi), (a, b)),
              lambda r, g: (_bdot(g, r[1], _NT3, hi), _bdot(r[0], g, _TN3, hi)))
    nt.defvjp(lambda a, b: (_bdot(a, b, _NT3, hi), (a, b)),
              lambda r, g: (_bdot(g, r[1], _NN3, hi), _bdot(g, r[0], _TN3, hi)))
    tn.defvjp(lambda a, b: (_bdot(a, b, _TN3, hi), (a, b)),
              lambda r, g: (_bdot(r[1], g, _NT3, hi), _bdot(r[0], g, _NN3, hi)))
    return nn, nt, tn


_bmm, _bmm_nt, _bmm_tn = _batched_matmuls(False)
_bhi, _bhi_nt, _bhi_tn = _batched_matmuls(HI)


def _split2(x):
    hi = x.astype(_BF)
    return hi, (x - hi.astype(F32)).astype(_BF)


def _pdot(a, b, mode):
    (a_hi, a_lo), (b_hi, b_lo) = _split2(a), _split2(b)
    a_ax, b_ax, dims = {"nn": (2, 1, _NN3), "nt": (2, 2, _NT3), "tn": (1, 1, _TN3)}[mode]
    lhs = jnp.concatenate([a_hi, a_lo, a_hi], axis=a_ax)
    rhs = jnp.concatenate([b_hi, b_hi, b_lo], axis=b_ax)
    return lax.dot_general(lhs, rhs, dims, preferred_element_type=F32)


def _packed_matmuls():
    nn = jax.custom_vjp(lambda a, b: _pdot(a, b, "nn"))
    nn.defvjp(lambda a, b: (_pdot(a, b, "nn"), (a, b)), lambda r, g: (_pdot(g, r[1], "nt"), _pdot(r[0], g, "tn")))
    return nn


_bh3 = _packed_matmuls()
_bh3_passes = _batched_matmuls(lax.Precision.HIGH)[0]


@functools.partial(jax.custom_vjp, nondiff_argnums=(1, 2))
def _roll(x, shift, axis):
    return pltpu.roll(x, shift, axis)


def _roll_fwd(x, shift, axis):
    return pltpu.roll(x, shift, axis), None


def _roll_bwd(shift, axis, _, g):
    n = g.shape[axis]
    return (pltpu.roll(g, (n - shift) % n, axis),)


_roll.defvjp(_roll_fwd, _roll_bwd)


def _rms(x, gain):
    return x * lax.rsqrt(jnp.mean(x * x, axis=-1, keepdims=True) + EPS) * gain


MM_TILE = 1024
MM_DEPTH = 2048


def matmul(name, a, b, mode):
    if mode == "nn":
        (m, k), (k2, n) = a.shape, b.shape
    elif mode == "nt":
        (m, k), (n, k2) = a.shape, b.shape
    else:
        (k, m), (k2, n) = a.shape, b.shape
    assert k == k2, (name, a.shape, b.shape)
    tm, tn, tk = min(MM_TILE, m), min(MM_TILE, n), min(MM_DEPTH, k)
    assert m % tm == 0 and n % tn == 0 and k % tk == 0, (name, m, n, k)
    dot = {"nn": _d_nn, "nt": _d_nt, "tn": _d_tn}[mode]

    def body(a_ref, b_ref, o_ref):
        kk = pl.program_id(2)
        part = dot(a_ref[...], b_ref[...])

        @pl.when(kk == 0)
        def _():
            o_ref[...] = part

        @pl.when(kk != 0)
        def _():
            o_ref[...] += part

    if mode == "nn":
        a_spec = pl.BlockSpec((tm, tk), lambda j, i, kk: (i, kk))
        b_spec = pl.BlockSpec((tk, tn), lambda j, i, kk: (kk, j))
    elif mode == "nt":
        a_spec = pl.BlockSpec((tm, tk), lambda j, i, kk: (i, kk))
        b_spec = pl.BlockSpec((tn, tk), lambda j, i, kk: (j, kk))
    else:
        a_spec = pl.BlockSpec((tk, tm), lambda j, i, kk: (kk, i))
        b_spec = pl.BlockSpec((tk, tn), lambda j, i, kk: (kk, j))
    return pl.pallas_call(
        body,
        name=name,
        grid=(n // tn, m // tm, k // tk),
        in_specs=[a_spec, b_spec],
        out_specs=pl.BlockSpec((tm, tn), lambda j, i, kk: (i, j)),
        out_shape=jax.ShapeDtypeStruct((m, n), F32),
        compiler_params=_cparams("parallel", "parallel", "arbitrary"),
    )(a, b)


def rowwise(name, fn, t_len, tile, row_in, full_in, row_out, acc_out=(), carries=(), reverse=False):
    tile = min(tile, t_len)
    n = t_len // tile
    assert t_len % tile == 0 and tile % SUBLANE == 0
    n_in, n_ro, n_acc, n_car = len(row_in) + len(full_in), len(row_out), len(acc_out), len(carries)

    def ti(i):
        return (n - 1 - i) if reverse else i

    in_specs, args = [], []
    for arr, kind in row_in:
        if kind[0] == "r":
            in_specs.append(pl.BlockSpec((tile, kind[1]), lambda i, c=kind[2]: (ti(i), c)))
        elif kind[0] == "h":
            in_specs.append(pl.BlockSpec((arr.shape[0], tile, arr.shape[2]), lambda i: (0, ti(i), 0)))
        else:
            in_specs.append(pl.BlockSpec(
                (SUBLANE, kind[1]), lambda i, c=kind[2]: (jnp.maximum(ti(i) * (tile // SUBLANE) - 1, 0), c)))
        args.append(arr)
    for arr in full_in:
        in_specs.append(pl.BlockSpec(arr.shape, lambda i, nd=arr.ndim: (0,) * nd))
        args.append(arr)
    out_specs, out_shape = [], []
    for kind in row_out:
        if kind[0] == "r":
            out_specs.append(pl.BlockSpec((tile, kind[1]), lambda i: (ti(i), 0)))
            out_shape.append(jax.ShapeDtypeStruct((t_len, kind[1]), kind[2]))
        else:
            out_specs.append(pl.BlockSpec((kind[1], tile, kind[2]), lambda i: (0, ti(i), 0)))
            out_shape.append(jax.ShapeDtypeStruct((kind[1], t_len, kind[2]), kind[3]))
    for shp in acc_out:
        out_specs.append(pl.BlockSpec(shp, lambda i, nd=len(shp): (0,) * nd))
        out_shape.append(jax.ShapeDtypeStruct(shp, F32))

    def body(*refs):
        in_refs = refs[:n_in]
        ro_refs = refs[n_in:n_in + n_ro]
        acc_refs = refs[n_in + n_ro:n_in + n_ro + n_acc]
        car_refs = refs[n_in + n_ro + n_acc:]
        step = pl.program_id(0)
        if n_car:
            @pl.when(step == 0)
            def _():
                for r in car_refs:
                    r[...] = jnp.zeros_like(r)
        vals = [r[...].astype(F32) for r in in_refs] + [r[...] for r in car_refs]
        outs = fn(ti(step), *vals)
        assert len(outs) == n_ro + n_acc + n_car, (name, len(outs))
        for r, o in zip(ro_refs, outs[:n_ro]):
            r[...] = o.astype(r.dtype)
        for r, o in zip(acc_refs, outs[n_ro:n_ro + n_acc]):
            @pl.when(step == 0)
            def _(r=r, o=o):
                r[...] = o

            @pl.when(step != 0)
            def _(r=r, o=o):
                r[...] += o
        for r, o in zip(car_refs, outs[n_ro + n_acc:]):
            r[...] = o

    res = pl.pallas_call(
        body,
        name=name,
        grid=(n,),
        in_specs=in_specs,
        out_specs=out_specs,
        out_shape=out_shape,
        scratch_shapes=[pltpu.VMEM(s, F32) for s in carries],
        compiler_params=_cparams("arbitrary"),
    )(*args)
    return list(res)


def _vjp_fn(fn, n_diff, n_out):
    def g(i, *a):
        ins, cts = a[:len(a) - n_out], a[len(a) - n_out:]
        diff, rest = ins[:n_diff], ins[n_diff:]
        _, pull = jax.vjp(lambda *d: tuple(fn(i, *d, *rest)), *diff)
        return tuple(pull(tuple(cts)))

    return g


def f_rms_x(i, x, gain):
    return (_rms(x, gain),)


def f_lat(i, cq, ckv, gq, gkv):
    return _rms(cq, gq), _rms(ckv, gkv)


def _rope_tables(pos, invf):
    ang = pos * invf
    lane = lax.broadcasted_iota(jnp.int32, (1, LANE), 1)
    cosv, sinv = jnp.cos(ang), jnp.sin(ang)
    half = ROPE // 2
    c = jnp.where(lane < ROPE, cosv, 0.0)
    sa = jnp.where(lane < half, -sinv, 0.0)
    sb = jnp.where((lane >= half) & (lane < ROPE), sinv, 0.0)
    return c, sa, sb


def _rope(xh, tabs):
    c, sa, sb = tabs
    half = ROPE // 2
    return xh * c + _roll(xh, LANE - half, 1) * sa + _roll(xh, half, 1) * sb


def f_head(i, q_raw, kv_raw, kr, qg, kg, pos, invf):
    tabs = _rope_tables(pos, invf)
    qs, ks, vs = [], [], []
    kr_ss = jnp.sum(kr * kr, axis=-1, keepdims=True)
    for h in range(HEADS):
        lo = q_raw[:, HEAD_PAD * h:HEAD_PAD * h + NOPE]
        hi = q_raw[:, HEAD_PAD * h + NOPE:HEAD_PAD * (h + 1)]
        ss = jnp.sum(lo * lo, axis=-1, keepdims=True) + jnp.sum(hi * hi, axis=-1, keepdims=True)
        r = lax.rsqrt(ss * (1.0 / QK) + EPS)
        qs.append(jnp.concatenate([lo * r * qg[:, :NOPE], _rope(hi * r * qg[:, NOPE:], tabs)], axis=1))
        lo = kv_raw[:, 2 * NOPE * h:2 * NOPE * h + NOPE]
        ss = jnp.sum(lo * lo, axis=-1, keepdims=True) + kr_ss
        r = lax.rsqrt(ss * (1.0 / QK) + EPS)
        ks.append(jnp.concatenate([lo * r * kg[:, :NOPE], _rope(kr * r * kg[:, NOPE:], tabs)], axis=1))
        vs.append(kv_raw[:, 2 * NOPE * h + NOPE:2 * NOPE * (h + 1)])
    return jnp.stack(qs), jnp.stack(ks), jnp.stack(vs)


def f_mix(i, o_mla, mgate, o_gdn, ggate, og):
    parts = [o_mla * jax.nn.silu(mgate)]
    for h in range(HEADS):
        parts.append(_rms(o_gdn[h], og) * jax.nn.silu(ggate[:, LANE * h:LANE * (h + 1)]))
    return (jnp.concatenate(parts, axis=1),)


def _row(a, j):
    rows = lax.broadcasted_iota(jnp.int32, a.shape, 0)
    return jnp.sum(jnp.where(rows == j, a, 0.0), axis=0, keepdims=True)


def _shift_rows(x, halo, d):
    xs = _roll(x, d, 0)
    hs = _roll(halo, d, 0)
    r8 = lax.broadcasted_iota(jnp.int32, hs.shape, 0)
    top = jnp.where(r8 < d, hs, xs[:SUBLANE])
    return jnp.concatenate([top, xs[SUBLANE:]], axis=0)


def _conv_silu(x, halo, w):
    y = _row(w, CONV_W - 1) * x
    for j in range(CONV_W - 1):
        y = y + _row(w, j) * _shift_rows(x, halo, CONV_W - 1 - j)
    return jax.nn.silu(y)


def _head_select(offset):
    r = lax.broadcasted_iota(jnp.int32, (LANE, WIDTH), 0)
    c = lax.broadcasted_iota(jnp.int32, (LANE, WIDTH), 1)
    return (r == offset + lax.shift_right_logical(c, 7)).astype(_BF)


def _split3(x):
    x1 = x.astype(_BF)
    r1 = x - x1.astype(F32)
    x2 = r1.astype(_BF)
    return x1, x2, (r1 - x2.astype(F32)).astype(_BF)


@jax.custom_vjp
def _spread(x, sel):
    return _d_nn(jnp.concatenate(_split3(x), axis=1), jnp.concatenate([sel, sel, sel], axis=0))


def _spread_fwd(x, sel):
    return _spread(x, sel), sel


def _spread_bwd(sel, g):
    g1, g2, g3 = _split3(g)
    return _d_nt(g1, sel) + _d_nt(g2, sel) + _d_nt(g3, sel), jnp.zeros_like(sel)


_spread.defvjp(_spread_fwd, _spread_bwd)


def f_gdn_pre(i, gq, gk, gv, gab, hq, hk, hv, cwq, cwk, cwv, alog, dtb):
    live = jnp.where(i == 0, 0.0, 1.0)
    q = _conv_silu(gq, hq * live, cwq)
    k = _conv_silu(gk, hk * live, cwk)
    v = _conv_silu(gv, hv * live, cwv)
    g = _spread(-jnp.exp(alog) * jax.nn.softplus(gab + dtb), _head_select(0))
    beta = _spread(jax.nn.sigmoid(gab), _head_select(HEADS))
    qs, ks, vs, gs, bs = [], [], [], [], []
    for h in range(HEADS):
        sl = slice(LANE * h, LANE * (h + 1))
        qh, kh = q[:, sl], k[:, sl]
        qs.append(qh * lax.rsqrt(jnp.sum(qh * qh, axis=-1, keepdims=True) + EPS) * (GDN_DIM ** -0.5))
        ks.append(kh * lax.rsqrt(jnp.sum(kh * kh, axis=-1, keepdims=True) + EPS))
        vs.append(v[:, sl])
        gs.append(g[:, sl])
        bs.append(beta[:, sl])
    return jnp.stack(qs), jnp.stack(ks), jnp.stack(vs), jnp.stack(gs), jnp.stack(bs)


def gdn_pre_bwd(i, gq, gk, gv, gab, hq, hk, hv, dq, dk, dv, dg, db, cwq, cwk, cwv, alog, dtb, cq, ck, cv):
    grads = _vjp_fn(f_gdn_pre, 12, 5)(i, gq, gk, gv, gab, hq, hk, hv, cwq, cwk, cwv, alog, dtb, dq, dk, dv, dg, db)
    dgq, dgk, dgv, dgab, dhq, dhk, dhv, dcwq, dcwk, dcwv, dalog, ddtb = grads

    def add_tail(dx, carry):
        return jnp.concatenate([dx[:-SUBLANE], dx[-SUBLANE:] + carry], axis=0)

    return (add_tail(dgq, cq), add_tail(dgk, ck), add_tail(dgv, cv), dgab,
            dcwq, dcwk, dcwv, dalog, ddtb, dhq, dhk, dhv)


def f_loss(i, x, h, tgt):
    e = x + h - tgt
    part = 0.5 * jnp.sum(e * e) * (1.0 / D_MODEL)
    dy = e * (1.0 / D_MODEL)
    return dy, dy, jnp.zeros((SUBLANE, LANE), F32) + part


def _flash_tile(t_len):
    return min(512, t_len)


FLASH_HEADS = 4
FLASH_BWD_HEADS = 2
LOG2E = 1.4426950408889634


def _causal(rows0, shape):
    r = rows0 + lax.broadcasted_iota(jnp.int32, shape, 0)
    c = lax.broadcasted_iota(jnp.int32, shape, 1)
    return c <= r


def flash_fwd(q, k, v):
    h_n, t_len, _ = q.shape
    tq = _flash_tile(t_len)
    nq = t_len // tq
    hb = FLASH_HEADS
    kw = 2 if nq % 2 == 0 else 1
    tk = kw * tq
    c2 = (QK ** -0.5) * LOG2E
    pairs = [(i, j) for i in range(nq) for j in range(i // kw + 1)]
    qt = jnp.array([p[0] for p in pairs], jnp.int32)
    kt = jnp.array([p[1] for p in pairs], jnp.int32)

    def body(qt_ref, kt_ref, q_ref, k_ref, v_ref, o_ref, lse_ref, m_s, acc_s):
        step = pl.program_id(1)
        qi, kj = qt_ref[step], kt_ref[step]
        last = qi // kw

        @pl.when(kj == 0)
        def _():
            m_s[...] = jnp.full_like(m_s, -jnp.inf)
            acc_s[...] = jnp.zeros_like(acc_s)

        def tile(diagonal):
            s = _bdot(q_ref[...], k_ref[...], _NT3, False) * c2
            if diagonal:
                s = jnp.where(_causal((qi % kw) * tq, (tq, tk))[None], s, -jnp.inf)
            m_old = m_s[...]
            m_new = jnp.maximum(m_old, jnp.max(s, axis=-1, keepdims=True))
            p = jnp.exp2(s - m_new).astype(_BF)
            v_ones = jnp.concatenate([v_ref[...], jnp.ones((hb, tk, LANE), _BF)], axis=2)
            acc_s[...] = jnp.exp2(m_old - m_new) * acc_s[...] + _bdot(p, v_ones, _NN3, False)
            m_s[...] = m_new

        @pl.when(kj < last)
        def _():
            tile(False)

        @pl.when(kj == last)
        def _():
            tile(True)
            acc = acc_s[...]
            l_sum = acc[:, :, LANE:]
            o = acc[:, :, :LANE] / l_sum
            for hh in range(hb):
                o_ref[:, LANE * hh:LANE * (hh + 1)] = o[hh]
            lse_ref[...] = m_s[...] + jnp.log2(jnp.max(l_sum, axis=-1, keepdims=True))

    return pl.pallas_call(
        body,
        name="flash_fwd",
        grid_spec=pltpu.PrefetchScalarGridSpec(
            num_scalar_prefetch=2,
            grid=(h_n // hb, qt.shape[0]),
            in_specs=[
                pl.BlockSpec((hb, tq, HEAD_PAD), lambda h, s, qt_ref, kt_ref: (h, qt_ref[s], 0)),
                pl.BlockSpec((hb, tk, HEAD_PAD), lambda h, s, qt_ref, kt_ref: (h, kt_ref[s], 0)),
                pl.BlockSpec((hb, tk, LANE), lambda h, s, qt_ref, kt_ref: (h, kt_ref[s], 0)),
            ],
            out_specs=[
                pl.BlockSpec((tq, hb * LANE), lambda h, s, qt_ref, kt_ref: (qt_ref[s], h)),
                pl.BlockSpec((hb, tq, 1), lambda h, s, qt_ref, kt_ref: (h, qt_ref[s], 0)),
            ],
            scratch_shapes=[pltpu.VMEM((hb, tq, 1), F32), pltpu.VMEM((hb, tq, 2 * LANE), F32)],
        ),
        out_shape=[jax.ShapeDtypeStruct((t_len, h_n * LANE), F32), jax.ShapeDtypeStruct((h_n, t_len, 1), F32)],
        compiler_params=_cparams("parallel", "arbitrary"),
    )(qt, kt, q, k, v)


def flash_bwd(q, k, v, do, lse, delta):
    h_n, t_len, _ = q.shape
    tq = _flash_tile(t_len)
    nq = t_len // tq
    hb = FLASH_BWD_HEADS
    kw = 2 if nq % 2 == 0 else 1
    tk = kw * tq
    pairs = [(i, j) for j in range(nq // kw) for i in range(kw * j, nq)]
    n_steps = len(pairs)
    qt = jnp.array([p[0] for p in pairs], jnp.int32)
    kt = jnp.array([p[1] for p in pairs], jnp.int32)
    scale = QK ** -0.5
    c2 = scale * LOG2E

    def body(qt_ref, kt_ref, q_ref, k_ref, v_ref, do_ref, lse_ref, dl_ref, dq_hbm, dk_ref, dv_ref, dq_s, dq_sem):
        group, step = pl.program_id(0), pl.program_id(1)
        qi, kj = qt_ref[step], kt_ref[step]

        @pl.when(step == 0)
        def _():
            dq_s[...] = jnp.zeros_like(dq_s)

        def tile(diagonal):
            qb, kb = q_ref[...], k_ref[...]
            dob = jnp.stack([do_ref[:, LANE * hh:LANE * (hh + 1)] for hh in range(hb)])
            p = jnp.exp2(_bdot(qb, kb, _NT3, False) * c2 - lse_ref[...])
            if diagonal:
                p = jnp.where(_causal((qi % kw) * tq, (tq, tk))[None], p, 0.0)
            dv = _bdot(p, dob, _TN3, False)
            ds = p * (_bdot(dob, v_ref[...], _NT3, False) - dl_ref[...]) * scale
            dk = _bdot(ds, qb, _TN3, False)
            dq_s[:, pl.ds(pl.multiple_of(qi * tq, tq), tq), :] += _bdot(ds, kb, _NN3, False)
            return dk, dv

        @pl.when(qi == kw * kj)
        def _():
            dk_ref[...], dv_ref[...] = tile(True)

        @pl.when((qi != kw * kj) & (qi // kw == kj))
        def _():
            dk, dv = tile(True)
            dk_ref[...] += dk
            dv_ref[...] += dv

        @pl.when(qi // kw > kj)
        def _():
            dk, dv = tile(False)
            dk_ref[...] += dk
            dv_ref[...] += dv

        @pl.when(step == n_steps - 1)
        def _():
            out = pltpu.make_async_copy(dq_s, dq_hbm.at[pl.ds(group * hb, hb)], dq_sem)
            out.start()
            out.wait()

    def qmap(h, s, qt_ref, kt_ref):
        return (h, qt_ref[s], 0)

    def kmap(h, s, qt_ref, kt_ref):
        return (h, kt_ref[s], 0)

    return pl.pallas_call(
        body,
        name="flash_bwd",
        grid_spec=pltpu.PrefetchScalarGridSpec(
            num_scalar_prefetch=2,
            grid=(h_n // hb, n_steps),
            in_specs=[
                pl.BlockSpec((hb, tq, HEAD_PAD), qmap),
                pl.BlockSpec((hb, tk, HEAD_PAD), kmap),
                pl.BlockSpec((hb, tk, LANE), kmap),
                pl.BlockSpec((tq, hb * LANE), lambda h, s, qt_ref, kt_ref: (qt_ref[s], h)),
                pl.BlockSpec((hb, tq, 1), qmap),
                pl.BlockSpec((hb, tq, 1), qmap),
            ],
            out_specs=[
                pl.BlockSpec(memory_space=pl.ANY),
                pl.BlockSpec((hb, tk, HEAD_PAD), kmap),
                pl.BlockSpec((hb, tk, LANE), kmap),
            ],
            scratch_shapes=[pltpu.VMEM((hb, t_len, HEAD_PAD), F32), pltpu.SemaphoreType.DMA],
        ),
        out_shape=[
            jax.ShapeDtypeStruct((h_n, t_len, HEAD_PAD), F32),
            jax.ShapeDtypeStruct((h_n, t_len, HEAD_PAD), F32),
            jax.ShapeDtypeStruct((h_n, t_len, LANE), F32),
        ],
        compiler_params=_cparams("parallel", "arbitrary"),
    )(qt, kt, q, k, v, do, lse, delta)


def _tri_ones(h_n):
    ii = lax.broadcasted_iota(jnp.int32, (h_n, CHUNK, CHUNK), 1)
    jj = lax.broadcasted_iota(jnp.int32, (h_n, CHUNK, CHUNK), 2)
    return (ii >= jj).astype(_BF)


@jax.custom_vjp
def _chunk_cumsum(gb):
    tri = _tri_ones(gb.shape[0])
    return _bdot(jnp.concatenate([tri, tri, tri], axis=2), jnp.concatenate(_split3(gb), axis=1), _NN3, False)


def _chunk_cumsum_bwd(_, ct):
    tri = _tri_ones(ct.shape[0])
    return (_bdot(jnp.concatenate([tri, tri, tri], axis=1), jnp.concatenate(_split3(ct), axis=1), _TN3, False),)


_chunk_cumsum.defvjp(lambda gb: (_chunk_cumsum(gb), None), _chunk_cumsum_bwd)


@jax.custom_vjp
def _pair_diff(gcb):
    g1, g2, g3 = _split3(gcb)
    lane = lax.broadcasted_iota(jnp.int32, (1, 1, LANE), 2)
    one, zero = jnp.ones((), _BF), jnp.zeros((), _BF)
    a = jnp.where(lane == 0, g1, jnp.where(lane == 1, g2, jnp.where(lane == 2, g3, jnp.where(lane < 6, one, zero))))
    b = jnp.where(lane < 3, one, jnp.where(lane == 3, -g1, jnp.where(lane == 4, -g2, jnp.where(lane == 5, -g3, zero))))
    return _bdot(a, b, _NT3, False)


def _pair_diff_bwd(_, ct):
    parts = _split3(ct)
    ones = jnp.ones((ct.shape[0], 3 * CHUNK, LANE), _BF)
    rows = _bdot(jnp.concatenate(parts, axis=2), ones, _NN3, False)
    cols = _bdot(jnp.concatenate(parts, axis=1), ones, _TN3, False)
    lane = lax.broadcasted_iota(jnp.int32, (1, 1, LANE), 2)
    return (jnp.where(lane == 0, rows - cols, 0.0),)


_pair_diff.defvjp(lambda gcb: (_pair_diff(gcb), None), _pair_diff_bwd)


def gdn_step(s, q, k, v, gb, bb, mm3=_bh3):
    c = CHUNK
    ii = lax.broadcasted_iota(jnp.int32, (1, c, c), 1)
    jj = lax.broadcasted_iota(jnp.int32, (1, c, c), 2)
    incl, strict = ii >= jj, ii > jj
    gcb = _chunk_cumsum(gb)
    diff = _pair_diff(gcb)
    decay = jnp.where(incl, jnp.exp(jnp.where(incl, diff, 0.0)), 0.0)
    kb, vb = k * bb, v * bb
    egc = jnp.exp(gcb)
    lmat = jnp.where(strict, _bmm_nt(kb, k) * decay, 0.0)
    inv = (ii == jj).astype(F32) - lmat
    pw = mm3(lmat, lmat)
    for step in range(5):
        inv = inv + mm3(inv, pw)
        if step < 4:
            pw = mm3(pw, pw)
    u = mm3(inv, vb)
    w = mm3(inv, kb * egc)
    attn = _bmm_nt(q, k) * decay
    qd = q * egc
    g_end = jnp.sum(gb, axis=1, keepdims=True)
    kd = k * jnp.exp(g_end - gcb)
    v_new = u - _bmm(w, s)
    o = _bmm(qd, s) + _bmm(attn, v_new)
    s_new = s * jnp.exp(g_end) + _bmm_tn(kd, v_new)
    return s_new, o


def gdn_fwd(q, k, v, gb, bb):
    h_n, t_len, d = q.shape
    n = t_len // CHUNK
    blk = pl.BlockSpec((h_n, CHUNK, d), lambda i: (0, i, 0))

    def body(q_ref, k_ref, v_ref, g_ref, b_ref, o_ref, sall_ref, s_s):
        @pl.when(pl.program_id(0) == 0)
        def _():
            s_s[...] = jnp.zeros_like(s_s)

        s = s_s[...]
        sall_ref[0] = s
        s_new, o = gdn_step(s, q_ref[...], k_ref[...], v_ref[...], g_ref[...], b_ref[...], mm3=_bh3_passes)
        o_ref[...] = o
        s_s[...] = s_new

    return pl.pallas_call(
        body,
        name="gdn_fwd",
        grid=(n,),
        in_specs=[blk] * 5,
        out_specs=[blk, pl.BlockSpec((1, h_n, d, d), lambda i: (i, 0, 0, 0))],
        out_shape=[jax.ShapeDtypeStruct((h_n, t_len, d), F32), jax.ShapeDtypeStruct((n, h_n, d, d), F32)],
        scratch_shapes=[pltpu.VMEM((h_n, d, d), F32)],
        compiler_params=_cparams("arbitrary"),
    )(q, k, v, gb, bb)


def gdn_bwd(q, k, v, gb, bb, s_all, do):
    h_n, t_len, d = q.shape
    n = t_len // CHUNK
    blk = pl.BlockSpec((h_n, CHUNK, d), lambda i: (0, n - 1 - i, 0))

    def body(q_ref, k_ref, v_ref, g_ref, b_ref, sall_ref, do_ref, dq_ref, dk_ref, dv_ref, dg_ref, db_ref, ds_s):
        @pl.when(pl.program_id(0) == 0)
        def _():
            ds_s[...] = jnp.zeros_like(ds_s)

        _, pull = jax.vjp(gdn_step, sall_ref[0], q_ref[...], k_ref[...], v_ref[...], g_ref[...], b_ref[...])
        ds, dq, dk, dv, dg, db = pull((ds_s[...], do_ref[...]))
        ds_s[...] = ds
        dq_ref[...], dk_ref[...], dv_ref[...], dg_ref[...], db_ref[...] = dq, dk, dv, dg, db

    return pl.pallas_call(
        body,
        name="gdn_bwd",
        grid=(n,),
        in_specs=[blk] * 5 + [pl.BlockSpec((1, h_n, d, d), lambda i: (n - 1 - i, 0, 0, 0)), blk],
        out_specs=[blk] * 5,
        out_shape=[jax.ShapeDtypeStruct((h_n, t_len, d), F32)] * 5,
        scratch_shapes=[pltpu.VMEM((h_n, d, d), F32)],
        compiler_params=_cparams("arbitrary"),
    )(q, k, v, gb, bb, s_all, do)


def _pad_cols(a, n):
    return jnp.pad(a, ((0, 0), (0, n - a.shape[1])))


def arrange_w_in(w):
    pieces, start = [], 0
    for n in R_SPLITS:
        pieces.append(w[:, start:start + n])
        start += n
    cq, ckv, kr, mgate, gq, gk, gv, ga, gb, ggate = pieces
    return jnp.concatenate([mgate, gq, gk, gv, ggate, cq, ckv, _pad_cols(kr, LANE),
                            _pad_cols(jnp.concatenate([ga, gb], axis=1), LANE)], axis=1)


def unarrange_w_in(g):
    def cols(start, n):
        return g[:, start:start + n]
    return jnp.concatenate([cols(P_CQ, Q_LORA), cols(P_CKV, KV_LORA), cols(P_KR, ROPE), cols(P_MGATE, WIDTH),
                            cols(P_GQ, WIDTH), cols(P_GK, WIDTH), cols(P_GV, WIDTH), cols(P_GAB, HEADS),
                            cols(P_GAB + HEADS, HEADS), cols(P_GGATE, WIDTH)], axis=1)


def arrange_w_uq(w):
    w = w.reshape(w.shape[0], HEADS, QK)
    return jnp.pad(w, ((0, 0), (0, 0), (0, HEAD_PAD - QK))).reshape(w.shape[0], HEADS * HEAD_PAD)


def unarrange_w_uq(g):
    return g.reshape(g.shape[0], HEADS, HEAD_PAD)[:, :, :QK].reshape(g.shape[0], HEADS * QK)


def local_step(x, pos, tgt, p):
    t_len = x.shape[0]
    w_in, w_uq, w_ukv, w_out = p["w_in"], p["w_uq"], p["w_ukv"], p["w_out"]
    norm_gain = p["norm_gain"].reshape(1, D_MODEL)
    qa_gain = p["mla_q_a_gain"].reshape(1, Q_LORA)
    kva_gain = p["mla_kv_a_gain"].reshape(1, KV_LORA)
    qg = _pad_cols(p["mla_q_norm_gain"].reshape(1, QK), HEAD_PAD)
    kg = _pad_cols(p["mla_k_norm_gain"].reshape(1, QK), HEAD_PAD)
    cw = p["gdn_conv_w"].reshape(CONV_W, 3 * WIDTH)
    cwq, cwk, cwv = cw[:, :WIDTH], cw[:, WIDTH:2 * WIDTH], cw[:, 2 * WIDTH:]
    alog = _pad_cols(p["gdn_a_log"].reshape(1, HEADS), LANE)
    dtb = _pad_cols(p["gdn_dt_bias"].reshape(1, HEADS), LANE)
    og = p["gdn_out_norm_gain"].reshape(1, GDN_DIM)
    half = ROPE // 2
    inv_freq = jnp.power(ROPE_THETA, -jnp.arange(half, dtype=F32) / half)
    invf = _pad_cols(jnp.concatenate([inv_freq, inv_freq]).reshape(1, ROPE), LANE)

    rt = 256
    r = "r"
    (xn,) = rowwise("rms_x", f_rms_x, t_len, rt, [(x, (r, D_MODEL, 0))], [norm_gain], [(r, D_MODEL, _BF)])
    proj = matmul("proj", xn, w_in, "nn")
    cq_in = (proj, (r, Q_LORA, P_CQ // Q_LORA))
    ckv_in = (proj, (r, KV_LORA, P_CKV // KV_LORA))
    kr_in = (proj, (r, LANE, P_KR // LANE))
    mgate_in = (proj, (r, WIDTH, P_MGATE // WIDTH))
    ggate_in = (proj, (r, WIDTH, P_GGATE // WIDTH))
    gqkv_in = [(proj, (r, WIDTH, P_GQ // WIDTH)), (proj, (r, WIDTH, P_GK // WIDTH)), (proj, (r, WIDTH, P_GV // WIDTH))]
    gab_in = (proj, (r, LANE, P_GAB // LANE))
    halos = [(proj, ("halo", WIDTH, P_GQ // WIDTH)), (proj, ("halo", WIDTH, P_GK // WIDTH)),
             (proj, ("halo", WIDTH, P_GV // WIDTH))]

    q_lat, kv_lat = rowwise("lat", f_lat, t_len, rt, [cq_in, ckv_in], [qa_gain, kva_gain],
                            [(r, Q_LORA, _BF), (r, KV_LORA, _BF)])
    q_raw = matmul("q_up", q_lat, w_uq, "nn")
    kv_raw = matmul("kv_up", kv_lat, w_ukv, "nn")
    wide = HEADS * HEAD_PAD
    head_in = [(q_raw, (r, wide, 0)), (kv_raw, (r, wide, 0)), kr_in]
    pos_in = (pos, (r, 1, 0))
    q_full, k_full, v_mla = rowwise(
        "head", lambda i, qr, kvr, kr, ps, qg_, kg_, iv: f_head(i, qr, kvr, kr, qg_, kg_, ps, iv), t_len, rt,
        head_in + [pos_in], [qg, kg, invf],
        [("h", HEADS, HEAD_PAD, _BF), ("h", HEADS, HEAD_PAD, _BF), ("h", HEADS, LANE, _BF)])
    o_mla, lse = flash_fwd(q_full, k_full, v_mla)

    pre_in = gqkv_in + [gab_in] + halos
    pre_full = [cwq, cwk, cwv, alog, dtb]
    hkind = ("h", HEADS, GDN_DIM, F32)
    gq_n, gk_n, gv_n, g_b, b_b = rowwise("gdn_pre", f_gdn_pre, t_len, rt, pre_in, pre_full, [hkind] * 5)
    o_gdn, s_all = gdn_fwd(gq_n, gk_n, gv_n, g_b, b_b)

    mix_in = [(o_mla, (r, WIDTH, 0)), mgate_in, (o_gdn, ("h",)), ggate_in]
    (mixed,) = rowwise("mix", f_mix, t_len, rt, mix_in, [og], [(r, 2 * WIDTH, _BF)])
    h_out = matmul("out_proj", mixed, w_out, "nn")
    dy, dy_mx, loss_acc = rowwise("loss", f_loss, t_len, rt,
                                  [(x, (r, D_MODEL, 0)), (h_out, (r, D_MODEL, 0)), (tgt, (r, D_MODEL, 0))], [],
                                  [(r, D_MODEL, F32), (r, D_MODEL, _BF)], [(SUBLANE, LANE)])
    loss = loss_acc[0, 0]

    d_mixed = matmul("d_mixed", dy_mx, w_out, "nt")
    g_w_out = matmul("g_w_out", mixed, dy_mx, "tn")

    def mix_bwd(i, o_mla_, mgate_, o_gdn_, ggate_, d_mixed_, og_):
        do_mla_, d_mgate_, do_gdn_, d_ggate_, g_og_ = _vjp_fn(f_mix, 5, 1)(i, o_mla_, mgate_, o_gdn_, ggate_, og_, d_mixed_)
        delta_ = jnp.stack([jnp.sum(o_mla_[:, LANE * h:LANE * (h + 1)] * do_mla_[:, LANE * h:LANE * (h + 1)],
                                    axis=-1, keepdims=True) for h in range(HEADS)])
        return do_mla_, d_mgate_, do_gdn_, d_ggate_, delta_, g_og_

    do_mla, d_mgate, do_gdn, d_ggate, delta, g_og = rowwise(
        "mix_bwd", mix_bwd, t_len, rt, mix_in + [(d_mixed, (r, 2 * WIDTH, 0))], [og],
        [(r, WIDTH, F32), (r, WIDTH, _BF), hkind, (r, WIDTH, _BF), ("h", HEADS, 1, F32)], [(1, GDN_DIM)])
    dq_n, dk_n, dv_n, dg_b, db_b = gdn_bwd(gq_n, gk_n, gv_n, g_b, b_b, s_all, do_gdn)
    cts_in = [(a, ("h",)) for a in (dq_n, dk_n, dv_n, dg_b, db_b)]
    d_gq, d_gk, d_gv, d_gab, g_cwq, g_cwk, g_cwv, g_alog, g_dtb = rowwise(
        "gdn_pre_bwd", gdn_pre_bwd, t_len, rt, pre_in + cts_in, pre_full,
        [(r, WIDTH, _BF)] * 3 + [(r, LANE, _BF)],
        [(CONV_W, WIDTH)] * 3 + [(1, LANE)] * 2, carries=[(SUBLANE, WIDTH)] * 3, reverse=True)

    dq_full, dk_full, dv_mla = flash_bwd(q_full, k_full, v_mla, do_mla, lse, delta)
    head_cts = [(a, ("h",)) for a in (dq_full, dk_full, dv_mla)]

    def head_bwd(i, q_raw_, kv_raw_, kr_, pos_, dq_, dk_, dv_, qg_, kg_, invf_):
        return _vjp_fn(f_head, 5, 3)(i, q_raw_, kv_raw_, kr_, qg_, kg_, pos_, invf_, dq_, dk_, dv_)

    dq_raw, dkv_raw, d_kr, g_qg, g_kg = rowwise(
        "head_bwd", head_bwd, t_len, rt, head_in + [pos_in] + head_cts, [qg, kg, invf],
        [(r, wide, _BF), (r, wide, _BF), (r, LANE, _BF)], [(1, HEAD_PAD), (1, HEAD_PAD)])
    dq_lat = matmul("dq_lat", dq_raw, w_uq, "nt")
    g_w_uq = matmul("g_w_uq", q_lat, dq_raw, "tn")
    dkv_lat = matmul("dkv_lat", dkv_raw, w_ukv, "nt")
    g_w_ukv = matmul("g_w_ukv", kv_lat, dkv_raw, "tn")

    def lat_bwd(i, cq_, ckv_, dql_, dkl_, gq_, gkv_):
        return _vjp_fn(f_lat, 4, 2)(i, cq_, ckv_, gq_, gkv_, dql_, dkl_)

    d_cq, d_ckv, g_qa, g_kva = rowwise(
        "lat_bwd", lat_bwd, t_len, rt, [cq_in, ckv_in, (dq_lat, (r, Q_LORA, 0)), (dkv_lat, (r, KV_LORA, 0))],
        [qa_gain, kva_gain], [(r, Q_LORA, _BF), (r, KV_LORA, _BF)], [(1, Q_LORA), (1, KV_LORA)])

    d_proj = jnp.concatenate([d_mgate, d_gq, d_gk, d_gv, d_ggate, d_cq, d_ckv, d_kr, d_gab], axis=1)
    d_xn = matmul("d_xn", d_proj, w_in, "nt")
    g_w_in = matmul("g_w_in", xn, d_proj, "tn")

    def rms_x_bwd(i, x_, dxn_, dy_, gain_):
        dx, dgain = _vjp_fn(f_rms_x, 2, 1)(i, x_, gain_, dxn_)
        return dx + dy_, dgain

    grad_x, g_norm = rowwise("rms_x_bwd", rms_x_bwd, t_len, rt,
                             [(x, (r, D_MODEL, 0)), (d_xn, (r, D_MODEL, 0)), (dy, (r, D_MODEL, 0))], [norm_gain],
                             [(r, D_MODEL, F32)], [(1, D_MODEL)])

    grads = {
        "norm_gain": g_norm, "w_in": g_w_in, "mla_q_a_gain": g_qa, "mla_kv_a_gain": g_kva, "w_uq": g_w_uq,
        "w_ukv": g_w_ukv, "mla_q_norm_gain": g_qg[:, :QK], "mla_k_norm_gain": g_kg[:, :QK],
        "gdn_conv_w": jnp.concatenate([g_cwq, g_cwk, g_cwv], axis=1), "gdn_a_log": g_alog[:, :HEADS],
        "gdn_dt_bias": g_dtb[:, :HEADS], "gdn_out_norm_gain": g_og, "w_out": g_w_out,
    }
    return loss, grad_x, grads


MESH = pl.DeviceIdType.MESH
ANY = pl.BlockSpec(memory_space=pl.ANY)
CHIP_FLIPS = ((1, 0), (0, 1), (1, 1))


def _place():
    return lax.axis_index("x"), lax.axis_index("y"), lax.axis_index("c")


def _flip(v, f):
    return 1 - v if f else v


def all_gather(shards):
    n_arr = len(shards)

    def body(*refs):
        x_refs, o_refs = refs[:n_arr], refs[n_arr:2 * n_arr]
        send_sems, recv_sems, local_sems = refs[2 * n_arr:]
        x, y, c = _place()
        me, sibling = (x, y, c), (x, y, 1 - c)
        chips = [(_flip(x, fx), _flip(y, fy)) for fx, fy in CHIP_FLIPS]

        def copy(a, k, block, to, src=None):
            px, py, pc = block
            dst = o_refs[a].at[4 * px + 2 * py + pc]
            return pltpu.make_async_remote_copy(
                src_ref=dst if src is None else src, dst_ref=dst, send_sem=send_sems.at[a, k],
                recv_sem=recv_sems.at[a, k], device_id=to, device_id_type=MESH)

        mine, first, passed = [], [], []
        for a in range(n_arr):
            cp = pltpu.make_async_copy(x_refs[a], o_refs[a].at[4 * x + 2 * y + c], local_sems.at[a])
            cp.start()
            mine.append(cp)
            first.append(copy(a, 0, me, sibling, src=x_refs[a]))
            first += [copy(a, 1 + j, me, (*chip, c), src=x_refs[a]) for j, chip in enumerate(chips)]
        for cp in first:
            cp.start()
        for j, chip in enumerate(chips):
            for a in range(n_arr):
                copy(a, 1 + j, (*chip, c), me).wait_recv()
                cp = copy(a, 4 + j, (*chip, c), sibling)
                cp.start()
                passed.append(cp)
        for a in range(n_arr):
            copy(a, 0, sibling, me).wait_recv()
            for j, chip in enumerate(chips):
                copy(a, 4 + j, (*chip, 1 - c), me).wait_recv()
        for cp in first + passed:
            cp.wait_send()
        for cp in mine:
            cp.wait()

    return pl.pallas_call(
        body,
        name="all_gather",
        out_shape=[jax.ShapeDtypeStruct((N_DEV,) + s.shape, s.dtype) for s in shards],
        in_specs=[ANY] * n_arr,
        out_specs=[ANY] * n_arr,
        scratch_shapes=[pltpu.SemaphoreType.DMA((n_arr, 7)), pltpu.SemaphoreType.DMA((n_arr, 7)),
                        pltpu.SemaphoreType.DMA((n_arr,))],
    )(*shards)


def exchange_cores(grads):
    n_arr = len(grads)

    def body(*refs):
        g_refs, o_refs = refs[:n_arr], refs[n_arr:2 * n_arr]
        send_sems, recv_sems = refs[2 * n_arr:]
        x, y, c = _place()
        copies = []
        for a in range(n_arr):
            for q in range(4):
                cp = pltpu.make_async_remote_copy(
                    src_ref=g_refs[a].at[2 * q + (1 - c)], dst_ref=o_refs[a].at[q], send_sem=send_sems.at[a, q],
                    recv_sem=recv_sems.at[a, q], device_id=(x, y, 1 - c), device_id_type=MESH)
                cp.start()
                copies.append(cp)
        for cp in copies:
            cp.wait()

    return pl.pallas_call(
        body,
        name="exchange_cores",
        out_shape=[jax.ShapeDtypeStruct((4,) + g.shape[1:], g.dtype) for g in grads],
        in_specs=[ANY] * n_arr,
        out_specs=[ANY] * n_arr,
        scratch_shapes=[pltpu.SemaphoreType.DMA((n_arr, 4)), pltpu.SemaphoreType.DMA((n_arr, 4))],
    )(*grads)


def exchange_chips(parts):
    n_arr = len(parts)

    def body(*refs):
        p_refs, o_refs = refs[:n_arr], refs[n_arr:2 * n_arr]
        send_sems, recv_sems = refs[2 * n_arr:]
        x, y, c = _place()
        copies = []
        for a in range(n_arr):
            for j, (fx, fy) in enumerate(CHIP_FLIPS):
                px, py = _flip(x, fx), _flip(y, fy)
                cp = pltpu.make_async_remote_copy(
                    src_ref=p_refs[a].at[2 * px + py], dst_ref=o_refs[a].at[j], send_sem=send_sems.at[a, j],
                    recv_sem=recv_sems.at[a, j], device_id=(px, py, c), device_id_type=MESH)
                cp.start()
                copies.append(cp)
        for cp in copies:
            cp.wait()

    return pl.pallas_call(
        body,
        name="exchange_chips",
        out_shape=[jax.ShapeDtypeStruct((3,) + p.shape[1:], p.dtype) for p in parts],
        in_specs=[ANY] * n_arr,
        out_specs=[ANY] * n_arr,
        scratch_shapes=[pltpu.SemaphoreType.DMA((n_arr, 3)), pltpu.SemaphoreType.DMA((n_arr, 3))],
    )(*parts)


def gather_small(v):
    def body(v_ref, o_ref, send_sems, recv_sems, local_sem):
        x, y, c = _place()
        me = 4 * x + 2 * y + c
        mine = pltpu.make_async_copy(v_ref, o_ref.at[me], local_sem)
        mine.start()
        copies = []
        for k in range(1, N_DEV):
            fx, fy, fc = (k >> 2) & 1, (k >> 1) & 1, k & 1
            cp = pltpu.make_async_remote_copy(
                src_ref=v_ref, dst_ref=o_ref.at[me], send_sem=send_sems.at[k - 1], recv_sem=recv_sems.at[k - 1],
                device_id=(_flip(x, fx), _flip(y, fy), _flip(c, fc)), device_id_type=MESH)
            cp.start()
            copies.append(cp)
        for cp in copies:
            cp.wait()
        mine.wait()

    return pl.pallas_call(
        body,
        name="gather_small",
        out_shape=jax.ShapeDtypeStruct((N_DEV,) + v.shape, v.dtype),
        in_specs=[ANY],
        out_specs=ANY,
        scratch_shapes=[pltpu.SemaphoreType.DMA((N_DEV - 1,)), pltpu.SemaphoreType.DMA((N_DEV - 1,)),
                        pltpu.SemaphoreType.DMA],
    )(v)


def _row_tile(rows):
    for t in (256, 128, 64, 32, 16, 8):
        if rows % t == 0:
            return t
    return rows


def add_core_parts(name, g, recv, c_idx, wire):
    _, rows, cols = g.shape
    tr = _row_tile(rows)

    def body(c_ref, g_ref, r_ref, o_ref, w_ref):
        part = g_ref[...] + r_ref[...]
        o_ref[...] = part
        w_ref[...] = part.astype(w_ref.dtype)

    blk = pl.BlockSpec((1, tr, cols), lambda q, i, c_ref: (q, i, 0))
    return pl.pallas_call(
        body,
        name=name,
        grid_spec=pltpu.PrefetchScalarGridSpec(
            num_scalar_prefetch=1,
            grid=(4, rows // tr),
            in_specs=[pl.BlockSpec((1, tr, cols), lambda q, i, c_ref: (2 * q + c_ref[0], i, 0)), blk],
            out_specs=[blk, blk],
        ),
        out_shape=[jax.ShapeDtypeStruct((4, rows, cols), F32), jax.ShapeDtypeStruct((4, rows, cols), wire)],
        compiler_params=_cparams("parallel", "parallel"),
    )(c_idx, g, recv)


def _adamw(w, g, m, v):
    m = ADAM_B1 * m + (1.0 - ADAM_B1) * g
    v = ADAM_B2 * v + (1.0 - ADAM_B2) * (g * g)
    m_hat = m / (1.0 - ADAM_B1 ** ADAM_STEP)
    v_hat = v / (1.0 - ADAM_B2 ** ADAM_STEP)
    delta = -ADAM_LR * (m_hat / (jnp.sqrt(v_hat) + ADAM_EPS) + ADAM_WD * w)
    return delta, m, v


def adamw_sharded(name, parts, recv, q_idx, w, m, v):
    rows, cols = w.shape
    tr = _row_tile(rows)

    def body(q_ref, p_ref, r_ref, w_ref, m_ref, v_ref, g_out, d_out, m_out, v_out):
        g = p_ref[0] + r_ref[0].astype(F32) + r_ref[1].astype(F32) + r_ref[2].astype(F32)
        d, m_new, v_new = _adamw(w_ref[...], g, m_ref[...], v_ref[...])
        g_out[...], d_out[...], m_out[...], v_out[...] = g, d, m_new, v_new

    blk = pl.BlockSpec((tr, cols), lambda i, q_ref: (i, 0))
    return pl.pallas_call(
        body,
        name=name,
        grid_spec=pltpu.PrefetchScalarGridSpec(
            num_scalar_prefetch=1,
            grid=(rows // tr,),
            in_specs=[pl.BlockSpec((1, tr, cols), lambda i, q_ref: (q_ref[0], i, 0)),
                      pl.BlockSpec((3, tr, cols), lambda i, q_ref: (0, i, 0)), blk, blk, blk],
            out_specs=[blk] * 4,
        ),
        out_shape=[jax.ShapeDtypeStruct((rows, cols), F32)] * 4,
        compiler_params=_cparams("parallel"),
    )(q_idx, parts, recv, w, m, v)


def adamw_small(gathered, w, m, v):
    def body(g_ref, w_ref, m_ref, v_ref, g_out, d_out, m_out, v_out):
        g = g_ref[0]
        for j in range(1, N_DEV):
            g = g + g_ref[j]
        d, m_new, v_new = _adamw(w_ref[...], g, m_ref[...], v_ref[...])
        g_out[...], d_out[...], m_out[...], v_out[...] = g, d, m_new, v_new

    return pl.pallas_call(body, name="adamw_small", out_shape=[jax.ShapeDtypeStruct(w.shape, F32)] * 4)(gathered, w, m, v)


SHARDED = ("w_in", "w_uq", "w_ukv", "gdn_conv_w", "w_out")
SMALL = (("norm_gain", D_MODEL), ("mla_q_a_gain", Q_LORA), ("mla_kv_a_gain", KV_LORA), ("mla_q_norm_gain", QK),
         ("mla_k_norm_gain", QK), ("gdn_a_log", HEADS), ("gdn_dt_bias", HEADS), ("gdn_out_norm_gain", GDN_DIM))
WEIGHT_ORDER = ("norm_gain", "w_in", "mla_q_a_gain", "mla_kv_a_gain", "w_uq", "w_ukv", "mla_q_norm_gain",
                "mla_k_norm_gain", "gdn_conv_w", "gdn_a_log", "gdn_dt_bias", "gdn_out_norm_gain", "w_out")


def _pack_small(d):
    rows = []
    for name, n in SMALL:
        a = d[name].reshape(-1).astype(F32)
        n_pad = -(-n // LANE) * LANE
        rows.append(jnp.pad(a, (0, n_pad - n)).reshape(n_pad // LANE, LANE))
    packed = jnp.concatenate(rows, axis=0)
    return jnp.pad(packed, ((0, -packed.shape[0] % SUBLANE), (0, 0)))


def _unpack_small(packed):
    out, row = {}, 0
    for name, n in SMALL:
        n_rows = -(-n // LANE)
        out[name] = packed[row:row + n_rows].reshape(-1)[:n].reshape(1, n)
        row += n_rows
    return out


def kernel(x, positions, norm_gain, w_in, mla_q_a_gain, mla_kv_a_gain, w_uq, w_ukv, mla_q_norm_gain, mla_k_norm_gain, gdn_conv_w, gdn_a_log, gdn_dt_bias, gdn_out_norm_gain, w_out, loss_target, m_norm_gain, m_w_in, m_mla_q_a_gain, m_mla_kv_a_gain, m_w_uq, m_w_ukv, m_mla_q_norm_gain, m_mla_k_norm_gain, m_gdn_conv_w, m_gdn_a_log, m_gdn_dt_bias, m_gdn_out_norm_gain, m_w_out, v_norm_gain, v_w_in, v_mla_q_a_gain, v_mla_kv_a_gain, v_w_uq, v_w_ukv, v_mla_q_norm_gain, v_mla_k_norm_gain, v_gdn_conv_w, v_gdn_a_log, v_gdn_dt_bias, v_gdn_out_norm_gain, v_w_out):
    w = dict(norm_gain=norm_gain, w_in=w_in, mla_q_a_gain=mla_q_a_gain, mla_kv_a_gain=mla_kv_a_gain, w_uq=w_uq,
             w_ukv=w_ukv, mla_q_norm_gain=mla_q_norm_gain, mla_k_norm_gain=mla_k_norm_gain, gdn_conv_w=gdn_conv_w,
             gdn_a_log=gdn_a_log, gdn_dt_bias=gdn_dt_bias, gdn_out_norm_gain=gdn_out_norm_gain, w_out=w_out)
    m = dict(norm_gain=m_norm_gain, w_in=m_w_in, mla_q_a_gain=m_mla_q_a_gain, mla_kv_a_gain=m_mla_kv_a_gain,
             w_uq=m_w_uq, w_ukv=m_w_ukv, mla_q_norm_gain=m_mla_q_norm_gain, mla_k_norm_gain=m_mla_k_norm_gain,
             gdn_conv_w=m_gdn_conv_w, gdn_a_log=m_gdn_a_log, gdn_dt_bias=m_gdn_dt_bias,
             gdn_out_norm_gain=m_gdn_out_norm_gain, w_out=m_w_out)
    v = dict(norm_gain=v_norm_gain, w_in=v_w_in, mla_q_a_gain=v_mla_q_a_gain, mla_kv_a_gain=v_mla_kv_a_gain,
             w_uq=v_w_uq, w_ukv=v_w_ukv, mla_q_norm_gain=v_mla_q_norm_gain, mla_k_norm_gain=v_mla_k_norm_gain,
             gdn_conv_w=v_gdn_conv_w, gdn_a_log=v_gdn_a_log, gdn_dt_bias=v_gdn_dt_bias,
             gdn_out_norm_gain=v_gdn_out_norm_gain, w_out=v_w_out)
    t_len = x.shape[1]

    shards = [w[n][0] if n == "gdn_conv_w" else w[n][0].astype(_BF) for n in SHARDED]
    a_w_in, a_w_uq, a_w_ukv, a_cw, a_w_out = all_gather(shards)

    def cols_whole(g):
        return g.transpose(1, 0, 2).reshape(g.shape[1], N_DEV * g.shape[2])

    p = {n: w[n] for n, _ in SMALL}
    p["w_in"] = arrange_w_in(cols_whole(a_w_in))
    p["w_uq"] = arrange_w_uq(cols_whole(a_w_uq))
    p["w_ukv"] = cols_whole(a_w_ukv)
    p["gdn_conv_w"] = cols_whole(a_cw)
    p["w_out"] = a_w_out.reshape(N_DEV * a_w_out.shape[1], a_w_out.shape[2])

    pos = positions.reshape(t_len, 1).astype(F32)
    loss, grad_x, grads = local_step(x.reshape(t_len, D_MODEL), pos, loss_target.reshape(t_len, D_MODEL), p)
    loss = lax.psum(loss, ("x", "y", "c"))

    def col_blocks(g):
        return g.reshape(g.shape[0], N_DEV, g.shape[1] // N_DEV).transpose(1, 0, 2)

    blocks = [col_blocks(unarrange_w_in(grads["w_in"])), col_blocks(unarrange_w_uq(grads["w_uq"])),
              col_blocks(grads["w_ukv"]), col_blocks(grads["gdn_conv_w"]),
              grads["w_out"].reshape(N_DEV, D_MODEL // N_DEV, D_MODEL)]
    xi, yi, ci = _place()
    c_idx = jnp.reshape(ci, (1,)).astype(jnp.int32)
    q_idx = jnp.reshape(2 * xi + yi, (1,)).astype(jnp.int32)
    from_sibling = exchange_cores(blocks)
    parts = [add_core_parts("add_" + n, g, r, c_idx, F32 if n == "gdn_conv_w" else _BF)
             for n, g, r in zip(SHARDED, blocks, from_sibling)]
    from_chips = exchange_chips([wire for _, wire in parts])
    out = {}
    for n, (prt, _), rcv in zip(SHARDED, parts, from_chips):
        shape = w[n].shape
        res = adamw_sharded("adamw_" + n, prt, rcv, q_idx, w[n].reshape(shape[-2:]), m[n].reshape(shape[-2:]),
                            v[n].reshape(shape[-2:]))
        out[n] = [a.reshape(shape) for a in res]

    small_all = gather_small(_pack_small(grads))
    res = adamw_small(small_all, _pack_small(w), _pack_small(m), _pack_small(v))
    unpacked = [_unpack_small(a) for a in res]
    for n, _ in SMALL:
        out[n] = [u[n] for u in unpacked]

    return (loss, grad_x.reshape(x.shape), *[out[n][0] for n in WEIGHT_ORDER], *[out[n][1] for n in WEIGHT_ORDER],
            *[out[n][2] for n in WEIGHT_ORDER], *[out[n][3] for n in WEIGHT_ORDER])
```

```python
import functools

import jax
import jax.numpy as jnp
from jax import lax
from jax.experimental import pallas as pl
from jax.experimental.pallas import tpu as pltpu

F32 = jnp.float32
_BF = jnp.bfloat16
HI = lax.Precision.HIGHEST

D_MODEL = 2048
HEADS = 8
NOPE = 128
ROPE = 64
QK = NOPE + ROPE
Q_LORA = 512
KV_LORA = 256
HEAD_PAD = 256
GDN_DIM = 128
WIDTH = HEADS * 128
CONV_W = 4
CHUNK = 64
ROPE_THETA = 10000.0
EPS = 1e-6
N_DEV = 8
LANE = 128
SUBLANE = 8
VMEM_LIMIT = 48 * 1024 * 1024

ADAM_LR, ADAM_B1, ADAM_B2, ADAM_EPS, ADAM_WD, ADAM_STEP = 0.001, 0.9, 0.999, 1e-08, 0.01, 10

P_MGATE, P_GQ, P_GK, P_GV, P_GGATE = 0, 1024, 2048, 3072, 4096
P_CQ, P_CKV, P_KR, P_GAB = 5120, 5632, 5888, 6016
P_COLS = 6144
R_SPLITS = (512, 256, 64, 1024, 1024, 1024, 1024, 8, 8, 1024)


def _cparams(*sem):
    return pltpu.CompilerParams(dimension_semantics=sem, vmem_limit_bytes=VMEM_LIMIT)


def _d_nn(a, b):
    return jnp.dot(a.astype(_BF), b.astype(_BF), preferred_element_type=F32)


def _d_nt(a, b):
    return lax.dot_general(a.astype(_BF), b.astype(_BF), (((1,), (1,)), ((), ())), preferred_element_type=F32)


def _d_tn(a, b):
    return lax.dot_general(a.astype(_BF), b.astype(_BF), (((0,), (0,)), ((), ())), preferred_element_type=F32)


@jax.custom_vjp
def _mm(a, b):
    return _d_nn(a, b)


_mm.defvjp(lambda a, b: (_d_nn(a, b), (a, b)), lambda r, g: (_d_nt(g, r[1]), _d_tn(r[0], g)))


@jax.custom_vjp
def _mm_nt(a, b):
    return _d_nt(a, b)


_mm_nt.defvjp(lambda a, b: (_d_nt(a, b), (a, b)), lambda r, g: (_d_nn(g, r[1]), _d_tn(g, r[0])))


@jax.custom_vjp
def _mm_tn(a, b):
    return _d_tn(a, b)


_mm_tn.defvjp(lambda a, b: (_d_tn(a, b), (a, b)), lambda r, g: (_d_nt(r[1], g), _d_nn(r[0], g)))


def _hi(a, b):
    return jnp.dot(a, b, preferred_element_type=F32, precision=HI)


_NN3 = (((2,), (1,)), ((0,), (0,)))
_NT3 = (((2,), (2,)), ((0,), (0,)))
_TN3 = (((1,), (1,)), ((0,), (0,)))


def _bdot(a, b, dims, hi):
    if hi:
        return lax.dot_general(a, b, dims, preferred_element_type=F32, precision=hi)
    return lax.dot_general(a.astype(_BF), b.astype(_BF), dims, preferred_element_type=F32)


def _batched_matmuls(hi):
    nn = jax.custom_vjp(lambda a, b: _bdot(a, b, _NN3, hi))
    nt = jax.custom_vjp(lambda a, b: _bdot(a, b, _NT3, hi))
    tn = jax.custom_vjp(lambda a, b: _bdot(a, b, _TN3, hi))
    nn.defvjp(lambda a, b: (_bdot(a, b, _NN3, hi), (a, b)),
              lambda r, g: (_bdot(g, r[1], _NT3, hi), _bdot(r[0], g, _TN3, hi)))
    nt.defvjp(lambda a, b: (_bdot(a, b, _NT3, hi), (a, b)),
              lambda r, g: (_bdot(g, r[1], _NN3, hi), _bdot(g, r[0], _TN3, hi)))
    tn.defvjp(lambda a, b: (_bdot(a, b, _TN3, hi), (a, b)),
              lambda r, g: (_bdot(r[1], g, _NT3, hi), _bdot(r[0], g, _NN3, hi)))
    return nn, nt, tn


_bmm, _bmm_nt, _bmm_tn = _batched_matmuls(False)
_bhi, _bhi_nt, _bhi_tn = _batched_matmuls(HI)


def _split2(x):
    hi = x.astype(_BF)
    return hi, (x - hi.astype(F32)).astype(_BF)


def _pdot(a, b, mode):
    (a_hi, a_lo), (b_hi, b_lo) = _split2(a), _split2(b)
    a_ax, b_ax, dims = {"nn": (2, 1, _NN3), "nt": (2, 2, _NT3), "tn": (1, 1, _TN3)}[mode]
    lhs = jnp.concatenate([a_hi, a_lo, a_hi], axis=a_ax)
    rhs = jnp.concatenate([b_hi, b_hi, b_lo], axis=b_ax)
    return lax.dot_general(lhs, rhs, dims, preferred_element_type=F32)


def _packed_matmuls():
    nn = jax.custom_vjp(lambda a, b: _pdot(a, b, "nn"))
    nn.defvjp(lambda a, b: (_pdot(a, b, "nn"), (a, b)), lambda r, g: (_pdot(g, r[1], "nt"), _pdot(r[0], g, "tn")))
    return nn


_bh3 = _packed_matmuls()
_bh3_passes = _batched_matmuls(lax.Precision.HIGH)[0]


@functools.partial(jax.custom_vjp, nondiff_argnums=(1, 2))
def _roll(x, shift, axis):
    return pltpu.roll(x, shift, axis)


def _roll_fwd(x, shift, axis):
    return pltpu.roll(x, shift, axis), None


def _roll_bwd(shift, axis, _, g):
    n = g.shape[axis]
    return (pltpu.roll(g, (n - shift) % n, axis),)


_roll.defvjp(_roll_fwd, _roll_bwd)


def _rms(x, gain):
    return x * lax.rsqrt(jnp.mean(x * x, axis=-1, keepdims=True) + EPS) * gain


MM_TILE = 1024
MM_DEPTH = 2048


def matmul(name, a, b, mode, after=None):
    if mode == "nn":
        (m, k), (k2, n) = a.shape, b.shape
    elif mode == "nt":
        (m, k), (n, k2) = a.shape, b.shape
    else:
        (k, m), (k2, n) = a.shape, b.shape
    assert k == k2, (name, a.shape, b.shape)
    tm, tn, tk = min(MM_TILE, m), min(MM_TILE, n), min(MM_DEPTH, k)
    assert m % tm == 0 and n % tn == 0 and k % tk == 0, (name, m, n, k)
    dot = {"nn": _d_nn, "nt": _d_nt, "tn": _d_tn}[mode]

    def body(a_ref, b_ref, *rest):
        o_ref = rest[-1]
        kk = pl.program_id(2)
        part = dot(a_ref[...], b_ref[...])

        @pl.when(kk == 0)
        def _():
            o_ref[...] = part

        @pl.when(kk != 0)
        def _():
            o_ref[...] += part

    if mode == "nn":
        a_spec = pl.BlockSpec((tm, tk), lambda j, i, kk: (i, kk))
        b_spec = pl.BlockSpec((tk, tn), lambda j, i, kk: (kk, j))
    elif mode == "nt":
        a_spec = pl.BlockSpec((tm, tk), lambda j, i, kk: (i, kk))
        b_spec = pl.BlockSpec((tn, tk), lambda j, i, kk: (j, kk))
    else:
        a_spec = pl.BlockSpec((tk, tm), lambda j, i, kk: (kk, i))
        b_spec = pl.BlockSpec((tk, tn), lambda j, i, kk: (kk, j))
    return pl.pallas_call(
        body,
        name=name,
        grid=(n // tn, m // tm, k // tk),
        in_specs=[a_spec, b_spec] + ([] if after is None else [pl.BlockSpec(memory_space=pl.ANY)]),
        out_specs=pl.BlockSpec((tm, tn), lambda j, i, kk: (i, j)),
        out_shape=jax.ShapeDtypeStruct((m, n), F32),
        compiler_params=_cparams("parallel", "parallel", "arbitrary"),
    )(*((a, b) if after is None else (a, b, after)))


def rowwise(name, fn, t_len, tile, row_in, full_in, row_out, acc_out=(), carries=(), reverse=False):
    tile = min(tile, t_len)
    n = t_len // tile
    assert t_len % tile == 0 and tile % SUBLANE == 0
    n_in, n_ro, n_acc, n_car = len(row_in) + len(full_in), len(row_out), len(acc_out), len(carries)

    def ti(i):
        return (n - 1 - i) if reverse else i

    in_specs, args = [], []
    for arr, kind in row_in:
        if kind[0] == "r":
            in_specs.append(pl.BlockSpec((tile, kind[1]), lambda i, c=kind[2]: (ti(i), c)))
        elif kind[0] == "h":
            in_specs.append(pl.BlockSpec((arr.shape[0], tile, arr.shape[2]), lambda i: (0, ti(i), 0)))
        else:
            in_specs.append(pl.BlockSpec(
                (SUBLANE, kind[1]), lambda i, c=kind[2]: (jnp.maximum(ti(i) * (tile // SUBLANE) - 1, 0), c)))
        args.append(arr)
    for arr in full_in:
        in_specs.append(pl.BlockSpec(arr.shape, lambda i, nd=arr.ndim: (0,) * nd))
        args.append(arr)
    out_specs, out_shape = [], []
    for kind in row_out:
        if kind[0] == "r":
            out_specs.append(pl.BlockSpec((tile, kind[1]), lambda i: (ti(i), 0)))
            out_shape.append(jax.ShapeDtypeStruct((t_len, kind[1]), kind[2]))
        else:
            out_specs.append(pl.BlockSpec((kind[1], tile, kind[2]), lambda i: (0, ti(i), 0)))
            out_shape.append(jax.ShapeDtypeStruct((kind[1], t_len, kind[2]), kind[3]))
    for shp in acc_out:
        out_specs.append(pl.BlockSpec(shp, lambda i, nd=len(shp): (0,) * nd))
        out_shape.append(jax.ShapeDtypeStruct(shp, F32))

    def body(*refs):
        in_refs = refs[:n_in]
        ro_refs = refs[n_in:n_in + n_ro]
        acc_refs = refs[n_in + n_ro:n_in + n_ro + n_acc]
        car_refs = refs[n_in + n_ro + n_acc:]
        step = pl.program_id(0)
        if n_car:
            @pl.when(step == 0)
            def _():
                for r in car_refs:
                    r[...] = jnp.zeros_like(r)
        vals = [r[...].astype(F32) for r in in_refs] + [r[...] for r in car_refs]
        outs = fn(ti(step), *vals)
        assert len(outs) == n_ro + n_acc + n_car, (name, len(outs))
        for r, o in zip(ro_refs, outs[:n_ro]):
            r[...] = o.astype(r.dtype)
        for r, o in zip(acc_refs, outs[n_ro:n_ro + n_acc]):
            @pl.when(step == 0)
            def _(r=r, o=o):
                r[...] = o

            @pl.when(step != 0)
            def _(r=r, o=o):
                r[...] += o
        for r, o in zip(car_refs, outs[n_ro + n_acc:]):
            r[...] = o

    res = pl.pallas_call(
        body,
        name=name,
        grid=(n,),
        in_specs=in_specs,
        out_specs=out_specs,
        out_shape=out_shape,
        scratch_shapes=[pltpu.VMEM(s, F32) for s in carries],
        compiler_params=_cparams("arbitrary"),
    )(*args)
    return list(res)


def _vjp_fn(fn, n_diff, n_out):
    def g(i, *a):
        ins, cts = a[:len(a) - n_out], a[len(a) - n_out:]
        diff, rest = ins[:n_diff], ins[n_diff:]
        _, pull = jax.vjp(lambda *d: tuple(fn(i, *d, *rest)), *diff)
        return tuple(pull(tuple(cts)))

    return g


def f_rms_x(i, x, gain):
    return (_rms(x, gain),)


def f_lat(i, cq, ckv, gq, gkv):
    return _rms(cq, gq), _rms(ckv, gkv)


def _rope_tables(pos, invf):
    ang = pos * invf
    lane = lax.broadcasted_iota(jnp.int32, (1, LANE), 1)
    cosv, sinv = jnp.cos(ang), jnp.sin(ang)
    half = ROPE // 2
    c = jnp.where(lane < ROPE, cosv, 0.0)
    sa = jnp.where(lane < half, -sinv, 0.0)
    sb = jnp.where((lane >= half) & (lane < ROPE), sinv, 0.0)
    return c, sa, sb


def _rope(xh, tabs):
    c, sa, sb = tabs
    half = ROPE // 2
    return xh * c + _roll(xh, LANE - half, 1) * sa + _roll(xh, half, 1) * sb


def f_head(i, q_raw, kv_raw, kr, qg, kg, pos, invf):
    tabs = _rope_tables(pos, invf)
    qs, ks, vs = [], [], []
    kr_ss = jnp.sum(kr * kr, axis=-1, keepdims=True)
    for h in range(HEADS):
        lo = q_raw[:, HEAD_PAD * h:HEAD_PAD * h + NOPE]
        hi = q_raw[:, HEAD_PAD * h + NOPE:HEAD_PAD * (h + 1)]
        ss = jnp.sum(lo * lo, axis=-1, keepdims=True) + jnp.sum(hi * hi, axis=-1, keepdims=True)
        r = lax.rsqrt(ss * (1.0 / QK) + EPS)
        qs.append(jnp.concatenate([lo * r * qg[:, :NOPE], _rope(hi * r * qg[:, NOPE:], tabs)], axis=1))
        lo = kv_raw[:, 2 * NOPE * h:2 * NOPE * h + NOPE]
        ss = jnp.sum(lo * lo, axis=-1, keepdims=True) + kr_ss
        r = lax.rsqrt(ss * (1.0 / QK) + EPS)
        ks.append(jnp.concatenate([lo * r * kg[:, :NOPE], _rope(kr * r * kg[:, NOPE:], tabs)], axis=1))
        vs.append(kv_raw[:, 2 * NOPE * h + NOPE:2 * NOPE * (h + 1)])
    return jnp.stack(qs), jnp.stack(ks), jnp.stack(vs)


def f_mix(i, o_mla, mgate, o_gdn, ggate, og):
    parts = [o_mla * jax.nn.silu(mgate)]
    for h in range(HEADS):
        parts.append(_rms(o_gdn[h], og) * jax.nn.silu(ggate[:, LANE * h:LANE * (h + 1)]))
    return (jnp.concatenate(parts, axis=1),)


def _row(a, j):
    rows = lax.broadcasted_iota(jnp.int32, a.shape, 0)
    return jnp.sum(jnp.where(rows == j, a, 0.0), axis=0, keepdims=True)


def _shift_rows(x, halo, d):
    xs = _roll(x, d, 0)
    hs = _roll(halo, d, 0)
    r8 = lax.broadcasted_iota(jnp.int32, hs.shape, 0)
    top = jnp.where(r8 < d, hs, xs[:SUBLANE])
    return jnp.concatenate([top, xs[SUBLANE:]], axis=0)


def _conv_silu(x, halo, w):
    y = _row(w, CONV_W - 1) * x
    for j in range(CONV_W - 1):
        y = y + _row(w, j) * _shift_rows(x, halo, CONV_W - 1 - j)
    return jax.nn.silu(y)


def _head_select(offset):
    r = lax.broadcasted_iota(jnp.int32, (LANE, WIDTH), 0)
    c = lax.broadcasted_iota(jnp.int32, (LANE, WIDTH), 1)
    return (r == offset + lax.shift_right_logical(c, 7)).astype(_BF)


def _split3(x):
    x1 = x.astype(_BF)
    r1 = x - x1.astype(F32)
    x2 = r1.astype(_BF)
    return x1, x2, (r1 - x2.astype(F32)).astype(_BF)


@jax.custom_vjp
def _spread(x, sel):
    return _d_nn(jnp.concatenate(_split3(x), axis=1), jnp.concatenate([sel, sel, sel], axis=0))


def _spread_fwd(x, sel):
    return _spread(x, sel), sel


def _spread_bwd(sel, g):
    g1, g2, g3 = _split3(g)
    return _d_nt(g1, sel) + _d_nt(g2, sel) + _d_nt(g3, sel), jnp.zeros_like(sel)


_spread.defvjp(_spread_fwd, _spread_bwd)


def f_gdn_pre(i, gq, gk, gv, gab, hq, hk, hv, cwq, cwk, cwv, alog, dtb):
    live = jnp.where(i == 0, 0.0, 1.0)
    q = _conv_silu(gq, hq * live, cwq)
    k = _conv_silu(gk, hk * live, cwk)
    v = _conv_silu(gv, hv * live, cwv)
    g = _spread(-jnp.exp(alog) * jax.nn.softplus(gab + dtb), _head_select(0))
    beta = _spread(jax.nn.sigmoid(gab), _head_select(HEADS))
    qs, ks, vs, gs, bs = [], [], [], [], []
    for h in range(HEADS):
        sl = slice(LANE * h, LANE * (h + 1))
        qh, kh = q[:, sl], k[:, sl]
        qs.append(qh * lax.rsqrt(jnp.sum(qh * qh, axis=-1, keepdims=True) + EPS) * (GDN_DIM ** -0.5))
        ks.append(kh * lax.rsqrt(jnp.sum(kh * kh, axis=-1, keepdims=True) + EPS))
        vs.append(v[:, sl])
        gs.append(g[:, sl])
        bs.append(beta[:, sl])
    return jnp.stack(qs), jnp.stack(ks), jnp.stack(vs), jnp.stack(gs), jnp.stack(bs)


def gdn_pre_bwd(i, gq, gk, gv, gab, hq, hk, hv, dq, dk, dv, dg, db, cwq, cwk, cwv, alog, dtb, cq, ck, cv):
    grads = _vjp_fn(f_gdn_pre, 12, 5)(i, gq, gk, gv, gab, hq, hk, hv, cwq, cwk, cwv, alog, dtb, dq, dk, dv, dg, db)
    dgq, dgk, dgv, dgab, dhq, dhk, dhv, dcwq, dcwk, dcwv, dalog, ddtb = grads

    def add_tail(dx, carry):
        return jnp.concatenate([dx[:-SUBLANE], dx[-SUBLANE:] + carry], axis=0)

    return (add_tail(dgq, cq), add_tail(dgk, ck), add_tail(dgv, cv), dgab,
            dcwq, dcwk, dcwv, dalog, ddtb, dhq, dhk, dhv)


def f_loss(i, x, h, tgt):
    e = x + h - tgt
    part = 0.5 * jnp.sum(e * e) * (1.0 / D_MODEL)
    dy = e * (1.0 / D_MODEL)
    return dy, dy, jnp.zeros((SUBLANE, LANE), F32) + part


def _flash_tile(t_len):
    return min(512, t_len)


FLASH_HEADS = 4
FLASH_BWD_HEADS = 2
LOG2E = 1.4426950408889634


def _causal(rows0, shape):
    r = rows0 + lax.broadcasted_iota(jnp.int32, shape, 0)
    c = lax.broadcasted_iota(jnp.int32, shape, 1)
    return c <= r


def flash_fwd(q, k, v):
    h_n, t_len, _ = q.shape
    tq = _flash_tile(t_len)
    nq = t_len // tq
    hb = FLASH_HEADS
    kw = 2 if nq % 2 == 0 else 1
    tk = kw * tq
    c2 = (QK ** -0.5) * LOG2E
    pairs = [(i, j) for i in range(nq) for j in range(i // kw + 1)]
    qt = jnp.array([p[0] for p in pairs], jnp.int32)
    kt = jnp.array([p[1] for p in pairs], jnp.int32)

    def body(qt_ref, kt_ref, q_ref, k_ref, v_ref, o_ref, lse_ref, m_s, acc_s):
        step = pl.program_id(1)
        qi, kj = qt_ref[step], kt_ref[step]
        last = qi // kw

        @pl.when(kj == 0)
        def _():
            m_s[...] = jnp.full_like(m_s, -jnp.inf)
            acc_s[...] = jnp.zeros_like(acc_s)

        def tile(diagonal):
            s = _bdot(q_ref[...], k_ref[...], _NT3, False) * c2
            if diagonal:
                s = jnp.where(_causal((qi % kw) * tq, (tq, tk))[None], s, -jnp.inf)
            m_old = m_s[...]
            m_new = jnp.maximum(m_old, jnp.max(s, axis=-1, keepdims=True))
            p = jnp.exp2(s - m_new).astype(_BF)
            v_ones = jnp.concatenate([v_ref[...], jnp.ones((hb, tk, LANE), _BF)], axis=2)
            acc_s[...] = jnp.exp2(m_old - m_new) * acc_s[...] + _bdot(p, v_ones, _NN3, False)
            m_s[...] = m_new

        @pl.when(kj < last)
        def _():
            tile(False)

        @pl.when(kj == last)
        def _():
            tile(True)
            acc = acc_s[...]
            l_sum = acc[:, :, LANE:]
            o = acc[:, :, :LANE] / l_sum
            for hh in range(hb):
                o_ref[:, LANE * hh:LANE * (hh + 1)] = o[hh]
            lse_ref[...] = m_s[...] + jnp.log2(jnp.max(l_sum, axis=-1, keepdims=True))

    return pl.pallas_call(
        body,
        name="flash_fwd",
        grid_spec=pltpu.PrefetchScalarGridSpec(
            num_scalar_prefetch=2,
            grid=(h_n // hb, qt.shape[0]),
            in_specs=[
                pl.BlockSpec((hb, tq, HEAD_PAD), lambda h, s, qt_ref, kt_ref: (h, qt_ref[s], 0)),
                pl.BlockSpec((hb, tk, HEAD_PAD), lambda h, s, qt_ref, kt_ref: (h, kt_ref[s], 0)),
                pl.BlockSpec((hb, tk, LANE), lambda h, s, qt_ref, kt_ref: (h, kt_ref[s], 0)),
            ],
            out_specs=[
                pl.BlockSpec((tq, hb * LANE), lambda h, s, qt_ref, kt_ref: (qt_ref[s], h)),
                pl.BlockSpec((hb, tq, 1), lambda h, s, qt_ref, kt_ref: (h, qt_ref[s], 0)),
            ],
            scratch_shapes=[pltpu.VMEM((hb, tq, 1), F32), pltpu.VMEM((hb, tq, 2 * LANE), F32)],
        ),
        out_shape=[jax.ShapeDtypeStruct((t_len, h_n * LANE), F32), jax.ShapeDtypeStruct((h_n, t_len, 1), F32)],
        compiler_params=_cparams("parallel", "arbitrary"),
    )(qt, kt, q, k, v)


def flash_bwd(q, k, v, do, lse, delta):
    h_n, t_len, _ = q.shape
    tq = _flash_tile(t_len)
    nq = t_len // tq
    hb = FLASH_BWD_HEADS
    kw = 2 if nq % 2 == 0 else 1
    tk = kw * tq
    pairs = [(i, j) for j in range(nq // kw) for i in range(kw * j, nq)]
    n_steps = len(pairs)
    qt = jnp.array([p[0] for p in pairs], jnp.int32)
    kt = jnp.array([p[1] for p in pairs], jnp.int32)
    scale = QK ** -0.5
    c2 = scale * LOG2E

    def body(qt_ref, kt_ref, q_ref, k_ref, v_ref, do_ref, lse_ref, dl_ref, dq_hbm, dk_ref, dv_ref, dq_s, dq_sem):
        group, step = pl.program_id(0), pl.program_id(1)
        qi, kj = qt_ref[step], kt_ref[step]

        @pl.when(step == 0)
        def _():
            dq_s[...] = jnp.zeros_like(dq_s)

        def tile(diagonal):
            qb, kb = q_ref[...], k_ref[...]
            dob = jnp.stack([do_ref[:, LANE * hh:LANE * (hh + 1)] for hh in range(hb)])
            p = jnp.exp2(_bdot(qb, kb, _NT3, False) * c2 - lse_ref[...])
            if diagonal:
                p = jnp.where(_causal((qi % kw) * tq, (tq, tk))[None], p, 0.0)
            dv = _bdot(p, dob, _TN3, False)
            ds = p * (_bdot(dob, v_ref[...], _NT3, False) - dl_ref[...]) * scale
            dk = _bdot(ds, qb, _TN3, False)
            dq_s[:, pl.ds(pl.multiple_of(qi * tq, tq), tq), :] += _bdot(ds, kb, _NN3, False)
            return dk, dv

        @pl.when(qi == kw * kj)
        def _():
            dk_ref[...], dv_ref[...] = tile(True)

        @pl.when((qi != kw * kj) & (qi // kw == kj))
        def _():
            dk, dv = tile(True)
            dk_ref[...] += dk
            dv_ref[...] += dv

        @pl.when(qi // kw > kj)
        def _():
            dk, dv = tile(False)
            dk_ref[...] += dk
            dv_ref[...] += dv

        @pl.when(step == n_steps - 1)
        def _():
            out = pltpu.make_async_copy(dq_s, dq_hbm.at[pl.ds(group * hb, hb)], dq_sem)
            out.start()
            out.wait()

    def qmap(h, s, qt_ref, kt_ref):
        return (h, qt_ref[s], 0)

    def kmap(h, s, qt_ref, kt_ref):
        return (h, kt_ref[s], 0)

    return pl.pallas_call(
        body,
        name="flash_bwd",
        grid_spec=pltpu.PrefetchScalarGridSpec(
            num_scalar_prefetch=2,
            grid=(h_n // hb, n_steps),
            in_specs=[
                pl.BlockSpec((hb, tq, HEAD_PAD), qmap),
                pl.BlockSpec((hb, tk, HEAD_PAD), kmap),
                pl.BlockSpec((hb, tk, LANE), kmap),
                pl.BlockSpec((tq, hb * LANE), lambda h, s, qt_ref, kt_ref: (qt_ref[s], h)),
                pl.BlockSpec((hb, tq, 1), qmap),
                pl.BlockSpec((hb, tq, 1), qmap),
            ],
            out_specs=[
                pl.BlockSpec(memory_space=pl.ANY),
                pl.BlockSpec((hb, tk, HEAD_PAD), kmap),
                pl.BlockSpec((hb, tk, LANE), kmap),
            ],
            scratch_shapes=[pltpu.VMEM((hb, t_len, HEAD_PAD), F32), pltpu.SemaphoreType.DMA],
        ),
        out_shape=[
            jax.ShapeDtypeStruct((h_n, t_len, HEAD_PAD), F32),
            jax.ShapeDtypeStruct((h_n, t_len, HEAD_PAD), F32),
            jax.ShapeDtypeStruct((h_n, t_len, LANE), F32),
        ],
        compiler_params=_cparams("parallel", "arbitrary"),
    )(qt, kt, q, k, v, do, lse, delta)


def _tri_ones(h_n):
    ii = lax.broadcasted_iota(jnp.int32, (h_n, CHUNK, CHUNK), 1)
    jj = lax.broadcasted_iota(jnp.int32, (h_n, CHUNK, CHUNK), 2)
    return (ii >= jj).astype(_BF)


@jax.custom_vjp
def _chunk_cumsum(gb):
    tri = _tri_ones(gb.shape[0])
    return _bdot(jnp.concatenate([tri, tri, tri], axis=2), jnp.concatenate(_split3(gb), axis=1), _NN3, False)


def _chunk_cumsum_bwd(_, ct):
    tri = _tri_ones(ct.shape[0])
    return (_bdot(jnp.concatenate([tri, tri, tri], axis=1), jnp.concatenate(_split3(ct), axis=1), _TN3, False),)


_chunk_cumsum.defvjp(lambda gb: (_chunk_cumsum(gb), None), _chunk_cumsum_bwd)


@jax.custom_vjp
def _pair_diff(gcb):
    g1, g2, g3 = _split3(gcb)
    lane = lax.broadcasted_iota(jnp.int32, (1, 1, LANE), 2)
    one, zero = jnp.ones((), _BF), jnp.zeros((), _BF)
    a = jnp.where(lane == 0, g1, jnp.where(lane == 1, g2, jnp.where(lane == 2, g3, jnp.where(lane < 6, one, zero))))
    b = jnp.where(lane < 3, one, jnp.where(lane == 3, -g1, jnp.where(lane == 4, -g2, jnp.where(lane == 5, -g3, zero))))
    return _bdot(a, b, _NT3, False)


def _pair_diff_bwd(_, ct):
    parts = _split3(ct)
    ones = jnp.ones((ct.shape[0], 3 * CHUNK, LANE), _BF)
    rows = _bdot(jnp.concatenate(parts, axis=2), ones, _NN3, False)
    cols = _bdot(jnp.concatenate(parts, axis=1), ones, _TN3, False)
    lane = lax.broadcasted_iota(jnp.int32, (1, 1, LANE), 2)
    return (jnp.where(lane == 0, rows - cols, 0.0),)


_pair_diff.defvjp(lambda gcb: (_pair_diff(gcb), None), _pair_diff_bwd)


def gdn_step(s, q, k, v, gb, bb, mm3=_bh3):
    c = CHUNK
    ii = lax.broadcasted_iota(jnp.int32, (1, c, c), 1)
    jj = lax.broadcasted_iota(jnp.int32, (1, c, c), 2)
    incl, strict = ii >= jj, ii > jj
    gcb = _chunk_cumsum(gb)
    diff = _pair_diff(gcb)
    decay = jnp.where(incl, jnp.exp(jnp.where(incl, diff, 0.0)), 0.0)
    kb, vb = k * bb, v * bb
    egc = jnp.exp(gcb)
    lmat = jnp.where(strict, _bmm_nt(kb, k) * decay, 0.0)
    inv = (ii == jj).astype(F32) - lmat
    pw = mm3(lmat, lmat)
    for step in range(5):
        inv = inv + mm3(inv, pw)
        if step < 4:
            pw = mm3(pw, pw)
    u = mm3(inv, vb)
    w = mm3(inv, kb * egc)
    attn = _bmm_nt(q, k) * decay
    qd = q * egc
    g_end = jnp.sum(gb, axis=1, keepdims=True)
    kd = k * jnp.exp(g_end - gcb)
    v_new = u - _bmm(w, s)
    o = _bmm(qd, s) + _bmm(attn, v_new)
    s_new = s * jnp.exp(g_end) + _bmm_tn(kd, v_new)
    return s_new, o


def gdn_fwd(q, k, v, gb, bb):
    h_n, t_len, d = q.shape
    n = t_len // CHUNK
    blk = pl.BlockSpec((h_n, CHUNK, d), lambda i: (0, i, 0))

    def body(q_ref, k_ref, v_ref, g_ref, b_ref, o_ref, sall_ref, s_s):
        @pl.when(pl.program_id(0) == 0)
        def _():
            s_s[...] = jnp.zeros_like(s_s)

        s = s_s[...]
        sall_ref[0] = s
        s_new, o = gdn_step(s, q_ref[...], k_ref[...], v_ref[...], g_ref[...], b_ref[...], mm3=_bh3_passes)
        o_ref[...] = o
        s_s[...] = s_new

    return pl.pallas_call(
        body,
        name="gdn_fwd",
        grid=(n,),
        in_specs=[blk] * 5,
        out_specs=[blk, pl.BlockSpec((1, h_n, d, d), lambda i: (i, 0, 0, 0))],
        out_shape=[jax.ShapeDtypeStruct((h_n, t_len, d), F32), jax.ShapeDtypeStruct((n, h_n, d, d), F32)],
        scratch_shapes=[pltpu.VMEM((h_n, d, d), F32)],
        compiler_params=_cparams("arbitrary"),
    )(q, k, v, gb, bb)


def gdn_bwd(q, k, v, gb, bb, s_all, do):
    h_n, t_len, d = q.shape
    n = t_len // CHUNK
    blk = pl.BlockSpec((h_n, CHUNK, d), lambda i: (0, n - 1 - i, 0))

    def body(q_ref, k_ref, v_ref, g_ref, b_ref, sall_ref, do_ref, dq_ref, dk_ref, dv_ref, dg_ref, db_ref, ds_s):
        @pl.when(pl.program_id(0) == 0)
        def _():
            ds_s[...] = jnp.zeros_like(ds_s)

        _, pull = jax.vjp(gdn_step, sall_ref[0], q_ref[...], k_ref[...], v_ref[...], g_ref[...], b_ref[...])
        ds, dq, dk, dv, dg, db = pull((ds_s[...], do_ref[...]))
        ds_s[...] = ds
        dq_ref[...], dk_ref[...], dv_ref[...], dg_ref[...], db_ref[...] = dq, dk, dv, dg, db

    return pl.pallas_call(
        body,
        name="gdn_bwd",
        grid=(n,),
        in_specs=[blk] * 5 + [pl.BlockSpec((1, h_n, d, d), lambda i: (n - 1 - i, 0, 0, 0)), blk],
        out_specs=[blk] * 5,
        out_shape=[jax.ShapeDtypeStruct((h_n, t_len, d), F32)] * 5,
        scratch_shapes=[pltpu.VMEM((h_n, d, d), F32)],
        compiler_params=_cparams("arbitrary"),
    )(q, k, v, gb, bb, s_all, do)


def _pad_cols(a, n):
    return jnp.pad(a, ((0, 0), (0, n - a.shape[1])))


def arrange_w_in(w):
    pieces, start = [], 0
    for n in R_SPLITS:
        pieces.append(w[:, start:start + n])
        start += n
    cq, ckv, kr, mgate, gq, gk, gv, ga, gb, ggate = pieces
    return jnp.concatenate([mgate, gq, gk, gv, ggate, cq, ckv, _pad_cols(kr, LANE),
                            _pad_cols(jnp.concatenate([ga, gb], axis=1), LANE)], axis=1)


def unarrange_w_in(g):
    def cols(start, n):
        return g[:, start:start + n]
    return jnp.concatenate([cols(P_CQ, Q_LORA), cols(P_CKV, KV_LORA), cols(P_KR, ROPE), cols(P_MGATE, WIDTH),
                            cols(P_GQ, WIDTH), cols(P_GK, WIDTH), cols(P_GV, WIDTH), cols(P_GAB, HEADS),
                            cols(P_GAB + HEADS, HEADS), cols(P_GGATE, WIDTH)], axis=1)


def arrange_w_uq(w):
    w = w.reshape(w.shape[0], HEADS, QK)
    return jnp.pad(w, ((0, 0), (0, 0), (0, HEAD_PAD - QK))).reshape(w.shape[0], HEADS * HEAD_PAD)


def unarrange_w_uq(g):
    return g.reshape(g.shape[0], HEADS, HEAD_PAD)[:, :, :QK].reshape(g.shape[0], HEADS * QK)


def local_step(x, pos, tgt, p, on_weight_grads=None, on_d_xn=None):
    t_len = x.shape[0]
    w_in, w_uq, w_ukv, w_out = p["w_in"], p["w_uq"], p["w_ukv"], p["w_out"]
    norm_gain = p["norm_gain"].reshape(1, D_MODEL)
    qa_gain = p["mla_q_a_gain"].reshape(1, Q_LORA)
    kva_gain = p["mla_kv_a_gain"].reshape(1, KV_LORA)
    qg = _pad_cols(p["mla_q_norm_gain"].reshape(1, QK), HEAD_PAD)
    kg = _pad_cols(p["mla_k_norm_gain"].reshape(1, QK), HEAD_PAD)
    cw = p["gdn_conv_w"].reshape(CONV_W, 3 * WIDTH)
    cwq, cwk, cwv = cw[:, :WIDTH], cw[:, WIDTH:2 * WIDTH], cw[:, 2 * WIDTH:]
    alog = _pad_cols(p["gdn_a_log"].reshape(1, HEADS), LANE)
    dtb = _pad_cols(p["gdn_dt_bias"].reshape(1, HEADS), LANE)
    og = p["gdn_out_norm_gain"].reshape(1, GDN_DIM)
    half = ROPE // 2
    inv_freq = jnp.power(ROPE_THETA, -jnp.arange(half, dtype=F32) / half)
    invf = _pad_cols(jnp.concatenate([inv_freq, inv_freq]).reshape(1, ROPE), LANE)

    rt = 256
    r = "r"
    (xn,) = rowwise("rms_x", f_rms_x, t_len, rt, [(x, (r, D_MODEL, 0))], [norm_gain], [(r, D_MODEL, _BF)])
    proj = matmul("proj", xn, w_in, "nn")
    cq_in = (proj, (r, Q_LORA, P_CQ // Q_LORA))
    ckv_in = (proj, (r, KV_LORA, P_CKV // KV_LORA))
    kr_in = (proj, (r, LANE, P_KR // LANE))
    mgate_in = (proj, (r, WIDTH, P_MGATE // WIDTH))
    ggate_in = (proj, (r, WIDTH, P_GGATE // WIDTH))
    gqkv_in = [(proj, (r, WIDTH, P_GQ // WIDTH)), (proj, (r, WIDTH, P_GK // WIDTH)), (proj, (r, WIDTH, P_GV // WIDTH))]
    gab_in = (proj, (r, LANE, P_GAB // LANE))
    halos = [(proj, ("halo", WIDTH, P_GQ // WIDTH)), (proj, ("halo", WIDTH, P_GK // WIDTH)),
             (proj, ("halo", WIDTH, P_GV // WIDTH))]

    q_lat, kv_lat = rowwise("lat", f_lat, t_len, rt, [cq_in, ckv_in], [qa_gain, kva_gain],
                            [(r, Q_LORA, _BF), (r, KV_LORA, _BF)])
    q_raw = matmul("q_up", q_lat, w_uq, "nn")
    kv_raw = matmul("kv_up", kv_lat, w_ukv, "nn")
    wide = HEADS * HEAD_PAD
    head_in = [(q_raw, (r, wide, 0)), (kv_raw, (r, wide, 0)), kr_in]
    pos_in = (pos, (r, 1, 0))
    q_full, k_full, v_mla = rowwise(
        "head", lambda i, qr, kvr, kr, ps, qg_, kg_, iv: f_head(i, qr, kvr, kr, qg_, kg_, ps, iv), t_len, rt,
        head_in + [pos_in], [qg, kg, invf],
        [("h", HEADS, HEAD_PAD, _BF), ("h", HEADS, HEAD_PAD, _BF), ("h", HEADS, LANE, _BF)])
    o_mla, lse = flash_fwd(q_full, k_full, v_mla)

    pre_in = gqkv_in + [gab_in] + halos
    pre_full = [cwq, cwk, cwv, alog, dtb]
    hkind = ("h", HEADS, GDN_DIM, F32)
    gq_n, gk_n, gv_n, g_b, b_b = rowwise("gdn_pre", f_gdn_pre, t_len, rt, pre_in, pre_full, [hkind] * 5)
    o_gdn, s_all = gdn_fwd(gq_n, gk_n, gv_n, g_b, b_b)

    mix_in = [(o_mla, (r, WIDTH, 0)), mgate_in, (o_gdn, ("h",)), ggate_in]
    (mixed,) = rowwise("mix", f_mix, t_len, rt, mix_in, [og], [(r, 2 * WIDTH, _BF)])
    h_out = matmul("out_proj", mixed, w_out, "nn")
    dy, dy_mx, loss_acc = rowwise("loss", f_loss, t_len, rt,
                                  [(x, (r, D_MODEL, 0)), (h_out, (r, D_MODEL, 0)), (tgt, (r, D_MODEL, 0))], [],
                                  [(r, D_MODEL, F32), (r, D_MODEL, _BF)], [(SUBLANE, LANE)])
    loss = loss_acc[0, 0]

    d_mixed = matmul("d_mixed", dy_mx, w_out, "nt")
    g_w_out = matmul("g_w_out", mixed, dy_mx, "tn")

    def mix_bwd(i, o_mla_, mgate_, o_gdn_, ggate_, d_mixed_, og_):
        do_mla_, d_mgate_, do_gdn_, d_ggate_, g_og_ = _vjp_fn(f_mix, 5, 1)(i, o_mla_, mgate_, o_gdn_, ggate_, og_, d_mixed_)
        delta_ = jnp.stack([jnp.sum(o_mla_[:, LANE * h:LANE * (h + 1)] * do_mla_[:, LANE * h:LANE * (h + 1)],
                                    axis=-1, keepdims=True) for h in range(HEADS)])
        return do_mla_, d_mgate_, do_gdn_, d_ggate_, delta_, g_og_

    do_mla, d_mgate, do_gdn, d_ggate, delta, g_og = rowwise(
        "mix_bwd", mix_bwd, t_len, rt, mix_in + [(d_mixed, (r, 2 * WIDTH, 0))], [og],
        [(r, WIDTH, F32), (r, WIDTH, _BF), hkind, (r, WIDTH, _BF), ("h", HEADS, 1, F32)], [(1, GDN_DIM)])
    dq_n, dk_n, dv_n, dg_b, db_b = gdn_bwd(gq_n, gk_n, gv_n, g_b, b_b, s_all, do_gdn)
    cts_in = [(a, ("h",)) for a in (dq_n, dk_n, dv_n, dg_b, db_b)]
    d_gq, d_gk, d_gv, d_gab, g_cwq, g_cwk, g_cwv, g_alog, g_dtb = rowwise(
        "gdn_pre_bwd", gdn_pre_bwd, t_len, rt, pre_in + cts_in, pre_full,
        [(r, WIDTH, _BF)] * 3 + [(r, LANE, _BF)],
        [(CONV_W, WIDTH)] * 3 + [(1, LANE)] * 2, carries=[(SUBLANE, WIDTH)] * 3, reverse=True)

    dq_full, dk_full, dv_mla = flash_bwd(q_full, k_full, v_mla, do_mla, lse, delta)
    head_cts = [(a, ("h",)) for a in (dq_full, dk_full, dv_mla)]

    def head_bwd(i, q_raw_, kv_raw_, kr_, pos_, dq_, dk_, dv_, qg_, kg_, invf_):
        return _vjp_fn(f_head, 5, 3)(i, q_raw_, kv_raw_, kr_, qg_, kg_, pos_, invf_, dq_, dk_, dv_)

    dq_raw, dkv_raw, d_kr, g_qg, g_kg = rowwise(
        "head_bwd", head_bwd, t_len, rt, head_in + [pos_in] + head_cts, [qg, kg, invf],
        [(r, wide, _BF), (r, wide, _BF), (r, LANE, _BF)], [(1, HEAD_PAD), (1, HEAD_PAD)])
    dq_lat = matmul("dq_lat", dq_raw, w_uq, "nt")
    g_w_uq = matmul("g_w_uq", q_lat, dq_raw, "tn")
    dkv_lat = matmul("dkv_lat", dkv_raw, w_ukv, "nt")
    g_w_ukv = matmul("g_w_ukv", kv_lat, dkv_raw, "tn")

    def lat_bwd(i, cq_, ckv_, dql_, dkl_, gq_, gkv_):
        return _vjp_fn(f_lat, 4, 2)(i, cq_, ckv_, gq_, gkv_, dql_, dkl_)

    d_cq, d_ckv, g_qa, g_kva = rowwise(
        "lat_bwd", lat_bwd, t_len, rt, [cq_in, ckv_in, (dq_lat, (r, Q_LORA, 0)), (dkv_lat, (r, KV_LORA, 0))],
        [qa_gain, kva_gain], [(r, Q_LORA, _BF), (r, KV_LORA, _BF)], [(1, Q_LORA), (1, KV_LORA)])

    d_proj = jnp.concatenate([d_mgate, d_gq, d_gk, d_gv, d_ggate, d_cq, d_ckv, d_kr, d_gab], axis=1)
    g_w_in = matmul("g_w_in", xn, d_proj, "tn")
    grads = {
        "w_in": g_w_in, "mla_q_a_gain": g_qa, "mla_kv_a_gain": g_kva, "w_uq": g_w_uq,
        "w_ukv": g_w_ukv, "mla_q_norm_gain": g_qg[:, :QK], "mla_k_norm_gain": g_kg[:, :QK],
        "gdn_conv_w": jnp.concatenate([g_cwq, g_cwk, g_cwv], axis=1), "gdn_a_log": g_alog[:, :HEADS],
        "gdn_dt_bias": g_dtb[:, :HEADS], "gdn_out_norm_gain": g_og, "w_out": g_w_out,
    }
    after = None if on_weight_grads is None else on_weight_grads(grads)
    d_xn = matmul("d_xn", d_proj, w_in, "nt", after=after)
    after = jnp.zeros((SUBLANE, LANE), F32) if on_d_xn is None else on_d_xn(d_xn)

    def rms_x_bwd(i, x_, dxn_, dy_, gain_, after_):
        dx, dgain = _vjp_fn(f_rms_x, 2, 1)(i, x_, gain_, dxn_)
        return dx + dy_, dgain

    grad_x, grads["norm_gain"] = rowwise(
        "rms_x_bwd", rms_x_bwd, t_len, rt, [(x, (r, D_MODEL, 0)), (d_xn, (r, D_MODEL, 0)), (dy, (r, D_MODEL, 0))],
        [norm_gain, after], [(r, D_MODEL, F32)], [(1, D_MODEL)])
    return loss, grad_x, grads


MESH = pl.DeviceIdType.MESH
ANY = pl.BlockSpec(memory_space=pl.ANY)
CHIP_FLIPS = ((1, 0), (0, 1), (1, 1))


def _place():
    return lax.axis_index("x"), lax.axis_index("y"), lax.axis_index("c")


def _flip(v, f):
    return 1 - v if f else v


def all_gather(shards):
    n_arr = len(shards)

    def body(*refs):
        x_refs, o_refs = refs[:n_arr], refs[n_arr:2 * n_arr]
        send_sems, recv_sems, local_sems = refs[2 * n_arr:]
        x, y, c = _place()
        me, sibling = (x, y, c), (x, y, 1 - c)
        chips = [(_flip(x, fx), _flip(y, fy)) for fx, fy in CHIP_FLIPS]

        def copy(a, k, block, to, src=None):
            px, py, pc = block
            dst = o_refs[a].at[4 * px + 2 * py + pc]
            return pltpu.make_async_remote_copy(
                src_ref=dst if src is None else src, dst_ref=dst, send_sem=send_sems.at[a, k],
                recv_sem=recv_sems.at[a, k], device_id=to, device_id_type=MESH)

        mine, first, passed = [], [], []
        for a in range(n_arr):
            cp = pltpu.make_async_copy(x_refs[a], o_refs[a].at[4 * x + 2 * y + c], local_sems.at[a])
            cp.start()
            mine.append(cp)
            first.append(copy(a, 0, me, sibling, src=x_refs[a]))
            first += [copy(a, 1 + j, me, (*chip, c), src=x_refs[a]) for j, chip in enumerate(chips)]
        for cp in first:
            cp.start()
        for j, chip in enumerate(chips):
            for a in range(n_arr):
                copy(a, 1 + j, (*chip, c), me).wait_recv()
                cp = copy(a, 4 + j, (*chip, c), sibling)
                cp.start()
                passed.append(cp)
        for a in range(n_arr):
            copy(a, 0, sibling, me).wait_recv()
            for j, chip in enumerate(chips):
                copy(a, 4 + j, (*chip, 1 - c), me).wait_recv()
        for cp in first + passed:
            cp.wait_send()
        for cp in mine:
            cp.wait()

    return pl.pallas_call(
        body,
        name="all_gather",
        out_shape=[jax.ShapeDtypeStruct((N_DEV,) + s.shape, s.dtype) for s in shards],
        in_specs=[ANY] * n_arr,
        out_specs=[ANY] * n_arr,
        scratch_shapes=[pltpu.SemaphoreType.DMA((n_arr, 7)), pltpu.SemaphoreType.DMA((n_arr, 7)),
                        pltpu.SemaphoreType.DMA((n_arr,))],
    )(*shards)


HBM = pl.BlockSpec(memory_space=pltpu.HBM)
SEMS = pl.BlockSpec(memory_space=pltpu.SEMAPHORE)
SIDE_EFFECT = pltpu.SideEffectType.DATAFLOW_SIDE_EFFECTING


def core_routes(x, y, c):
    return [(2 * q + (1 - c), q, (x, y, 1 - c)) for q in range(4)]


def chip_routes(x, y, c):
    routes = []
    for j, (fx, fy) in enumerate(CHIP_FLIPS):
        px, py = _flip(x, fx), _flip(y, fy)
        routes.append((2 * px + py, j, (px, py, c)))
    return routes


def _route_copies(routes, n_routes, src_refs, land_refs, sems):
    x, y, c = _place()
    n_copies = len(src_refs) * n_routes
    return [pltpu.make_async_remote_copy(src_ref=src.at[s], dst_ref=land.at[d], send_sem=sems[a * n_routes + k],
                                         recv_sem=sems[n_copies + a * n_routes + k], device_id=dev,
                                         device_id_type=MESH)
            for a, (src, land) in enumerate(zip(src_refs, land_refs)) for k, (s, d, dev) in enumerate(routes(x, y, c))]


def exchange_start(name, routes, n_routes, srcs):
    n = len(srcs)
    n_sems = 2 * n * n_routes
    lands = [lax.empty((n_routes,) + s.shape[1:], s.dtype) for s in srcs]

    def body(*refs):
        for cp in _route_copies(routes, n_routes, refs[:n], refs[n:2 * n], refs[2 * n:2 * n + n_sems]):
            cp.start()
        refs[-1][...] = jnp.zeros_like(refs[-1])

    res = pl.pallas_call(
        body,
        name=name,
        out_shape=(*[pltpu.SemaphoreType.DMA(())] * n_sems, *[pltpu.HBM(a.shape, a.dtype) for a in srcs + lands],
                   jax.ShapeDtypeStruct((SUBLANE, LANE), F32)),
        in_specs=[HBM] * (2 * n),
        out_specs=(*[SEMS] * n_sems, *[HBM] * (2 * n), pl.BlockSpec(memory_space=pltpu.VMEM)),
        input_output_aliases={i: n_sems + i for i in range(2 * n)},
        compiler_params=pltpu.CompilerParams(has_side_effects=SIDE_EFFECT),
    )(*[pltpu.with_memory_space_constraint(a, pltpu.HBM) for a in srcs + lands])
    return (res[:n_sems], res[n_sems:-1]), res[-1]


def exchange_wait(name, routes, handle, after):
    sems, thru = handle
    n, n_sems = len(thru) // 2, len(sems)
    n_routes = n_sems // (2 * n)

    def body(*refs):
        for cp in _route_copies(routes, n_routes, refs[:n], refs[n:2 * n], refs[2 * n:2 * n + n_sems]):
            cp.wait_send()
            cp.wait_recv()

    res = pl.pallas_call(
        body,
        name=name,
        out_shape=tuple(pltpu.HBM(a.shape, a.dtype) for a in thru),
        in_specs=[HBM] * (2 * n) + [SEMS] * n_sems + [ANY],
        out_specs=tuple([HBM] * (2 * n)),
        input_output_aliases={i: i for i in range(2 * n)},
        compiler_params=pltpu.CompilerParams(has_side_effects=SIDE_EFFECT),
    )(*thru, *sems, after)
    return list(res[:n]), list(res[n:])


def gather_small(v):
    def body(v_ref, o_ref, send_sems, recv_sems, local_sem):
        x, y, c = _place()
        me = 4 * x + 2 * y + c
        mine = pltpu.make_async_copy(v_ref, o_ref.at[me], local_sem)
        mine.start()
        copies = []
        for k in range(1, N_DEV):
            fx, fy, fc = (k >> 2) & 1, (k >> 1) & 1, k & 1
            cp = pltpu.make_async_remote_copy(
                src_ref=v_ref, dst_ref=o_ref.at[me], send_sem=send_sems.at[k - 1], recv_sem=recv_sems.at[k - 1],
                device_id=(_flip(x, fx), _flip(y, fy), _flip(c, fc)), device_id_type=MESH)
            cp.start()
            copies.append(cp)
        for cp in copies:
            cp.wait()
        mine.wait()

    return pl.pallas_call(
        body,
        name="gather_small",
        out_shape=jax.ShapeDtypeStruct((N_DEV,) + v.shape, v.dtype),
        in_specs=[ANY],
        out_specs=ANY,
        scratch_shapes=[pltpu.SemaphoreType.DMA((N_DEV - 1,)), pltpu.SemaphoreType.DMA((N_DEV - 1,)),
                        pltpu.SemaphoreType.DMA],
    )(v)


def _row_tile(rows):
    for t in (256, 128, 64, 32, 16, 8):
        if rows % t == 0:
            return t
    return rows


def add_core_parts(name, g, recv, c_idx, wire):
    _, rows, cols = g.shape
    tr = _row_tile(rows)

    def body(c_ref, g_ref, r_ref, o_ref, w_ref):
        part = g_ref[...] + r_ref[...]
        o_ref[...] = part
        w_ref[...] = part.astype(w_ref.dtype)

    blk = pl.BlockSpec((1, tr, cols), lambda q, i, c_ref: (q, i, 0))
    return pl.pallas_call(
        body,
        name=name,
        grid_spec=pltpu.PrefetchScalarGridSpec(
            num_scalar_prefetch=1,
            grid=(4, rows // tr),
            in_specs=[pl.BlockSpec((1, tr, cols), lambda q, i, c_ref: (2 * q + c_ref[0], i, 0)), blk],
            out_specs=[blk, blk],
        ),
        out_shape=[jax.ShapeDtypeStruct((4, rows, cols), F32), jax.ShapeDtypeStruct((4, rows, cols), wire)],
        compiler_params=_cparams("parallel", "parallel"),
    )(c_idx, g, recv)


def _adamw(w, g, m, v):
    m = ADAM_B1 * m + (1.0 - ADAM_B1) * g
    v = ADAM_B2 * v + (1.0 - ADAM_B2) * (g * g)
    m_hat = m / (1.0 - ADAM_B1 ** ADAM_STEP)
    v_hat = v / (1.0 - ADAM_B2 ** ADAM_STEP)
    delta = -ADAM_LR * (m_hat / (jnp.sqrt(v_hat) + ADAM_EPS) + ADAM_WD * w)
    return delta, m, v


def adamw_sharded(name, parts, recv, q_idx, w, m, v):
    rows, cols = w.shape
    tr = _row_tile(rows)

    def body(q_ref, p_ref, r_ref, w_ref, m_ref, v_ref, g_out, d_out, m_out, v_out):
        g = p_ref[0] + r_ref[0].astype(F32) + r_ref[1].astype(F32) + r_ref[2].astype(F32)
        d, m_new, v_new = _adamw(w_ref[...], g, m_ref[...], v_ref[...])
        g_out[...], d_out[...], m_out[...], v_out[...] = g, d, m_new, v_new

    blk = pl.BlockSpec((tr, cols), lambda i, q_ref: (i, 0))
    return pl.pallas_call(
        body,
        name=name,
        grid_spec=pltpu.PrefetchScalarGridSpec(
            num_scalar_prefetch=1,
            grid=(rows // tr,),
            in_specs=[pl.BlockSpec((1, tr, cols), lambda i, q_ref: (q_ref[0], i, 0)),
                      pl.BlockSpec((3, tr, cols), lambda i, q_ref: (0, i, 0)), blk, blk, blk],
            out_specs=[blk] * 4,
        ),
        out_shape=[jax.ShapeDtypeStruct((rows, cols), F32)] * 4,
        compiler_params=_cparams("parallel"),
    )(q_idx, parts, recv, w, m, v)


def adamw_small(gathered, w, m, v):
    def body(g_ref, w_ref, m_ref, v_ref, g_out, d_out, m_out, v_out):
        g = g_ref[0]
        for j in range(1, N_DEV):
            g = g + g_ref[j]
        d, m_new, v_new = _adamw(w_ref[...], g, m_ref[...], v_ref[...])
        g_out[...], d_out[...], m_out[...], v_out[...] = g, d, m_new, v_new

    return pl.pallas_call(body, name="adamw_small", out_shape=[jax.ShapeDtypeStruct(w.shape, F32)] * 4)(gathered, w, m, v)


SHARDED = ("w_in", "w_uq", "w_ukv", "gdn_conv_w", "w_out")
SMALL = (("norm_gain", D_MODEL), ("mla_q_a_gain", Q_LORA), ("mla_kv_a_gain", KV_LORA), ("mla_q_norm_gain", QK),
         ("mla_k_norm_gain", QK), ("gdn_a_log", HEADS), ("gdn_dt_bias", HEADS), ("gdn_out_norm_gain", GDN_DIM))
WEIGHT_ORDER = ("norm_gain", "w_in", "mla_q_a_gain", "mla_kv_a_gain", "w_uq", "w_ukv", "mla_q_norm_gain",
                "mla_k_norm_gain", "gdn_conv_w", "gdn_a_log", "gdn_dt_bias", "gdn_out_norm_gain", "w_out")


def _pack_small(d):
    rows = []
    for name, n in SMALL:
        a = d[name].reshape(-1).astype(F32)
        n_pad = -(-n // LANE) * LANE
        rows.append(jnp.pad(a, (0, n_pad - n)).reshape(n_pad // LANE, LANE))
    packed = jnp.concatenate(rows, axis=0)
    return jnp.pad(packed, ((0, -packed.shape[0] % SUBLANE), (0, 0)))


def _unpack_small(packed):
    out, row = {}, 0
    for name, n in SMALL:
        n_rows = -(-n // LANE)
        out[name] = packed[row:row + n_rows].reshape(-1)[:n].reshape(1, n)
        row += n_rows
    return out


def kernel(x, positions, norm_gain, w_in, mla_q_a_gain, mla_kv_a_gain, w_uq, w_ukv, mla_q_norm_gain, mla_k_norm_gain, gdn_conv_w, gdn_a_log, gdn_dt_bias, gdn_out_norm_gain, w_out, loss_target, m_norm_gain, m_w_in, m_mla_q_a_gain, m_mla_kv_a_gain, m_w_uq, m_w_ukv, m_mla_q_norm_gain, m_mla_k_norm_gain, m_gdn_conv_w, m_gdn_a_log, m_gdn_dt_bias, m_gdn_out_norm_gain, m_w_out, v_norm_gain, v_w_in, v_mla_q_a_gain, v_mla_kv_a_gain, v_w_uq, v_w_ukv, v_mla_q_norm_gain, v_mla_k_norm_gain, v_gdn_conv_w, v_gdn_a_log, v_gdn_dt_bias, v_gdn_out_norm_gain, v_w_out):
    w = dict(norm_gain=norm_gain, w_in=w_in, mla_q_a_gain=mla_q_a_gain, mla_kv_a_gain=mla_kv_a_gain, w_uq=w_uq,
             w_ukv=w_ukv, mla_q_norm_gain=mla_q_norm_gain, mla_k_norm_gain=mla_k_norm_gain, gdn_conv_w=gdn_conv_w,
             gdn_a_log=gdn_a_log, gdn_dt_bias=gdn_dt_bias, gdn_out_norm_gain=gdn_out_norm_gain, w_out=w_out)
    m = dict(norm_gain=m_norm_gain, w_in=m_w_in, mla_q_a_gain=m_mla_q_a_gain, mla_kv_a_gain=m_mla_kv_a_gain,
             w_uq=m_w_uq, w_ukv=m_w_ukv, mla_q_norm_gain=m_mla_q_norm_gain, mla_k_norm_gain=m_mla_k_norm_gain,
             gdn_conv_w=m_gdn_conv_w, gdn_a_log=m_gdn_a_log, gdn_dt_bias=m_gdn_dt_bias,
             gdn_out_norm_gain=m_gdn_out_norm_gain, w_out=m_w_out)
    v = dict(norm_gain=v_norm_gain, w_in=v_w_in, mla_q_a_gain=v_mla_q_a_gain, mla_kv_a_gain=v_mla_kv_a_gain,
             w_uq=v_w_uq, w_ukv=v_w_ukv, mla_q_norm_gain=v_mla_q_norm_gain, mla_k_norm_gain=v_mla_k_norm_gain,
             gdn_conv_w=v_gdn_conv_w, gdn_a_log=v_gdn_a_log, gdn_dt_bias=v_gdn_dt_bias,
             gdn_out_norm_gain=v_gdn_out_norm_gain, w_out=v_w_out)
    t_len = x.shape[1]

    shards = [w[n][0] if n == "gdn_conv_w" else w[n][0].astype(_BF) for n in SHARDED]
    a_w_in, a_w_uq, a_w_ukv, a_cw, a_w_out = all_gather(shards)

    def cols_whole(g):
        return g.transpose(1, 0, 2).reshape(g.shape[1], N_DEV * g.shape[2])

    p = {n: w[n] for n, _ in SMALL}
    p["w_in"] = arrange_w_in(cols_whole(a_w_in))
    p["w_uq"] = arrange_w_uq(cols_whole(a_w_uq))
    p["w_ukv"] = cols_whole(a_w_ukv)
    p["gdn_conv_w"] = cols_whole(a_cw)
    p["w_out"] = a_w_out.reshape(N_DEV * a_w_out.shape[1], a_w_out.shape[2])

    def col_blocks(g):
        return g.reshape(g.shape[0], N_DEV, g.shape[1] // N_DEV).transpose(1, 0, 2)

    xi, yi, ci = _place()
    c_idx = jnp.reshape(ci, (1,)).astype(jnp.int32)
    q_idx = jnp.reshape(2 * xi + yi, (1,)).astype(jnp.int32)
    flight = {}

    def start_core_exchange(grads):
        blocks = [col_blocks(unarrange_w_in(grads["w_in"])), col_blocks(unarrange_w_uq(grads["w_uq"])),
                  col_blocks(grads["w_ukv"]), col_blocks(grads["gdn_conv_w"]),
                  grads["w_out"].reshape(N_DEV, D_MODEL // N_DEV, D_MODEL)]
        flight["cores"], token = exchange_start("cores_start", core_routes, 4, blocks)
        return token

    def start_chip_exchange(d_xn):
        blocks, landed = exchange_wait("cores_wait", core_routes, flight["cores"], d_xn)
        flight["parts"] = [add_core_parts("add_" + n, g, r, c_idx, F32 if n == "gdn_conv_w" else _BF)
                           for n, g, r in zip(SHARDED, blocks, landed)]
        flight["chips"], token = exchange_start("chips_start", chip_routes, 3, [wire for _, wire in flight["parts"]])
        return token

    pos = positions.reshape(t_len, 1).astype(F32)
    loss, grad_x, grads = local_step(x.reshape(t_len, D_MODEL), pos, loss_target.reshape(t_len, D_MODEL), p,
                                     on_weight_grads=start_core_exchange, on_d_xn=start_chip_exchange)
    loss = lax.psum(loss, ("x", "y", "c"))
    _, from_chips = exchange_wait("chips_wait", chip_routes, flight["chips"], grad_x)
    out = {}
    for n, (prt, _), rcv in zip(SHARDED, flight["parts"], from_chips):
        shape = w[n].shape
        res = adamw_sharded("adamw_" + n, prt, rcv, q_idx, w[n].reshape(shape[-2:]), m[n].reshape(shape[-2:]),
                            v[n].reshape(shape[-2:]))
        out[n] = [a.reshape(shape) for a in res]

    small_all = gather_small(_pack_small(grads))
    res = adamw_small(small_all, _pack_small(w), _pack_small(m), _pack_small(v))
    unpacked = [_unpack_small(a) for a in res]
    for n, _ in SMALL:
        out[n] = [u[n] for u in unpacked]

    return (loss, grad_x.reshape(x.shape), *[out[n][0] for n in WEIGHT_ORDER], *[out[n][1] for n in WEIGHT_ORDER],
            *[out[n][2] for n in WEIGHT_ORDER], *[out[n][3] for n in WEIGHT_ORDER])
```

```python
import functools

import jax
import jax.numpy as jnp
from jax import lax
from jax.experimental import pallas as pl
from jax.experimental.pallas import tpu as pltpu

F32 = jnp.float32
_BF = jnp.bfloat16
HI = lax.Precision.HIGHEST

D_MODEL = 2048
HEADS = 8
NOPE = 128
ROPE = 64
QK = NOPE + ROPE
Q_LORA = 512
KV_LORA = 256
HEAD_PAD = 256
GDN_DIM = 128
WIDTH = HEADS * 128
CONV_W = 4
CHUNK = 64
ROPE_THETA = 10000.0
EPS = 1e-6
N_DEV = 8
LANE = 128
SUBLANE = 8
VMEM_LIMIT = 48 * 1024 * 1024

ADAM_LR, ADAM_B1, ADAM_B2, ADAM_EPS, ADAM_WD, ADAM_STEP = 0.001, 0.9, 0.999, 1e-08, 0.01, 10

P_MGATE, P_GQ, P_GK, P_GV, P_GGATE = 0, 1024, 2048, 3072, 4096
P_CQ, P_CKV, P_KR, P_GAB = 5120, 5632, 5888, 6016
P_COLS = 6144
R_SPLITS = (512, 256, 64, 1024, 1024, 1024, 1024, 8, 8, 1024)


def _cparams(*sem):
    return pltpu.CompilerParams(dimension_semantics=sem, vmem_limit_bytes=VMEM_LIMIT)


def _d_nn(a, b):
    return jnp.dot(a.astype(_BF), b.astype(_BF), preferred_element_type=F32)


def _d_nt(a, b):
    return lax.dot_general(a.astype(_BF), b.astype(_BF), (((1,), (1,)), ((), ())), preferred_element_type=F32)


def _d_tn(a, b):
    return lax.dot_general(a.astype(_BF), b.astype(_BF), (((0,), (0,)), ((), ())), preferred_element_type=F32)


@jax.custom_vjp
def _mm(a, b):
    return _d_nn(a, b)


_mm.defvjp(lambda a, b: (_d_nn(a, b), (a, b)), lambda r, g: (_d_nt(g, r[1]), _d_tn(r[0], g)))


@jax.custom_vjp
def _mm_nt(a, b):
    return _d_nt(a, b)


_mm_nt.defvjp(lambda a, b: (_d_nt(a, b), (a, b)), lambda r, g: (_d_nn(g, r[1]), _d_tn(g, r[0])))


@jax.custom_vjp
def _mm_tn(a, b):
    return _d_tn(a, b)


_mm_tn.defvjp(lambda a, b: (_d_tn(a, b), (a, b)), lambda r, g: (_d_nt(r[1], g), _d_nn(r[0], g)))


def _hi(a, b):
    return jnp.dot(a, b, preferred_element_type=F32, precision=HI)


_NN3 = (((2,), (1,)), ((0,), (0,)))
_NT3 = (((2,), (2,)), ((0,), (0,)))
_TN3 = (((1,), (1,)), ((0,), (0,)))


def _bdot(a, b, dims, hi):
    if hi:
        return lax.dot_general(a, b, dims, preferred_element_type=F32, precision=hi)
    return lax.dot_general(a.astype(_BF), b.astype(_BF), dims, preferred_element_type=F32)


def _batched_matmuls(hi):
    nn = jax.custom_vjp(lambda a, b: _bdot(a, b, _NN3, hi))
    nt = jax.custom_vjp(lambda a, b: _bdot(a, b, _NT3, hi))
    tn = jax.custom_vjp(lambda a, b: _bdot(a, b, _TN3, hi))
    nn.defvjp(lambda a, b: (_bdot(a, b, _NN3, hi), (a, b)),
              lambda r, g: (_bdot(g, r[1], _NT3, hi), _bdot(r[0], g, _TN3, hi)))
    nt.defvjp(lambda a, b: (_bdot(a, b, _NT3, hi), (a, b)),
              lambda r, g: (_bdot(g, r[1], _NN3, hi), _bdot(g, r[0], _TN3, hi)))
    tn.defvjp(lambda a, b: (_bdot(a, b, _TN3, hi), (a, b)),
              lambda r, g: (_bdot(r[1], g, _NT3, hi), _bdot(r[0], g, _NN3, hi)))
    return nn, nt, tn


_bmm, _bmm_nt, _bmm_tn = _batched_matmuls(False)
_bhi, _bhi_nt, _bhi_tn = _batched_matmuls(HI)


def _split2(x):
    hi = x.astype(_BF)
    return hi, (x - hi.astype(F32)).astype(_BF)


def _pdot(a, b, mode):
    (a_hi, a_lo), (b_hi, b_lo) = _split2(a), _split2(b)
    a_ax, b_ax, dims = {"nn": (2, 1, _NN3), "nt": (2, 2, _NT3), "tn": (1, 1, _TN3)}[mode]
    lhs = jnp.concatenate([a_hi, a_lo, a_hi], axis=a_ax)
    rhs = jnp.concatenate([b_hi, b_hi, b_lo], axis=b_ax)
    return lax.dot_general(lhs, rhs, dims, preferred_element_type=F32)


def _packed_matmuls():
    nn = jax.custom_vjp(lambda a, b: _pdot(a, b, "nn"))
    nn.defvjp(lambda a, b: (_pdot(a, b, "nn"), (a, b)), lambda r, g: (_pdot(g, r[1], "nt"), _pdot(r[0], g, "tn")))
    return nn


_bh3 = _packed_matmuls()
_bh3_passes = _batched_matmuls(lax.Precision.HIGH)[0]


@functools.partial(jax.custom_vjp, nondiff_argnums=(1, 2))
def _roll(x, shift, axis):
    return pltpu.roll(x, shift, axis)


def _roll_fwd(x, shift, axis):
    return pltpu.roll(x, shift, axis), None


def _roll_bwd(shift, axis, _, g):
    n = g.shape[axis]
    return (pltpu.roll(g, (n - shift) % n, axis),)


_roll.defvjp(_roll_fwd, _roll_bwd)


def _rms(x, gain):
    return x * lax.rsqrt(jnp.mean(x * x, axis=-1, keepdims=True) + EPS) * gain


MM_TILE = 1024
MM_DEPTH = 2048


def matmul(name, a, b, mode, after=None):
    if mode == "nn":
        (m, k), (k2, n) = a.shape, b.shape
    elif mode == "nt":
        (m, k), (n, k2) = a.shape, b.shape
    else:
        (k, m), (k2, n) = a.shape, b.shape
    assert k == k2, (name, a.shape, b.shape)
    tm, tn, tk = min(MM_TILE, m), min(MM_TILE, n), min(MM_DEPTH, k)
    assert m % tm == 0 and n % tn == 0 and k % tk == 0, (name, m, n, k)
    dot = {"nn": _d_nn, "nt": _d_nt, "tn": _d_tn}[mode]

    def body(a_ref, b_ref, *rest):
        o_ref = rest[-1]
        kk = pl.program_id(2)
        part = dot(a_ref[...], b_ref[...])

        @pl.when(kk == 0)
        def _():
            o_ref[...] = part

        @pl.when(kk != 0)
        def _():
            o_ref[...] += part

    if mode == "nn":
        a_spec = pl.BlockSpec((tm, tk), lambda j, i, kk: (i, kk))
        b_spec = pl.BlockSpec((tk, tn), lambda j, i, kk: (kk, j))
    elif mode == "nt":
        a_spec = pl.BlockSpec((tm, tk), lambda j, i, kk: (i, kk))
        b_spec = pl.BlockSpec((tn, tk), lambda j, i, kk: (j, kk))
    else:
        a_spec = pl.BlockSpec((tk, tm), lambda j, i, kk: (kk, i))
        b_spec = pl.BlockSpec((tk, tn), lambda j, i, kk: (kk, j))
    return pl.pallas_call(
        body,
        name=name,
        grid=(n // tn, m // tm, k // tk),
        in_specs=[a_spec, b_spec] + ([] if after is None else [pl.BlockSpec(memory_space=pl.ANY)]),
        out_specs=pl.BlockSpec((tm, tn), lambda j, i, kk: (i, j)),
        out_shape=jax.ShapeDtypeStruct((m, n), F32),
        compiler_params=_cparams("parallel", "parallel", "arbitrary"),
    )(*((a, b) if after is None else (a, b, after)))


def rowwise(name, fn, t_len, tile, row_in, full_in, row_out, acc_out=(), carries=(), reverse=False):
    tile = min(tile, t_len)
    n = t_len // tile
    assert t_len % tile == 0 and tile % SUBLANE == 0
    n_in, n_ro, n_acc, n_car = len(row_in) + len(full_in), len(row_out), len(acc_out), len(carries)

    def ti(i):
        return (n - 1 - i) if reverse else i

    in_specs, args = [], []
    for arr, kind in row_in:
        if kind[0] == "r":
            in_specs.append(pl.BlockSpec((tile, kind[1]), lambda i, c=kind[2]: (ti(i), c)))
        elif kind[0] == "h":
            in_specs.append(pl.BlockSpec((arr.shape[0], tile, arr.shape[2]), lambda i: (0, ti(i), 0)))
        else:
            in_specs.append(pl.BlockSpec(
                (SUBLANE, kind[1]), lambda i, c=kind[2]: (jnp.maximum(ti(i) * (tile // SUBLANE) - 1, 0), c)))
        args.append(arr)
    for arr in full_in:
        in_specs.append(pl.BlockSpec(arr.shape, lambda i, nd=arr.ndim: (0,) * nd))
        args.append(arr)
    out_specs, out_shape = [], []
    for kind in row_out:
        if kind[0] == "r":
            out_specs.append(pl.BlockSpec((tile, kind[1]), lambda i: (ti(i), 0)))
            out_shape.append(jax.ShapeDtypeStruct((t_len, kind[1]), kind[2]))
        else:
            out_specs.append(pl.BlockSpec((kind[1], tile, kind[2]), lambda i: (0, ti(i), 0)))
            out_shape.append(jax.ShapeDtypeStruct((kind[1], t_len, kind[2]), kind[3]))
    for shp in acc_out:
        out_specs.append(pl.BlockSpec(shp, lambda i, nd=len(shp): (0,) * nd))
        out_shape.append(jax.ShapeDtypeStruct(shp, F32))

    def body(*refs):
        in_refs = refs[:n_in]
        ro_refs = refs[n_in:n_in + n_ro]
        acc_refs = refs[n_in + n_ro:n_in + n_ro + n_acc]
        car_refs = refs[n_in + n_ro + n_acc:]
        step = pl.program_id(0)
        if n_car:
            @pl.when(step == 0)
            def _():
                for r in car_refs:
                    r[...] = jnp.zeros_like(r)
        vals = [r[...].astype(F32) for r in in_refs] + [r[...] for r in car_refs]
        outs = fn(ti(step), *vals)
        assert len(outs) == n_ro + n_acc + n_car, (name, len(outs))
        for r, o in zip(ro_refs, outs[:n_ro]):
            r[...] = o.astype(r.dtype)
        for r, o in zip(acc_refs, outs[n_ro:n_ro + n_acc]):
            @pl.when(step == 0)
            def _(r=r, o=o):
                r[...] = o

            @pl.when(step != 0)
            def _(r=r, o=o):
                r[...] += o
        for r, o in zip(car_refs, outs[n_ro + n_acc:]):
            r[...] = o

    res = pl.pallas_call(
        body,
        name=name,
        grid=(n,),
        in_specs=in_specs,
        out_specs=out_specs,
        out_shape=out_shape,
        scratch_shapes=[pltpu.VMEM(s, F32) for s in carries],
        compiler_params=_cparams("arbitrary"),
    )(*args)
    return list(res)


def _vjp_fn(fn, n_diff, n_out):
    def g(i, *a):
        ins, cts = a[:len(a) - n_out], a[len(a) - n_out:]
        diff, rest = ins[:n_diff], ins[n_diff:]
        _, pull = jax.vjp(lambda *d: tuple(fn(i, *d, *rest)), *diff)
        return tuple(pull(tuple(cts)))

    return g


def f_rms_x(i, x, gain):
    return (_rms(x, gain),)


def f_lat(i, cq, ckv, gq, gkv):
    return _rms(cq, gq), _rms(ckv, gkv)


def _rope_tables(pos, invf):
    ang = pos * invf
    lane = lax.broadcasted_iota(jnp.int32, (1, LANE), 1)
    cosv, sinv = jnp.cos(ang), jnp.sin(ang)
    half = ROPE // 2
    c = jnp.where(lane < ROPE, cosv, 0.0)
    sa = jnp.where(lane < half, -sinv, 0.0)
    sb = jnp.where((lane >= half) & (lane < ROPE), sinv, 0.0)
    return c, sa, sb


def _rope(xh, tabs):
    c, sa, sb = tabs
    half = ROPE // 2
    return xh * c + _roll(xh, LANE - half, 1) * sa + _roll(xh, half, 1) * sb


def f_head(i, q_raw, kv_raw, kr, qg, kg, pos, invf):
    tabs = _rope_tables(pos, invf)
    qs, ks, vs = [], [], []
    kr_ss = jnp.sum(kr * kr, axis=-1, keepdims=True)
    for h in range(HEADS):
        lo = q_raw[:, HEAD_PAD * h:HEAD_PAD * h + NOPE]
        hi = q_raw[:, HEAD_PAD * h + NOPE:HEAD_PAD * (h + 1)]
        ss = jnp.sum(lo * lo, axis=-1, keepdims=True) + jnp.sum(hi * hi, axis=-1, keepdims=True)
        r = lax.rsqrt(ss * (1.0 / QK) + EPS)
        qs.append(jnp.concatenate([lo * r * qg[:, :NOPE], _rope(hi * r * qg[:, NOPE:], tabs)], axis=1))
        lo = kv_raw[:, 2 * NOPE * h:2 * NOPE * h + NOPE]
        ss = jnp.sum(lo * lo, axis=-1, keepdims=True) + kr_ss
        r = lax.rsqrt(ss * (1.0 / QK) + EPS)
        ks.append(jnp.concatenate([lo * r * kg[:, :NOPE], _rope(kr * r * kg[:, NOPE:], tabs)], axis=1))
        vs.append(kv_raw[:, 2 * NOPE * h + NOPE:2 * NOPE * (h + 1)])
    return jnp.stack(qs), jnp.stack(ks), jnp.stack(vs)


def f_mix(i, o_mla, mgate, o_gdn, ggate, og):
    parts = [o_mla * jax.nn.silu(mgate)]
    for h in range(HEADS):
        parts.append(_rms(o_gdn[h], og) * jax.nn.silu(ggate[:, LANE * h:LANE * (h + 1)]))
    return (jnp.concatenate(parts, axis=1),)


def _row(a, j):
    rows = lax.broadcasted_iota(jnp.int32, a.shape, 0)
    return jnp.sum(jnp.where(rows == j, a, 0.0), axis=0, keepdims=True)


def _shift_rows(x, halo, d):
    xs = _roll(x, d, 0)
    hs = _roll(halo, d, 0)
    r8 = lax.broadcasted_iota(jnp.int32, hs.shape, 0)
    top = jnp.where(r8 < d, hs, xs[:SUBLANE])
    return jnp.concatenate([top, xs[SUBLANE:]], axis=0)


def _conv_silu(x, halo, w):
    y = _row(w, CONV_W - 1) * x
    for j in range(CONV_W - 1):
        y = y + _row(w, j) * _shift_rows(x, halo, CONV_W - 1 - j)
    return jax.nn.silu(y)


def _head_select(offset):
    r = lax.broadcasted_iota(jnp.int32, (LANE, WIDTH), 0)
    c = lax.broadcasted_iota(jnp.int32, (LANE, WIDTH), 1)
    return (r == offset + lax.shift_right_logical(c, 7)).astype(_BF)


def _split3(x):
    x1 = x.astype(_BF)
    r1 = x - x1.astype(F32)
    x2 = r1.astype(_BF)
    return x1, x2, (r1 - x2.astype(F32)).astype(_BF)


@jax.custom_vjp
def _spread(x, sel):
    return _d_nn(jnp.concatenate(_split3(x), axis=1), jnp.concatenate([sel, sel, sel], axis=0))


def _spread_fwd(x, sel):
    return _spread(x, sel), sel


def _spread_bwd(sel, g):
    g1, g2, g3 = _split3(g)
    return _d_nt(g1, sel) + _d_nt(g2, sel) + _d_nt(g3, sel), jnp.zeros_like(sel)


_spread.defvjp(_spread_fwd, _spread_bwd)


def f_gdn_pre(i, gq, gk, gv, gab, hq, hk, hv, cwq, cwk, cwv, alog, dtb):
    live = jnp.where(i == 0, 0.0, 1.0)
    q = _conv_silu(gq, hq * live, cwq)
    k = _conv_silu(gk, hk * live, cwk)
    v = _conv_silu(gv, hv * live, cwv)
    g = _spread(-jnp.exp(alog) * jax.nn.softplus(gab + dtb), _head_select(0))
    beta = _spread(jax.nn.sigmoid(gab), _head_select(HEADS))
    qs, ks, vs, gs, bs = [], [], [], [], []
    for h in range(HEADS):
        sl = slice(LANE * h, LANE * (h + 1))
        qh, kh = q[:, sl], k[:, sl]
        qs.append(qh * lax.rsqrt(jnp.sum(qh * qh, axis=-1, keepdims=True) + EPS) * (GDN_DIM ** -0.5))
        ks.append(kh * lax.rsqrt(jnp.sum(kh * kh, axis=-1, keepdims=True) + EPS))
        vs.append(v[:, sl])
        gs.append(g[:, sl])
        bs.append(beta[:, sl])
    return jnp.stack(qs), jnp.stack(ks), jnp.stack(vs), jnp.stack(gs), jnp.stack(bs)


def gdn_pre_bwd(i, gq, gk, gv, gab, hq, hk, hv, dq, dk, dv, dg, db, cwq, cwk, cwv, alog, dtb, cq, ck, cv):
    grads = _vjp_fn(f_gdn_pre, 12, 5)(i, gq, gk, gv, gab, hq, hk, hv, cwq, cwk, cwv, alog, dtb, dq, dk, dv, dg, db)
    dgq, dgk, dgv, dgab, dhq, dhk, dhv, dcwq, dcwk, dcwv, dalog, ddtb = grads

    def add_tail(dx, carry):
        return jnp.concatenate([dx[:-SUBLANE], dx[-SUBLANE:] + carry], axis=0)

    return (add_tail(dgq, cq), add_tail(dgk, ck), add_tail(dgv, cv), dgab,
            dcwq, dcwk, dcwv, dalog, ddtb, dhq, dhk, dhv)


def f_loss(i, x, h, tgt):
    e = x + h - tgt
    part = 0.5 * jnp.sum(e * e) * (1.0 / D_MODEL)
    dy = e * (1.0 / D_MODEL)
    return dy, dy, jnp.zeros((SUBLANE, LANE), F32) + part


def _flash_tile(t_len):
    return min(512, t_len)


FLASH_HEADS = 4
FLASH_BWD_HEADS = 2
LOG2E = 1.4426950408889634


def _causal(rows0, shape):
    r = rows0 + lax.broadcasted_iota(jnp.int32, shape, 0)
    c = lax.broadcasted_iota(jnp.int32, shape, 1)
    return c <= r


def flash_fwd(q, k, v):
    h_n, t_len, _ = q.shape
    tq = _flash_tile(t_len)
    nq = t_len // tq
    hb = FLASH_HEADS
    kw = 2 if nq % 2 == 0 else 1
    tk = kw * tq
    c2 = (QK ** -0.5) * LOG2E
    pairs = [(i, j) for i in range(nq) for j in range(i // kw + 1)]
    qt = jnp.array([p[0] for p in pairs], jnp.int32)
    kt = jnp.array([p[1] for p in pairs], jnp.int32)

    def body(qt_ref, kt_ref, q_ref, k_ref, v_ref, o_ref, lse_ref, m_s, acc_s):
        step = pl.program_id(1)
        qi, kj = qt_ref[step], kt_ref[step]
        last = qi // kw

        @pl.when(kj == 0)
        def _():
            m_s[...] = jnp.full_like(m_s, -jnp.inf)
            acc_s[...] = jnp.zeros_like(acc_s)

        def tile(diagonal):
            s = _bdot(q_ref[...], k_ref[...], _NT3, False) * c2
            if diagonal:
                s = jnp.where(_causal((qi % kw) * tq, (tq, tk))[None], s, -jnp.inf)
            m_old = m_s[...]
            m_new = jnp.maximum(m_old, jnp.max(s, axis=-1, keepdims=True))
            p = jnp.exp2(s - m_new).astype(_BF)
            v_ones = jnp.concatenate([v_ref[...], jnp.ones((hb, tk, LANE), _BF)], axis=2)
            acc_s[...] = jnp.exp2(m_old - m_new) * acc_s[...] + _bdot(p, v_ones, _NN3, False)
            m_s[...] = m_new

        @pl.when(kj < last)
        def _():
            tile(False)

        @pl.when(kj == last)
        def _():
            tile(True)
            acc = acc_s[...]
            l_sum = acc[:, :, LANE:]
            o = acc[:, :, :LANE] / l_sum
            for hh in range(hb):
                o_ref[:, LANE * hh:LANE * (hh + 1)] = o[hh]
            lse_ref[...] = m_s[...] + jnp.log2(jnp.max(l_sum, axis=-1, keepdims=True))

    return pl.pallas_call(
        body,
        name="flash_fwd",
        grid_spec=pltpu.PrefetchScalarGridSpec(
            num_scalar_prefetch=2,
            grid=(h_n // hb, qt.shape[0]),
            in_specs=[
                pl.BlockSpec((hb, tq, HEAD_PAD), lambda h, s, qt_ref, kt_ref: (h, qt_ref[s], 0)),
                pl.BlockSpec((hb, tk, HEAD_PAD), lambda h, s, qt_ref, kt_ref: (h, kt_ref[s], 0)),
                pl.BlockSpec((hb, tk, LANE), lambda h, s, qt_ref, kt_ref: (h, kt_ref[s], 0)),
            ],
            out_specs=[
                pl.BlockSpec((tq, hb * LANE), lambda h, s, qt_ref, kt_ref: (qt_ref[s], h)),
                pl.BlockSpec((hb, tq, 1), lambda h, s, qt_ref, kt_ref: (h, qt_ref[s], 0)),
            ],
            scratch_shapes=[pltpu.VMEM((hb, tq, 1), F32), pltpu.VMEM((hb, tq, 2 * LANE), F32)],
        ),
        out_shape=[jax.ShapeDtypeStruct((t_len, h_n * LANE), F32), jax.ShapeDtypeStruct((h_n, t_len, 1), F32)],
        compiler_params=_cparams("parallel", "arbitrary"),
    )(qt, kt, q, k, v)


def flash_bwd(q, k, v, do, lse, delta):
    h_n, t_len, _ = q.shape
    tq = _flash_tile(t_len)
    nq = t_len // tq
    hb = FLASH_BWD_HEADS
    kw = 2 if nq % 2 == 0 else 1
    tk = kw * tq
    pairs = [(i, j) for j in range(nq // kw) for i in range(kw * j, nq)]
    n_steps = len(pairs)
    qt = jnp.array([p[0] for p in pairs], jnp.int32)
    kt = jnp.array([p[1] for p in pairs], jnp.int32)
    scale = QK ** -0.5
    c2 = scale * LOG2E

    def body(qt_ref, kt_ref, q_ref, k_ref, v_ref, do_ref, lse_ref, dl_ref, dq_hbm, dk_ref, dv_ref, dq_s, dq_sem):
        group, step = pl.program_id(0), pl.program_id(1)
        qi, kj = qt_ref[step], kt_ref[step]

        @pl.when(step == 0)
        def _():
            dq_s[...] = jnp.zeros_like(dq_s)

        def tile(diagonal):
            qb, kb = q_ref[...], k_ref[...]
            dob = jnp.stack([do_ref[:, LANE * hh:LANE * (hh + 1)] for hh in range(hb)])
            p = jnp.exp2(_bdot(qb, kb, _NT3, False) * c2 - lse_ref[...])
            if diagonal:
                p = jnp.where(_causal((qi % kw) * tq, (tq, tk))[None], p, 0.0)
            dv = _bdot(p, dob, _TN3, False)
            ds = p * (_bdot(dob, v_ref[...], _NT3, False) - dl_ref[...]) * scale
            dk = _bdot(ds, qb, _TN3, False)
            dq_s[:, pl.ds(pl.multiple_of(qi * tq, tq), tq), :] += _bdot(ds, kb, _NN3, False)
            return dk, dv

        @pl.when(qi == kw * kj)
        def _():
            dk_ref[...], dv_ref[...] = tile(True)

        @pl.when((qi != kw * kj) & (qi // kw == kj))
        def _():
            dk, dv = tile(True)
            dk_ref[...] += dk
            dv_ref[...] += dv

        @pl.when(qi // kw > kj)
        def _():
            dk, dv = tile(False)
            dk_ref[...] += dk
            dv_ref[...] += dv

        @pl.when(step == n_steps - 1)
        def _():
            out = pltpu.make_async_copy(dq_s, dq_hbm.at[pl.ds(group * hb, hb)], dq_sem)
            out.start()
            out.wait()

    def qmap(h, s, qt_ref, kt_ref):
        return (h, qt_ref[s], 0)

    def kmap(h, s, qt_ref, kt_ref):
        return (h, kt_ref[s], 0)

    return pl.pallas_call(
        body,
        name="flash_bwd",
        grid_spec=pltpu.PrefetchScalarGridSpec(
            num_scalar_prefetch=2,
            grid=(h_n // hb, n_steps),
            in_specs=[
                pl.BlockSpec((hb, tq, HEAD_PAD), qmap),
                pl.BlockSpec((hb, tk, HEAD_PAD), kmap),
                pl.BlockSpec((hb, tk, LANE), kmap),
                pl.BlockSpec((tq, hb * LANE), lambda h, s, qt_ref, kt_ref: (qt_ref[s], h)),
                pl.BlockSpec((hb, tq, 1), qmap),
                pl.BlockSpec((hb, tq, 1), qmap),
            ],
            out_specs=[
                pl.BlockSpec(memory_space=pl.ANY),
                pl.BlockSpec((hb, tk, HEAD_PAD), kmap),
                pl.BlockSpec((hb, tk, LANE), kmap),
            ],
            scratch_shapes=[pltpu.VMEM((hb, t_len, HEAD_PAD), F32), pltpu.SemaphoreType.DMA],
        ),
        out_shape=[
            jax.ShapeDtypeStruct((h_n, t_len, HEAD_PAD), F32),
            jax.ShapeDtypeStruct((h_n, t_len, HEAD_PAD), F32),
            jax.ShapeDtypeStruct((h_n, t_len, LANE), F32),
        ],
        compiler_params=_cparams("parallel", "arbitrary"),
    )(qt, kt, q, k, v, do, lse, delta)


def _tri_ones(h_n):
    ii = lax.broadcasted_iota(jnp.int32, (h_n, CHUNK, CHUNK), 1)
    jj = lax.broadcasted_iota(jnp.int32, (h_n, CHUNK, CHUNK), 2)
    return (ii >= jj).astype(_BF)


@jax.custom_vjp
def _chunk_cumsum(gb):
    tri = _tri_ones(gb.shape[0])
    return _bdot(jnp.concatenate([tri, tri, tri], axis=2), jnp.concatenate(_split3(gb), axis=1), _NN3, False)


def _chunk_cumsum_bwd(_, ct):
    tri = _tri_ones(ct.shape[0])
    return (_bdot(jnp.concatenate([tri, tri, tri], axis=1), jnp.concatenate(_split3(ct), axis=1), _TN3, False),)


_chunk_cumsum.defvjp(lambda gb: (_chunk_cumsum(gb), None), _chunk_cumsum_bwd)


@jax.custom_vjp
def _pair_diff(gcb):
    g1, g2, g3 = _split3(gcb)
    lane = lax.broadcasted_iota(jnp.int32, (1, 1, LANE), 2)
    one, zero = jnp.ones((), _BF), jnp.zeros((), _BF)
    a = jnp.where(lane == 0, g1, jnp.where(lane == 1, g2, jnp.where(lane == 2, g3, jnp.where(lane < 6, one, zero))))
    b = jnp.where(lane < 3, one, jnp.where(lane == 3, -g1, jnp.where(lane == 4, -g2, jnp.where(lane == 5, -g3, zero))))
    return _bdot(a, b, _NT3, False)


def _pair_diff_bwd(_, ct):
    parts = _split3(ct)
    ones = jnp.ones((ct.shape[0], 3 * CHUNK, LANE), _BF)
    rows = _bdot(jnp.concatenate(parts, axis=2), ones, _NN3, False)
    cols = _bdot(jnp.concatenate(parts, axis=1), ones, _TN3, False)
    lane = lax.broadcasted_iota(jnp.int32, (1, 1, LANE), 2)
    return (jnp.where(lane == 0, rows - cols, 0.0),)


_pair_diff.defvjp(lambda gcb: (_pair_diff(gcb), None), _pair_diff_bwd)


@jax.custom_vjp
def _saved_inverse(lmat, inv):
    return inv


def _saved_inverse_bwd(inv, g):
    return -_pdot(_pdot(inv, g, "tn"), inv, "nt"), jnp.zeros_like(inv)


_saved_inverse.defvjp(lambda lmat, inv: (inv, inv), _saved_inverse_bwd)


def gdn_step(s, q, k, v, gb, bb, inv_saved=None):
    c = CHUNK
    ii = lax.broadcasted_iota(jnp.int32, (1, c, c), 1)
    jj = lax.broadcasted_iota(jnp.int32, (1, c, c), 2)
    incl, strict = ii >= jj, ii > jj
    gcb = _chunk_cumsum(gb)
    diff = _pair_diff(gcb)
    decay = jnp.where(incl, jnp.exp(jnp.where(incl, diff, 0.0)), 0.0)
    kb, vb = k * bb, v * bb
    egc = jnp.exp(gcb)
    lmat = jnp.where(strict, _bmm_nt(kb, k) * decay, 0.0)
    if inv_saved is None:
        mm3 = _bh3_passes
        inv = (ii == jj).astype(F32) - lmat
        pw = mm3(lmat, lmat)
        for step in range(5):
            inv = inv + mm3(inv, pw)
            if step < 4:
                pw = mm3(pw, pw)
    else:
        mm3 = _bh3
        inv = _saved_inverse(lmat, inv_saved)
    u = mm3(inv, vb)
    w = mm3(inv, kb * egc)
    attn = _bmm_nt(q, k) * decay
    qd = q * egc
    g_end = jnp.sum(gb, axis=1, keepdims=True)
    kd = k * jnp.exp(g_end - gcb)
    v_new = u - _bmm(w, s)
    o = _bmm(qd, s) + _bmm(attn, v_new)
    s_new = s * jnp.exp(g_end) + _bmm_tn(kd, v_new)
    return s_new, o, inv


def gdn_fwd(q, k, v, gb, bb):
    h_n, t_len, d = q.shape
    n = t_len // CHUNK
    blk = pl.BlockSpec((h_n, CHUNK, d), lambda i: (0, i, 0))

    def body(q_ref, k_ref, v_ref, g_ref, b_ref, o_ref, sall_ref, inv_ref, s_s):
        @pl.when(pl.program_id(0) == 0)
        def _():
            s_s[...] = jnp.zeros_like(s_s)

        s = s_s[...]
        sall_ref[0] = s
        s_s[...], o_ref[...], inv_ref[0] = gdn_step(s, q_ref[...], k_ref[...], v_ref[...], g_ref[...], b_ref[...])

    return pl.pallas_call(
        body,
        name="gdn_fwd",
        grid=(n,),
        in_specs=[blk] * 5,
        out_specs=[blk, pl.BlockSpec((1, h_n, d, d), lambda i: (i, 0, 0, 0)),
                   pl.BlockSpec((1, h_n, CHUNK, CHUNK), lambda i: (i, 0, 0, 0))],
        out_shape=[jax.ShapeDtypeStruct((h_n, t_len, d), F32), jax.ShapeDtypeStruct((n, h_n, d, d), F32),
                   jax.ShapeDtypeStruct((n, h_n, CHUNK, CHUNK), F32)],
        scratch_shapes=[pltpu.VMEM((h_n, d, d), F32)],
        compiler_params=_cparams("arbitrary"),
    )(q, k, v, gb, bb)


def gdn_bwd(q, k, v, gb, bb, s_all, inv_all, do):
    h_n, t_len, d = q.shape
    n = t_len // CHUNK
    blk = pl.BlockSpec((h_n, CHUNK, d), lambda i: (0, n - 1 - i, 0))

    def body(q_ref, k_ref, v_ref, g_ref, b_ref, sall_ref, inv_ref, do_ref, dq_ref, dk_ref, dv_ref, dg_ref, db_ref,
             ds_s):
        @pl.when(pl.program_id(0) == 0)
        def _():
            ds_s[...] = jnp.zeros_like(ds_s)

        inv = inv_ref[0]
        _, pull = jax.vjp(lambda *a: gdn_step(*a, inv_saved=inv)[:2], sall_ref[0], q_ref[...], k_ref[...], v_ref[...],
                          g_ref[...], b_ref[...])
        ds_s[...], dq_ref[...], dk_ref[...], dv_ref[...], dg_ref[...], db_ref[...] = pull((ds_s[...], do_ref[...]))

    return pl.pallas_call(
        body,
        name="gdn_bwd",
        grid=(n,),
        in_specs=[blk] * 5 + [pl.BlockSpec((1, h_n, d, d), lambda i: (n - 1 - i, 0, 0, 0)),
                              pl.BlockSpec((1, h_n, CHUNK, CHUNK), lambda i: (n - 1 - i, 0, 0, 0)), blk],
        out_specs=[blk] * 5,
        out_shape=[jax.ShapeDtypeStruct((h_n, t_len, d), F32)] * 5,
        scratch_shapes=[pltpu.VMEM((h_n, d, d), F32)],
        compiler_params=_cparams("arbitrary"),
    )(q, k, v, gb, bb, s_all, inv_all, do)


def _pad_cols(a, n):
    return jnp.pad(a, ((0, 0), (0, n - a.shape[1])))


def arrange_w_in(w):
    pieces, start = [], 0
    for n in R_SPLITS:
        pieces.append(w[:, start:start + n])
        start += n
    cq, ckv, kr, mgate, gq, gk, gv, ga, gb, ggate = pieces
    return jnp.concatenate([mgate, gq, gk, gv, ggate, cq, ckv, _pad_cols(kr, LANE),
                            _pad_cols(jnp.concatenate([ga, gb], axis=1), LANE)], axis=1)


def unarrange_w_in(g):
    def cols(start, n):
        return g[:, start:start + n]
    return jnp.concatenate([cols(P_CQ, Q_LORA), cols(P_CKV, KV_LORA), cols(P_KR, ROPE), cols(P_MGATE, WIDTH),
                            cols(P_GQ, WIDTH), cols(P_GK, WIDTH), cols(P_GV, WIDTH), cols(P_GAB, HEADS),
                            cols(P_GAB + HEADS, HEADS), cols(P_GGATE, WIDTH)], axis=1)


def arrange_w_uq(w):
    w = w.reshape(w.shape[0], HEADS, QK)
    return jnp.pad(w, ((0, 0), (0, 0), (0, HEAD_PAD - QK))).reshape(w.shape[0], HEADS * HEAD_PAD)


def unarrange_w_uq(g):
    return g.reshape(g.shape[0], HEADS, HEAD_PAD)[:, :, :QK].reshape(g.shape[0], HEADS * QK)


def local_step(x, pos, tgt, p, on_weight_grads=None, on_d_xn=None):
    t_len = x.shape[0]
    w_in, w_uq, w_ukv, w_out = p["w_in"], p["w_uq"], p["w_ukv"], p["w_out"]
    norm_gain = p["norm_gain"].reshape(1, D_MODEL)
    qa_gain = p["mla_q_a_gain"].reshape(1, Q_LORA)
    kva_gain = p["mla_kv_a_gain"].reshape(1, KV_LORA)
    qg = _pad_cols(p["mla_q_norm_gain"].reshape(1, QK), HEAD_PAD)
    kg = _pad_cols(p["mla_k_norm_gain"].reshape(1, QK), HEAD_PAD)
    cw = p["gdn_conv_w"].reshape(CONV_W, 3 * WIDTH)
    cwq, cwk, cwv = cw[:, :WIDTH], cw[:, WIDTH:2 * WIDTH], cw[:, 2 * WIDTH:]
    alog = _pad_cols(p["gdn_a_log"].reshape(1, HEADS), LANE)
    dtb = _pad_cols(p["gdn_dt_bias"].reshape(1, HEADS), LANE)
    og = p["gdn_out_norm_gain"].reshape(1, GDN_DIM)
    half = ROPE // 2
    inv_freq = jnp.power(ROPE_THETA, -jnp.arange(half, dtype=F32) / half)
    invf = _pad_cols(jnp.concatenate([inv_freq, inv_freq]).reshape(1, ROPE), LANE)

    rt = 256
    r = "r"
    (xn,) = rowwise("rms_x", f_rms_x, t_len, rt, [(x, (r, D_MODEL, 0))], [norm_gain], [(r, D_MODEL, _BF)])
    proj = matmul("proj", xn, w_in, "nn")
    cq_in = (proj, (r, Q_LORA, P_CQ // Q_LORA))
    ckv_in = (proj, (r, KV_LORA, P_CKV // KV_LORA))
    kr_in = (proj, (r, LANE, P_KR // LANE))
    mgate_in = (proj, (r, WIDTH, P_MGATE // WIDTH))
    ggate_in = (proj, (r, WIDTH, P_GGATE // WIDTH))
    gqkv_in = [(proj, (r, WIDTH, P_GQ // WIDTH)), (proj, (r, WIDTH, P_GK // WIDTH)), (proj, (r, WIDTH, P_GV // WIDTH))]
    gab_in = (proj, (r, LANE, P_GAB // LANE))
    halos = [(proj, ("halo", WIDTH, P_GQ // WIDTH)), (proj, ("halo", WIDTH, P_GK // WIDTH)),
             (proj, ("halo", WIDTH, P_GV // WIDTH))]

    q_lat, kv_lat = rowwise("lat", f_lat, t_len, rt, [cq_in, ckv_in], [qa_gain, kva_gain],
                            [(r, Q_LORA, _BF), (r, KV_LORA, _BF)])
    q_raw = matmul("q_up", q_lat, w_uq, "nn")
    kv_raw = matmul("kv_up", kv_lat, w_ukv, "nn")
    wide = HEADS * HEAD_PAD
    head_in = [(q_raw, (r, wide, 0)), (kv_raw, (r, wide, 0)), kr_in]
    pos_in = (pos, (r, 1, 0))
    q_full, k_full, v_mla = rowwise(
        "head", lambda i, qr, kvr, kr, ps, qg_, kg_, iv: f_head(i, qr, kvr, kr, qg_, kg_, ps, iv), t_len, rt,
        head_in + [pos_in], [qg, kg, invf],
        [("h", HEADS, HEAD_PAD, _BF), ("h", HEADS, HEAD_PAD, _BF), ("h", HEADS, LANE, _BF)])
    o_mla, lse = flash_fwd(q_full, k_full, v_mla)

    pre_in = gqkv_in + [gab_in] + halos
    pre_full = [cwq, cwk, cwv, alog, dtb]
    hkind = ("h", HEADS, GDN_DIM, F32)
    gq_n, gk_n, gv_n, g_b, b_b = rowwise("gdn_pre", f_gdn_pre, t_len, rt, pre_in, pre_full, [hkind] * 5)
    o_gdn, s_all, inv_all = gdn_fwd(gq_n, gk_n, gv_n, g_b, b_b)

    mix_in = [(o_mla, (r, WIDTH, 0)), mgate_in, (o_gdn, ("h",)), ggate_in]
    (mixed,) = rowwise("mix", f_mix, t_len, rt, mix_in, [og], [(r, 2 * WIDTH, _BF)])
    h_out = matmul("out_proj", mixed, w_out, "nn")
    dy, dy_mx, loss_acc = rowwise("loss", f_loss, t_len, rt,
                                  [(x, (r, D_MODEL, 0)), (h_out, (r, D_MODEL, 0)), (tgt, (r, D_MODEL, 0))], [],
                                  [(r, D_MODEL, F32), (r, D_MODEL, _BF)], [(SUBLANE, LANE)])
    loss = loss_acc[0, 0]

    d_mixed = matmul("d_mixed", dy_mx, w_out, "nt")
    g_w_out = matmul("g_w_out", mixed, dy_mx, "tn")

    def mix_bwd(i, o_mla_, mgate_, o_gdn_, ggate_, d_mixed_, og_):
        do_mla_, d_mgate_, do_gdn_, d_ggate_, g_og_ = _vjp_fn(f_mix, 5, 1)(i, o_mla_, mgate_, o_gdn_, ggate_, og_, d_mixed_)
        delta_ = jnp.stack([jnp.sum(o_mla_[:, LANE * h:LANE * (h + 1)] * do_mla_[:, LANE * h:LANE * (h + 1)],
                                    axis=-1, keepdims=True) for h in range(HEADS)])
        return do_mla_, d_mgate_, do_gdn_, d_ggate_, delta_, g_og_

    do_mla, d_mgate, do_gdn, d_ggate, delta, g_og = rowwise(
        "mix_bwd", mix_bwd, t_len, rt, mix_in + [(d_mixed, (r, 2 * WIDTH, 0))], [og],
        [(r, WIDTH, F32), (r, WIDTH, _BF), hkind, (r, WIDTH, _BF), ("h", HEADS, 1, F32)], [(1, GDN_DIM)])
    dq_n, dk_n, dv_n, dg_b, db_b = gdn_bwd(gq_n, gk_n, gv_n, g_b, b_b, s_all, inv_all, do_gdn)
    cts_in = [(a, ("h",)) for a in (dq_n, dk_n, dv_n, dg_b, db_b)]
    d_gq, d_gk, d_gv, d_gab, g_cwq, g_cwk, g_cwv, g_alog, g_dtb = rowwise(
        "gdn_pre_bwd", gdn_pre_bwd, t_len, rt, pre_in + cts_in, pre_full,
        [(r, WIDTH, _BF)] * 3 + [(r, LANE, _BF)],
        [(CONV_W, WIDTH)] * 3 + [(1, LANE)] * 2, carries=[(SUBLANE, WIDTH)] * 3, reverse=True)

    dq_full, dk_full, dv_mla = flash_bwd(q_full, k_full, v_mla, do_mla, lse, delta)
    head_cts = [(a, ("h",)) for a in (dq_full, dk_full, dv_mla)]

    def head_bwd(i, q_raw_, kv_raw_, kr_, pos_, dq_, dk_, dv_, qg_, kg_, invf_):
        return _vjp_fn(f_head, 5, 3)(i, q_raw_, kv_raw_, kr_, qg_, kg_, pos_, invf_, dq_, dk_, dv_)

    dq_raw, dkv_raw, d_kr, g_qg, g_kg = rowwise(
        "head_bwd", head_bwd, t_len, rt, head_in + [pos_in] + head_cts, [qg, kg, invf],
        [(r, wide, _BF), (r, wide, _BF), (r, LANE, _BF)], [(1, HEAD_PAD), (1, HEAD_PAD)])
    dq_lat = matmul("dq_lat", dq_raw, w_uq, "nt")
    g_w_uq = matmul("g_w_uq", q_lat, dq_raw, "tn")
    dkv_lat = matmul("dkv_lat", dkv_raw, w_ukv, "nt")
    g_w_ukv = matmul("g_w_ukv", kv_lat, dkv_raw, "tn")

    def lat_bwd(i, cq_, ckv_, dql_, dkl_, gq_, gkv_):
        return _vjp_fn(f_lat, 4, 2)(i, cq_, ckv_, gq_, gkv_, dql_, dkl_)

    d_cq, d_ckv, g_qa, g_kva = rowwise(
        "lat_bwd", lat_bwd, t_len, rt, [cq_in, ckv_in, (dq_lat, (r, Q_LORA, 0)), (dkv_lat, (r, KV_LORA, 0))],
        [qa_gain, kva_gain], [(r, Q_LORA, _BF), (r, KV_LORA, _BF)], [(1, Q_LORA), (1, KV_LORA)])

    d_proj = jnp.concatenate([d_mgate, d_gq, d_gk, d_gv, d_ggate, d_cq, d_ckv, d_kr, d_gab], axis=1)
    g_w_in = matmul("g_w_in", xn, d_proj, "tn")
    grads = {
        "w_in": g_w_in, "mla_q_a_gain": g_qa, "mla_kv_a_gain": g_kva, "w_uq": g_w_uq,
        "w_ukv": g_w_ukv, "mla_q_norm_gain": g_qg[:, :QK], "mla_k_norm_gain": g_kg[:, :QK],
        "gdn_conv_w": jnp.concatenate([g_cwq, g_cwk, g_cwv], axis=1), "gdn_a_log": g_alog[:, :HEADS],
        "gdn_dt_bias": g_dtb[:, :HEADS], "gdn_out_norm_gain": g_og, "w_out": g_w_out,
    }
    after = None if on_weight_grads is None else on_weight_grads(grads)
    d_xn = matmul("d_xn", d_proj, w_in, "nt", after=after)
    after = jnp.zeros((SUBLANE, LANE), F32) if on_d_xn is None else on_d_xn(d_xn)

    def rms_x_bwd(i, x_, dxn_, dy_, gain_, after_):
        dx, dgain = _vjp_fn(f_rms_x, 2, 1)(i, x_, gain_, dxn_)
        return dx + dy_, dgain

    grad_x, grads["norm_gain"] = rowwise(
        "rms_x_bwd", rms_x_bwd, t_len, rt, [(x, (r, D_MODEL, 0)), (d_xn, (r, D_MODEL, 0)), (dy, (r, D_MODEL, 0))],
        [norm_gain, after], [(r, D_MODEL, F32)], [(1, D_MODEL)])
    return loss, grad_x, grads


MESH = pl.DeviceIdType.MESH
ANY = pl.BlockSpec(memory_space=pl.ANY)
CHIP_FLIPS = ((1, 0), (0, 1), (1, 1))


def _place():
    return lax.axis_index("x"), lax.axis_index("y"), lax.axis_index("c")


def _flip(v, f):
    return 1 - v if f else v


def all_gather(shards):
    n_arr = len(shards)

    def body(*refs):
        x_refs, o_refs = refs[:n_arr], refs[n_arr:2 * n_arr]
        send_sems, recv_sems, local_sems = refs[2 * n_arr:]
        x, y, c = _place()
        me, sibling = (x, y, c), (x, y, 1 - c)
        chips = [(_flip(x, fx), _flip(y, fy)) for fx, fy in CHIP_FLIPS]

        def copy(a, k, block, to, src=None):
            px, py, pc = block
            dst = o_refs[a].at[4 * px + 2 * py + pc]
            return pltpu.make_async_remote_copy(
                src_ref=dst if src is None else src, dst_ref=dst, send_sem=send_sems.at[a, k],
                recv_sem=recv_sems.at[a, k], device_id=to, device_id_type=MESH)

        mine, first, passed = [], [], []
        for a in range(n_arr):
            cp = pltpu.make_async_copy(x_refs[a], o_refs[a].at[4 * x + 2 * y + c], local_sems.at[a])
            cp.start()
            mine.append(cp)
            first.append(copy(a, 0, me, sibling, src=x_refs[a]))
            first += [copy(a, 1 + j, me, (*chip, c), src=x_refs[a]) for j, chip in enumerate(chips)]
        for cp in first:
            cp.start()
        for j, chip in enumerate(chips):
            for a in range(n_arr):
                copy(a, 1 + j, (*chip, c), me).wait_recv()
                cp = copy(a, 4 + j, (*chip, c), sibling)
                cp.start()
                passed.append(cp)
        for a in range(n_arr):
            copy(a, 0, sibling, me).wait_recv()
            for j, chip in enumerate(chips):
                copy(a, 4 + j, (*chip, 1 - c), me).wait_recv()
        for cp in first + passed:
            cp.wait_send()
        for cp in mine:
            cp.wait()

    return pl.pallas_call(
        body,
        name="all_gather",
        out_shape=[jax.ShapeDtypeStruct((N_DEV,) + s.shape, s.dtype) for s in shards],
        in_specs=[ANY] * n_arr,
        out_specs=[ANY] * n_arr,
        scratch_shapes=[pltpu.SemaphoreType.DMA((n_arr, 7)), pltpu.SemaphoreType.DMA((n_arr, 7)),
                        pltpu.SemaphoreType.DMA((n_arr,))],
    )(*shards)


HBM = pl.BlockSpec(memory_space=pltpu.HBM)
SEMS = pl.BlockSpec(memory_space=pltpu.SEMAPHORE)
SIDE_EFFECT = pltpu.SideEffectType.DATAFLOW_SIDE_EFFECTING


def core_routes(x, y, c):
    return [(2 * q + (1 - c), q, (x, y, 1 - c)) for q in range(4)]


def chip_routes(x, y, c):
    routes = []
    for j, (fx, fy) in enumerate(CHIP_FLIPS):
        px, py = _flip(x, fx), _flip(y, fy)
        routes.append((2 * px + py, j, (px, py, c)))
    return routes


def _route_copies(routes, n_routes, src_refs, land_refs, sems):
    x, y, c = _place()
    n_copies = len(src_refs) * n_routes
    return [pltpu.make_async_remote_copy(src_ref=src.at[s], dst_ref=land.at[d], send_sem=sems[a * n_routes + k],
                                         recv_sem=sems[n_copies + a * n_routes + k], device_id=dev,
                                         device_id_type=MESH)
            for a, (src, land) in enumerate(zip(src_refs, land_refs)) for k, (s, d, dev) in enumerate(routes(x, y, c))]


def exchange_start(name, routes, n_routes, srcs):
    n = len(srcs)
    n_sems = 2 * n * n_routes
    lands = [lax.empty((n_routes,) + s.shape[1:], s.dtype) for s in srcs]

    def body(*refs):
        for cp in _route_copies(routes, n_routes, refs[:n], refs[n:2 * n], refs[2 * n:2 * n + n_sems]):
            cp.start()
        refs[-1][...] = jnp.zeros_like(refs[-1])

    res = pl.pallas_call(
        body,
        name=name,
        out_shape=(*[pltpu.SemaphoreType.DMA(())] * n_sems, *[pltpu.HBM(a.shape, a.dtype) for a in srcs + lands],
                   jax.ShapeDtypeStruct((SUBLANE, LANE), F32)),
        in_specs=[HBM] * (2 * n),
        out_specs=(*[SEMS] * n_sems, *[HBM] * (2 * n), pl.BlockSpec(memory_space=pltpu.VMEM)),
        input_output_aliases={i: n_sems + i for i in range(2 * n)},
        compiler_params=pltpu.CompilerParams(has_side_effects=SIDE_EFFECT),
    )(*[pltpu.with_memory_space_constraint(a, pltpu.HBM) for a in srcs + lands])
    return (res[:n_sems], res[n_sems:-1]), res[-1]


def exchange_wait(name, routes, handle, after):
    sems, thru = handle
    n, n_sems = len(thru) // 2, len(sems)
    n_routes = n_sems // (2 * n)

    def body(*refs):
        for cp in _route_copies(routes, n_routes, refs[:n], refs[n:2 * n], refs[2 * n:2 * n + n_sems]):
            cp.wait_send()
            cp.wait_recv()

    res = pl.pallas_call(
        body,
        name=name,
        out_shape=tuple(pltpu.HBM(a.shape, a.dtype) for a in thru),
        in_specs=[HBM] * (2 * n) + [SEMS] * n_sems + [ANY],
        out_specs=tuple([HBM] * (2 * n)),
        input_output_aliases={i: i for i in range(2 * n)},
        compiler_params=pltpu.CompilerParams(has_side_effects=SIDE_EFFECT),
    )(*thru, *sems, after)
    return list(res[:n]), list(res[n:])


def gather_small(v):
    def body(v_ref, o_ref, send_sems, recv_sems, local_sem):
        x, y, c = _place()
        me = 4 * x + 2 * y + c
        mine = pltpu.make_async_copy(v_ref, o_ref.at[me], local_sem)
        mine.start()
        copies = []
        for k in range(1, N_DEV):
            fx, fy, fc = (k >> 2) & 1, (k >> 1) & 1, k & 1
            cp = pltpu.make_async_remote_copy(
                src_ref=v_ref, dst_ref=o_ref.at[me], send_sem=send_sems.at[k - 1], recv_sem=recv_sems.at[k - 1],
                device_id=(_flip(x, fx), _flip(y, fy), _flip(c, fc)), device_id_type=MESH)
            cp.start()
            copies.append(cp)
        for cp in copies:
            cp.wait()
        mine.wait()

    return pl.pallas_call(
        body,
        name="gather_small",
        out_shape=jax.ShapeDtypeStruct((N_DEV,) + v.shape, v.dtype),
        in_specs=[ANY],
        out_specs=ANY,
        scratch_shapes=[pltpu.SemaphoreType.DMA((N_DEV - 1,)), pltpu.SemaphoreType.DMA((N_DEV - 1,)),
                        pltpu.SemaphoreType.DMA],
    )(v)


def _row_tile(rows):
    for t in (256, 128, 64, 32, 16, 8):
        if rows % t == 0:
            return t
    return rows


def add_core_parts(name, g, recv, c_idx, wire):
    _, rows, cols = g.shape
    tr = _row_tile(rows)

    def body(c_ref, g_ref, r_ref, o_ref, w_ref):
        part = g_ref[...] + r_ref[...]
        o_ref[...] = part
        w_ref[...] = part.astype(w_ref.dtype)

    blk = pl.BlockSpec((1, tr, cols), lambda q, i, c_ref: (q, i, 0))
    return pl.pallas_call(
        body,
        name=name,
        grid_spec=pltpu.PrefetchScalarGridSpec(
            num_scalar_prefetch=1,
            grid=(4, rows // tr),
            in_specs=[pl.BlockSpec((1, tr, cols), lambda q, i, c_ref: (2 * q + c_ref[0], i, 0)), blk],
            out_specs=[blk, blk],
        ),
        out_shape=[jax.ShapeDtypeStruct((4, rows, cols), F32), jax.ShapeDtypeStruct((4, rows, cols), wire)],
        compiler_params=_cparams("parallel", "parallel"),
    )(c_idx, g, recv)


def _adamw(w, g, m, v):
    m = ADAM_B1 * m + (1.0 - ADAM_B1) * g
    v = ADAM_B2 * v + (1.0 - ADAM_B2) * (g * g)
    m_hat = m / (1.0 - ADAM_B1 ** ADAM_STEP)
    v_hat = v / (1.0 - ADAM_B2 ** ADAM_STEP)
    delta = -ADAM_LR * (m_hat / (jnp.sqrt(v_hat) + ADAM_EPS) + ADAM_WD * w)
    return delta, m, v


def adamw_sharded(name, parts, recv, q_idx, w, m, v):
    rows, cols = w.shape
    tr = _row_tile(rows)

    def body(q_ref, p_ref, r_ref, w_ref, m_ref, v_ref, g_out, d_out, m_out, v_out):
        g = p_ref[0] + r_ref[0].astype(F32) + r_ref[1].astype(F32) + r_ref[2].astype(F32)
        d, m_new, v_new = _adamw(w_ref[...], g, m_ref[...], v_ref[...])
        g_out[...], d_out[...], m_out[...], v_out[...] = g, d, m_new, v_new

    blk = pl.BlockSpec((tr, cols), lambda i, q_ref: (i, 0))
    return pl.pallas_call(
        body,
        name=name,
        grid_spec=pltpu.PrefetchScalarGridSpec(
            num_scalar_prefetch=1,
            grid=(rows // tr,),
            in_specs=[pl.BlockSpec((1, tr, cols), lambda i, q_ref: (q_ref[0], i, 0)),
                      pl.BlockSpec((3, tr, cols), lambda i, q_ref: (0, i, 0)), blk, blk, blk],
            out_specs=[blk] * 4,
        ),
        out_shape=[jax.ShapeDtypeStruct((rows, cols), F32)] * 4,
        compiler_params=_cparams("parallel"),
    )(q_idx, parts, recv, w, m, v)


def adamw_small(gathered, w, m, v):
    def body(g_ref, w_ref, m_ref, v_ref, g_out, d_out, m_out, v_out):
        g = g_ref[0]
        for j in range(1, N_DEV):
            g = g + g_ref[j]
        d, m_new, v_new = _adamw(w_ref[...], g, m_ref[...], v_ref[...])
        g_out[...], d_out[...], m_out[...], v_out[...] = g, d, m_new, v_new

    return pl.pallas_call(body, name="adamw_small", out_shape=[jax.ShapeDtypeStruct(w.shape, F32)] * 4)(gathered, w, m, v)


SHARDED = ("w_in", "w_uq", "w_ukv", "gdn_conv_w", "w_out")
SMALL = (("norm_gain", D_MODEL), ("mla_q_a_gain", Q_LORA), ("mla_kv_a_gain", KV_LORA), ("mla_q_norm_gain", QK),
         ("mla_k_norm_gain", QK), ("gdn_a_log", HEADS), ("gdn_dt_bias", HEADS), ("gdn_out_norm_gain", GDN_DIM))
WEIGHT_ORDER = ("norm_gain", "w_in", "mla_q_a_gain", "mla_kv_a_gain", "w_uq", "w_ukv", "mla_q_norm_gain",
                "mla_k_norm_gain", "gdn_conv_w", "gdn_a_log", "gdn_dt_bias", "gdn_out_norm_gain", "w_out")


def _pack_small(d):
    rows = []
    for name, n in SMALL:
        a = d[name].reshape(-1).astype(F32)
        n_pad = -(-n // LANE) * LANE
        rows.append(jnp.pad(a, (0, n_pad - n)).reshape(n_pad // LANE, LANE))
    packed = jnp.concatenate(rows, axis=0)
    return jnp.pad(packed, ((0, -packed.shape[0] % SUBLANE), (0, 0)))


def _unpack_small(packed):
    out, row = {}, 0
    for name, n in SMALL:
        n_rows = -(-n // LANE)
        out[name] = packed[row:row + n_rows].reshape(-1)[:n].reshape(1, n)
        row += n_rows
    return out


def kernel(x, positions, norm_gain, w_in, mla_q_a_gain, mla_kv_a_gain, w_uq, w_ukv, mla_q_norm_gain, mla_k_norm_gain, gdn_conv_w, gdn_a_log, gdn_dt_bias, gdn_out_norm_gain, w_out, loss_target, m_norm_gain, m_w_in, m_mla_q_a_gain, m_mla_kv_a_gain, m_w_uq, m_w_ukv, m_mla_q_norm_gain, m_mla_k_norm_gain, m_gdn_conv_w, m_gdn_a_log, m_gdn_dt_bias, m_gdn_out_norm_gain, m_w_out, v_norm_gain, v_w_in, v_mla_q_a_gain, v_mla_kv_a_gain, v_w_uq, v_w_ukv, v_mla_q_norm_gain, v_mla_k_norm_gain, v_gdn_conv_w, v_gdn_a_log, v_gdn_dt_bias, v_gdn_out_norm_gain, v_w_out):
    w = dict(norm_gain=norm_gain, w_in=w_in, mla_q_a_gain=mla_q_a_gain, mla_kv_a_gain=mla_kv_a_gain, w_uq=w_uq,
             w_ukv=w_ukv, mla_q_norm_gain=mla_q_norm_gain, mla_k_norm_gain=mla_k_norm_gain, gdn_conv_w=gdn_conv_w,
             gdn_a_log=gdn_a_log, gdn_dt_bias=gdn_dt_bias, gdn_out_norm_gain=gdn_out_norm_gain, w_out=w_out)
    m = dict(norm_gain=m_norm_gain, w_in=m_w_in, mla_q_a_gain=m_mla_q_a_gain, mla_kv_a_gain=m_mla_kv_a_gain,
             w_uq=m_w_uq, w_ukv=m_w_ukv, mla_q_norm_gain=m_mla_q_norm_gain, mla_k_norm_gain=m_mla_k_norm_gain,
             gdn_conv_w=m_gdn_conv_w, gdn_a_log=m_gdn_a_log, gdn_dt_bias=m_gdn_dt_bias,
             gdn_out_norm_gain=m_gdn_out_norm_gain, w_out=m_w_out)
    v = dict(norm_gain=v_norm_gain, w_in=v_w_in, mla_q_a_gain=v_mla_q_a_gain, mla_kv_a_gain=v_mla_kv_a_gain,
             w_uq=v_w_uq, w_ukv=v_w_ukv, mla_q_norm_gain=v_mla_q_norm_gain, mla_k_norm_gain=v_mla_k_norm_gain,
             gdn_conv_w=v_gdn_conv_w, gdn_a_log=v_gdn_a_log, gdn_dt_bias=v_gdn_dt_bias,
             gdn_out_norm_gain=v_gdn_out_norm_gain, w_out=v_w_out)
    t_len = x.shape[1]

    shards = [w[n][0] if n == "gdn_conv_w" else w[n][0].astype(_BF) for n in SHARDED]
    a_w_in, a_w_uq, a_w_ukv, a_cw, a_w_out = all_gather(shards)

    def cols_whole(g):
        return g.transpose(1, 0, 2).reshape(g.shape[1], N_DEV * g.shape[2])

    p = {n: w[n] for n, _ in SMALL}
    p["w_in"] = arrange_w_in(cols_whole(a_w_in))
    p["w_uq"] = arrange_w_uq(cols_whole(a_w_uq))
    p["w_ukv"] = cols_whole(a_w_ukv)
    p["gdn_conv_w"] = cols_whole(a_cw)
    p["w_out"] = a_w_out.reshape(N_DEV * a_w_out.shape[1], a_w_out.shape[2])

    def col_blocks(g):
        return g.reshape(g.shape[0], N_DEV, g.shape[1] // N_DEV).transpose(1, 0, 2)

    xi, yi, ci = _place()
    c_idx = jnp.reshape(ci, (1,)).astype(jnp.int32)
    q_idx = jnp.reshape(2 * xi + yi, (1,)).astype(jnp.int32)
    flight = {}

    def start_core_exchange(grads):
        blocks = [col_blocks(unarrange_w_in(grads["w_in"])), col_blocks(unarrange_w_uq(grads["w_uq"])),
                  col_blocks(grads["w_ukv"]), col_blocks(grads["gdn_conv_w"]),
                  grads["w_out"].reshape(N_DEV, D_MODEL // N_DEV, D_MODEL)]
        flight["cores"], token = exchange_start("cores_start", core_routes, 4, blocks)
        return token

    def start_chip_exchange(d_xn):
        blocks, landed = exchange_wait("cores_wait", core_routes, flight["cores"], d_xn)
        flight["parts"] = [add_core_parts("add_" + n, g, r, c_idx, F32 if n == "gdn_conv_w" else _BF)
                           for n, g, r in zip(SHARDED, blocks, landed)]
        flight["chips"], token = exchange_start("chips_start", chip_routes, 3, [wire for _, wire in flight["parts"]])
        return token

    pos = positions.reshape(t_len, 1).astype(F32)
    loss, grad_x, grads = local_step(x.reshape(t_len, D_MODEL), pos, loss_target.reshape(t_len, D_MODEL), p,
                                     on_weight_grads=start_core_exchange, on_d_xn=start_chip_exchange)
    loss = lax.psum(loss, ("x", "y", "c"))
    out = {}
    small_all = gather_small(_pack_small(grads))
    res = adamw_small(small_all, _pack_small(w), _pack_small(m), _pack_small(v))
    unpacked = [_unpack_small(a) for a in res]
    for n, _ in SMALL:
        out[n] = [u[n] for u in unpacked]

    _, from_chips = exchange_wait("chips_wait", chip_routes, flight["chips"], res[0])
    for n, (prt, _), rcv in zip(SHARDED, flight["parts"], from_chips):
        shape = w[n].shape
        res = adamw_sharded("adamw_" + n, prt, rcv, q_idx, w[n].reshape(shape[-2:]), m[n].reshape(shape[-2:]),
                            v[n].reshape(shape[-2:]))
        out[n] = [a.reshape(shape) for a in res]

    return (loss, grad_x.reshape(x.shape), *[out[n][0] for n in WEIGHT_ORDER], *[out[n][1] for n in WEIGHT_ORDER],
            *[out[n][2] for n in WEIGHT_ORDER], *[out[n][3] for n in WEIGHT_ORDER])
```

```python
import functools

import jax
import jax.numpy as jnp
from jax import lax
from jax.experimental import pallas as pl
from jax.experimental.pallas import tpu as pltpu

F32 = jnp.float32
_BF = jnp.bfloat16
HI = lax.Precision.HIGHEST

D_MODEL = 2048
HEADS = 8
NOPE = 128
ROPE = 64
QK = NOPE + ROPE
Q_LORA = 512
KV_LORA = 256
HEAD_PAD = 256
GDN_DIM = 128
WIDTH = HEADS * 128
CONV_W = 4
CHUNK = 64
ROPE_THETA = 10000.0
EPS = 1e-6
N_DEV = 8
LANE = 128
SUBLANE = 8
VMEM_LIMIT = 48 * 1024 * 1024

ADAM_LR, ADAM_B1, ADAM_B2, ADAM_EPS, ADAM_WD, ADAM_STEP = 0.001, 0.9, 0.999, 1e-08, 0.01, 10

P_MGATE, P_GQ, P_GK, P_GV, P_GGATE = 0, 1024, 2048, 3072, 4096
P_CQ, P_CKV, P_KR, P_GAB = 5120, 5632, 5888, 6016
P_COLS = 6144
R_SPLITS = (512, 256, 64, 1024, 1024, 1024, 1024, 8, 8, 1024)


def _cparams(*sem):
    return pltpu.CompilerParams(dimension_semantics=sem, vmem_limit_bytes=VMEM_LIMIT)


def _d_nn(a, b):
    return jnp.dot(a.astype(_BF), b.astype(_BF), preferred_element_type=F32)


def _d_nt(a, b):
    return lax.dot_general(a.astype(_BF), b.astype(_BF), (((1,), (1,)), ((), ())), preferred_element_type=F32)


def _d_tn(a, b):
    return lax.dot_general(a.astype(_BF), b.astype(_BF), (((0,), (0,)), ((), ())), preferred_element_type=F32)


@jax.custom_vjp
def _mm(a, b):
    return _d_nn(a, b)


_mm.defvjp(lambda a, b: (_d_nn(a, b), (a, b)), lambda r, g: (_d_nt(g, r[1]), _d_tn(r[0], g)))


@jax.custom_vjp
def _mm_nt(a, b):
    return _d_nt(a, b)


_mm_nt.defvjp(lambda a, b: (_d_nt(a, b), (a, b)), lambda r, g: (_d_nn(g, r[1]), _d_tn(g, r[0])))


@jax.custom_vjp
def _mm_tn(a, b):
    return _d_tn(a, b)


_mm_tn.defvjp(lambda a, b: (_d_tn(a, b), (a, b)), lambda r, g: (_d_nt(r[1], g), _d_nn(r[0], g)))


def _hi(a, b):
    return jnp.dot(a, b, preferred_element_type=F32, precision=HI)


_NN3 = (((2,), (1,)), ((0,), (0,)))
_NT3 = (((2,), (2,)), ((0,), (0,)))
_TN3 = (((1,), (1,)), ((0,), (0,)))


def _bdot(a, b, dims, hi):
    if hi:
        return lax.dot_general(a, b, dims, preferred_element_type=F32, precision=hi)
    return lax.dot_general(a.astype(_BF), b.astype(_BF), dims, preferred_element_type=F32)


def _batched_matmuls(hi):
    nn = jax.custom_vjp(lambda a, b: _bdot(a, b, _NN3, hi))
    nt = jax.custom_vjp(lambda a, b: _bdot(a, b, _NT3, hi))
    tn = jax.custom_vjp(lambda a, b: _bdot(a, b, _TN3, hi))
    nn.defvjp(lambda a, b: (_bdot(a, b, _NN3, hi), (a, b)),
              lambda r, g: (_bdot(g, r[1], _NT3, hi), _bdot(r[0], g, _TN3, hi)))
    nt.defvjp(lambda a, b: (_bdot(a, b, _NT3, hi), (a, b)),
              lambda r, g: (_bdot(g, r[1], _NN3, hi), _bdot(g, r[0], _TN3, hi)))
    tn.defvjp(lambda a, b: (_bdot(a, b, _TN3, hi), (a, b)),
              lambda r, g: (_bdot(r[1], g, _NT3, hi), _bdot(r[0], g, _NN3, hi)))
    return nn, nt, tn


_bmm, _bmm_nt, _bmm_tn = _batched_matmuls(False)
_bhi, _bhi_nt, _bhi_tn = _batched_matmuls(HI)


def _split2(x):
    hi = x.astype(_BF)
    return hi, (x - hi.astype(F32)).astype(_BF)


def _pdot(a, b, mode):
    (a_hi, a_lo), (b_hi, b_lo) = _split2(a), _split2(b)
    a_ax, b_ax, dims = {"nn": (2, 1, _NN3), "nt": (2, 2, _NT3), "tn": (1, 1, _TN3)}[mode]
    lhs = jnp.concatenate([a_hi, a_lo, a_hi], axis=a_ax)
    rhs = jnp.concatenate([b_hi, b_hi, b_lo], axis=b_ax)
    return lax.dot_general(lhs, rhs, dims, preferred_element_type=F32)


def _packed_matmuls():
    nn = jax.custom_vjp(lambda a, b: _pdot(a, b, "nn"))
    nn.defvjp(lambda a, b: (_pdot(a, b, "nn"), (a, b)), lambda r, g: (_pdot(g, r[1], "nt"), _pdot(r[0], g, "tn")))
    return nn


_bh3 = _packed_matmuls()
_bh3_passes = _batched_matmuls(lax.Precision.HIGH)[0]


@functools.partial(jax.custom_vjp, nondiff_argnums=(1, 2))
def _roll(x, shift, axis):
    return pltpu.roll(x, shift, axis)


def _roll_fwd(x, shift, axis):
    return pltpu.roll(x, shift, axis), None


def _roll_bwd(shift, axis, _, g):
    n = g.shape[axis]
    return (pltpu.roll(g, (n - shift) % n, axis),)


_roll.defvjp(_roll_fwd, _roll_bwd)


def _rms(x, gain):
    return x * lax.rsqrt(jnp.mean(x * x, axis=-1, keepdims=True) + EPS) * gain


MM_TILE = 1024
MM_DEPTH = 2048


def matmul(name, a, b, mode, after=None):
    if mode == "nn":
        (m, k), (k2, n) = a.shape, b.shape
    elif mode == "nt":
        (m, k), (n, k2) = a.shape, b.shape
    else:
        (k, m), (k2, n) = a.shape, b.shape
    assert k == k2, (name, a.shape, b.shape)
    tm, tn, tk = min(MM_TILE, m), min(MM_TILE, n), min(MM_DEPTH, k)
    assert m % tm == 0 and n % tn == 0 and k % tk == 0, (name, m, n, k)
    dot = {"nn": _d_nn, "nt": _d_nt, "tn": _d_tn}[mode]

    def body(a_ref, b_ref, *rest):
        o_ref = rest[-1]
        kk = pl.program_id(2)
        part = dot(a_ref[...], b_ref[...])

        @pl.when(kk == 0)
        def _():
            o_ref[...] = part

        @pl.when(kk != 0)
        def _():
            o_ref[...] += part

    if mode == "nn":
        a_spec = pl.BlockSpec((tm, tk), lambda j, i, kk: (i, kk))
        b_spec = pl.BlockSpec((tk, tn), lambda j, i, kk: (kk, j))
    elif mode == "nt":
        a_spec = pl.BlockSpec((tm, tk), lambda j, i, kk: (i, kk))
        b_spec = pl.BlockSpec((tn, tk), lambda j, i, kk: (j, kk))
    else:
        a_spec = pl.BlockSpec((tk, tm), lambda j, i, kk: (kk, i))
        b_spec = pl.BlockSpec((tk, tn), lambda j, i, kk: (kk, j))
    return pl.pallas_call(
        body,
        name=name,
        grid=(n // tn, m // tm, k // tk),
        in_specs=[a_spec, b_spec] + ([] if after is None else [pl.BlockSpec(memory_space=pl.ANY)]),
        out_specs=pl.BlockSpec((tm, tn), lambda j, i, kk: (i, j)),
        out_shape=jax.ShapeDtypeStruct((m, n), F32),
        compiler_params=_cparams("parallel", "parallel", "arbitrary"),
    )(*((a, b) if after is None else (a, b, after)))


def rowwise(name, fn, t_len, tile, row_in, full_in, row_out, acc_out=(), carries=(), reverse=False):
    tile = min(tile, t_len)
    n = t_len // tile
    assert t_len % tile == 0 and tile % SUBLANE == 0
    n_in, n_ro, n_acc, n_car = len(row_in) + len(full_in), len(row_out), len(acc_out), len(carries)

    def ti(i):
        return (n - 1 - i) if reverse else i

    in_specs, args = [], []
    for arr, kind in row_in:
        if kind[0] == "r":
            in_specs.append(pl.BlockSpec((tile, kind[1]), lambda i, c=kind[2]: (ti(i), c)))
        elif kind[0] == "h":
            in_specs.append(pl.BlockSpec((arr.shape[0], tile, arr.shape[2]), lambda i: (0, ti(i), 0)))
        else:
            in_specs.append(pl.BlockSpec(
                (SUBLANE, kind[1]), lambda i, c=kind[2]: (jnp.maximum(ti(i) * (tile // SUBLANE) - 1, 0), c)))
        args.append(arr)
    for arr in full_in:
        in_specs.append(pl.BlockSpec(arr.shape, lambda i, nd=arr.ndim: (0,) * nd))
        args.append(arr)
    out_specs, out_shape = [], []
    for kind in row_out:
        if kind[0] == "r":
            out_specs.append(pl.BlockSpec((tile, kind[1]), lambda i: (ti(i), 0)))
            out_shape.append(jax.ShapeDtypeStruct((t_len, kind[1]), kind[2]))
        else:
            out_specs.append(pl.BlockSpec((kind[1], tile, kind[2]), lambda i: (0, ti(i), 0)))
            out_shape.append(jax.ShapeDtypeStruct((kind[1], t_len, kind[2]), kind[3]))
    for shp in acc_out:
        out_specs.append(pl.BlockSpec(shp, lambda i, nd=len(shp): (0,) * nd))
        out_shape.append(jax.ShapeDtypeStruct(shp, F32))

    def body(*refs):
        in_refs = refs[:n_in]
        ro_refs = refs[n_in:n_in + n_ro]
        acc_refs = refs[n_in + n_ro:n_in + n_ro + n_acc]
        car_refs = refs[n_in + n_ro + n_acc:]
        step = pl.program_id(0)
        if n_car:
            @pl.when(step == 0)
            def _():
                for r in car_refs:
                    r[...] = jnp.zeros_like(r)
        vals = [r[...].astype(F32) for r in in_refs] + [r[...] for r in car_refs]
        outs = fn(ti(step), *vals)
        assert len(outs) == n_ro + n_acc + n_car, (name, len(outs))
        for r, o in zip(ro_refs, outs[:n_ro]):
            r[...] = o.astype(r.dtype)
        for r, o in zip(acc_refs, outs[n_ro:n_ro + n_acc]):
            @pl.when(step == 0)
            def _(r=r, o=o):
                r[...] = o

            @pl.when(step != 0)
            def _(r=r, o=o):
                r[...] += o
        for r, o in zip(car_refs, outs[n_ro + n_acc:]):
            r[...] = o

    res = pl.pallas_call(
        body,
        name=name,
        grid=(n,),
        in_specs=in_specs,
        out_specs=out_specs,
        out_shape=out_shape,
        scratch_shapes=[pltpu.VMEM(s, F32) for s in carries],
        compiler_params=_cparams("arbitrary"),
    )(*args)
    return list(res)


def _vjp_fn(fn, n_diff, n_out):
    def g(i, *a):
        ins, cts = a[:len(a) - n_out], a[len(a) - n_out:]
        diff, rest = ins[:n_diff], ins[n_diff:]
        _, pull = jax.vjp(lambda *d: tuple(fn(i, *d, *rest)), *diff)
        return tuple(pull(tuple(cts)))

    return g


def f_rms_x(i, x, gain):
    return (_rms(x, gain),)


def f_lat(i, cq, ckv, gq, gkv):
    return _rms(cq, gq), _rms(ckv, gkv)


def _rope_tables(pos, invf):
    ang = pos * invf
    lane = lax.broadcasted_iota(jnp.int32, (1, LANE), 1)
    cosv, sinv = jnp.cos(ang), jnp.sin(ang)
    half = ROPE // 2
    c = jnp.where(lane < ROPE, cosv, 0.0)
    sa = jnp.where(lane < half, -sinv, 0.0)
    sb = jnp.where((lane >= half) & (lane < ROPE), sinv, 0.0)
    return c, sa, sb


def _rope(xh, tabs):
    c, sa, sb = tabs
    half = ROPE // 2
    return xh * c + _roll(xh, LANE - half, 1) * sa + _roll(xh, half, 1) * sb


def f_head(i, q_raw, kv_raw, kr, qg, kg, pos, invf):
    tabs = _rope_tables(pos, invf)
    qs, ks, vs = [], [], []
    kr_ss = jnp.sum(kr * kr, axis=-1, keepdims=True)
    for h in range(HEADS):
        lo = q_raw[:, HEAD_PAD * h:HEAD_PAD * h + NOPE]
        hi = q_raw[:, HEAD_PAD * h + NOPE:HEAD_PAD * (h + 1)]
        ss = jnp.sum(lo * lo, axis=-1, keepdims=True) + jnp.sum(hi * hi, axis=-1, keepdims=True)
        r = lax.rsqrt(ss * (1.0 / QK) + EPS)
        qs.append(jnp.concatenate([lo * r * qg[:, :NOPE], _rope(hi * r * qg[:, NOPE:], tabs)], axis=1))
        lo = kv_raw[:, 2 * NOPE * h:2 * NOPE * h + NOPE]
        ss = jnp.sum(lo * lo, axis=-1, keepdims=True) + kr_ss
        r = lax.rsqrt(ss * (1.0 / QK) + EPS)
        ks.append(jnp.concatenate([lo * r * kg[:, :NOPE], _rope(kr * r * kg[:, NOPE:], tabs)], axis=1))
        vs.append(kv_raw[:, 2 * NOPE * h + NOPE:2 * NOPE * (h + 1)])
    return jnp.stack(qs), jnp.stack(ks), jnp.stack(vs)


def f_mix(i, o_mla, mgate, o_gdn, ggate, og):
    parts = [o_mla * jax.nn.silu(mgate)]
    for h in range(HEADS):
        parts.append(_rms(o_gdn[h], og) * jax.nn.silu(ggate[:, LANE * h:LANE * (h + 1)]))
    return (jnp.concatenate(parts, axis=1),)


def _row(a, j):
    rows = lax.broadcasted_iota(jnp.int32, a.shape, 0)
    return jnp.sum(jnp.where(rows == j, a, 0.0), axis=0, keepdims=True)


def _shift_rows(x, halo, d):
    xs = _roll(x, d, 0)
    hs = _roll(halo, d, 0)
    r8 = lax.broadcasted_iota(jnp.int32, hs.shape, 0)
    top = jnp.where(r8 < d, hs, xs[:SUBLANE])
    return jnp.concatenate([top, xs[SUBLANE:]], axis=0)


def _conv_silu(x, halo, w):
    y = _row(w, CONV_W - 1) * x
    for j in range(CONV_W - 1):
        y = y + _row(w, j) * _shift_rows(x, halo, CONV_W - 1 - j)
    return jax.nn.silu(y)


def _head_select(offset):
    r = lax.broadcasted_iota(jnp.int32, (LANE, WIDTH), 0)
    c = lax.broadcasted_iota(jnp.int32, (LANE, WIDTH), 1)
    return (r == offset + lax.shift_right_logical(c, 7)).astype(_BF)


def _split3(x):
    x1 = x.astype(_BF)
    r1 = x - x1.astype(F32)
    x2 = r1.astype(_BF)
    return x1, x2, (r1 - x2.astype(F32)).astype(_BF)


@jax.custom_vjp
def _spread(x, sel):
    return _d_nn(jnp.concatenate(_split3(x), axis=1), jnp.concatenate([sel, sel, sel], axis=0))


def _spread_fwd(x, sel):
    return _spread(x, sel), sel


def _spread_bwd(sel, g):
    g1, g2, g3 = _split3(g)
    return _d_nt(g1, sel) + _d_nt(g2, sel) + _d_nt(g3, sel), jnp.zeros_like(sel)


_spread.defvjp(_spread_fwd, _spread_bwd)


def f_gdn_pre(i, gq, gk, gv, gab, hq, hk, hv, cwq, cwk, cwv, alog, dtb):
    live = jnp.where(i == 0, 0.0, 1.0)
    q = _conv_silu(gq, hq * live, cwq)
    k = _conv_silu(gk, hk * live, cwk)
    v = _conv_silu(gv, hv * live, cwv)
    g = _spread(-jnp.exp(alog) * jax.nn.softplus(gab + dtb), _head_select(0))
    beta = _spread(jax.nn.sigmoid(gab), _head_select(HEADS))
    qs, ks, vs, gs, bs = [], [], [], [], []
    for h in range(HEADS):
        sl = slice(LANE * h, LANE * (h + 1))
        qh, kh = q[:, sl], k[:, sl]
        qs.append(qh * lax.rsqrt(jnp.sum(qh * qh, axis=-1, keepdims=True) + EPS) * (GDN_DIM ** -0.5))
        ks.append(kh * lax.rsqrt(jnp.sum(kh * kh, axis=-1, keepdims=True) + EPS))
        vs.append(v[:, sl])
        gs.append(g[:, sl])
        bs.append(beta[:, sl])
    return jnp.stack(qs), jnp.stack(ks), jnp.stack(vs), jnp.stack(gs), jnp.stack(bs)


def gdn_pre_bwd(i, gq, gk, gv, gab, hq, hk, hv, dq, dk, dv, dg, db, cwq, cwk, cwv, alog, dtb, cq, ck, cv):
    grads = _vjp_fn(f_gdn_pre, 12, 5)(i, gq, gk, gv, gab, hq, hk, hv, cwq, cwk, cwv, alog, dtb, dq, dk, dv, dg, db)
    dgq, dgk, dgv, dgab, dhq, dhk, dhv, dcwq, dcwk, dcwv, dalog, ddtb = grads

    def add_tail(dx, carry):
        return jnp.concatenate([dx[:-SUBLANE], dx[-SUBLANE:] + carry], axis=0)

    return (add_tail(dgq, cq), add_tail(dgk, ck), add_tail(dgv, cv), dgab,
            dcwq, dcwk, dcwv, dalog, ddtb, dhq, dhk, dhv)


def f_loss(i, x, h, tgt):
    e = x + h - tgt
    part = 0.5 * jnp.sum(e * e) * (1.0 / D_MODEL)
    dy = e * (1.0 / D_MODEL)
    return dy, dy, jnp.zeros((SUBLANE, LANE), F32) + part


def _flash_tile(t_len):
    return min(512, t_len)


FLASH_HEADS = 4
FLASH_BWD_HEADS = 2
LOG2E = 1.4426950408889634


def _causal(rows0, shape):
    r = rows0 + lax.broadcasted_iota(jnp.int32, shape, 0)
    c = lax.broadcasted_iota(jnp.int32, shape, 1)
    return c <= r


def flash_fwd(q, k, v):
    h_n, t_len, _ = q.shape
    tq = _flash_tile(t_len)
    nq = t_len // tq
    hb = FLASH_HEADS
    kw = 2 if nq % 2 == 0 else 1
    tk = kw * tq
    c2 = (QK ** -0.5) * LOG2E
    pairs = [(i, j) for i in range(nq) for j in range(i // kw + 1)]
    qt = jnp.array([p[0] for p in pairs], jnp.int32)
    kt = jnp.array([p[1] for p in pairs], jnp.int32)

    def body(qt_ref, kt_ref, q_ref, k_ref, v_ref, o_ref, lse_ref, m_s, acc_s):
        step = pl.program_id(1)
        qi, kj = qt_ref[step], kt_ref[step]
        last = qi // kw

        @pl.when(kj == 0)
        def _():
            m_s[...] = jnp.full_like(m_s, -jnp.inf)
            acc_s[...] = jnp.zeros_like(acc_s)

        def tile(diagonal):
            s = _bdot(q_ref[...], k_ref[...], _NT3, False) * c2
            if diagonal:
                s = jnp.where(_causal((qi % kw) * tq, (tq, tk))[None], s, -jnp.inf)
            m_old = m_s[...]
            m_new = jnp.maximum(m_old, jnp.max(s, axis=-1, keepdims=True))
            p = jnp.exp2(s - m_new).astype(_BF)
            v_ones = jnp.concatenate([v_ref[...], jnp.ones((hb, tk, LANE), _BF)], axis=2)
            acc_s[...] = jnp.exp2(m_old - m_new) * acc_s[...] + _bdot(p, v_ones, _NN3, False)
            m_s[...] = m_new

        @pl.when(kj < last)
        def _():
            tile(False)

        @pl.when(kj == last)
        def _():
            tile(True)
            acc = acc_s[...]
            l_sum = acc[:, :, LANE:]
            o = acc[:, :, :LANE] / l_sum
            for hh in range(hb):
                o_ref[:, LANE * hh:LANE * (hh + 1)] = o[hh]
            lse_ref[...] = m_s[...] + jnp.log2(jnp.max(l_sum, axis=-1, keepdims=True))

    return pl.pallas_call(
        body,
        name="flash_fwd",
        grid_spec=pltpu.PrefetchScalarGridSpec(
            num_scalar_prefetch=2,
            grid=(h_n // hb, qt.shape[0]),
            in_specs=[
                pl.BlockSpec((hb, tq, HEAD_PAD), lambda h, s, qt_ref, kt_ref: (h, qt_ref[s], 0)),
                pl.BlockSpec((hb, tk, HEAD_PAD), lambda h, s, qt_ref, kt_ref: (h, kt_ref[s], 0)),
                pl.BlockSpec((hb, tk, LANE), lambda h, s, qt_ref, kt_ref: (h, kt_ref[s], 0)),
            ],
            out_specs=[
                pl.BlockSpec((tq, hb * LANE), lambda h, s, qt_ref, kt_ref: (qt_ref[s], h)),
                pl.BlockSpec((hb, tq, 1), lambda h, s, qt_ref, kt_ref: (h, qt_ref[s], 0)),
            ],
            scratch_shapes=[pltpu.VMEM((hb, tq, 1), F32), pltpu.VMEM((hb, tq, 2 * LANE), F32)],
        ),
        out_shape=[jax.ShapeDtypeStruct((t_len, h_n * LANE), F32), jax.ShapeDtypeStruct((h_n, t_len, 1), F32)],
        compiler_params=_cparams("parallel", "arbitrary"),
    )(qt, kt, q, k, v)


def flash_bwd(q, k, v, do, lse, delta):
    h_n, t_len, _ = q.shape
    tq = _flash_tile(t_len)
    nq = t_len // tq
    hb = FLASH_BWD_HEADS
    kw = 2 if nq % 2 == 0 else 1
    tk = kw * tq
    pairs = [(i, j) for j in range(nq // kw) for i in range(kw * j, nq)]
    n_steps = len(pairs)
    qt = jnp.array([p[0] for p in pairs], jnp.int32)
    kt = jnp.array([p[1] for p in pairs], jnp.int32)
    scale = QK ** -0.5
    c2 = scale * LOG2E

    def body(qt_ref, kt_ref, q_ref, k_ref, v_ref, do_ref, lse_ref, dl_ref, dq_hbm, dk_ref, dv_ref, dq_s, dq_sem):
        group, step = pl.program_id(0), pl.program_id(1)
        qi, kj = qt_ref[step], kt_ref[step]

        @pl.when(step == 0)
        def _():
            dq_s[...] = jnp.zeros_like(dq_s)

        def tile(diagonal):
            qb, kb = q_ref[...], k_ref[...]
            dob = jnp.stack([do_ref[:, LANE * hh:LANE * (hh + 1)] for hh in range(hb)])
            p = jnp.exp2(_bdot(qb, kb, _NT3, False) * c2 - lse_ref[...])
            if diagonal:
                p = jnp.where(_causal((qi % kw) * tq, (tq, tk))[None], p, 0.0)
            dv = _bdot(p, dob, _TN3, False)
            ds = p * (_bdot(dob, v_ref[...], _NT3, False) - dl_ref[...]) * scale
            dk = _bdot(ds, qb, _TN3, False)
            dq_s[:, pl.ds(pl.multiple_of(qi * tq, tq), tq), :] += _bdot(ds, kb, _NN3, False)
            return dk, dv

        @pl.when(qi == kw * kj)
        def _():
            dk_ref[...], dv_ref[...] = tile(True)

        @pl.when((qi != kw * kj) & (qi // kw == kj))
        def _():
            dk, dv = tile(True)
            dk_ref[...] += dk
            dv_ref[...] += dv

        @pl.when(qi // kw > kj)
        def _():
            dk, dv = tile(False)
            dk_ref[...] += dk
            dv_ref[...] += dv

        @pl.when(step == n_steps - 1)
        def _():
            out = pltpu.make_async_copy(dq_s, dq_hbm.at[pl.ds(group * hb, hb)], dq_sem)
            out.start()
            out.wait()

    def qmap(h, s, qt_ref, kt_ref):
        return (h, qt_ref[s], 0)

    def kmap(h, s, qt_ref, kt_ref):
        return (h, kt_ref[s], 0)

    return pl.pallas_call(
        body,
        name="flash_bwd",
        grid_spec=pltpu.PrefetchScalarGridSpec(
            num_scalar_prefetch=2,
            grid=(h_n // hb, n_steps),
            in_specs=[
                pl.BlockSpec((hb, tq, HEAD_PAD), qmap),
                pl.BlockSpec((hb, tk, HEAD_PAD), kmap),
                pl.BlockSpec((hb, tk, LANE), kmap),
                pl.BlockSpec((tq, hb * LANE), lambda h, s, qt_ref, kt_ref: (qt_ref[s], h)),
                pl.BlockSpec((hb, tq, 1), qmap),
                pl.BlockSpec((hb, tq, 1), qmap),
            ],
            out_specs=[
                pl.BlockSpec(memory_space=pl.ANY),
                pl.BlockSpec((hb, tk, HEAD_PAD), kmap),
                pl.BlockSpec((hb, tk, LANE), kmap),
            ],
            scratch_shapes=[pltpu.VMEM((hb, t_len, HEAD_PAD), F32), pltpu.SemaphoreType.DMA],
        ),
        out_shape=[
            jax.ShapeDtypeStruct((h_n, t_len, HEAD_PAD), F32),
            jax.ShapeDtypeStruct((h_n, t_len, HEAD_PAD), F32),
            jax.ShapeDtypeStruct((h_n, t_len, LANE), F32),
        ],
        compiler_params=_cparams("parallel", "arbitrary"),
    )(qt, kt, q, k, v, do, lse, delta)


def _tri_ones(h_n):
    ii = lax.broadcasted_iota(jnp.int32, (h_n, CHUNK, CHUNK), 1)
    jj = lax.broadcasted_iota(jnp.int32, (h_n, CHUNK, CHUNK), 2)
    return (ii >= jj).astype(_BF)


@jax.custom_vjp
def _chunk_cumsum(gb):
    tri = _tri_ones(gb.shape[0])
    return _bdot(jnp.concatenate([tri, tri, tri], axis=2), jnp.concatenate(_split3(gb), axis=1), _NN3, False)


def _chunk_cumsum_bwd(_, ct):
    tri = _tri_ones(ct.shape[0])
    return (_bdot(jnp.concatenate([tri, tri, tri], axis=1), jnp.concatenate(_split3(ct), axis=1), _TN3, False),)


_chunk_cumsum.defvjp(lambda gb: (_chunk_cumsum(gb), None), _chunk_cumsum_bwd)


@jax.custom_vjp
def _pair_diff(gcb):
    g1, g2, g3 = _split3(gcb)
    lane = lax.broadcasted_iota(jnp.int32, (1, 1, LANE), 2)
    one, zero = jnp.ones((), _BF), jnp.zeros((), _BF)
    a = jnp.where(lane == 0, g1, jnp.where(lane == 1, g2, jnp.where(lane == 2, g3, jnp.where(lane < 6, one, zero))))
    b = jnp.where(lane < 3, one, jnp.where(lane == 3, -g1, jnp.where(lane == 4, -g2, jnp.where(lane == 5, -g3, zero))))
    return _bdot(a, b, _NT3, False)


def _pair_diff_bwd(_, ct):
    parts = _split3(ct)
    ones = jnp.ones((ct.shape[0], 3 * CHUNK, LANE), _BF)
    rows = _bdot(jnp.concatenate(parts, axis=2), ones, _NN3, False)
    cols = _bdot(jnp.concatenate(parts, axis=1), ones, _TN3, False)
    lane = lax.broadcasted_iota(jnp.int32, (1, 1, LANE), 2)
    return (jnp.where(lane == 0, rows - cols, 0.0),)


_pair_diff.defvjp(lambda gcb: (_pair_diff(gcb), None), _pair_diff_bwd)


@jax.custom_vjp
def _saved_inverse(lmat, inv):
    return inv


def _saved_inverse_bwd(inv, g):
    return -_pdot(_pdot(inv, g, "tn"), inv, "nt"), jnp.zeros_like(inv)


_saved_inverse.defvjp(lambda lmat, inv: (inv, inv), _saved_inverse_bwd)


def gdn_step(s, q, k, v, gb, bb, inv_saved=None):
    c = CHUNK
    ii = lax.broadcasted_iota(jnp.int32, (1, c, c), 1)
    jj = lax.broadcasted_iota(jnp.int32, (1, c, c), 2)
    incl, strict = ii >= jj, ii > jj
    gcb = _chunk_cumsum(gb)
    diff = _pair_diff(gcb)
    decay = jnp.where(incl, jnp.exp(jnp.where(incl, diff, 0.0)), 0.0)
    kb, vb = k * bb, v * bb
    egc = jnp.exp(gcb)
    lmat = jnp.where(strict, _bmm_nt(kb, k) * decay, 0.0)
    if inv_saved is None:
        mm3 = _bh3_passes
        inv = (ii == jj).astype(F32) - lmat
        pw = mm3(lmat, lmat)
        for step in range(5):
            inv = inv + mm3(inv, pw)
            if step < 4:
                pw = mm3(pw, pw)
    else:
        mm3 = _bh3
        inv = _saved_inverse(lmat, inv_saved)
    u = mm3(inv, vb)
    w = mm3(inv, kb * egc)
    attn = _bmm_nt(q, k) * decay
    qd = q * egc
    g_end = jnp.sum(gb, axis=1, keepdims=True)
    kd = k * jnp.exp(g_end - gcb)
    v_new = u - _bmm(w, s)
    o = _bmm(qd, s) + _bmm(attn, v_new)
    s_new = s * jnp.exp(g_end) + _bmm_tn(kd, v_new)
    return s_new, o, inv


def gdn_fwd(q, k, v, gb, bb):
    h_n, t_len, d = q.shape
    n = t_len // CHUNK
    blk = pl.BlockSpec((h_n, CHUNK, d), lambda i: (0, i, 0))

    def body(q_ref, k_ref, v_ref, g_ref, b_ref, o_ref, sall_ref, inv_ref, s_s):
        @pl.when(pl.program_id(0) == 0)
        def _():
            s_s[...] = jnp.zeros_like(s_s)

        s = s_s[...]
        sall_ref[0] = s
        s_s[...], o_ref[...], inv_ref[0] = gdn_step(s, q_ref[...], k_ref[...], v_ref[...], g_ref[...], b_ref[...])

    return pl.pallas_call(
        body,
        name="gdn_fwd",
        grid=(n,),
        in_specs=[blk] * 5,
        out_specs=[blk, pl.BlockSpec((1, h_n, d, d), lambda i: (i, 0, 0, 0)),
                   pl.BlockSpec((1, h_n, CHUNK, CHUNK), lambda i: (i, 0, 0, 0))],
        out_shape=[jax.ShapeDtypeStruct((h_n, t_len, d), F32), jax.ShapeDtypeStruct((n, h_n, d, d), F32),
                   jax.ShapeDtypeStruct((n, h_n, CHUNK, CHUNK), F32)],
        scratch_shapes=[pltpu.VMEM((h_n, d, d), F32)],
        compiler_params=_cparams("arbitrary"),
    )(q, k, v, gb, bb)


def gdn_bwd(q, k, v, gb, bb, s_all, inv_all, do):
    h_n, t_len, d = q.shape
    n = t_len // CHUNK
    blk = pl.BlockSpec((h_n, CHUNK, d), lambda i: (0, n - 1 - i, 0))

    def body(q_ref, k_ref, v_ref, g_ref, b_ref, sall_ref, inv_ref, do_ref, dq_ref, dk_ref, dv_ref, dg_ref, db_ref,
             ds_s):
        @pl.when(pl.program_id(0) == 0)
        def _():
            ds_s[...] = jnp.zeros_like(ds_s)

        inv = inv_ref[0]
        _, pull = jax.vjp(lambda *a: gdn_step(*a, inv_saved=inv)[:2], sall_ref[0], q_ref[...], k_ref[...], v_ref[...],
                          g_ref[...], b_ref[...])
        ds_s[...], dq_ref[...], dk_ref[...], dv_ref[...], dg_ref[...], db_ref[...] = pull((ds_s[...], do_ref[...]))

    return pl.pallas_call(
        body,
        name="gdn_bwd",
        grid=(n,),
        in_specs=[blk] * 5 + [pl.BlockSpec((1, h_n, d, d), lambda i: (n - 1 - i, 0, 0, 0)),
                              pl.BlockSpec((1, h_n, CHUNK, CHUNK), lambda i: (n - 1 - i, 0, 0, 0)), blk],
        out_specs=[blk] * 5,
        out_shape=[jax.ShapeDtypeStruct((h_n, t_len, d), F32)] * 5,
        scratch_shapes=[pltpu.VMEM((h_n, d, d), F32)],
        compiler_params=_cparams("arbitrary"),
    )(q, k, v, gb, bb, s_all, inv_all, do)


def _pad_cols(a, n):
    return jnp.pad(a, ((0, 0), (0, n - a.shape[1])))


def arrange_w_in(w):
    pieces, start = [], 0
    for n in R_SPLITS:
        pieces.append(w[:, start:start + n])
        start += n
    cq, ckv, kr, mgate, gq, gk, gv, ga, gb, ggate = pieces
    return jnp.concatenate([mgate, gq, gk, gv, ggate, cq, ckv, _pad_cols(kr, LANE),
                            _pad_cols(jnp.concatenate([ga, gb], axis=1), LANE)], axis=1)


def unarrange_w_in(g):
    def cols(start, n):
        return g[:, start:start + n]
    return jnp.concatenate([cols(P_CQ, Q_LORA), cols(P_CKV, KV_LORA), cols(P_KR, ROPE), cols(P_MGATE, WIDTH),
                            cols(P_GQ, WIDTH), cols(P_GK, WIDTH), cols(P_GV, WIDTH), cols(P_GAB, HEADS),
                            cols(P_GAB + HEADS, HEADS), cols(P_GGATE, WIDTH)], axis=1)


def arrange_w_uq(w):
    w = w.reshape(w.shape[0], HEADS, QK)
    return jnp.pad(w, ((0, 0), (0, 0), (0, HEAD_PAD - QK))).reshape(w.shape[0], HEADS * HEAD_PAD)


def unarrange_w_uq(g):
    return g.reshape(g.shape[0], HEADS, HEAD_PAD)[:, :, :QK].reshape(g.shape[0], HEADS * QK)


def local_step(x, pos, tgt, p, on_weight_grads=None, on_d_xn=None, first_after=None, late_weights=None):
    t_len = x.shape[0]
    w_in = p["w_in"]
    norm_gain = p["norm_gain"].reshape(1, D_MODEL)
    qa_gain = p["mla_q_a_gain"].reshape(1, Q_LORA)
    kva_gain = p["mla_kv_a_gain"].reshape(1, KV_LORA)
    qg = _pad_cols(p["mla_q_norm_gain"].reshape(1, QK), HEAD_PAD)
    kg = _pad_cols(p["mla_k_norm_gain"].reshape(1, QK), HEAD_PAD)
    alog = _pad_cols(p["gdn_a_log"].reshape(1, HEADS), LANE)
    dtb = _pad_cols(p["gdn_dt_bias"].reshape(1, HEADS), LANE)
    og = p["gdn_out_norm_gain"].reshape(1, GDN_DIM)
    half = ROPE // 2
    inv_freq = jnp.power(ROPE_THETA, -jnp.arange(half, dtype=F32) / half)
    invf = _pad_cols(jnp.concatenate([inv_freq, inv_freq]).reshape(1, ROPE), LANE)

    rt = 256
    r = "r"
    first_after = jnp.zeros((SUBLANE, LANE), F32) if first_after is None else first_after
    (xn,) = rowwise("rms_x", lambda i, x_, gain_, after_: f_rms_x(i, x_, gain_), t_len, rt, [(x, (r, D_MODEL, 0))],
                    [norm_gain, first_after], [(r, D_MODEL, _BF)])
    proj = matmul("proj", xn, w_in, "nn")
    if late_weights is not None:
        p = {**p, **late_weights(proj)}
    w_uq, w_ukv, w_out = p["w_uq"], p["w_ukv"], p["w_out"]
    cw = p["gdn_conv_w"].reshape(CONV_W, 3 * WIDTH)
    cwq, cwk, cwv = cw[:, :WIDTH], cw[:, WIDTH:2 * WIDTH], cw[:, 2 * WIDTH:]
    cq_in = (proj, (r, Q_LORA, P_CQ // Q_LORA))
    ckv_in = (proj, (r, KV_LORA, P_CKV // KV_LORA))
    kr_in = (proj, (r, LANE, P_KR // LANE))
    mgate_in = (proj, (r, WIDTH, P_MGATE // WIDTH))
    ggate_in = (proj, (r, WIDTH, P_GGATE // WIDTH))
    gqkv_in = [(proj, (r, WIDTH, P_GQ // WIDTH)), (proj, (r, WIDTH, P_GK // WIDTH)), (proj, (r, WIDTH, P_GV // WIDTH))]
    gab_in = (proj, (r, LANE, P_GAB // LANE))
    halos = [(proj, ("halo", WIDTH, P_GQ // WIDTH)), (proj, ("halo", WIDTH, P_GK // WIDTH)),
             (proj, ("halo", WIDTH, P_GV // WIDTH))]

    q_lat, kv_lat = rowwise("lat", f_lat, t_len, rt, [cq_in, ckv_in], [qa_gain, kva_gain],
                            [(r, Q_LORA, _BF), (r, KV_LORA, _BF)])
    q_raw = matmul("q_up", q_lat, w_uq, "nn")
    kv_raw = matmul("kv_up", kv_lat, w_ukv, "nn")
    wide = HEADS * HEAD_PAD
    head_in = [(q_raw, (r, wide, 0)), (kv_raw, (r, wide, 0)), kr_in]
    pos_in = (pos, (r, 1, 0))
    q_full, k_full, v_mla = rowwise(
        "head", lambda i, qr, kvr, kr, ps, qg_, kg_, iv: f_head(i, qr, kvr, kr, qg_, kg_, ps, iv), t_len, rt,
        head_in + [pos_in], [qg, kg, invf],
        [("h", HEADS, HEAD_PAD, _BF), ("h", HEADS, HEAD_PAD, _BF), ("h", HEADS, LANE, _BF)])
    o_mla, lse = flash_fwd(q_full, k_full, v_mla)

    pre_in = gqkv_in + [gab_in] + halos
    pre_full = [cwq, cwk, cwv, alog, dtb]
    hkind = ("h", HEADS, GDN_DIM, F32)
    gq_n, gk_n, gv_n, g_b, b_b = rowwise("gdn_pre", f_gdn_pre, t_len, rt, pre_in, pre_full, [hkind] * 5)
    o_gdn, s_all, inv_all = gdn_fwd(gq_n, gk_n, gv_n, g_b, b_b)

    mix_in = [(o_mla, (r, WIDTH, 0)), mgate_in, (o_gdn, ("h",)), ggate_in]
    (mixed,) = rowwise("mix", f_mix, t_len, rt, mix_in, [og], [(r, 2 * WIDTH, _BF)])
    h_out = matmul("out_proj", mixed, w_out, "nn")
    dy, dy_mx, loss_acc = rowwise("loss", f_loss, t_len, rt,
                                  [(x, (r, D_MODEL, 0)), (h_out, (r, D_MODEL, 0)), (tgt, (r, D_MODEL, 0))], [],
                                  [(r, D_MODEL, F32), (r, D_MODEL, _BF)], [(SUBLANE, LANE)])
    loss = loss_acc[0, 0]

    d_mixed = matmul("d_mixed", dy_mx, w_out, "nt")
    g_w_out = matmul("g_w_out", mixed, dy_mx, "tn")

    def mix_bwd(i, o_mla_, mgate_, o_gdn_, ggate_, d_mixed_, og_):
        do_mla_, d_mgate_, do_gdn_, d_ggate_, g_og_ = _vjp_fn(f_mix, 5, 1)(i, o_mla_, mgate_, o_gdn_, ggate_, og_, d_mixed_)
        delta_ = jnp.stack([jnp.sum(o_mla_[:, LANE * h:LANE * (h + 1)] * do_mla_[:, LANE * h:LANE * (h + 1)],
                                    axis=-1, keepdims=True) for h in range(HEADS)])
        return do_mla_, d_mgate_, do_gdn_, d_ggate_, delta_, g_og_

    do_mla, d_mgate, do_gdn, d_ggate, delta, g_og = rowwise(
        "mix_bwd", mix_bwd, t_len, rt, mix_in + [(d_mixed, (r, 2 * WIDTH, 0))], [og],
        [(r, WIDTH, F32), (r, WIDTH, _BF), hkind, (r, WIDTH, _BF), ("h", HEADS, 1, F32)], [(1, GDN_DIM)])
    dq_n, dk_n, dv_n, dg_b, db_b = gdn_bwd(gq_n, gk_n, gv_n, g_b, b_b, s_all, inv_all, do_gdn)
    cts_in = [(a, ("h",)) for a in (dq_n, dk_n, dv_n, dg_b, db_b)]
    d_gq, d_gk, d_gv, d_gab, g_cwq, g_cwk, g_cwv, g_alog, g_dtb = rowwise(
        "gdn_pre_bwd", gdn_pre_bwd, t_len, rt, pre_in + cts_in, pre_full,
        [(r, WIDTH, _BF)] * 3 + [(r, LANE, _BF)],
        [(CONV_W, WIDTH)] * 3 + [(1, LANE)] * 2, carries=[(SUBLANE, WIDTH)] * 3, reverse=True)

    dq_full, dk_full, dv_mla = flash_bwd(q_full, k_full, v_mla, do_mla, lse, delta)
    head_cts = [(a, ("h",)) for a in (dq_full, dk_full, dv_mla)]

    def head_bwd(i, q_raw_, kv_raw_, kr_, pos_, dq_, dk_, dv_, qg_, kg_, invf_):
        return _vjp_fn(f_head, 5, 3)(i, q_raw_, kv_raw_, kr_, qg_, kg_, pos_, invf_, dq_, dk_, dv_)

    dq_raw, dkv_raw, d_kr, g_qg, g_kg = rowwise(
        "head_bwd", head_bwd, t_len, rt, head_in + [pos_in] + head_cts, [qg, kg, invf],
        [(r, wide, _BF), (r, wide, _BF), (r, LANE, _BF)], [(1, HEAD_PAD), (1, HEAD_PAD)])
    dq_lat = matmul("dq_lat", dq_raw, w_uq, "nt")
    g_w_uq = matmul("g_w_uq", q_lat, dq_raw, "tn")
    dkv_lat = matmul("dkv_lat", dkv_raw, w_ukv, "nt")
    g_w_ukv = matmul("g_w_ukv", kv_lat, dkv_raw, "tn")

    def lat_bwd(i, cq_, ckv_, dql_, dkl_, gq_, gkv_):
        return _vjp_fn(f_lat, 4, 2)(i, cq_, ckv_, gq_, gkv_, dql_, dkl_)

    d_cq, d_ckv, g_qa, g_kva = rowwise(
        "lat_bwd", lat_bwd, t_len, rt, [cq_in, ckv_in, (dq_lat, (r, Q_LORA, 0)), (dkv_lat, (r, KV_LORA, 0))],
        [qa_gain, kva_gain], [(r, Q_LORA, _BF), (r, KV_LORA, _BF)], [(1, Q_LORA), (1, KV_LORA)])

    d_proj = jnp.concatenate([d_mgate, d_gq, d_gk, d_gv, d_ggate, d_cq, d_ckv, d_kr, d_gab], axis=1)
    g_w_in = matmul("g_w_in", xn, d_proj, "tn")
    grads = {
        "w_in": g_w_in, "mla_q_a_gain": g_qa, "mla_kv_a_gain": g_kva, "w_uq": g_w_uq,
        "w_ukv": g_w_ukv, "mla_q_norm_gain": g_qg[:, :QK], "mla_k_norm_gain": g_kg[:, :QK],
        "gdn_conv_w": jnp.concatenate([g_cwq, g_cwk, g_cwv], axis=1), "gdn_a_log": g_alog[:, :HEADS],
        "gdn_dt_bias": g_dtb[:, :HEADS], "gdn_out_norm_gain": g_og, "w_out": g_w_out,
    }
    after = None if on_weight_grads is None else on_weight_grads(grads)
    d_xn = matmul("d_xn", d_proj, w_in, "nt", after=after)
    after = jnp.zeros((SUBLANE, LANE), F32) if on_d_xn is None else on_d_xn(d_xn)

    def rms_x_bwd(i, x_, dxn_, dy_, gain_, after_):
        dx, dgain = _vjp_fn(f_rms_x, 2, 1)(i, x_, gain_, dxn_)
        return dx + dy_, dgain

    grad_x, grads["norm_gain"] = rowwise(
        "rms_x_bwd", rms_x_bwd, t_len, rt, [(x, (r, D_MODEL, 0)), (d_xn, (r, D_MODEL, 0)), (dy, (r, D_MODEL, 0))],
        [norm_gain, after], [(r, D_MODEL, F32)], [(1, D_MODEL)])
    return loss, grad_x, grads


MESH = pl.DeviceIdType.MESH
ANY = pl.BlockSpec(memory_space=pl.ANY)
CHIP_FLIPS = ((1, 0), (0, 1), (1, 1))


def _place():
    return lax.axis_index("x"), lax.axis_index("y"), lax.axis_index("c")


def _flip(v, f):
    return 1 - v if f else v


def all_gather(shards):
    n_arr = len(shards)

    def body(*refs):
        x_refs, o_refs = refs[:n_arr], refs[n_arr:2 * n_arr]
        send_sems, recv_sems, local_sems = refs[2 * n_arr:]
        x, y, c = _place()
        me, sibling = (x, y, c), (x, y, 1 - c)
        chips = [(_flip(x, fx), _flip(y, fy)) for fx, fy in CHIP_FLIPS]

        def copy(a, k, block, to, src=None):
            px, py, pc = block
            dst = o_refs[a].at[4 * px + 2 * py + pc]
            return pltpu.make_async_remote_copy(
                src_ref=dst if src is None else src, dst_ref=dst, send_sem=send_sems.at[a, k],
                recv_sem=recv_sems.at[a, k], device_id=to, device_id_type=MESH)

        mine, first, passed = [], [], []
        for a in range(n_arr):
            cp = pltpu.make_async_copy(x_refs[a], o_refs[a].at[4 * x + 2 * y + c], local_sems.at[a])
            cp.start()
            mine.append(cp)
            first.append(copy(a, 0, me, sibling, src=x_refs[a]))
            first += [copy(a, 1 + j, me, (*chip, c), src=x_refs[a]) for j, chip in enumerate(chips)]
        for cp in first:
            cp.start()
        for j, chip in enumerate(chips):
            for a in range(n_arr):
                copy(a, 1 + j, (*chip, c), me).wait_recv()
                cp = copy(a, 4 + j, (*chip, c), sibling)
                cp.start()
                passed.append(cp)
        for a in range(n_arr):
            copy(a, 0, sibling, me).wait_recv()
            for j, chip in enumerate(chips):
                copy(a, 4 + j, (*chip, 1 - c), me).wait_recv()
        for cp in first + passed:
            cp.wait_send()
        for cp in mine:
            cp.wait()

    return pl.pallas_call(
        body,
        name="all_gather",
        out_shape=[jax.ShapeDtypeStruct((N_DEV,) + s.shape, s.dtype) for s in shards],
        in_specs=[ANY] * n_arr,
        out_specs=[ANY] * n_arr,
        scratch_shapes=[pltpu.SemaphoreType.DMA((n_arr, 7)), pltpu.SemaphoreType.DMA((n_arr, 7)),
                        pltpu.SemaphoreType.DMA((n_arr,))],
    )(*shards)


HBM = pl.BlockSpec(memory_space=pltpu.HBM)
SEMS = pl.BlockSpec(memory_space=pltpu.SEMAPHORE)
SIDE_EFFECT = pltpu.SideEffectType.DATAFLOW_SIDE_EFFECTING


def core_routes(x, y, c):
    return [(2 * q + (1 - c), q, (x, y, 1 - c)) for q in range(4)]


def chip_routes(x, y, c):
    routes = []
    for j, (fx, fy) in enumerate(CHIP_FLIPS):
        px, py = _flip(x, fx), _flip(y, fy)
        routes.append((2 * px + py, j, (px, py, c)))
    return routes


def _route_copies(routes, n_routes, src_refs, land_refs, sems):
    x, y, c = _place()
    n_copies = len(src_refs) * n_routes
    return [pltpu.make_async_remote_copy(src_ref=src.at[s], dst_ref=land.at[d], send_sem=sems[a * n_routes + k],
                                         recv_sem=sems[n_copies + a * n_routes + k], device_id=dev,
                                         device_id_type=MESH)
            for a, (src, land) in enumerate(zip(src_refs, land_refs)) for k, (s, d, dev) in enumerate(routes(x, y, c))]


def gather_routes(x, y, c):
    me = 4 * x + 2 * y + c
    return [(0, me, (_flip(x, (k >> 2) & 1), _flip(y, (k >> 1) & 1), _flip(c, k & 1))) for k in range(1, N_DEV)]


def exchange_start(name, routes, n_routes, srcs, n_slots=None):
    n = len(srcs)
    n_sems = 2 * n * n_routes
    lands = [lax.empty((n_routes if n_slots is None else n_slots,) + s.shape[1:], s.dtype) for s in srcs]

    def body(*refs):
        for cp in _route_copies(routes, n_routes, refs[:n], refs[n:2 * n], refs[2 * n:2 * n + n_sems]):
            cp.start()
        refs[-1][...] = jnp.zeros_like(refs[-1])

    res = pl.pallas_call(
        body,
        name=name,
        out_shape=(*[pltpu.SemaphoreType.DMA(())] * n_sems, *[pltpu.HBM(a.shape, a.dtype) for a in srcs + lands],
                   jax.ShapeDtypeStruct((SUBLANE, LANE), F32)),
        in_specs=[HBM] * (2 * n),
        out_specs=(*[SEMS] * n_sems, *[HBM] * (2 * n), pl.BlockSpec(memory_space=pltpu.VMEM)),
        input_output_aliases={i: n_sems + i for i in range(2 * n)},
        compiler_params=pltpu.CompilerParams(has_side_effects=SIDE_EFFECT),
    )(*[pltpu.with_memory_space_constraint(a, pltpu.HBM) for a in srcs + lands])
    return (res[:n_sems], res[n_sems:-1]), res[-1]


def exchange_wait(name, routes, handle, after):
    sems, thru = handle
    n, n_sems = len(thru) // 2, len(sems)
    n_routes = n_sems // (2 * n)

    def body(*refs):
        for cp in _route_copies(routes, n_routes, refs[:n], refs[n:2 * n], refs[2 * n:2 * n + n_sems]):
            cp.wait_send()
            cp.wait_recv()

    res = pl.pallas_call(
        body,
        name=name,
        out_shape=tuple(pltpu.HBM(a.shape, a.dtype) for a in thru),
        in_specs=[HBM] * (2 * n) + [SEMS] * n_sems + [ANY],
        out_specs=tuple([HBM] * (2 * n)),
        input_output_aliases={i: i for i in range(2 * n)},
        compiler_params=pltpu.CompilerParams(has_side_effects=SIDE_EFFECT),
    )(*thru, *sems, after)
    return list(res[:n]), list(res[n:])


def gather_small(v):
    def body(v_ref, o_ref, send_sems, recv_sems, local_sem):
        x, y, c = _place()
        me = 4 * x + 2 * y + c
        mine = pltpu.make_async_copy(v_ref, o_ref.at[me], local_sem)
        mine.start()
        copies = []
        for k in range(1, N_DEV):
            fx, fy, fc = (k >> 2) & 1, (k >> 1) & 1, k & 1
            cp = pltpu.make_async_remote_copy(
                src_ref=v_ref, dst_ref=o_ref.at[me], send_sem=send_sems.at[k - 1], recv_sem=recv_sems.at[k - 1],
                device_id=(_flip(x, fx), _flip(y, fy), _flip(c, fc)), device_id_type=MESH)
            cp.start()
            copies.append(cp)
        for cp in copies:
            cp.wait()
        mine.wait()

    return pl.pallas_call(
        body,
        name="gather_small",
        out_shape=jax.ShapeDtypeStruct((N_DEV,) + v.shape, v.dtype),
        in_specs=[ANY],
        out_specs=ANY,
        scratch_shapes=[pltpu.SemaphoreType.DMA((N_DEV - 1,)), pltpu.SemaphoreType.DMA((N_DEV - 1,)),
                        pltpu.SemaphoreType.DMA],
    )(v)


def _row_tile(rows):
    for t in (256, 128, 64, 32, 16, 8):
        if rows % t == 0:
            return t
    return rows


def add_core_parts(name, g, recv, c_idx, wire):
    _, rows, cols = g.shape
    tr = _row_tile(rows)

    def body(c_ref, g_ref, r_ref, o_ref, w_ref):
        part = g_ref[...] + r_ref[...]
        o_ref[...] = part
        w_ref[...] = part.astype(w_ref.dtype)

    blk = pl.BlockSpec((1, tr, cols), lambda q, i, c_ref: (q, i, 0))
    return pl.pallas_call(
        body,
        name=name,
        grid_spec=pltpu.PrefetchScalarGridSpec(
            num_scalar_prefetch=1,
            grid=(4, rows // tr),
            in_specs=[pl.BlockSpec((1, tr, cols), lambda q, i, c_ref: (2 * q + c_ref[0], i, 0)), blk],
            out_specs=[blk, blk],
        ),
        out_shape=[jax.ShapeDtypeStruct((4, rows, cols), F32), jax.ShapeDtypeStruct((4, rows, cols), wire)],
        compiler_params=_cparams("parallel", "parallel"),
    )(c_idx, g, recv)


def _adamw(w, g, m, v):
    m = ADAM_B1 * m + (1.0 - ADAM_B1) * g
    v = ADAM_B2 * v + (1.0 - ADAM_B2) * (g * g)
    m_hat = m / (1.0 - ADAM_B1 ** ADAM_STEP)
    v_hat = v / (1.0 - ADAM_B2 ** ADAM_STEP)
    delta = -ADAM_LR * (m_hat / (jnp.sqrt(v_hat) + ADAM_EPS) + ADAM_WD * w)
    return delta, m, v


def adamw_sharded(name, parts, recv, q_idx, w, m, v):
    rows, cols = w.shape
    tr = _row_tile(rows)

    def body(q_ref, p_ref, r_ref, w_ref, m_ref, v_ref, g_out, d_out, m_out, v_out):
        g = p_ref[0] + r_ref[0].astype(F32) + r_ref[1].astype(F32) + r_ref[2].astype(F32)
        d, m_new, v_new = _adamw(w_ref[...], g, m_ref[...], v_ref[...])
        g_out[...], d_out[...], m_out[...], v_out[...] = g, d, m_new, v_new

    blk = pl.BlockSpec((tr, cols), lambda i, q_ref: (i, 0))
    return pl.pallas_call(
        body,
        name=name,
        grid_spec=pltpu.PrefetchScalarGridSpec(
            num_scalar_prefetch=1,
            grid=(rows // tr,),
            in_specs=[pl.BlockSpec((1, tr, cols), lambda i, q_ref: (q_ref[0], i, 0)),
                      pl.BlockSpec((3, tr, cols), lambda i, q_ref: (0, i, 0)), blk, blk, blk],
            out_specs=[blk] * 4,
        ),
        out_shape=[jax.ShapeDtypeStruct((rows, cols), F32)] * 4,
        compiler_params=_cparams("parallel"),
    )(q_idx, parts, recv, w, m, v)


def adamw_small(gathered, w, m, v):
    def body(g_ref, w_ref, m_ref, v_ref, g_out, d_out, m_out, v_out):
        g = g_ref[0]
        for j in range(1, N_DEV):
            g = g + g_ref[j]
        d, m_new, v_new = _adamw(w_ref[...], g, m_ref[...], v_ref[...])
        g_out[...], d_out[...], m_out[...], v_out[...] = g, d, m_new, v_new

    return pl.pallas_call(body, name="adamw_small", out_shape=[jax.ShapeDtypeStruct(w.shape, F32)] * 4)(gathered, w, m, v)


SHARDED = ("w_in", "w_uq", "w_ukv", "gdn_conv_w", "w_out")
SMALL = (("norm_gain", D_MODEL), ("mla_q_a_gain", Q_LORA), ("mla_kv_a_gain", KV_LORA), ("mla_q_norm_gain", QK),
         ("mla_k_norm_gain", QK), ("gdn_a_log", HEADS), ("gdn_dt_bias", HEADS), ("gdn_out_norm_gain", GDN_DIM))
WEIGHT_ORDER = ("norm_gain", "w_in", "mla_q_a_gain", "mla_kv_a_gain", "w_uq", "w_ukv", "mla_q_norm_gain",
                "mla_k_norm_gain", "gdn_conv_w", "gdn_a_log", "gdn_dt_bias", "gdn_out_norm_gain", "w_out")


def _pack_small(d):
    rows = []
    for name, n in SMALL:
        a = d[name].reshape(-1).astype(F32)
        n_pad = -(-n // LANE) * LANE
        rows.append(jnp.pad(a, (0, n_pad - n)).reshape(n_pad // LANE, LANE))
    packed = jnp.concatenate(rows, axis=0)
    return jnp.pad(packed, ((0, -packed.shape[0] % SUBLANE), (0, 0)))


def _unpack_small(packed):
    out, row = {}, 0
    for name, n in SMALL:
        n_rows = -(-n // LANE)
        out[name] = packed[row:row + n_rows].reshape(-1)[:n].reshape(1, n)
        row += n_rows
    return out


def kernel(x, positions, norm_gain, w_in, mla_q_a_gain, mla_kv_a_gain, w_uq, w_ukv, mla_q_norm_gain, mla_k_norm_gain, gdn_conv_w, gdn_a_log, gdn_dt_bias, gdn_out_norm_gain, w_out, loss_target, m_norm_gain, m_w_in, m_mla_q_a_gain, m_mla_kv_a_gain, m_w_uq, m_w_ukv, m_mla_q_norm_gain, m_mla_k_norm_gain, m_gdn_conv_w, m_gdn_a_log, m_gdn_dt_bias, m_gdn_out_norm_gain, m_w_out, v_norm_gain, v_w_in, v_mla_q_a_gain, v_mla_kv_a_gain, v_w_uq, v_w_ukv, v_mla_q_norm_gain, v_mla_k_norm_gain, v_gdn_conv_w, v_gdn_a_log, v_gdn_dt_bias, v_gdn_out_norm_gain, v_w_out):
    w = dict(norm_gain=norm_gain, w_in=w_in, mla_q_a_gain=mla_q_a_gain, mla_kv_a_gain=mla_kv_a_gain, w_uq=w_uq,
             w_ukv=w_ukv, mla_q_norm_gain=mla_q_norm_gain, mla_k_norm_gain=mla_k_norm_gain, gdn_conv_w=gdn_conv_w,
             gdn_a_log=gdn_a_log, gdn_dt_bias=gdn_dt_bias, gdn_out_norm_gain=gdn_out_norm_gain, w_out=w_out)
    m = dict(norm_gain=m_norm_gain, w_in=m_w_in, mla_q_a_gain=m_mla_q_a_gain, mla_kv_a_gain=m_mla_kv_a_gain,
             w_uq=m_w_uq, w_ukv=m_w_ukv, mla_q_norm_gain=m_mla_q_norm_gain, mla_k_norm_gain=m_mla_k_norm_gain,
             gdn_conv_w=m_gdn_conv_w, gdn_a_log=m_gdn_a_log, gdn_dt_bias=m_gdn_dt_bias,
             gdn_out_norm_gain=m_gdn_out_norm_gain, w_out=m_w_out)
    v = dict(norm_gain=v_norm_gain, w_in=v_w_in, mla_q_a_gain=v_mla_q_a_gain, mla_kv_a_gain=v_mla_kv_a_gain,
             w_uq=v_w_uq, w_ukv=v_w_ukv, mla_q_norm_gain=v_mla_q_norm_gain, mla_k_norm_gain=v_mla_k_norm_gain,
             gdn_conv_w=v_gdn_conv_w, gdn_a_log=v_gdn_a_log, gdn_dt_bias=v_gdn_dt_bias,
             gdn_out_norm_gain=v_gdn_out_norm_gain, w_out=v_w_out)
    t_len = x.shape[1]

    shards = [w[n][0] if n == "gdn_conv_w" else w[n][0].astype(_BF) for n in SHARDED]
    xi, yi, ci = _place()
    c_idx = jnp.reshape(ci, (1,)).astype(jnp.int32)
    q_idx = jnp.reshape(2 * xi + yi, (1,)).astype(jnp.int32)
    flight = {}

    def cols_whole(g):
        return g.transpose(1, 0, 2).reshape(g.shape[1], N_DEV * g.shape[2])

    def col_blocks(g):
        return g.reshape(g.shape[0], N_DEV, g.shape[1] // N_DEV).transpose(1, 0, 2)

    (a_w_in,) = all_gather(shards[:1])
    p = {n: w[n] for n, _ in SMALL}
    p["w_in"] = arrange_w_in(cols_whole(a_w_in))
    flight["weights"], weights_token = exchange_start(
        "gather_start", gather_routes, N_DEV - 1, [s[None] for s in shards[1:]], n_slots=N_DEV)

    def late_weights(proj):
        _, landed = exchange_wait("gather_wait", gather_routes, flight["weights"], proj)
        me = 4 * xi + 2 * yi + ci
        a_w_uq, a_w_ukv, a_cw, a_w_out = [lax.dynamic_update_slice(land, s[None], (me, 0, 0))
                                          for land, s in zip(landed, shards[1:])]
        return {"w_uq": arrange_w_uq(cols_whole(a_w_uq)), "w_ukv": cols_whole(a_w_ukv),
                "gdn_conv_w": cols_whole(a_cw), "w_out": a_w_out.reshape(N_DEV * a_w_out.shape[1], a_w_out.shape[2])}

    def start_core_exchange(grads):
        blocks = [col_blocks(unarrange_w_in(grads["w_in"])), col_blocks(unarrange_w_uq(grads["w_uq"])),
                  col_blocks(grads["w_ukv"]), col_blocks(grads["gdn_conv_w"]),
                  grads["w_out"].reshape(N_DEV, D_MODEL // N_DEV, D_MODEL)]
        flight["cores"], token = exchange_start("cores_start", core_routes, 4, blocks)
        return token

    def start_chip_exchange(d_xn):
        blocks, landed = exchange_wait("cores_wait", core_routes, flight["cores"], d_xn)
        flight["parts"] = [add_core_parts("add_" + n, g, r, c_idx, F32 if n == "gdn_conv_w" else _BF)
                           for n, g, r in zip(SHARDED, blocks, landed)]
        flight["chips"], token = exchange_start("chips_start", chip_routes, 3, [wire for _, wire in flight["parts"]])
        return token

    pos = positions.reshape(t_len, 1).astype(F32)
    loss, grad_x, grads = local_step(x.reshape(t_len, D_MODEL), pos, loss_target.reshape(t_len, D_MODEL), p,
                                     on_weight_grads=start_core_exchange, on_d_xn=start_chip_exchange,
                                     first_after=weights_token, late_weights=late_weights)
    loss = lax.psum(loss, ("x", "y", "c"))
    out = {}
    small_all = gather_small(_pack_small(grads))
    res = adamw_small(small_all, _pack_small(w), _pack_small(m), _pack_small(v))
    unpacked = [_unpack_small(a) for a in res]
    for n, _ in SMALL:
        out[n] = [u[n] for u in unpacked]

    _, from_chips = exchange_wait("chips_wait", chip_routes, flight["chips"], res[0])
    for n, (prt, _), rcv in zip(SHARDED, flight["parts"], from_chips):
        shape = w[n].shape
        res = adamw_sharded("adamw_" + n, prt, rcv, q_idx, w[n].reshape(shape[-2:]), m[n].reshape(shape[-2:]),
                            v[n].reshape(shape[-2:]))
        out[n] = [a.reshape(shape) for a in res]

    return (loss, grad_x.reshape(x.shape), *[out[n][0] for n in WEIGHT_ORDER], *[out[n][1] for n in WEIGHT_ORDER],
            *[out[n][2] for n in WEIGHT_ORDER], *[out[n][3] for n in WEIGHT_ORDER])
```

```python
import functools

import jax
import jax.numpy as jnp
from jax import lax
from jax.experimental import pallas as pl
from jax.experimental.pallas import tpu as pltpu

F32 = jnp.float32
_BF = jnp.bfloat16
HI = lax.Precision.HIGHEST

D_MODEL = 2048
HEADS = 8
NOPE = 128
ROPE = 64
QK = NOPE + ROPE
Q_LORA = 512
KV_LORA = 256
HEAD_PAD = 256
GDN_DIM = 128
WIDTH = HEADS * 128
CONV_W = 4
CHUNK = 64
ROPE_THETA = 10000.0
EPS = 1e-6
N_DEV = 8
LANE = 128
SUBLANE = 8
VMEM_LIMIT = 48 * 1024 * 1024

ADAM_LR, ADAM_B1, ADAM_B2, ADAM_EPS, ADAM_WD, ADAM_STEP = 0.001, 0.9, 0.999, 1e-08, 0.01, 10

P_MGATE, P_GQ, P_GK, P_GV, P_GGATE = 0, 1024, 2048, 3072, 4096
P_CQ, P_CKV, P_KR, P_GAB = 5120, 5632, 5888, 6016
P_COLS = 6144
R_SPLITS = (512, 256, 64, 1024, 1024, 1024, 1024, 8, 8, 1024)


def _cparams(*sem):
    return pltpu.CompilerParams(dimension_semantics=sem, vmem_limit_bytes=VMEM_LIMIT)


def _d_nn(a, b):
    return jnp.dot(a.astype(_BF), b.astype(_BF), preferred_element_type=F32)


def _d_nt(a, b):
    return lax.dot_general(a.astype(_BF), b.astype(_BF), (((1,), (1,)), ((), ())), preferred_element_type=F32)


def _d_tn(a, b):
    return lax.dot_general(a.astype(_BF), b.astype(_BF), (((0,), (0,)), ((), ())), preferred_element_type=F32)


@jax.custom_vjp
def _mm(a, b):
    return _d_nn(a, b)


_mm.defvjp(lambda a, b: (_d_nn(a, b), (a, b)), lambda r, g: (_d_nt(g, r[1]), _d_tn(r[0], g)))


@jax.custom_vjp
def _mm_nt(a, b):
    return _d_nt(a, b)


_mm_nt.defvjp(lambda a, b: (_d_nt(a, b), (a, b)), lambda r, g: (_d_nn(g, r[1]), _d_tn(g, r[0])))


@jax.custom_vjp
def _mm_tn(a, b):
    return _d_tn(a, b)


_mm_tn.defvjp(lambda a, b: (_d_tn(a, b), (a, b)), lambda r, g: (_d_nt(r[1], g), _d_nn(r[0], g)))


def _hi(a, b):
    return jnp.dot(a, b, preferred_element_type=F32, precision=HI)


_NN3 = (((2,), (1,)), ((0,), (0,)))
_NT3 = (((2,), (2,)), ((0,), (0,)))
_TN3 = (((1,), (1,)), ((0,), (0,)))


def _bdot(a, b, dims, hi):
    if hi:
        return lax.dot_general(a, b, dims, preferred_element_type=F32, precision=hi)
    return lax.dot_general(a.astype(_BF), b.astype(_BF), dims, preferred_element_type=F32)


def _batched_matmuls(hi):
    nn = jax.custom_vjp(lambda a, b: _bdot(a, b, _NN3, hi))
    nt = jax.custom_vjp(lambda a, b: _bdot(a, b, _NT3, hi))
    tn = jax.custom_vjp(lambda a, b: _bdot(a, b, _TN3, hi))
    nn.defvjp(lambda a, b: (_bdot(a, b, _NN3, hi), (a, b)),
              lambda r, g: (_bdot(g, r[1], _NT3, hi), _bdot(r[0], g, _TN3, hi)))
    nt.defvjp(lambda a, b: (_bdot(a, b, _NT3, hi), (a, b)),
              lambda r, g: (_bdot(g, r[1], _NN3, hi), _bdot(g, r[0], _TN3, hi)))
    tn.defvjp(lambda a, b: (_bdot(a, b, _TN3, hi), (a, b)),
              lambda r, g: (_bdot(r[1], g, _NT3, hi), _bdot(r[0], g, _NN3, hi)))
    return nn, nt, tn


_bmm, _bmm_nt, _bmm_tn = _batched_matmuls(False)
_bhi, _bhi_nt, _bhi_tn = _batched_matmuls(HI)


def _split2(x):
    hi = x.astype(_BF)
    return hi, (x - hi.astype(F32)).astype(_BF)


def _pdot(a, b, mode):
    (a_hi, a_lo), (b_hi, b_lo) = _split2(a), _split2(b)
    a_ax, b_ax, dims = {"nn": (2, 1, _NN3), "nt": (2, 2, _NT3), "tn": (1, 1, _TN3)}[mode]
    lhs = jnp.concatenate([a_hi, a_lo, a_hi], axis=a_ax)
    rhs = jnp.concatenate([b_hi, b_hi, b_lo], axis=b_ax)
    return lax.dot_general(lhs, rhs, dims, preferred_element_type=F32)


def _packed_matmuls():
    nn = jax.custom_vjp(lambda a, b: _pdot(a, b, "nn"))
    nn.defvjp(lambda a, b: (_pdot(a, b, "nn"), (a, b)), lambda r, g: (_pdot(g, r[1], "nt"), _pdot(r[0], g, "tn")))
    return nn


_bh3 = _packed_matmuls()
_bh3_passes = _batched_matmuls(lax.Precision.HIGH)[0]


@functools.partial(jax.custom_vjp, nondiff_argnums=(1, 2))
def _roll(x, shift, axis):
    return pltpu.roll(x, shift, axis)


def _roll_fwd(x, shift, axis):
    return pltpu.roll(x, shift, axis), None


def _roll_bwd(shift, axis, _, g):
    n = g.shape[axis]
    return (pltpu.roll(g, (n - shift) % n, axis),)


_roll.defvjp(_roll_fwd, _roll_bwd)


def _rms(x, gain):
    return x * lax.rsqrt(jnp.mean(x * x, axis=-1, keepdims=True) + EPS) * gain


MM_TILE = 1024
MM_DEPTH = 2048


def matmul(name, a, b, mode, after=None):
    if mode == "nn":
        (m, k), (k2, n) = a.shape, b.shape
    elif mode == "nt":
        (m, k), (n, k2) = a.shape, b.shape
    else:
        (k, m), (k2, n) = a.shape, b.shape
    assert k == k2, (name, a.shape, b.shape)
    tm, tn, tk = min(MM_TILE, m), min(MM_TILE, n), min(MM_DEPTH, k)
    assert m % tm == 0 and n % tn == 0 and k % tk == 0, (name, m, n, k)
    dot = {"nn": _d_nn, "nt": _d_nt, "tn": _d_tn}[mode]

    def body(a_ref, b_ref, *rest):
        o_ref = rest[-1]
        kk = pl.program_id(2)
        part = dot(a_ref[...], b_ref[...])

        @pl.when(kk == 0)
        def _():
            o_ref[...] = part

        @pl.when(kk != 0)
        def _():
            o_ref[...] += part

    if mode == "nn":
        a_spec = pl.BlockSpec((tm, tk), lambda j, i, kk: (i, kk))
        b_spec = pl.BlockSpec((tk, tn), lambda j, i, kk: (kk, j))
    elif mode == "nt":
        a_spec = pl.BlockSpec((tm, tk), lambda j, i, kk: (i, kk))
        b_spec = pl.BlockSpec((tn, tk), lambda j, i, kk: (j, kk))
    else:
        a_spec = pl.BlockSpec((tk, tm), lambda j, i, kk: (kk, i))
        b_spec = pl.BlockSpec((tk, tn), lambda j, i, kk: (kk, j))
    return pl.pallas_call(
        body,
        name=name,
        grid=(n // tn, m // tm, k // tk),
        in_specs=[a_spec, b_spec] + ([] if after is None else [pl.BlockSpec(memory_space=pl.ANY)]),
        out_specs=pl.BlockSpec((tm, tn), lambda j, i, kk: (i, j)),
        out_shape=jax.ShapeDtypeStruct((m, n), F32),
        compiler_params=_cparams("parallel", "parallel", "arbitrary"),
    )(*((a, b) if after is None else (a, b, after)))


def rowwise(name, fn, t_len, tile, row_in, full_in, row_out, acc_out=(), carries=(), reverse=False):
    tile = min(tile, t_len)
    n = t_len // tile
    assert t_len % tile == 0 and tile % SUBLANE == 0
    n_in, n_ro, n_acc, n_car = len(row_in) + len(full_in), len(row_out), len(acc_out), len(carries)

    def ti(i):
        return (n - 1 - i) if reverse else i

    in_specs, args = [], []
    for arr, kind in row_in:
        if kind[0] == "r":
            in_specs.append(pl.BlockSpec((tile, kind[1]), lambda i, c=kind[2]: (ti(i), c)))
        elif kind[0] == "h":
            in_specs.append(pl.BlockSpec((arr.shape[0], tile, arr.shape[2]), lambda i: (0, ti(i), 0)))
        else:
            in_specs.append(pl.BlockSpec(
                (SUBLANE, kind[1]), lambda i, c=kind[2]: (jnp.maximum(ti(i) * (tile // SUBLANE) - 1, 0), c)))
        args.append(arr)
    for arr in full_in:
        in_specs.append(pl.BlockSpec(arr.shape, lambda i, nd=arr.ndim: (0,) * nd))
        args.append(arr)
    out_specs, out_shape = [], []
    for kind in row_out:
        if kind[0] == "r":
            out_specs.append(pl.BlockSpec((tile, kind[1]), lambda i: (ti(i), 0)))
            out_shape.append(jax.ShapeDtypeStruct((t_len, kind[1]), kind[2]))
        else:
            out_specs.append(pl.BlockSpec((kind[1], tile, kind[2]), lambda i: (0, ti(i), 0)))
            out_shape.append(jax.ShapeDtypeStruct((kind[1], t_len, kind[2]), kind[3]))
    for shp in acc_out:
        out_specs.append(pl.BlockSpec(shp, lambda i, nd=len(shp): (0,) * nd))
        out_shape.append(jax.ShapeDtypeStruct(shp, F32))

    def body(*refs):
        in_refs = refs[:n_in]
        ro_refs = refs[n_in:n_in + n_ro]
        acc_refs = refs[n_in + n_ro:n_in + n_ro + n_acc]
        car_refs = refs[n_in + n_ro + n_acc:]
        step = pl.program_id(0)
        if n_car:
            @pl.when(step == 0)
            def _():
                for r in car_refs:
                    r[...] = jnp.zeros_like(r)
        vals = [r[...].astype(F32) for r in in_refs] + [r[...] for r in car_refs]
        outs = fn(ti(step), *vals)
        assert len(outs) == n_ro + n_acc + n_car, (name, len(outs))
        for r, o in zip(ro_refs, outs[:n_ro]):
            r[...] = o.astype(r.dtype)
        for r, o in zip(acc_refs, outs[n_ro:n_ro + n_acc]):
            @pl.when(step == 0)
            def _(r=r, o=o):
                r[...] = o

            @pl.when(step != 0)
            def _(r=r, o=o):
                r[...] += o
        for r, o in zip(car_refs, outs[n_ro + n_acc:]):
            r[...] = o

    res = pl.pallas_call(
        body,
        name=name,
        grid=(n,),
        in_specs=in_specs,
        out_specs=out_specs,
        out_shape=out_shape,
        scratch_shapes=[pltpu.VMEM(s, F32) for s in carries],
        compiler_params=_cparams("arbitrary"),
    )(*args)
    return list(res)


def _vjp_fn(fn, n_diff, n_out):
    def g(i, *a):
        ins, cts = a[:len(a) - n_out], a[len(a) - n_out:]
        diff, rest = ins[:n_diff], ins[n_diff:]
        _, pull = jax.vjp(lambda *d: tuple(fn(i, *d, *rest)), *diff)
        return tuple(pull(tuple(cts)))

    return g


def f_rms_x(i, x, gain):
    return (_rms(x, gain),)


def f_lat(i, cq, ckv, gq, gkv):
    return _rms(cq, gq), _rms(ckv, gkv)


def _rope_tables(pos, invf):
    ang = pos * invf
    lane = lax.broadcasted_iota(jnp.int32, (1, LANE), 1)
    cosv, sinv = jnp.cos(ang), jnp.sin(ang)
    half = ROPE // 2
    c = jnp.where(lane < ROPE, cosv, 0.0)
    sa = jnp.where(lane < half, -sinv, 0.0)
    sb = jnp.where((lane >= half) & (lane < ROPE), sinv, 0.0)
    return c, sa, sb


def _rope(xh, tabs):
    c, sa, sb = tabs
    half = ROPE // 2
    return xh * c + _roll(xh, LANE - half, 1) * sa + _roll(xh, half, 1) * sb


def f_head(i, q_raw, kv_raw, kr, qg, kg, pos, invf):
    tabs = _rope_tables(pos, invf)
    qs, ks, vs = [], [], []
    kr_ss = jnp.sum(kr * kr, axis=-1, keepdims=True)
    for h in range(HEADS):
        lo = q_raw[:, HEAD_PAD * h:HEAD_PAD * h + NOPE]
        hi = q_raw[:, HEAD_PAD * h + NOPE:HEAD_PAD * (h + 1)]
        ss = jnp.sum(lo * lo, axis=-1, keepdims=True) + jnp.sum(hi * hi, axis=-1, keepdims=True)
        r = lax.rsqrt(ss * (1.0 / QK) + EPS)
        qs.append(jnp.concatenate([lo * r * qg[:, :NOPE], _rope(hi * r * qg[:, NOPE:], tabs)], axis=1))
        lo = kv_raw[:, 2 * NOPE * h:2 * NOPE * h + NOPE]
        ss = jnp.sum(lo * lo, axis=-1, keepdims=True) + kr_ss
        r = lax.rsqrt(ss * (1.0 / QK) + EPS)
        ks.append(jnp.concatenate([lo * r * kg[:, :NOPE], _rope(kr * r * kg[:, NOPE:], tabs)], axis=1))
        vs.append(kv_raw[:, 2 * NOPE * h + NOPE:2 * NOPE * (h + 1)])
    return jnp.stack(qs), jnp.stack(ks), jnp.stack(vs)


def f_mix(i, o_mla, mgate, o_gdn, ggate, og):
    parts = [o_mla * jax.nn.silu(mgate)]
    for h in range(HEADS):
        parts.append(_rms(o_gdn[h], og) * jax.nn.silu(ggate[:, LANE * h:LANE * (h + 1)]))
    return (jnp.concatenate(parts, axis=1),)


def _row(a, j):
    rows = lax.broadcasted_iota(jnp.int32, a.shape, 0)
    return jnp.sum(jnp.where(rows == j, a, 0.0), axis=0, keepdims=True)


def _shift_rows(x, halo, d):
    xs = _roll(x, d, 0)
    hs = _roll(halo, d, 0)
    r8 = lax.broadcasted_iota(jnp.int32, hs.shape, 0)
    top = jnp.where(r8 < d, hs, xs[:SUBLANE])
    return jnp.concatenate([top, xs[SUBLANE:]], axis=0)


def _conv_silu(x, halo, w):
    y = _row(w, CONV_W - 1) * x
    for j in range(CONV_W - 1):
        y = y + _row(w, j) * _shift_rows(x, halo, CONV_W - 1 - j)
    return jax.nn.silu(y)


def _head_select(offset):
    r = lax.broadcasted_iota(jnp.int32, (LANE, WIDTH), 0)
    c = lax.broadcasted_iota(jnp.int32, (LANE, WIDTH), 1)
    return (r == offset + lax.shift_right_logical(c, 7)).astype(_BF)


def _split3(x):
    x1 = x.astype(_BF)
    r1 = x - x1.astype(F32)
    x2 = r1.astype(_BF)
    return x1, x2, (r1 - x2.astype(F32)).astype(_BF)


@jax.custom_vjp
def _spread(x, sel):
    return _d_nn(jnp.concatenate(_split3(x), axis=1), jnp.concatenate([sel, sel, sel], axis=0))


def _spread_fwd(x, sel):
    return _spread(x, sel), sel


def _spread_bwd(sel, g):
    g1, g2, g3 = _split3(g)
    return _d_nt(g1, sel) + _d_nt(g2, sel) + _d_nt(g3, sel), jnp.zeros_like(sel)


_spread.defvjp(_spread_fwd, _spread_bwd)


def f_gdn_pre(i, gq, gk, gv, gab, hq, hk, hv, cwq, cwk, cwv, alog, dtb):
    live = jnp.where(i == 0, 0.0, 1.0)
    q = _conv_silu(gq, hq * live, cwq)
    k = _conv_silu(gk, hk * live, cwk)
    v = _conv_silu(gv, hv * live, cwv)
    g = _spread(-jnp.exp(alog) * jax.nn.softplus(gab + dtb), _head_select(0))
    beta = _spread(jax.nn.sigmoid(gab), _head_select(HEADS))
    qs, ks, vs, gs, bs = [], [], [], [], []
    for h in range(HEADS):
        sl = slice(LANE * h, LANE * (h + 1))
        qh, kh = q[:, sl], k[:, sl]
        qs.append(qh * lax.rsqrt(jnp.sum(qh * qh, axis=-1, keepdims=True) + EPS) * (GDN_DIM ** -0.5))
        ks.append(kh * lax.rsqrt(jnp.sum(kh * kh, axis=-1, keepdims=True) + EPS))
        vs.append(v[:, sl])
        gs.append(g[:, sl])
        bs.append(beta[:, sl])
    return jnp.stack(qs), jnp.stack(ks), jnp.stack(vs), jnp.stack(gs), jnp.stack(bs)


def gdn_pre_bwd(i, gq, gk, gv, gab, hq, hk, hv, dq, dk, dv, dg, db, cwq, cwk, cwv, alog, dtb, cq, ck, cv):
    grads = _vjp_fn(f_gdn_pre, 12, 5)(i, gq, gk, gv, gab, hq, hk, hv, cwq, cwk, cwv, alog, dtb, dq, dk, dv, dg, db)
    dgq, dgk, dgv, dgab, dhq, dhk, dhv, dcwq, dcwk, dcwv, dalog, ddtb = grads

    def add_tail(dx, carry):
        return jnp.concatenate([dx[:-SUBLANE], dx[-SUBLANE:] + carry], axis=0)

    return (add_tail(dgq, cq), add_tail(dgk, ck), add_tail(dgv, cv), dgab,
            dcwq, dcwk, dcwv, dalog, ddtb, dhq, dhk, dhv)


def f_loss(i, x, h, tgt):
    e = x + h - tgt
    part = 0.5 * jnp.sum(e * e) * (1.0 / D_MODEL)
    dy = e * (1.0 / D_MODEL)
    return dy, dy, jnp.zeros((SUBLANE, LANE), F32) + part


def _flash_tile(t_len):
    return min(512, t_len)


FLASH_HEADS = 4
FLASH_BWD_HEADS = 2
LOG2E = 1.4426950408889634


def _causal(rows0, shape):
    r = rows0 + lax.broadcasted_iota(jnp.int32, shape, 0)
    c = lax.broadcasted_iota(jnp.int32, shape, 1)
    return c <= r


def flash_fwd(q, k, v):
    h_n, t_len, _ = q.shape
    tq = _flash_tile(t_len)
    nq = t_len // tq
    hb = FLASH_HEADS
    kw = 2 if nq % 2 == 0 else 1
    tk = kw * tq
    c2 = (QK ** -0.5) * LOG2E
    pairs = [(i, j) for i in range(nq) for j in range(i // kw + 1)]
    qt = jnp.array([p[0] for p in pairs], jnp.int32)
    kt = jnp.array([p[1] for p in pairs], jnp.int32)

    def body(qt_ref, kt_ref, q_ref, k_ref, v_ref, o_ref, lse_ref, m_s, acc_s):
        step = pl.program_id(1)
        qi, kj = qt_ref[step], kt_ref[step]
        last = qi // kw

        @pl.when(kj == 0)
        def _():
            m_s[...] = jnp.full_like(m_s, -jnp.inf)
            acc_s[...] = jnp.zeros_like(acc_s)

        def tile(diagonal):
            s = _bdot(q_ref[...], k_ref[...], _NT3, False) * c2
            if diagonal:
                s = jnp.where(_causal((qi % kw) * tq, (tq, tk))[None], s, -jnp.inf)
            m_old = m_s[...]
            m_new = jnp.maximum(m_old, jnp.max(s, axis=-1, keepdims=True))
            p = jnp.exp2(s - m_new).astype(_BF)
            v_ones = jnp.concatenate([v_ref[...], jnp.ones((hb, tk, LANE), _BF)], axis=2)
            acc_s[...] = jnp.exp2(m_old - m_new) * acc_s[...] + _bdot(p, v_ones, _NN3, False)
            m_s[...] = m_new

        @pl.when(kj < last)
        def _():
            tile(False)

        @pl.when(kj == last)
        def _():
            tile(True)
            acc = acc_s[...]
            l_sum = acc[:, :, LANE:]
            o = acc[:, :, :LANE] / l_sum
            for hh in range(hb):
                o_ref[:, LANE * hh:LANE * (hh + 1)] = o[hh]
            lse_ref[...] = m_s[...] + jnp.log2(jnp.max(l_sum, axis=-1, keepdims=True))

    return pl.pallas_call(
        body,
        name="flash_fwd",
        grid_spec=pltpu.PrefetchScalarGridSpec(
            num_scalar_prefetch=2,
            grid=(h_n // hb, qt.shape[0]),
            in_specs=[
                pl.BlockSpec((hb, tq, HEAD_PAD), lambda h, s, qt_ref, kt_ref: (h, qt_ref[s], 0)),
                pl.BlockSpec((hb, tk, HEAD_PAD), lambda h, s, qt_ref, kt_ref: (h, kt_ref[s], 0)),
                pl.BlockSpec((hb, tk, LANE), lambda h, s, qt_ref, kt_ref: (h, kt_ref[s], 0)),
            ],
            out_specs=[
                pl.BlockSpec((tq, hb * LANE), lambda h, s, qt_ref, kt_ref: (qt_ref[s], h)),
                pl.BlockSpec((hb, tq, 1), lambda h, s, qt_ref, kt_ref: (h, qt_ref[s], 0)),
            ],
            scratch_shapes=[pltpu.VMEM((hb, tq, 1), F32), pltpu.VMEM((hb, tq, 2 * LANE), F32)],
        ),
        out_shape=[jax.ShapeDtypeStruct((t_len, h_n * LANE), F32), jax.ShapeDtypeStruct((h_n, t_len, 1), F32)],
        compiler_params=_cparams("parallel", "arbitrary"),
    )(qt, kt, q, k, v)


def flash_bwd(q, k, v, do, lse, delta):
    h_n, t_len, _ = q.shape
    tq = _flash_tile(t_len)
    nq = t_len // tq
    hb = FLASH_BWD_HEADS
    kw = 2 if nq % 2 == 0 else 1
    tk = kw * tq
    pairs = [(i, j) for j in range(nq // kw) for i in range(kw * j, nq)]
    n_steps = len(pairs)
    qt = jnp.array([p[0] for p in pairs], jnp.int32)
    kt = jnp.array([p[1] for p in pairs], jnp.int32)
    scale = QK ** -0.5
    c2 = scale * LOG2E

    def body(qt_ref, kt_ref, q_ref, k_ref, v_ref, do_ref, lse_ref, dl_ref, dq_hbm, dk_ref, dv_ref, dq_s, dq_sem):
        group, step = pl.program_id(0), pl.program_id(1)
        qi, kj = qt_ref[step], kt_ref[step]

        @pl.when(step == 0)
        def _():
            dq_s[...] = jnp.zeros_like(dq_s)

        def tile(diagonal):
            qb, kb = q_ref[...], k_ref[...]
            dob = jnp.stack([do_ref[:, LANE * hh:LANE * (hh + 1)] for hh in range(hb)])
            p = jnp.exp2(_bdot(qb, kb, _NT3, False) * c2 - lse_ref[...])
            if diagonal:
                p = jnp.where(_causal((qi % kw) * tq, (tq, tk))[None], p, 0.0)
            dv = _bdot(p, dob, _TN3, False)
            ds = p * (_bdot(dob, v_ref[...], _NT3, False) - dl_ref[...]) * scale
            dk = _bdot(ds, qb, _TN3, False)
            dq_s[:, pl.ds(pl.multiple_of(qi * tq, tq), tq), :] += _bdot(ds, kb, _NN3, False)
            return dk, dv

        @pl.when(qi == kw * kj)
        def _():
            dk_ref[...], dv_ref[...] = tile(True)

        @pl.when((qi != kw * kj) & (qi // kw == kj))
        def _():
            dk, dv = tile(True)
            dk_ref[...] += dk
            dv_ref[...] += dv

        @pl.when(qi // kw > kj)
        def _():
            dk, dv = tile(False)
            dk_ref[...] += dk
            dv_ref[...] += dv

        @pl.when(step == n_steps - 1)
        def _():
            out = pltpu.make_async_copy(dq_s, dq_hbm.at[pl.ds(group * hb, hb)], dq_sem)
            out.start()
            out.wait()

    def qmap(h, s, qt_ref, kt_ref):
        return (h, qt_ref[s], 0)

    def kmap(h, s, qt_ref, kt_ref):
        return (h, kt_ref[s], 0)

    return pl.pallas_call(
        body,
        name="flash_bwd",
        grid_spec=pltpu.PrefetchScalarGridSpec(
            num_scalar_prefetch=2,
            grid=(h_n // hb, n_steps),
            in_specs=[
                pl.BlockSpec((hb, tq, HEAD_PAD), qmap),
                pl.BlockSpec((hb, tk, HEAD_PAD), kmap),
                pl.BlockSpec((hb, tk, LANE), kmap),
                pl.BlockSpec((tq, hb * LANE), lambda h, s, qt_ref, kt_ref: (qt_ref[s], h)),
                pl.BlockSpec((hb, tq, 1), qmap),
                pl.BlockSpec((hb, tq, 1), qmap),
            ],
            out_specs=[
                pl.BlockSpec(memory_space=pl.ANY),
                pl.BlockSpec((hb, tk, HEAD_PAD), kmap),
                pl.BlockSpec((hb, tk, LANE), kmap),
            ],
            scratch_shapes=[pltpu.VMEM((hb, t_len, HEAD_PAD), F32), pltpu.SemaphoreType.DMA],
        ),
        out_shape=[
            jax.ShapeDtypeStruct((h_n, t_len, HEAD_PAD), F32),
            jax.ShapeDtypeStruct((h_n, t_len, HEAD_PAD), F32),
            jax.ShapeDtypeStruct((h_n, t_len, LANE), F32),
        ],
        compiler_params=_cparams("parallel", "arbitrary"),
    )(qt, kt, q, k, v, do, lse, delta)


def _tri_ones(h_n):
    ii = lax.broadcasted_iota(jnp.int32, (h_n, CHUNK, CHUNK), 1)
    jj = lax.broadcasted_iota(jnp.int32, (h_n, CHUNK, CHUNK), 2)
    return (ii >= jj).astype(_BF)


@jax.custom_vjp
def _chunk_cumsum(gb):
    tri = _tri_ones(gb.shape[0])
    return _bdot(jnp.concatenate([tri, tri, tri], axis=2), jnp.concatenate(_split3(gb), axis=1), _NN3, False)


def _chunk_cumsum_bwd(_, ct):
    tri = _tri_ones(ct.shape[0])
    return (_bdot(jnp.concatenate([tri, tri, tri], axis=1), jnp.concatenate(_split3(ct), axis=1), _TN3, False),)


_chunk_cumsum.defvjp(lambda gb: (_chunk_cumsum(gb), None), _chunk_cumsum_bwd)


@jax.custom_vjp
def _pair_diff(gcb):
    g1, g2, g3 = _split3(gcb)
    lane = lax.broadcasted_iota(jnp.int32, (1, 1, LANE), 2)
    one, zero = jnp.ones((), _BF), jnp.zeros((), _BF)
    a = jnp.where(lane == 0, g1, jnp.where(lane == 1, g2, jnp.where(lane == 2, g3, jnp.where(lane < 6, one, zero))))
    b = jnp.where(lane < 3, one, jnp.where(lane == 3, -g1, jnp.where(lane == 4, -g2, jnp.where(lane == 5, -g3, zero))))
    return _bdot(a, b, _NT3, False)


def _pair_diff_bwd(_, ct):
    parts = _split3(ct)
    ones = jnp.ones((ct.shape[0], 3 * CHUNK, LANE), _BF)
    rows = _bdot(jnp.concatenate(parts, axis=2), ones, _NN3, False)
    cols = _bdot(jnp.concatenate(parts, axis=1), ones, _TN3, False)
    lane = lax.broadcasted_iota(jnp.int32, (1, 1, LANE), 2)
    return (jnp.where(lane == 0, rows - cols, 0.0),)


_pair_diff.defvjp(lambda gcb: (_pair_diff(gcb), None), _pair_diff_bwd)


@jax.custom_vjp
def _saved_inverse(lmat, inv):
    return inv


def _saved_inverse_bwd(inv, g):
    return -_pdot(_pdot(inv, g, "tn"), inv, "nt"), jnp.zeros_like(inv)


_saved_inverse.defvjp(lambda lmat, inv: (inv, inv), _saved_inverse_bwd)


def gdn_step(s, q, k, v, gb, bb, inv_saved=None):
    c = CHUNK
    ii = lax.broadcasted_iota(jnp.int32, (1, c, c), 1)
    jj = lax.broadcasted_iota(jnp.int32, (1, c, c), 2)
    incl, strict = ii >= jj, ii > jj
    gcb = _chunk_cumsum(gb)
    diff = _pair_diff(gcb)
    decay = jnp.where(incl, jnp.exp(jnp.where(incl, diff, 0.0)), 0.0)
    kb, vb = k * bb, v * bb
    egc = jnp.exp(gcb)
    lmat = jnp.where(strict, _bmm_nt(kb, k) * decay, 0.0)
    if inv_saved is None:
        mm3 = _bh3_passes
        inv = (ii == jj).astype(F32) - lmat
        pw = mm3(lmat, lmat)
        for step in range(5):
            inv = inv + mm3(inv, pw)
            if step < 4:
                pw = mm3(pw, pw)
    else:
        mm3 = _bh3
        inv = _saved_inverse(lmat, inv_saved)
    u = mm3(inv, vb)
    w = mm3(inv, kb * egc)
    attn = _bmm_nt(q, k) * decay
    qd = q * egc
    g_end = jnp.sum(gb, axis=1, keepdims=True)
    kd = k * jnp.exp(g_end - gcb)
    v_new = u - _bmm(w, s)
    o = _bmm(qd, s) + _bmm(attn, v_new)
    s_new = s * jnp.exp(g_end) + _bmm_tn(kd, v_new)
    return s_new, o, inv


def gdn_fwd(q, k, v, gb, bb):
    h_n, t_len, d = q.shape
    n = t_len // CHUNK
    blk = pl.BlockSpec((h_n, CHUNK, d), lambda i: (0, i, 0))

    def body(q_ref, k_ref, v_ref, g_ref, b_ref, o_ref, sall_ref, inv_ref, s_s):
        @pl.when(pl.program_id(0) == 0)
        def _():
            s_s[...] = jnp.zeros_like(s_s)

        s = s_s[...]
        sall_ref[0] = s
        s_s[...], o_ref[...], inv_ref[0] = gdn_step(s, q_ref[...], k_ref[...], v_ref[...], g_ref[...], b_ref[...])

    return pl.pallas_call(
        body,
        name="gdn_fwd",
        grid=(n,),
        in_specs=[blk] * 5,
        out_specs=[blk, pl.BlockSpec((1, h_n, d, d), lambda i: (i, 0, 0, 0)),
                   pl.BlockSpec((1, h_n, CHUNK, CHUNK), lambda i: (i, 0, 0, 0))],
        out_shape=[jax.ShapeDtypeStruct((h_n, t_len, d), F32), jax.ShapeDtypeStruct((n, h_n, d, d), F32),
                   jax.ShapeDtypeStruct((n, h_n, CHUNK, CHUNK), F32)],
        scratch_shapes=[pltpu.VMEM((h_n, d, d), F32)],
        compiler_params=_cparams("arbitrary"),
    )(q, k, v, gb, bb)


def gdn_bwd(q, k, v, gb, bb, s_all, inv_all, do):
    h_n, t_len, d = q.shape
    n = t_len // CHUNK
    blk = pl.BlockSpec((h_n, CHUNK, d), lambda i: (0, n - 1 - i, 0))

    def body(q_ref, k_ref, v_ref, g_ref, b_ref, sall_ref, inv_ref, do_ref, dq_ref, dk_ref, dv_ref, dg_ref, db_ref,
             ds_s):
        @pl.when(pl.program_id(0) == 0)
        def _():
            ds_s[...] = jnp.zeros_like(ds_s)

        inv = inv_ref[0]
        _, pull = jax.vjp(lambda *a: gdn_step(*a, inv_saved=inv)[:2], sall_ref[0], q_ref[...], k_ref[...], v_ref[...],
                          g_ref[...], b_ref[...])
        ds_s[...], dq_ref[...], dk_ref[...], dv_ref[...], dg_ref[...], db_ref[...] = pull((ds_s[...], do_ref[...]))

    return pl.pallas_call(
        body,
        name="gdn_bwd",
        grid=(n,),
        in_specs=[blk] * 5 + [pl.BlockSpec((1, h_n, d, d), lambda i: (n - 1 - i, 0, 0, 0)),
                              pl.BlockSpec((1, h_n, CHUNK, CHUNK), lambda i: (n - 1 - i, 0, 0, 0)), blk],
        out_specs=[blk] * 5,
        out_shape=[jax.ShapeDtypeStruct((h_n, t_len, d), F32)] * 5,
        scratch_shapes=[pltpu.VMEM((h_n, d, d), F32)],
        compiler_params=_cparams("arbitrary"),
    )(q, k, v, gb, bb, s_all, inv_all, do)


def _pad_cols(a, n):
    return jnp.pad(a, ((0, 0), (0, n - a.shape[1])))


def arrange_w_in(w):
    pieces, start = [], 0
    for n in R_SPLITS:
        pieces.append(w[:, start:start + n])
        start += n
    cq, ckv, kr, mgate, gq, gk, gv, ga, gb, ggate = pieces
    return jnp.concatenate([mgate, gq, gk, gv, ggate, cq, ckv, _pad_cols(kr, LANE),
                            _pad_cols(jnp.concatenate([ga, gb], axis=1), LANE)], axis=1)


def unarrange_w_in(g):
    def cols(start, n):
        return g[:, start:start + n]
    return jnp.concatenate([cols(P_CQ, Q_LORA), cols(P_CKV, KV_LORA), cols(P_KR, ROPE), cols(P_MGATE, WIDTH),
                            cols(P_GQ, WIDTH), cols(P_GK, WIDTH), cols(P_GV, WIDTH), cols(P_GAB, HEADS),
                            cols(P_GAB + HEADS, HEADS), cols(P_GGATE, WIDTH)], axis=1)


def arrange_w_uq(w):
    w = w.reshape(w.shape[0], HEADS, QK)
    return jnp.pad(w, ((0, 0), (0, 0), (0, HEAD_PAD - QK))).reshape(w.shape[0], HEADS * HEAD_PAD)


def unarrange_w_uq(g):
    return g.reshape(g.shape[0], HEADS, HEAD_PAD)[:, :, :QK].reshape(g.shape[0], HEADS * QK)


def local_step(x, pos, tgt, p, on_early_grads=None, on_d_proj=None, on_weight_grads=None, on_d_xn=None,
               first_after=None, late_weights=None):
    t_len = x.shape[0]
    w_in = p["w_in"]
    norm_gain = p["norm_gain"].reshape(1, D_MODEL)
    qa_gain = p["mla_q_a_gain"].reshape(1, Q_LORA)
    kva_gain = p["mla_kv_a_gain"].reshape(1, KV_LORA)
    qg = _pad_cols(p["mla_q_norm_gain"].reshape(1, QK), HEAD_PAD)
    kg = _pad_cols(p["mla_k_norm_gain"].reshape(1, QK), HEAD_PAD)
    alog = _pad_cols(p["gdn_a_log"].reshape(1, HEADS), LANE)
    dtb = _pad_cols(p["gdn_dt_bias"].reshape(1, HEADS), LANE)
    og = p["gdn_out_norm_gain"].reshape(1, GDN_DIM)
    half = ROPE // 2
    inv_freq = jnp.power(ROPE_THETA, -jnp.arange(half, dtype=F32) / half)
    invf = _pad_cols(jnp.concatenate([inv_freq, inv_freq]).reshape(1, ROPE), LANE)

    rt = 256
    r = "r"
    first_after = jnp.zeros((SUBLANE, LANE), F32) if first_after is None else first_after
    (xn,) = rowwise("rms_x", lambda i, x_, gain_, after_: f_rms_x(i, x_, gain_), t_len, rt, [(x, (r, D_MODEL, 0))],
                    [norm_gain, first_after], [(r, D_MODEL, _BF)])
    proj = matmul("proj", xn, w_in, "nn")
    if late_weights is not None:
        p = {**p, **late_weights(proj)}
    w_uq, w_ukv, w_out = p["w_uq"], p["w_ukv"], p["w_out"]
    cw = p["gdn_conv_w"].reshape(CONV_W, 3 * WIDTH)
    cwq, cwk, cwv = cw[:, :WIDTH], cw[:, WIDTH:2 * WIDTH], cw[:, 2 * WIDTH:]
    cq_in = (proj, (r, Q_LORA, P_CQ // Q_LORA))
    ckv_in = (proj, (r, KV_LORA, P_CKV // KV_LORA))
    kr_in = (proj, (r, LANE, P_KR // LANE))
    mgate_in = (proj, (r, WIDTH, P_MGATE // WIDTH))
    ggate_in = (proj, (r, WIDTH, P_GGATE // WIDTH))
    gqkv_in = [(proj, (r, WIDTH, P_GQ // WIDTH)), (proj, (r, WIDTH, P_GK // WIDTH)), (proj, (r, WIDTH, P_GV // WIDTH))]
    gab_in = (proj, (r, LANE, P_GAB // LANE))
    halos = [(proj, ("halo", WIDTH, P_GQ // WIDTH)), (proj, ("halo", WIDTH, P_GK // WIDTH)),
             (proj, ("halo", WIDTH, P_GV // WIDTH))]

    q_lat, kv_lat = rowwise("lat", f_lat, t_len, rt, [cq_in, ckv_in], [qa_gain, kva_gain],
                            [(r, Q_LORA, _BF), (r, KV_LORA, _BF)])
    q_raw = matmul("q_up", q_lat, w_uq, "nn")
    kv_raw = matmul("kv_up", kv_lat, w_ukv, "nn")
    wide = HEADS * HEAD_PAD
    head_in = [(q_raw, (r, wide, 0)), (kv_raw, (r, wide, 0)), kr_in]
    pos_in = (pos, (r, 1, 0))
    q_full, k_full, v_mla = rowwise(
        "head", lambda i, qr, kvr, kr, ps, qg_, kg_, iv: f_head(i, qr, kvr, kr, qg_, kg_, ps, iv), t_len, rt,
        head_in + [pos_in], [qg, kg, invf],
        [("h", HEADS, HEAD_PAD, _BF), ("h", HEADS, HEAD_PAD, _BF), ("h", HEADS, LANE, _BF)])
    o_mla, lse = flash_fwd(q_full, k_full, v_mla)

    pre_in = gqkv_in + [gab_in] + halos
    pre_full = [cwq, cwk, cwv, alog, dtb]
    hkind = ("h", HEADS, GDN_DIM, F32)
    gq_n, gk_n, gv_n, g_b, b_b = rowwise("gdn_pre", f_gdn_pre, t_len, rt, pre_in, pre_full, [hkind] * 5)
    o_gdn, s_all, inv_all = gdn_fwd(gq_n, gk_n, gv_n, g_b, b_b)

    mix_in = [(o_mla, (r, WIDTH, 0)), mgate_in, (o_gdn, ("h",)), ggate_in]
    (mixed,) = rowwise("mix", f_mix, t_len, rt, mix_in, [og], [(r, 2 * WIDTH, _BF)])
    h_out = matmul("out_proj", mixed, w_out, "nn")
    dy, dy_mx, loss_acc = rowwise("loss", f_loss, t_len, rt,
                                  [(x, (r, D_MODEL, 0)), (h_out, (r, D_MODEL, 0)), (tgt, (r, D_MODEL, 0))], [],
                                  [(r, D_MODEL, F32), (r, D_MODEL, _BF)], [(SUBLANE, LANE)])
    loss = loss_acc[0, 0]

    d_mixed = matmul("d_mixed", dy_mx, w_out, "nt")
    g_w_out = matmul("g_w_out", mixed, dy_mx, "tn")

    def mix_bwd(i, o_mla_, mgate_, o_gdn_, ggate_, d_mixed_, og_):
        do_mla_, d_mgate_, do_gdn_, d_ggate_, g_og_ = _vjp_fn(f_mix, 5, 1)(i, o_mla_, mgate_, o_gdn_, ggate_, og_, d_mixed_)
        delta_ = jnp.stack([jnp.sum(o_mla_[:, LANE * h:LANE * (h + 1)] * do_mla_[:, LANE * h:LANE * (h + 1)],
                                    axis=-1, keepdims=True) for h in range(HEADS)])
        return do_mla_, d_mgate_, do_gdn_, d_ggate_, delta_, g_og_

    do_mla, d_mgate, do_gdn, d_ggate, delta, g_og = rowwise(
        "mix_bwd", mix_bwd, t_len, rt, mix_in + [(d_mixed, (r, 2 * WIDTH, 0))], [og],
        [(r, WIDTH, F32), (r, WIDTH, _BF), hkind, (r, WIDTH, _BF), ("h", HEADS, 1, F32)], [(1, GDN_DIM)])
    dq_n, dk_n, dv_n, dg_b, db_b = gdn_bwd(gq_n, gk_n, gv_n, g_b, b_b, s_all, inv_all, do_gdn)
    cts_in = [(a, ("h",)) for a in (dq_n, dk_n, dv_n, dg_b, db_b)]
    d_gq, d_gk, d_gv, d_gab, g_cwq, g_cwk, g_cwv, g_alog, g_dtb = rowwise(
        "gdn_pre_bwd", gdn_pre_bwd, t_len, rt, pre_in + cts_in, pre_full,
        [(r, WIDTH, _BF)] * 3 + [(r, LANE, _BF)],
        [(CONV_W, WIDTH)] * 3 + [(1, LANE)] * 2, carries=[(SUBLANE, WIDTH)] * 3, reverse=True)

    dq_full, dk_full, dv_mla = flash_bwd(q_full, k_full, v_mla, do_mla, lse, delta)
    head_cts = [(a, ("h",)) for a in (dq_full, dk_full, dv_mla)]

    def head_bwd(i, q_raw_, kv_raw_, kr_, pos_, dq_, dk_, dv_, qg_, kg_, invf_):
        return _vjp_fn(f_head, 5, 3)(i, q_raw_, kv_raw_, kr_, qg_, kg_, pos_, invf_, dq_, dk_, dv_)

    dq_raw, dkv_raw, d_kr, g_qg, g_kg = rowwise(
        "head_bwd", head_bwd, t_len, rt, head_in + [pos_in] + head_cts, [qg, kg, invf],
        [(r, wide, _BF), (r, wide, _BF), (r, LANE, _BF)], [(1, HEAD_PAD), (1, HEAD_PAD)])
    dq_lat = matmul("dq_lat", dq_raw, w_uq, "nt")
    g_w_uq = matmul("g_w_uq", q_lat, dq_raw, "tn")
    dkv_lat = matmul("dkv_lat", dkv_raw, w_ukv, "nt")
    g_w_ukv = matmul("g_w_ukv", kv_lat, dkv_raw, "tn")
    grads = {
        "w_uq": g_w_uq, "w_ukv": g_w_ukv, "gdn_conv_w": jnp.concatenate([g_cwq, g_cwk, g_cwv], axis=1),
        "w_out": g_w_out, "mla_q_norm_gain": g_qg[:, :QK], "mla_k_norm_gain": g_kg[:, :QK],
        "gdn_a_log": g_alog[:, :HEADS], "gdn_dt_bias": g_dtb[:, :HEADS], "gdn_out_norm_gain": g_og,
    }
    after = jnp.zeros((SUBLANE, LANE), F32) if on_early_grads is None else on_early_grads(grads)

    def lat_bwd(i, cq_, ckv_, dql_, dkl_, gq_, gkv_, after_):
        return _vjp_fn(f_lat, 4, 2)(i, cq_, ckv_, gq_, gkv_, dql_, dkl_)

    d_cq, d_ckv, grads["mla_q_a_gain"], grads["mla_kv_a_gain"] = rowwise(
        "lat_bwd", lat_bwd, t_len, rt, [cq_in, ckv_in, (dq_lat, (r, Q_LORA, 0)), (dkv_lat, (r, KV_LORA, 0))],
        [qa_gain, kva_gain, after], [(r, Q_LORA, _BF), (r, KV_LORA, _BF)], [(1, Q_LORA), (1, KV_LORA)])

    d_proj = jnp.concatenate([d_mgate, d_gq, d_gk, d_gv, d_ggate, d_cq, d_ckv, d_kr, d_gab], axis=1)
    after = None if on_d_proj is None else on_d_proj(d_proj)
    grads["w_in"] = matmul("g_w_in", xn, d_proj, "tn", after=after)
    after = None if on_weight_grads is None else on_weight_grads(grads)
    d_xn = matmul("d_xn", d_proj, w_in, "nt", after=after)
    after = jnp.zeros((SUBLANE, LANE), F32) if on_d_xn is None else on_d_xn(d_xn)

    def rms_x_bwd(i, x_, dxn_, dy_, gain_, after_):
        dx, dgain = _vjp_fn(f_rms_x, 2, 1)(i, x_, gain_, dxn_)
        return dx + dy_, dgain

    grad_x, grads["norm_gain"] = rowwise(
        "rms_x_bwd", rms_x_bwd, t_len, rt, [(x, (r, D_MODEL, 0)), (d_xn, (r, D_MODEL, 0)), (dy, (r, D_MODEL, 0))],
        [norm_gain, after], [(r, D_MODEL, F32)], [(1, D_MODEL)])
    return loss, grad_x, grads


MESH = pl.DeviceIdType.MESH
ANY = pl.BlockSpec(memory_space=pl.ANY)
CHIP_FLIPS = ((1, 0), (0, 1), (1, 1))


def _place():
    return lax.axis_index("x"), lax.axis_index("y"), lax.axis_index("c")


def _flip(v, f):
    return 1 - v if f else v


def all_gather(shards):
    n_arr = len(shards)

    def body(*refs):
        x_refs, o_refs = refs[:n_arr], refs[n_arr:2 * n_arr]
        send_sems, recv_sems, local_sems = refs[2 * n_arr:]
        x, y, c = _place()
        me, sibling = (x, y, c), (x, y, 1 - c)
        chips = [(_flip(x, fx), _flip(y, fy)) for fx, fy in CHIP_FLIPS]

        def copy(a, k, block, to, src=None):
            px, py, pc = block
            dst = o_refs[a].at[4 * px + 2 * py + pc]
            return pltpu.make_async_remote_copy(
                src_ref=dst if src is None else src, dst_ref=dst, send_sem=send_sems.at[a, k],
                recv_sem=recv_sems.at[a, k], device_id=to, device_id_type=MESH)

        mine, first, passed = [], [], []
        for a in range(n_arr):
            cp = pltpu.make_async_copy(x_refs[a], o_refs[a].at[4 * x + 2 * y + c], local_sems.at[a])
            cp.start()
            mine.append(cp)
            first.append(copy(a, 0, me, sibling, src=x_refs[a]))
            first += [copy(a, 1 + j, me, (*chip, c), src=x_refs[a]) for j, chip in enumerate(chips)]
        for cp in first:
            cp.start()
        for j, chip in enumerate(chips):
            for a in range(n_arr):
                copy(a, 1 + j, (*chip, c), me).wait_recv()
                cp = copy(a, 4 + j, (*chip, c), sibling)
                cp.start()
                passed.append(cp)
        for a in range(n_arr):
            copy(a, 0, sibling, me).wait_recv()
            for j, chip in enumerate(chips):
                copy(a, 4 + j, (*chip, 1 - c), me).wait_recv()
        for cp in first + passed:
            cp.wait_send()
        for cp in mine:
            cp.wait()

    return pl.pallas_call(
        body,
        name="all_gather",
        out_shape=[jax.ShapeDtypeStruct((N_DEV,) + s.shape, s.dtype) for s in shards],
        in_specs=[ANY] * n_arr,
        out_specs=[ANY] * n_arr,
        scratch_shapes=[pltpu.SemaphoreType.DMA((n_arr, 7)), pltpu.SemaphoreType.DMA((n_arr, 7)),
                        pltpu.SemaphoreType.DMA((n_arr,))],
    )(*shards)


HBM = pl.BlockSpec(memory_space=pltpu.HBM)
SEMS = pl.BlockSpec(memory_space=pltpu.SEMAPHORE)
SIDE_EFFECT = pltpu.SideEffectType.DATAFLOW_SIDE_EFFECTING


def core_routes(x, y, c):
    return [(2 * q + (1 - c), q, (x, y, 1 - c)) for q in range(4)]


def chip_routes(x, y, c):
    routes = []
    for j, (fx, fy) in enumerate(CHIP_FLIPS):
        px, py = _flip(x, fx), _flip(y, fy)
        routes.append((2 * px + py, j, (px, py, c)))
    return routes


def _route_copies(routes, n_routes, src_refs, land_refs, sems):
    x, y, c = _place()
    n_copies = len(src_refs) * n_routes
    return [pltpu.make_async_remote_copy(src_ref=src.at[s], dst_ref=land.at[d], send_sem=sems[a * n_routes + k],
                                         recv_sem=sems[n_copies + a * n_routes + k], device_id=dev,
                                         device_id_type=MESH)
            for a, (src, land) in enumerate(zip(src_refs, land_refs)) for k, (s, d, dev) in enumerate(routes(x, y, c))]


def gather_routes(x, y, c):
    me = 4 * x + 2 * y + c
    return [(0, me, (_flip(x, (k >> 2) & 1), _flip(y, (k >> 1) & 1), _flip(c, k & 1))) for k in range(1, N_DEV)]


def exchange_start(name, routes, n_routes, srcs, n_slots=None):
    n = len(srcs)
    n_sems = 2 * n * n_routes
    lands = [lax.empty((n_routes if n_slots is None else n_slots,) + s.shape[1:], s.dtype) for s in srcs]

    def body(*refs):
        for cp in _route_copies(routes, n_routes, refs[:n], refs[n:2 * n], refs[2 * n:2 * n + n_sems]):
            cp.start()
        refs[-1][...] = jnp.zeros_like(refs[-1])

    res = pl.pallas_call(
        body,
        name=name,
        out_shape=(*[pltpu.SemaphoreType.DMA(())] * n_sems, *[pltpu.HBM(a.shape, a.dtype) for a in srcs + lands],
                   jax.ShapeDtypeStruct((SUBLANE, LANE), F32)),
        in_specs=[HBM] * (2 * n),
        out_specs=(*[SEMS] * n_sems, *[HBM] * (2 * n), pl.BlockSpec(memory_space=pltpu.VMEM)),
        input_output_aliases={i: n_sems + i for i in range(2 * n)},
        compiler_params=pltpu.CompilerParams(has_side_effects=SIDE_EFFECT),
    )(*[pltpu.with_memory_space_constraint(a, pltpu.HBM) for a in srcs + lands])
    return (res[:n_sems], res[n_sems:-1]), res[-1]


def exchange_wait(name, routes, handle, after):
    sems, thru = handle
    n, n_sems = len(thru) // 2, len(sems)
    n_routes = n_sems // (2 * n)

    def body(*refs):
        for cp in _route_copies(routes, n_routes, refs[:n], refs[n:2 * n], refs[2 * n:2 * n + n_sems]):
            cp.wait_send()
            cp.wait_recv()

    res = pl.pallas_call(
        body,
        name=name,
        out_shape=tuple(pltpu.HBM(a.shape, a.dtype) for a in thru),
        in_specs=[HBM] * (2 * n) + [SEMS] * n_sems + [ANY],
        out_specs=tuple([HBM] * (2 * n)),
        input_output_aliases={i: i for i in range(2 * n)},
        compiler_params=pltpu.CompilerParams(has_side_effects=SIDE_EFFECT),
    )(*thru, *sems, after)
    return list(res[:n]), list(res[n:])


def gather_small(v):
    def body(v_ref, o_ref, send_sems, recv_sems, local_sem):
        x, y, c = _place()
        me = 4 * x + 2 * y + c
        mine = pltpu.make_async_copy(v_ref, o_ref.at[me], local_sem)
        mine.start()
        copies = []
        for k in range(1, N_DEV):
            fx, fy, fc = (k >> 2) & 1, (k >> 1) & 1, k & 1
            cp = pltpu.make_async_remote_copy(
                src_ref=v_ref, dst_ref=o_ref.at[me], send_sem=send_sems.at[k - 1], recv_sem=recv_sems.at[k - 1],
                device_id=(_flip(x, fx), _flip(y, fy), _flip(c, fc)), device_id_type=MESH)
            cp.start()
            copies.append(cp)
        for cp in copies:
            cp.wait()
        mine.wait()

    return pl.pallas_call(
        body,
        name="gather_small",
        out_shape=jax.ShapeDtypeStruct((N_DEV,) + v.shape, v.dtype),
        in_specs=[ANY],
        out_specs=ANY,
        scratch_shapes=[pltpu.SemaphoreType.DMA((N_DEV - 1,)), pltpu.SemaphoreType.DMA((N_DEV - 1,)),
                        pltpu.SemaphoreType.DMA],
    )(v)


def _row_tile(rows):
    for t in (256, 128, 64, 32, 16, 8):
        if rows % t == 0:
            return t
    return rows


def add_core_parts(name, g, recv, c_idx, wire):
    _, rows, cols = g.shape
    tr = _row_tile(rows)

    def body(c_ref, g_ref, r_ref, o_ref, w_ref):
        part = g_ref[...] + r_ref[...]
        o_ref[...] = part
        w_ref[...] = part.astype(w_ref.dtype)

    blk = pl.BlockSpec((1, tr, cols), lambda q, i, c_ref: (q, i, 0))
    return pl.pallas_call(
        body,
        name=name,
        grid_spec=pltpu.PrefetchScalarGridSpec(
            num_scalar_prefetch=1,
            grid=(4, rows // tr),
            in_specs=[pl.BlockSpec((1, tr, cols), lambda q, i, c_ref: (2 * q + c_ref[0], i, 0)), blk],
            out_specs=[blk, blk],
        ),
        out_shape=[jax.ShapeDtypeStruct((4, rows, cols), F32), jax.ShapeDtypeStruct((4, rows, cols), wire)],
        compiler_params=_cparams("parallel", "parallel"),
    )(c_idx, g, recv)


def _adamw(w, g, m, v):
    m = ADAM_B1 * m + (1.0 - ADAM_B1) * g
    v = ADAM_B2 * v + (1.0 - ADAM_B2) * (g * g)
    m_hat = m / (1.0 - ADAM_B1 ** ADAM_STEP)
    v_hat = v / (1.0 - ADAM_B2 ** ADAM_STEP)
    delta = -ADAM_LR * (m_hat / (jnp.sqrt(v_hat) + ADAM_EPS) + ADAM_WD * w)
    return delta, m, v


def adamw_sharded(name, parts, recv, q_idx, w, m, v):
    rows, cols = w.shape
    tr = _row_tile(rows)

    def body(q_ref, p_ref, r_ref, w_ref, m_ref, v_ref, g_out, d_out, m_out, v_out):
        g = p_ref[0] + r_ref[0].astype(F32) + r_ref[1].astype(F32) + r_ref[2].astype(F32)
        d, m_new, v_new = _adamw(w_ref[...], g, m_ref[...], v_ref[...])
        g_out[...], d_out[...], m_out[...], v_out[...] = g, d, m_new, v_new

    blk = pl.BlockSpec((tr, cols), lambda i, q_ref: (i, 0))
    return pl.pallas_call(
        body,
        name=name,
        grid_spec=pltpu.PrefetchScalarGridSpec(
            num_scalar_prefetch=1,
            grid=(rows // tr,),
            in_specs=[pl.BlockSpec((1, tr, cols), lambda i, q_ref: (q_ref[0], i, 0)),
                      pl.BlockSpec((3, tr, cols), lambda i, q_ref: (0, i, 0)), blk, blk, blk],
            out_specs=[blk] * 4,
        ),
        out_shape=[jax.ShapeDtypeStruct((rows, cols), F32)] * 4,
        compiler_params=_cparams("parallel"),
    )(q_idx, parts, recv, w, m, v)


def adamw_small(gathered, w, m, v):
    def body(g_ref, w_ref, m_ref, v_ref, g_out, d_out, m_out, v_out):
        g = g_ref[0]
        for j in range(1, N_DEV):
            g = g + g_ref[j]
        d, m_new, v_new = _adamw(w_ref[...], g, m_ref[...], v_ref[...])
        g_out[...], d_out[...], m_out[...], v_out[...] = g, d, m_new, v_new

    return pl.pallas_call(body, name="adamw_small", out_shape=[jax.ShapeDtypeStruct(w.shape, F32)] * 4)(gathered, w, m, v)


SHARDED = ("w_in", "w_uq", "w_ukv", "gdn_conv_w", "w_out")
SMALL = (("norm_gain", D_MODEL), ("mla_q_a_gain", Q_LORA), ("mla_kv_a_gain", KV_LORA), ("mla_q_norm_gain", QK),
         ("mla_k_norm_gain", QK), ("gdn_a_log", HEADS), ("gdn_dt_bias", HEADS), ("gdn_out_norm_gain", GDN_DIM))
WEIGHT_ORDER = ("norm_gain", "w_in", "mla_q_a_gain", "mla_kv_a_gain", "w_uq", "w_ukv", "mla_q_norm_gain",
                "mla_k_norm_gain", "gdn_conv_w", "gdn_a_log", "gdn_dt_bias", "gdn_out_norm_gain", "w_out")


def _pack_small(d):
    rows = []
    for name, n in SMALL:
        a = d[name].reshape(-1).astype(F32)
        n_pad = -(-n // LANE) * LANE
        rows.append(jnp.pad(a, (0, n_pad - n)).reshape(n_pad // LANE, LANE))
    packed = jnp.concatenate(rows, axis=0)
    return jnp.pad(packed, ((0, -packed.shape[0] % SUBLANE), (0, 0)))


def _unpack_small(packed):
    out, row = {}, 0
    for name, n in SMALL:
        n_rows = -(-n // LANE)
        out[name] = packed[row:row + n_rows].reshape(-1)[:n].reshape(1, n)
        row += n_rows
    return out


def kernel(x, positions, norm_gain, w_in, mla_q_a_gain, mla_kv_a_gain, w_uq, w_ukv, mla_q_norm_gain, mla_k_norm_gain, gdn_conv_w, gdn_a_log, gdn_dt_bias, gdn_out_norm_gain, w_out, loss_target, m_norm_gain, m_w_in, m_mla_q_a_gain, m_mla_kv_a_gain, m_w_uq, m_w_ukv, m_mla_q_norm_gain, m_mla_k_norm_gain, m_gdn_conv_w, m_gdn_a_log, m_gdn_dt_bias, m_gdn_out_norm_gain, m_w_out, v_norm_gain, v_w_in, v_mla_q_a_gain, v_mla_kv_a_gain, v_w_uq, v_w_ukv, v_mla_q_norm_gain, v_mla_k_norm_gain, v_gdn_conv_w, v_gdn_a_log, v_gdn_dt_bias, v_gdn_out_norm_gain, v_w_out):
    w = dict(norm_gain=norm_gain, w_in=w_in, mla_q_a_gain=mla_q_a_gain, mla_kv_a_gain=mla_kv_a_gain, w_uq=w_uq,
             w_ukv=w_ukv, mla_q_norm_gain=mla_q_norm_gain, mla_k_norm_gain=mla_k_norm_gain, gdn_conv_w=gdn_conv_w,
             gdn_a_log=gdn_a_log, gdn_dt_bias=gdn_dt_bias, gdn_out_norm_gain=gdn_out_norm_gain, w_out=w_out)
    m = dict(norm_gain=m_norm_gain, w_in=m_w_in, mla_q_a_gain=m_mla_q_a_gain, mla_kv_a_gain=m_mla_kv_a_gain,
             w_uq=m_w_uq, w_ukv=m_w_ukv, mla_q_norm_gain=m_mla_q_norm_gain, mla_k_norm_gain=m_mla_k_norm_gain,
             gdn_conv_w=m_gdn_conv_w, gdn_a_log=m_gdn_a_log, gdn_dt_bias=m_gdn_dt_bias,
             gdn_out_norm_gain=m_gdn_out_norm_gain, w_out=m_w_out)
    v = dict(norm_gain=v_norm_gain, w_in=v_w_in, mla_q_a_gain=v_mla_q_a_gain, mla_kv_a_gain=v_mla_kv_a_gain,
             w_uq=v_w_uq, w_ukv=v_w_ukv, mla_q_norm_gain=v_mla_q_norm_gain, mla_k_norm_gain=v_mla_k_norm_gain,
             gdn_conv_w=v_gdn_conv_w, gdn_a_log=v_gdn_a_log, gdn_dt_bias=v_gdn_dt_bias,
             gdn_out_norm_gain=v_gdn_out_norm_gain, w_out=v_w_out)
    t_len = x.shape[1]

    shards = [w[n][0] if n == "gdn_conv_w" else w[n][0].astype(_BF) for n in SHARDED]
    xi, yi, ci = _place()
    c_idx = jnp.reshape(ci, (1,)).astype(jnp.int32)
    q_idx = jnp.reshape(2 * xi + yi, (1,)).astype(jnp.int32)
    flight = {}

    def cols_whole(g):
        return g.transpose(1, 0, 2).reshape(g.shape[1], N_DEV * g.shape[2])

    def col_blocks(g):
        return g.reshape(g.shape[0], N_DEV, g.shape[1] // N_DEV).transpose(1, 0, 2)

    (a_w_in,) = all_gather(shards[:1])
    p = {n: w[n] for n, _ in SMALL}
    p["w_in"] = arrange_w_in(cols_whole(a_w_in))
    flight["weights"], weights_token = exchange_start(
        "gather_start", gather_routes, N_DEV - 1, [s[None] for s in shards[1:]], n_slots=N_DEV)

    def late_weights(proj):
        _, landed = exchange_wait("gather_wait", gather_routes, flight["weights"], proj)
        me = 4 * xi + 2 * yi + ci
        a_w_uq, a_w_ukv, a_cw, a_w_out = [lax.dynamic_update_slice(land, s[None], (me, 0, 0))
                                          for land, s in zip(landed, shards[1:])]
        return {"w_uq": arrange_w_uq(cols_whole(a_w_uq)), "w_ukv": cols_whole(a_w_ukv),
                "gdn_conv_w": cols_whole(a_cw), "w_out": a_w_out.reshape(N_DEV * a_w_out.shape[1], a_w_out.shape[2])}

    early, parts = SHARDED[1:], {}

    def core_stage(tag, names, blocks):
        flight["cores" + tag], token = exchange_start("cores_start" + tag, core_routes, 4, blocks)
        flight["names" + tag] = names
        return token

    def chip_stage(tag, after):
        blocks, landed = exchange_wait("cores_wait" + tag, core_routes, flight["cores" + tag], after)
        for n, g, r in zip(flight["names" + tag], blocks, landed):
            parts[n] = add_core_parts("add_" + n, g, r, c_idx, F32 if n == "gdn_conv_w" else _BF)
        wires = [parts[n][1] for n in flight["names" + tag]]
        flight["chips" + tag], token = exchange_start("chips_start" + tag, chip_routes, 3, wires)
        return token

    def on_early_grads(grads):
        return core_stage("_early", early, [
            col_blocks(unarrange_w_uq(grads["w_uq"])), col_blocks(grads["w_ukv"]), col_blocks(grads["gdn_conv_w"]),
            grads["w_out"].reshape(N_DEV, D_MODEL // N_DEV, D_MODEL)])

    def on_d_proj(d_proj):
        return chip_stage("_early", d_proj)

    def on_weight_grads(grads):
        return core_stage("", SHARDED[:1], [col_blocks(unarrange_w_in(grads["w_in"]))])

    def on_d_xn(d_xn):
        return chip_stage("", d_xn)

    pos = positions.reshape(t_len, 1).astype(F32)
    loss, grad_x, grads = local_step(x.reshape(t_len, D_MODEL), pos, loss_target.reshape(t_len, D_MODEL), p,
                                     on_early_grads=on_early_grads, on_d_proj=on_d_proj,
                                     on_weight_grads=on_weight_grads, on_d_xn=on_d_xn,
                                     first_after=weights_token, late_weights=late_weights)
    loss = lax.psum(loss, ("x", "y", "c"))
    out = {}
    small_all = gather_small(_pack_small(grads))
    res = adamw_small(small_all, _pack_small(w), _pack_small(m), _pack_small(v))
    unpacked = [_unpack_small(a) for a in res]
    for n, _ in SMALL:
        out[n] = [u[n] for u in unpacked]

    _, from_chips_early = exchange_wait("chips_wait_early", chip_routes, flight["chips_early"], res[0])
    _, from_chips = exchange_wait("chips_wait", chip_routes, flight["chips"], res[0])
    for n, rcv in zip(SHARDED, from_chips + from_chips_early):
        prt = parts[n][0]
        shape = w[n].shape
        res = adamw_sharded("adamw_" + n, prt, rcv, q_idx, w[n].reshape(shape[-2:]), m[n].reshape(shape[-2:]),
                            v[n].reshape(shape[-2:]))
        out[n] = [a.reshape(shape) for a in res]

    return (loss, grad_x.reshape(x.shape), *[out[n][0] for n in WEIGHT_ORDER], *[out[n][1] for n in WEIGHT_ORDER],
            *[out[n][2] for n in WEIGHT_ORDER], *[out[n][3] for n in WEIGHT_ORDER])
```

```python
import functools

import jax
import jax.numpy as jnp
from jax import lax
from jax.experimental import pallas as pl
from jax.experimental.pallas import tpu as pltpu

F32 = jnp.float32
_BF = jnp.bfloat16

D_MODEL = 2048
HEADS = 8
NOPE = 128
ROPE = 64
QK = NOPE + ROPE
Q_LORA = 512
KV_LORA = 256
HEAD_PAD = 256
GDN_DIM = 128
WIDTH = HEADS * 128
CONV_W = 4
CHUNK = 64
ROPE_THETA = 10000.0
EPS = 1e-6
N_DEV = 8
LANE = 128
SUBLANE = 8
VMEM_LIMIT = 48 * 1024 * 1024

ADAM_LR, ADAM_B1, ADAM_B2, ADAM_EPS, ADAM_WD, ADAM_STEP = 0.001, 0.9, 0.999, 1e-08, 0.01, 10

P_MGATE, P_GQ, P_GK, P_GV, P_GGATE = 0, 1024, 2048, 3072, 4096
P_CQ, P_CKV, P_KR, P_GAB = 5120, 5632, 5888, 6016
R_SPLITS = (512, 256, 64, 1024, 1024, 1024, 1024, 8, 8, 1024)


def _cparams(*sem):
    return pltpu.CompilerParams(dimension_semantics=sem, vmem_limit_bytes=VMEM_LIMIT)


def _d_nn(a, b):
    return jnp.dot(a.astype(_BF), b.astype(_BF), preferred_element_type=F32)


def _d_nt(a, b):
    return lax.dot_general(a.astype(_BF), b.astype(_BF), (((1,), (1,)), ((), ())), preferred_element_type=F32)


def _d_tn(a, b):
    return lax.dot_general(a.astype(_BF), b.astype(_BF), (((0,), (0,)), ((), ())), preferred_element_type=F32)


_NN3 = (((2,), (1,)), ((0,), (0,)))
_NT3 = (((2,), (2,)), ((0,), (0,)))
_TN3 = (((1,), (1,)), ((0,), (0,)))


def _bdot(a, b, dims, hi):
    if hi:
        return lax.dot_general(a, b, dims, preferred_element_type=F32, precision=hi)
    return lax.dot_general(a.astype(_BF), b.astype(_BF), dims, preferred_element_type=F32)


def _batched_matmuls(hi):
    nn = jax.custom_vjp(lambda a, b: _bdot(a, b, _NN3, hi))
    nt = jax.custom_vjp(lambda a, b: _bdot(a, b, _NT3, hi))
    tn = jax.custom_vjp(lambda a, b: _bdot(a, b, _TN3, hi))
    nn.defvjp(lambda a, b: (_bdot(a, b, _NN3, hi), (a, b)),
              lambda r, g: (_bdot(g, r[1], _NT3, hi), _bdot(r[0], g, _TN3, hi)))
    nt.defvjp(lambda a, b: (_bdot(a, b, _NT3, hi), (a, b)),
              lambda r, g: (_bdot(g, r[1], _NN3, hi), _bdot(g, r[0], _TN3, hi)))
    tn.defvjp(lambda a, b: (_bdot(a, b, _TN3, hi), (a, b)),
              lambda r, g: (_bdot(r[1], g, _NT3, hi), _bdot(r[0], g, _NN3, hi)))
    return nn, nt, tn


_bmm, _bmm_nt, _bmm_tn = _batched_matmuls(False)


def _split2(x):
    hi = x.astype(_BF)
    return hi, (x - hi.astype(F32)).astype(_BF)


def _pdot(a, b, mode):
    (a_hi, a_lo), (b_hi, b_lo) = _split2(a), _split2(b)
    a_ax, b_ax, dims = {"nn": (2, 1, _NN3), "nt": (2, 2, _NT3), "tn": (1, 1, _TN3)}[mode]
    lhs = jnp.concatenate([a_hi, a_lo, a_hi], axis=a_ax)
    rhs = jnp.concatenate([b_hi, b_hi, b_lo], axis=b_ax)
    return lax.dot_general(lhs, rhs, dims, preferred_element_type=F32)


def _packed_matmuls():
    nn = jax.custom_vjp(lambda a, b: _pdot(a, b, "nn"))
    nn.defvjp(lambda a, b: (_pdot(a, b, "nn"), (a, b)), lambda r, g: (_pdot(g, r[1], "nt"), _pdot(r[0], g, "tn")))
    return nn


_bh3 = _packed_matmuls()
_bh3_passes = _batched_matmuls(lax.Precision.HIGH)[0]


@functools.partial(jax.custom_vjp, nondiff_argnums=(1, 2))
def _roll(x, shift, axis):
    return pltpu.roll(x, shift, axis)


def _roll_fwd(x, shift, axis):
    return pltpu.roll(x, shift, axis), None


def _roll_bwd(shift, axis, _, g):
    n = g.shape[axis]
    return (pltpu.roll(g, (n - shift) % n, axis),)


_roll.defvjp(_roll_fwd, _roll_bwd)


def _rms(x, gain):
    return x * lax.rsqrt(jnp.mean(x * x, axis=-1, keepdims=True) + EPS) * gain


MM_TILE = 1024
MM_DEPTH = 2048


def matmul(name, a, b, mode, after=None):
    if mode == "nn":
        (m, k), (k2, n) = a.shape, b.shape
    elif mode == "nt":
        (m, k), (n, k2) = a.shape, b.shape
    else:
        (k, m), (k2, n) = a.shape, b.shape
    assert k == k2, (name, a.shape, b.shape)
    tm, tn, tk = min(MM_TILE, m), min(MM_TILE, n), min(MM_DEPTH, k)
    assert m % tm == 0 and n % tn == 0 and k % tk == 0, (name, m, n, k)
    dot = {"nn": _d_nn, "nt": _d_nt, "tn": _d_tn}[mode]

    def body(a_ref, b_ref, *rest):
        o_ref = rest[-1]
        kk = pl.program_id(2)
        part = dot(a_ref[...], b_ref[...])

        @pl.when(kk == 0)
        def _():
            o_ref[...] = part

        @pl.when(kk != 0)
        def _():
            o_ref[...] += part

    if mode == "nn":
        a_spec = pl.BlockSpec((tm, tk), lambda j, i, kk: (i, kk))
        b_spec = pl.BlockSpec((tk, tn), lambda j, i, kk: (kk, j))
    elif mode == "nt":
        a_spec = pl.BlockSpec((tm, tk), lambda j, i, kk: (i, kk))
        b_spec = pl.BlockSpec((tn, tk), lambda j, i, kk: (j, kk))
    else:
        a_spec = pl.BlockSpec((tk, tm), lambda j, i, kk: (kk, i))
        b_spec = pl.BlockSpec((tk, tn), lambda j, i, kk: (kk, j))
    return pl.pallas_call(
        body,
        name=name,
        grid=(n // tn, m // tm, k // tk),
        in_specs=[a_spec, b_spec] + ([] if after is None else [pl.BlockSpec(memory_space=pl.ANY)]),
        out_specs=pl.BlockSpec((tm, tn), lambda j, i, kk: (i, j)),
        out_shape=jax.ShapeDtypeStruct((m, n), F32),
        compiler_params=_cparams("parallel", "parallel", "arbitrary"),
    )(*((a, b) if after is None else (a, b, after)))


def out_proj_loss(mixed, w_out, x, tgt):
    (m, k), n = mixed.shape, w_out.shape[1]
    tm, tn = min(MM_TILE // 2, m), min(MM_TILE, n)
    assert m % tm == 0 and n % tn == 0

    def body(a_ref, b_ref, x_ref, t_ref, dy_ref, dy_mx_ref, loss_ref):
        first = (pl.program_id(0) == 0) & (pl.program_id(1) == 0)
        e = x_ref[...] + _d_nn(a_ref[...], b_ref[...]) - t_ref[...]
        dy = e * (1.0 / D_MODEL)
        dy_ref[...] = dy
        dy_mx_ref[...] = dy.astype(dy_mx_ref.dtype)
        part = jnp.zeros((SUBLANE, LANE), F32) + 0.5 * jnp.sum(e * e) * (1.0 / D_MODEL)

        @pl.when(first)
        def _():
            loss_ref[...] = part

        @pl.when(jnp.logical_not(first))
        def _():
            loss_ref[...] += part

    tile = pl.BlockSpec((tm, tn), lambda j, i: (i, j))
    return pl.pallas_call(
        body,
        name="out_proj_loss",
        grid=(n // tn, m // tm),
        in_specs=[pl.BlockSpec((tm, k), lambda j, i: (i, 0)), pl.BlockSpec((k, tn), lambda j, i: (0, j)), tile, tile],
        out_specs=[tile, tile, pl.BlockSpec((SUBLANE, LANE), lambda j, i: (0, 0))],
        out_shape=[jax.ShapeDtypeStruct((m, n), F32), jax.ShapeDtypeStruct((m, n), _BF),
                   jax.ShapeDtypeStruct((SUBLANE, LANE), F32)],
        compiler_params=_cparams("arbitrary", "arbitrary"),
    )(mixed, w_out, x, tgt)


def rowwise(name, fn, t_len, tile, row_in, full_in, row_out, acc_out=(), carries=(), reverse=False):
    tile = min(tile, t_len)
    n = t_len // tile
    assert t_len % tile == 0 and tile % SUBLANE == 0
    n_in, n_ro, n_acc, n_car = len(row_in) + len(full_in), len(row_out), len(acc_out), len(carries)

    def ti(i):
        return (n - 1 - i) if reverse else i

    in_specs, args = [], []
    for arr, kind in row_in:
        if kind[0] == "r":
            in_specs.append(pl.BlockSpec((tile, kind[1]), lambda i, c=kind[2]: (ti(i), c)))
        elif kind[0] == "h":
            in_specs.append(pl.BlockSpec((arr.shape[0], tile, arr.shape[2]), lambda i: (0, ti(i), 0)))
        else:
            in_specs.append(pl.BlockSpec(
                (SUBLANE, kind[1]), lambda i, c=kind[2]: (jnp.maximum(ti(i) * (tile // SUBLANE) - 1, 0), c)))
        args.append(arr)
    for arr in full_in:
        in_specs.append(pl.BlockSpec(arr.shape, lambda i, nd=arr.ndim: (0,) * nd))
        args.append(arr)
    out_specs, out_shape = [], []
    for kind in row_out:
        if kind[0] == "r":
            out_specs.append(pl.BlockSpec((tile, kind[1]), lambda i: (ti(i), 0)))
            out_shape.append(jax.ShapeDtypeStruct((t_len, kind[1]), kind[2]))
        else:
            out_specs.append(pl.BlockSpec((kind[1], tile, kind[2]), lambda i: (0, ti(i), 0)))
            out_shape.append(jax.ShapeDtypeStruct((kind[1], t_len, kind[2]), kind[3]))
    for shp in acc_out:
        out_specs.append(pl.BlockSpec(shp, lambda i, nd=len(shp): (0,) * nd))
        out_shape.append(jax.ShapeDtypeStruct(shp, F32))

    def body(*refs):
        in_refs = refs[:n_in]
        ro_refs = refs[n_in:n_in + n_ro]
        acc_refs = refs[n_in + n_ro:n_in + n_ro + n_acc]
        car_refs = refs[n_in + n_ro + n_acc:]
        step = pl.program_id(0)
        if n_car:
            @pl.when(step == 0)
            def _():
                for r in car_refs:
                    r[...] = jnp.zeros_like(r)
        vals = [r[...].astype(F32) for r in in_refs] + [r[...] for r in car_refs]
        outs = fn(ti(step), *vals)
        assert len(outs) == n_ro + n_acc + n_car, (name, len(outs))
        for r, o in zip(ro_refs, outs[:n_ro]):
            r[...] = o.astype(r.dtype)
        for r, o in zip(acc_refs, outs[n_ro:n_ro + n_acc]):
            @pl.when(step == 0)
            def _(r=r, o=o):
                r[...] = o

            @pl.when(step != 0)
            def _(r=r, o=o):
                r[...] += o
        for r, o in zip(car_refs, outs[n_ro + n_acc:]):
            r[...] = o

    res = pl.pallas_call(
        body,
        name=name,
        grid=(n,),
        in_specs=in_specs,
        out_specs=out_specs,
        out_shape=out_shape,
        scratch_shapes=[pltpu.VMEM(s, F32) for s in carries],
        compiler_params=_cparams("arbitrary"),
    )(*args)
    return list(res)


def _vjp_fn(fn, n_diff, n_out):
    def g(i, *a):
        ins, cts = a[:len(a) - n_out], a[len(a) - n_out:]
        diff, rest = ins[:n_diff], ins[n_diff:]
        _, pull = jax.vjp(lambda *d: tuple(fn(i, *d, *rest)), *diff)
        return tuple(pull(tuple(cts)))

    return g


def f_rms_x(i, x, gain):
    return (_rms(x, gain),)


def f_lat(i, cq, ckv, gq, gkv):
    return _rms(cq, gq), _rms(ckv, gkv)


def _rope_tables(pos, invf):
    ang = pos * invf
    lane = lax.broadcasted_iota(jnp.int32, (1, LANE), 1)
    cosv, sinv = jnp.cos(ang), jnp.sin(ang)
    half = ROPE // 2
    c = jnp.where(lane < ROPE, cosv, 0.0)
    sa = jnp.where(lane < half, -sinv, 0.0)
    sb = jnp.where((lane >= half) & (lane < ROPE), sinv, 0.0)
    return c, sa, sb


def _rope(xh, tabs):
    c, sa, sb = tabs
    half = ROPE // 2
    return xh * c + _roll(xh, LANE - half, 1) * sa + _roll(xh, half, 1) * sb


def f_head(i, q_raw, kv_raw, kr, qg, kg, pos, invf):
    tabs = _rope_tables(pos, invf)
    qs, ks, vs = [], [], []
    kr_ss = jnp.sum(kr * kr, axis=-1, keepdims=True)
    for h in range(HEADS):
        lo = q_raw[:, HEAD_PAD * h:HEAD_PAD * h + NOPE]
        hi = q_raw[:, HEAD_PAD * h + NOPE:HEAD_PAD * (h + 1)]
        ss = jnp.sum(lo * lo, axis=-1, keepdims=True) + jnp.sum(hi * hi, axis=-1, keepdims=True)
        r = lax.rsqrt(ss * (1.0 / QK) + EPS)
        qs.append(jnp.concatenate([lo * r * qg[:, :NOPE], _rope(hi * r * qg[:, NOPE:], tabs)], axis=1))
        lo = kv_raw[:, 2 * NOPE * h:2 * NOPE * h + NOPE]
        ss = jnp.sum(lo * lo, axis=-1, keepdims=True) + kr_ss
        r = lax.rsqrt(ss * (1.0 / QK) + EPS)
        ks.append(jnp.concatenate([lo * r * kg[:, :NOPE], _rope(kr * r * kg[:, NOPE:], tabs)], axis=1))
        vs.append(kv_raw[:, 2 * NOPE * h + NOPE:2 * NOPE * (h + 1)])
    return jnp.stack(qs), jnp.stack(ks), jnp.stack(vs)


def f_mix(i, o_mla, mgate, o_gdn, ggate, og):
    parts = [o_mla * jax.nn.silu(mgate)]
    for h in range(HEADS):
        parts.append(_rms(o_gdn[h], og) * jax.nn.silu(ggate[:, LANE * h:LANE * (h + 1)]))
    return (jnp.concatenate(parts, axis=1),)


def _row(a, j):
    rows = lax.broadcasted_iota(jnp.int32, a.shape, 0)
    return jnp.sum(jnp.where(rows == j, a, 0.0), axis=0, keepdims=True)


def _shift_rows(x, halo, d):
    xs = _roll(x, d, 0)
    hs = _roll(halo, d, 0)
    r8 = lax.broadcasted_iota(jnp.int32, hs.shape, 0)
    top = jnp.where(r8 < d, hs, xs[:SUBLANE])
    return jnp.concatenate([top, xs[SUBLANE:]], axis=0)


def _conv_silu(x, halo, w):
    y = _row(w, CONV_W - 1) * x
    for j in range(CONV_W - 1):
        y = y + _row(w, j) * _shift_rows(x, halo, CONV_W - 1 - j)
    return jax.nn.silu(y)


def _head_select(offset):
    r = lax.broadcasted_iota(jnp.int32, (LANE, WIDTH), 0)
    c = lax.broadcasted_iota(jnp.int32, (LANE, WIDTH), 1)
    return (r == offset + lax.shift_right_logical(c, 7)).astype(_BF)


def _split3(x):
    x1 = x.astype(_BF)
    r1 = x - x1.astype(F32)
    x2 = r1.astype(_BF)
    return x1, x2, (r1 - x2.astype(F32)).astype(_BF)


@jax.custom_vjp
def _spread(x, sel):
    return _d_nn(jnp.concatenate(_split3(x), axis=1), jnp.concatenate([sel, sel, sel], axis=0))


def _spread_fwd(x, sel):
    return _spread(x, sel), sel


def _spread_bwd(sel, g):
    g1, g2, g3 = _split3(g)
    return _d_nt(g1, sel) + _d_nt(g2, sel) + _d_nt(g3, sel), jnp.zeros_like(sel)


_spread.defvjp(_spread_fwd, _spread_bwd)


def f_gdn_pre(i, gq, gk, gv, gab, hq, hk, hv, cwq, cwk, cwv, alog, dtb):
    live = jnp.where(i == 0, 0.0, 1.0)
    q = _conv_silu(gq, hq * live, cwq)
    k = _conv_silu(gk, hk * live, cwk)
    v = _conv_silu(gv, hv * live, cwv)
    g = _spread(-jnp.exp(alog) * jax.nn.softplus(gab + dtb), _head_select(0))
    beta = _spread(jax.nn.sigmoid(gab), _head_select(HEADS))
    qs, ks, vs, gs, bs = [], [], [], [], []
    for h in range(HEADS):
        sl = slice(LANE * h, LANE * (h + 1))
        qh, kh = q[:, sl], k[:, sl]
        qs.append(qh * lax.rsqrt(jnp.sum(qh * qh, axis=-1, keepdims=True) + EPS) * (GDN_DIM ** -0.5))
        ks.append(kh * lax.rsqrt(jnp.sum(kh * kh, axis=-1, keepdims=True) + EPS))
        vs.append(v[:, sl])
        gs.append(g[:, sl])
        bs.append(beta[:, sl])
    return jnp.stack(qs), jnp.stack(ks), jnp.stack(vs), jnp.stack(gs), jnp.stack(bs)


def gdn_pre_bwd(i, gq, gk, gv, gab, hq, hk, hv, dq, dk, dv, dg, db, cwq, cwk, cwv, alog, dtb, cq, ck, cv):
    grads = _vjp_fn(f_gdn_pre, 12, 5)(i, gq, gk, gv, gab, hq, hk, hv, cwq, cwk, cwv, alog, dtb, dq, dk, dv, dg, db)
    dgq, dgk, dgv, dgab, dhq, dhk, dhv, dcwq, dcwk, dcwv, dalog, ddtb = grads

    def add_tail(dx, carry):
        return jnp.concatenate([dx[:-SUBLANE], dx[-SUBLANE:] + carry], axis=0)

    return (add_tail(dgq, cq), add_tail(dgk, ck), add_tail(dgv, cv), dgab,
            dcwq, dcwk, dcwv, dalog, ddtb, dhq, dhk, dhv)


def _flash_tile(t_len):
    return min(512, t_len)


FLASH_HEADS = 4
FLASH_BWD_HEADS = 2
LOG2E = 1.4426950408889634


def _causal(rows0, shape):
    r = rows0 + lax.broadcasted_iota(jnp.int32, shape, 0)
    c = lax.broadcasted_iota(jnp.int32, shape, 1)
    return c <= r


def flash_fwd(q, k, v):
    h_n, t_len, _ = q.shape
    tq = _flash_tile(t_len)
    nq = t_len // tq
    hb = FLASH_HEADS
    kw = 2 if nq % 2 == 0 else 1
    tk = kw * tq
    c2 = (QK ** -0.5) * LOG2E
    pairs = [(i, j) for i in range(nq) for j in range(i // kw + 1)]
    qt = jnp.array([p[0] for p in pairs], jnp.int32)
    kt = jnp.array([p[1] for p in pairs], jnp.int32)

    def body(qt_ref, kt_ref, q_ref, k_ref, v_ref, o_ref, lse_ref, m_s, acc_s):
        step = pl.program_id(1)
        qi, kj = qt_ref[step], kt_ref[step]
        last = qi // kw

        @pl.when(kj == 0)
        def _():
            m_s[...] = jnp.full_like(m_s, -jnp.inf)
            acc_s[...] = jnp.zeros_like(acc_s)

        def tile(diagonal):
            s = _bdot(q_ref[...], k_ref[...], _NT3, False) * c2
            if diagonal:
                s = jnp.where(_causal((qi % kw) * tq, (tq, tk))[None], s, -jnp.inf)
            m_old = m_s[...]
            m_new = jnp.maximum(m_old, jnp.max(s, axis=-1, keepdims=True))
            p = jnp.exp2(s - m_new).astype(_BF)
            v_ones = jnp.concatenate([v_ref[...], jnp.ones((hb, tk, LANE), _BF)], axis=2)
            acc_s[...] = jnp.exp2(m_old - m_new) * acc_s[...] + _bdot(p, v_ones, _NN3, False)
            m_s[...] = m_new

        @pl.when(kj < last)
        def _():
            tile(False)

        @pl.when(kj == last)
        def _():
            tile(True)
            acc = acc_s[...]
            l_sum = acc[:, :, LANE:]
            o = acc[:, :, :LANE] / l_sum
            for hh in range(hb):
                o_ref[:, LANE * hh:LANE * (hh + 1)] = o[hh]
            lse_ref[...] = m_s[...] + jnp.log2(jnp.max(l_sum, axis=-1, keepdims=True))

    return pl.pallas_call(
        body,
        name="flash_fwd",
        grid_spec=pltpu.PrefetchScalarGridSpec(
            num_scalar_prefetch=2,
            grid=(h_n // hb, qt.shape[0]),
            in_specs=[
                pl.BlockSpec((hb, tq, HEAD_PAD), lambda h, s, qt_ref, kt_ref: (h, qt_ref[s], 0)),
                pl.BlockSpec((hb, tk, HEAD_PAD), lambda h, s, qt_ref, kt_ref: (h, kt_ref[s], 0)),
                pl.BlockSpec((hb, tk, LANE), lambda h, s, qt_ref, kt_ref: (h, kt_ref[s], 0)),
            ],
            out_specs=[
                pl.BlockSpec((tq, hb * LANE), lambda h, s, qt_ref, kt_ref: (qt_ref[s], h)),
                pl.BlockSpec((hb, tq, 1), lambda h, s, qt_ref, kt_ref: (h, qt_ref[s], 0)),
            ],
            scratch_shapes=[pltpu.VMEM((hb, tq, 1), F32), pltpu.VMEM((hb, tq, 2 * LANE), F32)],
        ),
        out_shape=[jax.ShapeDtypeStruct((t_len, h_n * LANE), F32), jax.ShapeDtypeStruct((h_n, t_len, 1), F32)],
        compiler_params=_cparams("parallel", "arbitrary"),
    )(qt, kt, q, k, v)


def flash_bwd(q, k, v, do, lse, delta):
    h_n, t_len, _ = q.shape
    tq = _flash_tile(t_len)
    nq = t_len // tq
    hb = FLASH_BWD_HEADS
    kw = 2 if nq % 2 == 0 else 1
    tk = kw * tq
    pairs = [(i, j) for j in range(nq // kw) for i in range(kw * j, nq)]
    n_steps = len(pairs)
    qt = jnp.array([p[0] for p in pairs], jnp.int32)
    kt = jnp.array([p[1] for p in pairs], jnp.int32)
    scale = QK ** -0.5
    c2 = scale * LOG2E

    def body(qt_ref, kt_ref, q_ref, k_ref, v_ref, do_ref, lse_ref, dl_ref, dq_hbm, dk_ref, dv_ref, dq_s, dq_sem):
        group, step = pl.program_id(0), pl.program_id(1)
        qi, kj = qt_ref[step], kt_ref[step]

        @pl.when(step == 0)
        def _():
            dq_s[...] = jnp.zeros_like(dq_s)

        def tile(diagonal):
            qb, kb = q_ref[...], k_ref[...]
            dob = jnp.stack([do_ref[:, LANE * hh:LANE * (hh + 1)] for hh in range(hb)])
            p = jnp.exp2(_bdot(qb, kb, _NT3, False) * c2 - lse_ref[...])
            if diagonal:
                p = jnp.where(_causal((qi % kw) * tq, (tq, tk))[None], p, 0.0)
            dv = _bdot(p, dob, _TN3, False)
            ds = p * (_bdot(dob, v_ref[...], _NT3, False) - dl_ref[...]) * scale
            dk = _bdot(ds, qb, _TN3, False)
            dq_s[:, pl.ds(pl.multiple_of(qi * tq, tq), tq), :] += _bdot(ds, kb, _NN3, False)
            return dk, dv

        @pl.when(qi == kw * kj)
        def _():
            dk_ref[...], dv_ref[...] = tile(True)

        @pl.when((qi != kw * kj) & (qi // kw == kj))
        def _():
            dk, dv = tile(True)
            dk_ref[...] += dk
            dv_ref[...] += dv

        @pl.when(qi // kw > kj)
        def _():
            dk, dv = tile(False)
            dk_ref[...] += dk
            dv_ref[...] += dv

        @pl.when(step == n_steps - 1)
        def _():
            out = pltpu.make_async_copy(dq_s, dq_hbm.at[pl.ds(group * hb, hb)], dq_sem)
            out.start()
            out.wait()

    def qmap(h, s, qt_ref, kt_ref):
        return (h, qt_ref[s], 0)

    def kmap(h, s, qt_ref, kt_ref):
        return (h, kt_ref[s], 0)

    return pl.pallas_call(
        body,
        name="flash_bwd",
        grid_spec=pltpu.PrefetchScalarGridSpec(
            num_scalar_prefetch=2,
            grid=(h_n // hb, n_steps),
            in_specs=[
                pl.BlockSpec((hb, tq, HEAD_PAD), qmap),
                pl.BlockSpec((hb, tk, HEAD_PAD), kmap),
                pl.BlockSpec((hb, tk, LANE), kmap),
                pl.BlockSpec((tq, hb * LANE), lambda h, s, qt_ref, kt_ref: (qt_ref[s], h)),
                pl.BlockSpec((hb, tq, 1), qmap),
                pl.BlockSpec((hb, tq, 1), qmap),
            ],
            out_specs=[
                pl.BlockSpec(memory_space=pl.ANY),
                pl.BlockSpec((hb, tk, HEAD_PAD), kmap),
                pl.BlockSpec((hb, tk, LANE), kmap),
            ],
            scratch_shapes=[pltpu.VMEM((hb, t_len, HEAD_PAD), F32), pltpu.SemaphoreType.DMA],
        ),
        out_shape=[
            jax.ShapeDtypeStruct((h_n, t_len, HEAD_PAD), F32),
            jax.ShapeDtypeStruct((h_n, t_len, HEAD_PAD), F32),
            jax.ShapeDtypeStruct((h_n, t_len, LANE), F32),
        ],
        compiler_params=_cparams("parallel", "arbitrary"),
    )(qt, kt, q, k, v, do, lse, delta)


def _tri_ones(h_n):
    ii = lax.broadcasted_iota(jnp.int32, (h_n, CHUNK, CHUNK), 1)
    jj = lax.broadcasted_iota(jnp.int32, (h_n, CHUNK, CHUNK), 2)
    return (ii >= jj).astype(_BF)


@jax.custom_vjp
def _chunk_cumsum(gb):
    tri = _tri_ones(gb.shape[0])
    return _bdot(jnp.concatenate([tri, tri, tri], axis=2), jnp.concatenate(_split3(gb), axis=1), _NN3, False)


def _chunk_cumsum_bwd(_, ct):
    tri = _tri_ones(ct.shape[0])
    return (_bdot(jnp.concatenate([tri, tri, tri], axis=1), jnp.concatenate(_split3(ct), axis=1), _TN3, False),)


_chunk_cumsum.defvjp(lambda gb: (_chunk_cumsum(gb), None), _chunk_cumsum_bwd)


@jax.custom_vjp
def _pair_diff(gcb):
    g1, g2, g3 = _split3(gcb)
    lane = lax.broadcasted_iota(jnp.int32, (1, 1, LANE), 2)
    one, zero = jnp.ones((), _BF), jnp.zeros((), _BF)
    a = jnp.where(lane == 0, g1, jnp.where(lane == 1, g2, jnp.where(lane == 2, g3, jnp.where(lane < 6, one, zero))))
    b = jnp.where(lane < 3, one, jnp.where(lane == 3, -g1, jnp.where(lane == 4, -g2, jnp.where(lane == 5, -g3, zero))))
    return _bdot(a, b, _NT3, False)


def _pair_diff_bwd(_, ct):
    parts = _split3(ct)
    ones = jnp.ones((ct.shape[0], 3 * CHUNK, LANE), _BF)
    rows = _bdot(jnp.concatenate(parts, axis=2), ones, _NN3, False)
    cols = _bdot(jnp.concatenate(parts, axis=1), ones, _TN3, False)
    lane = lax.broadcasted_iota(jnp.int32, (1, 1, LANE), 2)
    return (jnp.where(lane == 0, rows - cols, 0.0),)


_pair_diff.defvjp(lambda gcb: (_pair_diff(gcb), None), _pair_diff_bwd)


@jax.custom_vjp
def _saved_inverse(lmat, inv):
    return inv


def _saved_inverse_bwd(inv, g):
    return -_pdot(_pdot(inv, g, "tn"), inv, "nt"), jnp.zeros_like(inv)


_saved_inverse.defvjp(lambda lmat, inv: (inv, inv), _saved_inverse_bwd)


def gdn_step(s, q, k, v, gb, bb, inv_saved=None):
    c = CHUNK
    ii = lax.broadcasted_iota(jnp.int32, (1, c, c), 1)
    jj = lax.broadcasted_iota(jnp.int32, (1, c, c), 2)
    incl, strict = ii >= jj, ii > jj
    gcb = _chunk_cumsum(gb)
    diff = _pair_diff(gcb)
    decay = jnp.where(incl, jnp.exp(jnp.where(incl, diff, 0.0)), 0.0)
    kb, vb = k * bb, v * bb
    egc = jnp.exp(gcb)
    lmat = jnp.where(strict, _bmm_nt(kb, k) * decay, 0.0)
    if inv_saved is None:
        mm3 = _bh3_passes
        inv = (ii == jj).astype(F32) - lmat
        pw = mm3(lmat, lmat)
        for step in range(5):
            inv = inv + mm3(inv, pw)
            if step < 4:
                pw = mm3(pw, pw)
    else:
        mm3 = _bh3
        inv = _saved_inverse(lmat, inv_saved)
    u = mm3(inv, vb)
    w = mm3(inv, kb * egc)
    attn = _bmm_nt(q, k) * decay
    qd = q * egc
    g_end = jnp.sum(gb, axis=1, keepdims=True)
    kd = k * jnp.exp(g_end - gcb)
    v_new = u - _bmm(w, s)
    o = _bmm(qd, s) + _bmm(attn, v_new)
    s_new = s * jnp.exp(g_end) + _bmm_tn(kd, v_new)
    return s_new, o, inv


def gdn_fwd(q, k, v, gb, bb):
    h_n, t_len, d = q.shape
    n = t_len // CHUNK
    blk = pl.BlockSpec((h_n, CHUNK, d), lambda i: (0, i, 0))

    def body(q_ref, k_ref, v_ref, g_ref, b_ref, o_ref, sall_ref, inv_ref, s_s):
        @pl.when(pl.program_id(0) == 0)
        def _():
            s_s[...] = jnp.zeros_like(s_s)

        s = s_s[...]
        sall_ref[0] = s
        s_s[...], o_ref[...], inv_ref[0] = gdn_step(s, q_ref[...], k_ref[...], v_ref[...], g_ref[...], b_ref[...])

    return pl.pallas_call(
        body,
        name="gdn_fwd",
        grid=(n,),
        in_specs=[blk] * 5,
        out_specs=[blk, pl.BlockSpec((1, h_n, d, d), lambda i: (i, 0, 0, 0)),
                   pl.BlockSpec((1, h_n, CHUNK, CHUNK), lambda i: (i, 0, 0, 0))],
        out_shape=[jax.ShapeDtypeStruct((h_n, t_len, d), F32), jax.ShapeDtypeStruct((n, h_n, d, d), F32),
                   jax.ShapeDtypeStruct((n, h_n, CHUNK, CHUNK), F32)],
        scratch_shapes=[pltpu.VMEM((h_n, d, d), F32)],
        compiler_params=_cparams("arbitrary"),
    )(q, k, v, gb, bb)


def gdn_bwd(q, k, v, gb, bb, s_all, inv_all, do):
    h_n, t_len, d = q.shape
    n = t_len // CHUNK
    blk = pl.BlockSpec((h_n, CHUNK, d), lambda i: (0, n - 1 - i, 0))

    def body(q_ref, k_ref, v_ref, g_ref, b_ref, sall_ref, inv_ref, do_ref, dq_ref, dk_ref, dv_ref, dg_ref, db_ref,
             ds_s):
        @pl.when(pl.program_id(0) == 0)
        def _():
            ds_s[...] = jnp.zeros_like(ds_s)

        inv = inv_ref[0]
        _, pull = jax.vjp(lambda *a: gdn_step(*a, inv_saved=inv)[:2], sall_ref[0], q_ref[...], k_ref[...], v_ref[...],
                          g_ref[...], b_ref[...])
        ds_s[...], dq_ref[...], dk_ref[...], dv_ref[...], dg_ref[...], db_ref[...] = pull((ds_s[...], do_ref[...]))

    return pl.pallas_call(
        body,
        name="gdn_bwd",
        grid=(n,),
        in_specs=[blk] * 5 + [pl.BlockSpec((1, h_n, d, d), lambda i: (n - 1 - i, 0, 0, 0)),
                              pl.BlockSpec((1, h_n, CHUNK, CHUNK), lambda i: (n - 1 - i, 0, 0, 0)), blk],
        out_specs=[blk] * 5,
        out_shape=[jax.ShapeDtypeStruct((h_n, t_len, d), F32)] * 5,
        scratch_shapes=[pltpu.VMEM((h_n, d, d), F32)],
        compiler_params=_cparams("arbitrary"),
    )(q, k, v, gb, bb, s_all, inv_all, do)


def _pad_cols(a, n):
    return jnp.pad(a, ((0, 0), (0, n - a.shape[1])))


def arrange_w_in(w):
    pieces, start = [], 0
    for n in R_SPLITS:
        pieces.append(w[:, start:start + n])
        start += n
    cq, ckv, kr, mgate, gq, gk, gv, ga, gb, ggate = pieces
    return jnp.concatenate([mgate, gq, gk, gv, ggate, cq, ckv, _pad_cols(kr, LANE),
                            _pad_cols(jnp.concatenate([ga, gb], axis=1), LANE)], axis=1)


def unarrange_w_in(g):
    def cols(start, n):
        return g[:, start:start + n]
    return jnp.concatenate([cols(P_CQ, Q_LORA), cols(P_CKV, KV_LORA), cols(P_KR, ROPE), cols(P_MGATE, WIDTH),
                            cols(P_GQ, WIDTH), cols(P_GK, WIDTH), cols(P_GV, WIDTH), cols(P_GAB, HEADS),
                            cols(P_GAB + HEADS, HEADS), cols(P_GGATE, WIDTH)], axis=1)


def arrange_w_uq(w):
    w = w.reshape(w.shape[0], HEADS, QK)
    return jnp.pad(w, ((0, 0), (0, 0), (0, HEAD_PAD - QK))).reshape(w.shape[0], HEADS * HEAD_PAD)


def unarrange_w_uq(g):
    return g.reshape(g.shape[0], HEADS, HEAD_PAD)[:, :, :QK].reshape(g.shape[0], HEADS * QK)


def local_step(x, pos, tgt, p, on_early_grads=None, on_d_proj=None, on_weight_grads=None, on_d_xn=None,
               first_after=None, late_weights=None):
    t_len = x.shape[0]
    w_in = p["w_in"]
    norm_gain = p["norm_gain"].reshape(1, D_MODEL)
    qa_gain = p["mla_q_a_gain"].reshape(1, Q_LORA)
    kva_gain = p["mla_kv_a_gain"].reshape(1, KV_LORA)
    qg = _pad_cols(p["mla_q_norm_gain"].reshape(1, QK), HEAD_PAD)
    kg = _pad_cols(p["mla_k_norm_gain"].reshape(1, QK), HEAD_PAD)
    alog = _pad_cols(p["gdn_a_log"].reshape(1, HEADS), LANE)
    dtb = _pad_cols(p["gdn_dt_bias"].reshape(1, HEADS), LANE)
    og = p["gdn_out_norm_gain"].reshape(1, GDN_DIM)
    half = ROPE // 2
    inv_freq = jnp.power(ROPE_THETA, -jnp.arange(half, dtype=F32) / half)
    invf = _pad_cols(jnp.concatenate([inv_freq, inv_freq]).reshape(1, ROPE), LANE)

    rt = 256
    r = "r"
    first_after = jnp.zeros((SUBLANE, LANE), F32) if first_after is None else first_after
    (xn,) = rowwise("rms_x", lambda i, x_, gain_, after_: f_rms_x(i, x_, gain_), t_len, rt, [(x, (r, D_MODEL, 0))],
                    [norm_gain, first_after], [(r, D_MODEL, _BF)])
    proj = matmul("proj", xn, w_in, "nn")
    if late_weights is not None:
        p = {**p, **late_weights(proj)}
    w_uq, w_ukv, w_out = p["w_uq"], p["w_ukv"], p["w_out"]
    cw = p["gdn_conv_w"].reshape(CONV_W, 3 * WIDTH)
    cwq, cwk, cwv = cw[:, :WIDTH], cw[:, WIDTH:2 * WIDTH], cw[:, 2 * WIDTH:]
    cq_in = (proj, (r, Q_LORA, P_CQ // Q_LORA))
    ckv_in = (proj, (r, KV_LORA, P_CKV // KV_LORA))
    kr_in = (proj, (r, LANE, P_KR // LANE))
    mgate_in = (proj, (r, WIDTH, P_MGATE // WIDTH))
    ggate_in = (proj, (r, WIDTH, P_GGATE // WIDTH))
    gqkv_in = [(proj, (r, WIDTH, P_GQ // WIDTH)), (proj, (r, WIDTH, P_GK // WIDTH)), (proj, (r, WIDTH, P_GV // WIDTH))]
    gab_in = (proj, (r, LANE, P_GAB // LANE))
    halos = [(proj, ("halo", WIDTH, P_GQ // WIDTH)), (proj, ("halo", WIDTH, P_GK // WIDTH)),
             (proj, ("halo", WIDTH, P_GV // WIDTH))]

    q_lat, kv_lat = rowwise("lat", f_lat, t_len, rt, [cq_in, ckv_in], [qa_gain, kva_gain],
                            [(r, Q_LORA, _BF), (r, KV_LORA, _BF)])
    q_raw = matmul("q_up", q_lat, w_uq, "nn")
    kv_raw = matmul("kv_up", kv_lat, w_ukv, "nn")
    wide = HEADS * HEAD_PAD
    head_in = [(q_raw, (r, wide, 0)), (kv_raw, (r, wide, 0)), kr_in]
    pos_in = (pos, (r, 1, 0))
    q_full, k_full, v_mla = rowwise(
        "head", lambda i, qr, kvr, kr, ps, qg_, kg_, iv: f_head(i, qr, kvr, kr, qg_, kg_, ps, iv), t_len, rt,
        head_in + [pos_in], [qg, kg, invf],
        [("h", HEADS, HEAD_PAD, _BF), ("h", HEADS, HEAD_PAD, _BF), ("h", HEADS, LANE, _BF)])
    o_mla, lse = flash_fwd(q_full, k_full, v_mla)

    pre_in = gqkv_in + [gab_in] + halos
    pre_full = [cwq, cwk, cwv, alog, dtb]
    hkind = ("h", HEADS, GDN_DIM, F32)
    gq_n, gk_n, gv_n, g_b, b_b = rowwise("gdn_pre", f_gdn_pre, t_len, rt, pre_in, pre_full, [hkind] * 5)
    o_gdn, s_all, inv_all = gdn_fwd(gq_n, gk_n, gv_n, g_b, b_b)

    mix_in = [(o_mla, (r, WIDTH, 0)), mgate_in, (o_gdn, ("h",)), ggate_in]
    (mixed,) = rowwise("mix", f_mix, t_len, rt, mix_in, [og], [(r, 2 * WIDTH, _BF)])
    dy, dy_mx, loss_acc = out_proj_loss(mixed, w_out, x, tgt)
    loss = loss_acc[0, 0]

    d_mixed = matmul("d_mixed", dy_mx, w_out, "nt")
    g_w_out = matmul("g_w_out", mixed, dy_mx, "tn")

    def mix_bwd(i, o_mla_, mgate_, o_gdn_, ggate_, d_mixed_, og_):
        do_mla_, d_mgate_, do_gdn_, d_ggate_, g_og_ = _vjp_fn(f_mix, 5, 1)(i, o_mla_, mgate_, o_gdn_, ggate_, og_, d_mixed_)
        delta_ = jnp.stack([jnp.sum(o_mla_[:, LANE * h:LANE * (h + 1)] * do_mla_[:, LANE * h:LANE * (h + 1)],
                                    axis=-1, keepdims=True) for h in range(HEADS)])
        return do_mla_, d_mgate_, do_gdn_, d_ggate_, delta_, g_og_

    do_mla, d_mgate, do_gdn, d_ggate, delta, g_og = rowwise(
        "mix_bwd", mix_bwd, t_len, rt, mix_in + [(d_mixed, (r, 2 * WIDTH, 0))], [og],
        [(r, WIDTH, F32), (r, WIDTH, _BF), hkind, (r, WIDTH, _BF), ("h", HEADS, 1, F32)], [(1, GDN_DIM)])
    dq_n, dk_n, dv_n, dg_b, db_b = gdn_bwd(gq_n, gk_n, gv_n, g_b, b_b, s_all, inv_all, do_gdn)
    cts_in = [(a, ("h",)) for a in (dq_n, dk_n, dv_n, dg_b, db_b)]
    d_gq, d_gk, d_gv, d_gab, g_cwq, g_cwk, g_cwv, g_alog, g_dtb = rowwise(
        "gdn_pre_bwd", gdn_pre_bwd, t_len, rt, pre_in + cts_in, pre_full,
        [(r, WIDTH, _BF)] * 3 + [(r, LANE, _BF)],
        [(CONV_W, WIDTH)] * 3 + [(1, LANE)] * 2, carries=[(SUBLANE, WIDTH)] * 3, reverse=True)

    dq_full, dk_full, dv_mla = flash_bwd(q_full, k_full, v_mla, do_mla, lse, delta)
    head_cts = [(a, ("h",)) for a in (dq_full, dk_full, dv_mla)]

    def head_bwd(i, q_raw_, kv_raw_, kr_, pos_, dq_, dk_, dv_, qg_, kg_, invf_):
        return _vjp_fn(f_head, 5, 3)(i, q_raw_, kv_raw_, kr_, qg_, kg_, pos_, invf_, dq_, dk_, dv_)

    dq_raw, dkv_raw, d_kr, g_qg, g_kg = rowwise(
        "head_bwd", head_bwd, t_len, rt, head_in + [pos_in] + head_cts, [qg, kg, invf],
        [(r, wide, _BF), (r, wide, _BF), (r, LANE, _BF)], [(1, HEAD_PAD), (1, HEAD_PAD)])
    dq_lat = matmul("dq_lat", dq_raw, w_uq, "nt")
    g_w_uq = matmul("g_w_uq", q_lat, dq_raw, "tn")
    dkv_lat = matmul("dkv_lat", dkv_raw, w_ukv, "nt")
    g_w_ukv = matmul("g_w_ukv", kv_lat, dkv_raw, "tn")
    grads = {
        "w_uq": g_w_uq, "w_ukv": g_w_ukv, "gdn_conv_w": jnp.concatenate([g_cwq, g_cwk, g_cwv], axis=1),
        "w_out": g_w_out, "mla_q_norm_gain": g_qg[:, :QK], "mla_k_norm_gain": g_kg[:, :QK],
        "gdn_a_log": g_alog[:, :HEADS], "gdn_dt_bias": g_dtb[:, :HEADS], "gdn_out_norm_gain": g_og,
    }
    after = jnp.zeros((SUBLANE, LANE), F32) if on_early_grads is None else on_early_grads(grads)

    def lat_bwd(i, cq_, ckv_, dql_, dkl_, gq_, gkv_, after_):
        return _vjp_fn(f_lat, 4, 2)(i, cq_, ckv_, gq_, gkv_, dql_, dkl_)

    d_cq, d_ckv, grads["mla_q_a_gain"], grads["mla_kv_a_gain"] = rowwise(
        "lat_bwd", lat_bwd, t_len, rt, [cq_in, ckv_in, (dq_lat, (r, Q_LORA, 0)), (dkv_lat, (r, KV_LORA, 0))],
        [qa_gain, kva_gain, after], [(r, Q_LORA, _BF), (r, KV_LORA, _BF)], [(1, Q_LORA), (1, KV_LORA)])

    d_proj = jnp.concatenate([d_mgate, d_gq, d_gk, d_gv, d_ggate, d_cq, d_ckv, d_kr, d_gab], axis=1)
    after = None if on_d_proj is None else on_d_proj(d_proj)
    grads["w_in"] = matmul("g_w_in", xn, d_proj, "tn", after=after)
    after = None if on_weight_grads is None else on_weight_grads(grads)
    d_xn = matmul("d_xn", d_proj, w_in, "nt", after=after)
    after = jnp.zeros((SUBLANE, LANE), F32) if on_d_xn is None else on_d_xn(d_xn)

    def rms_x_bwd(i, x_, dxn_, dy_, gain_, after_):
        dx, dgain = _vjp_fn(f_rms_x, 2, 1)(i, x_, gain_, dxn_)
        return dx + dy_, dgain

    grad_x, grads["norm_gain"] = rowwise(
        "rms_x_bwd", rms_x_bwd, t_len, rt, [(x, (r, D_MODEL, 0)), (d_xn, (r, D_MODEL, 0)), (dy, (r, D_MODEL, 0))],
        [norm_gain, after], [(r, D_MODEL, F32)], [(1, D_MODEL)])
    return loss, grad_x, grads


MESH = pl.DeviceIdType.MESH
ANY = pl.BlockSpec(memory_space=pl.ANY)
CHIP_FLIPS = ((1, 0), (0, 1), (1, 1))


def _place():
    return lax.axis_index("x"), lax.axis_index("y"), lax.axis_index("c")


def _flip(v, f):
    return 1 - v if f else v


def all_gather(shards):
    n_arr = len(shards)

    def body(*refs):
        x_refs, o_refs = refs[:n_arr], refs[n_arr:2 * n_arr]
        send_sems, recv_sems, local_sems = refs[2 * n_arr:]
        x, y, c = _place()
        me, sibling = (x, y, c), (x, y, 1 - c)
        chips = [(_flip(x, fx), _flip(y, fy)) for fx, fy in CHIP_FLIPS]

        def copy(a, k, block, to, src=None):
            px, py, pc = block
            dst = o_refs[a].at[4 * px + 2 * py + pc]
            return pltpu.make_async_remote_copy(
                src_ref=dst if src is None else src, dst_ref=dst, send_sem=send_sems.at[a, k],
                recv_sem=recv_sems.at[a, k], device_id=to, device_id_type=MESH)

        mine, first, passed = [], [], []
        for a in range(n_arr):
            cp = pltpu.make_async_copy(x_refs[a], o_refs[a].at[4 * x + 2 * y + c], local_sems.at[a])
            cp.start()
            mine.append(cp)
            first.append(copy(a, 0, me, sibling, src=x_refs[a]))
            first += [copy(a, 1 + j, me, (*chip, c), src=x_refs[a]) for j, chip in enumerate(chips)]
        for cp in first:
            cp.start()
        for j, chip in enumerate(chips):
            for a in range(n_arr):
                copy(a, 1 + j, (*chip, c), me).wait_recv()
                cp = copy(a, 4 + j, (*chip, c), sibling)
                cp.start()
                passed.append(cp)
        for a in range(n_arr):
            copy(a, 0, sibling, me).wait_recv()
            for j, chip in enumerate(chips):
                copy(a, 4 + j, (*chip, 1 - c), me).wait_recv()
        for cp in first + passed:
            cp.wait_send()
        for cp in mine:
            cp.wait()

    return pl.pallas_call(
        body,
        name="all_gather",
        out_shape=[jax.ShapeDtypeStruct((N_DEV,) + s.shape, s.dtype) for s in shards],
        in_specs=[ANY] * n_arr,
        out_specs=[ANY] * n_arr,
        scratch_shapes=[pltpu.SemaphoreType.DMA((n_arr, 7)), pltpu.SemaphoreType.DMA((n_arr, 7)),
                        pltpu.SemaphoreType.DMA((n_arr,))],
    )(*shards)


HBM = pl.BlockSpec(memory_space=pltpu.HBM)
SEMS = pl.BlockSpec(memory_space=pltpu.SEMAPHORE)
SIDE_EFFECT = pltpu.SideEffectType.DATAFLOW_SIDE_EFFECTING


def core_routes(x, y, c):
    return [(2 * q + (1 - c), q, (x, y, 1 - c)) for q in range(4)]


def chip_routes(x, y, c):
    routes = []
    for j, (fx, fy) in enumerate(CHIP_FLIPS):
        px, py = _flip(x, fx), _flip(y, fy)
        routes.append((2 * px + py, j, (px, py, c)))
    return routes


def _route_copies(routes, n_routes, src_refs, land_refs, sems):
    x, y, c = _place()
    n_copies = len(src_refs) * n_routes
    return [pltpu.make_async_remote_copy(src_ref=src.at[s], dst_ref=land.at[d], send_sem=sems[a * n_routes + k],
                                         recv_sem=sems[n_copies + a * n_routes + k], device_id=dev,
                                         device_id_type=MESH)
            for a, (src, land) in enumerate(zip(src_refs, land_refs)) for k, (s, d, dev) in enumerate(routes(x, y, c))]


def gather_routes(x, y, c):
    me = 4 * x + 2 * y + c
    return [(0, me, (_flip(x, (k >> 2) & 1), _flip(y, (k >> 1) & 1), _flip(c, k & 1))) for k in range(1, N_DEV)]


def exchange_start(name, routes, n_routes, srcs, n_slots=None):
    n = len(srcs)
    n_sems = 2 * n * n_routes
    lands = [lax.empty((n_routes if n_slots is None else n_slots,) + s.shape[1:], s.dtype) for s in srcs]

    def body(*refs):
        for cp in _route_copies(routes, n_routes, refs[:n], refs[n:2 * n], refs[2 * n:2 * n + n_sems]):
            cp.start()
        refs[-1][...] = jnp.zeros_like(refs[-1])

    res = pl.pallas_call(
        body,
        name=name,
        out_shape=(*[pltpu.SemaphoreType.DMA(())] * n_sems, *[pltpu.HBM(a.shape, a.dtype) for a in srcs + lands],
                   jax.ShapeDtypeStruct((SUBLANE, LANE), F32)),
        in_specs=[HBM] * (2 * n),
        out_specs=(*[SEMS] * n_sems, *[HBM] * (2 * n), pl.BlockSpec(memory_space=pltpu.VMEM)),
        input_output_aliases={i: n_sems + i for i in range(2 * n)},
        compiler_params=pltpu.CompilerParams(has_side_effects=SIDE_EFFECT),
    )(*[pltpu.with_memory_space_constraint(a, pltpu.HBM) for a in srcs + lands])
    return (res[:n_sems], res[n_sems:-1]), res[-1]


def exchange_wait(name, routes, handle, after):
    sems, thru = handle
    n, n_sems = len(thru) // 2, len(sems)
    n_routes = n_sems // (2 * n)

    def body(*refs):
        for cp in _route_copies(routes, n_routes, refs[:n], refs[n:2 * n], refs[2 * n:2 * n + n_sems]):
            cp.wait_send()
            cp.wait_recv()

    res = pl.pallas_call(
        body,
        name=name,
        out_shape=tuple(pltpu.HBM(a.shape, a.dtype) for a in thru),
        in_specs=[HBM] * (2 * n) + [SEMS] * n_sems + [ANY],
        out_specs=tuple([HBM] * (2 * n)),
        input_output_aliases={i: i for i in range(2 * n)},
        compiler_params=pltpu.CompilerParams(has_side_effects=SIDE_EFFECT),
    )(*thru, *sems, after)
    return list(res[:n]), list(res[n:])


def gather_small(v):
    def body(v_ref, o_ref, send_sems, recv_sems, local_sem):
        x, y, c = _place()
        me = 4 * x + 2 * y + c
        mine = pltpu.make_async_copy(v_ref, o_ref.at[me], local_sem)
        mine.start()
        copies = []
        for k in range(1, N_DEV):
            fx, fy, fc = (k >> 2) & 1, (k >> 1) & 1, k & 1
            cp = pltpu.make_async_remote_copy(
                src_ref=v_ref, dst_ref=o_ref.at[me], send_sem=send_sems.at[k - 1], recv_sem=recv_sems.at[k - 1],
                device_id=(_flip(x, fx), _flip(y, fy), _flip(c, fc)), device_id_type=MESH)
            cp.start()
            copies.append(cp)
        for cp in copies:
            cp.wait()
        mine.wait()

    return pl.pallas_call(
        body,
        name="gather_small",
        out_shape=jax.ShapeDtypeStruct((N_DEV,) + v.shape, v.dtype),
        in_specs=[ANY],
        out_specs=ANY,
        scratch_shapes=[pltpu.SemaphoreType.DMA((N_DEV - 1,)), pltpu.SemaphoreType.DMA((N_DEV - 1,)),
                        pltpu.SemaphoreType.DMA],
    )(v)


def _row_tile(rows):
    for t in (256, 128, 64, 32, 16, 8):
        if rows % t == 0:
            return t
    return rows


def add_core_parts(name, g, recv, c_idx, wire):
    _, rows, cols = g.shape
    tr = _row_tile(rows)

    def body(c_ref, g_ref, r_ref, o_ref, w_ref):
        part = g_ref[...] + r_ref[...]
        o_ref[...] = part
        w_ref[...] = part.astype(w_ref.dtype)

    blk = pl.BlockSpec((1, tr, cols), lambda q, i, c_ref: (q, i, 0))
    return pl.pallas_call(
        body,
        name=name,
        grid_spec=pltpu.PrefetchScalarGridSpec(
            num_scalar_prefetch=1,
            grid=(4, rows // tr),
            in_specs=[pl.BlockSpec((1, tr, cols), lambda q, i, c_ref: (2 * q + c_ref[0], i, 0)), blk],
            out_specs=[blk, blk],
        ),
        out_shape=[jax.ShapeDtypeStruct((4, rows, cols), F32), jax.ShapeDtypeStruct((4, rows, cols), wire)],
        compiler_params=_cparams("parallel", "parallel"),
    )(c_idx, g, recv)


def _adamw(w, g, m, v):
    m = ADAM_B1 * m + (1.0 - ADAM_B1) * g
    v = ADAM_B2 * v + (1.0 - ADAM_B2) * (g * g)
    m_hat = m / (1.0 - ADAM_B1 ** ADAM_STEP)
    v_hat = v / (1.0 - ADAM_B2 ** ADAM_STEP)
    delta = -ADAM_LR * (m_hat / (jnp.sqrt(v_hat) + ADAM_EPS) + ADAM_WD * w)
    return delta, m, v


def adamw_sharded(name, parts, recv, q_idx, w, m, v):
    rows, cols = w.shape
    tr = _row_tile(rows)

    def body(q_ref, p_ref, r_ref, w_ref, m_ref, v_ref, g_out, d_out, m_out, v_out):
        g = p_ref[0] + r_ref[0].astype(F32) + r_ref[1].astype(F32) + r_ref[2].astype(F32)
        d, m_new, v_new = _adamw(w_ref[...], g, m_ref[...], v_ref[...])
        g_out[...], d_out[...], m_out[...], v_out[...] = g, d, m_new, v_new

    blk = pl.BlockSpec((tr, cols), lambda i, q_ref: (i, 0))
    return pl.pallas_call(
        body,
        name=name,
        grid_spec=pltpu.PrefetchScalarGridSpec(
            num_scalar_prefetch=1,
            grid=(rows // tr,),
            in_specs=[pl.BlockSpec((1, tr, cols), lambda i, q_ref: (q_ref[0], i, 0)),
                      pl.BlockSpec((3, tr, cols), lambda i, q_ref: (0, i, 0)), blk, blk, blk],
            out_specs=[blk] * 4,
        ),
        out_shape=[jax.ShapeDtypeStruct((rows, cols), F32)] * 4,
        compiler_params=_cparams("parallel"),
    )(q_idx, parts, recv, w, m, v)


def adamw_small(gathered, w, m, v):
    def body(g_ref, w_ref, m_ref, v_ref, g_out, d_out, m_out, v_out):
        g = g_ref[0]
        for j in range(1, N_DEV):
            g = g + g_ref[j]
        d, m_new, v_new = _adamw(w_ref[...], g, m_ref[...], v_ref[...])
        g_out[...], d_out[...], m_out[...], v_out[...] = g, d, m_new, v_new

    return pl.pallas_call(body, name="adamw_small", out_shape=[jax.ShapeDtypeStruct(w.shape, F32)] * 4)(gathered, w, m, v)


SHARDED = ("w_in", "w_uq", "w_ukv", "gdn_conv_w", "w_out")
SMALL = (("norm_gain", D_MODEL), ("mla_q_a_gain", Q_LORA), ("mla_kv_a_gain", KV_LORA), ("mla_q_norm_gain", QK),
         ("mla_k_norm_gain", QK), ("gdn_a_log", HEADS), ("gdn_dt_bias", HEADS), ("gdn_out_norm_gain", GDN_DIM))
WEIGHT_ORDER = ("norm_gain", "w_in", "mla_q_a_gain", "mla_kv_a_gain", "w_uq", "w_ukv", "mla_q_norm_gain",
                "mla_k_norm_gain", "gdn_conv_w", "gdn_a_log", "gdn_dt_bias", "gdn_out_norm_gain", "w_out")


def _pack_small(d):
    rows = []
    for name, n in SMALL:
        a = d[name].reshape(-1).astype(F32)
        n_pad = -(-n // LANE) * LANE
        rows.append(jnp.pad(a, (0, n_pad - n)).reshape(n_pad // LANE, LANE))
    packed = jnp.concatenate(rows, axis=0)
    return jnp.pad(packed, ((0, -packed.shape[0] % SUBLANE), (0, 0)))


def _unpack_small(packed):
    out, row = {}, 0
    for name, n in SMALL:
        n_rows = -(-n // LANE)
        out[name] = packed[row:row + n_rows].reshape(-1)[:n].reshape(1, n)
        row += n_rows
    return out


def kernel(x, positions, norm_gain, w_in, mla_q_a_gain, mla_kv_a_gain, w_uq, w_ukv, mla_q_norm_gain, mla_k_norm_gain, gdn_conv_w, gdn_a_log, gdn_dt_bias, gdn_out_norm_gain, w_out, loss_target, m_norm_gain, m_w_in, m_mla_q_a_gain, m_mla_kv_a_gain, m_w_uq, m_w_ukv, m_mla_q_norm_gain, m_mla_k_norm_gain, m_gdn_conv_w, m_gdn_a_log, m_gdn_dt_bias, m_gdn_out_norm_gain, m_w_out, v_norm_gain, v_w_in, v_mla_q_a_gain, v_mla_kv_a_gain, v_w_uq, v_w_ukv, v_mla_q_norm_gain, v_mla_k_norm_gain, v_gdn_conv_w, v_gdn_a_log, v_gdn_dt_bias, v_gdn_out_norm_gain, v_w_out):
    w = dict(norm_gain=norm_gain, w_in=w_in, mla_q_a_gain=mla_q_a_gain, mla_kv_a_gain=mla_kv_a_gain, w_uq=w_uq,
             w_ukv=w_ukv, mla_q_norm_gain=mla_q_norm_gain, mla_k_norm_gain=mla_k_norm_gain, gdn_conv_w=gdn_conv_w,
             gdn_a_log=gdn_a_log, gdn_dt_bias=gdn_dt_bias, gdn_out_norm_gain=gdn_out_norm_gain, w_out=w_out)
    m = dict(norm_gain=m_norm_gain, w_in=m_w_in, mla_q_a_gain=m_mla_q_a_gain, mla_kv_a_gain=m_mla_kv_a_gain,
             w_uq=m_w_uq, w_ukv=m_w_ukv, mla_q_norm_gain=m_mla_q_norm_gain, mla_k_norm_gain=m_mla_k_norm_gain,
             gdn_conv_w=m_gdn_conv_w, gdn_a_log=m_gdn_a_log, gdn_dt_bias=m_gdn_dt_bias,
             gdn_out_norm_gain=m_gdn_out_norm_gain, w_out=m_w_out)
    v = dict(norm_gain=v_norm_gain, w_in=v_w_in, mla_q_a_gain=v_mla_q_a_gain, mla_kv_a_gain=v_mla_kv_a_gain,
             w_uq=v_w_uq, w_ukv=v_w_ukv, mla_q_norm_gain=v_mla_q_norm_gain, mla_k_norm_gain=v_mla_k_norm_gain,
             gdn_conv_w=v_gdn_conv_w, gdn_a_log=v_gdn_a_log, gdn_dt_bias=v_gdn_dt_bias,
             gdn_out_norm_gain=v_gdn_out_norm_gain, w_out=v_w_out)
    t_len = x.shape[1]

    shards = [w[n][0] if n == "gdn_conv_w" else w[n][0].astype(_BF) for n in SHARDED]
    xi, yi, ci = _place()
    c_idx = jnp.reshape(ci, (1,)).astype(jnp.int32)
    q_idx = jnp.reshape(2 * xi + yi, (1,)).astype(jnp.int32)
    flight = {}

    def cols_whole(g):
        return g.transpose(1, 0, 2).reshape(g.shape[1], N_DEV * g.shape[2])

    def col_blocks(g):
        return g.reshape(g.shape[0], N_DEV, g.shape[1] // N_DEV).transpose(1, 0, 2)

    (a_w_in,) = all_gather(shards[:1])
    p = {n: w[n] for n, _ in SMALL}
    p["w_in"] = arrange_w_in(cols_whole(a_w_in))
    flight["weights"], weights_token = exchange_start(
        "gather_start", gather_routes, N_DEV - 1, [s[None] for s in shards[1:]], n_slots=N_DEV)

    def late_weights(proj):
        _, landed = exchange_wait("gather_wait", gather_routes, flight["weights"], proj)
        me = 4 * xi + 2 * yi + ci
        a_w_uq, a_w_ukv, a_cw, a_w_out = [lax.dynamic_update_slice(land, s[None], (me, 0, 0))
                                          for land, s in zip(landed, shards[1:])]
        return {"w_uq": arrange_w_uq(cols_whole(a_w_uq)), "w_ukv": cols_whole(a_w_ukv),
                "gdn_conv_w": cols_whole(a_cw), "w_out": a_w_out.reshape(N_DEV * a_w_out.shape[1], a_w_out.shape[2])}

    early, parts = SHARDED[1:], {}

    def core_stage(tag, names, blocks):
        flight["cores" + tag], token = exchange_start("cores_start" + tag, core_routes, 4, blocks)
        flight["names" + tag] = names
        return token

    def chip_stage(tag, after):
        blocks, landed = exchange_wait("cores_wait" + tag, core_routes, flight["cores" + tag], after)
        for n, g, r in zip(flight["names" + tag], blocks, landed):
            parts[n] = add_core_parts("add_" + n, g, r, c_idx, F32 if n == "gdn_conv_w" else _BF)
        wires = [parts[n][1] for n in flight["names" + tag]]
        flight["chips" + tag], token = exchange_start("chips_start" + tag, chip_routes, 3, wires)
        return token

    def on_early_grads(grads):
        return core_stage("_early", early, [
            col_blocks(unarrange_w_uq(grads["w_uq"])), col_blocks(grads["w_ukv"]), col_blocks(grads["gdn_conv_w"]),
            grads["w_out"].reshape(N_DEV, D_MODEL // N_DEV, D_MODEL)])

    def on_d_proj(d_proj):
        return chip_stage("_early", d_proj)

    def on_weight_grads(grads):
        return core_stage("", SHARDED[:1], [col_blocks(unarrange_w_in(grads["w_in"]))])

    def on_d_xn(d_xn):
        return chip_stage("", d_xn)

    pos = positions.reshape(t_len, 1).astype(F32)
    loss, grad_x, grads = local_step(x.reshape(t_len, D_MODEL), pos, loss_target.reshape(t_len, D_MODEL), p,
                                     on_early_grads=on_early_grads, on_d_proj=on_d_proj,
                                     on_weight_grads=on_weight_grads, on_d_xn=on_d_xn,
                                     first_after=weights_token, late_weights=late_weights)
    loss = lax.psum(loss, ("x", "y", "c"))
    out = {}
    small_all = gather_small(_pack_small(grads))
    res = adamw_small(small_all, _pack_small(w), _pack_small(m), _pack_small(v))
    unpacked = [_unpack_small(a) for a in res]
    for n, _ in SMALL:
        out[n] = [u[n] for u in unpacked]

    _, from_chips_early = exchange_wait("chips_wait_early", chip_routes, flight["chips_early"], res[0])
    _, from_chips = exchange_wait("chips_wait", chip_routes, flight["chips"], res[0])
    for n, rcv in zip(SHARDED, from_chips + from_chips_early):
        prt = parts[n][0]
        shape = w[n].shape
        res = adamw_sharded("adamw_" + n, prt, rcv, q_idx, w[n].reshape(shape[-2:]), m[n].reshape(shape[-2:]),
                            v[n].reshape(shape[-2:]))
        out[n] = [a.reshape(shape) for a in res]

    return (loss, grad_x.reshape(x.shape), *[out[n][0] for n in WEIGHT_ORDER], *[out[n][1] for n in WEIGHT_ORDER],
            *[out[n][2] for n in WEIGHT_ORDER], *[out[n][3] for n in WEIGHT_ORDER])
```

```python
import functools

import jax
import jax.numpy as jnp
from jax import lax
from jax.experimental import pallas as pl
from jax.experimental.pallas import tpu as pltpu

F32 = jnp.float32
_BF = jnp.bfloat16

D_MODEL = 2048
HEADS = 8
NOPE = 128
ROPE = 64
QK = NOPE + ROPE
Q_LORA = 512
KV_LORA = 256
HEAD_PAD = 256
GDN_DIM = 128
WIDTH = HEADS * 128
CONV_W = 4
CHUNK = 64
ROPE_THETA = 10000.0
EPS = 1e-6
N_DEV = 8
LANE = 128
SUBLANE = 8
VMEM_LIMIT = 48 * 1024 * 1024

ADAM_LR, ADAM_B1, ADAM_B2, ADAM_EPS, ADAM_WD, ADAM_STEP = 0.001, 0.9, 0.999, 1e-08, 0.01, 10

P_MGATE, P_GQ, P_GK, P_GV, P_GGATE = 0, 1024, 2048, 3072, 4096
P_CQ, P_CKV, P_KR, P_GAB = 5120, 5632, 5888, 6016
R_SPLITS = (512, 256, 64, 1024, 1024, 1024, 1024, 8, 8, 1024)


def _cparams(*sem):
    return pltpu.CompilerParams(dimension_semantics=sem, vmem_limit_bytes=VMEM_LIMIT)


def _d_nn(a, b):
    return jnp.dot(a.astype(_BF), b.astype(_BF), preferred_element_type=F32)


def _d_nt(a, b):
    return lax.dot_general(a.astype(_BF), b.astype(_BF), (((1,), (1,)), ((), ())), preferred_element_type=F32)


def _d_tn(a, b):
    return lax.dot_general(a.astype(_BF), b.astype(_BF), (((0,), (0,)), ((), ())), preferred_element_type=F32)


_NN3 = (((2,), (1,)), ((0,), (0,)))
_NT3 = (((2,), (2,)), ((0,), (0,)))
_TN3 = (((1,), (1,)), ((0,), (0,)))


def _bdot(a, b, dims, hi):
    if hi:
        return lax.dot_general(a, b, dims, preferred_element_type=F32, precision=hi)
    return lax.dot_general(a.astype(_BF), b.astype(_BF), dims, preferred_element_type=F32)


def _batched_matmuls(hi):
    nn = jax.custom_vjp(lambda a, b: _bdot(a, b, _NN3, hi))
    nt = jax.custom_vjp(lambda a, b: _bdot(a, b, _NT3, hi))
    tn = jax.custom_vjp(lambda a, b: _bdot(a, b, _TN3, hi))
    nn.defvjp(lambda a, b: (_bdot(a, b, _NN3, hi), (a, b)),
              lambda r, g: (_bdot(g, r[1], _NT3, hi), _bdot(r[0], g, _TN3, hi)))
    nt.defvjp(lambda a, b: (_bdot(a, b, _NT3, hi), (a, b)),
              lambda r, g: (_bdot(g, r[1], _NN3, hi), _bdot(g, r[0], _TN3, hi)))
    tn.defvjp(lambda a, b: (_bdot(a, b, _TN3, hi), (a, b)),
              lambda r, g: (_bdot(r[1], g, _NT3, hi), _bdot(r[0], g, _NN3, hi)))
    return nn, nt, tn


_bmm, _bmm_nt, _bmm_tn = _batched_matmuls(False)


def _split2(x):
    hi = x.astype(_BF)
    return hi, (x - hi.astype(F32)).astype(_BF)


def _pdot(a, b, mode):
    (a_hi, a_lo), (b_hi, b_lo) = _split2(a), _split2(b)
    a_ax, b_ax, dims = {"nn": (2, 1, _NN3), "nt": (2, 2, _NT3), "tn": (1, 1, _TN3)}[mode]
    lhs = jnp.concatenate([a_hi, a_lo, a_hi], axis=a_ax)
    rhs = jnp.concatenate([b_hi, b_hi, b_lo], axis=b_ax)
    return lax.dot_general(lhs, rhs, dims, preferred_element_type=F32)


def _packed_matmuls():
    nn = jax.custom_vjp(lambda a, b: _pdot(a, b, "nn"))
    nn.defvjp(lambda a, b: (_pdot(a, b, "nn"), (a, b)), lambda r, g: (_pdot(g, r[1], "nt"), _pdot(r[0], g, "tn")))
    return nn


_bh3 = _packed_matmuls()
_bh3_passes = _batched_matmuls(lax.Precision.HIGH)[0]


@functools.partial(jax.custom_vjp, nondiff_argnums=(1, 2))
def _roll(x, shift, axis):
    return pltpu.roll(x, shift, axis)


def _roll_fwd(x, shift, axis):
    return pltpu.roll(x, shift, axis), None


def _roll_bwd(shift, axis, _, g):
    n = g.shape[axis]
    return (pltpu.roll(g, (n - shift) % n, axis),)


_roll.defvjp(_roll_fwd, _roll_bwd)


def _rms(x, gain):
    return x * lax.rsqrt(jnp.mean(x * x, axis=-1, keepdims=True) + EPS) * gain


MM_TILE = 1024
MM_DEPTH = 2048


def matmul(name, a, b, mode, after=None):
    if mode == "nn":
        (m, k), (k2, n) = a.shape, b.shape
    elif mode == "nt":
        (m, k), (n, k2) = a.shape, b.shape
    else:
        (k, m), (k2, n) = a.shape, b.shape
    assert k == k2, (name, a.shape, b.shape)
    tm, tn, tk = min(MM_TILE, m), min(MM_TILE, n), min(MM_DEPTH, k)
    assert m % tm == 0 and n % tn == 0 and k % tk == 0, (name, m, n, k)
    dot = {"nn": _d_nn, "nt": _d_nt, "tn": _d_tn}[mode]

    def body(a_ref, b_ref, *rest):
        o_ref = rest[-1]
        kk = pl.program_id(2)
        part = dot(a_ref[...], b_ref[...])

        @pl.when(kk == 0)
        def _():
            o_ref[...] = part

        @pl.when(kk != 0)
        def _():
            o_ref[...] += part

    if mode == "nn":
        a_spec = pl.BlockSpec((tm, tk), lambda j, i, kk: (i, kk))
        b_spec = pl.BlockSpec((tk, tn), lambda j, i, kk: (kk, j))
    elif mode == "nt":
        a_spec = pl.BlockSpec((tm, tk), lambda j, i, kk: (i, kk))
        b_spec = pl.BlockSpec((tn, tk), lambda j, i, kk: (j, kk))
    else:
        a_spec = pl.BlockSpec((tk, tm), lambda j, i, kk: (kk, i))
        b_spec = pl.BlockSpec((tk, tn), lambda j, i, kk: (kk, j))
    return pl.pallas_call(
        body,
        name=name,
        grid=(n // tn, m // tm, k // tk),
        in_specs=[a_spec, b_spec] + ([] if after is None else [pl.BlockSpec(memory_space=pl.ANY)]),
        out_specs=pl.BlockSpec((tm, tn), lambda j, i, kk: (i, j)),
        out_shape=jax.ShapeDtypeStruct((m, n), F32),
        compiler_params=_cparams("parallel", "parallel", "arbitrary"),
    )(*((a, b) if after is None else (a, b, after)))


def out_proj_loss(mixed, w_out, x, tgt):
    (m, k), n = mixed.shape, w_out.shape[1]
    tm, tn = min(MM_TILE // 2, m), min(MM_TILE, n)
    assert m % tm == 0 and n % tn == 0

    def body(a_ref, b_ref, x_ref, t_ref, dy_ref, dy_mx_ref, loss_ref):
        first = (pl.program_id(0) == 0) & (pl.program_id(1) == 0)
        e = x_ref[...] + _d_nn(a_ref[...], b_ref[...]) - t_ref[...]
        dy = e * (1.0 / D_MODEL)
        dy_ref[...] = dy
        dy_mx_ref[...] = dy.astype(dy_mx_ref.dtype)
        part = jnp.zeros((SUBLANE, LANE), F32) + 0.5 * jnp.sum(e * e) * (1.0 / D_MODEL)

        @pl.when(first)
        def _():
            loss_ref[...] = part

        @pl.when(jnp.logical_not(first))
        def _():
            loss_ref[...] += part

    tile = pl.BlockSpec((tm, tn), lambda j, i: (i, j))
    return pl.pallas_call(
        body,
        name="out_proj_loss",
        grid=(n // tn, m // tm),
        in_specs=[pl.BlockSpec((tm, k), lambda j, i: (i, 0)), pl.BlockSpec((k, tn), lambda j, i: (0, j)), tile, tile],
        out_specs=[tile, tile, pl.BlockSpec((SUBLANE, LANE), lambda j, i: (0, 0))],
        out_shape=[jax.ShapeDtypeStruct((m, n), F32), jax.ShapeDtypeStruct((m, n), _BF),
                   jax.ShapeDtypeStruct((SUBLANE, LANE), F32)],
        compiler_params=_cparams("arbitrary", "arbitrary"),
    )(mixed, w_out, x, tgt)


def rowwise(name, fn, t_len, tile, row_in, full_in, row_out, acc_out=(), carries=(), reverse=False):
    tile = min(tile, t_len)
    n = t_len // tile
    assert t_len % tile == 0 and tile % SUBLANE == 0
    n_in, n_ro, n_acc, n_car = len(row_in) + len(full_in), len(row_out), len(acc_out), len(carries)

    def ti(i):
        return (n - 1 - i) if reverse else i

    in_specs, args = [], []
    for arr, kind in row_in:
        if kind[0] == "r":
            in_specs.append(pl.BlockSpec((tile, kind[1]), lambda i, c=kind[2]: (ti(i), c)))
        elif kind[0] == "h":
            in_specs.append(pl.BlockSpec((arr.shape[0], tile, arr.shape[2]), lambda i: (0, ti(i), 0)))
        else:
            in_specs.append(pl.BlockSpec(
                (SUBLANE, kind[1]), lambda i, c=kind[2]: (jnp.maximum(ti(i) * (tile // SUBLANE) - 1, 0), c)))
        args.append(arr)
    for arr in full_in:
        in_specs.append(pl.BlockSpec(arr.shape, lambda i, nd=arr.ndim: (0,) * nd))
        args.append(arr)
    out_specs, out_shape = [], []
    for kind in row_out:
        if kind[0] == "r":
            out_specs.append(pl.BlockSpec((tile, kind[1]), lambda i: (ti(i), 0)))
            out_shape.append(jax.ShapeDtypeStruct((t_len, kind[1]), kind[2]))
        else:
            out_specs.append(pl.BlockSpec((kind[1], tile, kind[2]), lambda i: (0, ti(i), 0)))
            out_shape.append(jax.ShapeDtypeStruct((kind[1], t_len, kind[2]), kind[3]))
    for shp in acc_out:
        out_specs.append(pl.BlockSpec(shp, lambda i, nd=len(shp): (0,) * nd))
        out_shape.append(jax.ShapeDtypeStruct(shp, F32))

    def body(*refs):
        in_refs = refs[:n_in]
        ro_refs = refs[n_in:n_in + n_ro]
        acc_refs = refs[n_in + n_ro:n_in + n_ro + n_acc]
        car_refs = refs[n_in + n_ro + n_acc:]
        step = pl.program_id(0)
        if n_car:
            @pl.when(step == 0)
            def _():
                for r in car_refs:
                    r[...] = jnp.zeros_like(r)
        vals = [r[...].astype(F32) for r in in_refs] + [r[...] for r in car_refs]
        outs = fn(ti(step), *vals)
        assert len(outs) == n_ro + n_acc + n_car, (name, len(outs))
        for r, o in zip(ro_refs, outs[:n_ro]):
            r[...] = o.astype(r.dtype)
        for r, o in zip(acc_refs, outs[n_ro:n_ro + n_acc]):
            @pl.when(step == 0)
            def _(r=r, o=o):
                r[...] = o

            @pl.when(step != 0)
            def _(r=r, o=o):
                r[...] += o
        for r, o in zip(car_refs, outs[n_ro + n_acc:]):
            r[...] = o

    res = pl.pallas_call(
        body,
        name=name,
        grid=(n,),
        in_specs=in_specs,
        out_specs=out_specs,
        out_shape=out_shape,
        scratch_shapes=[pltpu.VMEM(s, F32) for s in carries],
        compiler_params=_cparams("arbitrary"),
    )(*args)
    return list(res)


def _vjp_fn(fn, n_diff, n_out):
    def g(i, *a):
        ins, cts = a[:len(a) - n_out], a[len(a) - n_out:]
        diff, rest = ins[:n_diff], ins[n_diff:]
        _, pull = jax.vjp(lambda *d: tuple(fn(i, *d, *rest)), *diff)
        return tuple(pull(tuple(cts)))

    return g


def f_rms_x(i, x, gain):
    return (_rms(x, gain),)


def f_lat(i, cq, ckv, gq, gkv):
    return _rms(cq, gq), _rms(ckv, gkv)


def _rope_tables(pos, invf):
    ang = pos * invf
    lane = lax.broadcasted_iota(jnp.int32, (1, LANE), 1)
    cosv, sinv = jnp.cos(ang), jnp.sin(ang)
    half = ROPE // 2
    c = jnp.where(lane < ROPE, cosv, 0.0)
    sa = jnp.where(lane < half, -sinv, 0.0)
    sb = jnp.where((lane >= half) & (lane < ROPE), sinv, 0.0)
    return c, sa, sb


def _rope(xh, tabs):
    c, sa, sb = tabs
    half = ROPE // 2
    return xh * c + _roll(xh, LANE - half, 1) * sa + _roll(xh, half, 1) * sb


def f_head(i, q_raw, kv_raw, kr, qg, kg, pos, invf):
    tabs = _rope_tables(pos, invf)
    qs, ks, vs = [], [], []
    kr_ss = jnp.sum(kr * kr, axis=-1, keepdims=True)
    kr_rot = _rope(kr * kg[:, NOPE:], tabs)
    for h in range(HEADS):
        lo = q_raw[:, HEAD_PAD * h:HEAD_PAD * h + NOPE]
        hi = q_raw[:, HEAD_PAD * h + NOPE:HEAD_PAD * (h + 1)]
        ss = jnp.sum(lo * lo, axis=-1, keepdims=True) + jnp.sum(hi * hi, axis=-1, keepdims=True)
        r = lax.rsqrt(ss * (1.0 / QK) + EPS)
        qs.append(jnp.concatenate([lo * r * qg[:, :NOPE], _rope(hi * r * qg[:, NOPE:], tabs)], axis=1))
        lo = kv_raw[:, 2 * NOPE * h:2 * NOPE * h + NOPE]
        ss = jnp.sum(lo * lo, axis=-1, keepdims=True) + kr_ss
        r = lax.rsqrt(ss * (1.0 / QK) + EPS)
        ks.append(jnp.concatenate([lo * r * kg[:, :NOPE], kr_rot * r], axis=1))
        vs.append(kv_raw[:, 2 * NOPE * h + NOPE:2 * NOPE * (h + 1)])
    return jnp.stack(qs), jnp.stack(ks), jnp.stack(vs)


def f_mix(i, o_mla, mgate, o_gdn, ggate, og):
    parts = [o_mla * jax.nn.silu(mgate)]
    for h in range(HEADS):
        parts.append(_rms(o_gdn[h], og) * jax.nn.silu(ggate[:, LANE * h:LANE * (h + 1)]))
    return (jnp.concatenate(parts, axis=1),)


def _row(a, j):
    rows = lax.broadcasted_iota(jnp.int32, a.shape, 0)
    return jnp.sum(jnp.where(rows == j, a, 0.0), axis=0, keepdims=True)


def _shift_rows(x, halo, d):
    xs = _roll(x, d, 0)
    hs = _roll(halo, d, 0)
    r8 = lax.broadcasted_iota(jnp.int32, hs.shape, 0)
    top = jnp.where(r8 < d, hs, xs[:SUBLANE])
    return jnp.concatenate([top, xs[SUBLANE:]], axis=0)


def _conv_silu(x, halo, w):
    y = _row(w, CONV_W - 1) * x
    for j in range(CONV_W - 1):
        y = y + _row(w, j) * _shift_rows(x, halo, CONV_W - 1 - j)
    return jax.nn.silu(y)


def _head_select(offset):
    r = lax.broadcasted_iota(jnp.int32, (LANE, WIDTH), 0)
    c = lax.broadcasted_iota(jnp.int32, (LANE, WIDTH), 1)
    return (r == offset + lax.shift_right_logical(c, 7)).astype(_BF)


def _split3(x):
    x1 = x.astype(_BF)
    r1 = x - x1.astype(F32)
    x2 = r1.astype(_BF)
    return x1, x2, (r1 - x2.astype(F32)).astype(_BF)


@jax.custom_vjp
def _spread(x, sel):
    return _d_nn(jnp.concatenate(_split3(x), axis=1), jnp.concatenate([sel, sel, sel], axis=0))


def _spread_fwd(x, sel):
    return _spread(x, sel), sel


def _spread_bwd(sel, g):
    g1, g2, g3 = _split3(g)
    return _d_nt(g1, sel) + _d_nt(g2, sel) + _d_nt(g3, sel), jnp.zeros_like(sel)


_spread.defvjp(_spread_fwd, _spread_bwd)


def f_gdn_pre(i, gq, gk, gv, gab, hq, hk, hv, cwq, cwk, cwv, alog, dtb):
    live = jnp.where(i == 0, 0.0, 1.0)
    q = _conv_silu(gq, hq * live, cwq)
    k = _conv_silu(gk, hk * live, cwk)
    v = _conv_silu(gv, hv * live, cwv)
    g = _spread(-jnp.exp(alog) * jax.nn.softplus(gab + dtb), _head_select(0))
    beta = _spread(jax.nn.sigmoid(gab), _head_select(HEADS))
    qs, ks, vs, gs, bs = [], [], [], [], []
    for h in range(HEADS):
        sl = slice(LANE * h, LANE * (h + 1))
        qh, kh = q[:, sl], k[:, sl]
        qs.append(qh * lax.rsqrt(jnp.sum(qh * qh, axis=-1, keepdims=True) + EPS) * (GDN_DIM ** -0.5))
        ks.append(kh * lax.rsqrt(jnp.sum(kh * kh, axis=-1, keepdims=True) + EPS))
        vs.append(v[:, sl])
        gs.append(g[:, sl])
        bs.append(beta[:, sl])
    return jnp.stack(qs), jnp.stack(ks), jnp.stack(vs), jnp.stack(gs), jnp.stack(bs)


def gdn_pre_bwd(i, gq, gk, gv, gab, hq, hk, hv, dq, dk, dv, dg, db, cwq, cwk, cwv, alog, dtb, cq, ck, cv):
    grads = _vjp_fn(f_gdn_pre, 12, 5)(i, gq, gk, gv, gab, hq, hk, hv, cwq, cwk, cwv, alog, dtb, dq, dk, dv, dg, db)
    dgq, dgk, dgv, dgab, dhq, dhk, dhv, dcwq, dcwk, dcwv, dalog, ddtb = grads

    def add_tail(dx, carry):
        return jnp.concatenate([dx[:-SUBLANE], dx[-SUBLANE:] + carry], axis=0)

    return (add_tail(dgq, cq), add_tail(dgk, ck), add_tail(dgv, cv), dgab,
            dcwq, dcwk, dcwv, dalog, ddtb, dhq, dhk, dhv)


def _flash_tile(t_len):
    return min(512, t_len)


FLASH_HEADS = 4
FLASH_BWD_HEADS = 2
LOG2E = 1.4426950408889634


def _causal(rows0, shape):
    r = rows0 + lax.broadcasted_iota(jnp.int32, shape, 0)
    c = lax.broadcasted_iota(jnp.int32, shape, 1)
    return c <= r


def flash_fwd(q, k, v):
    h_n, t_len, _ = q.shape
    tq = _flash_tile(t_len)
    nq = t_len // tq
    hb = FLASH_HEADS
    kw = 2 if nq % 2 == 0 else 1
    tk = kw * tq
    c2 = (QK ** -0.5) * LOG2E
    pairs = [(i, j) for i in range(nq) for j in range(i // kw + 1)]
    qt = jnp.array([p[0] for p in pairs], jnp.int32)
    kt = jnp.array([p[1] for p in pairs], jnp.int32)

    def body(qt_ref, kt_ref, q_ref, k_ref, v_ref, o_ref, lse_ref, m_s, acc_s):
        step = pl.program_id(1)
        qi, kj = qt_ref[step], kt_ref[step]
        last = qi // kw

        @pl.when(kj == 0)
        def _():
            m_s[...] = jnp.full_like(m_s, -jnp.inf)
            acc_s[...] = jnp.zeros_like(acc_s)

        def tile(diagonal):
            s = _bdot(q_ref[...], k_ref[...], _NT3, False) * c2
            if diagonal:
                s = jnp.where(_causal((qi % kw) * tq, (tq, tk))[None], s, -jnp.inf)
            m_old = m_s[...]
            m_new = jnp.maximum(m_old, jnp.max(s, axis=-1, keepdims=True))
            p = jnp.exp2(s - m_new).astype(_BF)
            v_ones = jnp.concatenate([v_ref[...], jnp.ones((hb, tk, LANE), _BF)], axis=2)
            acc_s[...] = jnp.exp2(m_old - m_new) * acc_s[...] + _bdot(p, v_ones, _NN3, False)
            m_s[...] = m_new

        @pl.when(kj < last)
        def _():
            tile(False)

        @pl.when(kj == last)
        def _():
            tile(True)
            acc = acc_s[...]
            l_sum = acc[:, :, LANE:]
            o = acc[:, :, :LANE] / l_sum
            for hh in range(hb):
                o_ref[:, LANE * hh:LANE * (hh + 1)] = o[hh]
            lse_ref[...] = m_s[...] + jnp.log2(jnp.max(l_sum, axis=-1, keepdims=True))

    return pl.pallas_call(
        body,
        name="flash_fwd",
        grid_spec=pltpu.PrefetchScalarGridSpec(
            num_scalar_prefetch=2,
            grid=(h_n // hb, qt.shape[0]),
            in_specs=[
                pl.BlockSpec((hb, tq, HEAD_PAD), lambda h, s, qt_ref, kt_ref: (h, qt_ref[s], 0)),
                pl.BlockSpec((hb, tk, HEAD_PAD), lambda h, s, qt_ref, kt_ref: (h, kt_ref[s], 0)),
                pl.BlockSpec((hb, tk, LANE), lambda h, s, qt_ref, kt_ref: (h, kt_ref[s], 0)),
            ],
            out_specs=[
                pl.BlockSpec((tq, hb * LANE), lambda h, s, qt_ref, kt_ref: (qt_ref[s], h)),
                pl.BlockSpec((hb, tq, 1), lambda h, s, qt_ref, kt_ref: (h, qt_ref[s], 0)),
            ],
            scratch_shapes=[pltpu.VMEM((hb, tq, 1), F32), pltpu.VMEM((hb, tq, 2 * LANE), F32)],
        ),
        out_shape=[jax.ShapeDtypeStruct((t_len, h_n * LANE), F32), jax.ShapeDtypeStruct((h_n, t_len, 1), F32)],
        compiler_params=_cparams("parallel", "arbitrary"),
    )(qt, kt, q, k, v)


def flash_bwd(q, k, v, do, lse, delta):
    h_n, t_len, _ = q.shape
    tq = _flash_tile(t_len)
    nq = t_len // tq
    hb = FLASH_BWD_HEADS
    kw = 2 if nq % 2 == 0 else 1
    tk = kw * tq
    pairs = [(i, j) for j in range(nq // kw) for i in range(kw * j, nq)]
    n_steps = len(pairs)
    qt = jnp.array([p[0] for p in pairs], jnp.int32)
    kt = jnp.array([p[1] for p in pairs], jnp.int32)
    scale = QK ** -0.5
    c2 = scale * LOG2E

    def body(qt_ref, kt_ref, q_ref, k_ref, v_ref, do_ref, lse_ref, dl_ref, dq_hbm, dk_ref, dv_ref, dq_s, dq_sem):
        group, step = pl.program_id(0), pl.program_id(1)
        qi, kj = qt_ref[step], kt_ref[step]

        @pl.when(step == 0)
        def _():
            dq_s[...] = jnp.zeros_like(dq_s)

        def tile(diagonal):
            qb, kb = q_ref[...], k_ref[...]
            dob = jnp.stack([do_ref[:, LANE * hh:LANE * (hh + 1)] for hh in range(hb)])
            p = jnp.exp2(_bdot(qb, kb, _NT3, False) * c2 - lse_ref[...])
            if diagonal:
                p = jnp.where(_causal((qi % kw) * tq, (tq, tk))[None], p, 0.0)
            dv = _bdot(p, dob, _TN3, False)
            ds = p * (_bdot(dob, v_ref[...], _NT3, False) - dl_ref[...]) * scale
            dk = _bdot(ds, qb, _TN3, False)
            dq_s[:, pl.ds(pl.multiple_of(qi * tq, tq), tq), :] += _bdot(ds, kb, _NN3, False)
            return dk, dv

        @pl.when(qi == kw * kj)
        def _():
            dk_ref[...], dv_ref[...] = tile(True)

        @pl.when((qi != kw * kj) & (qi // kw == kj))
        def _():
            dk, dv = tile(True)
            dk_ref[...] += dk
            dv_ref[...] += dv

        @pl.when(qi // kw > kj)
        def _():
            dk, dv = tile(False)
            dk_ref[...] += dk
            dv_ref[...] += dv

        @pl.when(step == n_steps - 1)
        def _():
            out = pltpu.make_async_copy(dq_s, dq_hbm.at[pl.ds(group * hb, hb)], dq_sem)
            out.start()
            out.wait()

    def qmap(h, s, qt_ref, kt_ref):
        return (h, qt_ref[s], 0)

    def kmap(h, s, qt_ref, kt_ref):
        return (h, kt_ref[s], 0)

    return pl.pallas_call(
        body,
        name="flash_bwd",
        grid_spec=pltpu.PrefetchScalarGridSpec(
            num_scalar_prefetch=2,
            grid=(h_n // hb, n_steps),
            in_specs=[
                pl.BlockSpec((hb, tq, HEAD_PAD), qmap),
                pl.BlockSpec((hb, tk, HEAD_PAD), kmap),
                pl.BlockSpec((hb, tk, LANE), kmap),
                pl.BlockSpec((tq, hb * LANE), lambda h, s, qt_ref, kt_ref: (qt_ref[s], h)),
                pl.BlockSpec((hb, tq, 1), qmap),
                pl.BlockSpec((hb, tq, 1), qmap),
            ],
            out_specs=[
                pl.BlockSpec(memory_space=pl.ANY),
                pl.BlockSpec((hb, tk, HEAD_PAD), kmap),
                pl.BlockSpec((hb, tk, LANE), kmap),
            ],
            scratch_shapes=[pltpu.VMEM((hb, t_len, HEAD_PAD), F32), pltpu.SemaphoreType.DMA],
        ),
        out_shape=[
            jax.ShapeDtypeStruct((h_n, t_len, HEAD_PAD), F32),
            jax.ShapeDtypeStruct((h_n, t_len, HEAD_PAD), F32),
            jax.ShapeDtypeStruct((h_n, t_len, LANE), F32),
        ],
        compiler_params=_cparams("parallel", "arbitrary"),
    )(qt, kt, q, k, v, do, lse, delta)


def _tri_ones(h_n):
    ii = lax.broadcasted_iota(jnp.int32, (h_n, CHUNK, CHUNK), 1)
    jj = lax.broadcasted_iota(jnp.int32, (h_n, CHUNK, CHUNK), 2)
    return (ii >= jj).astype(_BF)


@jax.custom_vjp
def _chunk_cumsum(gb):
    tri = _tri_ones(gb.shape[0])
    return _bdot(jnp.concatenate([tri, tri, tri], axis=2), jnp.concatenate(_split3(gb), axis=1), _NN3, False)


def _chunk_cumsum_bwd(_, ct):
    tri = _tri_ones(ct.shape[0])
    return (_bdot(jnp.concatenate([tri, tri, tri], axis=1), jnp.concatenate(_split3(ct), axis=1), _TN3, False),)


_chunk_cumsum.defvjp(lambda gb: (_chunk_cumsum(gb), None), _chunk_cumsum_bwd)


@jax.custom_vjp
def _pair_diff(gcb):
    g1, g2, g3 = _split3(gcb)
    lane = lax.broadcasted_iota(jnp.int32, (1, 1, LANE), 2)
    one, zero = jnp.ones((), _BF), jnp.zeros((), _BF)
    a = jnp.where(lane == 0, g1, jnp.where(lane == 1, g2, jnp.where(lane == 2, g3, jnp.where(lane < 6, one, zero))))
    b = jnp.where(lane < 3, one, jnp.where(lane == 3, -g1, jnp.where(lane == 4, -g2, jnp.where(lane == 5, -g3, zero))))
    return _bdot(a, b, _NT3, False)


def _pair_diff_bwd(_, ct):
    parts = _split3(ct)
    ones = jnp.ones((ct.shape[0], 3 * CHUNK, LANE), _BF)
    rows = _bdot(jnp.concatenate(parts, axis=2), ones, _NN3, False)
    cols = _bdot(jnp.concatenate(parts, axis=1), ones, _TN3, False)
    lane = lax.broadcasted_iota(jnp.int32, (1, 1, LANE), 2)
    return (jnp.where(lane == 0, rows - cols, 0.0),)


_pair_diff.defvjp(lambda gcb: (_pair_diff(gcb), None), _pair_diff_bwd)


@jax.custom_vjp
def _saved_inverse(lmat, inv):
    return inv


def _saved_inverse_bwd(inv, g):
    return -_pdot(_pdot(inv, g, "tn"), inv, "nt"), jnp.zeros_like(inv)


_saved_inverse.defvjp(lambda lmat, inv: (inv, inv), _saved_inverse_bwd)


def gdn_step(s, q, k, v, gb, bb, inv_saved=None):
    c = CHUNK
    ii = lax.broadcasted_iota(jnp.int32, (1, c, c), 1)
    jj = lax.broadcasted_iota(jnp.int32, (1, c, c), 2)
    incl, strict = ii >= jj, ii > jj
    gcb = _chunk_cumsum(gb)
    diff = _pair_diff(gcb)
    decay = jnp.where(incl, jnp.exp(jnp.where(incl, diff, 0.0)), 0.0)
    kb, vb = k * bb, v * bb
    egc = jnp.exp(gcb)
    lmat = jnp.where(strict, _bmm_nt(kb, k) * decay, 0.0)
    if inv_saved is None:
        mm3 = _bh3_passes
        inv = (ii == jj).astype(F32) - lmat
        pw = mm3(lmat, lmat)
        for step in range(5):
            inv = inv + mm3(inv, pw)
            if step < 4:
                pw = mm3(pw, pw)
    else:
        mm3 = _bh3
        inv = _saved_inverse(lmat, inv_saved)
    u = mm3(inv, vb)
    w = mm3(inv, kb * egc)
    attn = _bmm_nt(q, k) * decay
    qd = q * egc
    g_end = jnp.sum(gb, axis=1, keepdims=True)
    kd = k * jnp.exp(g_end - gcb)
    v_new = u - _bmm(w, s)
    o = _bmm(qd, s) + _bmm(attn, v_new)
    s_new = s * jnp.exp(g_end) + _bmm_tn(kd, v_new)
    return s_new, o, inv


def gdn_fwd(q, k, v, gb, bb):
    h_n, t_len, d = q.shape
    n = t_len // CHUNK
    blk = pl.BlockSpec((h_n, CHUNK, d), lambda i: (0, i, 0))

    def body(q_ref, k_ref, v_ref, g_ref, b_ref, o_ref, sall_ref, inv_ref, s_s):
        @pl.when(pl.program_id(0) == 0)
        def _():
            s_s[...] = jnp.zeros_like(s_s)

        s = s_s[...]
        sall_ref[0] = s
        s_s[...], o_ref[...], inv_ref[0] = gdn_step(s, q_ref[...], k_ref[...], v_ref[...], g_ref[...], b_ref[...])

    return pl.pallas_call(
        body,
        name="gdn_fwd",
        grid=(n,),
        in_specs=[blk] * 5,
        out_specs=[blk, pl.BlockSpec((1, h_n, d, d), lambda i: (i, 0, 0, 0)),
                   pl.BlockSpec((1, h_n, CHUNK, CHUNK), lambda i: (i, 0, 0, 0))],
        out_shape=[jax.ShapeDtypeStruct((h_n, t_len, d), F32), jax.ShapeDtypeStruct((n, h_n, d, d), F32),
                   jax.ShapeDtypeStruct((n, h_n, CHUNK, CHUNK), F32)],
        scratch_shapes=[pltpu.VMEM((h_n, d, d), F32)],
        compiler_params=_cparams("arbitrary"),
    )(q, k, v, gb, bb)


def gdn_bwd(q, k, v, gb, bb, s_all, inv_all, do):
    h_n, t_len, d = q.shape
    n = t_len // CHUNK
    blk = pl.BlockSpec((h_n, CHUNK, d), lambda i: (0, n - 1 - i, 0))

    def body(q_ref, k_ref, v_ref, g_ref, b_ref, sall_ref, inv_ref, do_ref, dq_ref, dk_ref, dv_ref, dg_ref, db_ref,
             ds_s):
        @pl.when(pl.program_id(0) == 0)
        def _():
            ds_s[...] = jnp.zeros_like(ds_s)

        inv = inv_ref[0]
        _, pull = jax.vjp(lambda *a: gdn_step(*a, inv_saved=inv)[:2], sall_ref[0], q_ref[...], k_ref[...], v_ref[...],
                          g_ref[...], b_ref[...])
        ds_s[...], dq_ref[...], dk_ref[...], dv_ref[...], dg_ref[...], db_ref[...] = pull((ds_s[...], do_ref[...]))

    return pl.pallas_call(
        body,
        name="gdn_bwd",
        grid=(n,),
        in_specs=[blk] * 5 + [pl.BlockSpec((1, h_n, d, d), lambda i: (n - 1 - i, 0, 0, 0)),
                              pl.BlockSpec((1, h_n, CHUNK, CHUNK), lambda i: (n - 1 - i, 0, 0, 0)), blk],
        out_specs=[blk] * 5,
        out_shape=[jax.ShapeDtypeStruct((h_n, t_len, d), F32)] * 5,
        scratch_shapes=[pltpu.VMEM((h_n, d, d), F32)],
        compiler_params=_cparams("arbitrary"),
    )(q, k, v, gb, bb, s_all, inv_all, do)


def _pad_cols(a, n):
    return jnp.pad(a, ((0, 0), (0, n - a.shape[1])))


def arrange_w_in(w):
    pieces, start = [], 0
    for n in R_SPLITS:
        pieces.append(w[:, start:start + n])
        start += n
    cq, ckv, kr, mgate, gq, gk, gv, ga, gb, ggate = pieces
    return jnp.concatenate([mgate, gq, gk, gv, ggate, cq, ckv, _pad_cols(kr, LANE),
                            _pad_cols(jnp.concatenate([ga, gb], axis=1), LANE)], axis=1)


def unarrange_w_in(g):
    def cols(start, n):
        return g[:, start:start + n]
    return jnp.concatenate([cols(P_CQ, Q_LORA), cols(P_CKV, KV_LORA), cols(P_KR, ROPE), cols(P_MGATE, WIDTH),
                            cols(P_GQ, WIDTH), cols(P_GK, WIDTH), cols(P_GV, WIDTH), cols(P_GAB, HEADS),
                            cols(P_GAB + HEADS, HEADS), cols(P_GGATE, WIDTH)], axis=1)


def arrange_w_uq(w):
    w = w.reshape(w.shape[0], HEADS, QK)
    return jnp.pad(w, ((0, 0), (0, 0), (0, HEAD_PAD - QK))).reshape(w.shape[0], HEADS * HEAD_PAD)


def unarrange_w_uq(g):
    return g.reshape(g.shape[0], HEADS, HEAD_PAD)[:, :, :QK].reshape(g.shape[0], HEADS * QK)


def local_step(x, pos, tgt, p, on_early_grads=None, on_d_proj=None, on_weight_grads=None, on_d_xn=None,
               first_after=None, late_weights=None):
    t_len = x.shape[0]
    w_in = p["w_in"]
    norm_gain = p["norm_gain"].reshape(1, D_MODEL)
    qa_gain = p["mla_q_a_gain"].reshape(1, Q_LORA)
    kva_gain = p["mla_kv_a_gain"].reshape(1, KV_LORA)
    qg = _pad_cols(p["mla_q_norm_gain"].reshape(1, QK), HEAD_PAD)
    kg = _pad_cols(p["mla_k_norm_gain"].reshape(1, QK), HEAD_PAD)
    alog = _pad_cols(p["gdn_a_log"].reshape(1, HEADS), LANE)
    dtb = _pad_cols(p["gdn_dt_bias"].reshape(1, HEADS), LANE)
    og = p["gdn_out_norm_gain"].reshape(1, GDN_DIM)
    half = ROPE // 2
    inv_freq = jnp.power(ROPE_THETA, -jnp.arange(half, dtype=F32) / half)
    invf = _pad_cols(jnp.concatenate([inv_freq, inv_freq]).reshape(1, ROPE), LANE)

    rt = 256
    r = "r"
    first_after = jnp.zeros((SUBLANE, LANE), F32) if first_after is None else first_after
    (xn,) = rowwise("rms_x", lambda i, x_, gain_, after_: f_rms_x(i, x_, gain_), t_len, rt, [(x, (r, D_MODEL, 0))],
                    [norm_gain, first_after], [(r, D_MODEL, _BF)])
    proj = matmul("proj", xn, w_in, "nn")
    if late_weights is not None:
        p = {**p, **late_weights(proj)}
    w_uq, w_ukv, w_out = p["w_uq"], p["w_ukv"], p["w_out"]
    cw = p["gdn_conv_w"].reshape(CONV_W, 3 * WIDTH)
    cwq, cwk, cwv = cw[:, :WIDTH], cw[:, WIDTH:2 * WIDTH], cw[:, 2 * WIDTH:]
    cq_in = (proj, (r, Q_LORA, P_CQ // Q_LORA))
    ckv_in = (proj, (r, KV_LORA, P_CKV // KV_LORA))
    kr_in = (proj, (r, LANE, P_KR // LANE))
    mgate_in = (proj, (r, WIDTH, P_MGATE // WIDTH))
    ggate_in = (proj, (r, WIDTH, P_GGATE // WIDTH))
    gqkv_in = [(proj, (r, WIDTH, P_GQ // WIDTH)), (proj, (r, WIDTH, P_GK // WIDTH)), (proj, (r, WIDTH, P_GV // WIDTH))]
    gab_in = (proj, (r, LANE, P_GAB // LANE))
    halos = [(proj, ("halo", WIDTH, P_GQ // WIDTH)), (proj, ("halo", WIDTH, P_GK // WIDTH)),
             (proj, ("halo", WIDTH, P_GV // WIDTH))]

    q_lat, kv_lat = rowwise("lat", f_lat, t_len, rt, [cq_in, ckv_in], [qa_gain, kva_gain],
                            [(r, Q_LORA, _BF), (r, KV_LORA, _BF)])
    q_raw = matmul("q_up", q_lat, w_uq, "nn")
    kv_raw = matmul("kv_up", kv_lat, w_ukv, "nn")
    wide = HEADS * HEAD_PAD
    head_in = [(q_raw, (r, wide, 0)), (kv_raw, (r, wide, 0)), kr_in]
    pos_in = (pos, (r, 1, 0))
    q_full, k_full, v_mla = rowwise(
        "head", lambda i, qr, kvr, kr, ps, qg_, kg_, iv: f_head(i, qr, kvr, kr, qg_, kg_, ps, iv), t_len, rt,
        head_in + [pos_in], [qg, kg, invf],
        [("h", HEADS, HEAD_PAD, _BF), ("h", HEADS, HEAD_PAD, _BF), ("h", HEADS, LANE, _BF)])
    o_mla, lse = flash_fwd(q_full, k_full, v_mla)

    pre_in = gqkv_in + [gab_in] + halos
    pre_full = [cwq, cwk, cwv, alog, dtb]
    hkind = ("h", HEADS, GDN_DIM, F32)
    gq_n, gk_n, gv_n, g_b, b_b = rowwise("gdn_pre", f_gdn_pre, t_len, rt, pre_in, pre_full, [hkind] * 5)
    o_gdn, s_all, inv_all = gdn_fwd(gq_n, gk_n, gv_n, g_b, b_b)

    mix_in = [(o_mla, (r, WIDTH, 0)), mgate_in, (o_gdn, ("h",)), ggate_in]
    (mixed,) = rowwise("mix", f_mix, t_len, rt, mix_in, [og], [(r, 2 * WIDTH, _BF)])
    dy, dy_mx, loss_acc = out_proj_loss(mixed, w_out, x, tgt)
    loss = loss_acc[0, 0]

    d_mixed = matmul("d_mixed", dy_mx, w_out, "nt")
    g_w_out = matmul("g_w_out", mixed, dy_mx, "tn")

    def mix_bwd(i, o_mla_, mgate_, o_gdn_, ggate_, d_mixed_, og_):
        do_mla_, d_mgate_, do_gdn_, d_ggate_, g_og_ = _vjp_fn(f_mix, 5, 1)(i, o_mla_, mgate_, o_gdn_, ggate_, og_, d_mixed_)
        delta_ = jnp.stack([jnp.sum(o_mla_[:, LANE * h:LANE * (h + 1)] * do_mla_[:, LANE * h:LANE * (h + 1)],
                                    axis=-1, keepdims=True) for h in range(HEADS)])
        return do_mla_, d_mgate_, do_gdn_, d_ggate_, delta_, g_og_

    do_mla, d_mgate, do_gdn, d_ggate, delta, g_og = rowwise(
        "mix_bwd", mix_bwd, t_len, rt, mix_in + [(d_mixed, (r, 2 * WIDTH, 0))], [og],
        [(r, WIDTH, F32), (r, WIDTH, _BF), hkind, (r, WIDTH, _BF), ("h", HEADS, 1, F32)], [(1, GDN_DIM)])
    dq_n, dk_n, dv_n, dg_b, db_b = gdn_bwd(gq_n, gk_n, gv_n, g_b, b_b, s_all, inv_all, do_gdn)
    cts_in = [(a, ("h",)) for a in (dq_n, dk_n, dv_n, dg_b, db_b)]
    d_gq, d_gk, d_gv, d_gab, g_cwq, g_cwk, g_cwv, g_alog, g_dtb = rowwise(
        "gdn_pre_bwd", gdn_pre_bwd, t_len, rt, pre_in + cts_in, pre_full,
        [(r, WIDTH, _BF)] * 3 + [(r, LANE, _BF)],
        [(CONV_W, WIDTH)] * 3 + [(1, LANE)] * 2, carries=[(SUBLANE, WIDTH)] * 3, reverse=True)

    dq_full, dk_full, dv_mla = flash_bwd(q_full, k_full, v_mla, do_mla, lse, delta)
    head_cts = [(a, ("h",)) for a in (dq_full, dk_full, dv_mla)]

    def head_bwd(i, q_raw_, kv_raw_, kr_, pos_, dq_, dk_, dv_, qg_, kg_, invf_):
        return _vjp_fn(f_head, 5, 3)(i, q_raw_, kv_raw_, kr_, qg_, kg_, pos_, invf_, dq_, dk_, dv_)

    dq_raw, dkv_raw, d_kr, g_qg, g_kg = rowwise(
        "head_bwd", head_bwd, t_len, rt, head_in + [pos_in] + head_cts, [qg, kg, invf],
        [(r, wide, _BF), (r, wide, _BF), (r, LANE, _BF)], [(1, HEAD_PAD), (1, HEAD_PAD)])
    dq_lat = matmul("dq_lat", dq_raw, w_uq, "nt")
    g_w_uq = matmul("g_w_uq", q_lat, dq_raw, "tn")
    dkv_lat = matmul("dkv_lat", dkv_raw, w_ukv, "nt")
    g_w_ukv = matmul("g_w_ukv", kv_lat, dkv_raw, "tn")
    grads = {
        "w_uq": g_w_uq, "w_ukv": g_w_ukv, "gdn_conv_w": jnp.concatenate([g_cwq, g_cwk, g_cwv], axis=1),
        "w_out": g_w_out, "mla_q_norm_gain": g_qg[:, :QK], "mla_k_norm_gain": g_kg[:, :QK],
        "gdn_a_log": g_alog[:, :HEADS], "gdn_dt_bias": g_dtb[:, :HEADS], "gdn_out_norm_gain": g_og,
    }
    after = jnp.zeros((SUBLANE, LANE), F32) if on_early_grads is None else on_early_grads(grads)

    def lat_bwd(i, cq_, ckv_, dql_, dkl_, gq_, gkv_, after_):
        return _vjp_fn(f_lat, 4, 2)(i, cq_, ckv_, gq_, gkv_, dql_, dkl_)

    d_cq, d_ckv, grads["mla_q_a_gain"], grads["mla_kv_a_gain"] = rowwise(
        "lat_bwd", lat_bwd, t_len, rt, [cq_in, ckv_in, (dq_lat, (r, Q_LORA, 0)), (dkv_lat, (r, KV_LORA, 0))],
        [qa_gain, kva_gain, after], [(r, Q_LORA, _BF), (r, KV_LORA, _BF)], [(1, Q_LORA), (1, KV_LORA)])

    d_proj = jnp.concatenate([d_mgate, d_gq, d_gk, d_gv, d_ggate, d_cq, d_ckv, d_kr, d_gab], axis=1)
    after = None if on_d_proj is None else on_d_proj(d_proj)
    grads["w_in"] = matmul("g_w_in", xn, d_proj, "tn", after=after)
    after = None if on_weight_grads is None else on_weight_grads(grads)
    d_xn = matmul("d_xn", d_proj, w_in, "nt", after=after)
    after = jnp.zeros((SUBLANE, LANE), F32) if on_d_xn is None else on_d_xn(d_xn)

    def rms_x_bwd(i, x_, dxn_, dy_, gain_, after_):
        dx, dgain = _vjp_fn(f_rms_x, 2, 1)(i, x_, gain_, dxn_)
        return dx + dy_, dgain

    grad_x, grads["norm_gain"] = rowwise(
        "rms_x_bwd", rms_x_bwd, t_len, rt, [(x, (r, D_MODEL, 0)), (d_xn, (r, D_MODEL, 0)), (dy, (r, D_MODEL, 0))],
        [norm_gain, after], [(r, D_MODEL, F32)], [(1, D_MODEL)])
    return loss, grad_x, grads


MESH = pl.DeviceIdType.MESH
ANY = pl.BlockSpec(memory_space=pl.ANY)
CHIP_FLIPS = ((1, 0), (0, 1), (1, 1))


def _place():
    return lax.axis_index("x"), lax.axis_index("y"), lax.axis_index("c")


def _flip(v, f):
    return 1 - v if f else v


def all_gather(shards):
    n_arr = len(shards)

    def body(*refs):
        x_refs, o_refs = refs[:n_arr], refs[n_arr:2 * n_arr]
        send_sems, recv_sems, local_sems = refs[2 * n_arr:]
        x, y, c = _place()
        me, sibling = (x, y, c), (x, y, 1 - c)
        chips = [(_flip(x, fx), _flip(y, fy)) for fx, fy in CHIP_FLIPS]

        def copy(a, k, block, to, src=None):
            px, py, pc = block
            dst = o_refs[a].at[4 * px + 2 * py + pc]
            return pltpu.make_async_remote_copy(
                src_ref=dst if src is None else src, dst_ref=dst, send_sem=send_sems.at[a, k],
                recv_sem=recv_sems.at[a, k], device_id=to, device_id_type=MESH)

        mine, first, passed = [], [], []
        for a in range(n_arr):
            cp = pltpu.make_async_copy(x_refs[a], o_refs[a].at[4 * x + 2 * y + c], local_sems.at[a])
            cp.start()
            mine.append(cp)
            first.append(copy(a, 0, me, sibling, src=x_refs[a]))
            first += [copy(a, 1 + j, me, (*chip, c), src=x_refs[a]) for j, chip in enumerate(chips)]
        for cp in first:
            cp.start()
        for j, chip in enumerate(chips):
            for a in range(n_arr):
                copy(a, 1 + j, (*chip, c), me).wait_recv()
                cp = copy(a, 4 + j, (*chip, c), sibling)
                cp.start()
                passed.append(cp)
        for a in range(n_arr):
            copy(a, 0, sibling, me).wait_recv()
            for j, chip in enumerate(chips):
                copy(a, 4 + j, (*chip, 1 - c), me).wait_recv()
        for cp in first + passed:
            cp.wait_send()
        for cp in mine:
            cp.wait()

    return pl.pallas_call(
        body,
        name="all_gather",
        out_shape=[jax.ShapeDtypeStruct((N_DEV,) + s.shape, s.dtype) for s in shards],
        in_specs=[ANY] * n_arr,
        out_specs=[ANY] * n_arr,
        scratch_shapes=[pltpu.SemaphoreType.DMA((n_arr, 7)), pltpu.SemaphoreType.DMA((n_arr, 7)),
                        pltpu.SemaphoreType.DMA((n_arr,))],
    )(*shards)


HBM = pl.BlockSpec(memory_space=pltpu.HBM)
SEMS = pl.BlockSpec(memory_space=pltpu.SEMAPHORE)
SIDE_EFFECT = pltpu.SideEffectType.DATAFLOW_SIDE_EFFECTING


def core_routes(x, y, c):
    return [(2 * q + (1 - c), q, (x, y, 1 - c)) for q in range(4)]


def chip_routes(x, y, c):
    routes = []
    for j, (fx, fy) in enumerate(CHIP_FLIPS):
        px, py = _flip(x, fx), _flip(y, fy)
        routes.append((2 * px + py, j, (px, py, c)))
    return routes


def _route_copies(routes, n_routes, src_refs, land_refs, sems):
    x, y, c = _place()
    n_copies = len(src_refs) * n_routes
    return [pltpu.make_async_remote_copy(src_ref=src.at[s], dst_ref=land.at[d], send_sem=sems[a * n_routes + k],
                                         recv_sem=sems[n_copies + a * n_routes + k], device_id=dev,
                                         device_id_type=MESH)
            for a, (src, land) in enumerate(zip(src_refs, land_refs)) for k, (s, d, dev) in enumerate(routes(x, y, c))]


def gather_routes(x, y, c):
    me = 4 * x + 2 * y + c
    return [(0, me, (_flip(x, (k >> 2) & 1), _flip(y, (k >> 1) & 1), _flip(c, k & 1))) for k in range(1, N_DEV)]


def exchange_start(name, routes, n_routes, srcs, n_slots=None):
    n = len(srcs)
    n_sems = 2 * n * n_routes
    lands = [lax.empty((n_routes if n_slots is None else n_slots,) + s.shape[1:], s.dtype) for s in srcs]

    def body(*refs):
        for cp in _route_copies(routes, n_routes, refs[:n], refs[n:2 * n], refs[2 * n:2 * n + n_sems]):
            cp.start()
        refs[-1][...] = jnp.zeros_like(refs[-1])

    res = pl.pallas_call(
        body,
        name=name,
        out_shape=(*[pltpu.SemaphoreType.DMA(())] * n_sems, *[pltpu.HBM(a.shape, a.dtype) for a in srcs + lands],
                   jax.ShapeDtypeStruct((SUBLANE, LANE), F32)),
        in_specs=[HBM] * (2 * n),
        out_specs=(*[SEMS] * n_sems, *[HBM] * (2 * n), pl.BlockSpec(memory_space=pltpu.VMEM)),
        input_output_aliases={i: n_sems + i for i in range(2 * n)},
        compiler_params=pltpu.CompilerParams(has_side_effects=SIDE_EFFECT),
    )(*[pltpu.with_memory_space_constraint(a, pltpu.HBM) for a in srcs + lands])
    return (res[:n_sems], res[n_sems:-1]), res[-1]


def exchange_wait(name, routes, handle, after):
    sems, thru = handle
    n, n_sems = len(thru) // 2, len(sems)
    n_routes = n_sems // (2 * n)

    def body(*refs):
        for cp in _route_copies(routes, n_routes, refs[:n], refs[n:2 * n], refs[2 * n:2 * n + n_sems]):
            cp.wait_send()
            cp.wait_recv()

    res = pl.pallas_call(
        body,
        name=name,
        out_shape=tuple(pltpu.HBM(a.shape, a.dtype) for a in thru),
        in_specs=[HBM] * (2 * n) + [SEMS] * n_sems + [ANY],
        out_specs=tuple([HBM] * (2 * n)),
        input_output_aliases={i: i for i in range(2 * n)},
        compiler_params=pltpu.CompilerParams(has_side_effects=SIDE_EFFECT),
    )(*thru, *sems, after)
    return list(res[:n]), list(res[n:])


def gather_small(v):
    def body(v_ref, o_ref, send_sems, recv_sems, local_sem):
        x, y, c = _place()
        me = 4 * x + 2 * y + c
        mine = pltpu.make_async_copy(v_ref, o_ref.at[me], local_sem)
        mine.start()
        copies = []
        for k in range(1, N_DEV):
            fx, fy, fc = (k >> 2) & 1, (k >> 1) & 1, k & 1
            cp = pltpu.make_async_remote_copy(
                src_ref=v_ref, dst_ref=o_ref.at[me], send_sem=send_sems.at[k - 1], recv_sem=recv_sems.at[k - 1],
                device_id=(_flip(x, fx), _flip(y, fy), _flip(c, fc)), device_id_type=MESH)
            cp.start()
            copies.append(cp)
        for cp in copies:
            cp.wait()
        mine.wait()

    return pl.pallas_call(
        body,
        name="gather_small",
        out_shape=jax.ShapeDtypeStruct((N_DEV,) + v.shape, v.dtype),
        in_specs=[ANY],
        out_specs=ANY,
        scratch_shapes=[pltpu.SemaphoreType.DMA((N_DEV - 1,)), pltpu.SemaphoreType.DMA((N_DEV - 1,)),
                        pltpu.SemaphoreType.DMA],
    )(v)


def _row_tile(rows):
    for t in (256, 128, 64, 32, 16, 8):
        if rows % t == 0:
            return t
    return rows


def add_core_parts(name, g, recv, c_idx, wire):
    _, rows, cols = g.shape
    tr = _row_tile(rows)

    def body(c_ref, g_ref, r_ref, o_ref, w_ref):
        part = g_ref[...] + r_ref[...]
        o_ref[...] = part
        w_ref[...] = part.astype(w_ref.dtype)

    blk = pl.BlockSpec((1, tr, cols), lambda q, i, c_ref: (q, i, 0))
    return pl.pallas_call(
        body,
        name=name,
        grid_spec=pltpu.PrefetchScalarGridSpec(
            num_scalar_prefetch=1,
            grid=(4, rows // tr),
            in_specs=[pl.BlockSpec((1, tr, cols), lambda q, i, c_ref: (2 * q + c_ref[0], i, 0)), blk],
            out_specs=[blk, blk],
        ),
        out_shape=[jax.ShapeDtypeStruct((4, rows, cols), F32), jax.ShapeDtypeStruct((4, rows, cols), wire)],
        compiler_params=_cparams("parallel", "parallel"),
    )(c_idx, g, recv)


def _adamw(w, g, m, v):
    m = ADAM_B1 * m + (1.0 - ADAM_B1) * g
    v = ADAM_B2 * v + (1.0 - ADAM_B2) * (g * g)
    m_hat = m / (1.0 - ADAM_B1 ** ADAM_STEP)
    v_hat = v / (1.0 - ADAM_B2 ** ADAM_STEP)
    delta = -ADAM_LR * (m_hat / (jnp.sqrt(v_hat) + ADAM_EPS) + ADAM_WD * w)
    return delta, m, v


def adamw_sharded(name, parts, recv, q_idx, w, m, v):
    rows, cols = w.shape
    tr = _row_tile(rows)

    def body(q_ref, p_ref, r_ref, w_ref, m_ref, v_ref, g_out, d_out, m_out, v_out):
        g = p_ref[0] + r_ref[0].astype(F32) + r_ref[1].astype(F32) + r_ref[2].astype(F32)
        d, m_new, v_new = _adamw(w_ref[...], g, m_ref[...], v_ref[...])
        g_out[...], d_out[...], m_out[...], v_out[...] = g, d, m_new, v_new

    blk = pl.BlockSpec((tr, cols), lambda i, q_ref: (i, 0))
    return pl.pallas_call(
        body,
        name=name,
        grid_spec=pltpu.PrefetchScalarGridSpec(
            num_scalar_prefetch=1,
            grid=(rows // tr,),
            in_specs=[pl.BlockSpec((1, tr, cols), lambda i, q_ref: (q_ref[0], i, 0)),
                      pl.BlockSpec((3, tr, cols), lambda i, q_ref: (0, i, 0)), blk, blk, blk],
            out_specs=[blk] * 4,
        ),
        out_shape=[jax.ShapeDtypeStruct((rows, cols), F32)] * 4,
        compiler_params=_cparams("parallel"),
    )(q_idx, parts, recv, w, m, v)


def adamw_small(gathered, w, m, v):
    def body(g_ref, w_ref, m_ref, v_ref, g_out, d_out, m_out, v_out):
        g = g_ref[0]
        for j in range(1, N_DEV):
            g = g + g_ref[j]
        d, m_new, v_new = _adamw(w_ref[...], g, m_ref[...], v_ref[...])
        g_out[...], d_out[...], m_out[...], v_out[...] = g, d, m_new, v_new

    return pl.pallas_call(body, name="adamw_small", out_shape=[jax.ShapeDtypeStruct(w.shape, F32)] * 4)(gathered, w, m, v)


SHARDED = ("w_in", "w_uq", "w_ukv", "gdn_conv_w", "w_out")
SMALL = (("norm_gain", D_MODEL), ("mla_q_a_gain", Q_LORA), ("mla_kv_a_gain", KV_LORA), ("mla_q_norm_gain", QK),
         ("mla_k_norm_gain", QK), ("gdn_a_log", HEADS), ("gdn_dt_bias", HEADS), ("gdn_out_norm_gain", GDN_DIM))
WEIGHT_ORDER = ("norm_gain", "w_in", "mla_q_a_gain", "mla_kv_a_gain", "w_uq", "w_ukv", "mla_q_norm_gain",
                "mla_k_norm_gain", "gdn_conv_w", "gdn_a_log", "gdn_dt_bias", "gdn_out_norm_gain", "w_out")


def _pack_small(d):
    rows = []
    for name, n in SMALL:
        a = d[name].reshape(-1).astype(F32)
        n_pad = -(-n // LANE) * LANE
        rows.append(jnp.pad(a, (0, n_pad - n)).reshape(n_pad // LANE, LANE))
    packed = jnp.concatenate(rows, axis=0)
    return jnp.pad(packed, ((0, -packed.shape[0] % SUBLANE), (0, 0)))


def _unpack_small(packed):
    out, row = {}, 0
    for name, n in SMALL:
        n_rows = -(-n // LANE)
        out[name] = packed[row:row + n_rows].reshape(-1)[:n].reshape(1, n)
        row += n_rows
    return out


def kernel(x, positions, norm_gain, w_in, mla_q_a_gain, mla_kv_a_gain, w_uq, w_ukv, mla_q_norm_gain, mla_k_norm_gain, gdn_conv_w, gdn_a_log, gdn_dt_bias, gdn_out_norm_gain, w_out, loss_target, m_norm_gain, m_w_in, m_mla_q_a_gain, m_mla_kv_a_gain, m_w_uq, m_w_ukv, m_mla_q_norm_gain, m_mla_k_norm_gain, m_gdn_conv_w, m_gdn_a_log, m_gdn_dt_bias, m_gdn_out_norm_gain, m_w_out, v_norm_gain, v_w_in, v_mla_q_a_gain, v_mla_kv_a_gain, v_w_uq, v_w_ukv, v_mla_q_norm_gain, v_mla_k_norm_gain, v_gdn_conv_w, v_gdn_a_log, v_gdn_dt_bias, v_gdn_out_norm_gain, v_w_out):
    w = dict(norm_gain=norm_gain, w_in=w_in, mla_q_a_gain=mla_q_a_gain, mla_kv_a_gain=mla_kv_a_gain, w_uq=w_uq,
             w_ukv=w_ukv, mla_q_norm_gain=mla_q_norm_gain, mla_k_norm_gain=mla_k_norm_gain, gdn_conv_w=gdn_conv_w,
             gdn_a_log=gdn_a_log, gdn_dt_bias=gdn_dt_bias, gdn_out_norm_gain=gdn_out_norm_gain, w_out=w_out)
    m = dict(norm_gain=m_norm_gain, w_in=m_w_in, mla_q_a_gain=m_mla_q_a_gain, mla_kv_a_gain=m_mla_kv_a_gain,
             w_uq=m_w_uq, w_ukv=m_w_ukv, mla_q_norm_gain=m_mla_q_norm_gain, mla_k_norm_gain=m_mla_k_norm_gain,
             gdn_conv_w=m_gdn_conv_w, gdn_a_log=m_gdn_a_log, gdn_dt_bias=m_gdn_dt_bias,
             gdn_out_norm_gain=m_gdn_out_norm_gain, w_out=m_w_out)
    v = dict(norm_gain=v_norm_gain, w_in=v_w_in, mla_q_a_gain=v_mla_q_a_gain, mla_kv_a_gain=v_mla_kv_a_gain,
             w_uq=v_w_uq, w_ukv=v_w_ukv, mla_q_norm_gain=v_mla_q_norm_gain, mla_k_norm_gain=v_mla_k_norm_gain,
             gdn_conv_w=v_gdn_conv_w, gdn_a_log=v_gdn_a_log, gdn_dt_bias=v_gdn_dt_bias,
             gdn_out_norm_gain=v_gdn_out_norm_gain, w_out=v_w_out)
    t_len = x.shape[1]

    shards = [w[n][0] if n == "gdn_conv_w" else w[n][0].astype(_BF) for n in SHARDED]
    xi, yi, ci = _place()
    c_idx = jnp.reshape(ci, (1,)).astype(jnp.int32)
    q_idx = jnp.reshape(2 * xi + yi, (1,)).astype(jnp.int32)
    flight = {}

    def cols_whole(g):
        return g.transpose(1, 0, 2).reshape(g.shape[1], N_DEV * g.shape[2])

    def col_blocks(g):
        return g.reshape(g.shape[0], N_DEV, g.shape[1] // N_DEV).transpose(1, 0, 2)

    (a_w_in,) = all_gather(shards[:1])
    p = {n: w[n] for n, _ in SMALL}
    p["w_in"] = arrange_w_in(cols_whole(a_w_in))
    flight["weights"], weights_token = exchange_start(
        "gather_start", gather_routes, N_DEV - 1, [s[None] for s in shards[1:]], n_slots=N_DEV)

    def late_weights(proj):
        _, landed = exchange_wait("gather_wait", gather_routes, flight["weights"], proj)
        me = 4 * xi + 2 * yi + ci
        a_w_uq, a_w_ukv, a_cw, a_w_out = [lax.dynamic_update_slice(land, s[None], (me, 0, 0))
                                          for land, s in zip(landed, shards[1:])]
        return {"w_uq": arrange_w_uq(cols_whole(a_w_uq)), "w_ukv": cols_whole(a_w_ukv),
                "gdn_conv_w": cols_whole(a_cw), "w_out": a_w_out.reshape(N_DEV * a_w_out.shape[1], a_w_out.shape[2])}

    early, parts = SHARDED[1:], {}

    def core_stage(tag, names, blocks):
        flight["cores" + tag], token = exchange_start("cores_start" + tag, core_routes, 4, blocks)
        flight["names" + tag] = names
        return token

    def chip_stage(tag, after):
        blocks, landed = exchange_wait("cores_wait" + tag, core_routes, flight["cores" + tag], after)
        for n, g, r in zip(flight["names" + tag], blocks, landed):
            parts[n] = add_core_parts("add_" + n, g, r, c_idx, F32 if n == "gdn_conv_w" else _BF)
        wires = [parts[n][1] for n in flight["names" + tag]]
        flight["chips" + tag], token = exchange_start("chips_start" + tag, chip_routes, 3, wires)
        return token

    def on_early_grads(grads):
        return core_stage("_early", early, [
            col_blocks(unarrange_w_uq(grads["w_uq"])), col_blocks(grads["w_ukv"]), col_blocks(grads["gdn_conv_w"]),
            grads["w_out"].reshape(N_DEV, D_MODEL // N_DEV, D_MODEL)])

    def on_d_proj(d_proj):
        return chip_stage("_early", d_proj)

    def on_weight_grads(grads):
        return core_stage("", SHARDED[:1], [col_blocks(unarrange_w_in(grads["w_in"]))])

    def on_d_xn(d_xn):
        return chip_stage("", d_xn)

    pos = positions.reshape(t_len, 1).astype(F32)
    loss, grad_x, grads = local_step(x.reshape(t_len, D_MODEL), pos, loss_target.reshape(t_len, D_MODEL), p,
                                     on_early_grads=on_early_grads, on_d_proj=on_d_proj,
                                     on_weight_grads=on_weight_grads, on_d_xn=on_d_xn,
                                     first_after=weights_token, late_weights=late_weights)
    loss = lax.psum(loss, ("x", "y", "c"))
    out = {}
    small_all = gather_small(_pack_small(grads))
    res = adamw_small(small_all, _pack_small(w), _pack_small(m), _pack_small(v))
    unpacked = [_unpack_small(a) for a in res]
    for n, _ in SMALL:
        out[n] = [u[n] for u in unpacked]

    _, from_chips_early = exchange_wait("chips_wait_early", chip_routes, flight["chips_early"], res[0])
    _, from_chips = exchange_wait("chips_wait", chip_routes, flight["chips"], res[0])
    for n, rcv in zip(SHARDED, from_chips + from_chips_early):
        prt = parts[n][0]
        shape = w[n].shape
        res = adamw_sharded("adamw_" + n, prt, rcv, q_idx, w[n].reshape(shape[-2:]), m[n].reshape(shape[-2:]),
                            v[n].reshape(shape[-2:]))
        out[n] = [a.reshape(shape) for a in res]

    return (loss, grad_x.reshape(x.shape), *[out[n][0] for n in WEIGHT_ORDER], *[out[n][1] for n in WEIGHT_ORDER],
            *[out[n][2] for n in WEIGHT_ORDER], *[out[n][3] for n in WEIGHT_ORDER])
```

```python
import functools

import jax
import jax.numpy as jnp
from jax import lax
from jax.experimental import pallas as pl
from jax.experimental.pallas import tpu as pltpu

F32 = jnp.float32
_BF = jnp.bfloat16

D_MODEL = 2048
HEADS = 8
NOPE = 128
ROPE = 64
QK = NOPE + ROPE
Q_LORA = 512
KV_LORA = 256
HEAD_PAD = 256
GDN_DIM = 128
WIDTH = HEADS * 128
CONV_W = 4
CHUNK = 64
ROPE_THETA = 10000.0
EPS = 1e-6
N_DEV = 8
LANE = 128
SUBLANE = 8
VMEM_LIMIT = 48 * 1024 * 1024

ADAM_LR, ADAM_B1, ADAM_B2, ADAM_EPS, ADAM_WD, ADAM_STEP = 0.001, 0.9, 0.999, 1e-08, 0.01, 10

P_MGATE, P_GQ, P_GK, P_GV, P_GGATE = 0, 1024, 2048, 3072, 4096
P_CQ, P_CKV, P_KR, P_GAB = 5120, 5632, 5888, 6016
R_SPLITS = (512, 256, 64, 1024, 1024, 1024, 1024, 8, 8, 1024)


def _cparams(*sem):
    return pltpu.CompilerParams(dimension_semantics=sem, vmem_limit_bytes=VMEM_LIMIT)


def _d_nn(a, b):
    return jnp.dot(a.astype(_BF), b.astype(_BF), preferred_element_type=F32)


def _d_nt(a, b):
    return lax.dot_general(a.astype(_BF), b.astype(_BF), (((1,), (1,)), ((), ())), preferred_element_type=F32)


def _d_tn(a, b):
    return lax.dot_general(a.astype(_BF), b.astype(_BF), (((0,), (0,)), ((), ())), preferred_element_type=F32)


_NN3 = (((2,), (1,)), ((0,), (0,)))
_NT3 = (((2,), (2,)), ((0,), (0,)))
_TN3 = (((1,), (1,)), ((0,), (0,)))


def _bdot(a, b, dims, hi):
    if hi:
        return lax.dot_general(a, b, dims, preferred_element_type=F32, precision=hi)
    return lax.dot_general(a.astype(_BF), b.astype(_BF), dims, preferred_element_type=F32)


def _batched_matmuls(hi):
    nn = jax.custom_vjp(lambda a, b: _bdot(a, b, _NN3, hi))
    nt = jax.custom_vjp(lambda a, b: _bdot(a, b, _NT3, hi))
    tn = jax.custom_vjp(lambda a, b: _bdot(a, b, _TN3, hi))
    nn.defvjp(lambda a, b: (_bdot(a, b, _NN3, hi), (a, b)),
              lambda r, g: (_bdot(g, r[1], _NT3, hi), _bdot(r[0], g, _TN3, hi)))
    nt.defvjp(lambda a, b: (_bdot(a, b, _NT3, hi), (a, b)),
              lambda r, g: (_bdot(g, r[1], _NN3, hi), _bdot(g, r[0], _TN3, hi)))
    tn.defvjp(lambda a, b: (_bdot(a, b, _TN3, hi), (a, b)),
              lambda r, g: (_bdot(r[1], g, _NT3, hi), _bdot(r[0], g, _NN3, hi)))
    return nn, nt, tn


_bmm, _bmm_nt, _bmm_tn = _batched_matmuls(False)


def _split2(x):
    hi = x.astype(_BF)
    return hi, (x - hi.astype(F32)).astype(_BF)


def _pdot(a, b, mode):
    (a_hi, a_lo), (b_hi, b_lo) = _split2(a), _split2(b)
    a_ax, b_ax, dims = {"nn": (2, 1, _NN3), "nt": (2, 2, _NT3), "tn": (1, 1, _TN3)}[mode]
    lhs = jnp.concatenate([a_hi, a_lo, a_hi], axis=a_ax)
    rhs = jnp.concatenate([b_hi, b_hi, b_lo], axis=b_ax)
    return lax.dot_general(lhs, rhs, dims, preferred_element_type=F32)


def _packed_matmuls():
    nn = jax.custom_vjp(lambda a, b: _pdot(a, b, "nn"))
    nn.defvjp(lambda a, b: (_pdot(a, b, "nn"), (a, b)), lambda r, g: (_pdot(g, r[1], "nt"), _pdot(r[0], g, "tn")))
    return nn


_bh3 = _packed_matmuls()
_bh3_passes = _batched_matmuls(lax.Precision.HIGH)[0]


@functools.partial(jax.custom_vjp, nondiff_argnums=(1, 2))
def _roll(x, shift, axis):
    return pltpu.roll(x, shift, axis)


def _roll_fwd(x, shift, axis):
    return pltpu.roll(x, shift, axis), None


def _roll_bwd(shift, axis, _, g):
    n = g.shape[axis]
    return (pltpu.roll(g, (n - shift) % n, axis),)


_roll.defvjp(_roll_fwd, _roll_bwd)


def _rms(x, gain):
    return x * lax.rsqrt(jnp.mean(x * x, axis=-1, keepdims=True) + EPS) * gain


MM_TILE = 1024
MM_DEPTH = 2048


def matmul(name, a, b, mode, after=None):
    if mode == "nn":
        (m, k), (k2, n) = a.shape, b.shape
    elif mode == "nt":
        (m, k), (n, k2) = a.shape, b.shape
    else:
        (k, m), (k2, n) = a.shape, b.shape
    assert k == k2, (name, a.shape, b.shape)
    tm, tn, tk = min(MM_TILE, m), min(MM_TILE, n), min(MM_DEPTH, k)
    assert m % tm == 0 and n % tn == 0 and k % tk == 0, (name, m, n, k)
    dot = {"nn": _d_nn, "nt": _d_nt, "tn": _d_tn}[mode]

    def body(a_ref, b_ref, *rest):
        o_ref = rest[-1]
        kk = pl.program_id(2)
        part = dot(a_ref[...], b_ref[...])

        @pl.when(kk == 0)
        def _():
            o_ref[...] = part

        @pl.when(kk != 0)
        def _():
            o_ref[...] += part

    if mode == "nn":
        a_spec = pl.BlockSpec((tm, tk), lambda j, i, kk: (i, kk))
        b_spec = pl.BlockSpec((tk, tn), lambda j, i, kk: (kk, j))
    elif mode == "nt":
        a_spec = pl.BlockSpec((tm, tk), lambda j, i, kk: (i, kk))
        b_spec = pl.BlockSpec((tn, tk), lambda j, i, kk: (j, kk))
    else:
        a_spec = pl.BlockSpec((tk, tm), lambda j, i, kk: (kk, i))
        b_spec = pl.BlockSpec((tk, tn), lambda j, i, kk: (kk, j))
    return pl.pallas_call(
        body,
        name=name,
        grid=(n // tn, m // tm, k // tk),
        in_specs=[a_spec, b_spec] + ([] if after is None else [pl.BlockSpec(memory_space=pl.ANY)]),
        out_specs=pl.BlockSpec((tm, tn), lambda j, i, kk: (i, j)),
        out_shape=jax.ShapeDtypeStruct((m, n), F32),
        compiler_params=_cparams("parallel", "parallel", "arbitrary"),
    )(*((a, b) if after is None else (a, b, after)))


def matmul_pieces(name, pieces, b, after=None):
    n_p, (t_len, width), n = len(pieces), pieces[0].shape, b.shape[0]
    assert width == MM_TILE and all(p.shape == pieces[0].shape for p in pieces)
    tile, tn = min(MM_TILE, t_len), min(MM_TILE, n)
    extra = [] if after is None else [after]

    def body(*refs):
        b_ref, o_ref, kk = refs[n_p], refs[-1], pl.program_id(2)
        for p in range(n_p):
            @pl.when(kk == p)
            def _(p=p):
                part = _d_nt(refs[p][...], b_ref[...])
                if p == 0:
                    o_ref[...] = part
                else:
                    o_ref[...] += part

    return pl.pallas_call(
        body,
        name=name,
        grid=(n // tn, t_len // tile, n_p),
        in_specs=[pl.BlockSpec((tile, width), lambda j, i, kk: (i, 0))] * n_p
        + [pl.BlockSpec((tn, width), lambda j, i, kk: (j, kk))] + [pl.BlockSpec(memory_space=pl.ANY)] * len(extra),
        out_specs=pl.BlockSpec((tile, tn), lambda j, i, kk: (i, j)),
        out_shape=jax.ShapeDtypeStruct((t_len, n), F32),
        compiler_params=_cparams("parallel", "parallel", "arbitrary"),
    )(*pieces, b, *extra)


def out_proj_loss(mixed, w_out, x, tgt):
    (m, k), n = mixed.shape, w_out.shape[1]
    tm, tn = min(MM_TILE // 2, m), min(MM_TILE, n)
    assert m % tm == 0 and n % tn == 0

    def body(a_ref, b_ref, x_ref, t_ref, dy_ref, dy_mx_ref, loss_ref):
        first = (pl.program_id(0) == 0) & (pl.program_id(1) == 0)
        e = x_ref[...] + _d_nn(a_ref[...], b_ref[...]) - t_ref[...]
        dy = e * (1.0 / D_MODEL)
        dy_ref[...] = dy
        dy_mx_ref[...] = dy.astype(dy_mx_ref.dtype)
        part = jnp.zeros((SUBLANE, LANE), F32) + 0.5 * jnp.sum(e * e) * (1.0 / D_MODEL)

        @pl.when(first)
        def _():
            loss_ref[...] = part

        @pl.when(jnp.logical_not(first))
        def _():
            loss_ref[...] += part

    tile = pl.BlockSpec((tm, tn), lambda j, i: (i, j))
    return pl.pallas_call(
        body,
        name="out_proj_loss",
        grid=(n // tn, m // tm),
        in_specs=[pl.BlockSpec((tm, k), lambda j, i: (i, 0)), pl.BlockSpec((k, tn), lambda j, i: (0, j)), tile, tile],
        out_specs=[tile, tile, pl.BlockSpec((SUBLANE, LANE), lambda j, i: (0, 0))],
        out_shape=[jax.ShapeDtypeStruct((m, n), F32), jax.ShapeDtypeStruct((m, n), _BF),
                   jax.ShapeDtypeStruct((SUBLANE, LANE), F32)],
        compiler_params=_cparams("arbitrary", "arbitrary"),
    )(mixed, w_out, x, tgt)


def rowwise(name, fn, t_len, tile, row_in, full_in, row_out, acc_out=(), carries=(), reverse=False):
    tile = min(tile, t_len)
    n = t_len // tile
    assert t_len % tile == 0 and tile % SUBLANE == 0
    n_in, n_ro, n_acc, n_car = len(row_in) + len(full_in), len(row_out), len(acc_out), len(carries)

    def ti(i):
        return (n - 1 - i) if reverse else i

    in_specs, args = [], []
    for arr, kind in row_in:
        if kind[0] == "r":
            in_specs.append(pl.BlockSpec((tile, kind[1]), lambda i, c=kind[2]: (ti(i), c)))
        elif kind[0] == "h":
            in_specs.append(pl.BlockSpec((arr.shape[0], tile, arr.shape[2]), lambda i: (0, ti(i), 0)))
        else:
            in_specs.append(pl.BlockSpec(
                (SUBLANE, kind[1]), lambda i, c=kind[2]: (jnp.maximum(ti(i) * (tile // SUBLANE) - 1, 0), c)))
        args.append(arr)
    for arr in full_in:
        in_specs.append(pl.BlockSpec(arr.shape, lambda i, nd=arr.ndim: (0,) * nd))
        args.append(arr)
    out_specs, out_shape = [], []
    for kind in row_out:
        if kind[0] == "r":
            out_specs.append(pl.BlockSpec((tile, kind[1]), lambda i: (ti(i), 0)))
            out_shape.append(jax.ShapeDtypeStruct((t_len, kind[1]), kind[2]))
        else:
            out_specs.append(pl.BlockSpec((kind[1], tile, kind[2]), lambda i: (0, ti(i), 0)))
            out_shape.append(jax.ShapeDtypeStruct((kind[1], t_len, kind[2]), kind[3]))
    for shp in acc_out:
        out_specs.append(pl.BlockSpec(shp, lambda i, nd=len(shp): (0,) * nd))
        out_shape.append(jax.ShapeDtypeStruct(shp, F32))

    def body(*refs):
        in_refs = refs[:n_in]
        ro_refs = refs[n_in:n_in + n_ro]
        acc_refs = refs[n_in + n_ro:n_in + n_ro + n_acc]
        car_refs = refs[n_in + n_ro + n_acc:]
        step = pl.program_id(0)
        if n_car:
            @pl.when(step == 0)
            def _():
                for r in car_refs:
                    r[...] = jnp.zeros_like(r)
        vals = [r[...].astype(F32) for r in in_refs] + [r[...] for r in car_refs]
        outs = fn(ti(step), *vals)
        assert len(outs) == n_ro + n_acc + n_car, (name, len(outs))
        for r, o in zip(ro_refs, outs[:n_ro]):
            r[...] = o.astype(r.dtype)
        for r, o in zip(acc_refs, outs[n_ro:n_ro + n_acc]):
            @pl.when(step == 0)
            def _(r=r, o=o):
                r[...] = o

            @pl.when(step != 0)
            def _(r=r, o=o):
                r[...] += o
        for r, o in zip(car_refs, outs[n_ro + n_acc:]):
            r[...] = o

    res = pl.pallas_call(
        body,
        name=name,
        grid=(n,),
        in_specs=in_specs,
        out_specs=out_specs,
        out_shape=out_shape,
        scratch_shapes=[pltpu.VMEM(s, F32) for s in carries],
        compiler_params=_cparams("arbitrary"),
    )(*args)
    return list(res)


def _vjp_fn(fn, n_diff, n_out):
    def g(i, *a):
        ins, cts = a[:len(a) - n_out], a[len(a) - n_out:]
        diff, rest = ins[:n_diff], ins[n_diff:]
        _, pull = jax.vjp(lambda *d: tuple(fn(i, *d, *rest)), *diff)
        return tuple(pull(tuple(cts)))

    return g


def f_rms_x(i, x, gain):
    return (_rms(x, gain),)


def f_lat(i, cq, ckv, gq, gkv):
    return _rms(cq, gq), _rms(ckv, gkv)


def _rope_tables(pos, invf):
    ang = pos * invf
    lane = lax.broadcasted_iota(jnp.int32, (1, LANE), 1)
    cosv, sinv = jnp.cos(ang), jnp.sin(ang)
    half = ROPE // 2
    c = jnp.where(lane < ROPE, cosv, 0.0)
    sa = jnp.where(lane < half, -sinv, 0.0)
    sb = jnp.where((lane >= half) & (lane < ROPE), sinv, 0.0)
    return c, sa, sb


def _rope(xh, tabs):
    c, sa, sb = tabs
    half = ROPE // 2
    return xh * c + _roll(xh, LANE - half, 1) * sa + _roll(xh, half, 1) * sb


def f_head(i, q_raw, kv_raw, kr, qg, kg, pos, invf):
    tabs = _rope_tables(pos, invf)
    qs, ks, vs = [], [], []
    kr_ss = jnp.sum(kr * kr, axis=-1, keepdims=True)
    kr_rot = _rope(kr * kg[:, NOPE:], tabs)
    for h in range(HEADS):
        lo = q_raw[:, HEAD_PAD * h:HEAD_PAD * h + NOPE]
        hi = q_raw[:, HEAD_PAD * h + NOPE:HEAD_PAD * (h + 1)]
        ss = jnp.sum(lo * lo, axis=-1, keepdims=True) + jnp.sum(hi * hi, axis=-1, keepdims=True)
        r = lax.rsqrt(ss * (1.0 / QK) + EPS)
        qs.append(jnp.concatenate([lo * r * qg[:, :NOPE], _rope(hi * r * qg[:, NOPE:], tabs)], axis=1))
        lo = kv_raw[:, 2 * NOPE * h:2 * NOPE * h + NOPE]
        ss = jnp.sum(lo * lo, axis=-1, keepdims=True) + kr_ss
        r = lax.rsqrt(ss * (1.0 / QK) + EPS)
        ks.append(jnp.concatenate([lo * r * kg[:, :NOPE], kr_rot * r], axis=1))
        vs.append(kv_raw[:, 2 * NOPE * h + NOPE:2 * NOPE * (h + 1)])
    return jnp.stack(qs), jnp.stack(ks), jnp.stack(vs)


def f_mix(i, o_mla, mgate, o_gdn, ggate, og):
    parts = [o_mla * jax.nn.silu(mgate)]
    for h in range(HEADS):
        parts.append(_rms(o_gdn[h], og) * jax.nn.silu(ggate[:, LANE * h:LANE * (h + 1)]))
    return (jnp.concatenate(parts, axis=1),)


def _row(a, j):
    rows = lax.broadcasted_iota(jnp.int32, a.shape, 0)
    return jnp.sum(jnp.where(rows == j, a, 0.0), axis=0, keepdims=True)


def _shift_rows(x, halo, d):
    xs = _roll(x, d, 0)
    hs = _roll(halo, d, 0)
    r8 = lax.broadcasted_iota(jnp.int32, hs.shape, 0)
    top = jnp.where(r8 < d, hs, xs[:SUBLANE])
    return jnp.concatenate([top, xs[SUBLANE:]], axis=0)


def _conv_silu(x, halo, w):
    y = _row(w, CONV_W - 1) * x
    for j in range(CONV_W - 1):
        y = y + _row(w, j) * _shift_rows(x, halo, CONV_W - 1 - j)
    return jax.nn.silu(y)


def _head_select(offset):
    r = lax.broadcasted_iota(jnp.int32, (LANE, WIDTH), 0)
    c = lax.broadcasted_iota(jnp.int32, (LANE, WIDTH), 1)
    return (r == offset + lax.shift_right_logical(c, 7)).astype(_BF)


def _split3(x):
    x1 = x.astype(_BF)
    r1 = x - x1.astype(F32)
    x2 = r1.astype(_BF)
    return x1, x2, (r1 - x2.astype(F32)).astype(_BF)


@jax.custom_vjp
def _spread(x, sel):
    return _d_nn(jnp.concatenate(_split3(x), axis=1), jnp.concatenate([sel, sel, sel], axis=0))


def _spread_fwd(x, sel):
    return _spread(x, sel), sel


def _spread_bwd(sel, g):
    g1, g2, g3 = _split3(g)
    return _d_nt(g1, sel) + _d_nt(g2, sel) + _d_nt(g3, sel), jnp.zeros_like(sel)


_spread.defvjp(_spread_fwd, _spread_bwd)


def f_gdn_pre(i, gq, gk, gv, gab, hq, hk, hv, cwq, cwk, cwv, alog, dtb):
    live = jnp.where(i == 0, 0.0, 1.0)
    q = _conv_silu(gq, hq * live, cwq)
    k = _conv_silu(gk, hk * live, cwk)
    v = _conv_silu(gv, hv * live, cwv)
    g = _spread(-jnp.exp(alog) * jax.nn.softplus(gab + dtb), _head_select(0))
    beta = _spread(jax.nn.sigmoid(gab), _head_select(HEADS))
    qs, ks, vs, gs, bs = [], [], [], [], []
    for h in range(HEADS):
        sl = slice(LANE * h, LANE * (h + 1))
        qh, kh = q[:, sl], k[:, sl]
        qs.append(qh * lax.rsqrt(jnp.sum(qh * qh, axis=-1, keepdims=True) + EPS) * (GDN_DIM ** -0.5))
        ks.append(kh * lax.rsqrt(jnp.sum(kh * kh, axis=-1, keepdims=True) + EPS))
        vs.append(v[:, sl])
        gs.append(g[:, sl])
        bs.append(beta[:, sl])
    return jnp.stack(qs), jnp.stack(ks), jnp.stack(vs), jnp.stack(gs), jnp.stack(bs)


def gdn_pre_bwd(i, gq, gk, gv, gab, hq, hk, hv, dq, dk, dv, dg, db, cwq, cwk, cwv, alog, dtb, cq, ck, cv):
    grads = _vjp_fn(f_gdn_pre, 12, 5)(i, gq, gk, gv, gab, hq, hk, hv, cwq, cwk, cwv, alog, dtb, dq, dk, dv, dg, db)
    dgq, dgk, dgv, dgab, dhq, dhk, dhv, dcwq, dcwk, dcwv, dalog, ddtb = grads

    def add_tail(dx, carry):
        return jnp.concatenate([dx[:-SUBLANE], dx[-SUBLANE:] + carry], axis=0)

    return (add_tail(dgq, cq), add_tail(dgk, ck), add_tail(dgv, cv), dgab,
            dcwq, dcwk, dcwv, dalog, ddtb, dhq, dhk, dhv)


def _flash_tile(t_len):
    return min(512, t_len)


FLASH_HEADS = 4
FLASH_BWD_HEADS = 2
LOG2E = 1.4426950408889634


def _causal(rows0, shape):
    r = rows0 + lax.broadcasted_iota(jnp.int32, shape, 0)
    c = lax.broadcasted_iota(jnp.int32, shape, 1)
    return c <= r


def flash_fwd(q, k, v):
    h_n, t_len, _ = q.shape
    tq = _flash_tile(t_len)
    nq = t_len // tq
    hb = FLASH_HEADS
    kw = 2 if nq % 2 == 0 else 1
    tk = kw * tq
    c2 = (QK ** -0.5) * LOG2E
    pairs = [(i, j) for i in range(nq) for j in range(i // kw + 1)]
    qt = jnp.array([p[0] for p in pairs], jnp.int32)
    kt = jnp.array([p[1] for p in pairs], jnp.int32)

    def body(qt_ref, kt_ref, q_ref, k_ref, v_ref, o_ref, lse_ref, m_s, acc_s):
        step = pl.program_id(1)
        qi, kj = qt_ref[step], kt_ref[step]
        last = qi // kw

        @pl.when(kj == 0)
        def _():
            m_s[...] = jnp.full_like(m_s, -jnp.inf)
            acc_s[...] = jnp.zeros_like(acc_s)

        def tile(diagonal):
            s = _bdot(q_ref[...], k_ref[...], _NT3, False) * c2
            if diagonal:
                s = jnp.where(_causal((qi % kw) * tq, (tq, tk))[None], s, -jnp.inf)
            m_old = m_s[...]
            m_new = jnp.maximum(m_old, jnp.max(s, axis=-1, keepdims=True))
            p = jnp.exp2(s - m_new).astype(_BF)
            v_ones = jnp.concatenate([v_ref[...], jnp.ones((hb, tk, LANE), _BF)], axis=2)
            acc_s[...] = jnp.exp2(m_old - m_new) * acc_s[...] + _bdot(p, v_ones, _NN3, False)
            m_s[...] = m_new

        @pl.when(kj < last)
        def _():
            tile(False)

        @pl.when(kj == last)
        def _():
            tile(True)
            acc = acc_s[...]
            l_sum = acc[:, :, LANE:]
            o = acc[:, :, :LANE] / l_sum
            for hh in range(hb):
                o_ref[:, LANE * hh:LANE * (hh + 1)] = o[hh]
            lse_ref[...] = m_s[...] + jnp.log2(jnp.max(l_sum, axis=-1, keepdims=True))

    return pl.pallas_call(
        body,
        name="flash_fwd",
        grid_spec=pltpu.PrefetchScalarGridSpec(
            num_scalar_prefetch=2,
            grid=(h_n // hb, qt.shape[0]),
            in_specs=[
                pl.BlockSpec((hb, tq, HEAD_PAD), lambda h, s, qt_ref, kt_ref: (h, qt_ref[s], 0)),
                pl.BlockSpec((hb, tk, HEAD_PAD), lambda h, s, qt_ref, kt_ref: (h, kt_ref[s], 0)),
                pl.BlockSpec((hb, tk, LANE), lambda h, s, qt_ref, kt_ref: (h, kt_ref[s], 0)),
            ],
            out_specs=[
                pl.BlockSpec((tq, hb * LANE), lambda h, s, qt_ref, kt_ref: (qt_ref[s], h)),
                pl.BlockSpec((hb, tq, 1), lambda h, s, qt_ref, kt_ref: (h, qt_ref[s], 0)),
            ],
            scratch_shapes=[pltpu.VMEM((hb, tq, 1), F32), pltpu.VMEM((hb, tq, 2 * LANE), F32)],
        ),
        out_shape=[jax.ShapeDtypeStruct((t_len, h_n * LANE), F32), jax.ShapeDtypeStruct((h_n, t_len, 1), F32)],
        compiler_params=_cparams("parallel", "arbitrary"),
    )(qt, kt, q, k, v)


def flash_bwd(q, k, v, do, lse, delta):
    h_n, t_len, _ = q.shape
    tq = _flash_tile(t_len)
    nq = t_len // tq
    hb = FLASH_BWD_HEADS
    kw = 2 if nq % 2 == 0 else 1
    tk = kw * tq
    pairs = [(i, j) for j in range(nq // kw) for i in range(kw * j, nq)]
    n_steps = len(pairs)
    qt = jnp.array([p[0] for p in pairs], jnp.int32)
    kt = jnp.array([p[1] for p in pairs], jnp.int32)
    scale = QK ** -0.5
    c2 = scale * LOG2E

    def body(qt_ref, kt_ref, q_ref, k_ref, v_ref, do_ref, lse_ref, dl_ref, dq_hbm, dk_ref, dv_ref, dq_s, dq_sem):
        group, step = pl.program_id(0), pl.program_id(1)
        qi, kj = qt_ref[step], kt_ref[step]

        @pl.when(step == 0)
        def _():
            dq_s[...] = jnp.zeros_like(dq_s)

        def tile(diagonal):
            qb, kb = q_ref[...], k_ref[...]
            dob = jnp.stack([do_ref[:, LANE * hh:LANE * (hh + 1)] for hh in range(hb)])
            p = jnp.exp2(_bdot(qb, kb, _NT3, False) * c2 - lse_ref[...])
            if diagonal:
                p = jnp.where(_causal((qi % kw) * tq, (tq, tk))[None], p, 0.0)
            dv = _bdot(p, dob, _TN3, False)
            ds = p * (_bdot(dob, v_ref[...], _NT3, False) - dl_ref[...]) * scale
            dk = _bdot(ds, qb, _TN3, False)
            dq_s[:, pl.ds(pl.multiple_of(qi * tq, tq), tq), :] += _bdot(ds, kb, _NN3, False)
            return dk, dv

        @pl.when(qi == kw * kj)
        def _():
            dk_ref[...], dv_ref[...] = tile(True)

        @pl.when((qi != kw * kj) & (qi // kw == kj))
        def _():
            dk, dv = tile(True)
            dk_ref[...] += dk
            dv_ref[...] += dv

        @pl.when(qi // kw > kj)
        def _():
            dk, dv = tile(False)
            dk_ref[...] += dk
            dv_ref[...] += dv

        @pl.when(step == n_steps - 1)
        def _():
            out = pltpu.make_async_copy(dq_s, dq_hbm.at[pl.ds(group * hb, hb)], dq_sem)
            out.start()
            out.wait()

    def qmap(h, s, qt_ref, kt_ref):
        return (h, qt_ref[s], 0)

    def kmap(h, s, qt_ref, kt_ref):
        return (h, kt_ref[s], 0)

    return pl.pallas_call(
        body,
        name="flash_bwd",
        grid_spec=pltpu.PrefetchScalarGridSpec(
            num_scalar_prefetch=2,
            grid=(h_n // hb, n_steps),
            in_specs=[
                pl.BlockSpec((hb, tq, HEAD_PAD), qmap),
                pl.BlockSpec((hb, tk, HEAD_PAD), kmap),
                pl.BlockSpec((hb, tk, LANE), kmap),
                pl.BlockSpec((tq, hb * LANE), lambda h, s, qt_ref, kt_ref: (qt_ref[s], h)),
                pl.BlockSpec((hb, tq, 1), qmap),
                pl.BlockSpec((hb, tq, 1), qmap),
            ],
            out_specs=[
                pl.BlockSpec(memory_space=pl.ANY),
                pl.BlockSpec((hb, tk, HEAD_PAD), kmap),
                pl.BlockSpec((hb, tk, LANE), kmap),
            ],
            scratch_shapes=[pltpu.VMEM((hb, t_len, HEAD_PAD), F32), pltpu.SemaphoreType.DMA],
        ),
        out_shape=[
            jax.ShapeDtypeStruct((h_n, t_len, HEAD_PAD), F32),
            jax.ShapeDtypeStruct((h_n, t_len, HEAD_PAD), F32),
            jax.ShapeDtypeStruct((h_n, t_len, LANE), F32),
        ],
        compiler_params=_cparams("parallel", "arbitrary"),
    )(qt, kt, q, k, v, do, lse, delta)


def _tri_ones(h_n):
    ii = lax.broadcasted_iota(jnp.int32, (h_n, CHUNK, CHUNK), 1)
    jj = lax.broadcasted_iota(jnp.int32, (h_n, CHUNK, CHUNK), 2)
    return (ii >= jj).astype(_BF)


@jax.custom_vjp
def _chunk_cumsum(gb):
    tri = _tri_ones(gb.shape[0])
    return _bdot(jnp.concatenate([tri, tri, tri], axis=2), jnp.concatenate(_split3(gb), axis=1), _NN3, False)


def _chunk_cumsum_bwd(_, ct):
    tri = _tri_ones(ct.shape[0])
    return (_bdot(jnp.concatenate([tri, tri, tri], axis=1), jnp.concatenate(_split3(ct), axis=1), _TN3, False),)


_chunk_cumsum.defvjp(lambda gb: (_chunk_cumsum(gb), None), _chunk_cumsum_bwd)


@jax.custom_vjp
def _pair_diff(gcb):
    g1, g2, g3 = _split3(gcb)
    lane = lax.broadcasted_iota(jnp.int32, (1, 1, LANE), 2)
    one, zero = jnp.ones((), _BF), jnp.zeros((), _BF)
    a = jnp.where(lane == 0, g1, jnp.where(lane == 1, g2, jnp.where(lane == 2, g3, jnp.where(lane < 6, one, zero))))
    b = jnp.where(lane < 3, one, jnp.where(lane == 3, -g1, jnp.where(lane == 4, -g2, jnp.where(lane == 5, -g3, zero))))
    return _bdot(a, b, _NT3, False)


def _pair_diff_bwd(_, ct):
    parts = _split3(ct)
    ones = jnp.ones((ct.shape[0], 3 * CHUNK, LANE), _BF)
    rows = _bdot(jnp.concatenate(parts, axis=2), ones, _NN3, False)
    cols = _bdot(jnp.concatenate(parts, axis=1), ones, _TN3, False)
    lane = lax.broadcasted_iota(jnp.int32, (1, 1, LANE), 2)
    return (jnp.where(lane == 0, rows - cols, 0.0),)


_pair_diff.defvjp(lambda gcb: (_pair_diff(gcb), None), _pair_diff_bwd)


@jax.custom_vjp
def _saved_inverse(lmat, inv):
    return inv


def _saved_inverse_bwd(inv, g):
    return -_pdot(_pdot(inv, g, "tn"), inv, "nt"), jnp.zeros_like(inv)


_saved_inverse.defvjp(lambda lmat, inv: (inv, inv), _saved_inverse_bwd)


def gdn_step(s, q, k, v, gb, bb, inv_saved=None):
    c = CHUNK
    ii = lax.broadcasted_iota(jnp.int32, (1, c, c), 1)
    jj = lax.broadcasted_iota(jnp.int32, (1, c, c), 2)
    incl, strict = ii >= jj, ii > jj
    gcb = _chunk_cumsum(gb)
    diff = _pair_diff(gcb)
    decay = jnp.where(incl, jnp.exp(jnp.where(incl, diff, 0.0)), 0.0)
    kb, vb = k * bb, v * bb
    egc = jnp.exp(gcb)
    lmat = jnp.where(strict, _bmm_nt(kb, k) * decay, 0.0)
    if inv_saved is None:
        mm3 = _bh3_passes
        inv = (ii == jj).astype(F32) - lmat
        pw = mm3(lmat, lmat)
        for step in range(5):
            inv = inv + mm3(inv, pw)
            if step < 4:
                pw = mm3(pw, pw)
    else:
        mm3 = _bh3
        inv = _saved_inverse(lmat, inv_saved)
    u = mm3(inv, vb)
    w = mm3(inv, kb * egc)
    attn = _bmm_nt(q, k) * decay
    qd = q * egc
    g_end = jnp.sum(gb, axis=1, keepdims=True)
    kd = k * jnp.exp(g_end - gcb)
    v_new = u - _bmm(w, s)
    o = _bmm(qd, s) + _bmm(attn, v_new)
    s_new = s * jnp.exp(g_end) + _bmm_tn(kd, v_new)
    return s_new, o, inv


def gdn_fwd(q, k, v, gb, bb):
    h_n, t_len, d = q.shape
    n = t_len // CHUNK
    blk = pl.BlockSpec((h_n, CHUNK, d), lambda i: (0, i, 0))

    def body(q_ref, k_ref, v_ref, g_ref, b_ref, o_ref, sall_ref, inv_ref, s_s):
        @pl.when(pl.program_id(0) == 0)
        def _():
            s_s[...] = jnp.zeros_like(s_s)

        s = s_s[...]
        sall_ref[0] = s
        s_s[...], o_ref[...], inv_ref[0] = gdn_step(s, q_ref[...], k_ref[...], v_ref[...], g_ref[...], b_ref[...])

    return pl.pallas_call(
        body,
        name="gdn_fwd",
        grid=(n,),
        in_specs=[blk] * 5,
        out_specs=[blk, pl.BlockSpec((1, h_n, d, d), lambda i: (i, 0, 0, 0)),
                   pl.BlockSpec((1, h_n, CHUNK, CHUNK), lambda i: (i, 0, 0, 0))],
        out_shape=[jax.ShapeDtypeStruct((h_n, t_len, d), F32), jax.ShapeDtypeStruct((n, h_n, d, d), F32),
                   jax.ShapeDtypeStruct((n, h_n, CHUNK, CHUNK), F32)],
        scratch_shapes=[pltpu.VMEM((h_n, d, d), F32)],
        compiler_params=_cparams("arbitrary"),
    )(q, k, v, gb, bb)


def gdn_bwd(q, k, v, gb, bb, s_all, inv_all, do):
    h_n, t_len, d = q.shape
    n = t_len // CHUNK
    blk = pl.BlockSpec((h_n, CHUNK, d), lambda i: (0, n - 1 - i, 0))

    def body(q_ref, k_ref, v_ref, g_ref, b_ref, sall_ref, inv_ref, do_ref, dq_ref, dk_ref, dv_ref, dg_ref, db_ref,
             ds_s):
        @pl.when(pl.program_id(0) == 0)
        def _():
            ds_s[...] = jnp.zeros_like(ds_s)

        inv = inv_ref[0]
        _, pull = jax.vjp(lambda *a: gdn_step(*a, inv_saved=inv)[:2], sall_ref[0], q_ref[...], k_ref[...], v_ref[...],
                          g_ref[...], b_ref[...])
        ds_s[...], dq_ref[...], dk_ref[...], dv_ref[...], dg_ref[...], db_ref[...] = pull((ds_s[...], do_ref[...]))

    return pl.pallas_call(
        body,
        name="gdn_bwd",
        grid=(n,),
        in_specs=[blk] * 5 + [pl.BlockSpec((1, h_n, d, d), lambda i: (n - 1 - i, 0, 0, 0)),
                              pl.BlockSpec((1, h_n, CHUNK, CHUNK), lambda i: (n - 1 - i, 0, 0, 0)), blk],
        out_specs=[blk] * 5,
        out_shape=[jax.ShapeDtypeStruct((h_n, t_len, d), F32)] * 5,
        scratch_shapes=[pltpu.VMEM((h_n, d, d), F32)],
        compiler_params=_cparams("arbitrary"),
    )(q, k, v, gb, bb, s_all, inv_all, do)


def _pad_cols(a, n):
    return jnp.pad(a, ((0, 0), (0, n - a.shape[1])))


def arrange_w_in(w):
    pieces, start = [], 0
    for n in R_SPLITS:
        pieces.append(w[:, start:start + n])
        start += n
    cq, ckv, kr, mgate, gq, gk, gv, ga, gb, ggate = pieces
    return jnp.concatenate([mgate, gq, gk, gv, ggate, cq, ckv, _pad_cols(kr, LANE),
                            _pad_cols(jnp.concatenate([ga, gb], axis=1), LANE)], axis=1)


def unarrange_w_in(pieces):
    mgate, gq, gk, gv, ggate, rest = pieces

    def cols(start, n):
        return rest[:, start - P_CQ:start - P_CQ + n]
    return jnp.concatenate([cols(P_CQ, Q_LORA), cols(P_CKV, KV_LORA), cols(P_KR, ROPE), mgate, gq, gk, gv,
                            cols(P_GAB, HEADS), cols(P_GAB + HEADS, HEADS), ggate], axis=1)


def arrange_w_uq(w):
    w = w.reshape(w.shape[0], HEADS, QK)
    return jnp.pad(w, ((0, 0), (0, 0), (0, HEAD_PAD - QK))).reshape(w.shape[0], HEADS * HEAD_PAD)


def unarrange_w_uq(g):
    return g.reshape(g.shape[0], HEADS, HEAD_PAD)[:, :, :QK].reshape(g.shape[0], HEADS * QK)


def local_step(x, pos, tgt, p, on_early_grads=None, on_d_proj=None, on_weight_grads=None, on_d_xn=None,
               first_after=None, late_weights=None):
    t_len = x.shape[0]
    w_in = p["w_in"]
    norm_gain = p["norm_gain"].reshape(1, D_MODEL)
    qa_gain = p["mla_q_a_gain"].reshape(1, Q_LORA)
    kva_gain = p["mla_kv_a_gain"].reshape(1, KV_LORA)
    qg = _pad_cols(p["mla_q_norm_gain"].reshape(1, QK), HEAD_PAD)
    kg = _pad_cols(p["mla_k_norm_gain"].reshape(1, QK), HEAD_PAD)
    alog = _pad_cols(p["gdn_a_log"].reshape(1, HEADS), LANE)
    dtb = _pad_cols(p["gdn_dt_bias"].reshape(1, HEADS), LANE)
    og = p["gdn_out_norm_gain"].reshape(1, GDN_DIM)
    half = ROPE // 2
    inv_freq = jnp.power(ROPE_THETA, -jnp.arange(half, dtype=F32) / half)
    invf = _pad_cols(jnp.concatenate([inv_freq, inv_freq]).reshape(1, ROPE), LANE)

    rt = 256
    r = "r"
    first_after = jnp.zeros((SUBLANE, LANE), F32) if first_after is None else first_after
    (xn,) = rowwise("rms_x", lambda i, x_, gain_, after_: f_rms_x(i, x_, gain_), t_len, rt, [(x, (r, D_MODEL, 0))],
                    [norm_gain, first_after], [(r, D_MODEL, _BF)])
    proj = matmul("proj", xn, w_in, "nn")
    if late_weights is not None:
        p = {**p, **late_weights(proj)}
    w_uq, w_ukv, w_out = p["w_uq"], p["w_ukv"], p["w_out"]
    cw = p["gdn_conv_w"].reshape(CONV_W, 3 * WIDTH)
    cwq, cwk, cwv = cw[:, :WIDTH], cw[:, WIDTH:2 * WIDTH], cw[:, 2 * WIDTH:]
    cq_in = (proj, (r, Q_LORA, P_CQ // Q_LORA))
    ckv_in = (proj, (r, KV_LORA, P_CKV // KV_LORA))
    kr_in = (proj, (r, LANE, P_KR // LANE))
    mgate_in = (proj, (r, WIDTH, P_MGATE // WIDTH))
    ggate_in = (proj, (r, WIDTH, P_GGATE // WIDTH))
    gqkv_in = [(proj, (r, WIDTH, P_GQ // WIDTH)), (proj, (r, WIDTH, P_GK // WIDTH)), (proj, (r, WIDTH, P_GV // WIDTH))]
    gab_in = (proj, (r, LANE, P_GAB // LANE))
    halos = [(proj, ("halo", WIDTH, P_GQ // WIDTH)), (proj, ("halo", WIDTH, P_GK // WIDTH)),
             (proj, ("halo", WIDTH, P_GV // WIDTH))]

    q_lat, kv_lat = rowwise("lat", f_lat, t_len, rt, [cq_in, ckv_in], [qa_gain, kva_gain],
                            [(r, Q_LORA, _BF), (r, KV_LORA, _BF)])
    q_raw = matmul("q_up", q_lat, w_uq, "nn")
    kv_raw = matmul("kv_up", kv_lat, w_ukv, "nn")
    wide = HEADS * HEAD_PAD
    head_in = [(q_raw, (r, wide, 0)), (kv_raw, (r, wide, 0)), kr_in]
    pos_in = (pos, (r, 1, 0))
    q_full, k_full, v_mla = rowwise(
        "head", lambda i, qr, kvr, kr, ps, qg_, kg_, iv: f_head(i, qr, kvr, kr, qg_, kg_, ps, iv), t_len, rt,
        head_in + [pos_in], [qg, kg, invf],
        [("h", HEADS, HEAD_PAD, _BF), ("h", HEADS, HEAD_PAD, _BF), ("h", HEADS, LANE, _BF)])
    o_mla, lse = flash_fwd(q_full, k_full, v_mla)

    pre_in = gqkv_in + [gab_in] + halos
    pre_full = [cwq, cwk, cwv, alog, dtb]
    hkind = ("h", HEADS, GDN_DIM, F32)
    gq_n, gk_n, gv_n, g_b, b_b = rowwise("gdn_pre", f_gdn_pre, t_len, rt, pre_in, pre_full, [hkind] * 5)
    o_gdn, s_all, inv_all = gdn_fwd(gq_n, gk_n, gv_n, g_b, b_b)

    mix_in = [(o_mla, (r, WIDTH, 0)), mgate_in, (o_gdn, ("h",)), ggate_in]
    (mixed,) = rowwise("mix", f_mix, t_len, rt, mix_in, [og], [(r, 2 * WIDTH, _BF)])
    dy, dy_mx, loss_acc = out_proj_loss(mixed, w_out, x, tgt)
    loss = loss_acc[0, 0]

    d_mixed = matmul("d_mixed", dy_mx, w_out, "nt")
    g_w_out = matmul("g_w_out", mixed, dy_mx, "tn")

    def mix_bwd(i, o_mla_, mgate_, o_gdn_, ggate_, d_mixed_, og_):
        do_mla_, d_mgate_, do_gdn_, d_ggate_, g_og_ = _vjp_fn(f_mix, 5, 1)(i, o_mla_, mgate_, o_gdn_, ggate_, og_, d_mixed_)
        delta_ = jnp.stack([jnp.sum(o_mla_[:, LANE * h:LANE * (h + 1)] * do_mla_[:, LANE * h:LANE * (h + 1)],
                                    axis=-1, keepdims=True) for h in range(HEADS)])
        return do_mla_, d_mgate_, do_gdn_, d_ggate_, delta_, g_og_

    do_mla, d_mgate, do_gdn, d_ggate, delta, g_og = rowwise(
        "mix_bwd", mix_bwd, t_len, rt, mix_in + [(d_mixed, (r, 2 * WIDTH, 0))], [og],
        [(r, WIDTH, F32), (r, WIDTH, _BF), hkind, (r, WIDTH, _BF), ("h", HEADS, 1, F32)], [(1, GDN_DIM)])
    dq_n, dk_n, dv_n, dg_b, db_b = gdn_bwd(gq_n, gk_n, gv_n, g_b, b_b, s_all, inv_all, do_gdn)
    cts_in = [(a, ("h",)) for a in (dq_n, dk_n, dv_n, dg_b, db_b)]
    d_gq, d_gk, d_gv, d_gab, g_cwq, g_cwk, g_cwv, g_alog, g_dtb = rowwise(
        "gdn_pre_bwd", gdn_pre_bwd, t_len, rt, pre_in + cts_in, pre_full,
        [(r, WIDTH, _BF)] * 3 + [(r, LANE, _BF)],
        [(CONV_W, WIDTH)] * 3 + [(1, LANE)] * 2, carries=[(SUBLANE, WIDTH)] * 3, reverse=True)

    dq_full, dk_full, dv_mla = flash_bwd(q_full, k_full, v_mla, do_mla, lse, delta)
    head_cts = [(a, ("h",)) for a in (dq_full, dk_full, dv_mla)]

    def head_bwd(i, q_raw_, kv_raw_, kr_, pos_, dq_, dk_, dv_, qg_, kg_, invf_):
        return _vjp_fn(f_head, 5, 3)(i, q_raw_, kv_raw_, kr_, qg_, kg_, pos_, invf_, dq_, dk_, dv_)

    dq_raw, dkv_raw, d_kr, g_qg, g_kg = rowwise(
        "head_bwd", head_bwd, t_len, rt, head_in + [pos_in] + head_cts, [qg, kg, invf],
        [(r, wide, _BF), (r, wide, _BF), (r, LANE, _BF)], [(1, HEAD_PAD), (1, HEAD_PAD)])
    dq_lat = matmul("dq_lat", dq_raw, w_uq, "nt")
    g_w_uq = matmul("g_w_uq", q_lat, dq_raw, "tn")
    dkv_lat = matmul("dkv_lat", dkv_raw, w_ukv, "nt")
    g_w_ukv = matmul("g_w_ukv", kv_lat, dkv_raw, "tn")
    grads = {
        "w_uq": g_w_uq, "w_ukv": g_w_ukv, "gdn_conv_w": jnp.concatenate([g_cwq, g_cwk, g_cwv], axis=1),
        "w_out": g_w_out, "mla_q_norm_gain": g_qg[:, :QK], "mla_k_norm_gain": g_kg[:, :QK],
        "gdn_a_log": g_alog[:, :HEADS], "gdn_dt_bias": g_dtb[:, :HEADS], "gdn_out_norm_gain": g_og,
    }
    after = jnp.zeros((SUBLANE, LANE), F32) if on_early_grads is None else on_early_grads(grads)

    def lat_bwd(i, cq_, ckv_, dql_, dkl_, gq_, gkv_, after_):
        return _vjp_fn(f_lat, 4, 2)(i, cq_, ckv_, gq_, gkv_, dql_, dkl_)

    d_cq, d_ckv, grads["mla_q_a_gain"], grads["mla_kv_a_gain"] = rowwise(
        "lat_bwd", lat_bwd, t_len, rt, [cq_in, ckv_in, (dq_lat, (r, Q_LORA, 0)), (dkv_lat, (r, KV_LORA, 0))],
        [qa_gain, kva_gain, after], [(r, Q_LORA, _BF), (r, KV_LORA, _BF)], [(1, Q_LORA), (1, KV_LORA)])

    d_proj = [d_mgate, d_gq, d_gk, d_gv, d_ggate, jnp.concatenate([d_cq, d_ckv, d_kr, d_gab], axis=1)]
    after = None if on_d_proj is None else on_d_proj(d_proj[-1])
    grads["w_in"] = [matmul("g_w_in_%d" % j, xn, piece, "tn", after=after if j == 0 else None)
                     for j, piece in enumerate(d_proj)]
    after = None if on_weight_grads is None else on_weight_grads(grads)
    d_xn = matmul_pieces("d_xn", d_proj, w_in, after=after)
    after = jnp.zeros((SUBLANE, LANE), F32) if on_d_xn is None else on_d_xn(d_xn)

    def rms_x_bwd(i, x_, dxn_, dy_, gain_, after_):
        dx, dgain = _vjp_fn(f_rms_x, 2, 1)(i, x_, gain_, dxn_)
        return dx + dy_, dgain

    grad_x, grads["norm_gain"] = rowwise(
        "rms_x_bwd", rms_x_bwd, t_len, rt, [(x, (r, D_MODEL, 0)), (d_xn, (r, D_MODEL, 0)), (dy, (r, D_MODEL, 0))],
        [norm_gain, after], [(r, D_MODEL, F32)], [(1, D_MODEL)])
    return loss, grad_x, grads


MESH = pl.DeviceIdType.MESH
ANY = pl.BlockSpec(memory_space=pl.ANY)
CHIP_FLIPS = ((1, 0), (0, 1), (1, 1))


def _place():
    return lax.axis_index("x"), lax.axis_index("y"), lax.axis_index("c")


def _flip(v, f):
    return 1 - v if f else v


def all_gather(shards):
    n_arr = len(shards)

    def body(*refs):
        x_refs, o_refs = refs[:n_arr], refs[n_arr:2 * n_arr]
        send_sems, recv_sems, local_sems = refs[2 * n_arr:]
        x, y, c = _place()
        me, sibling = (x, y, c), (x, y, 1 - c)
        chips = [(_flip(x, fx), _flip(y, fy)) for fx, fy in CHIP_FLIPS]

        def copy(a, k, block, to, src=None):
            px, py, pc = block
            dst = o_refs[a].at[4 * px + 2 * py + pc]
            return pltpu.make_async_remote_copy(
                src_ref=dst if src is None else src, dst_ref=dst, send_sem=send_sems.at[a, k],
                recv_sem=recv_sems.at[a, k], device_id=to, device_id_type=MESH)

        mine, first, passed = [], [], []
        for a in range(n_arr):
            cp = pltpu.make_async_copy(x_refs[a], o_refs[a].at[4 * x + 2 * y + c], local_sems.at[a])
            cp.start()
            mine.append(cp)
            first.append(copy(a, 0, me, sibling, src=x_refs[a]))
            first += [copy(a, 1 + j, me, (*chip, c), src=x_refs[a]) for j, chip in enumerate(chips)]
        for cp in first:
            cp.start()
        for j, chip in enumerate(chips):
            for a in range(n_arr):
                copy(a, 1 + j, (*chip, c), me).wait_recv()
                cp = copy(a, 4 + j, (*chip, c), sibling)
                cp.start()
                passed.append(cp)
        for a in range(n_arr):
            copy(a, 0, sibling, me).wait_recv()
            for j, chip in enumerate(chips):
                copy(a, 4 + j, (*chip, 1 - c), me).wait_recv()
        for cp in first + passed:
            cp.wait_send()
        for cp in mine:
            cp.wait()

    return pl.pallas_call(
        body,
        name="all_gather",
        out_shape=[jax.ShapeDtypeStruct((N_DEV,) + s.shape, s.dtype) for s in shards],
        in_specs=[ANY] * n_arr,
        out_specs=[ANY] * n_arr,
        scratch_shapes=[pltpu.SemaphoreType.DMA((n_arr, 7)), pltpu.SemaphoreType.DMA((n_arr, 7)),
                        pltpu.SemaphoreType.DMA((n_arr,))],
    )(*shards)


HBM = pl.BlockSpec(memory_space=pltpu.HBM)
SEMS = pl.BlockSpec(memory_space=pltpu.SEMAPHORE)
SIDE_EFFECT = pltpu.SideEffectType.DATAFLOW_SIDE_EFFECTING


def core_routes(x, y, c):
    return [(2 * q + (1 - c), q, (x, y, 1 - c)) for q in range(4)]


def chip_routes(x, y, c):
    routes = []
    for j, (fx, fy) in enumerate(CHIP_FLIPS):
        px, py = _flip(x, fx), _flip(y, fy)
        routes.append((2 * px + py, j, (px, py, c)))
    return routes


def _route_copies(routes, n_routes, src_refs, land_refs, sems):
    x, y, c = _place()
    n_copies = len(src_refs) * n_routes
    return [pltpu.make_async_remote_copy(src_ref=src.at[s], dst_ref=land.at[d], send_sem=sems[a * n_routes + k],
                                         recv_sem=sems[n_copies + a * n_routes + k], device_id=dev,
                                         device_id_type=MESH)
            for a, (src, land) in enumerate(zip(src_refs, land_refs)) for k, (s, d, dev) in enumerate(routes(x, y, c))]


def gather_routes(x, y, c):
    me = 4 * x + 2 * y + c
    return [(0, me, (_flip(x, (k >> 2) & 1), _flip(y, (k >> 1) & 1), _flip(c, k & 1))) for k in range(1, N_DEV)]


def exchange_start(name, routes, n_routes, srcs, n_slots=None):
    n = len(srcs)
    n_sems = 2 * n * n_routes
    lands = [lax.empty((n_routes if n_slots is None else n_slots,) + s.shape[1:], s.dtype) for s in srcs]

    def body(*refs):
        for cp in _route_copies(routes, n_routes, refs[:n], refs[n:2 * n], refs[2 * n:2 * n + n_sems]):
            cp.start()
        refs[-1][...] = jnp.zeros_like(refs[-1])

    res = pl.pallas_call(
        body,
        name=name,
        out_shape=(*[pltpu.SemaphoreType.DMA(())] * n_sems, *[pltpu.HBM(a.shape, a.dtype) for a in srcs + lands],
                   jax.ShapeDtypeStruct((SUBLANE, LANE), F32)),
        in_specs=[HBM] * (2 * n),
        out_specs=(*[SEMS] * n_sems, *[HBM] * (2 * n), pl.BlockSpec(memory_space=pltpu.VMEM)),
        input_output_aliases={i: n_sems + i for i in range(2 * n)},
        compiler_params=pltpu.CompilerParams(has_side_effects=SIDE_EFFECT),
    )(*[pltpu.with_memory_space_constraint(a, pltpu.HBM) for a in srcs + lands])
    return (res[:n_sems], res[n_sems:-1]), res[-1]


def exchange_wait(name, routes, handle, after):
    sems, thru = handle
    n, n_sems = len(thru) // 2, len(sems)
    n_routes = n_sems // (2 * n)

    def body(*refs):
        for cp in _route_copies(routes, n_routes, refs[:n], refs[n:2 * n], refs[2 * n:2 * n + n_sems]):
            cp.wait_send()
            cp.wait_recv()

    res = pl.pallas_call(
        body,
        name=name,
        out_shape=tuple(pltpu.HBM(a.shape, a.dtype) for a in thru),
        in_specs=[HBM] * (2 * n) + [SEMS] * n_sems + [ANY],
        out_specs=tuple([HBM] * (2 * n)),
        input_output_aliases={i: i for i in range(2 * n)},
        compiler_params=pltpu.CompilerParams(has_side_effects=SIDE_EFFECT),
    )(*thru, *sems, after)
    return list(res[:n]), list(res[n:])


def gather_small(v):
    def body(v_ref, o_ref, send_sems, recv_sems, local_sem):
        x, y, c = _place()
        me = 4 * x + 2 * y + c
        mine = pltpu.make_async_copy(v_ref, o_ref.at[me], local_sem)
        mine.start()
        copies = []
        for k in range(1, N_DEV):
            fx, fy, fc = (k >> 2) & 1, (k >> 1) & 1, k & 1
            cp = pltpu.make_async_remote_copy(
                src_ref=v_ref, dst_ref=o_ref.at[me], send_sem=send_sems.at[k - 1], recv_sem=recv_sems.at[k - 1],
                device_id=(_flip(x, fx), _flip(y, fy), _flip(c, fc)), device_id_type=MESH)
            cp.start()
            copies.append(cp)
        for cp in copies:
            cp.wait()
        mine.wait()

    return pl.pallas_call(
        body,
        name="gather_small",
        out_shape=jax.ShapeDtypeStruct((N_DEV,) + v.shape, v.dtype),
        in_specs=[ANY],
        out_specs=ANY,
        scratch_shapes=[pltpu.SemaphoreType.DMA((N_DEV - 1,)), pltpu.SemaphoreType.DMA((N_DEV - 1,)),
                        pltpu.SemaphoreType.DMA],
    )(v)


def _row_tile(rows):
    for t in (256, 128, 64, 32, 16, 8):
        if rows % t == 0:
            return t
    return rows


def add_core_parts(name, g, recv, c_idx, wire):
    _, rows, cols = g.shape
    tr = _row_tile(rows)

    def body(c_ref, g_ref, r_ref, o_ref, w_ref):
        part = g_ref[...] + r_ref[...]
        o_ref[...] = part
        w_ref[...] = part.astype(w_ref.dtype)

    blk = pl.BlockSpec((1, tr, cols), lambda q, i, c_ref: (q, i, 0))
    return pl.pallas_call(
        body,
        name=name,
        grid_spec=pltpu.PrefetchScalarGridSpec(
            num_scalar_prefetch=1,
            grid=(4, rows // tr),
            in_specs=[pl.BlockSpec((1, tr, cols), lambda q, i, c_ref: (2 * q + c_ref[0], i, 0)), blk],
            out_specs=[blk, blk],
        ),
        out_shape=[jax.ShapeDtypeStruct((4, rows, cols), F32), jax.ShapeDtypeStruct((4, rows, cols), wire)],
        compiler_params=_cparams("parallel", "parallel"),
    )(c_idx, g, recv)


def _adamw(w, g, m, v):
    m = ADAM_B1 * m + (1.0 - ADAM_B1) * g
    v = ADAM_B2 * v + (1.0 - ADAM_B2) * (g * g)
    m_hat = m / (1.0 - ADAM_B1 ** ADAM_STEP)
    v_hat = v / (1.0 - ADAM_B2 ** ADAM_STEP)
    delta = -ADAM_LR * (m_hat / (jnp.sqrt(v_hat) + ADAM_EPS) + ADAM_WD * w)
    return delta, m, v


def adamw_sharded(name, parts, recv, q_idx, w, m, v):
    rows, cols = w.shape
    tr = _row_tile(rows)

    def body(q_ref, p_ref, r_ref, w_ref, m_ref, v_ref, g_out, d_out, m_out, v_out):
        g = p_ref[0] + r_ref[0].astype(F32) + r_ref[1].astype(F32) + r_ref[2].astype(F32)
        d, m_new, v_new = _adamw(w_ref[...], g, m_ref[...], v_ref[...])
        g_out[...], d_out[...], m_out[...], v_out[...] = g, d, m_new, v_new

    blk = pl.BlockSpec((tr, cols), lambda i, q_ref: (i, 0))
    return pl.pallas_call(
        body,
        name=name,
        grid_spec=pltpu.PrefetchScalarGridSpec(
            num_scalar_prefetch=1,
            grid=(rows // tr,),
            in_specs=[pl.BlockSpec((1, tr, cols), lambda i, q_ref: (q_ref[0], i, 0)),
                      pl.BlockSpec((3, tr, cols), lambda i, q_ref: (0, i, 0)), blk, blk, blk],
            out_specs=[blk] * 4,
        ),
        out_shape=[jax.ShapeDtypeStruct((rows, cols), F32)] * 4,
        compiler_params=_cparams("parallel"),
    )(q_idx, parts, recv, w, m, v)


def adamw_small(gathered, w, m, v):
    def body(g_ref, w_ref, m_ref, v_ref, g_out, d_out, m_out, v_out):
        g = g_ref[0]
        for j in range(1, N_DEV):
            g = g + g_ref[j]
        d, m_new, v_new = _adamw(w_ref[...], g, m_ref[...], v_ref[...])
        g_out[...], d_out[...], m_out[...], v_out[...] = g, d, m_new, v_new

    return pl.pallas_call(body, name="adamw_small", out_shape=[jax.ShapeDtypeStruct(w.shape, F32)] * 4)(gathered, w, m, v)


SHARDED = ("w_in", "w_uq", "w_ukv", "gdn_conv_w", "w_out")
SMALL = (("norm_gain", D_MODEL), ("mla_q_a_gain", Q_LORA), ("mla_kv_a_gain", KV_LORA), ("mla_q_norm_gain", QK),
         ("mla_k_norm_gain", QK), ("gdn_a_log", HEADS), ("gdn_dt_bias", HEADS), ("gdn_out_norm_gain", GDN_DIM))
WEIGHT_ORDER = ("norm_gain", "w_in", "mla_q_a_gain", "mla_kv_a_gain", "w_uq", "w_ukv", "mla_q_norm_gain",
                "mla_k_norm_gain", "gdn_conv_w", "gdn_a_log", "gdn_dt_bias", "gdn_out_norm_gain", "w_out")


def _pack_small(d):
    rows = []
    for name, n in SMALL:
        a = d[name].reshape(-1).astype(F32)
        n_pad = -(-n // LANE) * LANE
        rows.append(jnp.pad(a, (0, n_pad - n)).reshape(n_pad // LANE, LANE))
    packed = jnp.concatenate(rows, axis=0)
    return jnp.pad(packed, ((0, -packed.shape[0] % SUBLANE), (0, 0)))


def _unpack_small(packed):
    out, row = {}, 0
    for name, n in SMALL:
        n_rows = -(-n // LANE)
        out[name] = packed[row:row + n_rows].reshape(-1)[:n].reshape(1, n)
        row += n_rows
    return out


def kernel(x, positions, norm_gain, w_in, mla_q_a_gain, mla_kv_a_gain, w_uq, w_ukv, mla_q_norm_gain, mla_k_norm_gain, gdn_conv_w, gdn_a_log, gdn_dt_bias, gdn_out_norm_gain, w_out, loss_target, m_norm_gain, m_w_in, m_mla_q_a_gain, m_mla_kv_a_gain, m_w_uq, m_w_ukv, m_mla_q_norm_gain, m_mla_k_norm_gain, m_gdn_conv_w, m_gdn_a_log, m_gdn_dt_bias, m_gdn_out_norm_gain, m_w_out, v_norm_gain, v_w_in, v_mla_q_a_gain, v_mla_kv_a_gain, v_w_uq, v_w_ukv, v_mla_q_norm_gain, v_mla_k_norm_gain, v_gdn_conv_w, v_gdn_a_log, v_gdn_dt_bias, v_gdn_out_norm_gain, v_w_out):
    w = dict(norm_gain=norm_gain, w_in=w_in, mla_q_a_gain=mla_q_a_gain, mla_kv_a_gain=mla_kv_a_gain, w_uq=w_uq,
             w_ukv=w_ukv, mla_q_norm_gain=mla_q_norm_gain, mla_k_norm_gain=mla_k_norm_gain, gdn_conv_w=gdn_conv_w,
             gdn_a_log=gdn_a_log, gdn_dt_bias=gdn_dt_bias, gdn_out_norm_gain=gdn_out_norm_gain, w_out=w_out)
    m = dict(norm_gain=m_norm_gain, w_in=m_w_in, mla_q_a_gain=m_mla_q_a_gain, mla_kv_a_gain=m_mla_kv_a_gain,
             w_uq=m_w_uq, w_ukv=m_w_ukv, mla_q_norm_gain=m_mla_q_norm_gain, mla_k_norm_gain=m_mla_k_norm_gain,
             gdn_conv_w=m_gdn_conv_w, gdn_a_log=m_gdn_a_log, gdn_dt_bias=m_gdn_dt_bias,
             gdn_out_norm_gain=m_gdn_out_norm_gain, w_out=m_w_out)
    v = dict(norm_gain=v_norm_gain, w_in=v_w_in, mla_q_a_gain=v_mla_q_a_gain, mla_kv_a_gain=v_mla_kv_a_gain,
             w_uq=v_w_uq, w_ukv=v_w_ukv, mla_q_norm_gain=v_mla_q_norm_gain, mla_k_norm_gain=v_mla_k_norm_gain,
             gdn_conv_w=v_gdn_conv_w, gdn_a_log=v_gdn_a_log, gdn_dt_bias=v_gdn_dt_bias,
             gdn_out_norm_gain=v_gdn_out_norm_gain, w_out=v_w_out)
    t_len = x.shape[1]

    shards = [w[n][0] if n == "gdn_conv_w" else w[n][0].astype(_BF) for n in SHARDED]
    xi, yi, ci = _place()
    c_idx = jnp.reshape(ci, (1,)).astype(jnp.int32)
    q_idx = jnp.reshape(2 * xi + yi, (1,)).astype(jnp.int32)
    flight = {}

    def cols_whole(g):
        return g.transpose(1, 0, 2).reshape(g.shape[1], N_DEV * g.shape[2])

    def col_blocks(g):
        return g.reshape(g.shape[0], N_DEV, g.shape[1] // N_DEV).transpose(1, 0, 2)

    (a_w_in,) = all_gather(shards[:1])
    p = {n: w[n] for n, _ in SMALL}
    p["w_in"] = arrange_w_in(cols_whole(a_w_in))
    flight["weights"], weights_token = exchange_start(
        "gather_start", gather_routes, N_DEV - 1, [s[None] for s in shards[1:]], n_slots=N_DEV)

    def late_weights(proj):
        _, landed = exchange_wait("gather_wait", gather_routes, flight["weights"], proj)
        me = 4 * xi + 2 * yi + ci
        a_w_uq, a_w_ukv, a_cw, a_w_out = [lax.dynamic_update_slice(land, s[None], (me, 0, 0))
                                          for land, s in zip(landed, shards[1:])]
        return {"w_uq": arrange_w_uq(cols_whole(a_w_uq)), "w_ukv": cols_whole(a_w_ukv),
                "gdn_conv_w": cols_whole(a_cw), "w_out": a_w_out.reshape(N_DEV * a_w_out.shape[1], a_w_out.shape[2])}

    early, parts = SHARDED[1:], {}

    def core_stage(tag, names, blocks):
        flight["cores" + tag], token = exchange_start("cores_start" + tag, core_routes, 4, blocks)
        flight["names" + tag] = names
        return token

    def chip_stage(tag, after):
        blocks, landed = exchange_wait("cores_wait" + tag, core_routes, flight["cores" + tag], after)
        for n, g, r in zip(flight["names" + tag], blocks, landed):
            parts[n] = add_core_parts("add_" + n, g, r, c_idx, F32 if n == "gdn_conv_w" else _BF)
        wires = [parts[n][1] for n in flight["names" + tag]]
        flight["chips" + tag], token = exchange_start("chips_start" + tag, chip_routes, 3, wires)
        return token

    def on_early_grads(grads):
        return core_stage("_early", early, [
            col_blocks(unarrange_w_uq(grads["w_uq"])), col_blocks(grads["w_ukv"]), col_blocks(grads["gdn_conv_w"]),
            grads["w_out"].reshape(N_DEV, D_MODEL // N_DEV, D_MODEL)])

    def on_d_proj(d_proj):
        return chip_stage("_early", d_proj)

    def on_weight_grads(grads):
        return core_stage("", SHARDED[:1], [col_blocks(unarrange_w_in(grads["w_in"]))])

    def on_d_xn(d_xn):
        return chip_stage("", d_xn)

    pos = positions.reshape(t_len, 1).astype(F32)
    loss, grad_x, grads = local_step(x.reshape(t_len, D_MODEL), pos, loss_target.reshape(t_len, D_MODEL), p,
                                     on_early_grads=on_early_grads, on_d_proj=on_d_proj,
                                     on_weight_grads=on_weight_grads, on_d_xn=on_d_xn,
                                     first_after=weights_token, late_weights=late_weights)
    loss = lax.psum(loss, ("x", "y", "c"))
    out = {}
    small_all = gather_small(_pack_small(grads))
    res = adamw_small(small_all, _pack_small(w), _pack_small(m), _pack_small(v))
    unpacked = [_unpack_small(a) for a in res]
    for n, _ in SMALL:
        out[n] = [u[n] for u in unpacked]

    _, from_chips_early = exchange_wait("chips_wait_early", chip_routes, flight["chips_early"], res[0])
    _, from_chips = exchange_wait("chips_wait", chip_routes, flight["chips"], res[0])
    for n, rcv in zip(SHARDED, from_chips + from_chips_early):
        prt = parts[n][0]
        shape = w[n].shape
        res = adamw_sharded("adamw_" + n, prt, rcv, q_idx, w[n].reshape(shape[-2:]), m[n].reshape(shape[-2:]),
                            v[n].reshape(shape[-2:]))
        out[n] = [a.reshape(shape) for a in res]

    return (loss, grad_x.reshape(x.shape), *[out[n][0] for n in WEIGHT_ORDER], *[out[n][1] for n in WEIGHT_ORDER],
            *[out[n][2] for n in WEIGHT_ORDER], *[out[n][3] for n in WEIGHT_ORDER])
```

```python
import functools

import jax
import jax.numpy as jnp
from jax import lax
from jax.experimental import pallas as pl
from jax.experimental.pallas import tpu as pltpu

F32 = jnp.float32
_BF = jnp.bfloat16

D_MODEL = 2048
HEADS = 8
NOPE = 128
ROPE = 64
QK = NOPE + ROPE
Q_LORA = 512
KV_LORA = 256
HEAD_PAD = 256
GDN_DIM = 128
WIDTH = HEADS * 128
CONV_W = 4
CHUNK = 64
ROPE_THETA = 10000.0
EPS = 1e-6
N_DEV = 8
LANE = 128
SUBLANE = 8
VMEM_LIMIT = 48 * 1024 * 1024

ADAM_LR, ADAM_B1, ADAM_B2, ADAM_EPS, ADAM_WD, ADAM_STEP = 0.001, 0.9, 0.999, 1e-08, 0.01, 10

P_MGATE, P_GQ, P_GK, P_GV, P_GGATE = 0, 1024, 2048, 3072, 4096
P_CQ, P_CKV, P_KR, P_GAB = 5120, 5632, 5888, 6016
R_SPLITS = (512, 256, 64, 1024, 1024, 1024, 1024, 8, 8, 1024)


def _cparams(*sem):
    return pltpu.CompilerParams(dimension_semantics=sem, vmem_limit_bytes=VMEM_LIMIT)


def _d_nn(a, b):
    return jnp.dot(a.astype(_BF), b.astype(_BF), preferred_element_type=F32)


def _d_nt(a, b):
    return lax.dot_general(a.astype(_BF), b.astype(_BF), (((1,), (1,)), ((), ())), preferred_element_type=F32)


def _d_tn(a, b):
    return lax.dot_general(a.astype(_BF), b.astype(_BF), (((0,), (0,)), ((), ())), preferred_element_type=F32)


_NN3 = (((2,), (1,)), ((0,), (0,)))
_NT3 = (((2,), (2,)), ((0,), (0,)))
_TN3 = (((1,), (1,)), ((0,), (0,)))


def _bdot(a, b, dims, hi):
    if hi:
        return lax.dot_general(a, b, dims, preferred_element_type=F32, precision=hi)
    return lax.dot_general(a.astype(_BF), b.astype(_BF), dims, preferred_element_type=F32)


def _batched_matmuls(hi):
    nn = jax.custom_vjp(lambda a, b: _bdot(a, b, _NN3, hi))
    nt = jax.custom_vjp(lambda a, b: _bdot(a, b, _NT3, hi))
    tn = jax.custom_vjp(lambda a, b: _bdot(a, b, _TN3, hi))
    nn.defvjp(lambda a, b: (_bdot(a, b, _NN3, hi), (a, b)),
              lambda r, g: (_bdot(g, r[1], _NT3, hi), _bdot(r[0], g, _TN3, hi)))
    nt.defvjp(lambda a, b: (_bdot(a, b, _NT3, hi), (a, b)),
              lambda r, g: (_bdot(g, r[1], _NN3, hi), _bdot(g, r[0], _TN3, hi)))
    tn.defvjp(lambda a, b: (_bdot(a, b, _TN3, hi), (a, b)),
              lambda r, g: (_bdot(r[1], g, _NT3, hi), _bdot(r[0], g, _NN3, hi)))
    return nn, nt, tn


_bmm, _bmm_nt, _bmm_tn = _batched_matmuls(False)


def _split2(x):
    hi = x.astype(_BF)
    return hi, (x - hi.astype(F32)).astype(_BF)


def _pdot(a, b, mode):
    (a_hi, a_lo), (b_hi, b_lo) = _split2(a), _split2(b)
    a_ax, b_ax, dims = {"nn": (2, 1, _NN3), "nt": (2, 2, _NT3), "tn": (1, 1, _TN3)}[mode]
    lhs = jnp.concatenate([a_hi, a_lo, a_hi], axis=a_ax)
    rhs = jnp.concatenate([b_hi, b_hi, b_lo], axis=b_ax)
    return lax.dot_general(lhs, rhs, dims, preferred_element_type=F32)


def _packed_matmuls():
    nn = jax.custom_vjp(lambda a, b: _pdot(a, b, "nn"))
    nn.defvjp(lambda a, b: (_pdot(a, b, "nn"), (a, b)), lambda r, g: (_pdot(g, r[1], "nt"), _pdot(r[0], g, "tn")))
    return nn


_bh3 = _packed_matmuls()
_bh3_passes = _batched_matmuls(lax.Precision.HIGH)[0]


@functools.partial(jax.custom_vjp, nondiff_argnums=(1, 2))
def _roll(x, shift, axis):
    return pltpu.roll(x, shift, axis)


def _roll_fwd(x, shift, axis):
    return pltpu.roll(x, shift, axis), None


def _roll_bwd(shift, axis, _, g):
    n = g.shape[axis]
    return (pltpu.roll(g, (n - shift) % n, axis),)


_roll.defvjp(_roll_fwd, _roll_bwd)


def _rms(x, gain):
    return x * lax.rsqrt(jnp.mean(x * x, axis=-1, keepdims=True) + EPS) * gain


MM_TILE = 1024
MM_DEPTH = 2048


def matmul(name, a, b, mode, after=None):
    if mode == "nn":
        (m, k), (k2, n) = a.shape, b.shape
    elif mode == "nt":
        (m, k), (n, k2) = a.shape, b.shape
    else:
        (k, m), (k2, n) = a.shape, b.shape
    assert k == k2, (name, a.shape, b.shape)
    tm, tn, tk = min(MM_TILE, m), min(MM_TILE, n), min(MM_DEPTH, k)
    assert m % tm == 0 and n % tn == 0 and k % tk == 0, (name, m, n, k)
    dot = {"nn": _d_nn, "nt": _d_nt, "tn": _d_tn}[mode]

    def body(a_ref, b_ref, *rest):
        o_ref = rest[-1]
        kk = pl.program_id(2)
        part = dot(a_ref[...], b_ref[...])

        @pl.when(kk == 0)
        def _():
            o_ref[...] = part

        @pl.when(kk != 0)
        def _():
            o_ref[...] += part

    if mode == "nn":
        a_spec = pl.BlockSpec((tm, tk), lambda j, i, kk: (i, kk))
        b_spec = pl.BlockSpec((tk, tn), lambda j, i, kk: (kk, j))
    elif mode == "nt":
        a_spec = pl.BlockSpec((tm, tk), lambda j, i, kk: (i, kk))
        b_spec = pl.BlockSpec((tn, tk), lambda j, i, kk: (j, kk))
    else:
        a_spec = pl.BlockSpec((tk, tm), lambda j, i, kk: (kk, i))
        b_spec = pl.BlockSpec((tk, tn), lambda j, i, kk: (kk, j))
    return pl.pallas_call(
        body,
        name=name,
        grid=(n // tn, m // tm, k // tk),
        in_specs=[a_spec, b_spec] + ([] if after is None else [pl.BlockSpec(memory_space=pl.ANY)]),
        out_specs=pl.BlockSpec((tm, tn), lambda j, i, kk: (i, j)),
        out_shape=jax.ShapeDtypeStruct((m, n), F32),
        compiler_params=_cparams("parallel", "parallel", "arbitrary"),
    )(*((a, b) if after is None else (a, b, after)))


def matmul_pieces(name, pieces, b, after=None):
    n_p, (t_len, width), n = len(pieces), pieces[0].shape, b.shape[0]
    assert width == MM_TILE and n_p % 2 == 0 and all(p.shape == pieces[0].shape for p in pieces)
    tile, tn = min(MM_TILE, t_len), min(MM_TILE, n)
    extra = [] if after is None else [after]

    def body(*refs):
        b_ref, o_ref, kk = refs[n_p], refs[-1], pl.program_id(2)
        for s in range(n_p // 2):
            @pl.when(kk == s)
            def _(s=s):
                part = _d_nt(refs[2 * s][...], b_ref[:, :width]) + _d_nt(refs[2 * s + 1][...], b_ref[:, width:])
                if s == 0:
                    o_ref[...] = part
                else:
                    o_ref[...] += part

    return pl.pallas_call(
        body,
        name=name,
        grid=(n // tn, t_len // tile, n_p // 2),
        in_specs=[pl.BlockSpec((tile, width), lambda j, i, kk: (i, 0))] * n_p
        + [pl.BlockSpec((tn, 2 * width), lambda j, i, kk: (j, kk))] + [pl.BlockSpec(memory_space=pl.ANY)] * len(extra),
        out_specs=pl.BlockSpec((tile, tn), lambda j, i, kk: (i, j)),
        out_shape=jax.ShapeDtypeStruct((t_len, n), F32),
        compiler_params=_cparams("parallel", "parallel", "arbitrary"),
    )(*pieces, b, *extra)


def out_proj_loss(mixed, w_out, x, tgt):
    (m, k), n = mixed.shape, w_out.shape[1]
    tm, tn = min(MM_TILE // 2, m), min(MM_TILE, n)
    assert m % tm == 0 and n % tn == 0

    def body(a_ref, b_ref, x_ref, t_ref, dy_ref, dy_mx_ref, loss_ref):
        first = (pl.program_id(0) == 0) & (pl.program_id(1) == 0)
        e = x_ref[...] + _d_nn(a_ref[...], b_ref[...]) - t_ref[...]
        dy = e * (1.0 / D_MODEL)
        dy_ref[...] = dy
        dy_mx_ref[...] = dy.astype(dy_mx_ref.dtype)
        part = jnp.zeros((SUBLANE, LANE), F32) + 0.5 * jnp.sum(e * e) * (1.0 / D_MODEL)

        @pl.when(first)
        def _():
            loss_ref[...] = part

        @pl.when(jnp.logical_not(first))
        def _():
            loss_ref[...] += part

    tile = pl.BlockSpec((tm, tn), lambda j, i: (i, j))
    return pl.pallas_call(
        body,
        name="out_proj_loss",
        grid=(n // tn, m // tm),
        in_specs=[pl.BlockSpec((tm, k), lambda j, i: (i, 0)), pl.BlockSpec((k, tn), lambda j, i: (0, j)), tile, tile],
        out_specs=[tile, tile, pl.BlockSpec((SUBLANE, LANE), lambda j, i: (0, 0))],
        out_shape=[jax.ShapeDtypeStruct((m, n), F32), jax.ShapeDtypeStruct((m, n), _BF),
                   jax.ShapeDtypeStruct((SUBLANE, LANE), F32)],
        compiler_params=_cparams("arbitrary", "arbitrary"),
    )(mixed, w_out, x, tgt)


def rowwise(name, fn, t_len, tile, row_in, full_in, row_out, acc_out=(), carries=(), reverse=False):
    tile = min(tile, t_len)
    n = t_len // tile
    assert t_len % tile == 0 and tile % SUBLANE == 0
    n_in, n_ro, n_acc, n_car = len(row_in) + len(full_in), len(row_out), len(acc_out), len(carries)

    def ti(i):
        return (n - 1 - i) if reverse else i

    in_specs, args = [], []
    for arr, kind in row_in:
        if kind[0] == "r":
            in_specs.append(pl.BlockSpec((tile, kind[1]), lambda i, c=kind[2]: (ti(i), c)))
        elif kind[0] == "h":
            in_specs.append(pl.BlockSpec((arr.shape[0], tile, arr.shape[2]), lambda i: (0, ti(i), 0)))
        else:
            in_specs.append(pl.BlockSpec(
                (SUBLANE, kind[1]), lambda i, c=kind[2]: (jnp.maximum(ti(i) * (tile // SUBLANE) - 1, 0), c)))
        args.append(arr)
    for arr in full_in:
        in_specs.append(pl.BlockSpec(arr.shape, lambda i, nd=arr.ndim: (0,) * nd))
        args.append(arr)
    out_specs, out_shape = [], []
    for kind in row_out:
        if kind[0] == "r":
            out_specs.append(pl.BlockSpec((tile, kind[1]), lambda i: (ti(i), 0)))
            out_shape.append(jax.ShapeDtypeStruct((t_len, kind[1]), kind[2]))
        else:
            out_specs.append(pl.BlockSpec((kind[1], tile, kind[2]), lambda i: (0, ti(i), 0)))
            out_shape.append(jax.ShapeDtypeStruct((kind[1], t_len, kind[2]), kind[3]))
    for shp in acc_out:
        out_specs.append(pl.BlockSpec(shp, lambda i, nd=len(shp): (0,) * nd))
        out_shape.append(jax.ShapeDtypeStruct(shp, F32))

    def body(*refs):
        in_refs = refs[:n_in]
        ro_refs = refs[n_in:n_in + n_ro]
        acc_refs = refs[n_in + n_ro:n_in + n_ro + n_acc]
        car_refs = refs[n_in + n_ro + n_acc:]
        step = pl.program_id(0)
        if n_car:
            @pl.when(step == 0)
            def _():
                for r in car_refs:
                    r[...] = jnp.zeros_like(r)
        vals = [r[...].astype(F32) for r in in_refs] + [r[...] for r in car_refs]
        outs = fn(ti(step), *vals)
        assert len(outs) == n_ro + n_acc + n_car, (name, len(outs))
        for r, o in zip(ro_refs, outs[:n_ro]):
            r[...] = o.astype(r.dtype)
        for r, o in zip(acc_refs, outs[n_ro:n_ro + n_acc]):
            @pl.when(step == 0)
            def _(r=r, o=o):
                r[...] = o

            @pl.when(step != 0)
            def _(r=r, o=o):
                r[...] += o
        for r, o in zip(car_refs, outs[n_ro + n_acc:]):
            r[...] = o

    res = pl.pallas_call(
        body,
        name=name,
        grid=(n,),
        in_specs=in_specs,
        out_specs=out_specs,
        out_shape=out_shape,
        scratch_shapes=[pltpu.VMEM(s, F32) for s in carries],
        compiler_params=_cparams("arbitrary"),
    )(*args)
    return list(res)


def _vjp_fn(fn, n_diff, n_out):
    def g(i, *a):
        ins, cts = a[:len(a) - n_out], a[len(a) - n_out:]
        diff, rest = ins[:n_diff], ins[n_diff:]
        _, pull = jax.vjp(lambda *d: tuple(fn(i, *d, *rest)), *diff)
        return tuple(pull(tuple(cts)))

    return g


def f_rms_x(i, x, gain):
    return (_rms(x, gain),)


def f_lat(i, cq, ckv, gq, gkv):
    return _rms(cq, gq), _rms(ckv, gkv)


def _rope_tables(pos, invf):
    ang = pos * invf
    lane = lax.broadcasted_iota(jnp.int32, (1, LANE), 1)
    cosv, sinv = jnp.cos(ang), jnp.sin(ang)
    half = ROPE // 2
    c = jnp.where(lane < ROPE, cosv, 0.0)
    sa = jnp.where(lane < half, -sinv, 0.0)
    sb = jnp.where((lane >= half) & (lane < ROPE), sinv, 0.0)
    return c, sa, sb


def _rope(xh, tabs):
    c, sa, sb = tabs
    half = ROPE // 2
    return xh * c + _roll(xh, LANE - half, 1) * sa + _roll(xh, half, 1) * sb


def f_head(i, q_raw, kv_raw, kr, qg, kg, pos, invf):
    tabs = _rope_tables(pos, invf)
    qs, ks, vs = [], [], []
    kr_ss = jnp.sum(kr * kr, axis=-1, keepdims=True)
    kr_rot = _rope(kr * kg[:, NOPE:], tabs)
    for h in range(HEADS):
        lo = q_raw[:, HEAD_PAD * h:HEAD_PAD * h + NOPE]
        hi = q_raw[:, HEAD_PAD * h + NOPE:HEAD_PAD * (h + 1)]
        ss = jnp.sum(lo * lo, axis=-1, keepdims=True) + jnp.sum(hi * hi, axis=-1, keepdims=True)
        r = lax.rsqrt(ss * (1.0 / QK) + EPS)
        qs.append(jnp.concatenate([lo * r * qg[:, :NOPE], _rope(hi * r * qg[:, NOPE:], tabs)], axis=1))
        lo = kv_raw[:, 2 * NOPE * h:2 * NOPE * h + NOPE]
        ss = jnp.sum(lo * lo, axis=-1, keepdims=True) + kr_ss
        r = lax.rsqrt(ss * (1.0 / QK) + EPS)
        ks.append(jnp.concatenate([lo * r * kg[:, :NOPE], kr_rot * r], axis=1))
        vs.append(kv_raw[:, 2 * NOPE * h + NOPE:2 * NOPE * (h + 1)])
    return jnp.stack(qs), jnp.stack(ks), jnp.stack(vs)


def f_mix(i, o_mla, mgate, o_gdn, ggate, og):
    parts = [o_mla * jax.nn.silu(mgate)]
    for h in range(HEADS):
        parts.append(_rms(o_gdn[h], og) * jax.nn.silu(ggate[:, LANE * h:LANE * (h + 1)]))
    return (jnp.concatenate(parts, axis=1),)


def _row(a, j):
    rows = lax.broadcasted_iota(jnp.int32, a.shape, 0)
    return jnp.sum(jnp.where(rows == j, a, 0.0), axis=0, keepdims=True)


def _shift_rows(x, halo, d):
    xs = _roll(x, d, 0)
    hs = _roll(halo, d, 0)
    r8 = lax.broadcasted_iota(jnp.int32, hs.shape, 0)
    top = jnp.where(r8 < d, hs, xs[:SUBLANE])
    return jnp.concatenate([top, xs[SUBLANE:]], axis=0)


def _conv_silu(x, halo, w):
    y = _row(w, CONV_W - 1) * x
    for j in range(CONV_W - 1):
        y = y + _row(w, j) * _shift_rows(x, halo, CONV_W - 1 - j)
    return jax.nn.silu(y)


def _head_select(offset):
    r = lax.broadcasted_iota(jnp.int32, (LANE, WIDTH), 0)
    c = lax.broadcasted_iota(jnp.int32, (LANE, WIDTH), 1)
    return (r == offset + lax.shift_right_logical(c, 7)).astype(_BF)


def _split3(x):
    x1 = x.astype(_BF)
    r1 = x - x1.astype(F32)
    x2 = r1.astype(_BF)
    return x1, x2, (r1 - x2.astype(F32)).astype(_BF)


@jax.custom_vjp
def _spread(x, sel):
    return _d_nn(jnp.concatenate(_split3(x), axis=1), jnp.concatenate([sel, sel, sel], axis=0))


def _spread_fwd(x, sel):
    return _spread(x, sel), sel


def _spread_bwd(sel, g):
    g1, g2, g3 = _split3(g)
    return _d_nt(g1, sel) + _d_nt(g2, sel) + _d_nt(g3, sel), jnp.zeros_like(sel)


_spread.defvjp(_spread_fwd, _spread_bwd)


def f_gdn_pre(i, gq, gk, gv, gab, hq, hk, hv, cwq, cwk, cwv, alog, dtb):
    live = jnp.where(i == 0, 0.0, 1.0)
    q = _conv_silu(gq, hq * live, cwq)
    k = _conv_silu(gk, hk * live, cwk)
    v = _conv_silu(gv, hv * live, cwv)
    g = _spread(-jnp.exp(alog) * jax.nn.softplus(gab + dtb), _head_select(0))
    beta = _spread(jax.nn.sigmoid(gab), _head_select(HEADS))
    qs, ks, vs, gs, bs = [], [], [], [], []
    for h in range(HEADS):
        sl = slice(LANE * h, LANE * (h + 1))
        qh, kh = q[:, sl], k[:, sl]
        qs.append(qh * lax.rsqrt(jnp.sum(qh * qh, axis=-1, keepdims=True) + EPS) * (GDN_DIM ** -0.5))
        ks.append(kh * lax.rsqrt(jnp.sum(kh * kh, axis=-1, keepdims=True) + EPS))
        vs.append(v[:, sl])
        gs.append(g[:, sl])
        bs.append(beta[:, sl])
    return jnp.stack(qs), jnp.stack(ks), jnp.stack(vs), jnp.stack(gs), jnp.stack(bs)


def gdn_pre_bwd(i, gq, gk, gv, gab, hq, hk, hv, dq, dk, dv, dg, db, cwq, cwk, cwv, alog, dtb, cq, ck, cv):
    grads = _vjp_fn(f_gdn_pre, 12, 5)(i, gq, gk, gv, gab, hq, hk, hv, cwq, cwk, cwv, alog, dtb, dq, dk, dv, dg, db)
    dgq, dgk, dgv, dgab, dhq, dhk, dhv, dcwq, dcwk, dcwv, dalog, ddtb = grads

    def add_tail(dx, carry):
        return jnp.concatenate([dx[:-SUBLANE], dx[-SUBLANE:] + carry], axis=0)

    return (add_tail(dgq, cq), add_tail(dgk, ck), add_tail(dgv, cv), dgab,
            dcwq, dcwk, dcwv, dalog, ddtb, dhq, dhk, dhv)


def _flash_tile(t_len):
    return min(512, t_len)


FLASH_HEADS = 4
FLASH_BWD_HEADS = 2
LOG2E = 1.4426950408889634


def _causal(rows0, shape):
    r = rows0 + lax.broadcasted_iota(jnp.int32, shape, 0)
    c = lax.broadcasted_iota(jnp.int32, shape, 1)
    return c <= r


def flash_fwd(q, k, v):
    h_n, t_len, _ = q.shape
    tq = _flash_tile(t_len)
    nq = t_len // tq
    hb = FLASH_HEADS
    kw = 2 if nq % 2 == 0 else 1
    tk = kw * tq
    c2 = (QK ** -0.5) * LOG2E
    pairs = [(i, j) for i in range(nq) for j in range(i // kw + 1)]
    qt = jnp.array([p[0] for p in pairs], jnp.int32)
    kt = jnp.array([p[1] for p in pairs], jnp.int32)

    def body(qt_ref, kt_ref, q_ref, k_ref, v_ref, o_ref, lse_ref, m_s, acc_s):
        step = pl.program_id(1)
        qi, kj = qt_ref[step], kt_ref[step]
        last = qi // kw

        @pl.when(kj == 0)
        def _():
            m_s[...] = jnp.full_like(m_s, -jnp.inf)
            acc_s[...] = jnp.zeros_like(acc_s)

        def tile(diagonal):
            s = _bdot(q_ref[...], k_ref[...], _NT3, False) * c2
            if diagonal:
                s = jnp.where(_causal((qi % kw) * tq, (tq, tk))[None], s, -jnp.inf)
            m_old = m_s[...]
            m_new = jnp.maximum(m_old, jnp.max(s, axis=-1, keepdims=True))
            p = jnp.exp2(s - m_new).astype(_BF)
            v_ones = jnp.concatenate([v_ref[...], jnp.ones((hb, tk, LANE), _BF)], axis=2)
            acc_s[...] = jnp.exp2(m_old - m_new) * acc_s[...] + _bdot(p, v_ones, _NN3, False)
            m_s[...] = m_new

        @pl.when(kj < last)
        def _():
            tile(False)

        @pl.when(kj == last)
        def _():
            tile(True)
            acc = acc_s[...]
            l_sum = acc[:, :, LANE:]
            o = acc[:, :, :LANE] / l_sum
            for hh in range(hb):
                o_ref[:, LANE * hh:LANE * (hh + 1)] = o[hh]
            lse_ref[...] = m_s[...] + jnp.log2(jnp.max(l_sum, axis=-1, keepdims=True))

    return pl.pallas_call(
        body,
        name="flash_fwd",
        grid_spec=pltpu.PrefetchScalarGridSpec(
            num_scalar_prefetch=2,
            grid=(h_n // hb, qt.shape[0]),
            in_specs=[
                pl.BlockSpec((hb, tq, HEAD_PAD), lambda h, s, qt_ref, kt_ref: (h, qt_ref[s], 0)),
                pl.BlockSpec((hb, tk, HEAD_PAD), lambda h, s, qt_ref, kt_ref: (h, kt_ref[s], 0)),
                pl.BlockSpec((hb, tk, LANE), lambda h, s, qt_ref, kt_ref: (h, kt_ref[s], 0)),
            ],
            out_specs=[
                pl.BlockSpec((tq, hb * LANE), lambda h, s, qt_ref, kt_ref: (qt_ref[s], h)),
                pl.BlockSpec((hb, tq, 1), lambda h, s, qt_ref, kt_ref: (h, qt_ref[s], 0)),
            ],
            scratch_shapes=[pltpu.VMEM((hb, tq, 1), F32), pltpu.VMEM((hb, tq, 2 * LANE), F32)],
        ),
        out_shape=[jax.ShapeDtypeStruct((t_len, h_n * LANE), F32), jax.ShapeDtypeStruct((h_n, t_len, 1), F32)],
        compiler_params=_cparams("parallel", "arbitrary"),
    )(qt, kt, q, k, v)


def flash_bwd(q, k, v, do, lse, delta):
    h_n, t_len, _ = q.shape
    tq = _flash_tile(t_len)
    nq = t_len // tq
    hb = FLASH_BWD_HEADS
    kw = 2 if nq % 2 == 0 else 1
    tk = kw * tq
    pairs = [(i, j) for j in range(nq // kw) for i in range(kw * j, nq)]
    n_steps = len(pairs)
    qt = jnp.array([p[0] for p in pairs], jnp.int32)
    kt = jnp.array([p[1] for p in pairs], jnp.int32)
    scale = QK ** -0.5
    c2 = scale * LOG2E

    def body(qt_ref, kt_ref, q_ref, k_ref, v_ref, do_ref, lse_ref, dl_ref, dq_hbm, dk_ref, dv_ref, dq_s, dq_sem):
        group, step = pl.program_id(0), pl.program_id(1)
        qi, kj = qt_ref[step], kt_ref[step]

        @pl.when(step == 0)
        def _():
            dq_s[...] = jnp.zeros_like(dq_s)

        def tile(diagonal):
            qb, kb = q_ref[...], k_ref[...]
            dob = jnp.stack([do_ref[:, LANE * hh:LANE * (hh + 1)] for hh in range(hb)])
            p = jnp.exp2(_bdot(qb, kb, _NT3, False) * c2 - lse_ref[...])
            if diagonal:
                p = jnp.where(_causal((qi % kw) * tq, (tq, tk))[None], p, 0.0)
            dv = _bdot(p, dob, _TN3, False)
            ds = p * (_bdot(dob, v_ref[...], _NT3, False) - dl_ref[...]) * scale
            dk = _bdot(ds, qb, _TN3, False)
            dq_s[:, pl.ds(pl.multiple_of(qi * tq, tq), tq), :] += _bdot(ds, kb, _NN3, False)
            return dk, dv

        @pl.when(qi == kw * kj)
        def _():
            dk_ref[...], dv_ref[...] = tile(True)

        @pl.when((qi != kw * kj) & (qi // kw == kj))
        def _():
            dk, dv = tile(True)
            dk_ref[...] += dk
            dv_ref[...] += dv

        @pl.when(qi // kw > kj)
        def _():
            dk, dv = tile(False)
            dk_ref[...] += dk
            dv_ref[...] += dv

        @pl.when(step == n_steps - 1)
        def _():
            out = pltpu.make_async_copy(dq_s, dq_hbm.at[pl.ds(group * hb, hb)], dq_sem)
            out.start()
            out.wait()

    def qmap(h, s, qt_ref, kt_ref):
        return (h, qt_ref[s], 0)

    def kmap(h, s, qt_ref, kt_ref):
        return (h, kt_ref[s], 0)

    return pl.pallas_call(
        body,
        name="flash_bwd",
        grid_spec=pltpu.PrefetchScalarGridSpec(
            num_scalar_prefetch=2,
            grid=(h_n // hb, n_steps),
            in_specs=[
                pl.BlockSpec((hb, tq, HEAD_PAD), qmap),
                pl.BlockSpec((hb, tk, HEAD_PAD), kmap),
                pl.BlockSpec((hb, tk, LANE), kmap),
                pl.BlockSpec((tq, hb * LANE), lambda h, s, qt_ref, kt_ref: (qt_ref[s], h)),
                pl.BlockSpec((hb, tq, 1), qmap),
                pl.BlockSpec((hb, tq, 1), qmap),
            ],
            out_specs=[
                pl.BlockSpec(memory_space=pl.ANY),
                pl.BlockSpec((hb, tk, HEAD_PAD), kmap),
                pl.BlockSpec((hb, tk, LANE), kmap),
            ],
            scratch_shapes=[pltpu.VMEM((hb, t_len, HEAD_PAD), F32), pltpu.SemaphoreType.DMA],
        ),
        out_shape=[
            jax.ShapeDtypeStruct((h_n, t_len, HEAD_PAD), F32),
            jax.ShapeDtypeStruct((h_n, t_len, HEAD_PAD), F32),
            jax.ShapeDtypeStruct((h_n, t_len, LANE), F32),
        ],
        compiler_params=_cparams("parallel", "arbitrary"),
    )(qt, kt, q, k, v, do, lse, delta)


def _tri_ones(h_n):
    ii = lax.broadcasted_iota(jnp.int32, (h_n, CHUNK, CHUNK), 1)
    jj = lax.broadcasted_iota(jnp.int32, (h_n, CHUNK, CHUNK), 2)
    return (ii >= jj).astype(_BF)


@jax.custom_vjp
def _chunk_cumsum(gb):
    tri = _tri_ones(gb.shape[0])
    return _bdot(jnp.concatenate([tri, tri, tri], axis=2), jnp.concatenate(_split3(gb), axis=1), _NN3, False)


def _chunk_cumsum_bwd(_, ct):
    tri = _tri_ones(ct.shape[0])
    return (_bdot(jnp.concatenate([tri, tri, tri], axis=1), jnp.concatenate(_split3(ct), axis=1), _TN3, False),)


_chunk_cumsum.defvjp(lambda gb: (_chunk_cumsum(gb), None), _chunk_cumsum_bwd)


@jax.custom_vjp
def _pair_diff(gcb):
    g1, g2, g3 = _split3(gcb)
    lane = lax.broadcasted_iota(jnp.int32, (1, 1, LANE), 2)
    one, zero = jnp.ones((), _BF), jnp.zeros((), _BF)
    a = jnp.where(lane == 0, g1, jnp.where(lane == 1, g2, jnp.where(lane == 2, g3, jnp.where(lane < 6, one, zero))))
    b = jnp.where(lane < 3, one, jnp.where(lane == 3, -g1, jnp.where(lane == 4, -g2, jnp.where(lane == 5, -g3, zero))))
    return _bdot(a, b, _NT3, False)


def _pair_diff_bwd(_, ct):
    parts = _split3(ct)
    ones = jnp.ones((ct.shape[0], 3 * CHUNK, LANE), _BF)
    rows = _bdot(jnp.concatenate(parts, axis=2), ones, _NN3, False)
    cols = _bdot(jnp.concatenate(parts, axis=1), ones, _TN3, False)
    lane = lax.broadcasted_iota(jnp.int32, (1, 1, LANE), 2)
    return (jnp.where(lane == 0, rows - cols, 0.0),)


_pair_diff.defvjp(lambda gcb: (_pair_diff(gcb), None), _pair_diff_bwd)


@jax.custom_vjp
def _saved_inverse(lmat, inv):
    return inv


def _saved_inverse_bwd(inv, g):
    return -_pdot(_pdot(inv, g, "tn"), inv, "nt"), jnp.zeros_like(inv)


_saved_inverse.defvjp(lambda lmat, inv: (inv, inv), _saved_inverse_bwd)


def gdn_step(s, q, k, v, gb, bb, inv_saved=None):
    c = CHUNK
    ii = lax.broadcasted_iota(jnp.int32, (1, c, c), 1)
    jj = lax.broadcasted_iota(jnp.int32, (1, c, c), 2)
    incl, strict = ii >= jj, ii > jj
    gcb = _chunk_cumsum(gb)
    diff = _pair_diff(gcb)
    decay = jnp.where(incl, jnp.exp(jnp.where(incl, diff, 0.0)), 0.0)
    kb, vb = k * bb, v * bb
    egc = jnp.exp(gcb)
    lmat = jnp.where(strict, _bmm_nt(kb, k) * decay, 0.0)
    if inv_saved is None:
        mm3 = _bh3_passes
        inv = (ii == jj).astype(F32) - lmat
        pw = mm3(lmat, lmat)
        for step in range(5):
            inv = inv + mm3(inv, pw)
            if step < 4:
                pw = mm3(pw, pw)
    else:
        mm3 = _bh3
        inv = _saved_inverse(lmat, inv_saved)
    u = mm3(inv, vb)
    w = mm3(inv, kb * egc)
    attn = _bmm_nt(q, k) * decay
    qd = q * egc
    g_end = jnp.sum(gb, axis=1, keepdims=True)
    kd = k * jnp.exp(g_end - gcb)
    v_new = u - _bmm(w, s)
    o = _bmm(qd, s) + _bmm(attn, v_new)
    s_new = s * jnp.exp(g_end) + _bmm_tn(kd, v_new)
    return s_new, o, inv


def gdn_fwd(q, k, v, gb, bb):
    h_n, t_len, d = q.shape
    n = t_len // CHUNK
    blk = pl.BlockSpec((h_n, CHUNK, d), lambda i: (0, i, 0))

    def body(q_ref, k_ref, v_ref, g_ref, b_ref, o_ref, sall_ref, inv_ref, s_s):
        @pl.when(pl.program_id(0) == 0)
        def _():
            s_s[...] = jnp.zeros_like(s_s)

        s = s_s[...]
        sall_ref[0] = s
        s_s[...], o_ref[...], inv_ref[0] = gdn_step(s, q_ref[...], k_ref[...], v_ref[...], g_ref[...], b_ref[...])

    return pl.pallas_call(
        body,
        name="gdn_fwd",
        grid=(n,),
        in_specs=[blk] * 5,
        out_specs=[blk, pl.BlockSpec((1, h_n, d, d), lambda i: (i, 0, 0, 0)),
                   pl.BlockSpec((1, h_n, CHUNK, CHUNK), lambda i: (i, 0, 0, 0))],
        out_shape=[jax.ShapeDtypeStruct((h_n, t_len, d), F32), jax.ShapeDtypeStruct((n, h_n, d, d), F32),
                   jax.ShapeDtypeStruct((n, h_n, CHUNK, CHUNK), F32)],
        scratch_shapes=[pltpu.VMEM((h_n, d, d), F32)],
        compiler_params=_cparams("arbitrary"),
    )(q, k, v, gb, bb)


def gdn_bwd(q, k, v, gb, bb, s_all, inv_all, do):
    h_n, t_len, d = q.shape
    n = t_len // CHUNK
    blk = pl.BlockSpec((h_n, CHUNK, d), lambda i: (0, n - 1 - i, 0))

    def body(q_ref, k_ref, v_ref, g_ref, b_ref, sall_ref, inv_ref, do_ref, dq_ref, dk_ref, dv_ref, dg_ref, db_ref,
             ds_s):
        @pl.when(pl.program_id(0) == 0)
        def _():
            ds_s[...] = jnp.zeros_like(ds_s)

        inv = inv_ref[0]
        _, pull = jax.vjp(lambda *a: gdn_step(*a, inv_saved=inv)[:2], sall_ref[0], q_ref[...], k_ref[...], v_ref[...],
                          g_ref[...], b_ref[...])
        ds_s[...], dq_ref[...], dk_ref[...], dv_ref[...], dg_ref[...], db_ref[...] = pull((ds_s[...], do_ref[...]))

    return pl.pallas_call(
        body,
        name="gdn_bwd",
        grid=(n,),
        in_specs=[blk] * 5 + [pl.BlockSpec((1, h_n, d, d), lambda i: (n - 1 - i, 0, 0, 0)),
                              pl.BlockSpec((1, h_n, CHUNK, CHUNK), lambda i: (n - 1 - i, 0, 0, 0)), blk],
        out_specs=[blk] * 5,
        out_shape=[jax.ShapeDtypeStruct((h_n, t_len, d), F32)] * 5,
        scratch_shapes=[pltpu.VMEM((h_n, d, d), F32)],
        compiler_params=_cparams("arbitrary"),
    )(q, k, v, gb, bb, s_all, inv_all, do)


def _pad_cols(a, n):
    return jnp.pad(a, ((0, 0), (0, n - a.shape[1])))


def arrange_w_in(w):
    pieces, start = [], 0
    for n in R_SPLITS:
        pieces.append(w[:, start:start + n])
        start += n
    cq, ckv, kr, mgate, gq, gk, gv, ga, gb, ggate = pieces
    return jnp.concatenate([mgate, gq, gk, gv, ggate, cq, ckv, _pad_cols(kr, LANE),
                            _pad_cols(jnp.concatenate([ga, gb], axis=1), LANE)], axis=1)


def unarrange_w_in(pieces):
    mgate, gq, gk, gv, ggate, rest = pieces

    def cols(start, n):
        return rest[:, start - P_CQ:start - P_CQ + n]
    return jnp.concatenate([cols(P_CQ, Q_LORA), cols(P_CKV, KV_LORA), cols(P_KR, ROPE), mgate, gq, gk, gv,
                            cols(P_GAB, HEADS), cols(P_GAB + HEADS, HEADS), ggate], axis=1)


def arrange_w_uq(w):
    w = w.reshape(w.shape[0], HEADS, QK)
    return jnp.pad(w, ((0, 0), (0, 0), (0, HEAD_PAD - QK))).reshape(w.shape[0], HEADS * HEAD_PAD)


def unarrange_w_uq(g):
    return g.reshape(g.shape[0], HEADS, HEAD_PAD)[:, :, :QK].reshape(g.shape[0], HEADS * QK)


def local_step(x, pos, tgt, p, on_early_grads=None, on_d_proj=None, on_weight_grads=None, on_d_xn=None,
               first_after=None, late_weights=None):
    t_len = x.shape[0]
    w_in = p["w_in"]
    norm_gain = p["norm_gain"].reshape(1, D_MODEL)
    qa_gain = p["mla_q_a_gain"].reshape(1, Q_LORA)
    kva_gain = p["mla_kv_a_gain"].reshape(1, KV_LORA)
    qg = _pad_cols(p["mla_q_norm_gain"].reshape(1, QK), HEAD_PAD)
    kg = _pad_cols(p["mla_k_norm_gain"].reshape(1, QK), HEAD_PAD)
    alog = _pad_cols(p["gdn_a_log"].reshape(1, HEADS), LANE)
    dtb = _pad_cols(p["gdn_dt_bias"].reshape(1, HEADS), LANE)
    og = p["gdn_out_norm_gain"].reshape(1, GDN_DIM)
    half = ROPE // 2
    inv_freq = jnp.power(ROPE_THETA, -jnp.arange(half, dtype=F32) / half)
    invf = _pad_cols(jnp.concatenate([inv_freq, inv_freq]).reshape(1, ROPE), LANE)

    rt = 256
    r = "r"
    first_after = jnp.zeros((SUBLANE, LANE), F32) if first_after is None else first_after
    (xn,) = rowwise("rms_x", lambda i, x_, gain_, after_: f_rms_x(i, x_, gain_), t_len, rt, [(x, (r, D_MODEL, 0))],
                    [norm_gain, first_after], [(r, D_MODEL, _BF)])
    proj = matmul("proj", xn, w_in, "nn")
    if late_weights is not None:
        p = {**p, **late_weights(proj)}
    w_uq, w_ukv, w_out = p["w_uq"], p["w_ukv"], p["w_out"]
    cw = p["gdn_conv_w"].reshape(CONV_W, 3 * WIDTH)
    cwq, cwk, cwv = cw[:, :WIDTH], cw[:, WIDTH:2 * WIDTH], cw[:, 2 * WIDTH:]
    cq_in = (proj, (r, Q_LORA, P_CQ // Q_LORA))
    ckv_in = (proj, (r, KV_LORA, P_CKV // KV_LORA))
    kr_in = (proj, (r, LANE, P_KR // LANE))
    mgate_in = (proj, (r, WIDTH, P_MGATE // WIDTH))
    ggate_in = (proj, (r, WIDTH, P_GGATE // WIDTH))
    gqkv_in = [(proj, (r, WIDTH, P_GQ // WIDTH)), (proj, (r, WIDTH, P_GK // WIDTH)), (proj, (r, WIDTH, P_GV // WIDTH))]
    gab_in = (proj, (r, LANE, P_GAB // LANE))
    halos = [(proj, ("halo", WIDTH, P_GQ // WIDTH)), (proj, ("halo", WIDTH, P_GK // WIDTH)),
             (proj, ("halo", WIDTH, P_GV // WIDTH))]

    q_lat, kv_lat = rowwise("lat", f_lat, t_len, rt, [cq_in, ckv_in], [qa_gain, kva_gain],
                            [(r, Q_LORA, _BF), (r, KV_LORA, _BF)])
    q_raw = matmul("q_up", q_lat, w_uq, "nn")
    kv_raw = matmul("kv_up", kv_lat, w_ukv, "nn")
    wide = HEADS * HEAD_PAD
    head_in = [(q_raw, (r, wide, 0)), (kv_raw, (r, wide, 0)), kr_in]
    pos_in = (pos, (r, 1, 0))
    q_full, k_full, v_mla = rowwise(
        "head", lambda i, qr, kvr, kr, ps, qg_, kg_, iv: f_head(i, qr, kvr, kr, qg_, kg_, ps, iv), t_len, rt,
        head_in + [pos_in], [qg, kg, invf],
        [("h", HEADS, HEAD_PAD, _BF), ("h", HEADS, HEAD_PAD, _BF), ("h", HEADS, LANE, _BF)])
    o_mla, lse = flash_fwd(q_full, k_full, v_mla)

    pre_in = gqkv_in + [gab_in] + halos
    pre_full = [cwq, cwk, cwv, alog, dtb]
    hkind = ("h", HEADS, GDN_DIM, F32)
    gq_n, gk_n, gv_n, g_b, b_b = rowwise("gdn_pre", f_gdn_pre, t_len, rt, pre_in, pre_full, [hkind] * 5)
    o_gdn, s_all, inv_all = gdn_fwd(gq_n, gk_n, gv_n, g_b, b_b)

    mix_in = [(o_mla, (r, WIDTH, 0)), mgate_in, (o_gdn, ("h",)), ggate_in]
    (mixed,) = rowwise("mix", f_mix, t_len, rt, mix_in, [og], [(r, 2 * WIDTH, _BF)])
    dy, dy_mx, loss_acc = out_proj_loss(mixed, w_out, x, tgt)
    loss = loss_acc[0, 0]

    d_mixed = matmul("d_mixed", dy_mx, w_out, "nt")
    g_w_out = matmul("g_w_out", mixed, dy_mx, "tn")

    def mix_bwd(i, o_mla_, mgate_, o_gdn_, ggate_, d_mixed_, og_):
        do_mla_, d_mgate_, do_gdn_, d_ggate_, g_og_ = _vjp_fn(f_mix, 5, 1)(i, o_mla_, mgate_, o_gdn_, ggate_, og_, d_mixed_)
        delta_ = jnp.stack([jnp.sum(o_mla_[:, LANE * h:LANE * (h + 1)] * do_mla_[:, LANE * h:LANE * (h + 1)],
                                    axis=-1, keepdims=True) for h in range(HEADS)])
        return do_mla_, d_mgate_, do_gdn_, d_ggate_, delta_, g_og_

    do_mla, d_mgate, do_gdn, d_ggate, delta, g_og = rowwise(
        "mix_bwd", mix_bwd, t_len, rt, mix_in + [(d_mixed, (r, 2 * WIDTH, 0))], [og],
        [(r, WIDTH, F32), (r, WIDTH, _BF), hkind, (r, WIDTH, _BF), ("h", HEADS, 1, F32)], [(1, GDN_DIM)])
    dq_n, dk_n, dv_n, dg_b, db_b = gdn_bwd(gq_n, gk_n, gv_n, g_b, b_b, s_all, inv_all, do_gdn)
    cts_in = [(a, ("h",)) for a in (dq_n, dk_n, dv_n, dg_b, db_b)]
    d_gq, d_gk, d_gv, d_gab, g_cwq, g_cwk, g_cwv, g_alog, g_dtb = rowwise(
        "gdn_pre_bwd", gdn_pre_bwd, t_len, rt, pre_in + cts_in, pre_full,
        [(r, WIDTH, _BF)] * 3 + [(r, LANE, _BF)],
        [(CONV_W, WIDTH)] * 3 + [(1, LANE)] * 2, carries=[(SUBLANE, WIDTH)] * 3, reverse=True)

    dq_full, dk_full, dv_mla = flash_bwd(q_full, k_full, v_mla, do_mla, lse, delta)
    head_cts = [(a, ("h",)) for a in (dq_full, dk_full, dv_mla)]

    def head_bwd(i, q_raw_, kv_raw_, kr_, pos_, dq_, dk_, dv_, qg_, kg_, invf_):
        return _vjp_fn(f_head, 5, 3)(i, q_raw_, kv_raw_, kr_, qg_, kg_, pos_, invf_, dq_, dk_, dv_)

    dq_raw, dkv_raw, d_kr, g_qg, g_kg = rowwise(
        "head_bwd", head_bwd, t_len, rt, head_in + [pos_in] + head_cts, [qg, kg, invf],
        [(r, wide, _BF), (r, wide, _BF), (r, LANE, _BF)], [(1, HEAD_PAD), (1, HEAD_PAD)])
    dq_lat = matmul("dq_lat", dq_raw, w_uq, "nt")
    g_w_uq = matmul("g_w_uq", q_lat, dq_raw, "tn")
    dkv_lat = matmul("dkv_lat", dkv_raw, w_ukv, "nt")
    g_w_ukv = matmul("g_w_ukv", kv_lat, dkv_raw, "tn")
    grads = {
        "w_uq": g_w_uq, "w_ukv": g_w_ukv, "gdn_conv_w": jnp.concatenate([g_cwq, g_cwk, g_cwv], axis=1),
        "w_out": g_w_out, "mla_q_norm_gain": g_qg[:, :QK], "mla_k_norm_gain": g_kg[:, :QK],
        "gdn_a_log": g_alog[:, :HEADS], "gdn_dt_bias": g_dtb[:, :HEADS], "gdn_out_norm_gain": g_og,
    }
    after = jnp.zeros((SUBLANE, LANE), F32) if on_early_grads is None else on_early_grads(grads)

    def lat_bwd(i, cq_, ckv_, dql_, dkl_, gq_, gkv_, after_):
        return _vjp_fn(f_lat, 4, 2)(i, cq_, ckv_, gq_, gkv_, dql_, dkl_)

    d_cq, d_ckv, grads["mla_q_a_gain"], grads["mla_kv_a_gain"] = rowwise(
        "lat_bwd", lat_bwd, t_len, rt, [cq_in, ckv_in, (dq_lat, (r, Q_LORA, 0)), (dkv_lat, (r, KV_LORA, 0))],
        [qa_gain, kva_gain, after], [(r, Q_LORA, _BF), (r, KV_LORA, _BF)], [(1, Q_LORA), (1, KV_LORA)])

    d_proj = [d_mgate, d_gq, d_gk, d_gv, d_ggate, jnp.concatenate([d_cq, d_ckv, d_kr, d_gab], axis=1)]
    after = None if on_d_proj is None else on_d_proj(d_proj[-1])
    grads["w_in"] = [matmul("g_w_in_%d" % j, xn, piece, "tn", after=after if j == 0 else None)
                     for j, piece in enumerate(d_proj)]
    after = None if on_weight_grads is None else on_weight_grads(grads)
    d_xn = matmul_pieces("d_xn", d_proj, w_in, after=after)
    after = jnp.zeros((SUBLANE, LANE), F32) if on_d_xn is None else on_d_xn(d_xn)

    def rms_x_bwd(i, x_, dxn_, dy_, gain_, after_):
        dx, dgain = _vjp_fn(f_rms_x, 2, 1)(i, x_, gain_, dxn_)
        return dx + dy_, dgain

    grad_x, grads["norm_gain"] = rowwise(
        "rms_x_bwd", rms_x_bwd, t_len, rt, [(x, (r, D_MODEL, 0)), (d_xn, (r, D_MODEL, 0)), (dy, (r, D_MODEL, 0))],
        [norm_gain, after], [(r, D_MODEL, F32)], [(1, D_MODEL)])
    return loss, grad_x, grads


MESH = pl.DeviceIdType.MESH
ANY = pl.BlockSpec(memory_space=pl.ANY)
CHIP_FLIPS = ((1, 0), (0, 1), (1, 1))


def _place():
    return lax.axis_index("x"), lax.axis_index("y"), lax.axis_index("c")


def _flip(v, f):
    return 1 - v if f else v


def all_gather(shards):
    n_arr = len(shards)

    def body(*refs):
        x_refs, o_refs = refs[:n_arr], refs[n_arr:2 * n_arr]
        send_sems, recv_sems, local_sems = refs[2 * n_arr:]
        x, y, c = _place()
        me, sibling = (x, y, c), (x, y, 1 - c)
        chips = [(_flip(x, fx), _flip(y, fy)) for fx, fy in CHIP_FLIPS]

        def copy(a, k, block, to, src=None):
            px, py, pc = block
            dst = o_refs[a].at[4 * px + 2 * py + pc]
            return pltpu.make_async_remote_copy(
                src_ref=dst if src is None else src, dst_ref=dst, send_sem=send_sems.at[a, k],
                recv_sem=recv_sems.at[a, k], device_id=to, device_id_type=MESH)

        mine, first, passed = [], [], []
        for a in range(n_arr):
            cp = pltpu.make_async_copy(x_refs[a], o_refs[a].at[4 * x + 2 * y + c], local_sems.at[a])
            cp.start()
            mine.append(cp)
            first.append(copy(a, 0, me, sibling, src=x_refs[a]))
            first += [copy(a, 1 + j, me, (*chip, c), src=x_refs[a]) for j, chip in enumerate(chips)]
        for cp in first:
            cp.start()
        for j, chip in enumerate(chips):
            for a in range(n_arr):
                copy(a, 1 + j, (*chip, c), me).wait_recv()
                cp = copy(a, 4 + j, (*chip, c), sibling)
                cp.start()
                passed.append(cp)
        for a in range(n_arr):
            copy(a, 0, sibling, me).wait_recv()
            for j, chip in enumerate(chips):
                copy(a, 4 + j, (*chip, 1 - c), me).wait_recv()
        for cp in first + passed:
            cp.wait_send()
        for cp in mine:
            cp.wait()

    return pl.pallas_call(
        body,
        name="all_gather",
        out_shape=[jax.ShapeDtypeStruct((N_DEV,) + s.shape, s.dtype) for s in shards],
        in_specs=[ANY] * n_arr,
        out_specs=[ANY] * n_arr,
        scratch_shapes=[pltpu.SemaphoreType.DMA((n_arr, 7)), pltpu.SemaphoreType.DMA((n_arr, 7)),
                        pltpu.SemaphoreType.DMA((n_arr,))],
    )(*shards)


HBM = pl.BlockSpec(memory_space=pltpu.HBM)
SEMS = pl.BlockSpec(memory_space=pltpu.SEMAPHORE)
SIDE_EFFECT = pltpu.SideEffectType.DATAFLOW_SIDE_EFFECTING


def core_routes(x, y, c):
    return [(2 * q + (1 - c), q, (x, y, 1 - c)) for q in range(4)]


def chip_routes(x, y, c):
    routes = []
    for j, (fx, fy) in enumerate(CHIP_FLIPS):
        px, py = _flip(x, fx), _flip(y, fy)
        routes.append((2 * px + py, j, (px, py, c)))
    return routes


def _route_copies(routes, n_routes, src_refs, land_refs, sems):
    x, y, c = _place()
    n_copies = len(src_refs) * n_routes
    return [pltpu.make_async_remote_copy(src_ref=src.at[s], dst_ref=land.at[d], send_sem=sems[a * n_routes + k],
                                         recv_sem=sems[n_copies + a * n_routes + k], device_id=dev,
                                         device_id_type=MESH)
            for a, (src, land) in enumerate(zip(src_refs, land_refs)) for k, (s, d, dev) in enumerate(routes(x, y, c))]


def gather_routes(x, y, c):
    me = 4 * x + 2 * y + c
    return [(0, me, (_flip(x, (k >> 2) & 1), _flip(y, (k >> 1) & 1), _flip(c, k & 1))) for k in range(1, N_DEV)]


def exchange_start(name, routes, n_routes, srcs, n_slots=None):
    n = len(srcs)
    n_sems = 2 * n * n_routes
    lands = [lax.empty((n_routes if n_slots is None else n_slots,) + s.shape[1:], s.dtype) for s in srcs]

    def body(*refs):
        for cp in _route_copies(routes, n_routes, refs[:n], refs[n:2 * n], refs[2 * n:2 * n + n_sems]):
            cp.start()
        refs[-1][...] = jnp.zeros_like(refs[-1])

    res = pl.pallas_call(
        body,
        name=name,
        out_shape=(*[pltpu.SemaphoreType.DMA(())] * n_sems, *[pltpu.HBM(a.shape, a.dtype) for a in srcs + lands],
                   jax.ShapeDtypeStruct((SUBLANE, LANE), F32)),
        in_specs=[HBM] * (2 * n),
        out_specs=(*[SEMS] * n_sems, *[HBM] * (2 * n), pl.BlockSpec(memory_space=pltpu.VMEM)),
        input_output_aliases={i: n_sems + i for i in range(2 * n)},
        compiler_params=pltpu.CompilerParams(has_side_effects=SIDE_EFFECT),
    )(*[pltpu.with_memory_space_constraint(a, pltpu.HBM) for a in srcs + lands])
    return (res[:n_sems], res[n_sems:-1]), res[-1]


def exchange_wait(name, routes, handle, after):
    sems, thru = handle
    n, n_sems = len(thru) // 2, len(sems)
    n_routes = n_sems // (2 * n)

    def body(*refs):
        for cp in _route_copies(routes, n_routes, refs[:n], refs[n:2 * n], refs[2 * n:2 * n + n_sems]):
            cp.wait_send()
            cp.wait_recv()

    res = pl.pallas_call(
        body,
        name=name,
        out_shape=tuple(pltpu.HBM(a.shape, a.dtype) for a in thru),
        in_specs=[HBM] * (2 * n) + [SEMS] * n_sems + [ANY],
        out_specs=tuple([HBM] * (2 * n)),
        input_output_aliases={i: i for i in range(2 * n)},
        compiler_params=pltpu.CompilerParams(has_side_effects=SIDE_EFFECT),
    )(*thru, *sems, after)
    return list(res[:n]), list(res[n:])


def gather_small(v):
    def body(v_ref, o_ref, send_sems, recv_sems, local_sem):
        x, y, c = _place()
        me = 4 * x + 2 * y + c
        mine = pltpu.make_async_copy(v_ref, o_ref.at[me], local_sem)
        mine.start()
        copies = []
        for k in range(1, N_DEV):
            fx, fy, fc = (k >> 2) & 1, (k >> 1) & 1, k & 1
            cp = pltpu.make_async_remote_copy(
                src_ref=v_ref, dst_ref=o_ref.at[me], send_sem=send_sems.at[k - 1], recv_sem=recv_sems.at[k - 1],
                device_id=(_flip(x, fx), _flip(y, fy), _flip(c, fc)), device_id_type=MESH)
            cp.start()
            copies.append(cp)
        for cp in copies:
            cp.wait()
        mine.wait()

    return pl.pallas_call(
        body,
        name="gather_small",
        out_shape=jax.ShapeDtypeStruct((N_DEV,) + v.shape, v.dtype),
        in_specs=[ANY],
        out_specs=ANY,
        scratch_shapes=[pltpu.SemaphoreType.DMA((N_DEV - 1,)), pltpu.SemaphoreType.DMA((N_DEV - 1,)),
                        pltpu.SemaphoreType.DMA],
    )(v)


def _row_tile(rows):
    for t in (256, 128, 64, 32, 16, 8):
        if rows % t == 0:
            return t
    return rows


def add_core_parts(name, g, recv, c_idx, wire):
    _, rows, cols = g.shape
    tr = _row_tile(rows)

    def body(c_ref, g_ref, r_ref, o_ref, w_ref):
        part = g_ref[...] + r_ref[...]
        o_ref[...] = part
        w_ref[...] = part.astype(w_ref.dtype)

    blk = pl.BlockSpec((1, tr, cols), lambda q, i, c_ref: (q, i, 0))
    return pl.pallas_call(
        body,
        name=name,
        grid_spec=pltpu.PrefetchScalarGridSpec(
            num_scalar_prefetch=1,
            grid=(4, rows // tr),
            in_specs=[pl.BlockSpec((1, tr, cols), lambda q, i, c_ref: (2 * q + c_ref[0], i, 0)), blk],
            out_specs=[blk, blk],
        ),
        out_shape=[jax.ShapeDtypeStruct((4, rows, cols), F32), jax.ShapeDtypeStruct((4, rows, cols), wire)],
        compiler_params=_cparams("parallel", "parallel"),
    )(c_idx, g, recv)


def _adamw(w, g, m, v):
    m = ADAM_B1 * m + (1.0 - ADAM_B1) * g
    v = ADAM_B2 * v + (1.0 - ADAM_B2) * (g * g)
    m_hat = m / (1.0 - ADAM_B1 ** ADAM_STEP)
    v_hat = v / (1.0 - ADAM_B2 ** ADAM_STEP)
    delta = -ADAM_LR * (m_hat / (jnp.sqrt(v_hat) + ADAM_EPS) + ADAM_WD * w)
    return delta, m, v


def adamw_sharded(name, parts, recv, q_idx, w, m, v):
    rows, cols = w.shape
    tr = _row_tile(rows)

    def body(q_ref, p_ref, r_ref, w_ref, m_ref, v_ref, g_out, d_out, m_out, v_out):
        g = p_ref[0] + r_ref[0].astype(F32) + r_ref[1].astype(F32) + r_ref[2].astype(F32)
        d, m_new, v_new = _adamw(w_ref[...], g, m_ref[...], v_ref[...])
        g_out[...], d_out[...], m_out[...], v_out[...] = g, d, m_new, v_new

    blk = pl.BlockSpec((tr, cols), lambda i, q_ref: (i, 0))
    return pl.pallas_call(
        body,
        name=name,
        grid_spec=pltpu.PrefetchScalarGridSpec(
            num_scalar_prefetch=1,
            grid=(rows // tr,),
            in_specs=[pl.BlockSpec((1, tr, cols), lambda i, q_ref: (q_ref[0], i, 0)),
                      pl.BlockSpec((3, tr, cols), lambda i, q_ref: (0, i, 0)), blk, blk, blk],
            out_specs=[blk] * 4,
        ),
        out_shape=[jax.ShapeDtypeStruct((rows, cols), F32)] * 4,
        compiler_params=_cparams("parallel"),
    )(q_idx, parts, recv, w, m, v)


def adamw_small(gathered, w, m, v):
    def body(g_ref, w_ref, m_ref, v_ref, g_out, d_out, m_out, v_out):
        g = g_ref[0]
        for j in range(1, N_DEV):
            g = g + g_ref[j]
        d, m_new, v_new = _adamw(w_ref[...], g, m_ref[...], v_ref[...])
        g_out[...], d_out[...], m_out[...], v_out[...] = g, d, m_new, v_new

    return pl.pallas_call(body, name="adamw_small", out_shape=[jax.ShapeDtypeStruct(w.shape, F32)] * 4)(gathered, w, m, v)


SHARDED = ("w_in", "w_uq", "w_ukv", "gdn_conv_w", "w_out")
SMALL = (("norm_gain", D_MODEL), ("mla_q_a_gain", Q_LORA), ("mla_kv_a_gain", KV_LORA), ("mla_q_norm_gain", QK),
         ("mla_k_norm_gain", QK), ("gdn_a_log", HEADS), ("gdn_dt_bias", HEADS), ("gdn_out_norm_gain", GDN_DIM))
WEIGHT_ORDER = ("norm_gain", "w_in", "mla_q_a_gain", "mla_kv_a_gain", "w_uq", "w_ukv", "mla_q_norm_gain",
                "mla_k_norm_gain", "gdn_conv_w", "gdn_a_log", "gdn_dt_bias", "gdn_out_norm_gain", "w_out")


def _pack_small(d):
    rows = []
    for name, n in SMALL:
        a = d[name].reshape(-1).astype(F32)
        n_pad = -(-n // LANE) * LANE
        rows.append(jnp.pad(a, (0, n_pad - n)).reshape(n_pad // LANE, LANE))
    packed = jnp.concatenate(rows, axis=0)
    return jnp.pad(packed, ((0, -packed.shape[0] % SUBLANE), (0, 0)))


def _unpack_small(packed):
    out, row = {}, 0
    for name, n in SMALL:
        n_rows = -(-n // LANE)
        out[name] = packed[row:row + n_rows].reshape(-1)[:n].reshape(1, n)
        row += n_rows
    return out


def kernel(x, positions, norm_gain, w_in, mla_q_a_gain, mla_kv_a_gain, w_uq, w_ukv, mla_q_norm_gain, mla_k_norm_gain, gdn_conv_w, gdn_a_log, gdn_dt_bias, gdn_out_norm_gain, w_out, loss_target, m_norm_gain, m_w_in, m_mla_q_a_gain, m_mla_kv_a_gain, m_w_uq, m_w_ukv, m_mla_q_norm_gain, m_mla_k_norm_gain, m_gdn_conv_w, m_gdn_a_log, m_gdn_dt_bias, m_gdn_out_norm_gain, m_w_out, v_norm_gain, v_w_in, v_mla_q_a_gain, v_mla_kv_a_gain, v_w_uq, v_w_ukv, v_mla_q_norm_gain, v_mla_k_norm_gain, v_gdn_conv_w, v_gdn_a_log, v_gdn_dt_bias, v_gdn_out_norm_gain, v_w_out):
    w = dict(norm_gain=norm_gain, w_in=w_in, mla_q_a_gain=mla_q_a_gain, mla_kv_a_gain=mla_kv_a_gain, w_uq=w_uq,
             w_ukv=w_ukv, mla_q_norm_gain=mla_q_norm_gain, mla_k_norm_gain=mla_k_norm_gain, gdn_conv_w=gdn_conv_w,
             gdn_a_log=gdn_a_log, gdn_dt_bias=gdn_dt_bias, gdn_out_norm_gain=gdn_out_norm_gain, w_out=w_out)
    m = dict(norm_gain=m_norm_gain, w_in=m_w_in, mla_q_a_gain=m_mla_q_a_gain, mla_kv_a_gain=m_mla_kv_a_gain,
             w_uq=m_w_uq, w_ukv=m_w_ukv, mla_q_norm_gain=m_mla_q_norm_gain, mla_k_norm_gain=m_mla_k_norm_gain,
             gdn_conv_w=m_gdn_conv_w, gdn_a_log=m_gdn_a_log, gdn_dt_bias=m_gdn_dt_bias,
             gdn_out_norm_gain=m_gdn_out_norm_gain, w_out=m_w_out)
    v = dict(norm_gain=v_norm_gain, w_in=v_w_in, mla_q_a_gain=v_mla_q_a_gain, mla_kv_a_gain=v_mla_kv_a_gain,
             w_uq=v_w_uq, w_ukv=v_w_ukv, mla_q_norm_gain=v_mla_q_norm_gain, mla_k_norm_gain=v_mla_k_norm_gain,
             gdn_conv_w=v_gdn_conv_w, gdn_a_log=v_gdn_a_log, gdn_dt_bias=v_gdn_dt_bias,
             gdn_out_norm_gain=v_gdn_out_norm_gain, w_out=v_w_out)
    t_len = x.shape[1]

    shards = [w[n][0] if n == "gdn_conv_w" else w[n][0].astype(_BF) for n in SHARDED]
    xi, yi, ci = _place()
    c_idx = jnp.reshape(ci, (1,)).astype(jnp.int32)
    q_idx = jnp.reshape(2 * xi + yi, (1,)).astype(jnp.int32)
    flight = {}

    def cols_whole(g):
        return g.transpose(1, 0, 2).reshape(g.shape[1], N_DEV * g.shape[2])

    def col_blocks(g):
        return g.reshape(g.shape[0], N_DEV, g.shape[1] // N_DEV).transpose(1, 0, 2)

    (a_w_in,) = all_gather(shards[:1])
    p = {n: w[n] for n, _ in SMALL}
    p["w_in"] = arrange_w_in(cols_whole(a_w_in))
    flight["weights"], weights_token = exchange_start(
        "gather_start", gather_routes, N_DEV - 1, [s[None] for s in shards[1:]], n_slots=N_DEV)

    def late_weights(proj):
        _, landed = exchange_wait("gather_wait", gather_routes, flight["weights"], proj)
        me = 4 * xi + 2 * yi + ci
        a_w_uq, a_w_ukv, a_cw, a_w_out = [lax.dynamic_update_slice(land, s[None], (me, 0, 0))
                                          for land, s in zip(landed, shards[1:])]
        return {"w_uq": arrange_w_uq(cols_whole(a_w_uq)), "w_ukv": cols_whole(a_w_ukv),
                "gdn_conv_w": cols_whole(a_cw), "w_out": a_w_out.reshape(N_DEV * a_w_out.shape[1], a_w_out.shape[2])}

    early, parts = SHARDED[1:], {}

    def core_stage(tag, names, blocks):
        flight["cores" + tag], token = exchange_start("cores_start" + tag, core_routes, 4, blocks)
        flight["names" + tag] = names
        return token

    def chip_stage(tag, after):
        blocks, landed = exchange_wait("cores_wait" + tag, core_routes, flight["cores" + tag], after)
        for n, g, r in zip(flight["names" + tag], blocks, landed):
            parts[n] = add_core_parts("add_" + n, g, r, c_idx, F32 if n == "gdn_conv_w" else _BF)
        wires = [parts[n][1] for n in flight["names" + tag]]
        flight["chips" + tag], token = exchange_start("chips_start" + tag, chip_routes, 3, wires)
        return token

    def on_early_grads(grads):
        return core_stage("_early", early, [
            col_blocks(unarrange_w_uq(grads["w_uq"])), col_blocks(grads["w_ukv"]), col_blocks(grads["gdn_conv_w"]),
            grads["w_out"].reshape(N_DEV, D_MODEL // N_DEV, D_MODEL)])

    def on_d_proj(d_proj):
        return chip_stage("_early", d_proj)

    def on_weight_grads(grads):
        return core_stage("", SHARDED[:1], [col_blocks(unarrange_w_in(grads["w_in"]))])

    def on_d_xn(d_xn):
        return chip_stage("", d_xn)

    pos = positions.reshape(t_len, 1).astype(F32)
    loss, grad_x, grads = local_step(x.reshape(t_len, D_MODEL), pos, loss_target.reshape(t_len, D_MODEL), p,
                                     on_early_grads=on_early_grads, on_d_proj=on_d_proj,
                                     on_weight_grads=on_weight_grads, on_d_xn=on_d_xn,
                                     first_after=weights_token, late_weights=late_weights)
    loss = lax.psum(loss, ("x", "y", "c"))
    out = {}
    small_all = gather_small(_pack_small(grads))
    res = adamw_small(small_all, _pack_small(w), _pack_small(m), _pack_small(v))
    unpacked = [_unpack_small(a) for a in res]
    for n, _ in SMALL:
        out[n] = [u[n] for u in unpacked]

    _, from_chips_early = exchange_wait("chips_wait_early", chip_routes, flight["chips_early"], res[0])
    _, from_chips = exchange_wait("chips_wait", chip_routes, flight["chips"], res[0])
    for n, rcv in zip(SHARDED, from_chips + from_chips_early):
        prt = parts[n][0]
        shape = w[n].shape
        res = adamw_sharded("adamw_" + n, prt, rcv, q_idx, w[n].reshape(shape[-2:]), m[n].reshape(shape[-2:]),
                            v[n].reshape(shape[-2:]))
        out[n] = [a.reshape(shape) for a in res]

    return (loss, grad_x.reshape(x.shape), *[out[n][0] for n in WEIGHT_ORDER], *[out[n][1] for n in WEIGHT_ORDER],
            *[out[n][2] for n in WEIGHT_ORDER], *[out[n][3] for n in WEIGHT_ORDER])
```

```python
import functools

import jax
import jax.numpy as jnp
from jax import lax
from jax.experimental import pallas as pl
from jax.experimental.pallas import tpu as pltpu

F32 = jnp.float32
_BF = jnp.bfloat16

D_MODEL = 2048
HEADS = 8
NOPE = 128
ROPE = 64
QK = NOPE + ROPE
Q_LORA = 512
KV_LORA = 256
HEAD_PAD = 256
GDN_DIM = 128
WIDTH = HEADS * 128
CONV_W = 4
CHUNK = 64
ROPE_THETA = 10000.0
EPS = 1e-6
N_DEV = 8
LANE = 128
SUBLANE = 8
VMEM_LIMIT = 48 * 1024 * 1024

ADAM_LR, ADAM_B1, ADAM_B2, ADAM_EPS, ADAM_WD, ADAM_STEP = 0.001, 0.9, 0.999, 1e-08, 0.01, 10

P_MGATE, P_GQ, P_GK, P_GV, P_GGATE = 0, 1024, 2048, 3072, 4096
P_CQ, P_CKV, P_KR, P_GAB = 5120, 5632, 5888, 6016
R_SPLITS = (512, 256, 64, 1024, 1024, 1024, 1024, 8, 8, 1024)


def _cparams(*sem):
    return pltpu.CompilerParams(dimension_semantics=sem, vmem_limit_bytes=VMEM_LIMIT)


def _d_nn(a, b):
    return jnp.dot(a.astype(_BF), b.astype(_BF), preferred_element_type=F32)


def _d_nt(a, b):
    return lax.dot_general(a.astype(_BF), b.astype(_BF), (((1,), (1,)), ((), ())), preferred_element_type=F32)


def _d_tn(a, b):
    return lax.dot_general(a.astype(_BF), b.astype(_BF), (((0,), (0,)), ((), ())), preferred_element_type=F32)


_NN3 = (((2,), (1,)), ((0,), (0,)))
_NT3 = (((2,), (2,)), ((0,), (0,)))
_TN3 = (((1,), (1,)), ((0,), (0,)))


def _bdot(a, b, dims, hi):
    if hi:
        return lax.dot_general(a, b, dims, preferred_element_type=F32, precision=hi)
    return lax.dot_general(a.astype(_BF), b.astype(_BF), dims, preferred_element_type=F32)


def _batched_matmuls(hi):
    nn = jax.custom_vjp(lambda a, b: _bdot(a, b, _NN3, hi))
    nt = jax.custom_vjp(lambda a, b: _bdot(a, b, _NT3, hi))
    tn = jax.custom_vjp(lambda a, b: _bdot(a, b, _TN3, hi))
    nn.defvjp(lambda a, b: (_bdot(a, b, _NN3, hi), (a, b)),
              lambda r, g: (_bdot(g, r[1], _NT3, hi), _bdot(r[0], g, _TN3, hi)))
    nt.defvjp(lambda a, b: (_bdot(a, b, _NT3, hi), (a, b)),
              lambda r, g: (_bdot(g, r[1], _NN3, hi), _bdot(g, r[0], _TN3, hi)))
    tn.defvjp(lambda a, b: (_bdot(a, b, _TN3, hi), (a, b)),
              lambda r, g: (_bdot(r[1], g, _NT3, hi), _bdot(r[0], g, _NN3, hi)))
    return nn, nt, tn


_bmm, _bmm_nt, _bmm_tn = _batched_matmuls(False)


def _split2(x):
    hi = x.astype(_BF)
    return hi, (x - hi.astype(F32)).astype(_BF)


def _pdot(a, b, mode):
    (a_hi, a_lo), (b_hi, b_lo) = _split2(a), _split2(b)
    a_ax, b_ax, dims = {"nn": (2, 1, _NN3), "nt": (2, 2, _NT3), "tn": (1, 1, _TN3)}[mode]
    lhs = jnp.concatenate([a_hi, a_lo, a_hi], axis=a_ax)
    rhs = jnp.concatenate([b_hi, b_hi, b_lo], axis=b_ax)
    return lax.dot_general(lhs, rhs, dims, preferred_element_type=F32)


def _packed_matmuls():
    nn = jax.custom_vjp(lambda a, b: _pdot(a, b, "nn"))
    nn.defvjp(lambda a, b: (_pdot(a, b, "nn"), (a, b)), lambda r, g: (_pdot(g, r[1], "nt"), _pdot(r[0], g, "tn")))
    return nn


_bh3 = _packed_matmuls()
_bh3_passes = _batched_matmuls(lax.Precision.HIGH)[0]


@functools.partial(jax.custom_vjp, nondiff_argnums=(1, 2))
def _roll(x, shift, axis):
    return pltpu.roll(x, shift, axis)


def _roll_fwd(x, shift, axis):
    return pltpu.roll(x, shift, axis), None


def _roll_bwd(shift, axis, _, g):
    n = g.shape[axis]
    return (pltpu.roll(g, (n - shift) % n, axis),)


_roll.defvjp(_roll_fwd, _roll_bwd)


def _rms(x, gain):
    return x * lax.rsqrt(jnp.mean(x * x, axis=-1, keepdims=True) + EPS) * gain


MM_TILE = 1024
MM_DEPTH = 2048


def matmul(name, a, b, mode, after=None):
    if mode == "nn":
        (m, k), (k2, n) = a.shape, b.shape
    elif mode == "nt":
        (m, k), (n, k2) = a.shape, b.shape
    else:
        (k, m), (k2, n) = a.shape, b.shape
    assert k == k2, (name, a.shape, b.shape)
    tm, tn, tk = min(MM_TILE, m), min(MM_TILE, n), min(MM_DEPTH, k)
    assert m % tm == 0 and n % tn == 0 and k % tk == 0, (name, m, n, k)
    dot = {"nn": _d_nn, "nt": _d_nt, "tn": _d_tn}[mode]

    def body(a_ref, b_ref, *rest):
        o_ref = rest[-1]
        kk = pl.program_id(2)
        part = dot(a_ref[...], b_ref[...])

        @pl.when(kk == 0)
        def _():
            o_ref[...] = part

        @pl.when(kk != 0)
        def _():
            o_ref[...] += part

    if mode == "nn":
        a_spec = pl.BlockSpec((tm, tk), lambda j, i, kk: (i, kk))
        b_spec = pl.BlockSpec((tk, tn), lambda j, i, kk: (kk, j))
    elif mode == "nt":
        a_spec = pl.BlockSpec((tm, tk), lambda j, i, kk: (i, kk))
        b_spec = pl.BlockSpec((tn, tk), lambda j, i, kk: (j, kk))
    else:
        a_spec = pl.BlockSpec((tk, tm), lambda j, i, kk: (kk, i))
        b_spec = pl.BlockSpec((tk, tn), lambda j, i, kk: (kk, j))
    return pl.pallas_call(
        body,
        name=name,
        grid=(n // tn, m // tm, k // tk),
        in_specs=[a_spec, b_spec] + ([] if after is None else [pl.BlockSpec(memory_space=pl.ANY)]),
        out_specs=pl.BlockSpec((tm, tn), lambda j, i, kk: (i, j)),
        out_shape=jax.ShapeDtypeStruct((m, n), F32),
        compiler_params=_cparams("parallel", "parallel", "arbitrary"),
    )(*((a, b) if after is None else (a, b, after)))


def matmul_pieces(name, pieces, b, after=None):
    n_p, (t_len, width), n = len(pieces), pieces[0].shape, b.shape[0]
    assert width == MM_TILE and n_p % 2 == 0 and all(p.shape == pieces[0].shape for p in pieces)
    tile, tn = min(MM_TILE, t_len), min(MM_TILE, n)
    extra = [] if after is None else [after]

    def body(*refs):
        b_ref, o_ref, kk = refs[n_p], refs[-1], pl.program_id(2)
        for s in range(n_p // 2):
            @pl.when(kk == s)
            def _(s=s):
                part = _d_nt(refs[2 * s][...], b_ref[:, :width]) + _d_nt(refs[2 * s + 1][...], b_ref[:, width:])
                if s == 0:
                    o_ref[...] = part
                else:
                    o_ref[...] += part

    return pl.pallas_call(
        body,
        name=name,
        grid=(n // tn, t_len // tile, n_p // 2),
        in_specs=[pl.BlockSpec((tile, width), lambda j, i, kk: (i, 0))] * n_p
        + [pl.BlockSpec((tn, 2 * width), lambda j, i, kk: (j, kk))] + [pl.BlockSpec(memory_space=pl.ANY)] * len(extra),
        out_specs=pl.BlockSpec((tile, tn), lambda j, i, kk: (i, j)),
        out_shape=jax.ShapeDtypeStruct((t_len, n), F32),
        compiler_params=_cparams("parallel", "parallel", "arbitrary"),
    )(*pieces, b, *extra)


def out_proj_loss(mixed, w_out, x, tgt):
    (m, k), n = mixed.shape, w_out.shape[1]
    tm, tn = min(MM_TILE // 2, m), min(MM_TILE, n)
    assert m % tm == 0 and n % tn == 0

    def body(a_ref, b_ref, x_ref, t_ref, dy_ref, dy_mx_ref, loss_ref):
        first = (pl.program_id(0) == 0) & (pl.program_id(1) == 0)
        e = x_ref[...] + _d_nn(a_ref[...], b_ref[...]) - t_ref[...]
        dy = e * (1.0 / D_MODEL)
        dy_ref[...] = dy
        dy_mx_ref[...] = dy.astype(dy_mx_ref.dtype)
        part = jnp.zeros((SUBLANE, LANE), F32) + 0.5 * jnp.sum(e * e) * (1.0 / D_MODEL)

        @pl.when(first)
        def _():
            loss_ref[...] = part

        @pl.when(jnp.logical_not(first))
        def _():
            loss_ref[...] += part

    tile = pl.BlockSpec((tm, tn), lambda j, i: (i, j))
    return pl.pallas_call(
        body,
        name="out_proj_loss",
        grid=(n // tn, m // tm),
        in_specs=[pl.BlockSpec((tm, k), lambda j, i: (i, 0)), pl.BlockSpec((k, tn), lambda j, i: (0, j)), tile, tile],
        out_specs=[tile, tile, pl.BlockSpec((SUBLANE, LANE), lambda j, i: (0, 0))],
        out_shape=[jax.ShapeDtypeStruct((m, n), F32), jax.ShapeDtypeStruct((m, n), _BF),
                   jax.ShapeDtypeStruct((SUBLANE, LANE), F32)],
        compiler_params=_cparams("arbitrary", "arbitrary"),
    )(mixed, w_out, x, tgt)


def rowwise(name, fn, t_len, tile, row_in, full_in, row_out, acc_out=(), carries=(), reverse=False):
    tile = min(tile, t_len)
    n = t_len // tile
    assert t_len % tile == 0 and tile % SUBLANE == 0
    n_in, n_ro, n_acc, n_car = len(row_in) + len(full_in), len(row_out), len(acc_out), len(carries)

    def ti(i):
        return (n - 1 - i) if reverse else i

    in_specs, args = [], []
    for arr, kind in row_in:
        if kind[0] == "r":
            in_specs.append(pl.BlockSpec((tile, kind[1]), lambda i, c=kind[2]: (ti(i), c)))
        elif kind[0] == "h":
            in_specs.append(pl.BlockSpec((arr.shape[0], tile, arr.shape[2]), lambda i: (0, ti(i), 0)))
        else:
            in_specs.append(pl.BlockSpec(
                (SUBLANE, kind[1]), lambda i, c=kind[2]: (jnp.maximum(ti(i) * (tile // SUBLANE) - 1, 0), c)))
        args.append(arr)
    for arr in full_in:
        in_specs.append(pl.BlockSpec(arr.shape, lambda i, nd=arr.ndim: (0,) * nd))
        args.append(arr)
    out_specs, out_shape = [], []
    for kind in row_out:
        if kind[0] == "r":
            out_specs.append(pl.BlockSpec((tile, kind[1]), lambda i: (ti(i), 0)))
            out_shape.append(jax.ShapeDtypeStruct((t_len, kind[1]), kind[2]))
        else:
            out_specs.append(pl.BlockSpec((kind[1], tile, kind[2]), lambda i: (0, ti(i), 0)))
            out_shape.append(jax.ShapeDtypeStruct((kind[1], t_len, kind[2]), kind[3]))
    for shp in acc_out:
        out_specs.append(pl.BlockSpec(shp, lambda i, nd=len(shp): (0,) * nd))
        out_shape.append(jax.ShapeDtypeStruct(shp, F32))

    def body(*refs):
        in_refs = refs[:n_in]
        ro_refs = refs[n_in:n_in + n_ro]
        acc_refs = refs[n_in + n_ro:n_in + n_ro + n_acc]
        car_refs = refs[n_in + n_ro + n_acc:]
        step = pl.program_id(0)
        if n_car:
            @pl.when(step == 0)
            def _():
                for r in car_refs:
                    r[...] = jnp.zeros_like(r)
        vals = [r[...].astype(F32) for r in in_refs] + [r[...] for r in car_refs]
        outs = fn(ti(step), *vals)
        assert len(outs) == n_ro + n_acc + n_car, (name, len(outs))
        for r, o in zip(ro_refs, outs[:n_ro]):
            r[...] = o.astype(r.dtype)
        for r, o in zip(acc_refs, outs[n_ro:n_ro + n_acc]):
            @pl.when(step == 0)
            def _(r=r, o=o):
                r[...] = o

            @pl.when(step != 0)
            def _(r=r, o=o):
                r[...] += o
        for r, o in zip(car_refs, outs[n_ro + n_acc:]):
            r[...] = o

    res = pl.pallas_call(
        body,
        name=name,
        grid=(n,),
        in_specs=in_specs,
        out_specs=out_specs,
        out_shape=out_shape,
        scratch_shapes=[pltpu.VMEM(s, F32) for s in carries],
        compiler_params=_cparams("arbitrary"),
    )(*args)
    return list(res)


def _vjp_fn(fn, n_diff, n_out):
    def g(i, *a):
        ins, cts = a[:len(a) - n_out], a[len(a) - n_out:]
        diff, rest = ins[:n_diff], ins[n_diff:]
        _, pull = jax.vjp(lambda *d: tuple(fn(i, *d, *rest)), *diff)
        return tuple(pull(tuple(cts)))

    return g


def f_rms_x(i, x, gain):
    return (_rms(x, gain),)


def f_lat(i, cq, ckv, gq, gkv):
    return _rms(cq, gq), _rms(ckv, gkv)


def _rope_tables(pos, invf):
    ang = pos * invf
    lane = lax.broadcasted_iota(jnp.int32, (1, LANE), 1)
    cosv, sinv = jnp.cos(ang), jnp.sin(ang)
    half = ROPE // 2
    c = jnp.where(lane < ROPE, cosv, 0.0)
    sa = jnp.where(lane < half, -sinv, 0.0)
    sb = jnp.where((lane >= half) & (lane < ROPE), sinv, 0.0)
    return c, sa, sb


def _rope(xh, tabs):
    c, sa, sb = tabs
    half = ROPE // 2
    return xh * c + _roll(xh, LANE - half, 1) * sa + _roll(xh, half, 1) * sb


def f_head(i, q_raw, kv_raw, kr, qg, kg, pos, invf):
    tabs = _rope_tables(pos, invf)
    qs, ks, vs = [], [], []
    kr_ss = jnp.sum(kr * kr, axis=-1, keepdims=True)
    kr_rot = _rope(kr * kg[:, NOPE:], tabs)
    for h in range(HEADS):
        lo = q_raw[:, HEAD_PAD * h:HEAD_PAD * h + NOPE]
        hi = q_raw[:, HEAD_PAD * h + NOPE:HEAD_PAD * (h + 1)]
        ss = jnp.sum(lo * lo, axis=-1, keepdims=True) + jnp.sum(hi * hi, axis=-1, keepdims=True)
        r = lax.rsqrt(ss * (1.0 / QK) + EPS)
        qs.append(jnp.concatenate([lo * r * qg[:, :NOPE], _rope(hi * r * qg[:, NOPE:], tabs)], axis=1))
        lo = kv_raw[:, 2 * NOPE * h:2 * NOPE * h + NOPE]
        ss = jnp.sum(lo * lo, axis=-1, keepdims=True) + kr_ss
        r = lax.rsqrt(ss * (1.0 / QK) + EPS)
        ks.append(jnp.concatenate([lo * r * kg[:, :NOPE], kr_rot * r], axis=1))
        vs.append(kv_raw[:, 2 * NOPE * h + NOPE:2 * NOPE * (h + 1)])
    return jnp.stack(qs), jnp.stack(ks), jnp.stack(vs)


def f_mix(i, o_mla, mgate, o_gdn, ggate, og):
    parts = [o_mla * jax.nn.silu(mgate)]
    for h in range(HEADS):
        parts.append(_rms(o_gdn[h], og) * jax.nn.silu(ggate[:, LANE * h:LANE * (h + 1)]))
    return (jnp.concatenate(parts, axis=1),)


def _row(a, j):
    rows = lax.broadcasted_iota(jnp.int32, a.shape, 0)
    return jnp.sum(jnp.where(rows == j, a, 0.0), axis=0, keepdims=True)


def _shift_rows(x, halo, d):
    xs = _roll(x, d, 0)
    hs = _roll(halo, d, 0)
    r8 = lax.broadcasted_iota(jnp.int32, hs.shape, 0)
    top = jnp.where(r8 < d, hs, xs[:SUBLANE])
    return jnp.concatenate([top, xs[SUBLANE:]], axis=0)


def _conv_silu(x, halo, w):
    y = _row(w, CONV_W - 1) * x
    for j in range(CONV_W - 1):
        y = y + _row(w, j) * _shift_rows(x, halo, CONV_W - 1 - j)
    return jax.nn.silu(y)


def _head_select(offset):
    r = lax.broadcasted_iota(jnp.int32, (LANE, WIDTH), 0)
    c = lax.broadcasted_iota(jnp.int32, (LANE, WIDTH), 1)
    return (r == offset + lax.shift_right_logical(c, 7)).astype(_BF)


def _split3(x):
    x1 = x.astype(_BF)
    r1 = x - x1.astype(F32)
    x2 = r1.astype(_BF)
    return x1, x2, (r1 - x2.astype(F32)).astype(_BF)


@jax.custom_vjp
def _spread(x, sel):
    return _d_nn(jnp.concatenate(_split3(x), axis=1), jnp.concatenate([sel, sel, sel], axis=0))


def _spread_fwd(x, sel):
    return _spread(x, sel), sel


def _spread_bwd(sel, g):
    g1, g2, g3 = _split3(g)
    return _d_nt(g1, sel) + _d_nt(g2, sel) + _d_nt(g3, sel), jnp.zeros_like(sel)


_spread.defvjp(_spread_fwd, _spread_bwd)


def f_gdn_pre(i, gq, gk, gv, gab, hq, hk, hv, cwq, cwk, cwv, alog, dtb):
    live = jnp.where(i == 0, 0.0, 1.0)
    q = _conv_silu(gq, hq * live, cwq)
    k = _conv_silu(gk, hk * live, cwk)
    v = _conv_silu(gv, hv * live, cwv)
    g = _spread(-jnp.exp(alog) * jax.nn.softplus(gab + dtb), _head_select(0))
    beta = _spread(jax.nn.sigmoid(gab), _head_select(HEADS))
    qs, ks, vs, gs, bs = [], [], [], [], []
    for h in range(HEADS):
        sl = slice(LANE * h, LANE * (h + 1))
        qh, kh = q[:, sl], k[:, sl]
        qs.append(qh * lax.rsqrt(jnp.sum(qh * qh, axis=-1, keepdims=True) + EPS) * (GDN_DIM ** -0.5))
        ks.append(kh * lax.rsqrt(jnp.sum(kh * kh, axis=-1, keepdims=True) + EPS))
        vs.append(v[:, sl])
        gs.append(g[:, sl])
        bs.append(beta[:, sl])
    return jnp.stack(qs), jnp.stack(ks), jnp.stack(vs), jnp.stack(gs), jnp.stack(bs)


def gdn_pre_bwd(i, gq, gk, gv, gab, hq, hk, hv, dq, dk, dv, dg, db, cwq, cwk, cwv, alog, dtb, cq, ck, cv):
    grads = _vjp_fn(f_gdn_pre, 12, 5)(i, gq, gk, gv, gab, hq, hk, hv, cwq, cwk, cwv, alog, dtb, dq, dk, dv, dg, db)
    dgq, dgk, dgv, dgab, dhq, dhk, dhv, dcwq, dcwk, dcwv, dalog, ddtb = grads

    def add_tail(dx, carry):
        return jnp.concatenate([dx[:-SUBLANE], dx[-SUBLANE:] + carry], axis=0)

    return (add_tail(dgq, cq), add_tail(dgk, ck), add_tail(dgv, cv), dgab,
            dcwq, dcwk, dcwv, dalog, ddtb, dhq, dhk, dhv)


def _flash_tile(t_len):
    return min(512, t_len)


FLASH_HEADS = 4
FLASH_BWD_HEADS = 2
LOG2E = 1.4426950408889634


def _causal(rows0, shape):
    r = rows0 + lax.broadcasted_iota(jnp.int32, shape, 0)
    c = lax.broadcasted_iota(jnp.int32, shape, 1)
    return c <= r


def flash_fwd(q, k, v):
    h_n, t_len, _ = q.shape
    tq = _flash_tile(t_len)
    nq = t_len // tq
    hb = FLASH_HEADS
    kw = 2 if nq % 2 == 0 else 1
    tk = kw * tq
    c2 = (QK ** -0.5) * LOG2E
    pairs = [(i, j) for i in range(nq) for j in range(i // kw + 1)]
    qt = jnp.array([p[0] for p in pairs], jnp.int32)
    kt = jnp.array([p[1] for p in pairs], jnp.int32)

    def body(qt_ref, kt_ref, q_ref, k_ref, v_ref, o_ref, lse_ref, m_s, acc_s):
        step = pl.program_id(1)
        qi, kj = qt_ref[step], kt_ref[step]
        last = qi // kw

        @pl.when(kj == 0)
        def _():
            m_s[...] = jnp.full_like(m_s, -jnp.inf)
            acc_s[...] = jnp.zeros_like(acc_s)

        def tile(diagonal, cols):
            s = _bdot(q_ref[...], k_ref[:, :cols], _NT3, False) * c2
            if diagonal:
                s = jnp.where(_causal((qi % kw) * tq, (tq, cols))[None], s, -jnp.inf)
            m_old = m_s[...]
            m_new = jnp.maximum(m_old, jnp.max(s, axis=-1, keepdims=True))
            p = jnp.exp2(s - m_new).astype(_BF)
            v_ones = jnp.concatenate([v_ref[:, :cols], jnp.ones((hb, cols, LANE), _BF)], axis=2)
            acc_s[...] = jnp.exp2(m_old - m_new) * acc_s[...] + _bdot(p, v_ones, _NN3, False)
            m_s[...] = m_new

        @pl.when(kj < last)
        def _():
            tile(False, tk)

        @pl.when((kj == last) & (qi % kw == 0))
        def _():
            tile(True, tq)

        if kw > 1:
            @pl.when((kj == last) & (qi % kw != 0))
            def _():
                tile(True, tk)

        @pl.when(kj == last)
        def _():
            acc = acc_s[...]
            l_sum = acc[:, :, LANE:]
            o = acc[:, :, :LANE] / l_sum
            for hh in range(hb):
                o_ref[:, LANE * hh:LANE * (hh + 1)] = o[hh]
            lse_ref[...] = m_s[...] + jnp.log2(jnp.max(l_sum, axis=-1, keepdims=True))

    return pl.pallas_call(
        body,
        name="flash_fwd",
        grid_spec=pltpu.PrefetchScalarGridSpec(
            num_scalar_prefetch=2,
            grid=(h_n // hb, qt.shape[0]),
            in_specs=[
                pl.BlockSpec((hb, tq, HEAD_PAD), lambda h, s, qt_ref, kt_ref: (h, qt_ref[s], 0)),
                pl.BlockSpec((hb, tk, HEAD_PAD), lambda h, s, qt_ref, kt_ref: (h, kt_ref[s], 0)),
                pl.BlockSpec((hb, tk, LANE), lambda h, s, qt_ref, kt_ref: (h, kt_ref[s], 0)),
            ],
            out_specs=[
                pl.BlockSpec((tq, hb * LANE), lambda h, s, qt_ref, kt_ref: (qt_ref[s], h)),
                pl.BlockSpec((hb, tq, 1), lambda h, s, qt_ref, kt_ref: (h, qt_ref[s], 0)),
            ],
            scratch_shapes=[pltpu.VMEM((hb, tq, 1), F32), pltpu.VMEM((hb, tq, 2 * LANE), F32)],
        ),
        out_shape=[jax.ShapeDtypeStruct((t_len, h_n * LANE), F32), jax.ShapeDtypeStruct((h_n, t_len, 1), F32)],
        compiler_params=_cparams("parallel", "arbitrary"),
    )(qt, kt, q, k, v)


def flash_bwd(q, k, v, do, lse, delta):
    h_n, t_len, _ = q.shape
    tq = _flash_tile(t_len)
    nq = t_len // tq
    hb = FLASH_BWD_HEADS
    kw = 2 if nq % 2 == 0 else 1
    tk = kw * tq
    pairs = [(i, j) for j in range(nq // kw) for i in range(kw * j, nq)]
    n_steps = len(pairs)
    qt = jnp.array([p[0] for p in pairs], jnp.int32)
    kt = jnp.array([p[1] for p in pairs], jnp.int32)
    scale = QK ** -0.5
    c2 = scale * LOG2E

    def body(qt_ref, kt_ref, q_ref, k_ref, v_ref, do_ref, lse_ref, dl_ref, dq_hbm, dk_ref, dv_ref, dq_s, dq_sem):
        group, step = pl.program_id(0), pl.program_id(1)
        qi, kj = qt_ref[step], kt_ref[step]

        @pl.when(step == 0)
        def _():
            dq_s[...] = jnp.zeros_like(dq_s)

        def tile(diagonal, cols):
            qb, kb = q_ref[...], k_ref[:, :cols]
            dob = jnp.stack([do_ref[:, LANE * hh:LANE * (hh + 1)] for hh in range(hb)])
            p = jnp.exp2(_bdot(qb, kb, _NT3, False) * c2 - lse_ref[...])
            if diagonal:
                p = jnp.where(_causal((qi % kw) * tq, (tq, cols))[None], p, 0.0)
            dv = _bdot(p, dob, _TN3, False)
            ds = p * (_bdot(dob, v_ref[:, :cols], _NT3, False) - dl_ref[...]) * scale
            dk = _bdot(ds, qb, _TN3, False)
            dq_s[:, pl.ds(pl.multiple_of(qi * tq, tq), tq), :] += _bdot(ds, kb, _NN3, False)
            return dk, dv

        @pl.when(qi == kw * kj)
        def _():
            dk, dv = tile(True, tq)
            dk_ref[:, :tq], dv_ref[:, :tq] = dk, dv
            if kw > 1:
                dk_ref[:, tq:] = jnp.zeros((hb, tk - tq, HEAD_PAD), F32)
                dv_ref[:, tq:] = jnp.zeros((hb, tk - tq, LANE), F32)

        if kw > 1:
            @pl.when((qi != kw * kj) & (qi // kw == kj))
            def _():
                dk, dv = tile(True, tk)
                dk_ref[...] += dk
                dv_ref[...] += dv

        @pl.when(qi // kw > kj)
        def _():
            dk, dv = tile(False, tk)
            dk_ref[...] += dk
            dv_ref[...] += dv

        @pl.when(step == n_steps - 1)
        def _():
            out = pltpu.make_async_copy(dq_s, dq_hbm.at[pl.ds(group * hb, hb)], dq_sem)
            out.start()
            out.wait()

    def qmap(h, s, qt_ref, kt_ref):
        return (h, qt_ref[s], 0)

    def kmap(h, s, qt_ref, kt_ref):
        return (h, kt_ref[s], 0)

    return pl.pallas_call(
        body,
        name="flash_bwd",
        grid_spec=pltpu.PrefetchScalarGridSpec(
            num_scalar_prefetch=2,
            grid=(h_n // hb, n_steps),
            in_specs=[
                pl.BlockSpec((hb, tq, HEAD_PAD), qmap),
                pl.BlockSpec((hb, tk, HEAD_PAD), kmap),
                pl.BlockSpec((hb, tk, LANE), kmap),
                pl.BlockSpec((tq, hb * LANE), lambda h, s, qt_ref, kt_ref: (qt_ref[s], h)),
                pl.BlockSpec((hb, tq, 1), qmap),
                pl.BlockSpec((hb, tq, 1), qmap),
            ],
            out_specs=[
                pl.BlockSpec(memory_space=pl.ANY),
                pl.BlockSpec((hb, tk, HEAD_PAD), kmap),
                pl.BlockSpec((hb, tk, LANE), kmap),
            ],
            scratch_shapes=[pltpu.VMEM((hb, t_len, HEAD_PAD), F32), pltpu.SemaphoreType.DMA],
        ),
        out_shape=[
            jax.ShapeDtypeStruct((h_n, t_len, HEAD_PAD), F32),
            jax.ShapeDtypeStruct((h_n, t_len, HEAD_PAD), F32),
            jax.ShapeDtypeStruct((h_n, t_len, LANE), F32),
        ],
        compiler_params=_cparams("parallel", "arbitrary"),
    )(qt, kt, q, k, v, do, lse, delta)


def _tri_ones(h_n):
    ii = lax.broadcasted_iota(jnp.int32, (h_n, CHUNK, CHUNK), 1)
    jj = lax.broadcasted_iota(jnp.int32, (h_n, CHUNK, CHUNK), 2)
    return (ii >= jj).astype(_BF)


@jax.custom_vjp
def _chunk_cumsum(gb):
    tri = _tri_ones(gb.shape[0])
    return _bdot(jnp.concatenate([tri, tri, tri], axis=2), jnp.concatenate(_split3(gb), axis=1), _NN3, False)


def _chunk_cumsum_bwd(_, ct):
    tri = _tri_ones(ct.shape[0])
    return (_bdot(jnp.concatenate([tri, tri, tri], axis=1), jnp.concatenate(_split3(ct), axis=1), _TN3, False),)


_chunk_cumsum.defvjp(lambda gb: (_chunk_cumsum(gb), None), _chunk_cumsum_bwd)


@jax.custom_vjp
def _pair_diff(gcb):
    g1, g2, g3 = _split3(gcb)
    lane = lax.broadcasted_iota(jnp.int32, (1, 1, LANE), 2)
    one, zero = jnp.ones((), _BF), jnp.zeros((), _BF)
    a = jnp.where(lane == 0, g1, jnp.where(lane == 1, g2, jnp.where(lane == 2, g3, jnp.where(lane < 6, one, zero))))
    b = jnp.where(lane < 3, one, jnp.where(lane == 3, -g1, jnp.where(lane == 4, -g2, jnp.where(lane == 5, -g3, zero))))
    return _bdot(a, b, _NT3, False)


def _pair_diff_bwd(_, ct):
    parts = _split3(ct)
    ones = jnp.ones((ct.shape[0], 3 * CHUNK, LANE), _BF)
    rows = _bdot(jnp.concatenate(parts, axis=2), ones, _NN3, False)
    cols = _bdot(jnp.concatenate(parts, axis=1), ones, _TN3, False)
    lane = lax.broadcasted_iota(jnp.int32, (1, 1, LANE), 2)
    return (jnp.where(lane == 0, rows - cols, 0.0),)


_pair_diff.defvjp(lambda gcb: (_pair_diff(gcb), None), _pair_diff_bwd)


@jax.custom_vjp
def _saved_inverse(lmat, inv):
    return inv


def _saved_inverse_bwd(inv, g):
    return -_pdot(_pdot(inv, g, "tn"), inv, "nt"), jnp.zeros_like(inv)


_saved_inverse.defvjp(lambda lmat, inv: (inv, inv), _saved_inverse_bwd)


def gdn_step(s, q, k, v, gb, bb, inv_saved=None):
    c = CHUNK
    ii = lax.broadcasted_iota(jnp.int32, (1, c, c), 1)
    jj = lax.broadcasted_iota(jnp.int32, (1, c, c), 2)
    incl, strict = ii >= jj, ii > jj
    gcb = _chunk_cumsum(gb)
    diff = _pair_diff(gcb)
    decay = jnp.where(incl, jnp.exp(jnp.where(incl, diff, 0.0)), 0.0)
    kb, vb = k * bb, v * bb
    egc = jnp.exp(gcb)
    lmat = jnp.where(strict, _bmm_nt(kb, k) * decay, 0.0)
    if inv_saved is None:
        mm3 = _bh3_passes
        inv = (ii == jj).astype(F32) - lmat
        pw = mm3(lmat, lmat)
        for step in range(5):
            inv = inv + mm3(inv, pw)
            if step < 4:
                pw = mm3(pw, pw)
    else:
        mm3 = _bh3
        inv = _saved_inverse(lmat, inv_saved)
    u = mm3(inv, vb)
    w = mm3(inv, kb * egc)
    attn = _bmm_nt(q, k) * decay
    qd = q * egc
    g_end = jnp.sum(gb, axis=1, keepdims=True)
    kd = k * jnp.exp(g_end - gcb)
    v_new = u - _bmm(w, s)
    o = _bmm(qd, s) + _bmm(attn, v_new)
    s_new = s * jnp.exp(g_end) + _bmm_tn(kd, v_new)
    return s_new, o, inv


def gdn_fwd(q, k, v, gb, bb):
    h_n, t_len, d = q.shape
    n = t_len // CHUNK
    blk = pl.BlockSpec((h_n, CHUNK, d), lambda i: (0, i, 0))

    def body(q_ref, k_ref, v_ref, g_ref, b_ref, o_ref, sall_ref, inv_ref, s_s):
        @pl.when(pl.program_id(0) == 0)
        def _():
            s_s[...] = jnp.zeros_like(s_s)

        s = s_s[...]
        sall_ref[0] = s
        s_s[...], o_ref[...], inv_ref[0] = gdn_step(s, q_ref[...], k_ref[...], v_ref[...], g_ref[...], b_ref[...])

    return pl.pallas_call(
        body,
        name="gdn_fwd",
        grid=(n,),
        in_specs=[blk] * 5,
        out_specs=[blk, pl.BlockSpec((1, h_n, d, d), lambda i: (i, 0, 0, 0)),
                   pl.BlockSpec((1, h_n, CHUNK, CHUNK), lambda i: (i, 0, 0, 0))],
        out_shape=[jax.ShapeDtypeStruct((h_n, t_len, d), F32), jax.ShapeDtypeStruct((n, h_n, d, d), F32),
                   jax.ShapeDtypeStruct((n, h_n, CHUNK, CHUNK), F32)],
        scratch_shapes=[pltpu.VMEM((h_n, d, d), F32)],
        compiler_params=_cparams("arbitrary"),
    )(q, k, v, gb, bb)


def gdn_bwd(q, k, v, gb, bb, s_all, inv_all, do):
    h_n, t_len, d = q.shape
    n = t_len // CHUNK
    blk = pl.BlockSpec((h_n, CHUNK, d), lambda i: (0, n - 1 - i, 0))

    def body(q_ref, k_ref, v_ref, g_ref, b_ref, sall_ref, inv_ref, do_ref, dq_ref, dk_ref, dv_ref, dg_ref, db_ref,
             ds_s):
        @pl.when(pl.program_id(0) == 0)
        def _():
            ds_s[...] = jnp.zeros_like(ds_s)

        inv = inv_ref[0]
        _, pull = jax.vjp(lambda *a: gdn_step(*a, inv_saved=inv)[:2], sall_ref[0], q_ref[...], k_ref[...], v_ref[...],
                          g_ref[...], b_ref[...])
        ds_s[...], dq_ref[...], dk_ref[...], dv_ref[...], dg_ref[...], db_ref[...] = pull((ds_s[...], do_ref[...]))

    return pl.pallas_call(
        body,
        name="gdn_bwd",
        grid=(n,),
        in_specs=[blk] * 5 + [pl.BlockSpec((1, h_n, d, d), lambda i: (n - 1 - i, 0, 0, 0)),
                              pl.BlockSpec((1, h_n, CHUNK, CHUNK), lambda i: (n - 1 - i, 0, 0, 0)), blk],
        out_specs=[blk] * 5,
        out_shape=[jax.ShapeDtypeStruct((h_n, t_len, d), F32)] * 5,
        scratch_shapes=[pltpu.VMEM((h_n, d, d), F32)],
        compiler_params=_cparams("arbitrary"),
    )(q, k, v, gb, bb, s_all, inv_all, do)


def _pad_cols(a, n):
    return jnp.pad(a, ((0, 0), (0, n - a.shape[1])))


def arrange_w_in(w):
    pieces, start = [], 0
    for n in R_SPLITS:
        pieces.append(w[:, start:start + n])
        start += n
    cq, ckv, kr, mgate, gq, gk, gv, ga, gb, ggate = pieces
    return jnp.concatenate([mgate, gq, gk, gv, ggate, cq, ckv, _pad_cols(kr, LANE),
                            _pad_cols(jnp.concatenate([ga, gb], axis=1), LANE)], axis=1)


def unarrange_w_in(pieces):
    mgate, gq, gk, gv, ggate, rest = pieces

    def cols(start, n):
        return rest[:, start - P_CQ:start - P_CQ + n]
    return jnp.concatenate([cols(P_CQ, Q_LORA), cols(P_CKV, KV_LORA), cols(P_KR, ROPE), mgate, gq, gk, gv,
                            cols(P_GAB, HEADS), cols(P_GAB + HEADS, HEADS), ggate], axis=1)


def arrange_w_uq(w):
    w = w.reshape(w.shape[0], HEADS, QK)
    return jnp.pad(w, ((0, 0), (0, 0), (0, HEAD_PAD - QK))).reshape(w.shape[0], HEADS * HEAD_PAD)


def unarrange_w_uq(g):
    return g.reshape(g.shape[0], HEADS, HEAD_PAD)[:, :, :QK].reshape(g.shape[0], HEADS * QK)


def local_step(x, pos, tgt, p, on_early_grads=None, on_d_proj=None, on_weight_grads=None, on_d_xn=None,
               first_after=None, late_weights=None):
    t_len = x.shape[0]
    w_in = p["w_in"]
    norm_gain = p["norm_gain"].reshape(1, D_MODEL)
    qa_gain = p["mla_q_a_gain"].reshape(1, Q_LORA)
    kva_gain = p["mla_kv_a_gain"].reshape(1, KV_LORA)
    qg = _pad_cols(p["mla_q_norm_gain"].reshape(1, QK), HEAD_PAD)
    kg = _pad_cols(p["mla_k_norm_gain"].reshape(1, QK), HEAD_PAD)
    alog = _pad_cols(p["gdn_a_log"].reshape(1, HEADS), LANE)
    dtb = _pad_cols(p["gdn_dt_bias"].reshape(1, HEADS), LANE)
    og = p["gdn_out_norm_gain"].reshape(1, GDN_DIM)
    half = ROPE // 2
    inv_freq = jnp.power(ROPE_THETA, -jnp.arange(half, dtype=F32) / half)
    invf = _pad_cols(jnp.concatenate([inv_freq, inv_freq]).reshape(1, ROPE), LANE)

    rt = 256
    r = "r"
    first_after = jnp.zeros((SUBLANE, LANE), F32) if first_after is None else first_after
    (xn,) = rowwise("rms_x", lambda i, x_, gain_, after_: f_rms_x(i, x_, gain_), t_len, rt, [(x, (r, D_MODEL, 0))],
                    [norm_gain, first_after], [(r, D_MODEL, _BF)])
    proj = matmul("proj", xn, w_in, "nn")
    if late_weights is not None:
        p = {**p, **late_weights(proj)}
    w_uq, w_ukv, w_out = p["w_uq"], p["w_ukv"], p["w_out"]
    cw = p["gdn_conv_w"].reshape(CONV_W, 3 * WIDTH)
    cwq, cwk, cwv = cw[:, :WIDTH], cw[:, WIDTH:2 * WIDTH], cw[:, 2 * WIDTH:]
    cq_in = (proj, (r, Q_LORA, P_CQ // Q_LORA))
    ckv_in = (proj, (r, KV_LORA, P_CKV // KV_LORA))
    kr_in = (proj, (r, LANE, P_KR // LANE))
    mgate_in = (proj, (r, WIDTH, P_MGATE // WIDTH))
    ggate_in = (proj, (r, WIDTH, P_GGATE // WIDTH))
    gqkv_in = [(proj, (r, WIDTH, P_GQ // WIDTH)), (proj, (r, WIDTH, P_GK // WIDTH)), (proj, (r, WIDTH, P_GV // WIDTH))]
    gab_in = (proj, (r, LANE, P_GAB // LANE))
    halos = [(proj, ("halo", WIDTH, P_GQ // WIDTH)), (proj, ("halo", WIDTH, P_GK // WIDTH)),
             (proj, ("halo", WIDTH, P_GV // WIDTH))]

    q_lat, kv_lat = rowwise("lat", f_lat, t_len, rt, [cq_in, ckv_in], [qa_gain, kva_gain],
                            [(r, Q_LORA, _BF), (r, KV_LORA, _BF)])
    q_raw = matmul("q_up", q_lat, w_uq, "nn")
    kv_raw = matmul("kv_up", kv_lat, w_ukv, "nn")
    wide = HEADS * HEAD_PAD
    head_in = [(q_raw, (r, wide, 0)), (kv_raw, (r, wide, 0)), kr_in]
    pos_in = (pos, (r, 1, 0))
    q_full, k_full, v_mla = rowwise(
        "head", lambda i, qr, kvr, kr, ps, qg_, kg_, iv: f_head(i, qr, kvr, kr, qg_, kg_, ps, iv), t_len, rt,
        head_in + [pos_in], [qg, kg, invf],
        [("h", HEADS, HEAD_PAD, _BF), ("h", HEADS, HEAD_PAD, _BF), ("h", HEADS, LANE, _BF)])
    o_mla, lse = flash_fwd(q_full, k_full, v_mla)

    pre_in = gqkv_in + [gab_in] + halos
    pre_full = [cwq, cwk, cwv, alog, dtb]
    hkind = ("h", HEADS, GDN_DIM, F32)
    gq_n, gk_n, gv_n, g_b, b_b = rowwise("gdn_pre", f_gdn_pre, t_len, rt, pre_in, pre_full, [hkind] * 5)
    o_gdn, s_all, inv_all = gdn_fwd(gq_n, gk_n, gv_n, g_b, b_b)

    mix_in = [(o_mla, (r, WIDTH, 0)), mgate_in, (o_gdn, ("h",)), ggate_in]
    (mixed,) = rowwise("mix", f_mix, t_len, rt, mix_in, [og], [(r, 2 * WIDTH, _BF)])
    dy, dy_mx, loss_acc = out_proj_loss(mixed, w_out, x, tgt)
    loss = loss_acc[0, 0]

    d_mixed = matmul("d_mixed", dy_mx, w_out, "nt")
    g_w_out = matmul("g_w_out", mixed, dy_mx, "tn")

    def mix_bwd(i, o_mla_, mgate_, o_gdn_, ggate_, d_mixed_, og_):
        do_mla_, d_mgate_, do_gdn_, d_ggate_, g_og_ = _vjp_fn(f_mix, 5, 1)(i, o_mla_, mgate_, o_gdn_, ggate_, og_, d_mixed_)
        delta_ = jnp.stack([jnp.sum(o_mla_[:, LANE * h:LANE * (h + 1)] * do_mla_[:, LANE * h:LANE * (h + 1)],
                                    axis=-1, keepdims=True) for h in range(HEADS)])
        return do_mla_, d_mgate_, do_gdn_, d_ggate_, delta_, g_og_

    do_mla, d_mgate, do_gdn, d_ggate, delta, g_og = rowwise(
        "mix_bwd", mix_bwd, t_len, rt, mix_in + [(d_mixed, (r, 2 * WIDTH, 0))], [og],
        [(r, WIDTH, F32), (r, WIDTH, _BF), hkind, (r, WIDTH, _BF), ("h", HEADS, 1, F32)], [(1, GDN_DIM)])
    dq_n, dk_n, dv_n, dg_b, db_b = gdn_bwd(gq_n, gk_n, gv_n, g_b, b_b, s_all, inv_all, do_gdn)
    cts_in = [(a, ("h",)) for a in (dq_n, dk_n, dv_n, dg_b, db_b)]
    d_gq, d_gk, d_gv, d_gab, g_cwq, g_cwk, g_cwv, g_alog, g_dtb = rowwise(
        "gdn_pre_bwd", gdn_pre_bwd, t_len, rt, pre_in + cts_in, pre_full,
        [(r, WIDTH, _BF)] * 3 + [(r, LANE, _BF)],
        [(CONV_W, WIDTH)] * 3 + [(1, LANE)] * 2, carries=[(SUBLANE, WIDTH)] * 3, reverse=True)

    dq_full, dk_full, dv_mla = flash_bwd(q_full, k_full, v_mla, do_mla, lse, delta)
    head_cts = [(a, ("h",)) for a in (dq_full, dk_full, dv_mla)]

    def head_bwd(i, q_raw_, kv_raw_, kr_, pos_, dq_, dk_, dv_, qg_, kg_, invf_):
        return _vjp_fn(f_head, 5, 3)(i, q_raw_, kv_raw_, kr_, qg_, kg_, pos_, invf_, dq_, dk_, dv_)

    dq_raw, dkv_raw, d_kr, g_qg, g_kg = rowwise(
        "head_bwd", head_bwd, t_len, rt, head_in + [pos_in] + head_cts, [qg, kg, invf],
        [(r, wide, _BF), (r, wide, _BF), (r, LANE, _BF)], [(1, HEAD_PAD), (1, HEAD_PAD)])
    dq_lat = matmul("dq_lat", dq_raw, w_uq, "nt")
    g_w_uq = matmul("g_w_uq", q_lat, dq_raw, "tn")
    dkv_lat = matmul("dkv_lat", dkv_raw, w_ukv, "nt")
    g_w_ukv = matmul("g_w_ukv", kv_lat, dkv_raw, "tn")
    grads = {
        "w_uq": g_w_uq, "w_ukv": g_w_ukv, "gdn_conv_w": jnp.concatenate([g_cwq, g_cwk, g_cwv], axis=1),
        "w_out": g_w_out, "mla_q_norm_gain": g_qg[:, :QK], "mla_k_norm_gain": g_kg[:, :QK],
        "gdn_a_log": g_alog[:, :HEADS], "gdn_dt_bias": g_dtb[:, :HEADS], "gdn_out_norm_gain": g_og,
    }
    after = jnp.zeros((SUBLANE, LANE), F32) if on_early_grads is None else on_early_grads(grads)

    def lat_bwd(i, cq_, ckv_, dql_, dkl_, gq_, gkv_, after_):
        return _vjp_fn(f_lat, 4, 2)(i, cq_, ckv_, gq_, gkv_, dql_, dkl_)

    d_cq, d_ckv, grads["mla_q_a_gain"], grads["mla_kv_a_gain"] = rowwise(
        "lat_bwd", lat_bwd, t_len, rt, [cq_in, ckv_in, (dq_lat, (r, Q_LORA, 0)), (dkv_lat, (r, KV_LORA, 0))],
        [qa_gain, kva_gain, after], [(r, Q_LORA, _BF), (r, KV_LORA, _BF)], [(1, Q_LORA), (1, KV_LORA)])

    d_proj = [d_mgate, d_gq, d_gk, d_gv, d_ggate, jnp.concatenate([d_cq, d_ckv, d_kr, d_gab], axis=1)]
    after = None if on_d_proj is None else on_d_proj(d_proj[-1])
    grads["w_in"] = [matmul("g_w_in_%d" % j, xn, piece, "tn", after=after if j == 0 else None)
                     for j, piece in enumerate(d_proj)]
    after = None if on_weight_grads is None else on_weight_grads(grads)
    d_xn = matmul_pieces("d_xn", d_proj, w_in, after=after)
    after = jnp.zeros((SUBLANE, LANE), F32) if on_d_xn is None else on_d_xn(d_xn)

    def rms_x_bwd(i, x_, dxn_, dy_, gain_, after_):
        dx, dgain = _vjp_fn(f_rms_x, 2, 1)(i, x_, gain_, dxn_)
        return dx + dy_, dgain

    grad_x, grads["norm_gain"] = rowwise(
        "rms_x_bwd", rms_x_bwd, t_len, rt, [(x, (r, D_MODEL, 0)), (d_xn, (r, D_MODEL, 0)), (dy, (r, D_MODEL, 0))],
        [norm_gain, after], [(r, D_MODEL, F32)], [(1, D_MODEL)])
    return loss, grad_x, grads


MESH = pl.DeviceIdType.MESH
ANY = pl.BlockSpec(memory_space=pl.ANY)
CHIP_FLIPS = ((1, 0), (0, 1), (1, 1))


def _place():
    return lax.axis_index("x"), lax.axis_index("y"), lax.axis_index("c")


def _flip(v, f):
    return 1 - v if f else v


def all_gather(shards):
    n_arr = len(shards)

    def body(*refs):
        x_refs, o_refs = refs[:n_arr], refs[n_arr:2 * n_arr]
        send_sems, recv_sems, local_sems = refs[2 * n_arr:]
        x, y, c = _place()
        me, sibling = (x, y, c), (x, y, 1 - c)
        chips = [(_flip(x, fx), _flip(y, fy)) for fx, fy in CHIP_FLIPS]

        def copy(a, k, block, to, src=None):
            px, py, pc = block
            dst = o_refs[a].at[4 * px + 2 * py + pc]
            return pltpu.make_async_remote_copy(
                src_ref=dst if src is None else src, dst_ref=dst, send_sem=send_sems.at[a, k],
                recv_sem=recv_sems.at[a, k], device_id=to, device_id_type=MESH)

        mine, first, passed = [], [], []
        for a in range(n_arr):
            cp = pltpu.make_async_copy(x_refs[a], o_refs[a].at[4 * x + 2 * y + c], local_sems.at[a])
            cp.start()
            mine.append(cp)
            first.append(copy(a, 0, me, sibling, src=x_refs[a]))
            first += [copy(a, 1 + j, me, (*chip, c), src=x_refs[a]) for j, chip in enumerate(chips)]
        for cp in first:
            cp.start()
        for j, chip in enumerate(chips):
            for a in range(n_arr):
                copy(a, 1 + j, (*chip, c), me).wait_recv()
                cp = copy(a, 4 + j, (*chip, c), sibling)
                cp.start()
                passed.append(cp)
        for a in range(n_arr):
            copy(a, 0, sibling, me).wait_recv()
            for j, chip in enumerate(chips):
                copy(a, 4 + j, (*chip, 1 - c), me).wait_recv()
        for cp in first + passed:
            cp.wait_send()
        for cp in mine:
            cp.wait()

    return pl.pallas_call(
        body,
        name="all_gather",
        out_shape=[jax.ShapeDtypeStruct((N_DEV,) + s.shape, s.dtype) for s in shards],
        in_specs=[ANY] * n_arr,
        out_specs=[ANY] * n_arr,
        scratch_shapes=[pltpu.SemaphoreType.DMA((n_arr, 7)), pltpu.SemaphoreType.DMA((n_arr, 7)),
                        pltpu.SemaphoreType.DMA((n_arr,))],
    )(*shards)


HBM = pl.BlockSpec(memory_space=pltpu.HBM)
SEMS = pl.BlockSpec(memory_space=pltpu.SEMAPHORE)
SIDE_EFFECT = pltpu.SideEffectType.DATAFLOW_SIDE_EFFECTING


def core_routes(x, y, c):
    return [(2 * q + (1 - c), q, (x, y, 1 - c)) for q in range(4)]


def chip_routes(x, y, c):
    routes = []
    for j, (fx, fy) in enumerate(CHIP_FLIPS):
        px, py = _flip(x, fx), _flip(y, fy)
        routes.append((2 * px + py, j, (px, py, c)))
    return routes


def _route_copies(routes, n_routes, src_refs, land_refs, sems):
    x, y, c = _place()
    n_copies = len(src_refs) * n_routes
    return [pltpu.make_async_remote_copy(src_ref=src.at[s], dst_ref=land.at[d], send_sem=sems[a * n_routes + k],
                                         recv_sem=sems[n_copies + a * n_routes + k], device_id=dev,
                                         device_id_type=MESH)
            for a, (src, land) in enumerate(zip(src_refs, land_refs)) for k, (s, d, dev) in enumerate(routes(x, y, c))]


def gather_routes(x, y, c):
    me = 4 * x + 2 * y + c
    return [(0, me, (_flip(x, (k >> 2) & 1), _flip(y, (k >> 1) & 1), _flip(c, k & 1))) for k in range(1, N_DEV)]


def exchange_start(name, routes, n_routes, srcs, n_slots=None):
    n = len(srcs)
    n_sems = 2 * n * n_routes
    lands = [lax.empty((n_routes if n_slots is None else n_slots,) + s.shape[1:], s.dtype) for s in srcs]

    def body(*refs):
        for cp in _route_copies(routes, n_routes, refs[:n], refs[n:2 * n], refs[2 * n:2 * n + n_sems]):
            cp.start()
        refs[-1][...] = jnp.zeros_like(refs[-1])

    res = pl.pallas_call(
        body,
        name=name,
        out_shape=(*[pltpu.SemaphoreType.DMA(())] * n_sems, *[pltpu.HBM(a.shape, a.dtype) for a in srcs + lands],
                   jax.ShapeDtypeStruct((SUBLANE, LANE), F32)),
        in_specs=[HBM] * (2 * n),
        out_specs=(*[SEMS] * n_sems, *[HBM] * (2 * n), pl.BlockSpec(memory_space=pltpu.VMEM)),
        input_output_aliases={i: n_sems + i for i in range(2 * n)},
        compiler_params=pltpu.CompilerParams(has_side_effects=SIDE_EFFECT),
    )(*[pltpu.with_memory_space_constraint(a, pltpu.HBM) for a in srcs + lands])
    return (res[:n_sems], res[n_sems:-1]), res[-1]


def exchange_wait(name, routes, handle, after):
    sems, thru = handle
    n, n_sems = len(thru) // 2, len(sems)
    n_routes = n_sems // (2 * n)

    def body(*refs):
        for cp in _route_copies(routes, n_routes, refs[:n], refs[n:2 * n], refs[2 * n:2 * n + n_sems]):
            cp.wait_send()
            cp.wait_recv()

    res = pl.pallas_call(
        body,
        name=name,
        out_shape=tuple(pltpu.HBM(a.shape, a.dtype) for a in thru),
        in_specs=[HBM] * (2 * n) + [SEMS] * n_sems + [ANY],
        out_specs=tuple([HBM] * (2 * n)),
        input_output_aliases={i: i for i in range(2 * n)},
        compiler_params=pltpu.CompilerParams(has_side_effects=SIDE_EFFECT),
    )(*thru, *sems, after)
    return list(res[:n]), list(res[n:])


def gather_small(v):
    def body(v_ref, o_ref, send_sems, recv_sems, local_sem):
        x, y, c = _place()
        me = 4 * x + 2 * y + c
        mine = pltpu.make_async_copy(v_ref, o_ref.at[me], local_sem)
        mine.start()
        copies = []
        for k in range(1, N_DEV):
            fx, fy, fc = (k >> 2) & 1, (k >> 1) & 1, k & 1
            cp = pltpu.make_async_remote_copy(
                src_ref=v_ref, dst_ref=o_ref.at[me], send_sem=send_sems.at[k - 1], recv_sem=recv_sems.at[k - 1],
                device_id=(_flip(x, fx), _flip(y, fy), _flip(c, fc)), device_id_type=MESH)
            cp.start()
            copies.append(cp)
        for cp in copies:
            cp.wait()
        mine.wait()

    return pl.pallas_call(
        body,
        name="gather_small",
        out_shape=jax.ShapeDtypeStruct((N_DEV,) + v.shape, v.dtype),
        in_specs=[ANY],
        out_specs=ANY,
        scratch_shapes=[pltpu.SemaphoreType.DMA((N_DEV - 1,)), pltpu.SemaphoreType.DMA((N_DEV - 1,)),
                        pltpu.SemaphoreType.DMA],
    )(v)


def _row_tile(rows):
    for t in (256, 128, 64, 32, 16, 8):
        if rows % t == 0:
            return t
    return rows


def add_core_parts(name, g, recv, c_idx, wire):
    _, rows, cols = g.shape
    tr = _row_tile(rows)

    def body(c_ref, g_ref, r_ref, o_ref, w_ref):
        part = g_ref[...] + r_ref[...]
        o_ref[...] = part
        w_ref[...] = part.astype(w_ref.dtype)

    blk = pl.BlockSpec((1, tr, cols), lambda q, i, c_ref: (q, i, 0))
    return pl.pallas_call(
        body,
        name=name,
        grid_spec=pltpu.PrefetchScalarGridSpec(
            num_scalar_prefetch=1,
            grid=(4, rows // tr),
            in_specs=[pl.BlockSpec((1, tr, cols), lambda q, i, c_ref: (2 * q + c_ref[0], i, 0)), blk],
            out_specs=[blk, blk],
        ),
        out_shape=[jax.ShapeDtypeStruct((4, rows, cols), F32), jax.ShapeDtypeStruct((4, rows, cols), wire)],
        compiler_params=_cparams("parallel", "parallel"),
    )(c_idx, g, recv)


def _adamw(w, g, m, v):
    m = ADAM_B1 * m + (1.0 - ADAM_B1) * g
    v = ADAM_B2 * v + (1.0 - ADAM_B2) * (g * g)
    m_hat = m / (1.0 - ADAM_B1 ** ADAM_STEP)
    v_hat = v / (1.0 - ADAM_B2 ** ADAM_STEP)
    delta = -ADAM_LR * (m_hat / (jnp.sqrt(v_hat) + ADAM_EPS) + ADAM_WD * w)
    return delta, m, v


def adamw_sharded(name, parts, recv, q_idx, w, m, v):
    rows, cols = w.shape
    tr = _row_tile(rows)

    def body(q_ref, p_ref, r_ref, w_ref, m_ref, v_ref, g_out, d_out, m_out, v_out):
        g = p_ref[0] + r_ref[0].astype(F32) + r_ref[1].astype(F32) + r_ref[2].astype(F32)
        d, m_new, v_new = _adamw(w_ref[...], g, m_ref[...], v_ref[...])
        g_out[...], d_out[...], m_out[...], v_out[...] = g, d, m_new, v_new

    blk = pl.BlockSpec((tr, cols), lambda i, q_ref: (i, 0))
    return pl.pallas_call(
        body,
        name=name,
        grid_spec=pltpu.PrefetchScalarGridSpec(
            num_scalar_prefetch=1,
            grid=(rows // tr,),
            in_specs=[pl.BlockSpec((1, tr, cols), lambda i, q_ref: (q_ref[0], i, 0)),
                      pl.BlockSpec((3, tr, cols), lambda i, q_ref: (0, i, 0)), blk, blk, blk],
            out_specs=[blk] * 4,
        ),
        out_shape=[jax.ShapeDtypeStruct((rows, cols), F32)] * 4,
        compiler_params=_cparams("parallel"),
    )(q_idx, parts, recv, w, m, v)


def adamw_small(gathered, w, m, v):
    def body(g_ref, w_ref, m_ref, v_ref, g_out, d_out, m_out, v_out):
        g = g_ref[0]
        for j in range(1, N_DEV):
            g = g + g_ref[j]
        d, m_new, v_new = _adamw(w_ref[...], g, m_ref[...], v_ref[...])
        g_out[...], d_out[...], m_out[...], v_out[...] = g, d, m_new, v_new

    return pl.pallas_call(body, name="adamw_small", out_shape=[jax.ShapeDtypeStruct(w.shape, F32)] * 4)(gathered, w, m, v)


SHARDED = ("w_in", "w_uq", "w_ukv", "gdn_conv_w", "w_out")
SMALL = (("norm_gain", D_MODEL), ("mla_q_a_gain", Q_LORA), ("mla_kv_a_gain", KV_LORA), ("mla_q_norm_gain", QK),
         ("mla_k_norm_gain", QK), ("gdn_a_log", HEADS), ("gdn_dt_bias", HEADS), ("gdn_out_norm_gain", GDN_DIM))
WEIGHT_ORDER = ("norm_gain", "w_in", "mla_q_a_gain", "mla_kv_a_gain", "w_uq", "w_ukv", "mla_q_norm_gain",
                "mla_k_norm_gain", "gdn_conv_w", "gdn_a_log", "gdn_dt_bias", "gdn_out_norm_gain", "w_out")


def _pack_small(d):
    rows = []
    for name, n in SMALL:
        a = d[name].reshape(-1).astype(F32)
        n_pad = -(-n // LANE) * LANE
        rows.append(jnp.pad(a, (0, n_pad - n)).reshape(n_pad // LANE, LANE))
    packed = jnp.concatenate(rows, axis=0)
    return jnp.pad(packed, ((0, -packed.shape[0] % SUBLANE), (0, 0)))


def _unpack_small(packed):
    out, row = {}, 0
    for name, n in SMALL:
        n_rows = -(-n // LANE)
        out[name] = packed[row:row + n_rows].reshape(-1)[:n].reshape(1, n)
        row += n_rows
    return out


def kernel(x, positions, norm_gain, w_in, mla_q_a_gain, mla_kv_a_gain, w_uq, w_ukv, mla_q_norm_gain, mla_k_norm_gain, gdn_conv_w, gdn_a_log, gdn_dt_bias, gdn_out_norm_gain, w_out, loss_target, m_norm_gain, m_w_in, m_mla_q_a_gain, m_mla_kv_a_gain, m_w_uq, m_w_ukv, m_mla_q_norm_gain, m_mla_k_norm_gain, m_gdn_conv_w, m_gdn_a_log, m_gdn_dt_bias, m_gdn_out_norm_gain, m_w_out, v_norm_gain, v_w_in, v_mla_q_a_gain, v_mla_kv_a_gain, v_w_uq, v_w_ukv, v_mla_q_norm_gain, v_mla_k_norm_gain, v_gdn_conv_w, v_gdn_a_log, v_gdn_dt_bias, v_gdn_out_norm_gain, v_w_out):
    w = dict(norm_gain=norm_gain, w_in=w_in, mla_q_a_gain=mla_q_a_gain, mla_kv_a_gain=mla_kv_a_gain, w_uq=w_uq,
             w_ukv=w_ukv, mla_q_norm_gain=mla_q_norm_gain, mla_k_norm_gain=mla_k_norm_gain, gdn_conv_w=gdn_conv_w,
             gdn_a_log=gdn_a_log, gdn_dt_bias=gdn_dt_bias, gdn_out_norm_gain=gdn_out_norm_gain, w_out=w_out)
    m = dict(norm_gain=m_norm_gain, w_in=m_w_in, mla_q_a_gain=m_mla_q_a_gain, mla_kv_a_gain=m_mla_kv_a_gain,
             w_uq=m_w_uq, w_ukv=m_w_ukv, mla_q_norm_gain=m_mla_q_norm_gain, mla_k_norm_gain=m_mla_k_norm_gain,
             gdn_conv_w=m_gdn_conv_w, gdn_a_log=m_gdn_a_log, gdn_dt_bias=m_gdn_dt_bias,
             gdn_out_norm_gain=m_gdn_out_norm_gain, w_out=m_w_out)
    v = dict(norm_gain=v_norm_gain, w_in=v_w_in, mla_q_a_gain=v_mla_q_a_gain, mla_kv_a_gain=v_mla_kv_a_gain,
             w_uq=v_w_uq, w_ukv=v_w_ukv, mla_q_norm_gain=v_mla_q_norm_gain, mla_k_norm_gain=v_mla_k_norm_gain,
             gdn_conv_w=v_gdn_conv_w, gdn_a_log=v_gdn_a_log, gdn_dt_bias=v_gdn_dt_bias,
             gdn_out_norm_gain=v_gdn_out_norm_gain, w_out=v_w_out)
    t_len = x.shape[1]

    shards = [w[n][0] if n == "gdn_conv_w" else w[n][0].astype(_BF) for n in SHARDED]
    xi, yi, ci = _place()
    c_idx = jnp.reshape(ci, (1,)).astype(jnp.int32)
    q_idx = jnp.reshape(2 * xi + yi, (1,)).astype(jnp.int32)
    flight = {}

    def cols_whole(g):
        return g.transpose(1, 0, 2).reshape(g.shape[1], N_DEV * g.shape[2])

    def col_blocks(g):
        return g.reshape(g.shape[0], N_DEV, g.shape[1] // N_DEV).transpose(1, 0, 2)

    (a_w_in,) = all_gather(shards[:1])
    p = {n: w[n] for n, _ in SMALL}
    p["w_in"] = arrange_w_in(cols_whole(a_w_in))
    flight["weights"], weights_token = exchange_start(
        "gather_start", gather_routes, N_DEV - 1, [s[None] for s in shards[1:]], n_slots=N_DEV)

    def late_weights(proj):
        _, landed = exchange_wait("gather_wait", gather_routes, flight["weights"], proj)
        me = 4 * xi + 2 * yi + ci
        a_w_uq, a_w_ukv, a_cw, a_w_out = [lax.dynamic_update_slice(land, s[None], (me, 0, 0))
                                          for land, s in zip(landed, shards[1:])]
        return {"w_uq": arrange_w_uq(cols_whole(a_w_uq)), "w_ukv": cols_whole(a_w_ukv),
                "gdn_conv_w": cols_whole(a_cw), "w_out": a_w_out.reshape(N_DEV * a_w_out.shape[1], a_w_out.shape[2])}

    early, parts = SHARDED[1:], {}

    def core_stage(tag, names, blocks):
        flight["cores" + tag], token = exchange_start("cores_start" + tag, core_routes, 4, blocks)
        flight["names" + tag] = names
        return token

    def chip_stage(tag, after):
        blocks, landed = exchange_wait("cores_wait" + tag, core_routes, flight["cores" + tag], after)
        for n, g, r in zip(flight["names" + tag], blocks, landed):
            parts[n] = add_core_parts("add_" + n, g, r, c_idx, F32 if n == "gdn_conv_w" else _BF)
        wires = [parts[n][1] for n in flight["names" + tag]]
        flight["chips" + tag], token = exchange_start("chips_start" + tag, chip_routes, 3, wires)
        return token

    def on_early_grads(grads):
        return core_stage("_early", early, [
            col_blocks(unarrange_w_uq(grads["w_uq"])), col_blocks(grads["w_ukv"]), col_blocks(grads["gdn_conv_w"]),
            grads["w_out"].reshape(N_DEV, D_MODEL // N_DEV, D_MODEL)])

    def on_d_proj(d_proj):
        return chip_stage("_early", d_proj)

    def on_weight_grads(grads):
        return core_stage("", SHARDED[:1], [col_blocks(unarrange_w_in(grads["w_in"]))])

    def on_d_xn(d_xn):
        return chip_stage("", d_xn)

    pos = positions.reshape(t_len, 1).astype(F32)
    loss, grad_x, grads = local_step(x.reshape(t_len, D_MODEL), pos, loss_target.reshape(t_len, D_MODEL), p,
                                     on_early_grads=on_early_grads, on_d_proj=on_d_proj,
                                     on_weight_grads=on_weight_grads, on_d_xn=on_d_xn,
                                     first_after=weights_token, late_weights=late_weights)
    loss = lax.psum(loss, ("x", "y", "c"))
    out = {}
    small_all = gather_small(_pack_small(grads))
    res = adamw_small(small_all, _pack_small(w), _pack_small(m), _pack_small(v))
    unpacked = [_unpack_small(a) for a in res]
    for n, _ in SMALL:
        out[n] = [u[n] for u in unpacked]

    _, from_chips_early = exchange_wait("chips_wait_early", chip_routes, flight["chips_early"], res[0])
    _, from_chips = exchange_wait("chips_wait", chip_routes, flight["chips"], res[0])
    for n, rcv in zip(SHARDED, from_chips + from_chips_early):
        prt = parts[n][0]
        shape = w[n].shape
        res = adamw_sharded("adamw_" + n, prt, rcv, q_idx, w[n].reshape(shape[-2:]), m[n].reshape(shape[-2:]),
                            v[n].reshape(shape[-2:]))
        out[n] = [a.reshape(shape) for a in res]

    return (loss, grad_x.reshape(x.shape), *[out[n][0] for n in WEIGHT_ORDER], *[out[n][1] for n in WEIGHT_ORDER],
            *[out[n][2] for n in WEIGHT_ORDER], *[out[n][3] for n in WEIGHT_ORDER])
```

```python
import functools

import jax
import jax.numpy as jnp
from jax import lax
from jax.experimental import pallas as pl
from jax.experimental.pallas import tpu as pltpu

F32 = jnp.float32
_BF = jnp.bfloat16

D_MODEL = 2048
HEADS = 8
NOPE = 128
ROPE = 64
QK = NOPE + ROPE
Q_LORA = 512
KV_LORA = 256
HEAD_PAD = 256
GDN_DIM = 128
WIDTH = HEADS * 128
CONV_W = 4
CHUNK = 64
ROPE_THETA = 10000.0
EPS = 1e-6
N_DEV = 8
LANE = 128
SUBLANE = 8
VMEM_LIMIT = 48 * 1024 * 1024

ADAM_LR, ADAM_B1, ADAM_B2, ADAM_EPS, ADAM_WD, ADAM_STEP = 0.001, 0.9, 0.999, 1e-08, 0.01, 10

P_MGATE, P_GQ, P_GK, P_GV, P_GGATE = 0, 1024, 2048, 3072, 4096
P_CQ, P_CKV, P_KR, P_GAB = 5120, 5632, 5888, 6016
R_SPLITS = (512, 256, 64, 1024, 1024, 1024, 1024, 8, 8, 1024)


def _cparams(*sem):
    return pltpu.CompilerParams(dimension_semantics=sem, vmem_limit_bytes=VMEM_LIMIT)


def _d_nn(a, b):
    return jnp.dot(a.astype(_BF), b.astype(_BF), preferred_element_type=F32)


def _d_nt(a, b):
    return lax.dot_general(a.astype(_BF), b.astype(_BF), (((1,), (1,)), ((), ())), preferred_element_type=F32)


def _d_tn(a, b):
    return lax.dot_general(a.astype(_BF), b.astype(_BF), (((0,), (0,)), ((), ())), preferred_element_type=F32)


_NN3 = (((2,), (1,)), ((0,), (0,)))
_NT3 = (((2,), (2,)), ((0,), (0,)))
_TN3 = (((1,), (1,)), ((0,), (0,)))


def _bdot(a, b, dims, hi):
    if hi:
        return lax.dot_general(a, b, dims, preferred_element_type=F32, precision=hi)
    return lax.dot_general(a.astype(_BF), b.astype(_BF), dims, preferred_element_type=F32)


def _batched_matmuls(hi):
    nn = jax.custom_vjp(lambda a, b: _bdot(a, b, _NN3, hi))
    nt = jax.custom_vjp(lambda a, b: _bdot(a, b, _NT3, hi))
    tn = jax.custom_vjp(lambda a, b: _bdot(a, b, _TN3, hi))
    nn.defvjp(lambda a, b: (_bdot(a, b, _NN3, hi), (a, b)),
              lambda r, g: (_bdot(g, r[1], _NT3, hi), _bdot(r[0], g, _TN3, hi)))
    nt.defvjp(lambda a, b: (_bdot(a, b, _NT3, hi), (a, b)),
              lambda r, g: (_bdot(g, r[1], _NN3, hi), _bdot(g, r[0], _TN3, hi)))
    tn.defvjp(lambda a, b: (_bdot(a, b, _TN3, hi), (a, b)),
              lambda r, g: (_bdot(r[1], g, _NT3, hi), _bdot(r[0], g, _NN3, hi)))
    return nn, nt, tn


_bmm, _bmm_nt, _bmm_tn = _batched_matmuls(False)


def _split2(x):
    hi = x.astype(_BF)
    return hi, (x - hi.astype(F32)).astype(_BF)


def _pdot(a, b, mode):
    (a_hi, a_lo), (b_hi, b_lo) = _split2(a), _split2(b)
    a_ax, b_ax, dims = {"nn": (2, 1, _NN3), "nt": (2, 2, _NT3), "tn": (1, 1, _TN3)}[mode]
    lhs = jnp.concatenate([a_hi, a_lo, a_hi], axis=a_ax)
    rhs = jnp.concatenate([b_hi, b_hi, b_lo], axis=b_ax)
    return lax.dot_general(lhs, rhs, dims, preferred_element_type=F32)


def _packed_matmuls():
    nn = jax.custom_vjp(lambda a, b: _pdot(a, b, "nn"))
    nn.defvjp(lambda a, b: (_pdot(a, b, "nn"), (a, b)), lambda r, g: (_pdot(g, r[1], "nt"), _pdot(r[0], g, "tn")))
    return nn


_bh3 = _packed_matmuls()
_bh3_passes = _batched_matmuls(lax.Precision.HIGH)[0]


@functools.partial(jax.custom_vjp, nondiff_argnums=(1, 2))
def _roll(x, shift, axis):
    return pltpu.roll(x, shift, axis)


def _roll_fwd(x, shift, axis):
    return pltpu.roll(x, shift, axis), None


def _roll_bwd(shift, axis, _, g):
    n = g.shape[axis]
    return (pltpu.roll(g, (n - shift) % n, axis),)


_roll.defvjp(_roll_fwd, _roll_bwd)


def _rms(x, gain):
    return x * lax.rsqrt(jnp.mean(x * x, axis=-1, keepdims=True) + EPS) * gain


MM_TILE = 1024
MM_DEPTH = 2048


def matmul(name, a, b, mode, after=None):
    if mode == "nn":
        (m, k), (k2, n) = a.shape, b.shape
    elif mode == "nt":
        (m, k), (n, k2) = a.shape, b.shape
    else:
        (k, m), (k2, n) = a.shape, b.shape
    assert k == k2, (name, a.shape, b.shape)
    tm, tn, tk = min(MM_TILE, m), min(MM_TILE, n), min(MM_DEPTH, k)
    assert m % tm == 0 and n % tn == 0 and k % tk == 0, (name, m, n, k)
    dot = {"nn": _d_nn, "nt": _d_nt, "tn": _d_tn}[mode]

    def body(a_ref, b_ref, *rest):
        o_ref = rest[-1]
        kk = pl.program_id(2)
        part = dot(a_ref[...], b_ref[...])

        @pl.when(kk == 0)
        def _():
            o_ref[...] = part

        @pl.when(kk != 0)
        def _():
            o_ref[...] += part

    if mode == "nn":
        a_spec = pl.BlockSpec((tm, tk), lambda j, i, kk: (i, kk))
        b_spec = pl.BlockSpec((tk, tn), lambda j, i, kk: (kk, j))
    elif mode == "nt":
        a_spec = pl.BlockSpec((tm, tk), lambda j, i, kk: (i, kk))
        b_spec = pl.BlockSpec((tn, tk), lambda j, i, kk: (j, kk))
    else:
        a_spec = pl.BlockSpec((tk, tm), lambda j, i, kk: (kk, i))
        b_spec = pl.BlockSpec((tk, tn), lambda j, i, kk: (kk, j))
    return pl.pallas_call(
        body,
        name=name,
        grid=(n // tn, m // tm, k // tk),
        in_specs=[a_spec, b_spec] + ([] if after is None else [pl.BlockSpec(memory_space=pl.ANY)]),
        out_specs=pl.BlockSpec((tm, tn), lambda j, i, kk: (i, j)),
        out_shape=jax.ShapeDtypeStruct((m, n), F32),
        compiler_params=_cparams("parallel", "parallel", "arbitrary"),
    )(*((a, b) if after is None else (a, b, after)))


def matmul_pieces(name, pieces, b, after=None):
    n_p, (t_len, width), n = len(pieces), pieces[0].shape, b.shape[0]
    assert width == MM_TILE and n_p % 2 == 0 and all(p.shape == pieces[0].shape for p in pieces)
    tile, tn = min(MM_TILE, t_len), min(MM_TILE, n)
    extra = [] if after is None else [after]

    def body(*refs):
        b_ref, o_ref, kk = refs[n_p], refs[-1], pl.program_id(2)
        for s in range(n_p // 2):
            @pl.when(kk == s)
            def _(s=s):
                part = _d_nt(refs[2 * s][...], b_ref[:, :width]) + _d_nt(refs[2 * s + 1][...], b_ref[:, width:])
                if s == 0:
                    o_ref[...] = part
                else:
                    o_ref[...] += part

    return pl.pallas_call(
        body,
        name=name,
        grid=(n // tn, t_len // tile, n_p // 2),
        in_specs=[pl.BlockSpec((tile, width), lambda j, i, kk: (i, 0))] * n_p
        + [pl.BlockSpec((tn, 2 * width), lambda j, i, kk: (j, kk))] + [pl.BlockSpec(memory_space=pl.ANY)] * len(extra),
        out_specs=pl.BlockSpec((tile, tn), lambda j, i, kk: (i, j)),
        out_shape=jax.ShapeDtypeStruct((t_len, n), F32),
        compiler_params=_cparams("parallel", "parallel", "arbitrary"),
    )(*pieces, b, *extra)


def out_proj_loss(mixed, w_out, x, tgt):
    (m, k), n = mixed.shape, w_out.shape[1]
    tm, tn = min(MM_TILE // 2, m), min(MM_TILE, n)
    assert m % tm == 0 and n % tn == 0

    def body(a_ref, b_ref, x_ref, t_ref, dy_ref, dy_mx_ref, loss_ref):
        first = (pl.program_id(0) == 0) & (pl.program_id(1) == 0)
        e = x_ref[...] + _d_nn(a_ref[...], b_ref[...]) - t_ref[...]
        dy = e * (1.0 / D_MODEL)
        dy_ref[...] = dy
        dy_mx_ref[...] = dy.astype(dy_mx_ref.dtype)
        part = jnp.zeros((SUBLANE, LANE), F32) + 0.5 * jnp.sum(e * e) * (1.0 / D_MODEL)

        @pl.when(first)
        def _():
            loss_ref[...] = part

        @pl.when(jnp.logical_not(first))
        def _():
            loss_ref[...] += part

    tile = pl.BlockSpec((tm, tn), lambda j, i: (i, j))
    return pl.pallas_call(
        body,
        name="out_proj_loss",
        grid=(n // tn, m // tm),
        in_specs=[pl.BlockSpec((tm, k), lambda j, i: (i, 0)), pl.BlockSpec((k, tn), lambda j, i: (0, j)), tile, tile],
        out_specs=[tile, tile, pl.BlockSpec((SUBLANE, LANE), lambda j, i: (0, 0))],
        out_shape=[jax.ShapeDtypeStruct((m, n), F32), jax.ShapeDtypeStruct((m, n), _BF),
                   jax.ShapeDtypeStruct((SUBLANE, LANE), F32)],
        compiler_params=_cparams("arbitrary", "arbitrary"),
    )(mixed, w_out, x, tgt)


def rowwise(name, fn, t_len, tile, row_in, full_in, row_out, acc_out=(), carries=(), reverse=False):
    tile = min(tile, t_len)
    n = t_len // tile
    assert t_len % tile == 0 and tile % SUBLANE == 0
    n_in, n_ro, n_acc, n_car = len(row_in) + len(full_in), len(row_out), len(acc_out), len(carries)

    def ti(i):
        return (n - 1 - i) if reverse else i

    in_specs, args = [], []
    for arr, kind in row_in:
        if kind[0] == "r":
            in_specs.append(pl.BlockSpec((tile, kind[1]), lambda i, c=kind[2]: (ti(i), c)))
        elif kind[0] == "h":
            in_specs.append(pl.BlockSpec((arr.shape[0], tile, arr.shape[2]), lambda i: (0, ti(i), 0)))
        else:
            in_specs.append(pl.BlockSpec(
                (SUBLANE, kind[1]), lambda i, c=kind[2]: (jnp.maximum(ti(i) * (tile // SUBLANE) - 1, 0), c)))
        args.append(arr)
    for arr in full_in:
        in_specs.append(pl.BlockSpec(arr.shape, lambda i, nd=arr.ndim: (0,) * nd))
        args.append(arr)
    out_specs, out_shape = [], []
    for kind in row_out:
        if kind[0] == "r":
            out_specs.append(pl.BlockSpec((tile, kind[1]), lambda i: (ti(i), 0)))
            out_shape.append(jax.ShapeDtypeStruct((t_len, kind[1]), kind[2]))
        else:
            out_specs.append(pl.BlockSpec((kind[1], tile, kind[2]), lambda i: (0, ti(i), 0)))
            out_shape.append(jax.ShapeDtypeStruct((kind[1], t_len, kind[2]), kind[3]))
    for shp in acc_out:
        out_specs.append(pl.BlockSpec(shp, lambda i, nd=len(shp): (0,) * nd))
        out_shape.append(jax.ShapeDtypeStruct(shp, F32))

    def body(*refs):
        in_refs = refs[:n_in]
        ro_refs = refs[n_in:n_in + n_ro]
        acc_refs = refs[n_in + n_ro:n_in + n_ro + n_acc]
        car_refs = refs[n_in + n_ro + n_acc:]
        step = pl.program_id(0)
        if n_car:
            @pl.when(step == 0)
            def _():
                for r in car_refs:
                    r[...] = jnp.zeros_like(r)
        vals = [r[...].astype(F32) for r in in_refs] + [r[...] for r in car_refs]
        outs = fn(ti(step), *vals)
        assert len(outs) == n_ro + n_acc + n_car, (name, len(outs))
        for r, o in zip(ro_refs, outs[:n_ro]):
            r[...] = o.astype(r.dtype)
        for r, o in zip(acc_refs, outs[n_ro:n_ro + n_acc]):
            @pl.when(step == 0)
            def _(r=r, o=o):
                r[...] = o

            @pl.when(step != 0)
            def _(r=r, o=o):
                r[...] += o
        for r, o in zip(car_refs, outs[n_ro + n_acc:]):
            r[...] = o

    res = pl.pallas_call(
        body,
        name=name,
        grid=(n,),
        in_specs=in_specs,
        out_specs=out_specs,
        out_shape=out_shape,
        scratch_shapes=[pltpu.VMEM(s, F32) for s in carries],
        compiler_params=_cparams("arbitrary"),
    )(*args)
    return list(res)


def _vjp_fn(fn, n_diff, n_out):
    def g(i, *a):
        ins, cts = a[:len(a) - n_out], a[len(a) - n_out:]
        diff, rest = ins[:n_diff], ins[n_diff:]
        _, pull = jax.vjp(lambda *d: tuple(fn(i, *d, *rest)), *diff)
        return tuple(pull(tuple(cts)))

    return g


def f_rms_x(i, x, gain):
    return (_rms(x, gain),)


def f_lat(i, cq, ckv, gq, gkv):
    return _rms(cq, gq), _rms(ckv, gkv)


def _rope_tables(pos, invf):
    ang = pos * invf
    lane = lax.broadcasted_iota(jnp.int32, (1, LANE), 1)
    cosv, sinv = jnp.cos(ang), jnp.sin(ang)
    half = ROPE // 2
    c = jnp.where(lane < ROPE, cosv, 0.0)
    sa = jnp.where(lane < half, -sinv, 0.0)
    sb = jnp.where((lane >= half) & (lane < ROPE), sinv, 0.0)
    return c, sa, sb


def _rope(xh, tabs):
    c, sa, sb = tabs
    half = ROPE // 2
    return xh * c + _roll(xh, LANE - half, 1) * sa + _roll(xh, half, 1) * sb


def f_head(i, q_raw, kv_raw, kr, qg, kg, pos, invf):
    tabs = _rope_tables(pos, invf)
    qs, ks, vs = [], [], []
    kr_ss = jnp.sum(kr * kr, axis=-1, keepdims=True)
    kr_rot = _rope(kr * kg[:, NOPE:], tabs)
    for h in range(HEADS):
        lo = q_raw[:, HEAD_PAD * h:HEAD_PAD * h + NOPE]
        hi = q_raw[:, HEAD_PAD * h + NOPE:HEAD_PAD * (h + 1)]
        ss = jnp.sum(lo * lo, axis=-1, keepdims=True) + jnp.sum(hi * hi, axis=-1, keepdims=True)
        r = lax.rsqrt(ss * (1.0 / QK) + EPS)
        qs.append(jnp.concatenate([lo * r * qg[:, :NOPE], _rope(hi * r * qg[:, NOPE:], tabs)], axis=1))
        lo = kv_raw[:, 2 * NOPE * h:2 * NOPE * h + NOPE]
        ss = jnp.sum(lo * lo, axis=-1, keepdims=True) + kr_ss
        r = lax.rsqrt(ss * (1.0 / QK) + EPS)
        ks.append(jnp.concatenate([lo * r * kg[:, :NOPE], kr_rot * r], axis=1))
        vs.append(kv_raw[:, 2 * NOPE * h + NOPE:2 * NOPE * (h + 1)])
    return jnp.stack(qs), jnp.stack(ks), jnp.stack(vs)


def f_mix(i, o_mla, mgate, o_gdn, ggate, og):
    parts = [o_mla * jax.nn.silu(mgate)]
    for h in range(HEADS):
        parts.append(_rms(o_gdn[h], og) * jax.nn.silu(ggate[:, LANE * h:LANE * (h + 1)]))
    return (jnp.concatenate(parts, axis=1),)


def _row(a, j):
    rows = lax.broadcasted_iota(jnp.int32, a.shape, 0)
    return jnp.sum(jnp.where(rows == j, a, 0.0), axis=0, keepdims=True)


def _shift_rows(x, halo, d):
    xs = _roll(x, d, 0)
    hs = _roll(halo, d, 0)
    r8 = lax.broadcasted_iota(jnp.int32, hs.shape, 0)
    top = jnp.where(r8 < d, hs, xs[:SUBLANE])
    return jnp.concatenate([top, xs[SUBLANE:]], axis=0)


def _conv_silu(x, halo, w):
    y = _row(w, CONV_W - 1) * x
    for j in range(CONV_W - 1):
        y = y + _row(w, j) * _shift_rows(x, halo, CONV_W - 1 - j)
    return jax.nn.silu(y)


def _head_select(offset):
    r = lax.broadcasted_iota(jnp.int32, (LANE, WIDTH), 0)
    c = lax.broadcasted_iota(jnp.int32, (LANE, WIDTH), 1)
    return (r == offset + lax.shift_right_logical(c, 7)).astype(_BF)


def _split3(x):
    x1 = x.astype(_BF)
    r1 = x - x1.astype(F32)
    x2 = r1.astype(_BF)
    return x1, x2, (r1 - x2.astype(F32)).astype(_BF)


@jax.custom_vjp
def _spread(x, sel):
    return _d_nn(jnp.concatenate(_split3(x), axis=1), jnp.concatenate([sel, sel, sel], axis=0))


def _spread_fwd(x, sel):
    return _spread(x, sel), sel


def _spread_bwd(sel, g):
    g1, g2, g3 = _split3(g)
    return _d_nt(g1, sel) + _d_nt(g2, sel) + _d_nt(g3, sel), jnp.zeros_like(sel)


_spread.defvjp(_spread_fwd, _spread_bwd)


def f_gdn_pre(i, gq, gk, gv, gab, hq, hk, hv, cwq, cwk, cwv, alog, dtb):
    live = jnp.where(i == 0, 0.0, 1.0)
    q = _conv_silu(gq, hq * live, cwq)
    k = _conv_silu(gk, hk * live, cwk)
    v = _conv_silu(gv, hv * live, cwv)
    g = _spread(-jnp.exp(alog) * jax.nn.softplus(gab + dtb), _head_select(0))
    beta = _spread(jax.nn.sigmoid(gab), _head_select(HEADS))
    qs, ks, vs, gs, bs = [], [], [], [], []
    for h in range(HEADS):
        sl = slice(LANE * h, LANE * (h + 1))
        qh, kh = q[:, sl], k[:, sl]
        qs.append(qh * lax.rsqrt(jnp.sum(qh * qh, axis=-1, keepdims=True) + EPS) * (GDN_DIM ** -0.5))
        ks.append(kh * lax.rsqrt(jnp.sum(kh * kh, axis=-1, keepdims=True) + EPS))
        vs.append(v[:, sl])
        gs.append(g[:, sl])
        bs.append(beta[:, sl])
    return jnp.stack(qs), jnp.stack(ks), jnp.stack(vs), jnp.stack(gs), jnp.stack(bs)


def gdn_pre_bwd(i, gq, gk, gv, gab, hq, hk, hv, dq, dk, dv, dg, db, cwq, cwk, cwv, alog, dtb, cq, ck, cv):
    grads = _vjp_fn(f_gdn_pre, 12, 5)(i, gq, gk, gv, gab, hq, hk, hv, cwq, cwk, cwv, alog, dtb, dq, dk, dv, dg, db)
    dgq, dgk, dgv, dgab, dhq, dhk, dhv, dcwq, dcwk, dcwv, dalog, ddtb = grads

    def add_tail(dx, carry):
        return jnp.concatenate([dx[:-SUBLANE], dx[-SUBLANE:] + carry], axis=0)

    return (add_tail(dgq, cq), add_tail(dgk, ck), add_tail(dgv, cv), dgab,
            dcwq, dcwk, dcwv, dalog, ddtb, dhq, dhk, dhv)


def _flash_tile(t_len):
    return min(512, t_len)


FLASH_HEADS = 4
FLASH_BWD_HEADS = 2
LOG2E = 1.4426950408889634


def _causal(rows0, shape):
    r = rows0 + lax.broadcasted_iota(jnp.int32, shape, 0)
    c = lax.broadcasted_iota(jnp.int32, shape, 1)
    return c <= r


def flash_fwd(q, k, v):
    h_n, t_len, _ = q.shape
    tq = _flash_tile(t_len)
    nq = t_len // tq
    hb = FLASH_HEADS
    kw = 2 if nq % 2 == 0 else 1
    tk = kw * tq
    c2 = (QK ** -0.5) * LOG2E
    pairs = [(i, j) for i in range(nq) for j in range(i // kw + 1)]
    qt = jnp.array([p[0] for p in pairs], jnp.int32)
    kt = jnp.array([p[1] for p in pairs], jnp.int32)

    def body(qt_ref, kt_ref, q_ref, k_ref, v_ref, o_ref, lse_ref, m_s, acc_s):
        step = pl.program_id(1)
        qi, kj = qt_ref[step], kt_ref[step]
        last = qi // kw

        @pl.when(kj == 0)
        def _():
            m_s[...] = jnp.full_like(m_s, -jnp.inf)
            acc_s[...] = jnp.zeros_like(acc_s)

        def tile(diagonal):
            s = _bdot(q_ref[...], k_ref[...], _NT3, False) * c2
            if diagonal:
                s = jnp.where(_causal((qi % kw) * tq, (tq, tk))[None], s, -jnp.inf)
            m_old = m_s[...]
            m_new = jnp.maximum(m_old, jnp.max(s, axis=-1, keepdims=True))
            p = jnp.exp2(s - m_new).astype(_BF)
            v_ones = jnp.concatenate([v_ref[...], jnp.ones((hb, tk, LANE), _BF)], axis=2)
            acc_s[...] = jnp.exp2(m_old - m_new) * acc_s[...] + _bdot(p, v_ones, _NN3, False)
            m_s[...] = m_new

        @pl.when(kj < last)
        def _():
            tile(False)

        @pl.when(kj == last)
        def _():
            tile(True)
            acc = acc_s[...]
            l_sum = acc[:, :, LANE:]
            o = acc[:, :, :LANE] / l_sum
            for hh in range(hb):
                o_ref[:, LANE * hh:LANE * (hh + 1)] = o[hh]
            lse_ref[...] = m_s[...] + jnp.log2(jnp.max(l_sum, axis=-1, keepdims=True))

    return pl.pallas_call(
        body,
        name="flash_fwd",
        grid_spec=pltpu.PrefetchScalarGridSpec(
            num_scalar_prefetch=2,
            grid=(h_n // hb, qt.shape[0]),
            in_specs=[
                pl.BlockSpec((hb, tq, HEAD_PAD), lambda h, s, qt_ref, kt_ref: (h, qt_ref[s], 0)),
                pl.BlockSpec((hb, tk, HEAD_PAD), lambda h, s, qt_ref, kt_ref: (h, kt_ref[s], 0)),
                pl.BlockSpec((hb, tk, LANE), lambda h, s, qt_ref, kt_ref: (h, kt_ref[s], 0)),
            ],
            out_specs=[
                pl.BlockSpec((tq, hb * LANE), lambda h, s, qt_ref, kt_ref: (qt_ref[s], h)),
                pl.BlockSpec((hb, tq, 1), lambda h, s, qt_ref, kt_ref: (h, qt_ref[s], 0)),
            ],
            scratch_shapes=[pltpu.VMEM((hb, tq, 1), F32), pltpu.VMEM((hb, tq, 2 * LANE), F32)],
        ),
        out_shape=[jax.ShapeDtypeStruct((t_len, h_n * LANE), F32), jax.ShapeDtypeStruct((h_n, t_len, 1), F32)],
        compiler_params=_cparams("parallel", "arbitrary"),
    )(qt, kt, q, k, v)


def flash_bwd(q, k, v, do, lse, delta):
    h_n, t_len, _ = q.shape
    tq = _flash_tile(t_len)
    nq = t_len // tq
    hb = FLASH_BWD_HEADS
    kw = 2 if nq % 2 == 0 else 1
    tk = kw * tq
    pairs = [(i, j) for j in range(nq // kw) for i in range(kw * j, nq)]
    n_steps = len(pairs)
    qt = jnp.array([p[0] for p in pairs], jnp.int32)
    kt = jnp.array([p[1] for p in pairs], jnp.int32)
    scale = QK ** -0.5
    c2 = scale * LOG2E

    def body(qt_ref, kt_ref, q_ref, k_ref, v_ref, do_ref, lse_ref, dl_ref, dq_hbm, dk_ref, dv_ref, dq_s, dq_sem):
        group, step = pl.program_id(0), pl.program_id(1)
        qi, kj = qt_ref[step], kt_ref[step]

        @pl.when(step == 0)
        def _():
            dq_s[...] = jnp.zeros_like(dq_s)

        def tile(diagonal, cols):
            qb, kb = q_ref[...], k_ref[:, :cols]
            dob = jnp.stack([do_ref[:, LANE * hh:LANE * (hh + 1)] for hh in range(hb)])
            p = jnp.exp2(_bdot(qb, kb, _NT3, False) * c2 - lse_ref[...])
            if diagonal:
                p = jnp.where(_causal((qi % kw) * tq, (tq, cols))[None], p, 0.0)
            dv = _bdot(p, dob, _TN3, False)
            ds = p * (_bdot(dob, v_ref[:, :cols], _NT3, False) - dl_ref[...]) * scale
            dk = _bdot(ds, qb, _TN3, False)
            dq_s[:, pl.ds(pl.multiple_of(qi * tq, tq), tq), :] += _bdot(ds, kb, _NN3, False)
            return dk, dv

        @pl.when(qi == kw * kj)
        def _():
            dk, dv = tile(True, tq)
            dk_ref[:, :tq], dv_ref[:, :tq] = dk, dv
            if kw > 1:
                dk_ref[:, tq:] = jnp.zeros((hb, tk - tq, HEAD_PAD), F32)
                dv_ref[:, tq:] = jnp.zeros((hb, tk - tq, LANE), F32)

        if kw > 1:
            @pl.when((qi != kw * kj) & (qi // kw == kj))
            def _():
                dk, dv = tile(True, tk)
                dk_ref[...] += dk
                dv_ref[...] += dv

        @pl.when(qi // kw > kj)
        def _():
            dk, dv = tile(False, tk)
            dk_ref[...] += dk
            dv_ref[...] += dv

        @pl.when(step == n_steps - 1)
        def _():
            out = pltpu.make_async_copy(dq_s, dq_hbm.at[pl.ds(group * hb, hb)], dq_sem)
            out.start()
            out.wait()

    def qmap(h, s, qt_ref, kt_ref):
        return (h, qt_ref[s], 0)

    def kmap(h, s, qt_ref, kt_ref):
        return (h, kt_ref[s], 0)

    return pl.pallas_call(
        body,
        name="flash_bwd",
        grid_spec=pltpu.PrefetchScalarGridSpec(
            num_scalar_prefetch=2,
            grid=(h_n // hb, n_steps),
            in_specs=[
                pl.BlockSpec((hb, tq, HEAD_PAD), qmap),
                pl.BlockSpec((hb, tk, HEAD_PAD), kmap),
                pl.BlockSpec((hb, tk, LANE), kmap),
                pl.BlockSpec((tq, hb * LANE), lambda h, s, qt_ref, kt_ref: (qt_ref[s], h)),
                pl.BlockSpec((hb, tq, 1), qmap),
                pl.BlockSpec((hb, tq, 1), qmap),
            ],
            out_specs=[
                pl.BlockSpec(memory_space=pl.ANY),
                pl.BlockSpec((hb, tk, HEAD_PAD), kmap),
                pl.BlockSpec((hb, tk, LANE), kmap),
            ],
            scratch_shapes=[pltpu.VMEM((hb, t_len, HEAD_PAD), F32), pltpu.SemaphoreType.DMA],
        ),
        out_shape=[
            jax.ShapeDtypeStruct((h_n, t_len, HEAD_PAD), F32),
            jax.ShapeDtypeStruct((h_n, t_len, HEAD_PAD), F32),
            jax.ShapeDtypeStruct((h_n, t_len, LANE), F32),
        ],
        compiler_params=_cparams("parallel", "arbitrary"),
    )(qt, kt, q, k, v, do, lse, delta)


def _tri_ones(h_n):
    ii = lax.broadcasted_iota(jnp.int32, (h_n, CHUNK, CHUNK), 1)
    jj = lax.broadcasted_iota(jnp.int32, (h_n, CHUNK, CHUNK), 2)
    return (ii >= jj).astype(_BF)


@jax.custom_vjp
def _chunk_cumsum(gb):
    tri = _tri_ones(gb.shape[0])
    return _bdot(jnp.concatenate([tri, tri, tri], axis=2), jnp.concatenate(_split3(gb), axis=1), _NN3, False)


def _chunk_cumsum_bwd(_, ct):
    tri = _tri_ones(ct.shape[0])
    return (_bdot(jnp.concatenate([tri, tri, tri], axis=1), jnp.concatenate(_split3(ct), axis=1), _TN3, False),)


_chunk_cumsum.defvjp(lambda gb: (_chunk_cumsum(gb), None), _chunk_cumsum_bwd)


@jax.custom_vjp
def _pair_diff(gcb):
    g1, g2, g3 = _split3(gcb)
    lane = lax.broadcasted_iota(jnp.int32, (1, 1, LANE), 2)
    one, zero = jnp.ones((), _BF), jnp.zeros((), _BF)
    a = jnp.where(lane == 0, g1, jnp.where(lane == 1, g2, jnp.where(lane == 2, g3, jnp.where(lane < 6, one, zero))))
    b = jnp.where(lane < 3, one, jnp.where(lane == 3, -g1, jnp.where(lane == 4, -g2, jnp.where(lane == 5, -g3, zero))))
    return _bdot(a, b, _NT3, False)


def _pair_diff_bwd(_, ct):
    parts = _split3(ct)
    ones = jnp.ones((ct.shape[0], 3 * CHUNK, LANE), _BF)
    rows = _bdot(jnp.concatenate(parts, axis=2), ones, _NN3, False)
    cols = _bdot(jnp.concatenate(parts, axis=1), ones, _TN3, False)
    lane = lax.broadcasted_iota(jnp.int32, (1, 1, LANE), 2)
    return (jnp.where(lane == 0, rows - cols, 0.0),)


_pair_diff.defvjp(lambda gcb: (_pair_diff(gcb), None), _pair_diff_bwd)


@jax.custom_vjp
def _saved_inverse(lmat, inv):
    return inv


def _saved_inverse_bwd(inv, g):
    return -_pdot(_pdot(inv, g, "tn"), inv, "nt"), jnp.zeros_like(inv)


_saved_inverse.defvjp(lambda lmat, inv: (inv, inv), _saved_inverse_bwd)


def gdn_step(s, q, k, v, gb, bb, inv_saved=None):
    c = CHUNK
    ii = lax.broadcasted_iota(jnp.int32, (1, c, c), 1)
    jj = lax.broadcasted_iota(jnp.int32, (1, c, c), 2)
    incl, strict = ii >= jj, ii > jj
    gcb = _chunk_cumsum(gb)
    diff = _pair_diff(gcb)
    decay = jnp.where(incl, jnp.exp(jnp.where(incl, diff, 0.0)), 0.0)
    kb, vb = k * bb, v * bb
    egc = jnp.exp(gcb)
    lmat = jnp.where(strict, _bmm_nt(kb, k) * decay, 0.0)
    if inv_saved is None:
        mm3 = _bh3_passes
        inv = (ii == jj).astype(F32) - lmat
        pw = mm3(lmat, lmat)
        for step in range(5):
            inv = inv + mm3(inv, pw)
            if step < 4:
                pw = mm3(pw, pw)
    else:
        mm3 = _bh3
        inv = _saved_inverse(lmat, inv_saved)
    u = mm3(inv, vb)
    w = mm3(inv, kb * egc)
    attn = _bmm_nt(q, k) * decay
    qd = q * egc
    g_end = jnp.sum(gb, axis=1, keepdims=True)
    kd = k * jnp.exp(g_end - gcb)
    v_new = u - _bmm(w, s)
    o = _bmm(qd, s) + _bmm(attn, v_new)
    s_new = s * jnp.exp(g_end) + _bmm_tn(kd, v_new)
    return s_new, o, inv


def gdn_fwd(q, k, v, gb, bb):
    h_n, t_len, d = q.shape
    n = t_len // CHUNK
    blk = pl.BlockSpec((h_n, CHUNK, d), lambda i: (0, i, 0))

    def body(q_ref, k_ref, v_ref, g_ref, b_ref, o_ref, sall_ref, inv_ref, s_s):
        @pl.when(pl.program_id(0) == 0)
        def _():
            s_s[...] = jnp.zeros_like(s_s)

        s = s_s[...]
        sall_ref[0] = s
        s_s[...], o_ref[...], inv_ref[0] = gdn_step(s, q_ref[...], k_ref[...], v_ref[...], g_ref[...], b_ref[...])

    return pl.pallas_call(
        body,
        name="gdn_fwd",
        grid=(n,),
        in_specs=[blk] * 5,
        out_specs=[blk, pl.BlockSpec((1, h_n, d, d), lambda i: (i, 0, 0, 0)),
                   pl.BlockSpec((1, h_n, CHUNK, CHUNK), lambda i: (i, 0, 0, 0))],
        out_shape=[jax.ShapeDtypeStruct((h_n, t_len, d), F32), jax.ShapeDtypeStruct((n, h_n, d, d), F32),
                   jax.ShapeDtypeStruct((n, h_n, CHUNK, CHUNK), F32)],
        scratch_shapes=[pltpu.VMEM((h_n, d, d), F32)],
        compiler_params=_cparams("arbitrary"),
    )(q, k, v, gb, bb)


def gdn_bwd(q, k, v, gb, bb, s_all, inv_all, do):
    h_n, t_len, d = q.shape
    n = t_len // CHUNK
    blk = pl.BlockSpec((h_n, CHUNK, d), lambda i: (0, n - 1 - i, 0))

    def body(q_ref, k_ref, v_ref, g_ref, b_ref, sall_ref, inv_ref, do_ref, dq_ref, dk_ref, dv_ref, dg_ref, db_ref,
             ds_s):
        @pl.when(pl.program_id(0) == 0)
        def _():
            ds_s[...] = jnp.zeros_like(ds_s)

        inv = inv_ref[0]
        _, pull = jax.vjp(lambda *a: gdn_step(*a, inv_saved=inv)[:2], sall_ref[0], q_ref[...], k_ref[...], v_ref[...],
                          g_ref[...], b_ref[...])
        ds_s[...], dq_ref[...], dk_ref[...], dv_ref[...], dg_ref[...], db_ref[...] = pull((ds_s[...], do_ref[...]))

    return pl.pallas_call(
        body,
        name="gdn_bwd",
        grid=(n,),
        in_specs=[blk] * 5 + [pl.BlockSpec((1, h_n, d, d), lambda i: (n - 1 - i, 0, 0, 0)),
                              pl.BlockSpec((1, h_n, CHUNK, CHUNK), lambda i: (n - 1 - i, 0, 0, 0)), blk],
        out_specs=[blk] * 5,
        out_shape=[jax.ShapeDtypeStruct((h_n, t_len, d), F32)] * 5,
        scratch_shapes=[pltpu.VMEM((h_n, d, d), F32)],
        compiler_params=_cparams("arbitrary"),
    )(q, k, v, gb, bb, s_all, inv_all, do)


def _pad_cols(a, n):
    return jnp.pad(a, ((0, 0), (0, n - a.shape[1])))


def arrange_w_in(w):
    pieces, start = [], 0
    for n in R_SPLITS:
        pieces.append(w[:, start:start + n])
        start += n
    cq, ckv, kr, mgate, gq, gk, gv, ga, gb, ggate = pieces
    return jnp.concatenate([mgate, gq, gk, gv, ggate, cq, ckv, _pad_cols(kr, LANE),
                            _pad_cols(jnp.concatenate([ga, gb], axis=1), LANE)], axis=1)


def unarrange_w_in(pieces):
    mgate, gq, gk, gv, ggate, rest = pieces

    def cols(start, n):
        return rest[:, start - P_CQ:start - P_CQ + n]
    return jnp.concatenate([cols(P_CQ, Q_LORA), cols(P_CKV, KV_LORA), cols(P_KR, ROPE), mgate, gq, gk, gv,
                            cols(P_GAB, HEADS), cols(P_GAB + HEADS, HEADS), ggate], axis=1)


def arrange_w_uq(w):
    w = w.reshape(w.shape[0], HEADS, QK)
    return jnp.pad(w, ((0, 0), (0, 0), (0, HEAD_PAD - QK))).reshape(w.shape[0], HEADS * HEAD_PAD)


def unarrange_w_uq(g):
    return g.reshape(g.shape[0], HEADS, HEAD_PAD)[:, :, :QK].reshape(g.shape[0], HEADS * QK)


def local_step(x, pos, tgt, p, on_early_grads=None, on_d_proj=None, on_weight_grads=None, on_d_xn=None,
               first_after=None, late_weights=None):
    t_len = x.shape[0]
    w_in = p["w_in"]
    norm_gain = p["norm_gain"].reshape(1, D_MODEL)
    qa_gain = p["mla_q_a_gain"].reshape(1, Q_LORA)
    kva_gain = p["mla_kv_a_gain"].reshape(1, KV_LORA)
    qg = _pad_cols(p["mla_q_norm_gain"].reshape(1, QK), HEAD_PAD)
    kg = _pad_cols(p["mla_k_norm_gain"].reshape(1, QK), HEAD_PAD)
    alog = _pad_cols(p["gdn_a_log"].reshape(1, HEADS), LANE)
    dtb = _pad_cols(p["gdn_dt_bias"].reshape(1, HEADS), LANE)
    og = p["gdn_out_norm_gain"].reshape(1, GDN_DIM)
    half = ROPE // 2
    inv_freq = jnp.power(ROPE_THETA, -jnp.arange(half, dtype=F32) / half)
    invf = _pad_cols(jnp.concatenate([inv_freq, inv_freq]).reshape(1, ROPE), LANE)

    rt = 256
    r = "r"
    first_after = jnp.zeros((SUBLANE, LANE), F32) if first_after is None else first_after
    (xn,) = rowwise("rms_x", lambda i, x_, gain_, after_: f_rms_x(i, x_, gain_), t_len, rt, [(x, (r, D_MODEL, 0))],
                    [norm_gain, first_after], [(r, D_MODEL, _BF)])
    proj = matmul("proj", xn, w_in, "nn")
    if late_weights is not None:
        p = {**p, **late_weights(proj)}
    w_uq, w_ukv, w_out = p["w_uq"], p["w_ukv"], p["w_out"]
    cw = p["gdn_conv_w"].reshape(CONV_W, 3 * WIDTH)
    cwq, cwk, cwv = cw[:, :WIDTH], cw[:, WIDTH:2 * WIDTH], cw[:, 2 * WIDTH:]
    cq_in = (proj, (r, Q_LORA, P_CQ // Q_LORA))
    ckv_in = (proj, (r, KV_LORA, P_CKV // KV_LORA))
    kr_in = (proj, (r, LANE, P_KR // LANE))
    mgate_in = (proj, (r, WIDTH, P_MGATE // WIDTH))
    ggate_in = (proj, (r, WIDTH, P_GGATE // WIDTH))
    gqkv_in = [(proj, (r, WIDTH, P_GQ // WIDTH)), (proj, (r, WIDTH, P_GK // WIDTH)), (proj, (r, WIDTH, P_GV // WIDTH))]
    gab_in = (proj, (r, LANE, P_GAB // LANE))
    halos = [(proj, ("halo", WIDTH, P_GQ // WIDTH)), (proj, ("halo", WIDTH, P_GK // WIDTH)),
             (proj, ("halo", WIDTH, P_GV // WIDTH))]

    q_lat, kv_lat = rowwise("lat", f_lat, t_len, rt, [cq_in, ckv_in], [qa_gain, kva_gain],
                            [(r, Q_LORA, _BF), (r, KV_LORA, _BF)])
    q_raw = matmul("q_up", q_lat, w_uq, "nn")
    kv_raw = matmul("kv_up", kv_lat, w_ukv, "nn")
    wide = HEADS * HEAD_PAD
    head_in = [(q_raw, (r, wide, 0)), (kv_raw, (r, wide, 0)), kr_in]
    pos_in = (pos, (r, 1, 0))
    q_full, k_full, v_mla = rowwise(
        "head", lambda i, qr, kvr, kr, ps, qg_, kg_, iv: f_head(i, qr, kvr, kr, qg_, kg_, ps, iv), t_len, rt,
        head_in + [pos_in], [qg, kg, invf],
        [("h", HEADS, HEAD_PAD, _BF), ("h", HEADS, HEAD_PAD, _BF), ("h", HEADS, LANE, _BF)])
    o_mla, lse = flash_fwd(q_full, k_full, v_mla)

    pre_in = gqkv_in + [gab_in] + halos
    pre_full = [cwq, cwk, cwv, alog, dtb]
    hkind = ("h", HEADS, GDN_DIM, F32)
    gq_n, gk_n, gv_n, g_b, b_b = rowwise("gdn_pre", f_gdn_pre, t_len, rt, pre_in, pre_full, [hkind] * 5)
    o_gdn, s_all, inv_all = gdn_fwd(gq_n, gk_n, gv_n, g_b, b_b)

    mix_in = [(o_mla, (r, WIDTH, 0)), mgate_in, (o_gdn, ("h",)), ggate_in]
    (mixed,) = rowwise("mix", f_mix, t_len, rt, mix_in, [og], [(r, 2 * WIDTH, _BF)])
    dy, dy_mx, loss_acc = out_proj_loss(mixed, w_out, x, tgt)
    loss = loss_acc[0, 0]

    d_mixed = matmul("d_mixed", dy_mx, w_out, "nt")
    g_w_out = matmul("g_w_out", mixed, dy_mx, "tn")

    def mix_bwd(i, o_mla_, mgate_, o_gdn_, ggate_, d_mixed_, og_):
        do_mla_, d_mgate_, do_gdn_, d_ggate_, g_og_ = _vjp_fn(f_mix, 5, 1)(i, o_mla_, mgate_, o_gdn_, ggate_, og_, d_mixed_)
        delta_ = jnp.stack([jnp.sum(o_mla_[:, LANE * h:LANE * (h + 1)] * do_mla_[:, LANE * h:LANE * (h + 1)],
                                    axis=-1, keepdims=True) for h in range(HEADS)])
        return do_mla_, d_mgate_, do_gdn_, d_ggate_, delta_, g_og_

    do_mla, d_mgate, do_gdn, d_ggate, delta, g_og = rowwise(
        "mix_bwd", mix_bwd, t_len, rt, mix_in + [(d_mixed, (r, 2 * WIDTH, 0))], [og],
        [(r, WIDTH, F32), (r, WIDTH, _BF), hkind, (r, WIDTH, _BF), ("h", HEADS, 1, F32)], [(1, GDN_DIM)])
    dq_n, dk_n, dv_n, dg_b, db_b = gdn_bwd(gq_n, gk_n, gv_n, g_b, b_b, s_all, inv_all, do_gdn)
    cts_in = [(a, ("h",)) for a in (dq_n, dk_n, dv_n, dg_b, db_b)]
    d_gq, d_gk, d_gv, d_gab, g_cwq, g_cwk, g_cwv, g_alog, g_dtb = rowwise(
        "gdn_pre_bwd", gdn_pre_bwd, t_len, rt, pre_in + cts_in, pre_full,
        [(r, WIDTH, _BF)] * 3 + [(r, LANE, _BF)],
        [(CONV_W, WIDTH)] * 3 + [(1, LANE)] * 2, carries=[(SUBLANE, WIDTH)] * 3, reverse=True)

    dq_full, dk_full, dv_mla = flash_bwd(q_full, k_full, v_mla, do_mla, lse, delta)
    head_cts = [(a, ("h",)) for a in (dq_full, dk_full, dv_mla)]

    def head_bwd(i, q_raw_, kv_raw_, kr_, pos_, dq_, dk_, dv_, qg_, kg_, invf_):
        return _vjp_fn(f_head, 5, 3)(i, q_raw_, kv_raw_, kr_, qg_, kg_, pos_, invf_, dq_, dk_, dv_)

    dq_raw, dkv_raw, d_kr, g_qg, g_kg = rowwise(
        "head_bwd", head_bwd, t_len, rt, head_in + [pos_in] + head_cts, [qg, kg, invf],
        [(r, wide, _BF), (r, wide, _BF), (r, LANE, _BF)], [(1, HEAD_PAD), (1, HEAD_PAD)])
    dq_lat = matmul("dq_lat", dq_raw, w_uq, "nt")
    g_w_uq = matmul("g_w_uq", q_lat, dq_raw, "tn")
    dkv_lat = matmul("dkv_lat", dkv_raw, w_ukv, "nt")
    g_w_ukv = matmul("g_w_ukv", kv_lat, dkv_raw, "tn")
    grads = {
        "w_uq": g_w_uq, "w_ukv": g_w_ukv, "gdn_conv_w": jnp.concatenate([g_cwq, g_cwk, g_cwv], axis=1),
        "w_out": g_w_out, "mla_q_norm_gain": g_qg[:, :QK], "mla_k_norm_gain": g_kg[:, :QK],
        "gdn_a_log": g_alog[:, :HEADS], "gdn_dt_bias": g_dtb[:, :HEADS], "gdn_out_norm_gain": g_og,
    }
    after = jnp.zeros((SUBLANE, LANE), F32) if on_early_grads is None else on_early_grads(grads)

    def lat_bwd(i, cq_, ckv_, dql_, dkl_, gq_, gkv_, after_):
        return _vjp_fn(f_lat, 4, 2)(i, cq_, ckv_, gq_, gkv_, dql_, dkl_)

    d_cq, d_ckv, grads["mla_q_a_gain"], grads["mla_kv_a_gain"] = rowwise(
        "lat_bwd", lat_bwd, t_len, rt, [cq_in, ckv_in, (dq_lat, (r, Q_LORA, 0)), (dkv_lat, (r, KV_LORA, 0))],
        [qa_gain, kva_gain, after], [(r, Q_LORA, _BF), (r, KV_LORA, _BF)], [(1, Q_LORA), (1, KV_LORA)])

    d_proj = [d_mgate, d_gq, d_gk, d_gv, d_ggate, jnp.concatenate([d_cq, d_ckv, d_kr, d_gab], axis=1)]
    after = None if on_d_proj is None else on_d_proj(d_proj[-1])
    grads["w_in"] = [matmul("g_w_in_%d" % j, xn, piece, "tn", after=after if j == 0 else None)
                     for j, piece in enumerate(d_proj)]
    after = None if on_weight_grads is None else on_weight_grads(grads)
    d_xn = matmul_pieces("d_xn", d_proj, w_in, after=after)
    after = jnp.zeros((SUBLANE, LANE), F32) if on_d_xn is None else on_d_xn(d_xn)

    def rms_x_bwd(i, x_, dxn_, dy_, gain_, after_):
        dx, dgain = _vjp_fn(f_rms_x, 2, 1)(i, x_, gain_, dxn_)
        return dx + dy_, dgain

    grad_x, grads["norm_gain"] = rowwise(
        "rms_x_bwd", rms_x_bwd, t_len, rt, [(x, (r, D_MODEL, 0)), (d_xn, (r, D_MODEL, 0)), (dy, (r, D_MODEL, 0))],
        [norm_gain, after], [(r, D_MODEL, F32)], [(1, D_MODEL)])
    return loss, grad_x, grads


MESH = pl.DeviceIdType.MESH
ANY = pl.BlockSpec(memory_space=pl.ANY)
CHIP_FLIPS = ((1, 0), (0, 1), (1, 1))


def _place():
    return lax.axis_index("x"), lax.axis_index("y"), lax.axis_index("c")


def _flip(v, f):
    return 1 - v if f else v


def all_gather(shards):
    n_arr = len(shards)

    def body(*refs):
        x_refs, o_refs = refs[:n_arr], refs[n_arr:2 * n_arr]
        send_sems, recv_sems, local_sems = refs[2 * n_arr:]
        x, y, c = _place()
        me, sibling = (x, y, c), (x, y, 1 - c)
        chips = [(_flip(x, fx), _flip(y, fy)) for fx, fy in CHIP_FLIPS]

        def copy(a, k, block, to, src=None):
            px, py, pc = block
            dst = o_refs[a].at[4 * px + 2 * py + pc]
            return pltpu.make_async_remote_copy(
                src_ref=dst if src is None else src, dst_ref=dst, send_sem=send_sems.at[a, k],
                recv_sem=recv_sems.at[a, k], device_id=to, device_id_type=MESH)

        mine, first, passed = [], [], []
        for a in range(n_arr):
            cp = pltpu.make_async_copy(x_refs[a], o_refs[a].at[4 * x + 2 * y + c], local_sems.at[a])
            cp.start()
            mine.append(cp)
            first.append(copy(a, 0, me, sibling, src=x_refs[a]))
            first += [copy(a, 1 + j, me, (*chip, c), src=x_refs[a]) for j, chip in enumerate(chips)]
        for cp in first:
            cp.start()
        for j, chip in enumerate(chips):
            for a in range(n_arr):
                copy(a, 1 + j, (*chip, c), me).wait_recv()
                cp = copy(a, 4 + j, (*chip, c), sibling)
                cp.start()
                passed.append(cp)
        for a in range(n_arr):
            copy(a, 0, sibling, me).wait_recv()
            for j, chip in enumerate(chips):
                copy(a, 4 + j, (*chip, 1 - c), me).wait_recv()
        for cp in first + passed:
            cp.wait_send()
        for cp in mine:
            cp.wait()

    return pl.pallas_call(
        body,
        name="all_gather",
        out_shape=[jax.ShapeDtypeStruct((N_DEV,) + s.shape, s.dtype) for s in shards],
        in_specs=[ANY] * n_arr,
        out_specs=[ANY] * n_arr,
        scratch_shapes=[pltpu.SemaphoreType.DMA((n_arr, 7)), pltpu.SemaphoreType.DMA((n_arr, 7)),
                        pltpu.SemaphoreType.DMA((n_arr,))],
    )(*shards)


HBM = pl.BlockSpec(memory_space=pltpu.HBM)
SEMS = pl.BlockSpec(memory_space=pltpu.SEMAPHORE)
SIDE_EFFECT = pltpu.SideEffectType.DATAFLOW_SIDE_EFFECTING


def core_routes(x, y, c):
    return [(2 * q + (1 - c), q, (x, y, 1 - c)) for q in range(4)]


def chip_routes(x, y, c):
    routes = []
    for j, (fx, fy) in enumerate(CHIP_FLIPS):
        px, py = _flip(x, fx), _flip(y, fy)
        routes.append((2 * px + py, j, (px, py, c)))
    return routes


def _route_copies(routes, n_routes, src_refs, land_refs, sems):
    x, y, c = _place()
    n_copies = len(src_refs) * n_routes
    return [pltpu.make_async_remote_copy(src_ref=src.at[s], dst_ref=land.at[d], send_sem=sems[a * n_routes + k],
                                         recv_sem=sems[n_copies + a * n_routes + k], device_id=dev,
                                         device_id_type=MESH)
            for a, (src, land) in enumerate(zip(src_refs, land_refs)) for k, (s, d, dev) in enumerate(routes(x, y, c))]


def gather_routes(x, y, c):
    me = 4 * x + 2 * y + c
    return [(0, me, (_flip(x, (k >> 2) & 1), _flip(y, (k >> 1) & 1), _flip(c, k & 1))) for k in range(1, N_DEV)]


def exchange_start(name, routes, n_routes, srcs, n_slots=None):
    n = len(srcs)
    n_sems = 2 * n * n_routes
    lands = [lax.empty((n_routes if n_slots is None else n_slots,) + s.shape[1:], s.dtype) for s in srcs]

    def body(*refs):
        for cp in _route_copies(routes, n_routes, refs[:n], refs[n:2 * n], refs[2 * n:2 * n + n_sems]):
            cp.start()
        refs[-1][...] = jnp.zeros_like(refs[-1])

    res = pl.pallas_call(
        body,
        name=name,
        out_shape=(*[pltpu.SemaphoreType.DMA(())] * n_sems, *[pltpu.HBM(a.shape, a.dtype) for a in srcs + lands],
                   jax.ShapeDtypeStruct((SUBLANE, LANE), F32)),
        in_specs=[HBM] * (2 * n),
        out_specs=(*[SEMS] * n_sems, *[HBM] * (2 * n), pl.BlockSpec(memory_space=pltpu.VMEM)),
        input_output_aliases={i: n_sems + i for i in range(2 * n)},
        compiler_params=pltpu.CompilerParams(has_side_effects=SIDE_EFFECT),
    )(*[pltpu.with_memory_space_constraint(a, pltpu.HBM) for a in srcs + lands])
    return (res[:n_sems], res[n_sems:-1]), res[-1]


def exchange_wait(name, routes, handle, after):
    sems, thru = handle
    n, n_sems = len(thru) // 2, len(sems)
    n_routes = n_sems // (2 * n)

    def body(*refs):
        for cp in _route_copies(routes, n_routes, refs[:n], refs[n:2 * n], refs[2 * n:2 * n + n_sems]):
            cp.wait_send()
            cp.wait_recv()

    res = pl.pallas_call(
        body,
        name=name,
        out_shape=tuple(pltpu.HBM(a.shape, a.dtype) for a in thru),
        in_specs=[HBM] * (2 * n) + [SEMS] * n_sems + [ANY],
        out_specs=tuple([HBM] * (2 * n)),
        input_output_aliases={i: i for i in range(2 * n)},
        compiler_params=pltpu.CompilerParams(has_side_effects=SIDE_EFFECT),
    )(*thru, *sems, after)
    return list(res[:n]), list(res[n:])


def gather_small(v):
    def body(v_ref, o_ref, send_sems, recv_sems, local_sem):
        x, y, c = _place()
        me = 4 * x + 2 * y + c
        mine = pltpu.make_async_copy(v_ref, o_ref.at[me], local_sem)
        mine.start()
        copies = []
        for k in range(1, N_DEV):
            fx, fy, fc = (k >> 2) & 1, (k >> 1) & 1, k & 1
            cp = pltpu.make_async_remote_copy(
                src_ref=v_ref, dst_ref=o_ref.at[me], send_sem=send_sems.at[k - 1], recv_sem=recv_sems.at[k - 1],
                device_id=(_flip(x, fx), _flip(y, fy), _flip(c, fc)), device_id_type=MESH)
            cp.start()
            copies.append(cp)
        for cp in copies:
            cp.wait()
        mine.wait()

    return pl.pallas_call(
        body,
        name="gather_small",
        out_shape=jax.ShapeDtypeStruct((N_DEV,) + v.shape, v.dtype),
        in_specs=[ANY],
        out_specs=ANY,
        scratch_shapes=[pltpu.SemaphoreType.DMA((N_DEV - 1,)), pltpu.SemaphoreType.DMA((N_DEV - 1,)),
                        pltpu.SemaphoreType.DMA],
    )(v)


def _row_tile(rows):
    for t in (256, 128, 64, 32, 16, 8):
        if rows % t == 0:
            return t
    return rows


def add_core_parts(name, g, recv, c_idx, wire):
    _, rows, cols = g.shape
    tr = _row_tile(rows)

    def body(c_ref, g_ref, r_ref, o_ref, w_ref):
        part = g_ref[...] + r_ref[...]
        o_ref[...] = part
        w_ref[...] = part.astype(w_ref.dtype)

    blk = pl.BlockSpec((1, tr, cols), lambda q, i, c_ref: (q, i, 0))
    return pl.pallas_call(
        body,
        name=name,
        grid_spec=pltpu.PrefetchScalarGridSpec(
            num_scalar_prefetch=1,
            grid=(4, rows // tr),
            in_specs=[pl.BlockSpec((1, tr, cols), lambda q, i, c_ref: (2 * q + c_ref[0], i, 0)), blk],
            out_specs=[blk, blk],
        ),
        out_shape=[jax.ShapeDtypeStruct((4, rows, cols), F32), jax.ShapeDtypeStruct((4, rows, cols), wire)],
        compiler_params=_cparams("parallel", "parallel"),
    )(c_idx, g, recv)


def _adamw(w, g, m, v):
    m = ADAM_B1 * m + (1.0 - ADAM_B1) * g
    v = ADAM_B2 * v + (1.0 - ADAM_B2) * (g * g)
    m_hat = m / (1.0 - ADAM_B1 ** ADAM_STEP)
    v_hat = v / (1.0 - ADAM_B2 ** ADAM_STEP)
    delta = -ADAM_LR * (m_hat / (jnp.sqrt(v_hat) + ADAM_EPS) + ADAM_WD * w)
    return delta, m, v


def adamw_sharded(name, parts, recv, q_idx, w, m, v):
    rows, cols = w.shape
    tr = _row_tile(rows)

    def body(q_ref, p_ref, r_ref, w_ref, m_ref, v_ref, g_out, d_out, m_out, v_out):
        g = p_ref[0] + r_ref[0].astype(F32) + r_ref[1].astype(F32) + r_ref[2].astype(F32)
        d, m_new, v_new = _adamw(w_ref[...], g, m_ref[...], v_ref[...])
        g_out[...], d_out[...], m_out[...], v_out[...] = g, d, m_new, v_new

    blk = pl.BlockSpec((tr, cols), lambda i, q_ref: (i, 0))
    return pl.pallas_call(
        body,
        name=name,
        grid_spec=pltpu.PrefetchScalarGridSpec(
            num_scalar_prefetch=1,
            grid=(rows // tr,),
            in_specs=[pl.BlockSpec((1, tr, cols), lambda i, q_ref: (q_ref[0], i, 0)),
                      pl.BlockSpec((3, tr, cols), lambda i, q_ref: (0, i, 0)), blk, blk, blk],
            out_specs=[blk] * 4,
        ),
        out_shape=[jax.ShapeDtypeStruct((rows, cols), F32)] * 4,
        compiler_params=_cparams("parallel"),
    )(q_idx, parts, recv, w, m, v)


def adamw_small(gathered, w, m, v):
    def body(g_ref, w_ref, m_ref, v_ref, g_out, d_out, m_out, v_out):
        g = g_ref[0]
        for j in range(1, N_DEV):
            g = g + g_ref[j]
        d, m_new, v_new = _adamw(w_ref[...], g, m_ref[...], v_ref[...])
        g_out[...], d_out[...], m_out[...], v_out[...] = g, d, m_new, v_new

    return pl.pallas_call(body, name="adamw_small", out_shape=[jax.ShapeDtypeStruct(w.shape, F32)] * 4)(gathered, w, m, v)


SHARDED = ("w_in", "w_uq", "w_ukv", "gdn_conv_w", "w_out")
SMALL = (("norm_gain", D_MODEL), ("mla_q_a_gain", Q_LORA), ("mla_kv_a_gain", KV_LORA), ("mla_q_norm_gain", QK),
         ("mla_k_norm_gain", QK), ("gdn_a_log", HEADS), ("gdn_dt_bias", HEADS), ("gdn_out_norm_gain", GDN_DIM))
WEIGHT_ORDER = ("norm_gain", "w_in", "mla_q_a_gain", "mla_kv_a_gain", "w_uq", "w_ukv", "mla_q_norm_gain",
                "mla_k_norm_gain", "gdn_conv_w", "gdn_a_log", "gdn_dt_bias", "gdn_out_norm_gain", "w_out")


def _pack_small(d):
    rows = []
    for name, n in SMALL:
        a = d[name].reshape(-1).astype(F32)
        n_pad = -(-n // LANE) * LANE
        rows.append(jnp.pad(a, (0, n_pad - n)).reshape(n_pad // LANE, LANE))
    packed = jnp.concatenate(rows, axis=0)
    return jnp.pad(packed, ((0, -packed.shape[0] % SUBLANE), (0, 0)))


def _unpack_small(packed):
    out, row = {}, 0
    for name, n in SMALL:
        n_rows = -(-n // LANE)
        out[name] = packed[row:row + n_rows].reshape(-1)[:n].reshape(1, n)
        row += n_rows
    return out


def kernel(x, positions, norm_gain, w_in, mla_q_a_gain, mla_kv_a_gain, w_uq, w_ukv, mla_q_norm_gain, mla_k_norm_gain, gdn_conv_w, gdn_a_log, gdn_dt_bias, gdn_out_norm_gain, w_out, loss_target, m_norm_gain, m_w_in, m_mla_q_a_gain, m_mla_kv_a_gain, m_w_uq, m_w_ukv, m_mla_q_norm_gain, m_mla_k_norm_gain, m_gdn_conv_w, m_gdn_a_log, m_gdn_dt_bias, m_gdn_out_norm_gain, m_w_out, v_norm_gain, v_w_in, v_mla_q_a_gain, v_mla_kv_a_gain, v_w_uq, v_w_ukv, v_mla_q_norm_gain, v_mla_k_norm_gain, v_gdn_conv_w, v_gdn_a_log, v_gdn_dt_bias, v_gdn_out_norm_gain, v_w_out):
    w = dict(norm_gain=norm_gain, w_in=w_in, mla_q_a_gain=mla_q_a_gain, mla_kv_a_gain=mla_kv_a_gain, w_uq=w_uq,
             w_ukv=w_ukv, mla_q_norm_gain=mla_q_norm_gain, mla_k_norm_gain=mla_k_norm_gain, gdn_conv_w=gdn_conv_w,
             gdn_a_log=gdn_a_log, gdn_dt_bias=gdn_dt_bias, gdn_out_norm_gain=gdn_out_norm_gain, w_out=w_out)
    m = dict(norm_gain=m_norm_gain, w_in=m_w_in, mla_q_a_gain=m_mla_q_a_gain, mla_kv_a_gain=m_mla_kv_a_gain,
             w_uq=m_w_uq, w_ukv=m_w_ukv, mla_q_norm_gain=m_mla_q_norm_gain, mla_k_norm_gain=m_mla_k_norm_gain,
             gdn_conv_w=m_gdn_conv_w, gdn_a_log=m_gdn_a_log, gdn_dt_bias=m_gdn_dt_bias,
             gdn_out_norm_gain=m_gdn_out_norm_gain, w_out=m_w_out)
    v = dict(norm_gain=v_norm_gain, w_in=v_w_in, mla_q_a_gain=v_mla_q_a_gain, mla_kv_a_gain=v_mla_kv_a_gain,
             w_uq=v_w_uq, w_ukv=v_w_ukv, mla_q_norm_gain=v_mla_q_norm_gain, mla_k_norm_gain=v_mla_k_norm_gain,
             gdn_conv_w=v_gdn_conv_w, gdn_a_log=v_gdn_a_log, gdn_dt_bias=v_gdn_dt_bias,
             gdn_out_norm_gain=v_gdn_out_norm_gain, w_out=v_w_out)
    t_len = x.shape[1]

    shards = [w[n][0] if n == "gdn_conv_w" else w[n][0].astype(_BF) for n in SHARDED]
    xi, yi, ci = _place()
    c_idx = jnp.reshape(ci, (1,)).astype(jnp.int32)
    q_idx = jnp.reshape(2 * xi + yi, (1,)).astype(jnp.int32)
    flight = {}

    def cols_whole(g):
        return g.transpose(1, 0, 2).reshape(g.shape[1], N_DEV * g.shape[2])

    def col_blocks(g):
        return g.reshape(g.shape[0], N_DEV, g.shape[1] // N_DEV).transpose(1, 0, 2)

    (a_w_in,) = all_gather(shards[:1])
    p = {n: w[n] for n, _ in SMALL}
    p["w_in"] = arrange_w_in(cols_whole(a_w_in))
    flight["weights"], weights_token = exchange_start(
        "gather_start", gather_routes, N_DEV - 1, [s[None] for s in shards[1:]], n_slots=N_DEV)

    def late_weights(proj):
        _, landed = exchange_wait("gather_wait", gather_routes, flight["weights"], proj)
        me = 4 * xi + 2 * yi + ci
        a_w_uq, a_w_ukv, a_cw, a_w_out = [lax.dynamic_update_slice(land, s[None], (me, 0, 0))
                                          for land, s in zip(landed, shards[1:])]
        return {"w_uq": arrange_w_uq(cols_whole(a_w_uq)), "w_ukv": cols_whole(a_w_ukv),
                "gdn_conv_w": cols_whole(a_cw), "w_out": a_w_out.reshape(N_DEV * a_w_out.shape[1], a_w_out.shape[2])}

    early, parts = SHARDED[1:], {}

    def core_stage(tag, names, blocks):
        flight["cores" + tag], token = exchange_start("cores_start" + tag, core_routes, 4, blocks)
        flight["names" + tag] = names
        return token

    def chip_stage(tag, after):
        blocks, landed = exchange_wait("cores_wait" + tag, core_routes, flight["cores" + tag], after)
        for n, g, r in zip(flight["names" + tag], blocks, landed):
            parts[n] = add_core_parts("add_" + n, g, r, c_idx, F32 if n == "gdn_conv_w" else _BF)
        wires = [parts[n][1] for n in flight["names" + tag]]
        flight["chips" + tag], token = exchange_start("chips_start" + tag, chip_routes, 3, wires)
        return token

    def on_early_grads(grads):
        return core_stage("_early", early, [
            col_blocks(unarrange_w_uq(grads["w_uq"])), col_blocks(grads["w_ukv"]), col_blocks(grads["gdn_conv_w"]),
            grads["w_out"].reshape(N_DEV, D_MODEL // N_DEV, D_MODEL)])

    def on_d_proj(d_proj):
        return chip_stage("_early", d_proj)

    def on_weight_grads(grads):
        return core_stage("", SHARDED[:1], [col_blocks(unarrange_w_in(grads["w_in"]))])

    def on_d_xn(d_xn):
        return chip_stage("", d_xn)

    pos = positions.reshape(t_len, 1).astype(F32)
    loss, grad_x, grads = local_step(x.reshape(t_len, D_MODEL), pos, loss_target.reshape(t_len, D_MODEL), p,
                                     on_early_grads=on_early_grads, on_d_proj=on_d_proj,
                                     on_weight_grads=on_weight_grads, on_d_xn=on_d_xn,
                                     first_after=weights_token, late_weights=late_weights)
    loss = lax.psum(loss, ("x", "y", "c"))
    out = {}
    small_all = gather_small(_pack_small(grads))
    res = adamw_small(small_all, _pack_small(w), _pack_small(m), _pack_small(v))
    unpacked = [_unpack_small(a) for a in res]
    for n, _ in SMALL:
        out[n] = [u[n] for u in unpacked]

    _, from_chips_early = exchange_wait("chips_wait_early", chip_routes, flight["chips_early"], res[0])
    _, from_chips = exchange_wait("chips_wait", chip_routes, flight["chips"], res[0])
    for n, rcv in zip(SHARDED, from_chips + from_chips_early):
        prt = parts[n][0]
        shape = w[n].shape
        res = adamw_sharded("adamw_" + n, prt, rcv, q_idx, w[n].reshape(shape[-2:]), m[n].reshape(shape[-2:]),
                            v[n].reshape(shape[-2:]))
        out[n] = [a.reshape(shape) for a in res]

    return (loss, grad_x.reshape(x.shape), *[out[n][0] for n in WEIGHT_ORDER], *[out[n][1] for n in WEIGHT_ORDER],
            *[out[n][2] for n in WEIGHT_ORDER], *[out[n][3] for n in WEIGHT_ORDER])
```

```python
import functools

import jax
import jax.numpy as jnp
from jax import lax
from jax.experimental import pallas as pl
from jax.experimental.pallas import tpu as pltpu

F32 = jnp.float32
_BF = jnp.bfloat16

D_MODEL = 2048
HEADS = 8
NOPE = 128
ROPE = 64
QK = NOPE + ROPE
Q_LORA = 512
KV_LORA = 256
HEAD_PAD = 256
GDN_DIM = 128
WIDTH = HEADS * 128
CONV_W = 4
CHUNK = 64
ROPE_THETA = 10000.0
EPS = 1e-6
N_DEV = 8
LANE = 128
SUBLANE = 8
VMEM_LIMIT = 48 * 1024 * 1024

ADAM_LR, ADAM_B1, ADAM_B2, ADAM_EPS, ADAM_WD, ADAM_STEP = 0.001, 0.9, 0.999, 1e-08, 0.01, 10

P_MGATE, P_GQ, P_GK, P_GV, P_GGATE = 0, 1024, 2048, 3072, 4096
P_CQ, P_CKV, P_KR, P_GAB = 5120, 5632, 5888, 6016
R_SPLITS = (512, 256, 64, 1024, 1024, 1024, 1024, 8, 8, 1024)


def _cparams(*sem):
    return pltpu.CompilerParams(dimension_semantics=sem, vmem_limit_bytes=VMEM_LIMIT)


def _d_nn(a, b):
    return jnp.dot(a.astype(_BF), b.astype(_BF), preferred_element_type=F32)


def _d_nt(a, b):
    return lax.dot_general(a.astype(_BF), b.astype(_BF), (((1,), (1,)), ((), ())), preferred_element_type=F32)


def _d_tn(a, b):
    return lax.dot_general(a.astype(_BF), b.astype(_BF), (((0,), (0,)), ((), ())), preferred_element_type=F32)


_NN3 = (((2,), (1,)), ((0,), (0,)))
_NT3 = (((2,), (2,)), ((0,), (0,)))
_TN3 = (((1,), (1,)), ((0,), (0,)))


def _bdot(a, b, dims, hi):
    if hi:
        return lax.dot_general(a, b, dims, preferred_element_type=F32, precision=hi)
    return lax.dot_general(a.astype(_BF), b.astype(_BF), dims, preferred_element_type=F32)


def _batched_matmuls(hi):
    nn = jax.custom_vjp(lambda a, b: _bdot(a, b, _NN3, hi))
    nt = jax.custom_vjp(lambda a, b: _bdot(a, b, _NT3, hi))
    tn = jax.custom_vjp(lambda a, b: _bdot(a, b, _TN3, hi))
    nn.defvjp(lambda a, b: (_bdot(a, b, _NN3, hi), (a, b)),
              lambda r, g: (_bdot(g, r[1], _NT3, hi), _bdot(r[0], g, _TN3, hi)))
    nt.defvjp(lambda a, b: (_bdot(a, b, _NT3, hi), (a, b)),
              lambda r, g: (_bdot(g, r[1], _NN3, hi), _bdot(g, r[0], _TN3, hi)))
    tn.defvjp(lambda a, b: (_bdot(a, b, _TN3, hi), (a, b)),
              lambda r, g: (_bdot(r[1], g, _NT3, hi), _bdot(r[0], g, _NN3, hi)))
    return nn, nt, tn


_bmm, _bmm_nt, _bmm_tn = _batched_matmuls(False)


def _split2(x):
    hi = x.astype(_BF)
    return hi, (x - hi.astype(F32)).astype(_BF)


def _pdot(a, b, mode):
    (a_hi, a_lo), (b_hi, b_lo) = _split2(a), _split2(b)
    a_ax, b_ax, dims = {"nn": (2, 1, _NN3), "nt": (2, 2, _NT3), "tn": (1, 1, _TN3)}[mode]
    lhs = jnp.concatenate([a_hi, a_lo, a_hi], axis=a_ax)
    rhs = jnp.concatenate([b_hi, b_hi, b_lo], axis=b_ax)
    return lax.dot_general(lhs, rhs, dims, preferred_element_type=F32)


def _packed_matmuls():
    nn = jax.custom_vjp(lambda a, b: _pdot(a, b, "nn"))
    nn.defvjp(lambda a, b: (_pdot(a, b, "nn"), (a, b)), lambda r, g: (_pdot(g, r[1], "nt"), _pdot(r[0], g, "tn")))
    return nn


_bh3 = _packed_matmuls()
_bh3_passes = _batched_matmuls(lax.Precision.HIGH)[0]


@functools.partial(jax.custom_vjp, nondiff_argnums=(1, 2))
def _roll(x, shift, axis):
    return pltpu.roll(x, shift, axis)


def _roll_fwd(x, shift, axis):
    return pltpu.roll(x, shift, axis), None


def _roll_bwd(shift, axis, _, g):
    n = g.shape[axis]
    return (pltpu.roll(g, (n - shift) % n, axis),)


_roll.defvjp(_roll_fwd, _roll_bwd)


def _rms(x, gain):
    return x * lax.rsqrt(jnp.mean(x * x, axis=-1, keepdims=True) + EPS) * gain


MM_TILE = 1024
MM_DEPTH = 2048


def matmul(name, a, b, mode, after=None):
    if mode == "nn":
        (m, k), (k2, n) = a.shape, b.shape
    elif mode == "nt":
        (m, k), (n, k2) = a.shape, b.shape
    else:
        (k, m), (k2, n) = a.shape, b.shape
    assert k == k2, (name, a.shape, b.shape)
    tm, tn, tk = min(MM_TILE, m), min(MM_TILE, n), min(MM_DEPTH, k)
    assert m % tm == 0 and n % tn == 0 and k % tk == 0, (name, m, n, k)
    dot = {"nn": _d_nn, "nt": _d_nt, "tn": _d_tn}[mode]

    def body(a_ref, b_ref, *rest):
        o_ref = rest[-1]
        kk = pl.program_id(2)
        part = dot(a_ref[...], b_ref[...])

        @pl.when(kk == 0)
        def _():
            o_ref[...] = part

        @pl.when(kk != 0)
        def _():
            o_ref[...] += part

    if mode == "nn":
        a_spec = pl.BlockSpec((tm, tk), lambda j, i, kk: (i, kk))
        b_spec = pl.BlockSpec((tk, tn), lambda j, i, kk: (kk, j))
    elif mode == "nt":
        a_spec = pl.BlockSpec((tm, tk), lambda j, i, kk: (i, kk))
        b_spec = pl.BlockSpec((tn, tk), lambda j, i, kk: (j, kk))
    else:
        a_spec = pl.BlockSpec((tk, tm), lambda j, i, kk: (kk, i))
        b_spec = pl.BlockSpec((tk, tn), lambda j, i, kk: (kk, j))
    return pl.pallas_call(
        body,
        name=name,
        grid=(n // tn, m // tm, k // tk),
        in_specs=[a_spec, b_spec] + ([] if after is None else [pl.BlockSpec(memory_space=pl.ANY)]),
        out_specs=pl.BlockSpec((tm, tn), lambda j, i, kk: (i, j)),
        out_shape=jax.ShapeDtypeStruct((m, n), F32),
        compiler_params=_cparams("parallel", "parallel", "arbitrary"),
    )(*((a, b) if after is None else (a, b, after)))


def matmul_pieces(name, pieces, b, after=None):
    n_p, (t_len, width), n = len(pieces), pieces[0].shape, b.shape[0]
    assert width == MM_TILE and n_p % 2 == 0 and all(p.shape == pieces[0].shape for p in pieces)
    tile, tn = min(MM_TILE, t_len), min(MM_TILE, n)
    extra = [] if after is None else [after]

    def body(*refs):
        b_ref, o_ref, kk = refs[n_p], refs[-1], pl.program_id(2)
        for s in range(n_p // 2):
            @pl.when(kk == s)
            def _(s=s):
                part = _d_nt(refs[2 * s][...], b_ref[:, :width]) + _d_nt(refs[2 * s + 1][...], b_ref[:, width:])
                if s == 0:
                    o_ref[...] = part
                else:
                    o_ref[...] += part

    return pl.pallas_call(
        body,
        name=name,
        grid=(n // tn, t_len // tile, n_p // 2),
        in_specs=[pl.BlockSpec((tile, width), lambda j, i, kk: (i, 0))] * n_p
        + [pl.BlockSpec((tn, 2 * width), lambda j, i, kk: (j, kk))] + [pl.BlockSpec(memory_space=pl.ANY)] * len(extra),
        out_specs=pl.BlockSpec((tile, tn), lambda j, i, kk: (i, j)),
        out_shape=jax.ShapeDtypeStruct((t_len, n), F32),
        compiler_params=_cparams("parallel", "parallel", "arbitrary"),
    )(*pieces, b, *extra)


def out_proj_loss(mixed, w_out, x, tgt):
    (m, k), n = mixed.shape, w_out.shape[1]
    tm, tn = min(MM_TILE // 2, m), min(MM_TILE, n)
    assert m % tm == 0 and n % tn == 0

    def body(a_ref, b_ref, x_ref, t_ref, dy_ref, dy_mx_ref, loss_ref):
        first = (pl.program_id(0) == 0) & (pl.program_id(1) == 0)
        e = x_ref[...] + _d_nn(a_ref[...], b_ref[...]) - t_ref[...]
        dy = e * (1.0 / D_MODEL)
        dy_ref[...] = dy
        dy_mx_ref[...] = dy.astype(dy_mx_ref.dtype)
        part = jnp.zeros((SUBLANE, LANE), F32) + 0.5 * jnp.sum(e * e) * (1.0 / D_MODEL)

        @pl.when(first)
        def _():
            loss_ref[...] = part

        @pl.when(jnp.logical_not(first))
        def _():
            loss_ref[...] += part

    tile = pl.BlockSpec((tm, tn), lambda j, i: (i, j))
    return pl.pallas_call(
        body,
        name="out_proj_loss",
        grid=(n // tn, m // tm),
        in_specs=[pl.BlockSpec((tm, k), lambda j, i: (i, 0)), pl.BlockSpec((k, tn), lambda j, i: (0, j)), tile, tile],
        out_specs=[tile, tile, pl.BlockSpec((SUBLANE, LANE), lambda j, i: (0, 0))],
        out_shape=[jax.ShapeDtypeStruct((m, n), F32), jax.ShapeDtypeStruct((m, n), _BF),
                   jax.ShapeDtypeStruct((SUBLANE, LANE), F32)],
        compiler_params=_cparams("arbitrary", "arbitrary"),
    )(mixed, w_out, x, tgt)


def rowwise(name, fn, t_len, tile, row_in, full_in, row_out, acc_out=(), carries=(), reverse=False):
    tile = min(tile, t_len)
    n = t_len // tile
    assert t_len % tile == 0 and tile % SUBLANE == 0
    n_in, n_ro, n_acc, n_car = len(row_in) + len(full_in), len(row_out), len(acc_out), len(carries)

    def ti(i):
        return (n - 1 - i) if reverse else i

    in_specs, args = [], []
    for arr, kind in row_in:
        if kind[0] == "r":
            in_specs.append(pl.BlockSpec((tile, kind[1]), lambda i, c=kind[2]: (ti(i), c)))
        elif kind[0] == "h":
            in_specs.append(pl.BlockSpec((arr.shape[0], tile, arr.shape[2]), lambda i: (0, ti(i), 0)))
        else:
            in_specs.append(pl.BlockSpec(
                (SUBLANE, kind[1]), lambda i, c=kind[2]: (jnp.maximum(ti(i) * (tile // SUBLANE) - 1, 0), c)))
        args.append(arr)
    for arr in full_in:
        in_specs.append(pl.BlockSpec(arr.shape, lambda i, nd=arr.ndim: (0,) * nd))
        args.append(arr)
    out_specs, out_shape = [], []
    for kind in row_out:
        if kind[0] == "r":
            out_specs.append(pl.BlockSpec((tile, kind[1]), lambda i: (ti(i), 0)))
            out_shape.append(jax.ShapeDtypeStruct((t_len, kind[1]), kind[2]))
        else:
            out_specs.append(pl.BlockSpec((kind[1], tile, kind[2]), lambda i: (0, ti(i), 0)))
            out_shape.append(jax.ShapeDtypeStruct((kind[1], t_len, kind[2]), kind[3]))
    for shp in acc_out:
        out_specs.append(pl.BlockSpec(shp, lambda i, nd=len(shp): (0,) * nd))
        out_shape.append(jax.ShapeDtypeStruct(shp, F32))

    def body(*refs):
        in_refs = refs[:n_in]
        ro_refs = refs[n_in:n_in + n_ro]
        acc_refs = refs[n_in + n_ro:n_in + n_ro + n_acc]
        car_refs = refs[n_in + n_ro + n_acc:]
        step = pl.program_id(0)
        if n_car:
            @pl.when(step == 0)
            def _():
                for r in car_refs:
                    r[...] = jnp.zeros_like(r)
        vals = [r[...].astype(F32) for r in in_refs] + [r[...] for r in car_refs]
        outs = fn(ti(step), *vals)
        assert len(outs) == n_ro + n_acc + n_car, (name, len(outs))
        for r, o in zip(ro_refs, outs[:n_ro]):
            r[...] = o.astype(r.dtype)
        for r, o in zip(acc_refs, outs[n_ro:n_ro + n_acc]):
            @pl.when(step == 0)
            def _(r=r, o=o):
                r[...] = o

            @pl.when(step != 0)
            def _(r=r, o=o):
                r[...] += o
        for r, o in zip(car_refs, outs[n_ro + n_acc:]):
            r[...] = o

    res = pl.pallas_call(
        body,
        name=name,
        grid=(n,),
        in_specs=in_specs,
        out_specs=out_specs,
        out_shape=out_shape,
        scratch_shapes=[pltpu.VMEM(s, F32) for s in carries],
        compiler_params=_cparams("arbitrary"),
    )(*args)
    return list(res)


def _vjp_fn(fn, n_diff, n_out):
    def g(i, *a):
        ins, cts = a[:len(a) - n_out], a[len(a) - n_out:]
        diff, rest = ins[:n_diff], ins[n_diff:]
        _, pull = jax.vjp(lambda *d: tuple(fn(i, *d, *rest)), *diff)
        return tuple(pull(tuple(cts)))

    return g


def f_rms_x(i, x, gain):
    return (_rms(x, gain),)


def f_lat(i, cq, ckv, gq, gkv):
    return _rms(cq, gq), _rms(ckv, gkv)


def _rope_tables(pos, invf):
    ang = pos * invf
    lane = lax.broadcasted_iota(jnp.int32, (1, LANE), 1)
    cosv, sinv = jnp.cos(ang), jnp.sin(ang)
    half = ROPE // 2
    c = jnp.where(lane < ROPE, cosv, 0.0)
    sa = jnp.where(lane < half, -sinv, 0.0)
    sb = jnp.where((lane >= half) & (lane < ROPE), sinv, 0.0)
    return c, sa, sb


def _rope(xh, tabs):
    c, sa, sb = tabs
    half = ROPE // 2
    return xh * c + _roll(xh, LANE - half, 1) * sa + _roll(xh, half, 1) * sb


def f_head(i, q_raw, kv_raw, kr, qg, kg, pos, invf):
    tabs = _rope_tables(pos, invf)
    qs, ks, vs = [], [], []
    kr_ss = jnp.sum(kr * kr, axis=-1, keepdims=True)
    kr_rot = _rope(kr * kg[:, NOPE:], tabs)
    for h in range(HEADS):
        lo = q_raw[:, HEAD_PAD * h:HEAD_PAD * h + NOPE]
        hi = q_raw[:, HEAD_PAD * h + NOPE:HEAD_PAD * (h + 1)]
        ss = jnp.sum(lo * lo, axis=-1, keepdims=True) + jnp.sum(hi * hi, axis=-1, keepdims=True)
        r = lax.rsqrt(ss * (1.0 / QK) + EPS)
        qs.append(jnp.concatenate([lo * r * qg[:, :NOPE], _rope(hi * r * qg[:, NOPE:], tabs)], axis=1))
        lo = kv_raw[:, 2 * NOPE * h:2 * NOPE * h + NOPE]
        ss = jnp.sum(lo * lo, axis=-1, keepdims=True) + kr_ss
        r = lax.rsqrt(ss * (1.0 / QK) + EPS)
        ks.append(jnp.concatenate([lo * r * kg[:, :NOPE], kr_rot * r], axis=1))
        vs.append(kv_raw[:, 2 * NOPE * h + NOPE:2 * NOPE * (h + 1)])
    return jnp.stack(qs), jnp.stack(ks), jnp.stack(vs)


def f_mix(i, o_mla, mgate, o_gdn, ggate, og):
    parts = [o_mla * jax.nn.silu(mgate)]
    for h in range(HEADS):
        parts.append(_rms(o_gdn[h], og) * jax.nn.silu(ggate[:, LANE * h:LANE * (h + 1)]))
    return (jnp.concatenate(parts, axis=1),)


def _row(a, j):
    rows = lax.broadcasted_iota(jnp.int32, a.shape, 0)
    return jnp.sum(jnp.where(rows == j, a, 0.0), axis=0, keepdims=True)


def _shift_rows(x, halo, d):
    xs = _roll(x, d, 0)
    hs = _roll(halo, d, 0)
    r8 = lax.broadcasted_iota(jnp.int32, hs.shape, 0)
    top = jnp.where(r8 < d, hs, xs[:SUBLANE])
    return jnp.concatenate([top, xs[SUBLANE:]], axis=0)


def _conv_silu(x, halo, w):
    y = _row(w, CONV_W - 1) * x
    for j in range(CONV_W - 1):
        y = y + _row(w, j) * _shift_rows(x, halo, CONV_W - 1 - j)
    return jax.nn.silu(y)


def _head_select(offset):
    r = lax.broadcasted_iota(jnp.int32, (LANE, WIDTH), 0)
    c = lax.broadcasted_iota(jnp.int32, (LANE, WIDTH), 1)
    return (r == offset + lax.shift_right_logical(c, 7)).astype(_BF)


def _split3(x):
    x1 = x.astype(_BF)
    r1 = x - x1.astype(F32)
    x2 = r1.astype(_BF)
    return x1, x2, (r1 - x2.astype(F32)).astype(_BF)


@jax.custom_vjp
def _spread(x, sel):
    return _d_nn(jnp.concatenate(_split3(x), axis=1), jnp.concatenate([sel, sel, sel], axis=0))


def _spread_fwd(x, sel):
    return _spread(x, sel), sel


def _spread_bwd(sel, g):
    g1, g2, g3 = _split3(g)
    return _d_nt(g1, sel) + _d_nt(g2, sel) + _d_nt(g3, sel), jnp.zeros_like(sel)


_spread.defvjp(_spread_fwd, _spread_bwd)


def f_gdn_pre(i, gq, gk, gv, gab, hq, hk, hv, cwq, cwk, cwv, alog, dtb):
    live = jnp.where(i == 0, 0.0, 1.0)
    q = _conv_silu(gq, hq * live, cwq)
    k = _conv_silu(gk, hk * live, cwk)
    v = _conv_silu(gv, hv * live, cwv)
    g = _spread(-jnp.exp(alog) * jax.nn.softplus(gab + dtb), _head_select(0))
    beta = _spread(jax.nn.sigmoid(gab), _head_select(HEADS))
    qs, ks, vs, gs, bs = [], [], [], [], []
    for h in range(HEADS):
        sl = slice(LANE * h, LANE * (h + 1))
        qh, kh = q[:, sl], k[:, sl]
        qs.append(qh * lax.rsqrt(jnp.sum(qh * qh, axis=-1, keepdims=True) + EPS) * (GDN_DIM ** -0.5))
        ks.append(kh * lax.rsqrt(jnp.sum(kh * kh, axis=-1, keepdims=True) + EPS))
        vs.append(v[:, sl])
        gs.append(g[:, sl])
        bs.append(beta[:, sl])
    return jnp.stack(qs), jnp.stack(ks), jnp.stack(vs), jnp.stack(gs), jnp.stack(bs)


def gdn_pre_bwd(i, gq, gk, gv, gab, hq, hk, hv, dq, dk, dv, dg, db, cwq, cwk, cwv, alog, dtb, cq, ck, cv):
    grads = _vjp_fn(f_gdn_pre, 12, 5)(i, gq, gk, gv, gab, hq, hk, hv, cwq, cwk, cwv, alog, dtb, dq, dk, dv, dg, db)
    dgq, dgk, dgv, dgab, dhq, dhk, dhv, dcwq, dcwk, dcwv, dalog, ddtb = grads

    def add_tail(dx, carry):
        return jnp.concatenate([dx[:-SUBLANE], dx[-SUBLANE:] + carry], axis=0)

    return (add_tail(dgq, cq), add_tail(dgk, ck), add_tail(dgv, cv), dgab,
            dcwq, dcwk, dcwv, dalog, ddtb, dhq, dhk, dhv)


def _flash_tile(t_len):
    return min(512, t_len)


FLASH_HEADS = 4
FLASH_BWD_HEADS = 2
LOG2E = 1.4426950408889634


def _causal(rows0, shape):
    r = rows0 + lax.broadcasted_iota(jnp.int32, shape, 0)
    c = lax.broadcasted_iota(jnp.int32, shape, 1)
    return c <= r


def flash_fwd(q, k, v):
    h_n, t_len, _ = q.shape
    tq = _flash_tile(t_len)
    nq = t_len // tq
    hb = FLASH_HEADS
    kw = 2 if nq % 2 == 0 else 1
    tk = kw * tq
    c2 = (QK ** -0.5) * LOG2E
    pairs = [(i, j) for i in range(nq) for j in range(i // kw + 1)]
    qt = jnp.array([p[0] for p in pairs], jnp.int32)
    kt = jnp.array([p[1] for p in pairs], jnp.int32)

    def body(qt_ref, kt_ref, q_ref, k_ref, v_ref, o_ref, lse_ref, m_s, acc_s):
        step = pl.program_id(1)
        qi, kj = qt_ref[step], kt_ref[step]
        last = qi // kw

        @pl.when(kj == 0)
        def _():
            m_s[...] = jnp.full_like(m_s, -jnp.inf)
            acc_s[...] = jnp.zeros_like(acc_s)

        def tile(diagonal, cols):
            s = _bdot(q_ref[...], k_ref[:, :cols], _NT3, False) * c2
            if diagonal:
                s = jnp.where(_causal((qi % kw) * tq, (tq, cols))[None], s, -jnp.inf)
            m_old = m_s[...]
            m_new = jnp.maximum(m_old, jnp.max(s, axis=-1, keepdims=True))
            p = jnp.exp2(s - m_new).astype(_BF)
            v_ones = jnp.concatenate([v_ref[:, :cols], jnp.ones((hb, cols, LANE), _BF)], axis=2)
            acc_s[...] = jnp.exp2(m_old - m_new) * acc_s[...] + _bdot(p, v_ones, _NN3, False)
            m_s[...] = m_new

        @pl.when(kj < last)
        def _():
            tile(False, tk)

        @pl.when((kj == last) & (qi % kw == 0))
        def _():
            tile(True, tq)

        if kw > 1:
            @pl.when((kj == last) & (qi % kw != 0))
            def _():
                tile(True, tk)

        @pl.when(kj == last)
        def _():
            acc = acc_s[...]
            l_sum = acc[:, :, LANE:]
            o = acc[:, :, :LANE] / l_sum
            for hh in range(hb):
                o_ref[:, LANE * hh:LANE * (hh + 1)] = o[hh]
            lse_ref[...] = m_s[...] + jnp.log2(jnp.max(l_sum, axis=-1, keepdims=True))

    return pl.pallas_call(
        body,
        name="flash_fwd",
        grid_spec=pltpu.PrefetchScalarGridSpec(
            num_scalar_prefetch=2,
            grid=(h_n // hb, qt.shape[0]),
            in_specs=[
                pl.BlockSpec((hb, tq, HEAD_PAD), lambda h, s, qt_ref, kt_ref: (h, qt_ref[s], 0)),
                pl.BlockSpec((hb, tk, HEAD_PAD), lambda h, s, qt_ref, kt_ref: (h, kt_ref[s], 0)),
                pl.BlockSpec((hb, tk, LANE), lambda h, s, qt_ref, kt_ref: (h, kt_ref[s], 0)),
            ],
            out_specs=[
                pl.BlockSpec((tq, hb * LANE), lambda h, s, qt_ref, kt_ref: (qt_ref[s], h)),
                pl.BlockSpec((hb, tq, 1), lambda h, s, qt_ref, kt_ref: (h, qt_ref[s], 0)),
            ],
            scratch_shapes=[pltpu.VMEM((hb, tq, 1), F32), pltpu.VMEM((hb, tq, 2 * LANE), F32)],
        ),
        out_shape=[jax.ShapeDtypeStruct((t_len, h_n * LANE), F32), jax.ShapeDtypeStruct((h_n, t_len, 1), F32)],
        compiler_params=_cparams("parallel", "arbitrary"),
    )(qt, kt, q, k, v)


def flash_bwd(q, k, v, do, lse, delta):
    h_n, t_len, _ = q.shape
    tq = _flash_tile(t_len)
    nq = t_len // tq
    hb = FLASH_BWD_HEADS
    kw = 2 if nq % 2 == 0 else 1
    tk = kw * tq
    pairs = [(i, j) for j in range(nq // kw) for i in range(kw * j, nq)]
    n_steps = len(pairs)
    qt = jnp.array([p[0] for p in pairs], jnp.int32)
    kt = jnp.array([p[1] for p in pairs], jnp.int32)
    scale = QK ** -0.5
    c2 = scale * LOG2E

    def body(qt_ref, kt_ref, q_ref, k_ref, v_ref, do_ref, lse_ref, dl_ref, dq_hbm, dk_ref, dv_ref, dq_s, dq_sem):
        group, step = pl.program_id(0), pl.program_id(1)
        qi, kj = qt_ref[step], kt_ref[step]

        @pl.when(step == 0)
        def _():
            dq_s[...] = jnp.zeros_like(dq_s)

        def tile(diagonal, cols):
            qb, kb = q_ref[...], k_ref[:, :cols]
            dob = jnp.stack([do_ref[:, LANE * hh:LANE * (hh + 1)] for hh in range(hb)])
            p = jnp.exp2(_bdot(qb, kb, _NT3, False) * c2 - lse_ref[...])
            if diagonal:
                p = jnp.where(_causal((qi % kw) * tq, (tq, cols))[None], p, 0.0)
            dv = _bdot(p, dob, _TN3, False)
            ds = p * (_bdot(dob, v_ref[:, :cols], _NT3, False) - dl_ref[...]) * scale
            dk = _bdot(ds, qb, _TN3, False)
            dq_s[:, pl.ds(pl.multiple_of(qi * tq, tq), tq), :] += _bdot(ds, kb, _NN3, False)
            return dk, dv

        @pl.when(qi == kw * kj)
        def _():
            dk, dv = tile(True, tq)
            dk_ref[:, :tq], dv_ref[:, :tq] = dk, dv
            if kw > 1:
                dk_ref[:, tq:] = jnp.zeros((hb, tk - tq, HEAD_PAD), F32)
                dv_ref[:, tq:] = jnp.zeros((hb, tk - tq, LANE), F32)

        if kw > 1:
            @pl.when((qi != kw * kj) & (qi // kw == kj))
            def _():
                dk, dv = tile(True, tk)
                dk_ref[...] += dk
                dv_ref[...] += dv

        @pl.when(qi // kw > kj)
        def _():
            dk, dv = tile(False, tk)
            dk_ref[...] += dk
            dv_ref[...] += dv

        @pl.when(step == n_steps - 1)
        def _():
            out = pltpu.make_async_copy(dq_s, dq_hbm.at[pl.ds(group * hb, hb)], dq_sem)
            out.start()
            out.wait()

    def qmap(h, s, qt_ref, kt_ref):
        return (h, qt_ref[s], 0)

    def kmap(h, s, qt_ref, kt_ref):
        return (h, kt_ref[s], 0)

    return pl.pallas_call(
        body,
        name="flash_bwd",
        grid_spec=pltpu.PrefetchScalarGridSpec(
            num_scalar_prefetch=2,
            grid=(h_n // hb, n_steps),
            in_specs=[
                pl.BlockSpec((hb, tq, HEAD_PAD), qmap),
                pl.BlockSpec((hb, tk, HEAD_PAD), kmap),
                pl.BlockSpec((hb, tk, LANE), kmap),
                pl.BlockSpec((tq, hb * LANE), lambda h, s, qt_ref, kt_ref: (qt_ref[s], h)),
                pl.BlockSpec((hb, tq, 1), qmap),
                pl.BlockSpec((hb, tq, 1), qmap),
            ],
            out_specs=[
                pl.BlockSpec(memory_space=pl.ANY),
                pl.BlockSpec((hb, tk, HEAD_PAD), kmap),
                pl.BlockSpec((hb, tk, LANE), kmap),
            ],
            scratch_shapes=[pltpu.VMEM((hb, t_len, HEAD_PAD), F32), pltpu.SemaphoreType.DMA],
        ),
        out_shape=[
            jax.ShapeDtypeStruct((h_n, t_len, HEAD_PAD), F32),
            jax.ShapeDtypeStruct((h_n, t_len, HEAD_PAD), F32),
            jax.ShapeDtypeStruct((h_n, t_len, LANE), F32),
        ],
        compiler_params=_cparams("parallel", "arbitrary"),
    )(qt, kt, q, k, v, do, lse, delta)


def _tri_ones(h_n):
    ii = lax.broadcasted_iota(jnp.int32, (h_n, CHUNK, CHUNK), 1)
    jj = lax.broadcasted_iota(jnp.int32, (h_n, CHUNK, CHUNK), 2)
    return (ii >= jj).astype(_BF)


@jax.custom_vjp
def _chunk_cumsum(gb):
    tri = _tri_ones(gb.shape[0])
    return _bdot(jnp.concatenate([tri, tri, tri], axis=2), jnp.concatenate(_split3(gb), axis=1), _NN3, False)


def _chunk_cumsum_bwd(_, ct):
    tri = _tri_ones(ct.shape[0])
    return (_bdot(jnp.concatenate([tri, tri, tri], axis=1), jnp.concatenate(_split3(ct), axis=1), _TN3, False),)


_chunk_cumsum.defvjp(lambda gb: (_chunk_cumsum(gb), None), _chunk_cumsum_bwd)


@jax.custom_vjp
def _pair_diff(gcb):
    g1, g2, g3 = _split3(gcb)
    lane = lax.broadcasted_iota(jnp.int32, (1, 1, LANE), 2)
    one, zero = jnp.ones((), _BF), jnp.zeros((), _BF)
    a = jnp.where(lane == 0, g1, jnp.where(lane == 1, g2, jnp.where(lane == 2, g3, jnp.where(lane < 6, one, zero))))
    b = jnp.where(lane < 3, one, jnp.where(lane == 3, -g1, jnp.where(lane == 4, -g2, jnp.where(lane == 5, -g3, zero))))
    return _bdot(a, b, _NT3, False)


def _pair_diff_bwd(_, ct):
    parts = _split3(ct)
    ones = jnp.ones((ct.shape[0], 3 * CHUNK, LANE), _BF)
    rows = _bdot(jnp.concatenate(parts, axis=2), ones, _NN3, False)
    cols = _bdot(jnp.concatenate(parts, axis=1), ones, _TN3, False)
    lane = lax.broadcasted_iota(jnp.int32, (1, 1, LANE), 2)
    return (jnp.where(lane == 0, rows - cols, 0.0),)


_pair_diff.defvjp(lambda gcb: (_pair_diff(gcb), None), _pair_diff_bwd)


@jax.custom_vjp
def _saved_inverse(lmat, inv):
    return inv


def _saved_inverse_bwd(inv, g):
    return -_pdot(_pdot(inv, g, "tn"), inv, "nt"), jnp.zeros_like(inv)


_saved_inverse.defvjp(lambda lmat, inv: (inv, inv), _saved_inverse_bwd)


def gdn_step(s, q, k, v, gb, bb, inv_saved=None):
    c = CHUNK
    ii = lax.broadcasted_iota(jnp.int32, (1, c, c), 1)
    jj = lax.broadcasted_iota(jnp.int32, (1, c, c), 2)
    incl, strict = ii >= jj, ii > jj
    gcb = _chunk_cumsum(gb)
    diff = _pair_diff(gcb)
    decay = jnp.where(incl, jnp.exp(jnp.where(incl, diff, 0.0)), 0.0)
    kb, vb = k * bb, v * bb
    egc = jnp.exp(gcb)
    lmat = jnp.where(strict, _bmm_nt(kb, k) * decay, 0.0)
    if inv_saved is None:
        mm3 = _bh3_passes
        inv = (ii == jj).astype(F32) - lmat
        pw = mm3(lmat, lmat)
        for step in range(5):
            inv = inv + mm3(inv, pw)
            if step < 4:
                pw = mm3(pw, pw)
    else:
        mm3 = _bh3
        inv = _saved_inverse(lmat, inv_saved)
    u = mm3(inv, vb)
    w = mm3(inv, kb * egc)
    attn = _bmm_nt(q, k) * decay
    qd = q * egc
    g_end = jnp.sum(gb, axis=1, keepdims=True)
    kd = k * jnp.exp(g_end - gcb)
    v_new = u - _bmm(w, s)
    o = _bmm(qd, s) + _bmm(attn, v_new)
    s_new = s * jnp.exp(g_end) + _bmm_tn(kd, v_new)
    return s_new, o, inv


def gdn_fwd(q, k, v, gb, bb):
    h_n, t_len, d = q.shape
    n = t_len // CHUNK
    blk = pl.BlockSpec((h_n, CHUNK, d), lambda i: (0, i, 0))

    def body(q_ref, k_ref, v_ref, g_ref, b_ref, o_ref, sall_ref, inv_ref, s_s):
        @pl.when(pl.program_id(0) == 0)
        def _():
            s_s[...] = jnp.zeros_like(s_s)

        s = s_s[...]
        sall_ref[0] = s
        s_s[...], o_ref[...], inv_ref[0] = gdn_step(s, q_ref[...], k_ref[...], v_ref[...], g_ref[...], b_ref[...])

    return pl.pallas_call(
        body,
        name="gdn_fwd",
        grid=(n,),
        in_specs=[blk] * 5,
        out_specs=[blk, pl.BlockSpec((1, h_n, d, d), lambda i: (i, 0, 0, 0)),
                   pl.BlockSpec((1, h_n, CHUNK, CHUNK), lambda i: (i, 0, 0, 0))],
        out_shape=[jax.ShapeDtypeStruct((h_n, t_len, d), F32), jax.ShapeDtypeStruct((n, h_n, d, d), F32),
                   jax.ShapeDtypeStruct((n, h_n, CHUNK, CHUNK), F32)],
        scratch_shapes=[pltpu.VMEM((h_n, d, d), F32)],
        compiler_params=_cparams("arbitrary"),
    )(q, k, v, gb, bb)


def gdn_bwd(q, k, v, gb, bb, s_all, inv_all, do):
    h_n, t_len, d = q.shape
    n = t_len // CHUNK
    blk = pl.BlockSpec((h_n, CHUNK, d), lambda i: (0, n - 1 - i, 0))

    def body(q_ref, k_ref, v_ref, g_ref, b_ref, sall_ref, inv_ref, do_ref, dq_ref, dk_ref, dv_ref, dg_ref, db_ref,
             ds_s):
        @pl.when(pl.program_id(0) == 0)
        def _():
            ds_s[...] = jnp.zeros_like(ds_s)

        inv = inv_ref[0]
        _, pull = jax.vjp(lambda *a: gdn_step(*a, inv_saved=inv)[:2], sall_ref[0], q_ref[...], k_ref[...], v_ref[...],
                          g_ref[...], b_ref[...])
        ds_s[...], dq_ref[...], dk_ref[...], dv_ref[...], dg_ref[...], db_ref[...] = pull((ds_s[...], do_ref[...]))

    return pl.pallas_call(
        body,
        name="gdn_bwd",
        grid=(n,),
        in_specs=[blk] * 5 + [pl.BlockSpec((1, h_n, d, d), lambda i: (n - 1 - i, 0, 0, 0)),
                              pl.BlockSpec((1, h_n, CHUNK, CHUNK), lambda i: (n - 1 - i, 0, 0, 0)), blk],
        out_specs=[blk] * 5,
        out_shape=[jax.ShapeDtypeStruct((h_n, t_len, d), F32)] * 5,
        scratch_shapes=[pltpu.VMEM((h_n, d, d), F32)],
        compiler_params=_cparams("arbitrary"),
    )(q, k, v, gb, bb, s_all, inv_all, do)


def _pad_cols(a, n):
    return jnp.pad(a, ((0, 0), (0, n - a.shape[1])))


def arrange_w_in(w):
    pieces, start = [], 0
    for n in R_SPLITS:
        pieces.append(w[:, start:start + n])
        start += n
    cq, ckv, kr, mgate, gq, gk, gv, ga, gb, ggate = pieces
    return jnp.concatenate([mgate, gq, gk, gv, ggate, cq, ckv, _pad_cols(kr, LANE),
                            _pad_cols(jnp.concatenate([ga, gb], axis=1), LANE)], axis=1)


def unarrange_w_in(pieces):
    mgate, gq, gk, gv, ggate, rest = pieces

    def cols(start, n):
        return rest[:, start - P_CQ:start - P_CQ + n]
    return jnp.concatenate([cols(P_CQ, Q_LORA), cols(P_CKV, KV_LORA), cols(P_KR, ROPE), mgate, gq, gk, gv,
                            cols(P_GAB, HEADS), cols(P_GAB + HEADS, HEADS), ggate], axis=1)


def arrange_w_uq(w):
    w = w.reshape(w.shape[0], HEADS, QK)
    return jnp.pad(w, ((0, 0), (0, 0), (0, HEAD_PAD - QK))).reshape(w.shape[0], HEADS * HEAD_PAD)


def unarrange_w_uq(g):
    return g.reshape(g.shape[0], HEADS, HEAD_PAD)[:, :, :QK].reshape(g.shape[0], HEADS * QK)


def local_step(x, pos, tgt, p, on_early_grads=None, on_d_proj=None, on_weight_grads=None, on_d_xn=None,
               first_after=None, late_weights=None):
    t_len = x.shape[0]
    w_in = p["w_in"]
    norm_gain = p["norm_gain"].reshape(1, D_MODEL)
    qa_gain = p["mla_q_a_gain"].reshape(1, Q_LORA)
    kva_gain = p["mla_kv_a_gain"].reshape(1, KV_LORA)
    qg = _pad_cols(p["mla_q_norm_gain"].reshape(1, QK), HEAD_PAD)
    kg = _pad_cols(p["mla_k_norm_gain"].reshape(1, QK), HEAD_PAD)
    alog = _pad_cols(p["gdn_a_log"].reshape(1, HEADS), LANE)
    dtb = _pad_cols(p["gdn_dt_bias"].reshape(1, HEADS), LANE)
    og = p["gdn_out_norm_gain"].reshape(1, GDN_DIM)
    half = ROPE // 2
    inv_freq = jnp.power(ROPE_THETA, -jnp.arange(half, dtype=F32) / half)
    invf = _pad_cols(jnp.concatenate([inv_freq, inv_freq]).reshape(1, ROPE), LANE)

    rt = 256
    rt_light = 512
    r = "r"
    first_after = jnp.zeros((SUBLANE, LANE), F32) if first_after is None else first_after
    (xn,) = rowwise("rms_x", lambda i, x_, gain_, after_: f_rms_x(i, x_, gain_), t_len, rt_light, [(x, (r, D_MODEL, 0))],
                    [norm_gain, first_after], [(r, D_MODEL, _BF)])
    proj = matmul("proj", xn, w_in, "nn")
    if late_weights is not None:
        p = {**p, **late_weights(proj)}
    w_uq, w_ukv, w_out = p["w_uq"], p["w_ukv"], p["w_out"]
    cw = p["gdn_conv_w"].reshape(CONV_W, 3 * WIDTH)
    cwq, cwk, cwv = cw[:, :WIDTH], cw[:, WIDTH:2 * WIDTH], cw[:, 2 * WIDTH:]
    cq_in = (proj, (r, Q_LORA, P_CQ // Q_LORA))
    ckv_in = (proj, (r, KV_LORA, P_CKV // KV_LORA))
    kr_in = (proj, (r, LANE, P_KR // LANE))
    mgate_in = (proj, (r, WIDTH, P_MGATE // WIDTH))
    ggate_in = (proj, (r, WIDTH, P_GGATE // WIDTH))
    gqkv_in = [(proj, (r, WIDTH, P_GQ // WIDTH)), (proj, (r, WIDTH, P_GK // WIDTH)), (proj, (r, WIDTH, P_GV // WIDTH))]
    gab_in = (proj, (r, LANE, P_GAB // LANE))
    halos = [(proj, ("halo", WIDTH, P_GQ // WIDTH)), (proj, ("halo", WIDTH, P_GK // WIDTH)),
             (proj, ("halo", WIDTH, P_GV // WIDTH))]

    q_lat, kv_lat = rowwise("lat", f_lat, t_len, rt_light, [cq_in, ckv_in], [qa_gain, kva_gain],
                            [(r, Q_LORA, _BF), (r, KV_LORA, _BF)])
    q_raw = matmul("q_up", q_lat, w_uq, "nn")
    kv_raw = matmul("kv_up", kv_lat, w_ukv, "nn")
    wide = HEADS * HEAD_PAD
    head_in = [(q_raw, (r, wide, 0)), (kv_raw, (r, wide, 0)), kr_in]
    pos_in = (pos, (r, 1, 0))
    q_full, k_full, v_mla = rowwise(
        "head", lambda i, qr, kvr, kr, ps, qg_, kg_, iv: f_head(i, qr, kvr, kr, qg_, kg_, ps, iv), t_len, rt,
        head_in + [pos_in], [qg, kg, invf],
        [("h", HEADS, HEAD_PAD, _BF), ("h", HEADS, HEAD_PAD, _BF), ("h", HEADS, LANE, _BF)])
    o_mla, lse = flash_fwd(q_full, k_full, v_mla)

    pre_in = gqkv_in + [gab_in] + halos
    pre_full = [cwq, cwk, cwv, alog, dtb]
    hkind = ("h", HEADS, GDN_DIM, F32)
    gq_n, gk_n, gv_n, g_b, b_b = rowwise("gdn_pre", f_gdn_pre, t_len, rt, pre_in, pre_full, [hkind] * 5)
    o_gdn, s_all, inv_all = gdn_fwd(gq_n, gk_n, gv_n, g_b, b_b)

    mix_in = [(o_mla, (r, WIDTH, 0)), mgate_in, (o_gdn, ("h",)), ggate_in]
    (mixed,) = rowwise("mix", f_mix, t_len, rt_light, mix_in, [og], [(r, 2 * WIDTH, _BF)])
    dy, dy_mx, loss_acc = out_proj_loss(mixed, w_out, x, tgt)
    loss = loss_acc[0, 0]

    d_mixed = matmul("d_mixed", dy_mx, w_out, "nt")
    g_w_out = matmul("g_w_out", mixed, dy_mx, "tn")

    def mix_bwd(i, o_mla_, mgate_, o_gdn_, ggate_, d_mixed_, og_):
        do_mla_, d_mgate_, do_gdn_, d_ggate_, g_og_ = _vjp_fn(f_mix, 5, 1)(i, o_mla_, mgate_, o_gdn_, ggate_, og_, d_mixed_)
        delta_ = jnp.stack([jnp.sum(o_mla_[:, LANE * h:LANE * (h + 1)] * do_mla_[:, LANE * h:LANE * (h + 1)],
                                    axis=-1, keepdims=True) for h in range(HEADS)])
        return do_mla_, d_mgate_, do_gdn_, d_ggate_, delta_, g_og_

    do_mla, d_mgate, do_gdn, d_ggate, delta, g_og = rowwise(
        "mix_bwd", mix_bwd, t_len, rt, mix_in + [(d_mixed, (r, 2 * WIDTH, 0))], [og],
        [(r, WIDTH, F32), (r, WIDTH, _BF), hkind, (r, WIDTH, _BF), ("h", HEADS, 1, F32)], [(1, GDN_DIM)])
    dq_n, dk_n, dv_n, dg_b, db_b = gdn_bwd(gq_n, gk_n, gv_n, g_b, b_b, s_all, inv_all, do_gdn)
    cts_in = [(a, ("h",)) for a in (dq_n, dk_n, dv_n, dg_b, db_b)]
    d_gq, d_gk, d_gv, d_gab, g_cwq, g_cwk, g_cwv, g_alog, g_dtb = rowwise(
        "gdn_pre_bwd", gdn_pre_bwd, t_len, rt, pre_in + cts_in, pre_full,
        [(r, WIDTH, _BF)] * 3 + [(r, LANE, _BF)],
        [(CONV_W, WIDTH)] * 3 + [(1, LANE)] * 2, carries=[(SUBLANE, WIDTH)] * 3, reverse=True)

    dq_full, dk_full, dv_mla = flash_bwd(q_full, k_full, v_mla, do_mla, lse, delta)
    head_cts = [(a, ("h",)) for a in (dq_full, dk_full, dv_mla)]

    def head_bwd(i, q_raw_, kv_raw_, kr_, pos_, dq_, dk_, dv_, qg_, kg_, invf_):
        return _vjp_fn(f_head, 5, 3)(i, q_raw_, kv_raw_, kr_, qg_, kg_, pos_, invf_, dq_, dk_, dv_)

    dq_raw, dkv_raw, d_kr, g_qg, g_kg = rowwise(
        "head_bwd", head_bwd, t_len, rt, head_in + [pos_in] + head_cts, [qg, kg, invf],
        [(r, wide, _BF), (r, wide, _BF), (r, LANE, _BF)], [(1, HEAD_PAD), (1, HEAD_PAD)])
    dq_lat = matmul("dq_lat", dq_raw, w_uq, "nt")
    g_w_uq = matmul("g_w_uq", q_lat, dq_raw, "tn")
    dkv_lat = matmul("dkv_lat", dkv_raw, w_ukv, "nt")
    g_w_ukv = matmul("g_w_ukv", kv_lat, dkv_raw, "tn")
    grads = {
        "w_uq": g_w_uq, "w_ukv": g_w_ukv, "gdn_conv_w": jnp.concatenate([g_cwq, g_cwk, g_cwv], axis=1),
        "w_out": g_w_out, "mla_q_norm_gain": g_qg[:, :QK], "mla_k_norm_gain": g_kg[:, :QK],
        "gdn_a_log": g_alog[:, :HEADS], "gdn_dt_bias": g_dtb[:, :HEADS], "gdn_out_norm_gain": g_og,
    }
    after = jnp.zeros((SUBLANE, LANE), F32) if on_early_grads is None else on_early_grads(grads)

    def lat_bwd(i, cq_, ckv_, dql_, dkl_, gq_, gkv_, after_):
        return _vjp_fn(f_lat, 4, 2)(i, cq_, ckv_, gq_, gkv_, dql_, dkl_)

    d_cq, d_ckv, grads["mla_q_a_gain"], grads["mla_kv_a_gain"] = rowwise(
        "lat_bwd", lat_bwd, t_len, rt_light, [cq_in, ckv_in, (dq_lat, (r, Q_LORA, 0)), (dkv_lat, (r, KV_LORA, 0))],
        [qa_gain, kva_gain, after], [(r, Q_LORA, _BF), (r, KV_LORA, _BF)], [(1, Q_LORA), (1, KV_LORA)])

    d_proj = [d_mgate, d_gq, d_gk, d_gv, d_ggate, jnp.concatenate([d_cq, d_ckv, d_kr, d_gab], axis=1)]
    after = None if on_d_proj is None else on_d_proj(d_proj[-1])
    grads["w_in"] = [matmul("g_w_in_%d" % j, xn, piece, "tn", after=after if j == 0 else None)
                     for j, piece in enumerate(d_proj)]
    after = None if on_weight_grads is None else on_weight_grads(grads)
    d_xn = matmul_pieces("d_xn", d_proj, w_in, after=after)
    after = jnp.zeros((SUBLANE, LANE), F32) if on_d_xn is None else on_d_xn(d_xn)

    def rms_x_bwd(i, x_, dxn_, dy_, gain_, after_):
        dx, dgain = _vjp_fn(f_rms_x, 2, 1)(i, x_, gain_, dxn_)
        return dx + dy_, dgain

    grad_x, grads["norm_gain"] = rowwise(
        "rms_x_bwd", rms_x_bwd, t_len, rt, [(x, (r, D_MODEL, 0)), (d_xn, (r, D_MODEL, 0)), (dy, (r, D_MODEL, 0))],
        [norm_gain, after], [(r, D_MODEL, F32)], [(1, D_MODEL)])
    return loss, grad_x, grads


MESH = pl.DeviceIdType.MESH
ANY = pl.BlockSpec(memory_space=pl.ANY)
CHIP_FLIPS = ((1, 0), (0, 1), (1, 1))


def _place():
    return lax.axis_index("x"), lax.axis_index("y"), lax.axis_index("c")


def _flip(v, f):
    return 1 - v if f else v


def all_gather(shards):
    n_arr = len(shards)

    def body(*refs):
        x_refs, o_refs = refs[:n_arr], refs[n_arr:2 * n_arr]
        send_sems, recv_sems, local_sems = refs[2 * n_arr:]
        x, y, c = _place()
        me, sibling = (x, y, c), (x, y, 1 - c)
        chips = [(_flip(x, fx), _flip(y, fy)) for fx, fy in CHIP_FLIPS]

        def copy(a, k, block, to, src=None):
            px, py, pc = block
            dst = o_refs[a].at[4 * px + 2 * py + pc]
            return pltpu.make_async_remote_copy(
                src_ref=dst if src is None else src, dst_ref=dst, send_sem=send_sems.at[a, k],
                recv_sem=recv_sems.at[a, k], device_id=to, device_id_type=MESH)

        mine, first, passed = [], [], []
        for a in range(n_arr):
            cp = pltpu.make_async_copy(x_refs[a], o_refs[a].at[4 * x + 2 * y + c], local_sems.at[a])
            cp.start()
            mine.append(cp)
            first.append(copy(a, 0, me, sibling, src=x_refs[a]))
            first += [copy(a, 1 + j, me, (*chip, c), src=x_refs[a]) for j, chip in enumerate(chips)]
        for cp in first:
            cp.start()
        for j, chip in enumerate(chips):
            for a in range(n_arr):
                copy(a, 1 + j, (*chip, c), me).wait_recv()
                cp = copy(a, 4 + j, (*chip, c), sibling)
                cp.start()
                passed.append(cp)
        for a in range(n_arr):
            copy(a, 0, sibling, me).wait_recv()
            for j, chip in enumerate(chips):
                copy(a, 4 + j, (*chip, 1 - c), me).wait_recv()
        for cp in first + passed:
            cp.wait_send()
        for cp in mine:
            cp.wait()

    return pl.pallas_call(
        body,
        name="all_gather",
        out_shape=[jax.ShapeDtypeStruct((N_DEV,) + s.shape, s.dtype) for s in shards],
        in_specs=[ANY] * n_arr,
        out_specs=[ANY] * n_arr,
        scratch_shapes=[pltpu.SemaphoreType.DMA((n_arr, 7)), pltpu.SemaphoreType.DMA((n_arr, 7)),
                        pltpu.SemaphoreType.DMA((n_arr,))],
    )(*shards)


HBM = pl.BlockSpec(memory_space=pltpu.HBM)
SEMS = pl.BlockSpec(memory_space=pltpu.SEMAPHORE)
SIDE_EFFECT = pltpu.SideEffectType.DATAFLOW_SIDE_EFFECTING


def core_routes(x, y, c):
    return [(2 * q + (1 - c), q, (x, y, 1 - c)) for q in range(4)]


def chip_routes(x, y, c):
    routes = []
    for j, (fx, fy) in enumerate(CHIP_FLIPS):
        px, py = _flip(x, fx), _flip(y, fy)
        routes.append((2 * px + py, j, (px, py, c)))
    return routes


def _route_copies(routes, n_routes, src_refs, land_refs, sems):
    x, y, c = _place()
    n_copies = len(src_refs) * n_routes
    return [pltpu.make_async_remote_copy(src_ref=src.at[s], dst_ref=land.at[d], send_sem=sems[a * n_routes + k],
                                         recv_sem=sems[n_copies + a * n_routes + k], device_id=dev,
                                         device_id_type=MESH)
            for a, (src, land) in enumerate(zip(src_refs, land_refs)) for k, (s, d, dev) in enumerate(routes(x, y, c))]


def gather_routes(x, y, c):
    me = 4 * x + 2 * y + c
    return [(0, me, (_flip(x, (k >> 2) & 1), _flip(y, (k >> 1) & 1), _flip(c, k & 1))) for k in range(1, N_DEV)]


def exchange_start(name, routes, n_routes, srcs, n_slots=None):
    n = len(srcs)
    n_sems = 2 * n * n_routes
    lands = [lax.empty((n_routes if n_slots is None else n_slots,) + s.shape[1:], s.dtype) for s in srcs]

    def body(*refs):
        for cp in _route_copies(routes, n_routes, refs[:n], refs[n:2 * n], refs[2 * n:2 * n + n_sems]):
            cp.start()
        refs[-1][...] = jnp.zeros_like(refs[-1])

    res = pl.pallas_call(
        body,
        name=name,
        out_shape=(*[pltpu.SemaphoreType.DMA(())] * n_sems, *[pltpu.HBM(a.shape, a.dtype) for a in srcs + lands],
                   jax.ShapeDtypeStruct((SUBLANE, LANE), F32)),
        in_specs=[HBM] * (2 * n),
        out_specs=(*[SEMS] * n_sems, *[HBM] * (2 * n), pl.BlockSpec(memory_space=pltpu.VMEM)),
        input_output_aliases={i: n_sems + i for i in range(2 * n)},
        compiler_params=pltpu.CompilerParams(has_side_effects=SIDE_EFFECT),
    )(*[pltpu.with_memory_space_constraint(a, pltpu.HBM) for a in srcs + lands])
    return (res[:n_sems], res[n_sems:-1]), res[-1]


def exchange_wait(name, routes, handle, after):
    sems, thru = handle
    n, n_sems = len(thru) // 2, len(sems)
    n_routes = n_sems // (2 * n)

    def body(*refs):
        for cp in _route_copies(routes, n_routes, refs[:n], refs[n:2 * n], refs[2 * n:2 * n + n_sems]):
            cp.wait_send()
            cp.wait_recv()

    res = pl.pallas_call(
        body,
        name=name,
        out_shape=tuple(pltpu.HBM(a.shape, a.dtype) for a in thru),
        in_specs=[HBM] * (2 * n) + [SEMS] * n_sems + [ANY],
        out_specs=tuple([HBM] * (2 * n)),
        input_output_aliases={i: i for i in range(2 * n)},
        compiler_params=pltpu.CompilerParams(has_side_effects=SIDE_EFFECT),
    )(*thru, *sems, after)
    return list(res[:n]), list(res[n:])


def gather_small(v):
    def body(v_ref, o_ref, send_sems, recv_sems, local_sem):
        x, y, c = _place()
        me = 4 * x + 2 * y + c
        mine = pltpu.make_async_copy(v_ref, o_ref.at[me], local_sem)
        mine.start()
        copies = []
        for k in range(1, N_DEV):
            fx, fy, fc = (k >> 2) & 1, (k >> 1) & 1, k & 1
            cp = pltpu.make_async_remote_copy(
                src_ref=v_ref, dst_ref=o_ref.at[me], send_sem=send_sems.at[k - 1], recv_sem=recv_sems.at[k - 1],
                device_id=(_flip(x, fx), _flip(y, fy), _flip(c, fc)), device_id_type=MESH)
            cp.start()
            copies.append(cp)
        for cp in copies:
            cp.wait()
        mine.wait()

    return pl.pallas_call(
        body,
        name="gather_small",
        out_shape=jax.ShapeDtypeStruct((N_DEV,) + v.shape, v.dtype),
        in_specs=[ANY],
        out_specs=ANY,
        scratch_shapes=[pltpu.SemaphoreType.DMA((N_DEV - 1,)), pltpu.SemaphoreType.DMA((N_DEV - 1,)),
                        pltpu.SemaphoreType.DMA],
    )(v)


def _row_tile(rows):
    for t in (256, 128, 64, 32, 16, 8):
        if rows % t == 0:
            return t
    return rows


def add_core_parts(name, g, recv, c_idx, wire):
    _, rows, cols = g.shape
    tr = _row_tile(rows)

    def body(c_ref, g_ref, r_ref, o_ref, w_ref):
        part = g_ref[...] + r_ref[...]
        o_ref[...] = part
        w_ref[...] = part.astype(w_ref.dtype)

    blk = pl.BlockSpec((1, tr, cols), lambda q, i, c_ref: (q, i, 0))
    return pl.pallas_call(
        body,
        name=name,
        grid_spec=pltpu.PrefetchScalarGridSpec(
            num_scalar_prefetch=1,
            grid=(4, rows // tr),
            in_specs=[pl.BlockSpec((1, tr, cols), lambda q, i, c_ref: (2 * q + c_ref[0], i, 0)), blk],
            out_specs=[blk, blk],
        ),
        out_shape=[jax.ShapeDtypeStruct((4, rows, cols), F32), jax.ShapeDtypeStruct((4, rows, cols), wire)],
        compiler_params=_cparams("parallel", "parallel"),
    )(c_idx, g, recv)


def _adamw(w, g, m, v):
    m = ADAM_B1 * m + (1.0 - ADAM_B1) * g
    v = ADAM_B2 * v + (1.0 - ADAM_B2) * (g * g)
    m_hat = m / (1.0 - ADAM_B1 ** ADAM_STEP)
    v_hat = v / (1.0 - ADAM_B2 ** ADAM_STEP)
    delta = -ADAM_LR * (m_hat / (jnp.sqrt(v_hat) + ADAM_EPS) + ADAM_WD * w)
    return delta, m, v


def adamw_sharded(name, parts, recv, q_idx, w, m, v):
    rows, cols = w.shape
    tr = _row_tile(rows)

    def body(q_ref, p_ref, r_ref, w_ref, m_ref, v_ref, g_out, d_out, m_out, v_out):
        g = p_ref[0] + r_ref[0].astype(F32) + r_ref[1].astype(F32) + r_ref[2].astype(F32)
        d, m_new, v_new = _adamw(w_ref[...], g, m_ref[...], v_ref[...])
        g_out[...], d_out[...], m_out[...], v_out[...] = g, d, m_new, v_new

    blk = pl.BlockSpec((tr, cols), lambda i, q_ref: (i, 0))
    return pl.pallas_call(
        body,
        name=name,
        grid_spec=pltpu.PrefetchScalarGridSpec(
            num_scalar_prefetch=1,
            grid=(rows // tr,),
            in_specs=[pl.BlockSpec((1, tr, cols), lambda i, q_ref: (q_ref[0], i, 0)),
                      pl.BlockSpec((3, tr, cols), lambda i, q_ref: (0, i, 0)), blk, blk, blk],
            out_specs=[blk] * 4,
        ),
        out_shape=[jax.ShapeDtypeStruct((rows, cols), F32)] * 4,
        compiler_params=_cparams("parallel"),
    )(q_idx, parts, recv, w, m, v)


def adamw_small(gathered, w, m, v):
    def body(g_ref, w_ref, m_ref, v_ref, g_out, d_out, m_out, v_out):
        g = g_ref[0]
        for j in range(1, N_DEV):
            g = g + g_ref[j]
        d, m_new, v_new = _adamw(w_ref[...], g, m_ref[...], v_ref[...])
        g_out[...], d_out[...], m_out[...], v_out[...] = g, d, m_new, v_new

    return pl.pallas_call(body, name="adamw_small", out_shape=[jax.ShapeDtypeStruct(w.shape, F32)] * 4)(gathered, w, m, v)


SHARDED = ("w_in", "w_uq", "w_ukv", "gdn_conv_w", "w_out")
SMALL = (("norm_gain", D_MODEL), ("mla_q_a_gain", Q_LORA), ("mla_kv_a_gain", KV_LORA), ("mla_q_norm_gain", QK),
         ("mla_k_norm_gain", QK), ("gdn_a_log", HEADS), ("gdn_dt_bias", HEADS), ("gdn_out_norm_gain", GDN_DIM))
WEIGHT_ORDER = ("norm_gain", "w_in", "mla_q_a_gain", "mla_kv_a_gain", "w_uq", "w_ukv", "mla_q_norm_gain",
                "mla_k_norm_gain", "gdn_conv_w", "gdn_a_log", "gdn_dt_bias", "gdn_out_norm_gain", "w_out")


def _pack_small(d):
    rows = []
    for name, n in SMALL:
        a = d[name].reshape(-1).astype(F32)
        n_pad = -(-n // LANE) * LANE
        rows.append(jnp.pad(a, (0, n_pad - n)).reshape(n_pad // LANE, LANE))
    packed = jnp.concatenate(rows, axis=0)
    return jnp.pad(packed, ((0, -packed.shape[0] % SUBLANE), (0, 0)))


def _unpack_small(packed):
    out, row = {}, 0
    for name, n in SMALL:
        n_rows = -(-n // LANE)
        out[name] = packed[row:row + n_rows].reshape(-1)[:n].reshape(1, n)
        row += n_rows
    return out


def kernel(x, positions, norm_gain, w_in, mla_q_a_gain, mla_kv_a_gain, w_uq, w_ukv, mla_q_norm_gain, mla_k_norm_gain, gdn_conv_w, gdn_a_log, gdn_dt_bias, gdn_out_norm_gain, w_out, loss_target, m_norm_gain, m_w_in, m_mla_q_a_gain, m_mla_kv_a_gain, m_w_uq, m_w_ukv, m_mla_q_norm_gain, m_mla_k_norm_gain, m_gdn_conv_w, m_gdn_a_log, m_gdn_dt_bias, m_gdn_out_norm_gain, m_w_out, v_norm_gain, v_w_in, v_mla_q_a_gain, v_mla_kv_a_gain, v_w_uq, v_w_ukv, v_mla_q_norm_gain, v_mla_k_norm_gain, v_gdn_conv_w, v_gdn_a_log, v_gdn_dt_bias, v_gdn_out_norm_gain, v_w_out):
    w = dict(norm_gain=norm_gain, w_in=w_in, mla_q_a_gain=mla_q_a_gain, mla_kv_a_gain=mla_kv_a_gain, w_uq=w_uq,
             w_ukv=w_ukv, mla_q_norm_gain=mla_q_norm_gain, mla_k_norm_gain=mla_k_norm_gain, gdn_conv_w=gdn_conv_w,
             gdn_a_log=gdn_a_log, gdn_dt_bias=gdn_dt_bias, gdn_out_norm_gain=gdn_out_norm_gain, w_out=w_out)
    m = dict(norm_gain=m_norm_gain, w_in=m_w_in, mla_q_a_gain=m_mla_q_a_gain, mla_kv_a_gain=m_mla_kv_a_gain,
             w_uq=m_w_uq, w_ukv=m_w_ukv, mla_q_norm_gain=m_mla_q_norm_gain, mla_k_norm_gain=m_mla_k_norm_gain,
             gdn_conv_w=m_gdn_conv_w, gdn_a_log=m_gdn_a_log, gdn_dt_bias=m_gdn_dt_bias,
             gdn_out_norm_gain=m_gdn_out_norm_gain, w_out=m_w_out)
    v = dict(norm_gain=v_norm_gain, w_in=v_w_in, mla_q_a_gain=v_mla_q_a_gain, mla_kv_a_gain=v_mla_kv_a_gain,
             w_uq=v_w_uq, w_ukv=v_w_ukv, mla_q_norm_gain=v_mla_q_norm_gain, mla_k_norm_gain=v_mla_k_norm_gain,
             gdn_conv_w=v_gdn_conv_w, gdn_a_log=v_gdn_a_log, gdn_dt_bias=v_gdn_dt_bias,
             gdn_out_norm_gain=v_gdn_out_norm_gain, w_out=v_w_out)
    t_len = x.shape[1]

    shards = [w[n][0] if n == "gdn_conv_w" else w[n][0].astype(_BF) for n in SHARDED]
    xi, yi, ci = _place()
    c_idx = jnp.reshape(ci, (1,)).astype(jnp.int32)
    q_idx = jnp.reshape(2 * xi + yi, (1,)).astype(jnp.int32)
    flight = {}

    def cols_whole(g):
        return g.transpose(1, 0, 2).reshape(g.shape[1], N_DEV * g.shape[2])

    def col_blocks(g):
        return g.reshape(g.shape[0], N_DEV, g.shape[1] // N_DEV).transpose(1, 0, 2)

    (a_w_in,) = all_gather(shards[:1])
    p = {n: w[n] for n, _ in SMALL}
    p["w_in"] = arrange_w_in(cols_whole(a_w_in))
    flight["weights"], weights_token = exchange_start(
        "gather_start", gather_routes, N_DEV - 1, [s[None] for s in shards[1:]], n_slots=N_DEV)

    def late_weights(proj):
        _, landed = exchange_wait("gather_wait", gather_routes, flight["weights"], proj)
        me = 4 * xi + 2 * yi + ci
        a_w_uq, a_w_ukv, a_cw, a_w_out = [lax.dynamic_update_slice(land, s[None], (me, 0, 0))
                                          for land, s in zip(landed, shards[1:])]
        return {"w_uq": arrange_w_uq(cols_whole(a_w_uq)), "w_ukv": cols_whole(a_w_ukv),
                "gdn_conv_w": cols_whole(a_cw), "w_out": a_w_out.reshape(N_DEV * a_w_out.shape[1], a_w_out.shape[2])}

    early, parts = SHARDED[1:], {}

    def core_stage(tag, names, blocks):
        flight["cores" + tag], token = exchange_start("cores_start" + tag, core_routes, 4, blocks)
        flight["names" + tag] = names
        return token

    def chip_stage(tag, after):
        blocks, landed = exchange_wait("cores_wait" + tag, core_routes, flight["cores" + tag], after)
        for n, g, r in zip(flight["names" + tag], blocks, landed):
            parts[n] = add_core_parts("add_" + n, g, r, c_idx, F32 if n == "gdn_conv_w" else _BF)
        wires = [parts[n][1] for n in flight["names" + tag]]
        flight["chips" + tag], token = exchange_start("chips_start" + tag, chip_routes, 3, wires)
        return token

    def on_early_grads(grads):
        return core_stage("_early", early, [
            col_blocks(unarrange_w_uq(grads["w_uq"])), col_blocks(grads["w_ukv"]), col_blocks(grads["gdn_conv_w"]),
            grads["w_out"].reshape(N_DEV, D_MODEL // N_DEV, D_MODEL)])

    def on_d_proj(d_proj):
        return chip_stage("_early", d_proj)

    def on_weight_grads(grads):
        return core_stage("", SHARDED[:1], [col_blocks(unarrange_w_in(grads["w_in"]))])

    def on_d_xn(d_xn):
        return chip_stage("", d_xn)

    pos = positions.reshape(t_len, 1).astype(F32)
    loss, grad_x, grads = local_step(x.reshape(t_len, D_MODEL), pos, loss_target.reshape(t_len, D_MODEL), p,
                                     on_early_grads=on_early_grads, on_d_proj=on_d_proj,
                                     on_weight_grads=on_weight_grads, on_d_xn=on_d_xn,
                                     first_after=weights_token, late_weights=late_weights)
    loss = lax.psum(loss, ("x", "y", "c"))
    out = {}
    small_all = gather_small(_pack_small(grads))
    res = adamw_small(small_all, _pack_small(w), _pack_small(m), _pack_small(v))
    unpacked = [_unpack_small(a) for a in res]
    for n, _ in SMALL:
        out[n] = [u[n] for u in unpacked]

    _, from_chips_early = exchange_wait("chips_wait_early", chip_routes, flight["chips_early"], res[0])
    _, from_chips = exchange_wait("chips_wait", chip_routes, flight["chips"], res[0])
    for n, rcv in zip(SHARDED, from_chips + from_chips_early):
        prt = parts[n][0]
        shape = w[n].shape
        res = adamw_sharded("adamw_" + n, prt, rcv, q_idx, w[n].reshape(shape[-2:]), m[n].reshape(shape[-2:]),
                            v[n].reshape(shape[-2:]))
        out[n] = [a.reshape(shape) for a in res]

    return (loss, grad_x.reshape(x.shape), *[out[n][0] for n in WEIGHT_ORDER], *[out[n][1] for n in WEIGHT_ORDER],
            *[out[n][2] for n in WEIGHT_ORDER], *[out[n][3] for n in WEIGHT_ORDER])
```

```python
import functools

import jax
import jax.numpy as jnp
from jax import lax
from jax.experimental import pallas as pl
from jax.experimental.pallas import tpu as pltpu

F32 = jnp.float32
_BF = jnp.bfloat16

D_MODEL = 2048
HEADS = 8
NOPE = 128
ROPE = 64
QK = NOPE + ROPE
Q_LORA = 512
KV_LORA = 256
HEAD_PAD = 256
GDN_DIM = 128
WIDTH = HEADS * 128
CONV_W = 4
CHUNK = 64
ROPE_THETA = 10000.0
EPS = 1e-6
N_DEV = 8
LANE = 128
SUBLANE = 8
VMEM_LIMIT = 48 * 1024 * 1024

ADAM_LR, ADAM_B1, ADAM_B2, ADAM_EPS, ADAM_WD, ADAM_STEP = 0.001, 0.9, 0.999, 1e-08, 0.01, 10

P_MGATE, P_GQ, P_GK, P_GV, P_GGATE = 0, 1024, 2048, 3072, 4096
P_CQ, P_CKV, P_KR, P_GAB = 5120, 5632, 5888, 6016
R_SPLITS = (512, 256, 64, 1024, 1024, 1024, 1024, 8, 8, 1024)


def _cparams(*sem):
    return pltpu.CompilerParams(dimension_semantics=sem, vmem_limit_bytes=VMEM_LIMIT)


def _d_nn(a, b):
    return jnp.dot(a.astype(_BF), b.astype(_BF), preferred_element_type=F32)


def _d_nt(a, b):
    return lax.dot_general(a.astype(_BF), b.astype(_BF), (((1,), (1,)), ((), ())), preferred_element_type=F32)


def _d_tn(a, b):
    return lax.dot_general(a.astype(_BF), b.astype(_BF), (((0,), (0,)), ((), ())), preferred_element_type=F32)


_NN3 = (((2,), (1,)), ((0,), (0,)))
_NT3 = (((2,), (2,)), ((0,), (0,)))
_TN3 = (((1,), (1,)), ((0,), (0,)))


def _bdot(a, b, dims, hi):
    if hi:
        return lax.dot_general(a, b, dims, preferred_element_type=F32, precision=hi)
    return lax.dot_general(a.astype(_BF), b.astype(_BF), dims, preferred_element_type=F32)


def _batched_matmuls(hi):
    nn = jax.custom_vjp(lambda a, b: _bdot(a, b, _NN3, hi))
    nt = jax.custom_vjp(lambda a, b: _bdot(a, b, _NT3, hi))
    tn = jax.custom_vjp(lambda a, b: _bdot(a, b, _TN3, hi))
    nn.defvjp(lambda a, b: (_bdot(a, b, _NN3, hi), (a, b)),
              lambda r, g: (_bdot(g, r[1], _NT3, hi), _bdot(r[0], g, _TN3, hi)))
    nt.defvjp(lambda a, b: (_bdot(a, b, _NT3, hi), (a, b)),
              lambda r, g: (_bdot(g, r[1], _NN3, hi), _bdot(g, r[0], _TN3, hi)))
    tn.defvjp(lambda a, b: (_bdot(a, b, _TN3, hi), (a, b)),
              lambda r, g: (_bdot(r[1], g, _NT3, hi), _bdot(r[0], g, _NN3, hi)))
    return nn, nt, tn


_bmm, _bmm_nt, _bmm_tn = _batched_matmuls(False)


def _split2(x):
    hi = x.astype(_BF)
    return hi, (x - hi.astype(F32)).astype(_BF)


def _pdot(a, b, mode):
    (a_hi, a_lo), (b_hi, b_lo) = _split2(a), _split2(b)
    a_ax, b_ax, dims = {"nn": (2, 1, _NN3), "nt": (2, 2, _NT3), "tn": (1, 1, _TN3)}[mode]
    lhs = jnp.concatenate([a_hi, a_lo, a_hi], axis=a_ax)
    rhs = jnp.concatenate([b_hi, b_hi, b_lo], axis=b_ax)
    return lax.dot_general(lhs, rhs, dims, preferred_element_type=F32)


def _packed_matmuls():
    nn = jax.custom_vjp(lambda a, b: _pdot(a, b, "nn"))
    nn.defvjp(lambda a, b: (_pdot(a, b, "nn"), (a, b)), lambda r, g: (_pdot(g, r[1], "nt"), _pdot(r[0], g, "tn")))
    return nn


_bh3 = _packed_matmuls()
_bh3_passes = _batched_matmuls(lax.Precision.HIGH)[0]


@functools.partial(jax.custom_vjp, nondiff_argnums=(1, 2))
def _roll(x, shift, axis):
    return pltpu.roll(x, shift, axis)


def _roll_fwd(x, shift, axis):
    return pltpu.roll(x, shift, axis), None


def _roll_bwd(shift, axis, _, g):
    n = g.shape[axis]
    return (pltpu.roll(g, (n - shift) % n, axis),)


_roll.defvjp(_roll_fwd, _roll_bwd)


def _rms(x, gain):
    return x * lax.rsqrt(jnp.mean(x * x, axis=-1, keepdims=True) + EPS) * gain


MM_TILE = 1024
MM_DEPTH = 2048


def matmul(name, a, b, mode, after=None):
    if mode == "nn":
        (m, k), (k2, n) = a.shape, b.shape
    elif mode == "nt":
        (m, k), (n, k2) = a.shape, b.shape
    else:
        (k, m), (k2, n) = a.shape, b.shape
    assert k == k2, (name, a.shape, b.shape)
    tm, tn, tk = min(MM_TILE, m), min(MM_TILE, n), min(MM_DEPTH, k)
    assert m % tm == 0 and n % tn == 0 and k % tk == 0, (name, m, n, k)
    dot = {"nn": _d_nn, "nt": _d_nt, "tn": _d_tn}[mode]

    def body(a_ref, b_ref, *rest):
        o_ref = rest[-1]
        kk = pl.program_id(2)
        part = dot(a_ref[...], b_ref[...])

        @pl.when(kk == 0)
        def _():
            o_ref[...] = part

        @pl.when(kk != 0)
        def _():
            o_ref[...] += part

    if mode == "nn":
        a_spec = pl.BlockSpec((tm, tk), lambda j, i, kk: (i, kk))
        b_spec = pl.BlockSpec((tk, tn), lambda j, i, kk: (kk, j))
    elif mode == "nt":
        a_spec = pl.BlockSpec((tm, tk), lambda j, i, kk: (i, kk))
        b_spec = pl.BlockSpec((tn, tk), lambda j, i, kk: (j, kk))
    else:
        a_spec = pl.BlockSpec((tk, tm), lambda j, i, kk: (kk, i))
        b_spec = pl.BlockSpec((tk, tn), lambda j, i, kk: (kk, j))
    return pl.pallas_call(
        body,
        name=name,
        grid=(n // tn, m // tm, k // tk),
        in_specs=[a_spec, b_spec] + ([] if after is None else [pl.BlockSpec(memory_space=pl.ANY)]),
        out_specs=pl.BlockSpec((tm, tn), lambda j, i, kk: (i, j)),
        out_shape=jax.ShapeDtypeStruct((m, n), F32),
        compiler_params=_cparams("parallel", "parallel", "arbitrary"),
    )(*((a, b) if after is None else (a, b, after)))


def matmul_pieces(name, pieces, b, after=None):
    n_p, (t_len, width), n = len(pieces), pieces[0].shape, b.shape[0]
    assert width == MM_TILE and n_p % 2 == 0 and all(p.shape == pieces[0].shape for p in pieces)
    tile, tn = min(MM_TILE, t_len), min(MM_TILE, n)
    extra = [] if after is None else [after]

    def body(*refs):
        b_ref, o_ref, kk = refs[n_p], refs[-1], pl.program_id(2)
        for s in range(n_p // 2):
            @pl.when(kk == s)
            def _(s=s):
                part = _d_nt(refs[2 * s][...], b_ref[:, :width]) + _d_nt(refs[2 * s + 1][...], b_ref[:, width:])
                if s == 0:
                    o_ref[...] = part
                else:
                    o_ref[...] += part

    return pl.pallas_call(
        body,
        name=name,
        grid=(n // tn, t_len // tile, n_p // 2),
        in_specs=[pl.BlockSpec((tile, width), lambda j, i, kk: (i, 0))] * n_p
        + [pl.BlockSpec((tn, 2 * width), lambda j, i, kk: (j, kk))] + [pl.BlockSpec(memory_space=pl.ANY)] * len(extra),
        out_specs=pl.BlockSpec((tile, tn), lambda j, i, kk: (i, j)),
        out_shape=jax.ShapeDtypeStruct((t_len, n), F32),
        compiler_params=_cparams("parallel", "parallel", "arbitrary"),
    )(*pieces, b, *extra)


def out_proj_loss(mixed, w_out, x, tgt):
    (m, k), n = mixed.shape, w_out.shape[1]
    tm, tn = min(MM_TILE // 2, m), min(MM_TILE, n)
    assert m % tm == 0 and n % tn == 0

    def body(a_ref, b_ref, x_ref, t_ref, dy_ref, dy_mx_ref, loss_ref):
        first = (pl.program_id(0) == 0) & (pl.program_id(1) == 0)
        e = x_ref[...] + _d_nn(a_ref[...], b_ref[...]) - t_ref[...]
        dy = e * (1.0 / D_MODEL)
        dy_ref[...] = dy
        dy_mx_ref[...] = dy.astype(dy_mx_ref.dtype)
        part = jnp.zeros((SUBLANE, LANE), F32) + 0.5 * jnp.sum(e * e) * (1.0 / D_MODEL)

        @pl.when(first)
        def _():
            loss_ref[...] = part

        @pl.when(jnp.logical_not(first))
        def _():
            loss_ref[...] += part

    tile = pl.BlockSpec((tm, tn), lambda j, i: (i, j))
    return pl.pallas_call(
        body,
        name="out_proj_loss",
        grid=(n // tn, m // tm),
        in_specs=[pl.BlockSpec((tm, k), lambda j, i: (i, 0)), pl.BlockSpec((k, tn), lambda j, i: (0, j)), tile, tile],
        out_specs=[tile, tile, pl.BlockSpec((SUBLANE, LANE), lambda j, i: (0, 0))],
        out_shape=[jax.ShapeDtypeStruct((m, n), F32), jax.ShapeDtypeStruct((m, n), _BF),
                   jax.ShapeDtypeStruct((SUBLANE, LANE), F32)],
        compiler_params=_cparams("arbitrary", "arbitrary"),
    )(mixed, w_out, x, tgt)


def rowwise(name, fn, t_len, tile, row_in, full_in, row_out, acc_out=(), carries=(), reverse=False):
    tile = min(tile, t_len)
    n = t_len // tile
    assert t_len % tile == 0 and tile % SUBLANE == 0
    n_in, n_ro, n_acc, n_car = len(row_in) + len(full_in), len(row_out), len(acc_out), len(carries)

    def ti(i):
        return (n - 1 - i) if reverse else i

    in_specs, args = [], []
    for arr, kind in row_in:
        if kind[0] == "r":
            in_specs.append(pl.BlockSpec((tile, kind[1]), lambda i, c=kind[2]: (ti(i), c)))
        elif kind[0] == "h":
            in_specs.append(pl.BlockSpec((arr.shape[0], tile, arr.shape[2]), lambda i: (0, ti(i), 0)))
        else:
            in_specs.append(pl.BlockSpec(
                (SUBLANE, kind[1]), lambda i, c=kind[2]: (jnp.maximum(ti(i) * (tile // SUBLANE) - 1, 0), c)))
        args.append(arr)
    for arr in full_in:
        in_specs.append(pl.BlockSpec(arr.shape, lambda i, nd=arr.ndim: (0,) * nd))
        args.append(arr)
    out_specs, out_shape = [], []
    for kind in row_out:
        if kind[0] == "r":
            out_specs.append(pl.BlockSpec((tile, kind[1]), lambda i: (ti(i), 0)))
            out_shape.append(jax.ShapeDtypeStruct((t_len, kind[1]), kind[2]))
        else:
            out_specs.append(pl.BlockSpec((kind[1], tile, kind[2]), lambda i: (0, ti(i), 0)))
            out_shape.append(jax.ShapeDtypeStruct((kind[1], t_len, kind[2]), kind[3]))
    for shp in acc_out:
        out_specs.append(pl.BlockSpec(shp, lambda i, nd=len(shp): (0,) * nd))
        out_shape.append(jax.ShapeDtypeStruct(shp, F32))

    def body(*refs):
        in_refs = refs[:n_in]
        ro_refs = refs[n_in:n_in + n_ro]
        acc_refs = refs[n_in + n_ro:n_in + n_ro + n_acc]
        car_refs = refs[n_in + n_ro + n_acc:]
        step = pl.program_id(0)
        if n_car:
            @pl.when(step == 0)
            def _():
                for r in car_refs:
                    r[...] = jnp.zeros_like(r)
        vals = [r[...].astype(F32) for r in in_refs] + [r[...] for r in car_refs]
        outs = fn(ti(step), *vals)
        assert len(outs) == n_ro + n_acc + n_car, (name, len(outs))
        for r, o in zip(ro_refs, outs[:n_ro]):
            r[...] = o.astype(r.dtype)
        for r, o in zip(acc_refs, outs[n_ro:n_ro + n_acc]):
            @pl.when(step == 0)
            def _(r=r, o=o):
                r[...] = o

            @pl.when(step != 0)
            def _(r=r, o=o):
                r[...] += o
        for r, o in zip(car_refs, outs[n_ro + n_acc:]):
            r[...] = o

    res = pl.pallas_call(
        body,
        name=name,
        grid=(n,),
        in_specs=in_specs,
        out_specs=out_specs,
        out_shape=out_shape,
        scratch_shapes=[pltpu.VMEM(s, F32) for s in carries],
        compiler_params=_cparams("arbitrary"),
    )(*args)
    return list(res)


def _vjp_fn(fn, n_diff, n_out):
    def g(i, *a):
        ins, cts = a[:len(a) - n_out], a[len(a) - n_out:]
        diff, rest = ins[:n_diff], ins[n_diff:]
        _, pull = jax.vjp(lambda *d: tuple(fn(i, *d, *rest)), *diff)
        return tuple(pull(tuple(cts)))

    return g


def f_rms_x(i, x, gain):
    return (_rms(x, gain),)


def f_lat(i, cq, ckv, gq, gkv):
    return _rms(cq, gq), _rms(ckv, gkv)


def _rope_tables(pos, invf):
    ang = pos * invf
    lane = lax.broadcasted_iota(jnp.int32, (1, LANE), 1)
    cosv, sinv = jnp.cos(ang), jnp.sin(ang)
    half = ROPE // 2
    c = jnp.where(lane < ROPE, cosv, 0.0)
    sa = jnp.where(lane < half, -sinv, 0.0)
    sb = jnp.where((lane >= half) & (lane < ROPE), sinv, 0.0)
    return c, sa, sb


def _rope(xh, tabs):
    c, sa, sb = tabs
    half = ROPE // 2
    return xh * c + _roll(xh, LANE - half, 1) * sa + _roll(xh, half, 1) * sb


def f_head(i, q_raw, kv_raw, kr, qg, kg, pos, invf):
    tabs = _rope_tables(pos, invf)
    qs, ks, vs = [], [], []
    kr_ss = jnp.sum(kr * kr, axis=-1, keepdims=True)
    kr_rot = _rope(kr * kg[:, NOPE:], tabs)
    for h in range(HEADS):
        lo = q_raw[:, HEAD_PAD * h:HEAD_PAD * h + NOPE]
        hi = q_raw[:, HEAD_PAD * h + NOPE:HEAD_PAD * (h + 1)]
        ss = jnp.sum(lo * lo, axis=-1, keepdims=True) + jnp.sum(hi * hi, axis=-1, keepdims=True)
        r = lax.rsqrt(ss * (1.0 / QK) + EPS)
        qs.append(jnp.concatenate([lo * r * qg[:, :NOPE], _rope(hi * r * qg[:, NOPE:], tabs)], axis=1))
        lo = kv_raw[:, 2 * NOPE * h:2 * NOPE * h + NOPE]
        ss = jnp.sum(lo * lo, axis=-1, keepdims=True) + kr_ss
        r = lax.rsqrt(ss * (1.0 / QK) + EPS)
        ks.append(jnp.concatenate([lo * r * kg[:, :NOPE], kr_rot * r], axis=1))
        vs.append(kv_raw[:, 2 * NOPE * h + NOPE:2 * NOPE * (h + 1)])
    return jnp.stack(qs), jnp.stack(ks), jnp.stack(vs)


def f_mix(i, o_mla, mgate, o_gdn, ggate, og):
    parts = [o_mla * jax.nn.silu(mgate)]
    for h in range(HEADS):
        parts.append(_rms(o_gdn[h], og) * jax.nn.silu(ggate[:, LANE * h:LANE * (h + 1)]))
    return (jnp.concatenate(parts, axis=1),)


def _row(a, j):
    rows = lax.broadcasted_iota(jnp.int32, a.shape, 0)
    return jnp.sum(jnp.where(rows == j, a, 0.0), axis=0, keepdims=True)


def _shift_rows(x, halo, d):
    xs = _roll(x, d, 0)
    hs = _roll(halo, d, 0)
    r8 = lax.broadcasted_iota(jnp.int32, hs.shape, 0)
    top = jnp.where(r8 < d, hs, xs[:SUBLANE])
    return jnp.concatenate([top, xs[SUBLANE:]], axis=0)


def _conv_silu(x, halo, w):
    y = _row(w, CONV_W - 1) * x
    for j in range(CONV_W - 1):
        y = y + _row(w, j) * _shift_rows(x, halo, CONV_W - 1 - j)
    return jax.nn.silu(y)


def _head_select(offset):
    r = lax.broadcasted_iota(jnp.int32, (LANE, WIDTH), 0)
    c = lax.broadcasted_iota(jnp.int32, (LANE, WIDTH), 1)
    return (r == offset + lax.shift_right_logical(c, 7)).astype(_BF)


def _split3(x):
    x1 = x.astype(_BF)
    r1 = x - x1.astype(F32)
    x2 = r1.astype(_BF)
    return x1, x2, (r1 - x2.astype(F32)).astype(_BF)


@jax.custom_vjp
def _spread(x, sel):
    return _d_nn(jnp.concatenate(_split3(x), axis=1), jnp.concatenate([sel, sel, sel], axis=0))


def _spread_fwd(x, sel):
    return _spread(x, sel), sel


def _spread_bwd(sel, g):
    g1, g2, g3 = _split3(g)
    return _d_nt(g1, sel) + _d_nt(g2, sel) + _d_nt(g3, sel), jnp.zeros_like(sel)


_spread.defvjp(_spread_fwd, _spread_bwd)


def f_gdn_pre(i, gq, gk, gv, gab, hq, hk, hv, cwq, cwk, cwv, alog, dtb):
    live = jnp.where(i == 0, 0.0, 1.0)
    q = _conv_silu(gq, hq * live, cwq)
    k = _conv_silu(gk, hk * live, cwk)
    v = _conv_silu(gv, hv * live, cwv)
    g = _spread(-jnp.exp(alog) * jax.nn.softplus(gab + dtb), _head_select(0))
    beta = _spread(jax.nn.sigmoid(gab), _head_select(HEADS))
    qs, ks, vs, gs, bs = [], [], [], [], []
    for h in range(HEADS):
        sl = slice(LANE * h, LANE * (h + 1))
        qh, kh = q[:, sl], k[:, sl]
        qs.append(qh * lax.rsqrt(jnp.sum(qh * qh, axis=-1, keepdims=True) + EPS) * (GDN_DIM ** -0.5))
        ks.append(kh * lax.rsqrt(jnp.sum(kh * kh, axis=-1, keepdims=True) + EPS))
        vs.append(v[:, sl])
        gs.append(g[:, sl])
        bs.append(beta[:, sl])
    return jnp.stack(qs), jnp.stack(ks), jnp.stack(vs), jnp.stack(gs), jnp.stack(bs)


def gdn_pre_bwd(i, gq, gk, gv, gab, hq, hk, hv, dq, dk, dv, dg, db, cwq, cwk, cwv, alog, dtb, cq, ck, cv):
    grads = _vjp_fn(f_gdn_pre, 12, 5)(i, gq, gk, gv, gab, hq, hk, hv, cwq, cwk, cwv, alog, dtb, dq, dk, dv, dg, db)
    dgq, dgk, dgv, dgab, dhq, dhk, dhv, dcwq, dcwk, dcwv, dalog, ddtb = grads

    def add_tail(dx, carry):
        return jnp.concatenate([dx[:-SUBLANE], dx[-SUBLANE:] + carry], axis=0)

    return (add_tail(dgq, cq), add_tail(dgk, ck), add_tail(dgv, cv), dgab,
            dcwq, dcwk, dcwv, dalog, ddtb, dhq, dhk, dhv)


def _flash_tile(t_len):
    return min(512, t_len)


FLASH_HEADS = 4
FLASH_BWD_HEADS = 2
LOG2E = 1.4426950408889634


def _causal(rows0, shape):
    r = rows0 + lax.broadcasted_iota(jnp.int32, shape, 0)
    c = lax.broadcasted_iota(jnp.int32, shape, 1)
    return c <= r


def flash_fwd(q, k, v):
    h_n, t_len, _ = q.shape
    tq = _flash_tile(t_len)
    nq = t_len // tq
    hb = FLASH_HEADS
    kw = 2 if nq % 2 == 0 else 1
    tk = kw * tq
    c2 = (QK ** -0.5) * LOG2E
    pairs = [(i, j) for i in range(nq) for j in range(i // kw + 1)]
    qt = jnp.array([p[0] for p in pairs], jnp.int32)
    kt = jnp.array([p[1] for p in pairs], jnp.int32)

    def body(qt_ref, kt_ref, q_ref, k_ref, v_ref, o_ref, lse_ref, m_s, acc_s):
        step = pl.program_id(1)
        qi, kj = qt_ref[step], kt_ref[step]
        last = qi // kw

        @pl.when(kj == 0)
        def _():
            m_s[...] = jnp.full_like(m_s, -jnp.inf)
            acc_s[...] = jnp.zeros_like(acc_s)

        def tile(diagonal, cols):
            s = _bdot(q_ref[...], k_ref[:, :cols], _NT3, False) * c2
            if diagonal:
                s = jnp.where(_causal((qi % kw) * tq, (tq, cols))[None], s, -jnp.inf)
            m_old = m_s[...]
            m_new = jnp.maximum(m_old, jnp.max(s, axis=-1, keepdims=True))
            p = jnp.exp2(s - m_new).astype(_BF)
            v_ones = jnp.concatenate([v_ref[:, :cols], jnp.ones((hb, cols, LANE), _BF)], axis=2)
            acc_s[...] = jnp.exp2(m_old - m_new) * acc_s[...] + _bdot(p, v_ones, _NN3, False)
            m_s[...] = m_new

        @pl.when(kj < last)
        def _():
            tile(False, tk)

        @pl.when((kj == last) & (qi % kw == 0))
        def _():
            tile(True, tq)

        if kw > 1:
            @pl.when((kj == last) & (qi % kw != 0))
            def _():
                tile(True, tk)

        @pl.when(kj == last)
        def _():
            acc = acc_s[...]
            l_sum = acc[:, :, LANE:]
            o = acc[:, :, :LANE] / l_sum
            for hh in range(hb):
                o_ref[:, LANE * hh:LANE * (hh + 1)] = o[hh]
            lse_ref[...] = m_s[...] + jnp.log2(jnp.max(l_sum, axis=-1, keepdims=True))

    return pl.pallas_call(
        body,
        name="flash_fwd",
        grid_spec=pltpu.PrefetchScalarGridSpec(
            num_scalar_prefetch=2,
            grid=(h_n // hb, qt.shape[0]),
            in_specs=[
                pl.BlockSpec((hb, tq, HEAD_PAD), lambda h, s, qt_ref, kt_ref: (h, qt_ref[s], 0)),
                pl.BlockSpec((hb, tk, HEAD_PAD), lambda h, s, qt_ref, kt_ref: (h, kt_ref[s], 0)),
                pl.BlockSpec((hb, tk, LANE), lambda h, s, qt_ref, kt_ref: (h, kt_ref[s], 0)),
            ],
            out_specs=[
                pl.BlockSpec((tq, hb * LANE), lambda h, s, qt_ref, kt_ref: (qt_ref[s], h)),
                pl.BlockSpec((hb, tq, 1), lambda h, s, qt_ref, kt_ref: (h, qt_ref[s], 0)),
            ],
            scratch_shapes=[pltpu.VMEM((hb, tq, 1), F32), pltpu.VMEM((hb, tq, 2 * LANE), F32)],
        ),
        out_shape=[jax.ShapeDtypeStruct((t_len, h_n * LANE), F32), jax.ShapeDtypeStruct((h_n, t_len, 1), F32)],
        compiler_params=_cparams("parallel", "arbitrary"),
    )(qt, kt, q, k, v)


def flash_bwd(q, k, v, do, lse, delta):
    h_n, t_len, _ = q.shape
    tq = _flash_tile(t_len)
    nq = t_len // tq
    hb = FLASH_BWD_HEADS
    kw = 2 if nq % 2 == 0 else 1
    tk = kw * tq
    pairs = [(i, j) for j in range(nq // kw) for i in range(kw * j, nq)]
    n_steps = len(pairs)
    qt = jnp.array([p[0] for p in pairs], jnp.int32)
    kt = jnp.array([p[1] for p in pairs], jnp.int32)
    scale = QK ** -0.5
    c2 = scale * LOG2E

    def body(qt_ref, kt_ref, q_ref, k_ref, v_ref, do_ref, lse_ref, dl_ref, dq_hbm, dk_ref, dv_ref, dq_s, dq_sem):
        group, step = pl.program_id(0), pl.program_id(1)
        qi, kj = qt_ref[step], kt_ref[step]

        @pl.when(step == 0)
        def _():
            dq_s[...] = jnp.zeros_like(dq_s)

        def tile(diagonal, cols):
            qb, kb = q_ref[...], k_ref[:, :cols]
            dob = jnp.stack([do_ref[:, LANE * hh:LANE * (hh + 1)] for hh in range(hb)])
            p = jnp.exp2(_bdot(kb, qb, _NT3, False) * c2 - lse_ref[...])
            if diagonal:
                key = lax.broadcasted_iota(jnp.int32, (1, cols, tq), 1)
                row = lax.broadcasted_iota(jnp.int32, (1, cols, tq), 2) + (qi % kw) * tq
                p = jnp.where(key <= row, p, 0.0)
            dv = _bdot(p, dob, _NN3, False)
            ds = p * (_bdot(v_ref[:, :cols], dob, _NT3, False) - dl_ref[...]) * scale
            dk = _bdot(ds, qb, _NN3, False)
            dq_s[:, pl.ds(pl.multiple_of(qi * tq, tq), tq), :] += _bdot(ds, kb, _TN3, False)
            return dk, dv

        @pl.when(qi == kw * kj)
        def _():
            dk, dv = tile(True, tq)
            dk_ref[:, :tq], dv_ref[:, :tq] = dk, dv
            if kw > 1:
                dk_ref[:, tq:] = jnp.zeros((hb, tk - tq, HEAD_PAD), F32)
                dv_ref[:, tq:] = jnp.zeros((hb, tk - tq, LANE), F32)

        if kw > 1:
            @pl.when((qi != kw * kj) & (qi // kw == kj))
            def _():
                dk, dv = tile(True, tk)
                dk_ref[...] += dk
                dv_ref[...] += dv

        @pl.when(qi // kw > kj)
        def _():
            dk, dv = tile(False, tk)
            dk_ref[...] += dk
            dv_ref[...] += dv

        @pl.when(step == n_steps - 1)
        def _():
            out = pltpu.make_async_copy(dq_s, dq_hbm.at[pl.ds(group * hb, hb)], dq_sem)
            out.start()
            out.wait()

    def qmap(h, s, qt_ref, kt_ref):
        return (h, qt_ref[s], 0)

    def kmap(h, s, qt_ref, kt_ref):
        return (h, kt_ref[s], 0)

    return pl.pallas_call(
        body,
        name="flash_bwd",
        grid_spec=pltpu.PrefetchScalarGridSpec(
            num_scalar_prefetch=2,
            grid=(h_n // hb, n_steps),
            in_specs=[
                pl.BlockSpec((hb, tq, HEAD_PAD), qmap),
                pl.BlockSpec((hb, tk, HEAD_PAD), kmap),
                pl.BlockSpec((hb, tk, LANE), kmap),
                pl.BlockSpec((tq, hb * LANE), lambda h, s, qt_ref, kt_ref: (qt_ref[s], h)),
                pl.BlockSpec((hb, 1, tq), lambda h, s, qt_ref, kt_ref: (h, 0, qt_ref[s])),
                pl.BlockSpec((hb, 1, tq), lambda h, s, qt_ref, kt_ref: (h, 0, qt_ref[s])),
            ],
            out_specs=[
                pl.BlockSpec(memory_space=pl.ANY),
                pl.BlockSpec((hb, tk, HEAD_PAD), kmap),
                pl.BlockSpec((hb, tk, LANE), kmap),
            ],
            scratch_shapes=[pltpu.VMEM((hb, t_len, HEAD_PAD), F32), pltpu.SemaphoreType.DMA],
        ),
        out_shape=[
            jax.ShapeDtypeStruct((h_n, t_len, HEAD_PAD), F32),
            jax.ShapeDtypeStruct((h_n, t_len, HEAD_PAD), F32),
            jax.ShapeDtypeStruct((h_n, t_len, LANE), F32),
        ],
        compiler_params=_cparams("parallel", "arbitrary"),
    )(qt, kt, q, k, v, do, lse, delta)


def _tri_ones(h_n):
    ii = lax.broadcasted_iota(jnp.int32, (h_n, CHUNK, CHUNK), 1)
    jj = lax.broadcasted_iota(jnp.int32, (h_n, CHUNK, CHUNK), 2)
    return (ii >= jj).astype(_BF)


@jax.custom_vjp
def _chunk_cumsum(gb):
    tri = _tri_ones(gb.shape[0])
    return _bdot(jnp.concatenate([tri, tri, tri], axis=2), jnp.concatenate(_split3(gb), axis=1), _NN3, False)


def _chunk_cumsum_bwd(_, ct):
    tri = _tri_ones(ct.shape[0])
    return (_bdot(jnp.concatenate([tri, tri, tri], axis=1), jnp.concatenate(_split3(ct), axis=1), _TN3, False),)


_chunk_cumsum.defvjp(lambda gb: (_chunk_cumsum(gb), None), _chunk_cumsum_bwd)


@jax.custom_vjp
def _pair_diff(gcb):
    g1, g2, g3 = _split3(gcb)
    lane = lax.broadcasted_iota(jnp.int32, (1, 1, LANE), 2)
    one, zero = jnp.ones((), _BF), jnp.zeros((), _BF)
    a = jnp.where(lane == 0, g1, jnp.where(lane == 1, g2, jnp.where(lane == 2, g3, jnp.where(lane < 6, one, zero))))
    b = jnp.where(lane < 3, one, jnp.where(lane == 3, -g1, jnp.where(lane == 4, -g2, jnp.where(lane == 5, -g3, zero))))
    return _bdot(a, b, _NT3, False)


def _pair_diff_bwd(_, ct):
    parts = _split3(ct)
    ones = jnp.ones((ct.shape[0], 3 * CHUNK, LANE), _BF)
    rows = _bdot(jnp.concatenate(parts, axis=2), ones, _NN3, False)
    cols = _bdot(jnp.concatenate(parts, axis=1), ones, _TN3, False)
    lane = lax.broadcasted_iota(jnp.int32, (1, 1, LANE), 2)
    return (jnp.where(lane == 0, rows - cols, 0.0),)


_pair_diff.defvjp(lambda gcb: (_pair_diff(gcb), None), _pair_diff_bwd)


@jax.custom_vjp
def _saved_inverse(lmat, inv):
    return inv


def _saved_inverse_bwd(inv, g):
    return -_pdot(_pdot(inv, g, "tn"), inv, "nt"), jnp.zeros_like(inv)


_saved_inverse.defvjp(lambda lmat, inv: (inv, inv), _saved_inverse_bwd)


def gdn_step(s, q, k, v, gb, bb, inv_saved=None):
    c = CHUNK
    ii = lax.broadcasted_iota(jnp.int32, (1, c, c), 1)
    jj = lax.broadcasted_iota(jnp.int32, (1, c, c), 2)
    incl, strict = ii >= jj, ii > jj
    gcb = _chunk_cumsum(gb)
    diff = _pair_diff(gcb)
    decay = jnp.where(incl, jnp.exp(jnp.where(incl, diff, 0.0)), 0.0)
    kb, vb = k * bb, v * bb
    egc = jnp.exp(gcb)
    lmat = jnp.where(strict, _bmm_nt(kb, k) * decay, 0.0)
    if inv_saved is None:
        mm3 = _bh3_passes
        inv = (ii == jj).astype(F32) - lmat
        pw = mm3(lmat, lmat)
        for step in range(5):
            inv = inv + mm3(inv, pw)
            if step < 4:
                pw = mm3(pw, pw)
    else:
        mm3 = _bh3
        inv = _saved_inverse(lmat, inv_saved)
    u = mm3(inv, vb)
    w = mm3(inv, kb * egc)
    attn = _bmm_nt(q, k) * decay
    qd = q * egc
    g_end = jnp.sum(gb, axis=1, keepdims=True)
    kd = k * jnp.exp(g_end - gcb)
    v_new = u - _bmm(w, s)
    o = _bmm(qd, s) + _bmm(attn, v_new)
    s_new = s * jnp.exp(g_end) + _bmm_tn(kd, v_new)
    return s_new, o, inv


def gdn_fwd(q, k, v, gb, bb):
    h_n, t_len, d = q.shape
    n = t_len // CHUNK
    blk = pl.BlockSpec((h_n, CHUNK, d), lambda i: (0, i, 0))

    def body(q_ref, k_ref, v_ref, g_ref, b_ref, o_ref, sall_ref, inv_ref, s_s):
        @pl.when(pl.program_id(0) == 0)
        def _():
            s_s[...] = jnp.zeros_like(s_s)

        s = s_s[...]
        sall_ref[0] = s
        s_s[...], o_ref[...], inv_ref[0] = gdn_step(s, q_ref[...], k_ref[...], v_ref[...], g_ref[...], b_ref[...])

    return pl.pallas_call(
        body,
        name="gdn_fwd",
        grid=(n,),
        in_specs=[blk] * 5,
        out_specs=[blk, pl.BlockSpec((1, h_n, d, d), lambda i: (i, 0, 0, 0)),
                   pl.BlockSpec((1, h_n, CHUNK, CHUNK), lambda i: (i, 0, 0, 0))],
        out_shape=[jax.ShapeDtypeStruct((h_n, t_len, d), F32), jax.ShapeDtypeStruct((n, h_n, d, d), F32),
                   jax.ShapeDtypeStruct((n, h_n, CHUNK, CHUNK), F32)],
        scratch_shapes=[pltpu.VMEM((h_n, d, d), F32)],
        compiler_params=_cparams("arbitrary"),
    )(q, k, v, gb, bb)


def gdn_bwd(q, k, v, gb, bb, s_all, inv_all, do):
    h_n, t_len, d = q.shape
    n = t_len // CHUNK
    blk = pl.BlockSpec((h_n, CHUNK, d), lambda i: (0, n - 1 - i, 0))

    def body(q_ref, k_ref, v_ref, g_ref, b_ref, sall_ref, inv_ref, do_ref, dq_ref, dk_ref, dv_ref, dg_ref, db_ref,
             ds_s):
        @pl.when(pl.program_id(0) == 0)
        def _():
            ds_s[...] = jnp.zeros_like(ds_s)

        inv = inv_ref[0]
        _, pull = jax.vjp(lambda *a: gdn_step(*a, inv_saved=inv)[:2], sall_ref[0], q_ref[...], k_ref[...], v_ref[...],
                          g_ref[...], b_ref[...])
        ds_s[...], dq_ref[...], dk_ref[...], dv_ref[...], dg_ref[...], db_ref[...] = pull((ds_s[...], do_ref[...]))

    return pl.pallas_call(
        body,
        name="gdn_bwd",
        grid=(n,),
        in_specs=[blk] * 5 + [pl.BlockSpec((1, h_n, d, d), lambda i: (n - 1 - i, 0, 0, 0)),
                              pl.BlockSpec((1, h_n, CHUNK, CHUNK), lambda i: (n - 1 - i, 0, 0, 0)), blk],
        out_specs=[blk] * 5,
        out_shape=[jax.ShapeDtypeStruct((h_n, t_len, d), F32)] * 5,
        scratch_shapes=[pltpu.VMEM((h_n, d, d), F32)],
        compiler_params=_cparams("arbitrary"),
    )(q, k, v, gb, bb, s_all, inv_all, do)


def _pad_cols(a, n):
    return jnp.pad(a, ((0, 0), (0, n - a.shape[1])))


def arrange_w_in(w):
    pieces, start = [], 0
    for n in R_SPLITS:
        pieces.append(w[:, start:start + n])
        start += n
    cq, ckv, kr, mgate, gq, gk, gv, ga, gb, ggate = pieces
    return jnp.concatenate([mgate, gq, gk, gv, ggate, cq, ckv, _pad_cols(kr, LANE),
                            _pad_cols(jnp.concatenate([ga, gb], axis=1), LANE)], axis=1)


def unarrange_w_in(pieces):
    mgate, gq, gk, gv, ggate, rest = pieces

    def cols(start, n):
        return rest[:, start - P_CQ:start - P_CQ + n]
    return jnp.concatenate([cols(P_CQ, Q_LORA), cols(P_CKV, KV_LORA), cols(P_KR, ROPE), mgate, gq, gk, gv,
                            cols(P_GAB, HEADS), cols(P_GAB + HEADS, HEADS), ggate], axis=1)


def arrange_w_uq(w):
    w = w.reshape(w.shape[0], HEADS, QK)
    return jnp.pad(w, ((0, 0), (0, 0), (0, HEAD_PAD - QK))).reshape(w.shape[0], HEADS * HEAD_PAD)


def unarrange_w_uq(g):
    return g.reshape(g.shape[0], HEADS, HEAD_PAD)[:, :, :QK].reshape(g.shape[0], HEADS * QK)


def local_step(x, pos, tgt, p, on_early_grads=None, on_d_proj=None, on_weight_grads=None, on_d_xn=None,
               first_after=None, late_weights=None):
    t_len = x.shape[0]
    w_in = p["w_in"]
    norm_gain = p["norm_gain"].reshape(1, D_MODEL)
    qa_gain = p["mla_q_a_gain"].reshape(1, Q_LORA)
    kva_gain = p["mla_kv_a_gain"].reshape(1, KV_LORA)
    qg = _pad_cols(p["mla_q_norm_gain"].reshape(1, QK), HEAD_PAD)
    kg = _pad_cols(p["mla_k_norm_gain"].reshape(1, QK), HEAD_PAD)
    alog = _pad_cols(p["gdn_a_log"].reshape(1, HEADS), LANE)
    dtb = _pad_cols(p["gdn_dt_bias"].reshape(1, HEADS), LANE)
    og = p["gdn_out_norm_gain"].reshape(1, GDN_DIM)
    half = ROPE // 2
    inv_freq = jnp.power(ROPE_THETA, -jnp.arange(half, dtype=F32) / half)
    invf = _pad_cols(jnp.concatenate([inv_freq, inv_freq]).reshape(1, ROPE), LANE)

    rt = 256
    rt_light = 512
    r = "r"
    first_after = jnp.zeros((SUBLANE, LANE), F32) if first_after is None else first_after
    (xn,) = rowwise("rms_x", lambda i, x_, gain_, after_: f_rms_x(i, x_, gain_), t_len, rt_light, [(x, (r, D_MODEL, 0))],
                    [norm_gain, first_after], [(r, D_MODEL, _BF)])
    proj = matmul("proj", xn, w_in, "nn")
    if late_weights is not None:
        p = {**p, **late_weights(proj)}
    w_uq, w_ukv, w_out = p["w_uq"], p["w_ukv"], p["w_out"]
    cw = p["gdn_conv_w"].reshape(CONV_W, 3 * WIDTH)
    cwq, cwk, cwv = cw[:, :WIDTH], cw[:, WIDTH:2 * WIDTH], cw[:, 2 * WIDTH:]
    cq_in = (proj, (r, Q_LORA, P_CQ // Q_LORA))
    ckv_in = (proj, (r, KV_LORA, P_CKV // KV_LORA))
    kr_in = (proj, (r, LANE, P_KR // LANE))
    mgate_in = (proj, (r, WIDTH, P_MGATE // WIDTH))
    ggate_in = (proj, (r, WIDTH, P_GGATE // WIDTH))
    gqkv_in = [(proj, (r, WIDTH, P_GQ // WIDTH)), (proj, (r, WIDTH, P_GK // WIDTH)), (proj, (r, WIDTH, P_GV // WIDTH))]
    gab_in = (proj, (r, LANE, P_GAB // LANE))
    halos = [(proj, ("halo", WIDTH, P_GQ // WIDTH)), (proj, ("halo", WIDTH, P_GK // WIDTH)),
             (proj, ("halo", WIDTH, P_GV // WIDTH))]

    q_lat, kv_lat = rowwise("lat", f_lat, t_len, rt_light, [cq_in, ckv_in], [qa_gain, kva_gain],
                            [(r, Q_LORA, _BF), (r, KV_LORA, _BF)])
    q_raw = matmul("q_up", q_lat, w_uq, "nn")
    kv_raw = matmul("kv_up", kv_lat, w_ukv, "nn")
    wide = HEADS * HEAD_PAD
    head_in = [(q_raw, (r, wide, 0)), (kv_raw, (r, wide, 0)), kr_in]
    pos_in = (pos, (r, 1, 0))
    q_full, k_full, v_mla = rowwise(
        "head", lambda i, qr, kvr, kr, ps, qg_, kg_, iv: f_head(i, qr, kvr, kr, qg_, kg_, ps, iv), t_len, rt,
        head_in + [pos_in], [qg, kg, invf],
        [("h", HEADS, HEAD_PAD, _BF), ("h", HEADS, HEAD_PAD, _BF), ("h", HEADS, LANE, _BF)])
    o_mla, lse = flash_fwd(q_full, k_full, v_mla)

    pre_in = gqkv_in + [gab_in] + halos
    pre_full = [cwq, cwk, cwv, alog, dtb]
    hkind = ("h", HEADS, GDN_DIM, F32)
    gq_n, gk_n, gv_n, g_b, b_b = rowwise("gdn_pre", f_gdn_pre, t_len, rt, pre_in, pre_full, [hkind] * 5)
    o_gdn, s_all, inv_all = gdn_fwd(gq_n, gk_n, gv_n, g_b, b_b)

    mix_in = [(o_mla, (r, WIDTH, 0)), mgate_in, (o_gdn, ("h",)), ggate_in]
    (mixed,) = rowwise("mix", f_mix, t_len, rt_light, mix_in, [og], [(r, 2 * WIDTH, _BF)])
    dy, dy_mx, loss_acc = out_proj_loss(mixed, w_out, x, tgt)
    loss = loss_acc[0, 0]

    d_mixed = matmul("d_mixed", dy_mx, w_out, "nt")
    g_w_out = matmul("g_w_out", mixed, dy_mx, "tn")

    def mix_bwd(i, o_mla_, mgate_, o_gdn_, ggate_, d_mixed_, og_):
        do_mla_, d_mgate_, do_gdn_, d_ggate_, g_og_ = _vjp_fn(f_mix, 5, 1)(i, o_mla_, mgate_, o_gdn_, ggate_, og_, d_mixed_)
        delta_ = jnp.stack([jnp.sum(o_mla_[:, LANE * h:LANE * (h + 1)] * do_mla_[:, LANE * h:LANE * (h + 1)],
                                    axis=-1, keepdims=True) for h in range(HEADS)])
        return do_mla_, d_mgate_, do_gdn_, d_ggate_, delta_, g_og_

    do_mla, d_mgate, do_gdn, d_ggate, delta, g_og = rowwise(
        "mix_bwd", mix_bwd, t_len, rt, mix_in + [(d_mixed, (r, 2 * WIDTH, 0))], [og],
        [(r, WIDTH, F32), (r, WIDTH, _BF), hkind, (r, WIDTH, _BF), ("h", HEADS, 1, F32)], [(1, GDN_DIM)])
    dq_n, dk_n, dv_n, dg_b, db_b = gdn_bwd(gq_n, gk_n, gv_n, g_b, b_b, s_all, inv_all, do_gdn)
    cts_in = [(a, ("h",)) for a in (dq_n, dk_n, dv_n, dg_b, db_b)]
    d_gq, d_gk, d_gv, d_gab, g_cwq, g_cwk, g_cwv, g_alog, g_dtb = rowwise(
        "gdn_pre_bwd", gdn_pre_bwd, t_len, rt, pre_in + cts_in, pre_full,
        [(r, WIDTH, _BF)] * 3 + [(r, LANE, _BF)],
        [(CONV_W, WIDTH)] * 3 + [(1, LANE)] * 2, carries=[(SUBLANE, WIDTH)] * 3, reverse=True)

    dq_full, dk_full, dv_mla = flash_bwd(q_full, k_full, v_mla, do_mla, lse.reshape(HEADS, 1, t_len),
                                         delta.reshape(HEADS, 1, t_len))
    head_cts = [(a, ("h",)) for a in (dq_full, dk_full, dv_mla)]

    def head_bwd(i, q_raw_, kv_raw_, kr_, pos_, dq_, dk_, dv_, qg_, kg_, invf_):
        return _vjp_fn(f_head, 5, 3)(i, q_raw_, kv_raw_, kr_, qg_, kg_, pos_, invf_, dq_, dk_, dv_)

    dq_raw, dkv_raw, d_kr, g_qg, g_kg = rowwise(
        "head_bwd", head_bwd, t_len, rt, head_in + [pos_in] + head_cts, [qg, kg, invf],
        [(r, wide, _BF), (r, wide, _BF), (r, LANE, _BF)], [(1, HEAD_PAD), (1, HEAD_PAD)])
    dq_lat = matmul("dq_lat", dq_raw, w_uq, "nt")
    g_w_uq = matmul("g_w_uq", q_lat, dq_raw, "tn")
    dkv_lat = matmul("dkv_lat", dkv_raw, w_ukv, "nt")
    g_w_ukv = matmul("g_w_ukv", kv_lat, dkv_raw, "tn")
    grads = {
        "w_uq": g_w_uq, "w_ukv": g_w_ukv, "gdn_conv_w": jnp.concatenate([g_cwq, g_cwk, g_cwv], axis=1),
        "w_out": g_w_out, "mla_q_norm_gain": g_qg[:, :QK], "mla_k_norm_gain": g_kg[:, :QK],
        "gdn_a_log": g_alog[:, :HEADS], "gdn_dt_bias": g_dtb[:, :HEADS], "gdn_out_norm_gain": g_og,
    }
    after = jnp.zeros((SUBLANE, LANE), F32) if on_early_grads is None else on_early_grads(grads)

    def lat_bwd(i, cq_, ckv_, dql_, dkl_, gq_, gkv_, after_):
        return _vjp_fn(f_lat, 4, 2)(i, cq_, ckv_, gq_, gkv_, dql_, dkl_)

    d_cq, d_ckv, grads["mla_q_a_gain"], grads["mla_kv_a_gain"] = rowwise(
        "lat_bwd", lat_bwd, t_len, rt_light, [cq_in, ckv_in, (dq_lat, (r, Q_LORA, 0)), (dkv_lat, (r, KV_LORA, 0))],
        [qa_gain, kva_gain, after], [(r, Q_LORA, _BF), (r, KV_LORA, _BF)], [(1, Q_LORA), (1, KV_LORA)])

    d_proj = [d_mgate, d_gq, d_gk, d_gv, d_ggate, jnp.concatenate([d_cq, d_ckv, d_kr, d_gab], axis=1)]
    after = None if on_d_proj is None else on_d_proj(d_proj[-1])
    grads["w_in"] = [matmul("g_w_in_%d" % j, xn, piece, "tn", after=after if j == 0 else None)
                     for j, piece in enumerate(d_proj)]
    after = None if on_weight_grads is None else on_weight_grads(grads)
    d_xn = matmul_pieces("d_xn", d_proj, w_in, after=after)
    after = jnp.zeros((SUBLANE, LANE), F32) if on_d_xn is None else on_d_xn(d_xn)

    def rms_x_bwd(i, x_, dxn_, dy_, gain_, after_):
        dx, dgain = _vjp_fn(f_rms_x, 2, 1)(i, x_, gain_, dxn_)
        return dx + dy_, dgain

    grad_x, grads["norm_gain"] = rowwise(
        "rms_x_bwd", rms_x_bwd, t_len, rt, [(x, (r, D_MODEL, 0)), (d_xn, (r, D_MODEL, 0)), (dy, (r, D_MODEL, 0))],
        [norm_gain, after], [(r, D_MODEL, F32)], [(1, D_MODEL)])
    return loss, grad_x, grads


MESH = pl.DeviceIdType.MESH
ANY = pl.BlockSpec(memory_space=pl.ANY)
CHIP_FLIPS = ((1, 0), (0, 1), (1, 1))


def _place():
    return lax.axis_index("x"), lax.axis_index("y"), lax.axis_index("c")


def _flip(v, f):
    return 1 - v if f else v


def all_gather(shards):
    n_arr = len(shards)

    def body(*refs):
        x_refs, o_refs = refs[:n_arr], refs[n_arr:2 * n_arr]
        send_sems, recv_sems, local_sems = refs[2 * n_arr:]
        x, y, c = _place()
        me, sibling = (x, y, c), (x, y, 1 - c)
        chips = [(_flip(x, fx), _flip(y, fy)) for fx, fy in CHIP_FLIPS]

        def copy(a, k, block, to, src=None):
            px, py, pc = block
            dst = o_refs[a].at[4 * px + 2 * py + pc]
            return pltpu.make_async_remote_copy(
                src_ref=dst if src is None else src, dst_ref=dst, send_sem=send_sems.at[a, k],
                recv_sem=recv_sems.at[a, k], device_id=to, device_id_type=MESH)

        mine, first, passed = [], [], []
        for a in range(n_arr):
            cp = pltpu.make_async_copy(x_refs[a], o_refs[a].at[4 * x + 2 * y + c], local_sems.at[a])
            cp.start()
            mine.append(cp)
            first.append(copy(a, 0, me, sibling, src=x_refs[a]))
            first += [copy(a, 1 + j, me, (*chip, c), src=x_refs[a]) for j, chip in enumerate(chips)]
        for cp in first:
            cp.start()
        for j, chip in enumerate(chips):
            for a in range(n_arr):
                copy(a, 1 + j, (*chip, c), me).wait_recv()
                cp = copy(a, 4 + j, (*chip, c), sibling)
                cp.start()
                passed.append(cp)
        for a in range(n_arr):
            copy(a, 0, sibling, me).wait_recv()
            for j, chip in enumerate(chips):
                copy(a, 4 + j, (*chip, 1 - c), me).wait_recv()
        for cp in first + passed:
            cp.wait_send()
        for cp in mine:
            cp.wait()

    return pl.pallas_call(
        body,
        name="all_gather",
        out_shape=[jax.ShapeDtypeStruct((N_DEV,) + s.shape, s.dtype) for s in shards],
        in_specs=[ANY] * n_arr,
        out_specs=[ANY] * n_arr,
        scratch_shapes=[pltpu.SemaphoreType.DMA((n_arr, 7)), pltpu.SemaphoreType.DMA((n_arr, 7)),
                        pltpu.SemaphoreType.DMA((n_arr,))],
    )(*shards)


HBM = pl.BlockSpec(memory_space=pltpu.HBM)
SEMS = pl.BlockSpec(memory_space=pltpu.SEMAPHORE)
SIDE_EFFECT = pltpu.SideEffectType.DATAFLOW_SIDE_EFFECTING


def core_routes(x, y, c):
    return [(2 * q + (1 - c), q, (x, y, 1 - c)) for q in range(4)]


def chip_routes(x, y, c):
    routes = []
    for j, (fx, fy) in enumerate(CHIP_FLIPS):
        px, py = _flip(x, fx), _flip(y, fy)
        routes.append((2 * px + py, j, (px, py, c)))
    return routes


def _route_copies(routes, n_routes, src_refs, land_refs, sems):
    x, y, c = _place()
    n_copies = len(src_refs) * n_routes
    return [pltpu.make_async_remote_copy(src_ref=src.at[s], dst_ref=land.at[d], send_sem=sems[a * n_routes + k],
                                         recv_sem=sems[n_copies + a * n_routes + k], device_id=dev,
                                         device_id_type=MESH)
            for a, (src, land) in enumerate(zip(src_refs, land_refs)) for k, (s, d, dev) in enumerate(routes(x, y, c))]


def gather_routes(x, y, c):
    me = 4 * x + 2 * y + c
    return [(0, me, (_flip(x, (k >> 2) & 1), _flip(y, (k >> 1) & 1), _flip(c, k & 1))) for k in range(1, N_DEV)]


def exchange_start(name, routes, n_routes, srcs, n_slots=None):
    n = len(srcs)
    n_sems = 2 * n * n_routes
    lands = [lax.empty((n_routes if n_slots is None else n_slots,) + s.shape[1:], s.dtype) for s in srcs]

    def body(*refs):
        for cp in _route_copies(routes, n_routes, refs[:n], refs[n:2 * n], refs[2 * n:2 * n + n_sems]):
            cp.start()
        refs[-1][...] = jnp.zeros_like(refs[-1])

    res = pl.pallas_call(
        body,
        name=name,
        out_shape=(*[pltpu.SemaphoreType.DMA(())] * n_sems, *[pltpu.HBM(a.shape, a.dtype) for a in srcs + lands],
                   jax.ShapeDtypeStruct((SUBLANE, LANE), F32)),
        in_specs=[HBM] * (2 * n),
        out_specs=(*[SEMS] * n_sems, *[HBM] * (2 * n), pl.BlockSpec(memory_space=pltpu.VMEM)),
        input_output_aliases={i: n_sems + i for i in range(2 * n)},
        compiler_params=pltpu.CompilerParams(has_side_effects=SIDE_EFFECT),
    )(*[pltpu.with_memory_space_constraint(a, pltpu.HBM) for a in srcs + lands])
    return (res[:n_sems], res[n_sems:-1]), res[-1]


def exchange_wait(name, routes, handle, after):
    sems, thru = handle
    n, n_sems = len(thru) // 2, len(sems)
    n_routes = n_sems // (2 * n)

    def body(*refs):
        for cp in _route_copies(routes, n_routes, refs[:n], refs[n:2 * n], refs[2 * n:2 * n + n_sems]):
            cp.wait_send()
            cp.wait_recv()

    res = pl.pallas_call(
        body,
        name=name,
        out_shape=tuple(pltpu.HBM(a.shape, a.dtype) for a in thru),
        in_specs=[HBM] * (2 * n) + [SEMS] * n_sems + [ANY],
        out_specs=tuple([HBM] * (2 * n)),
        input_output_aliases={i: i for i in range(2 * n)},
        compiler_params=pltpu.CompilerParams(has_side_effects=SIDE_EFFECT),
    )(*thru, *sems, after)
    return list(res[:n]), list(res[n:])


def gather_small(v):
    def body(v_ref, o_ref, send_sems, recv_sems, local_sem):
        x, y, c = _place()
        me = 4 * x + 2 * y + c
        mine = pltpu.make_async_copy(v_ref, o_ref.at[me], local_sem)
        mine.start()
        copies = []
        for k in range(1, N_DEV):
            fx, fy, fc = (k >> 2) & 1, (k >> 1) & 1, k & 1
            cp = pltpu.make_async_remote_copy(
                src_ref=v_ref, dst_ref=o_ref.at[me], send_sem=send_sems.at[k - 1], recv_sem=recv_sems.at[k - 1],
                device_id=(_flip(x, fx), _flip(y, fy), _flip(c, fc)), device_id_type=MESH)
            cp.start()
            copies.append(cp)
        for cp in copies:
            cp.wait()
        mine.wait()

    return pl.pallas_call(
        body,
        name="gather_small",
        out_shape=jax.ShapeDtypeStruct((N_DEV,) + v.shape, v.dtype),
        in_specs=[ANY],
        out_specs=ANY,
        scratch_shapes=[pltpu.SemaphoreType.DMA((N_DEV - 1,)), pltpu.SemaphoreType.DMA((N_DEV - 1,)),
                        pltpu.SemaphoreType.DMA],
    )(v)


def _row_tile(rows):
    for t in (256, 128, 64, 32, 16, 8):
        if rows % t == 0:
            return t
    return rows


def add_core_parts(name, g, recv, c_idx, wire):
    _, rows, cols = g.shape
    tr = _row_tile(rows)

    def body(c_ref, g_ref, r_ref, o_ref, w_ref):
        part = g_ref[...] + r_ref[...]
        o_ref[...] = part
        w_ref[...] = part.astype(w_ref.dtype)

    blk = pl.BlockSpec((1, tr, cols), lambda q, i, c_ref: (q, i, 0))
    return pl.pallas_call(
        body,
        name=name,
        grid_spec=pltpu.PrefetchScalarGridSpec(
            num_scalar_prefetch=1,
            grid=(4, rows // tr),
            in_specs=[pl.BlockSpec((1, tr, cols), lambda q, i, c_ref: (2 * q + c_ref[0], i, 0)), blk],
            out_specs=[blk, blk],
        ),
        out_shape=[jax.ShapeDtypeStruct((4, rows, cols), F32), jax.ShapeDtypeStruct((4, rows, cols), wire)],
        compiler_params=_cparams("parallel", "parallel"),
    )(c_idx, g, recv)


def _adamw(w, g, m, v):
    m = ADAM_B1 * m + (1.0 - ADAM_B1) * g
    v = ADAM_B2 * v + (1.0 - ADAM_B2) * (g * g)
    m_hat = m / (1.0 - ADAM_B1 ** ADAM_STEP)
    v_hat = v / (1.0 - ADAM_B2 ** ADAM_STEP)
    delta = -ADAM_LR * (m_hat / (jnp.sqrt(v_hat) + ADAM_EPS) + ADAM_WD * w)
    return delta, m, v


def adamw_sharded(name, parts, recv, q_idx, w, m, v):
    rows, cols = w.shape
    tr = _row_tile(rows)

    def body(q_ref, p_ref, r_ref, w_ref, m_ref, v_ref, g_out, d_out, m_out, v_out):
        g = p_ref[0] + r_ref[0].astype(F32) + r_ref[1].astype(F32) + r_ref[2].astype(F32)
        d, m_new, v_new = _adamw(w_ref[...], g, m_ref[...], v_ref[...])
        g_out[...], d_out[...], m_out[...], v_out[...] = g, d, m_new, v_new

    blk = pl.BlockSpec((tr, cols), lambda i, q_ref: (i, 0))
    return pl.pallas_call(
        body,
        name=name,
        grid_spec=pltpu.PrefetchScalarGridSpec(
            num_scalar_prefetch=1,
            grid=(rows // tr,),
            in_specs=[pl.BlockSpec((1, tr, cols), lambda i, q_ref: (q_ref[0], i, 0)),
                      pl.BlockSpec((3, tr, cols), lambda i, q_ref: (0, i, 0)), blk, blk, blk],
            out_specs=[blk] * 4,
        ),
        out_shape=[jax.ShapeDtypeStruct((rows, cols), F32)] * 4,
        compiler_params=_cparams("parallel"),
    )(q_idx, parts, recv, w, m, v)


def adamw_small(gathered, w, m, v):
    def body(g_ref, w_ref, m_ref, v_ref, g_out, d_out, m_out, v_out):
        g = g_ref[0]
        for j in range(1, N_DEV):
            g = g + g_ref[j]
        d, m_new, v_new = _adamw(w_ref[...], g, m_ref[...], v_ref[...])
        g_out[...], d_out[...], m_out[...], v_out[...] = g, d, m_new, v_new

    return pl.pallas_call(body, name="adamw_small", out_shape=[jax.ShapeDtypeStruct(w.shape, F32)] * 4)(gathered, w, m, v)


SHARDED = ("w_in", "w_uq", "w_ukv", "gdn_conv_w", "w_out")
SMALL = (("norm_gain", D_MODEL), ("mla_q_a_gain", Q_LORA), ("mla_kv_a_gain", KV_LORA), ("mla_q_norm_gain", QK),
         ("mla_k_norm_gain", QK), ("gdn_a_log", HEADS), ("gdn_dt_bias", HEADS), ("gdn_out_norm_gain", GDN_DIM))
WEIGHT_ORDER = ("norm_gain", "w_in", "mla_q_a_gain", "mla_kv_a_gain", "w_uq", "w_ukv", "mla_q_norm_gain",
                "mla_k_norm_gain", "gdn_conv_w", "gdn_a_log", "gdn_dt_bias", "gdn_out_norm_gain", "w_out")


def _pack_small(d):
    rows = []
    for name, n in SMALL:
        a = d[name].reshape(-1).astype(F32)
        n_pad = -(-n // LANE) * LANE
        rows.append(jnp.pad(a, (0, n_pad - n)).reshape(n_pad // LANE, LANE))
    packed = jnp.concatenate(rows, axis=0)
    return jnp.pad(packed, ((0, -packed.shape[0] % SUBLANE), (0, 0)))


def _unpack_small(packed):
    out, row = {}, 0
    for name, n in SMALL:
        n_rows = -(-n // LANE)
        out[name] = packed[row:row + n_rows].reshape(-1)[:n].reshape(1, n)
        row += n_rows
    return out


def kernel(x, positions, norm_gain, w_in, mla_q_a_gain, mla_kv_a_gain, w_uq, w_ukv, mla_q_norm_gain, mla_k_norm_gain, gdn_conv_w, gdn_a_log, gdn_dt_bias, gdn_out_norm_gain, w_out, loss_target, m_norm_gain, m_w_in, m_mla_q_a_gain, m_mla_kv_a_gain, m_w_uq, m_w_ukv, m_mla_q_norm_gain, m_mla_k_norm_gain, m_gdn_conv_w, m_gdn_a_log, m_gdn_dt_bias, m_gdn_out_norm_gain, m_w_out, v_norm_gain, v_w_in, v_mla_q_a_gain, v_mla_kv_a_gain, v_w_uq, v_w_ukv, v_mla_q_norm_gain, v_mla_k_norm_gain, v_gdn_conv_w, v_gdn_a_log, v_gdn_dt_bias, v_gdn_out_norm_gain, v_w_out):
    w = dict(norm_gain=norm_gain, w_in=w_in, mla_q_a_gain=mla_q_a_gain, mla_kv_a_gain=mla_kv_a_gain, w_uq=w_uq,
             w_ukv=w_ukv, mla_q_norm_gain=mla_q_norm_gain, mla_k_norm_gain=mla_k_norm_gain, gdn_conv_w=gdn_conv_w,
             gdn_a_log=gdn_a_log, gdn_dt_bias=gdn_dt_bias, gdn_out_norm_gain=gdn_out_norm_gain, w_out=w_out)
    m = dict(norm_gain=m_norm_gain, w_in=m_w_in, mla_q_a_gain=m_mla_q_a_gain, mla_kv_a_gain=m_mla_kv_a_gain,
             w_uq=m_w_uq, w_ukv=m_w_ukv, mla_q_norm_gain=m_mla_q_norm_gain, mla_k_norm_gain=m_mla_k_norm_gain,
             gdn_conv_w=m_gdn_conv_w, gdn_a_log=m_gdn_a_log, gdn_dt_bias=m_gdn_dt_bias,
             gdn_out_norm_gain=m_gdn_out_norm_gain, w_out=m_w_out)
    v = dict(norm_gain=v_norm_gain, w_in=v_w_in, mla_q_a_gain=v_mla_q_a_gain, mla_kv_a_gain=v_mla_kv_a_gain,
             w_uq=v_w_uq, w_ukv=v_w_ukv, mla_q_norm_gain=v_mla_q_norm_gain, mla_k_norm_gain=v_mla_k_norm_gain,
             gdn_conv_w=v_gdn_conv_w, gdn_a_log=v_gdn_a_log, gdn_dt_bias=v_gdn_dt_bias,
             gdn_out_norm_gain=v_gdn_out_norm_gain, w_out=v_w_out)
    t_len = x.shape[1]

    shards = [w[n][0] if n == "gdn_conv_w" else w[n][0].astype(_BF) for n in SHARDED]
    xi, yi, ci = _place()
    c_idx = jnp.reshape(ci, (1,)).astype(jnp.int32)
    q_idx = jnp.reshape(2 * xi + yi, (1,)).astype(jnp.int32)
    flight = {}

    def cols_whole(g):
        return g.transpose(1, 0, 2).reshape(g.shape[1], N_DEV * g.shape[2])

    def col_blocks(g):
        return g.reshape(g.shape[0], N_DEV, g.shape[1] // N_DEV).transpose(1, 0, 2)

    (a_w_in,) = all_gather(shards[:1])
    p = {n: w[n] for n, _ in SMALL}
    p["w_in"] = arrange_w_in(cols_whole(a_w_in))
    flight["weights"], weights_token = exchange_start(
        "gather_start", gather_routes, N_DEV - 1, [s[None] for s in shards[1:]], n_slots=N_DEV)

    def late_weights(proj):
        _, landed = exchange_wait("gather_wait", gather_routes, flight["weights"], proj)
        me = 4 * xi + 2 * yi + ci
        a_w_uq, a_w_ukv, a_cw, a_w_out = [lax.dynamic_update_slice(land, s[None], (me, 0, 0))
                                          for land, s in zip(landed, shards[1:])]
        return {"w_uq": arrange_w_uq(cols_whole(a_w_uq)), "w_ukv": cols_whole(a_w_ukv),
                "gdn_conv_w": cols_whole(a_cw), "w_out": a_w_out.reshape(N_DEV * a_w_out.shape[1], a_w_out.shape[2])}

    early, parts = SHARDED[1:], {}

    def core_stage(tag, names, blocks):
        flight["cores" + tag], token = exchange_start("cores_start" + tag, core_routes, 4, blocks)
        flight["names" + tag] = names
        return token

    def chip_stage(tag, after):
        blocks, landed = exchange_wait("cores_wait" + tag, core_routes, flight["cores" + tag], after)
        for n, g, r in zip(flight["names" + tag], blocks, landed):
            parts[n] = add_core_parts("add_" + n, g, r, c_idx, F32 if n == "gdn_conv_w" else _BF)
        wires = [parts[n][1] for n in flight["names" + tag]]
        flight["chips" + tag], token = exchange_start("chips_start" + tag, chip_routes, 3, wires)
        return token

    def on_early_grads(grads):
        return core_stage("_early", early, [
            col_blocks(unarrange_w_uq(grads["w_uq"])), col_blocks(grads["w_ukv"]), col_blocks(grads["gdn_conv_w"]),
            grads["w_out"].reshape(N_DEV, D_MODEL // N_DEV, D_MODEL)])

    def on_d_proj(d_proj):
        return chip_stage("_early", d_proj)

    def on_weight_grads(grads):
        return core_stage("", SHARDED[:1], [col_blocks(unarrange_w_in(grads["w_in"]))])

    def on_d_xn(d_xn):
        return chip_stage("", d_xn)

    pos = positions.reshape(t_len, 1).astype(F32)
    loss, grad_x, grads = local_step(x.reshape(t_len, D_MODEL), pos, loss_target.reshape(t_len, D_MODEL), p,
                                     on_early_grads=on_early_grads, on_d_proj=on_d_proj,
                                     on_weight_grads=on_weight_grads, on_d_xn=on_d_xn,
                                     first_after=weights_token, late_weights=late_weights)
    loss = lax.psum(loss, ("x", "y", "c"))
    out = {}
    small_all = gather_small(_pack_small(grads))
    res = adamw_small(small_all, _pack_small(w), _pack_small(m), _pack_small(v))
    unpacked = [_unpack_small(a) for a in res]
    for n, _ in SMALL:
        out[n] = [u[n] for u in unpacked]

    _, from_chips_early = exchange_wait("chips_wait_early", chip_routes, flight["chips_early"], res[0])
    _, from_chips = exchange_wait("chips_wait", chip_routes, flight["chips"], res[0])
    for n, rcv in zip(SHARDED, from_chips + from_chips_early):
        prt = parts[n][0]
        shape = w[n].shape
        res = adamw_sharded("adamw_" + n, prt, rcv, q_idx, w[n].reshape(shape[-2:]), m[n].reshape(shape[-2:]),
                            v[n].reshape(shape[-2:]))
        out[n] = [a.reshape(shape) for a in res]

    return (loss, grad_x.reshape(x.shape), *[out[n][0] for n in WEIGHT_ORDER], *[out[n][1] for n in WEIGHT_ORDER],
            *[out[n][2] for n in WEIGHT_ORDER], *[out[n][3] for n in WEIGHT_ORDER])
```
